```python
import jax, jax.numpy as jnp
from jax import lax
import numpy as np

D_MODEL = 1024
BATCH = 16
SEQ = 2048
DEPTH = 2

N_A = DEPTH // 2
N_B = DEPTH - N_A
N_META = 16
A_EXPAND = 128
A_HEADS = D_MODEL // A_EXPAND
A_DV = D_MODEL // A_HEADS
A_CHUNK = 64
B_HEADS = 16
B_HDIM = D_MODEL // B_HEADS
Q_BLOCK = 128
FG_BIAS_INIT = 2.0
D_FF = ((8 * D_MODEL // 3 + 63) // 64) * 64
CONV_W = 3
EPS = 1e-6

kernel_name = "yoco_hgrn2_fox_hybrid"


def rmsnorm(x, g):
    xf = x.astype(jnp.float32)
    y = xf * lax.rsqrt(jnp.mean(xf * xf, axis=-1, keepdims=True) + EPS)
    return (y * g.astype(jnp.float32)).astype(x.dtype)


def causal_dwconv(u, w):
    C = u.shape[-1]
    return lax.conv_general_dilated(
        u, w[:, None, :].astype(u.dtype), window_strides=(1,),
        padding=[(CONV_W - 1, 0)], dimension_numbers=("NWC", "WIO", "NWC"),
        feature_group_count=C)


def conv_ffn(x, w_up, conv_w, w_down):
    u = causal_dwconv(x @ w_up, conv_w)
    gate, val = jnp.split(u, 2, axis=-1)
    return (jax.nn.silu(gate) * val) @ w_down


def gla_chunks(q, k, v, logf, s0):
    b = jnp.cumsum(logf, axis=3)
    b_last = b[..., -1:, :]
    q_in = q * jnp.exp(b)
    k_in = k * jnp.exp(-b)
    k_out = k * jnp.exp(b_last - b)
    C = q.shape[3]
    causal = jnp.tril(jnp.ones((C, C), dtype=bool))
    attn = jnp.where(causal, jnp.einsum("bhnck,bhnsk->bhncs", q_in, k_in), 0.0)
    o_intra = jnp.einsum("bhncs,bhnsv->bhncv", attn, v)
    dS = jnp.einsum("bhnck,bhncv->bhnkv", k_out, v)
    decay = jnp.exp(b_last[..., 0, :])

    def step(S, inp):
        d, ds = inp
        return d[..., :, None] * S + ds, S

    S_fin, S_in = lax.scan(step, s0, (jnp.moveaxis(decay, 2, 0), jnp.moveaxis(dS, 2, 0)))
    S_in = jnp.moveaxis(S_in, 0, 2)
    o_inter = jnp.einsum("bhnck,bhnkv->bhncv", q_in, S_in)
    return o_intra + o_inter, S_fin


def hgrn2_mixer(x, w_in, lb, head_gain, w_out):
    Bsz, T, _ = x.shape
    q, f_pre, i, g = jnp.split(x @ w_in, 4, axis=-1)
    f = lb + (1.0 - lb) * jax.nn.sigmoid(f_pre.astype(jnp.float32))
    logf = jnp.log(f)
    k = 1.0 - f

    def heads(t):
        return t.astype(jnp.float32).reshape(Bsz, T, A_HEADS, -1).transpose(0, 2, 1, 3)

    qh, kh, vh, lh = heads(q), heads(k), heads(i), heads(logf)

    def meta_part(t):
        return t[:, :, :N_META][:, :, None]

    def real_part(t):
        return t[:, :, N_META:].reshape(Bsz, A_HEADS, -1, A_CHUNK, t.shape[-1])

    s0 = jnp.zeros((Bsz, A_HEADS, A_EXPAND, A_DV), jnp.float32)
    o_meta, s_meta = gla_chunks(meta_part(qh), meta_part(kh), meta_part(vh), meta_part(lh), s0)
    o_real, _ = gla_chunks(real_part(qh), real_part(kh), real_part(vh), real_part(lh), s_meta)
    o = jnp.concatenate([o_meta.reshape(Bsz, A_HEADS, N_META, A_DV),
                         o_real.reshape(Bsz, A_HEADS, T - N_META, A_DV)], axis=2)
    o = o.transpose(0, 2, 1, 3)
    o = o * lax.rsqrt(jnp.mean(o * o, axis=-1, keepdims=True) + EPS)
    o = o * head_gain.astype(jnp.float32).reshape(A_HEADS, A_DV)
    o = o.reshape(Bsz, T, D_MODEL).astype(x.dtype) * jax.nn.silu(g)
    return o @ w_out


def shared_kv(h, kv_norm, kv_w, fg_b):
    Bsz, T, _ = h.shape
    proj = rmsnorm(h, kv_norm) @ kv_w
    k = proj[..., :D_MODEL].reshape(Bsz, T, B_HEADS, B_HDIM).transpose(0, 2, 1, 3)
    v = proj[..., D_MODEL:2 * D_MODEL].reshape(Bsz, T, B_HEADS, B_HDIM).transpose(0, 2, 1, 3)
    zf = proj[..., 2 * D_MODEL:].astype(jnp.float32) + fg_b.astype(jnp.float32)
    c = jnp.cumsum(jax.nn.log_sigmoid(zf), axis=1).transpose(0, 2, 1)
    return k, v, c


def fox_mixer(x, w_q, w_out, k, v, c):
    Bsz, T, _ = x.shape
    q = (x @ w_q).reshape(Bsz, T, B_HEADS, B_HDIM).transpose(0, 2, 1, 3)
    scale = 1.0 / np.sqrt(B_HDIM).astype(np.float32)
    bounds = [(0, N_META)] + [(N_META + n * Q_BLOCK, N_META + (n + 1) * Q_BLOCK)
                              for n in range((T - N_META) // Q_BLOCK)]
    outs = []
    for s, e in bounds:
        logits = jnp.einsum("bhqd,bhkd->bhqk", q[:, :, s:e], k[:, :, :e]).astype(jnp.float32) * scale
        logits = logits + (c[:, :, s:e, None] - c[:, :, None, :e])
        mask = jnp.arange(s, e)[:, None] >= jnp.arange(e)[None, :]
        p = jax.nn.softmax(jnp.where(mask, logits, -1e30), axis=-1)
        outs.append(jnp.einsum("bhqk,bhkd->bhqd", p.astype(v.dtype), v[:, :, :e]))
    o = jnp.concatenate(outs, axis=2).transpose(0, 2, 1, 3).reshape(Bsz, T, D_MODEL)
    return o @ w_out


def _fwd_setup_inputs(seed: int = 0) -> dict:
    key = jax.random.key(seed)
    ks = jax.random.split(key, 16)
    D = D_MODEL

    def nrm(k, shape, scale):
        return jax.random.normal(k, shape, jnp.float32) * scale

    return {
        "x": nrm(ks[0], (BATCH, SEQ, D), 1.0),
        "meta_tokens": nrm(ks[1], (N_META, D), 1.0),
        "norm_gains": 1.0 + nrm(ks[2], (DEPTH, 4, D), 0.05),
        "a_w_in": nrm(ks[3], (N_A, D, 4 * D), D ** -0.5),
        "a_lb_logits": nrm(ks[4], (N_A + 1, D), 0.1),
        "a_head_norm": 1.0 + nrm(ks[5], (N_A, D), 0.05),
        "a_w_out": nrm(ks[6], (N_A, D, D), D ** -0.5),
        "kv_norm": 1.0 + nrm(ks[7], (D,), 0.05),
        "kv_w": nrm(ks[8], (D, 2 * D + B_HEADS), D ** -0.5),
        "fg_b": FG_BIAS_INIT + nrm(ks[9], (B_HEADS,), 0.1),
        "b_w_q": nrm(ks[10], (N_B, D, D), D ** -0.5),
        "b_w_out": nrm(ks[11], (N_B, D, D), D ** -0.5),
        "ffn_w_up": nrm(ks[12], (DEPTH, D, 2 * D_FF), D ** -0.5),
        "ffn_conv": nrm(ks[13], (DEPTH, CONV_W, 2 * D_FF), CONV_W ** -0.5),
        "ffn_w_down": nrm(ks[14], (DEPTH, D_FF, D), D_FF ** -0.5),
    }


def _fwd_reference(x, meta_tokens, norm_gains, a_w_in, a_lb_logits, a_head_norm, a_w_out,
              kv_norm, kv_w, fg_b, b_w_q, b_w_out, ffn_w_up, ffn_conv, ffn_w_down):
    Bsz = x.shape[0]
    meta = jnp.broadcast_to(meta_tokens[None].astype(x.dtype), (Bsz, N_META, D_MODEL))
    h = jnp.concatenate([meta, x], axis=1)
    lb_all = jnp.cumsum(jax.nn.softmax(a_lb_logits.astype(jnp.float32), axis=0), axis=0)
    k_sh = v_sh = c_sh = None
    for l in range(DEPTH):
        g = norm_gains[l]
        hn = rmsnorm(h, g[0])
        if l < N_A:
            mix = hgrn2_mixer(hn, a_w_in[l], lb_all[l], a_head_norm[l], a_w_out[l])
        else:
            if l == N_A:
                k_sh, v_sh, c_sh = shared_kv(h, kv_norm, kv_w, fg_b)
            j = l - N_A
            mix = fox_mixer(hn, b_w_q[j], b_w_out[j], k_sh, v_sh, c_sh)
        h = h + rmsnorm(mix, g[1])
        ff = conv_ffn(rmsnorm(h, g[2]), ffn_w_up[l], ffn_conv[l], ffn_w_down[l])
        h = h + rmsnorm(ff, g[3])
    return h[:, N_META:]


import jax as _jax
import jax.numpy as _jnp

TWIN_FORMAT = 'train_step'
FWD_PARAMS = ['x', 'meta_tokens', 'norm_gains', 'a_w_in', 'a_lb_logits', 'a_head_norm', 'a_w_out', 'kv_norm', 'kv_w', 'fg_b', 'b_w_q', 'b_w_out', 'ffn_w_up', 'ffn_conv', 'ffn_w_down']
TWIN_WEIGHTS = ['meta_tokens', 'norm_gains', 'a_w_in', 'a_lb_logits', 'a_head_norm', 'a_w_out', 'kv_norm', 'kv_w', 'fg_b', 'b_w_q', 'b_w_out', 'ffn_w_up', 'ffn_conv', 'ffn_w_down']
TWIN_DIFF_INPUT = 'x'
TWIN_INPUTS = ['x', 'meta_tokens', 'norm_gains', 'a_w_in', 'a_lb_logits', 'a_head_norm', 'a_w_out', 'kv_norm', 'kv_w', 'fg_b', 'b_w_q', 'b_w_out', 'ffn_w_up', 'ffn_conv', 'ffn_w_down', 'loss_target', 'm_meta_tokens', 'm_norm_gains', 'm_a_w_in', 'm_a_lb_logits', 'm_a_head_norm', 'm_a_w_out', 'm_kv_norm', 'm_kv_w', 'm_fg_b', 'm_b_w_q', 'm_b_w_out', 'm_ffn_w_up', 'm_ffn_conv', 'm_ffn_w_down', 'v_meta_tokens', 'v_norm_gains', 'v_a_w_in', 'v_a_lb_logits', 'v_a_head_norm', 'v_a_w_out', 'v_kv_norm', 'v_kv_w', 'v_fg_b', 'v_b_w_q', 'v_b_w_out', 'v_ffn_w_up', 'v_ffn_conv', 'v_ffn_w_down']
TWIN_OUTPUTS = ['loss', 'grad_x', 'grad_meta_tokens', 'grad_norm_gains', 'grad_a_w_in', 'grad_a_lb_logits', 'grad_a_head_norm', 'grad_a_w_out', 'grad_kv_norm', 'grad_kv_w', 'grad_fg_b', 'grad_b_w_q', 'grad_b_w_out', 'grad_ffn_w_up', 'grad_ffn_conv', 'grad_ffn_w_down', 'delta_meta_tokens', 'delta_norm_gains', 'delta_a_w_in', 'delta_a_lb_logits', 'delta_a_head_norm', 'delta_a_w_out', 'delta_kv_norm', 'delta_kv_w', 'delta_fg_b', 'delta_b_w_q', 'delta_b_w_out', 'delta_ffn_w_up', 'delta_ffn_conv', 'delta_ffn_w_down', 'new_m_meta_tokens', 'new_m_norm_gains', 'new_m_a_w_in', 'new_m_a_lb_logits', 'new_m_a_head_norm', 'new_m_a_w_out', 'new_m_kv_norm', 'new_m_kv_w', 'new_m_fg_b', 'new_m_b_w_q', 'new_m_b_w_out', 'new_m_ffn_w_up', 'new_m_ffn_conv', 'new_m_ffn_w_down', 'new_v_meta_tokens', 'new_v_norm_gains', 'new_v_a_w_in', 'new_v_a_lb_logits', 'new_v_a_head_norm', 'new_v_a_w_out', 'new_v_kv_norm', 'new_v_kv_w', 'new_v_fg_b', 'new_v_b_w_q', 'new_v_b_w_out', 'new_v_ffn_w_up', 'new_v_ffn_conv', 'new_v_ffn_w_down']
TWIN_LEAF_KINDS = {'loss': 'loss', 'grad_x': 'grad_x', 'grad_meta_tokens': 'grad_w', 'grad_norm_gains': 'grad_w', 'grad_a_w_in': 'grad_w', 'grad_a_lb_logits': 'grad_w', 'grad_a_head_norm': 'grad_w', 'grad_a_w_out': 'grad_w', 'grad_kv_norm': 'grad_w', 'grad_kv_w': 'grad_w', 'grad_fg_b': 'grad_w', 'grad_b_w_q': 'grad_w', 'grad_b_w_out': 'grad_w', 'grad_ffn_w_up': 'grad_w', 'grad_ffn_conv': 'grad_w', 'grad_ffn_w_down': 'grad_w', 'delta_meta_tokens': 'delta_w', 'delta_norm_gains': 'delta_w', 'delta_a_w_in': 'delta_w', 'delta_a_lb_logits': 'delta_w', 'delta_a_head_norm': 'delta_w', 'delta_a_w_out': 'delta_w', 'delta_kv_norm': 'delta_w', 'delta_kv_w': 'delta_w', 'delta_fg_b': 'delta_w', 'delta_b_w_q': 'delta_w', 'delta_b_w_out': 'delta_w', 'delta_ffn_w_up': 'delta_w', 'delta_ffn_conv': 'delta_w', 'delta_ffn_w_down': 'delta_w', 'new_m_meta_tokens': 'new_m', 'new_m_norm_gains': 'new_m', 'new_m_a_w_in': 'new_m', 'new_m_a_lb_logits': 'new_m', 'new_m_a_head_norm': 'new_m', 'new_m_a_w_out': 'new_m', 'new_m_kv_norm': 'new_m', 'new_m_kv_w': 'new_m', 'new_m_fg_b': 'new_m', 'new_m_b_w_q': 'new_m', 'new_m_b_w_out': 'new_m', 'new_m_ffn_w_up': 'new_m', 'new_m_ffn_conv': 'new_m', 'new_m_ffn_w_down': 'new_m', 'new_v_meta_tokens': 'new_v', 'new_v_norm_gains': 'new_v', 'new_v_a_w_in': 'new_v', 'new_v_a_lb_logits': 'new_v', 'new_v_a_head_norm': 'new_v', 'new_v_a_w_out': 'new_v', 'new_v_kv_norm': 'new_v', 'new_v_kv_w': 'new_v', 'new_v_fg_b': 'new_v', 'new_v_b_w_q': 'new_v', 'new_v_b_w_out': 'new_v', 'new_v_ffn_w_up': 'new_v', 'new_v_ffn_conv': 'new_v', 'new_v_ffn_w_down': 'new_v'}


def _forward(args):
    return _fwd_reference(*[args[k] for k in FWD_PARAMS])


def _output_shape():
    out = _jax.eval_shape(lambda: _forward(_fwd_setup_inputs(0)))
    return out.shape, out.dtype

N_MICROBATCH = 1
ADAM_LR = 0.001
ADAM_B1 = 0.9
ADAM_B2 = 0.999
ADAM_EPS = 1e-08
ADAM_WD = 0.01
ADAM_STEP = 10
PER_EXAMPLE_BATCH_AXIS = {'x': 0, 'loss_target': 0}
SHARED_INPUTS = []
_WEIGHT_DTYPES = {'meta_tokens': _jnp.float32, 'norm_gains': _jnp.float32, 'a_w_in': _jnp.float32, 'a_lb_logits': _jnp.float32, 'a_head_norm': _jnp.float32, 'a_w_out': _jnp.float32, 'kv_norm': _jnp.float32, 'kv_w': _jnp.float32, 'fg_b': _jnp.float32, 'b_w_q': _jnp.float32, 'b_w_out': _jnp.float32, 'ffn_w_up': _jnp.float32, 'ffn_conv': _jnp.float32, 'ffn_w_down': _jnp.float32}
MOMENT_SCALE = {'meta_tokens': 5.112801e-02, 'norm_gains': 2.209224e+01, 'a_w_in': 1.008142e+00, 'a_lb_logits': 6.098343e-01, 'a_head_norm': 9.621058e-01, 'a_w_out': 9.565537e-01, 'kv_norm': 7.418485e-01, 'kv_w': 5.399981e-01, 'fg_b': 2.362998e+00, 'b_w_q': 3.855996e-01, 'b_w_out': 6.678747e-01, 'ffn_w_up': 3.523282e-01, 'ffn_conv': 3.636763e-01, 'ffn_w_down': 6.153725e-01}


def _to_microbatches(a, axis):
    t = _jnp.moveaxis(a, axis, 0)
    t = t.reshape((N_MICROBATCH, t.shape[0] // N_MICROBATCH) + t.shape[1:])
    return _jnp.moveaxis(t, 1, axis + 1)


def setup_inputs(seed: int = 0) -> dict:
    inp = _fwd_setup_inputs(seed)
    key = _jax.random.fold_in(_jax.random.key(seed), 7919)
    shape, _ = _output_shape()
    out = dict(inp)
    out["loss_target"] = _jax.random.normal(_jax.random.fold_in(key, 0), shape, _jnp.float32)
    for i, name in enumerate(TWIN_WEIGHTS):
        w = inp[name].astype(_jnp.float32)
        if MOMENT_SCALE is None:
            s = _jnp.sqrt(_jnp.mean(_jnp.square(w)) + 1e-30)
        else:
            s = MOMENT_SCALE[name]
        km, kv = _jax.random.split(_jax.random.fold_in(key, i + 1))
        out[name] = w
        out["m_" + name] = s * _jax.random.normal(km, w.shape, _jnp.float32)
        out["v_" + name] = (s * s) * _jax.random.uniform(kv, w.shape, _jnp.float32, 0.5, 1.5)
    if N_MICROBATCH > 1:
        for name, axis in PER_EXAMPLE_BATCH_AXIS.items():
            out[name] = _to_microbatches(out[name], axis)
    return {'x': out['x'], 'meta_tokens': out['meta_tokens'], 'norm_gains': out['norm_gains'], 'a_w_in': out['a_w_in'], 'a_lb_logits': out['a_lb_logits'], 'a_head_norm': out['a_head_norm'], 'a_w_out': out['a_w_out'], 'kv_norm': out['kv_norm'], 'kv_w': out['kv_w'], 'fg_b': out['fg_b'], 'b_w_q': out['b_w_q'], 'b_w_out': out['b_w_out'], 'ffn_w_up': out['ffn_w_up'], 'ffn_conv': out['ffn_conv'], 'ffn_w_down': out['ffn_w_down'], 'loss_target': out['loss_target'], 'm_meta_tokens': out['m_meta_tokens'], 'm_norm_gains': out['m_norm_gains'], 'm_a_w_in': out['m_a_w_in'], 'm_a_lb_logits': out['m_a_lb_logits'], 'm_a_head_norm': out['m_a_head_norm'], 'm_a_w_out': out['m_a_w_out'], 'm_kv_norm': out['m_kv_norm'], 'm_kv_w': out['m_kv_w'], 'm_fg_b': out['m_fg_b'], 'm_b_w_q': out['m_b_w_q'], 'm_b_w_out': out['m_b_w_out'], 'm_ffn_w_up': out['m_ffn_w_up'], 'm_ffn_conv': out['m_ffn_conv'], 'm_ffn_w_down': out['m_ffn_w_down'], 'v_meta_tokens': out['v_meta_tokens'], 'v_norm_gains': out['v_norm_gains'], 'v_a_w_in': out['v_a_w_in'], 'v_a_lb_logits': out['v_a_lb_logits'], 'v_a_head_norm': out['v_a_head_norm'], 'v_a_w_out': out['v_a_w_out'], 'v_kv_norm': out['v_kv_norm'], 'v_kv_w': out['v_kv_w'], 'v_fg_b': out['v_fg_b'], 'v_b_w_q': out['v_b_w_q'], 'v_b_w_out': out['v_b_w_out'], 'v_ffn_w_up': out['v_ffn_w_up'], 'v_ffn_conv': out['v_ffn_conv'], 'v_ffn_w_down': out['v_ffn_w_down']}


def _loss(weights, diff, rest, loss_target):
    with _jax.named_scope("forward"):
        args = {**rest, TWIN_DIFF_INPUT: diff, **{k: w.astype(_WEIGHT_DTYPES[k]) for k, w in weights.items()}}
        y = _forward(args)
    with _jax.named_scope("loss_head"):
        err = _jnp.square(y.astype(_jnp.float32) - loss_target)
        return 0.5 * _jnp.sum(_jnp.mean(err, axis=-1)) if err.ndim else 0.5 * err


def _adamw(w, g, m, v):
    m = ADAM_B1 * m + (1.0 - ADAM_B1) * g
    v = ADAM_B2 * v + (1.0 - ADAM_B2) * _jnp.square(g)
    m_hat = m / (1.0 - ADAM_B1 ** ADAM_STEP)
    v_hat = v / (1.0 - ADAM_B2 ** ADAM_STEP)
    delta = -ADAM_LR * (m_hat / (_jnp.sqrt(v_hat) + ADAM_EPS) + ADAM_WD * w)
    return delta, m, v


def reference(x, meta_tokens, norm_gains, a_w_in, a_lb_logits, a_head_norm, a_w_out, kv_norm, kv_w, fg_b, b_w_q, b_w_out, ffn_w_up, ffn_conv, ffn_w_down, loss_target, m_meta_tokens, m_norm_gains, m_a_w_in, m_a_lb_logits, m_a_head_norm, m_a_w_out, m_kv_norm, m_kv_w, m_fg_b, m_b_w_q, m_b_w_out, m_ffn_w_up, m_ffn_conv, m_ffn_w_down, v_meta_tokens, v_norm_gains, v_a_w_in, v_a_lb_logits, v_a_head_norm, v_a_w_out, v_kv_norm, v_kv_w, v_fg_b, v_b_w_q, v_b_w_out, v_ffn_w_up, v_ffn_conv, v_ffn_w_down):
    given = dict(x=x, meta_tokens=meta_tokens, norm_gains=norm_gains, a_w_in=a_w_in, a_lb_logits=a_lb_logits, a_head_norm=a_head_norm, a_w_out=a_w_out, kv_norm=kv_norm, kv_w=kv_w, fg_b=fg_b, b_w_q=b_w_q, b_w_out=b_w_out, ffn_w_up=ffn_w_up, ffn_conv=ffn_conv, ffn_w_down=ffn_w_down, loss_target=loss_target, m_meta_tokens=m_meta_tokens, m_norm_gains=m_norm_gains, m_a_w_in=m_a_w_in, m_a_lb_logits=m_a_lb_logits, m_a_head_norm=m_a_head_norm, m_a_w_out=m_a_w_out, m_kv_norm=m_kv_norm, m_kv_w=m_kv_w, m_fg_b=m_fg_b, m_b_w_q=m_b_w_q, m_b_w_out=m_b_w_out, m_ffn_w_up=m_ffn_w_up, m_ffn_conv=m_ffn_conv, m_ffn_w_down=m_ffn_w_down, v_meta_tokens=v_meta_tokens, v_norm_gains=v_norm_gains, v_a_w_in=v_a_w_in, v_a_lb_logits=v_a_lb_logits, v_a_head_norm=v_a_head_norm, v_a_w_out=v_a_w_out, v_kv_norm=v_kv_norm, v_kv_w=v_kv_w, v_fg_b=v_fg_b, v_b_w_q=v_b_w_q, v_b_w_out=v_b_w_out, v_ffn_w_up=v_ffn_w_up, v_ffn_conv=v_ffn_conv, v_ffn_w_down=v_ffn_w_down)
    weights = {n: given[n] for n in TWIN_WEIGHTS}
    shared = {n: given[n] for n in SHARED_INPUTS}
    per_example = {n: given[n] for n in ['x']}
    grad_fn = _jax.value_and_grad(_loss, argnums=(0, 1))

    def one_microbatch(ex, loss_target):
        ex = dict(ex)
        diff = ex.pop(TWIN_DIFF_INPUT)
        return grad_fn(weights, diff, {**shared, **ex}, loss_target)

    if N_MICROBATCH == 1:
        loss, (grad_w, grad_x) = one_microbatch(per_example, given["loss_target"])
    else:
        def body(carry, xs):
            loss_sum, grad_sum = carry
            l_k, (gw_k, gx_k) = one_microbatch(xs[0], xs[1])
            with _jax.named_scope("update"):
                return (loss_sum + l_k, _jax.tree.map(_jnp.add, grad_sum, gw_k)), gx_k

        init = (_jnp.zeros((), _jnp.float32), _jax.tree.map(_jnp.zeros_like, weights))
        (loss, grad_w), grad_x = _jax.lax.scan(body, init, (per_example, given["loss_target"]))
    with _jax.named_scope("update"):
        delta_w, new_m, new_v = {}, {}, {}
        for n in TWIN_WEIGHTS:
            delta_w[n], new_m[n], new_v[n] = _adamw(weights[n], grad_w[n], given["m_" + n], given["v_" + n])
    return (loss, grad_x, *[grad_w[n] for n in TWIN_WEIGHTS], *[delta_w[n] for n in TWIN_WEIGHTS],
            *[new_m[n] for n in TWIN_WEIGHTS], *[new_v[n] for n in TWIN_WEIGHTS])
```

```python
import functools

import jax
import jax.numpy as jnp
from jax import lax
from jax.experimental import pallas as pl
from jax.experimental.pallas import tpu as pltpu

f32 = jnp.float32
bf16 = jnp.bfloat16
SDS = jax.ShapeDtypeStruct

EPS = 1e-6
A_DK = 128
A_CHUNK = 64
LANES = 128
BF16_ROWS = 16
VMEM_LIMIT = 56 * 1024 * 1024
ADAM_LR, ADAM_B1, ADAM_B2, ADAM_EPS, ADAM_WD, ADAM_STEP = 0.001, 0.9, 0.999, 1e-08, 0.01, 10
N_DEV = 8
MESH = pl.DeviceIdType.MESH

_NT = (((1,), (1,)), ((), ()))
_TN = (((0,), (0,)), ((), ()))
_HI = lax.Precision.HIGHEST


def _params(**kw):
    return pltpu.CompilerParams(vmem_limit_bytes=VMEM_LIMIT, **kw)


def _div_tile(n, cap, mult=BF16_ROWS):
    best = None
    for t in range(mult, min(n, cap) + 1, mult):
        if n % t == 0:
            best = t
    assert best is not None, (n, cap, mult)
    return best


def _bdot(a, b):
    return jnp.dot(a.astype(bf16), b.astype(bf16), preferred_element_type=f32)


def _bdot_nt(a, b):
    return lax.dot_general(a.astype(bf16), b.astype(bf16), _NT, preferred_element_type=f32)


def _bdot_tn(a, b):
    return lax.dot_general(a.astype(bf16), b.astype(bf16), _TN, preferred_element_type=f32)


def _iota2(shape, axis):
    return lax.broadcasted_iota(jnp.int32, shape, axis)


def _cumsum_rows(x):
    n = x.shape[0]
    tri = (_iota2((n, n), 0) >= _iota2((n, n), 1)).astype(f32)
    return jnp.dot(tri, x, precision=_HI, preferred_element_type=f32)


def _revcumsum_rows(x):
    n = x.shape[0]
    tri = (_iota2((n, n), 1) >= _iota2((n, n), 0)).astype(f32)
    return jnp.dot(tri, x, precision=_HI, preferred_element_type=f32)


def _sigmoid(x):
    return 1.0 / (1.0 + jnp.exp(-x))


def _rms_fwd(x, g, tm, name):
    n, d = x.shape

    def body(x_ref, g_ref, o_ref):
        xv = x_ref[...]
        r = lax.rsqrt(jnp.mean(xv * xv, axis=-1, keepdims=True) + EPS)
        o_ref[...] = (xv * r * g_ref[...]).astype(o_ref.dtype)

    return pl.pallas_call(
        body, grid=(n // tm,), name=name,
        in_specs=[pl.BlockSpec((tm, d), lambda i: (i, 0)), pl.BlockSpec((1, d), lambda i: (0, 0))],
        out_specs=pl.BlockSpec((tm, d), lambda i: (i, 0)),
        out_shape=SDS((n, d), bf16), compiler_params=_params(),
    )(x, g)


def _mm(a, w, out_dtype, tm, tn, name):
    n, k = a.shape
    m = w.shape[1]

    def body(a_ref, w_ref, o_ref):
        o_ref[...] = _bdot(a_ref[...], w_ref[...]).astype(o_ref.dtype)

    return pl.pallas_call(
        body, grid=(m // tn, n // tm), name=name,
        in_specs=[pl.BlockSpec((tm, k), lambda j, i: (i, 0)), pl.BlockSpec((k, tn), lambda j, i: (0, j))],
        out_specs=pl.BlockSpec((tm, tn), lambda j, i: (i, j)),
        out_shape=SDS((n, m), out_dtype), compiler_params=_params(),
    )(a, w)


def _mm_norm_res(a, w, g, h, tm, name):
    n, k = a.shape
    d = w.shape[1]

    def body(a_ref, w_ref, g_ref, h_ref, hn_ref, mix_ref):
        mix = _bdot(a_ref[...], w_ref[...])
        r = lax.rsqrt(jnp.mean(mix * mix, axis=-1, keepdims=True) + EPS)
        mix_ref[...] = mix
        hn_ref[...] = h_ref[...] + mix * r * g_ref[...]

    return pl.pallas_call(
        body, grid=(n // tm,), name=name,
        in_specs=[pl.BlockSpec((tm, k), lambda i: (i, 0)), pl.BlockSpec((k, d), lambda i: (0, 0)),
                  pl.BlockSpec((1, d), lambda i: (0, 0)), pl.BlockSpec((tm, d), lambda i: (i, 0))],
        out_specs=[pl.BlockSpec((tm, d), lambda i: (i, 0)), pl.BlockSpec((tm, d), lambda i: (i, 0))],
        out_shape=[SDS((n, d), f32), SDS((n, d), f32)], compiler_params=_params(),
    )(a, w, g, h)


def _rms_bwd(x, g, dy, dh_in, out_dtype, tm, name):
    n, d = x.shape
    has_add = dh_in is not None

    def body(*refs):
        if has_add:
            x_ref, g_ref, dy_ref, dh_ref, o_ref, dg_ref = refs
        else:
            x_ref, g_ref, dy_ref, o_ref, dg_ref = refs
        xv = x_ref[...]
        dyv = dy_ref[...].astype(f32)
        r = lax.rsqrt(jnp.mean(xv * xv, axis=-1, keepdims=True) + EPS)
        xr = xv * r
        gdy = dyv * g_ref[...]
        dx = r * gdy - xr * (r * r) * jnp.mean(xv * gdy, axis=-1, keepdims=True)
        if has_add:
            dx = dx + dh_ref[...]
        o_ref[...] = dx.astype(o_ref.dtype)

        @pl.when(pl.program_id(0) == 0)
        def _():
            dg_ref[...] = jnp.zeros_like(dg_ref)

        dg_ref[...] += jnp.sum(dyv * xr, axis=0, keepdims=True)

    row = pl.BlockSpec((tm, d), lambda i: (i, 0))
    vec = pl.BlockSpec((1, d), lambda i: (0, 0))
    ins = [x, g, dy] + ([dh_in] if has_add else [])
    return pl.pallas_call(
        body, grid=(n // tm,), name=name,
        in_specs=[row, vec, row] + ([row] if has_add else []),
        out_specs=[row, vec],
        out_shape=[SDS((n, d), out_dtype), SDS((1, d), f32)], compiler_params=_params(),
    )(*ins)


def _mm_nt(pairs, out_dtype, tm, tk, name):
    n = pairs[0][0].shape[0]
    k = pairs[0][1].shape[0]
    np_ = len(pairs)

    def body(*refs):
        o_ref = refs[-1]
        acc = None
        for p in range(np_):
            t = _bdot_nt(refs[2 * p][...], refs[2 * p + 1][...])
            acc = t if acc is None else acc + t
        o_ref[...] = acc.astype(o_ref.dtype)

    in_specs, ins = [], []
    for dy, w in pairs:
        m = dy.shape[1]
        in_specs += [pl.BlockSpec((tm, m), lambda j, i: (i, 0)), pl.BlockSpec((tk, m), lambda j, i: (j, 0))]
        ins += [dy, w]
    return pl.pallas_call(
        body, grid=(k // tk, n // tm), name=name, in_specs=in_specs,
        out_specs=pl.BlockSpec((tm, tk), lambda j, i: (i, j)),
        out_shape=SDS((n, k), out_dtype), compiler_params=_params(),
    )(*ins)


def _mm_tn(x, dy, tm, tk, tn, name):
    n, k = x.shape
    m = dy.shape[1]

    def body(x_ref, dy_ref, o_ref):
        @pl.when(pl.program_id(2) == 0)
        def _():
            o_ref[...] = jnp.zeros_like(o_ref)

        o_ref[...] += _bdot_tn(x_ref[...], dy_ref[...])

    return pl.pallas_call(
        body, grid=(k // tk, m // tn, n // tm), name=name,
        in_specs=[pl.BlockSpec((tm, tk), lambda a, b, i: (i, a)), pl.BlockSpec((tm, tn), lambda a, b, i: (i, b))],
        out_specs=pl.BlockSpec((tk, tn), lambda a, b, i: (a, b)),
        out_shape=SDS((k, m), f32), compiler_params=_params(),
    )(x, dy)


def _gla_chunk_fwd(qc, fc, vc, lb, st):
    c = qc.shape[0]
    sg = _sigmoid(fc)
    f = lb + (1.0 - lb) * sg
    k = 1.0 - f
    b = _cumsum_rows(jnp.log(f))
    bl = b[c - 1:c, :]
    e = jnp.exp(b)
    ebl = jnp.exp(bl)
    qi = qc * e
    ki = k * jnp.exp(-b)
    ko = k * jnp.exp(bl - b)
    causal = _iota2((c, c), 0) >= _iota2((c, c), 1)
    att = jnp.where(causal, _bdot_nt(qi, ki), 0.0)
    o = _bdot(att, vc) + _bdot_nt(qi, st)
    st_new = st * ebl + _bdot_tn(vc, ko)
    return dict(sg=sg, f=f, k=k, b=b, bl=bl, e=e, ebl=ebl, qi=qi, ki=ki, ko=ko, att=att, o=o, st_new=st_new, causal=causal)


def _head_out(o, ggc, hg):
    r = lax.rsqrt(jnp.mean(o * o, axis=-1, keepdims=True) + EPS)
    return o * r * hg * (ggc * _sigmoid(ggc))


def _gla_fwd(pmat, lb, hg, bl_, t, nm, name):
    n, d4 = pmat.shape
    d = d4 // 4
    nh = d // A_DK
    nreal = (t - nm) // A_CHUNK
    nch = nreal + 1

    def body(q_ref, f_ref, i_ref, gg_ref, lb_ref, hg_ref, og_ref, ss_ref):
        lbv, hgv = lb_ref[...], hg_ref[...]

        def run(rows, st, idx):
            ss_ref[0, idx] = st
            w = _gla_chunk_fwd(q_ref[rows, :], f_ref[rows, :], i_ref[rows, :], lbv, st)
            og_ref[rows, :] = _head_out(w["o"], gg_ref[rows, :], hgv).astype(og_ref.dtype)
            return w["st_new"]

        st = run(pl.ds(0, nm), jnp.zeros((A_DK, A_DK), f32), 0)

        def step(c, st):
            return run(pl.ds(pl.multiple_of(nm + c * A_CHUNK, BF16_ROWS), A_CHUNK), st, c + 1)

        lax.fori_loop(0, nreal, step, st)

    col = lambda o: pl.BlockSpec((t, A_DK), lambda b, h: (b, o * nh + h))
    vec = pl.BlockSpec((1, A_DK), lambda b, h: (0, h))
    return pl.pallas_call(
        body, grid=(bl_, nh), name=name,
        in_specs=[col(0), col(1), col(2), col(3), vec, vec],
        out_specs=[pl.BlockSpec((t, A_DK), lambda b, h: (b, h)),
                   pl.BlockSpec((1, nch, A_DK, A_DK), lambda b, h: (b * nh + h, 0, 0, 0))],
        out_shape=[SDS((n, d), bf16), SDS((bl_ * nh, nch, A_DK, A_DK), f32)], compiler_params=_params(),
    )(pmat, pmat, pmat, pmat, lb, hg)


def _gla_bwd(pmat, ss, dog, lb, hg, bl_, t, nm, name):
    n, d4 = pmat.shape
    d = d4 // 4
    nh = d // A_DK
    nreal = (t - nm) // A_CHUNK
    nch = nreal + 1

    def body(q_ref, f_ref, i_ref, gg_ref, ss_ref, dog_ref, lb_ref, hg_ref,
             dq_ref, df_ref, di_ref, dgg_ref, dlb_ref, dhg_ref):
        lbv, hgv = lb_ref[...], hg_ref[...]

        def run(rows, idx, carry):
            dst, dlb, dhg = carry
            qc, fc, vc, ggc = q_ref[rows, :], f_ref[rows, :], i_ref[rows, :], gg_ref[rows, :]
            dogc = dog_ref[rows, :].astype(f32)
            st = ss_ref[0, idx]
            w = _gla_chunk_fwd(qc, fc, vc, lbv, st)
            c = qc.shape[0]
            o, qi, ki, ko = w["o"], w["qi"], w["ki"], w["ko"]
            r = lax.rsqrt(jnp.mean(o * o, axis=-1, keepdims=True) + EPS)
            sgg = _sigmoid(ggc)
            sil = ggc * sgg
            on = o * r
            dhg = dhg + jnp.sum(dogc * sil * on, axis=0, keepdims=True)
            dgg_ref[rows, :] = (dogc * on * hgv * (sgg * (1.0 + ggc * (1.0 - sgg)))).astype(dgg_ref.dtype)
            tt = dogc * sil * hgv
            do = r * tt - on * (r * r) * jnp.mean(o * tt, axis=-1, keepdims=True)
            dst_in = dst * w["ebl"] + _bdot_tn(do, qi)
            d_ebl = jnp.sum(dst * st, axis=0, keepdims=True)
            datt = jnp.where(w["causal"], _bdot_nt(do, vc), 0.0)
            dv = _bdot_tn(w["att"], do) + _bdot_nt(ko, dst)
            dko = _bdot(vc, dst)
            dqi = _bdot(datt, ki) + _bdot(do, st)
            dki = _bdot_tn(datt, qi)
            dk = dki * jnp.exp(-w["b"]) + dko * jnp.exp(w["bl"] - w["b"])
            db = dqi * qi - dki * ki - dko * ko
            dbl = jnp.sum(dko * ko, axis=0, keepdims=True) + d_ebl * w["ebl"]
            db = db + jnp.where(_iota2(db.shape, 0) == c - 1, dbl, 0.0)
            dlogf = _revcumsum_rows(db)
            df = dlogf / w["f"] - dk
            sg = w["sg"]
            dq_ref[rows, :] = (dqi * w["e"]).astype(dq_ref.dtype)
            df_ref[rows, :] = (df * (1.0 - lbv) * sg * (1.0 - sg)).astype(df_ref.dtype)
            di_ref[rows, :] = dv.astype(di_ref.dtype)
            dlb = dlb + jnp.sum(df * (1.0 - sg), axis=0, keepdims=True)
            return dst_in, dlb, dhg

        zero = jnp.zeros((1, A_DK), f32)

        def step(j, carry):
            c = nreal - 1 - j
            return run(pl.ds(pl.multiple_of(nm + c * A_CHUNK, BF16_ROWS), A_CHUNK), c + 1, carry)

        carry = lax.fori_loop(0, nreal, step, (jnp.zeros((A_DK, A_DK), f32), zero, zero))
        _, dlb, dhg = run(pl.ds(0, nm), 0, carry)

        @pl.when(pl.program_id(1) == 0)
        def _():
            dlb_ref[...] = jnp.zeros_like(dlb_ref)
            dhg_ref[...] = jnp.zeros_like(dhg_ref)

        dlb_ref[...] += dlb
        dhg_ref[...] += dhg

    col = lambda o: pl.BlockSpec((t, A_DK), lambda h, b: (b, o * nh + h))
    blk = pl.BlockSpec((t, A_DK), lambda h, b: (b, h))
    vec = pl.BlockSpec((1, A_DK), lambda h, b: (0, h))
    return pl.pallas_call(
        body, grid=(nh, bl_), name=name,
        in_specs=[col(0), col(1), col(2), col(3),
                  pl.BlockSpec((1, nch, A_DK, A_DK), lambda h, b: (b * nh + h, 0, 0, 0)), blk, vec, vec],
        out_specs=[blk, blk, blk, blk, vec, vec],
        out_shape=[SDS((n, d), bf16)] * 4 + [SDS((1, d), f32)] * 2, compiler_params=_params(),
    )(pmat, pmat, pmat, pmat, ss, dog, lb, hg)


def _shift_down(x, prev2, s):
    row = _iota2(x.shape, 0)
    y = pltpu.roll(x, s, 0)
    if s == 1:
        return jnp.where(row == 0, prev2[1:2, :], y)
    return jnp.where(row == 0, prev2[0:1, :], jnp.where(row == 1, prev2[1:2, :], y))


def _shift_up(x, next2, s):
    n = x.shape[0]
    row = _iota2(x.shape, 0)
    y = pltpu.roll(x, n - s, 0)
    if s == 1:
        return jnp.where(row == n - 1, next2[0:1, :], y)
    return jnp.where(row == n - 1, next2[1:2, :], jnp.where(row == n - 2, next2[0:1, :], y))


def _conv3(x, prev2, w):
    return w[0:1, :] * _shift_down(x, prev2, 2) + w[1:2, :] * _shift_down(x, prev2, 1) + w[2:3, :] * x


def _conv_gate_fwd(ug, uv, cwg, cwv, bl_, t, tc, name):
    n, ff = ug.shape
    nt = t // tc

    def body(ug_ref, uv_ref, wg_ref, wv_ref, a_ref, hg_ref, hv_ref):
        @pl.when(pl.program_id(1) == 0)
        def _():
            hg_ref[...] = jnp.zeros_like(hg_ref)
            hv_ref[...] = jnp.zeros_like(hv_ref)

        xg = ug_ref[...].astype(f32)
        xv = uv_ref[...].astype(f32)
        cg = _conv3(xg, hg_ref[...], wg_ref[...])
        cv = _conv3(xv, hv_ref[...], wv_ref[...])
        a_ref[...] = (cg * _sigmoid(cg) * cv).astype(a_ref.dtype)
        hg_ref[...] = xg[tc - 2:tc, :]
        hv_ref[...] = xv[tc - 2:tc, :]

    row = pl.BlockSpec((tc, ff), lambda b, i: (b * nt + i, 0))
    wsp = pl.BlockSpec((3, ff), lambda b, i: (0, 0))
    return pl.pallas_call(
        body, grid=(bl_, nt), name=name, in_specs=[row, row, wsp, wsp], out_specs=row,
        out_shape=SDS((n, ff), bf16),
        scratch_shapes=[pltpu.VMEM((2, ff), f32), pltpu.VMEM((2, ff), f32)], compiler_params=_params(),
    )(ug, uv, cwg, cwv)


def _conv_gate_bwd(ug, uv, cwg, cwv, da, bl_, t, tc, name):
    n, ff = ug.shape
    nt = t // tc
    per = tc // BF16_ROWS

    def body(ug_ref, uv_ref, pg_ref, pv_ref, wg_ref, wv_ref, da_ref, dug_ref, duv_ref, dwg_ref, dwv_ref, ng_ref, nv_ref):
        first = jnp.logical_and(pl.program_id(0) == 0, pl.program_id(1) == 0)

        @pl.when(first)
        def _():
            dwg_ref[...] = jnp.zeros_like(dwg_ref)
            dwv_ref[...] = jnp.zeros_like(dwv_ref)

        @pl.when(pl.program_id(1) == 0)
        def _():
            ng_ref[...] = jnp.zeros_like(ng_ref)
            nv_ref[...] = jnp.zeros_like(nv_ref)

        seq_start = pl.program_id(1) == nt - 1
        dav = da_ref[...].astype(f32)

        def half(u_ref, p_ref, w_ref):
            x = u_ref[...].astype(f32)
            prev2 = jnp.where(seq_start, 0.0, p_ref[BF16_ROWS - 2:BF16_ROWS, :].astype(f32))
            x1, x2 = _shift_down(x, prev2, 1), _shift_down(x, prev2, 2)
            w = w_ref[...]
            return x, x1, x2, w[0:1, :] * x2 + w[1:2, :] * x1 + w[2:3, :] * x

        xg, xg1, xg2, cg = half(ug_ref, pg_ref, wg_ref)
        xv, xv1, xv2, cv = half(uv_ref, pv_ref, wv_ref)
        sg = _sigmoid(cg)
        dcg = dav * cv * (sg * (1.0 + cg * (1.0 - sg)))
        dcv = dav * (cg * sg)

        def back(dc, x, x1, x2, w_ref, nx_ref, du_ref, dw_ref):
            w = w_ref[...]
            nx = nx_ref[...]
            du = w[2:3, :] * dc + w[1:2, :] * _shift_up(dc, nx, 1) + w[0:1, :] * _shift_up(dc, nx, 2)
            du_ref[...] = du.astype(du_ref.dtype)
            dw_ref[0:1, :] += jnp.sum(dc * x2, axis=0, keepdims=True)
            dw_ref[1:2, :] += jnp.sum(dc * x1, axis=0, keepdims=True)
            dw_ref[2:3, :] += jnp.sum(dc * x, axis=0, keepdims=True)
            nx_ref[...] = dc[0:2, :]

        back(dcg, xg, xg1, xg2, wg_ref, ng_ref, dug_ref, dwg_ref)
        back(dcv, xv, xv1, xv2, wv_ref, nv_ref, duv_ref, dwv_ref)

    row = pl.BlockSpec((tc, ff), lambda b, i: (b * nt + nt - 1 - i, 0))
    prev = pl.BlockSpec((BF16_ROWS, ff), lambda b, i: (jnp.maximum((b * nt + nt - 1 - i) * per - 1, 0), 0))
    wsp = pl.BlockSpec((3, ff), lambda b, i: (0, 0))
    return pl.pallas_call(
        body, grid=(bl_, nt), name=name, in_specs=[row, row, prev, prev, wsp, wsp, row],
        out_specs=[row, row, wsp, wsp],
        out_shape=[SDS((n, ff), bf16), SDS((n, ff), bf16), SDS((3, ff), f32), SDS((3, ff), f32)],
        scratch_shapes=[pltpu.VMEM((2, ff), f32), pltpu.VMEM((2, ff), f32)], compiler_params=_params(),
    )(ug, uv, ug, uv, cwg, cwv, da)


def _zf_c(hk, wzf, fgb, bl_, t, tm, name):
    n, d = hk.shape
    nt = t // tm

    def body(hk_ref, w_ref, b_ref, zf_ref, c_ref, carry_ref):
        @pl.when(pl.program_id(1) == 0)
        def _():
            carry_ref[...] = jnp.zeros_like(carry_ref)

        z = _bdot(hk_ref[...], w_ref[...]) + b_ref[...]
        ls = jnp.minimum(z, 0.0) - jnp.log(1.0 + jnp.exp(-jnp.abs(z)))
        c = _cumsum_rows(ls) + carry_ref[...]
        zf_ref[...] = z
        c_ref[...] = c
        carry_ref[...] = c[tm - 1:tm, :]

    row = lambda w: pl.BlockSpec((tm, w), lambda b, i: (b * nt + i, 0))
    return pl.pallas_call(
        body, grid=(bl_, nt), name=name,
        in_specs=[row(d), pl.BlockSpec((d, LANES), lambda b, i: (0, 0)), pl.BlockSpec((1, LANES), lambda b, i: (0, 0))],
        out_specs=[row(LANES), row(LANES)],
        out_shape=[SDS((n, LANES), f32), SDS((n, LANES), f32)],
        scratch_shapes=[pltpu.VMEM((1, LANES), f32)], compiler_params=_params(),
    )(hk, wzf, fgb)


def _c_bwd(dc, zf, bl_, t, tm, name):
    n = dc.shape[0]
    nt = t // tm

    def body(dc_ref, zf_ref, dzf_ref, dfg_ref, carry_ref):
        @pl.when(jnp.logical_and(pl.program_id(0) == 0, pl.program_id(1) == 0))
        def _():
            dfg_ref[...] = jnp.zeros_like(dfg_ref)

        @pl.when(pl.program_id(1) == 0)
        def _():
            carry_ref[...] = jnp.zeros_like(carry_ref)

        rc = _revcumsum_rows(dc_ref[...]) + carry_ref[...]
        dz = rc * _sigmoid(-zf_ref[...])
        dzf_ref[...] = dz.astype(dzf_ref.dtype)
        dfg_ref[...] += jnp.sum(dz, axis=0, keepdims=True)
        carry_ref[...] = rc[0:1, :]

    row = pl.BlockSpec((tm, LANES), lambda b, i: (b * nt + nt - 1 - i, 0))
    vec = pl.BlockSpec((1, LANES), lambda b, i: (0, 0))
    return pl.pallas_call(
        body, grid=(bl_, nt), name=name, in_specs=[row, row], out_specs=[row, vec],
        out_shape=[SDS((n, LANES), bf16), SDS((1, LANES), f32)],
        scratch_shapes=[pltpu.VMEM((1, LANES), f32)], compiler_params=_params(),
    )(dc, zf)


def _attn_fwd(q, kv, cq, ck, bl_, t, tq, hd, name):
    n, d = q.shape
    npair = d // LANES
    hp = LANES // hd
    nq = t // tq
    scale = 1.0 / (hd ** 0.5)

    def body(q_ref, k_ref, v_ref, cq_ref, ck_ref, o_ref, lse_ref):
        i = pl.program_id(2)
        rowg = i * tq + _iota2((tq, tq), 0)
        for hh in range(hp):
            lanes = slice(hh * hd, (hh + 1) * hd)
            qh = q_ref[:, lanes]
            ct = cq_ref[0, :, hh:hh + 1]

            def step(j, carry, lanes=lanes, qh=qh, ct=ct, hh=hh):
                m, l, acc = carry
                rows = pl.ds(pl.multiple_of(j * tq, BF16_ROWS), tq)
                s = _bdot_nt(qh, k_ref[rows, lanes]) * scale + (ct - ck_ref[0, 0, j, hh:hh + 1, :])
                s = jnp.where(rowg >= j * tq + _iota2((tq, tq), 1), s, -1e30)
                m2 = jnp.maximum(m, jnp.max(s, axis=-1, keepdims=True))
                p = jnp.exp(s - m2)
                a = jnp.exp(m - m2)
                return m2, a * l + jnp.sum(p, axis=-1, keepdims=True), a * acc + _bdot(p, v_ref[rows, lanes])

            m, l, acc = lax.fori_loop(
                0, i + 1, step, (jnp.full((tq, 1), -1e30, f32), jnp.zeros((tq, 1), f32), jnp.zeros((tq, hd), f32)))
            o_ref[:, lanes] = (acc / l).astype(o_ref.dtype)
            lse_ref[:, lanes] = jnp.broadcast_to(m + jnp.log(l), (tq, hd))

    nk = nq
    return pl.pallas_call(
        body, grid=(bl_, npair, nq), name=name,
        in_specs=[pl.BlockSpec((tq, LANES), lambda b, p, i: (b * nq + i, p)),
                  pl.BlockSpec((t, LANES), lambda b, p, i: (b, p)),
                  pl.BlockSpec((t, LANES), lambda b, p, i: (b, npair + p)),
                  pl.BlockSpec((1, tq, hp), lambda b, p, i: (p, b * nq + i, 0)),
                  pl.BlockSpec((1, 1, nk, hp, tq), lambda b, p, i: (b, p, 0, 0, 0))],
        out_specs=[pl.BlockSpec((tq, LANES), lambda b, p, i: (b * nq + i, p)),
                   pl.BlockSpec((tq, LANES), lambda b, p, i: (b * nq + i, p))],
        out_shape=[SDS((n, d), f32), SDS((n, d), f32)], compiler_params=_params(),
    )(q, kv, kv, cq, ck)


def _attn_bwd(q, kv, o, do, lse, cq, ck, bl_, t, tq, hd, name):
    n, d = q.shape
    npair = d // LANES
    hp = LANES // hd
    nq = t // tq
    scale = 1.0 / (hd ** 0.5)

    def body(q_ref, k_ref, v_ref, o_ref, do_ref, lse_ref, cq_ref, ck_ref, dq_ref, dk_ref, dv_ref, dck_ref, dcq_ref):
        j = pl.program_id(2)

        @pl.when(j == 0)
        def _():
            dq_ref[...] = jnp.zeros_like(dq_ref)
            dcq_ref[...] = jnp.zeros_like(dcq_ref)

        colg = j * tq + _iota2((tq, tq), 1)
        for hh in range(hp):
            lanes = slice(hh * hd, (hh + 1) * hd)
            kh = k_ref[:, lanes]
            vh = v_ref[:, lanes]
            cs = ck_ref[0, 0, 0, hh:hh + 1, :]

            def step(i, carry, lanes=lanes, kh=kh, vh=vh, cs=cs, hh=hh):
                dk, dv, dcs = carry
                rows = pl.ds(pl.multiple_of(i * tq, BF16_ROWS), tq)
                qh = q_ref[rows, lanes]
                doh = do_ref[rows, lanes]
                s = _bdot_nt(qh, kh) * scale + (cq_ref[0, rows, hh:hh + 1] - cs)
                s = jnp.where(i * tq + _iota2((tq, tq), 0) >= colg, s, -1e30)
                p = jnp.exp(s - lse_ref[rows, hh * hd:hh * hd + 1])
                delta = jnp.sum(doh.astype(f32) * o_ref[rows, lanes].astype(f32), axis=-1, keepdims=True)
                ds = p * (_bdot_nt(doh, vh) - delta)
                dq_ref[rows, lanes] += _bdot(ds, kh) * scale
                dcq_ref[0, rows, hh:hh + 1] += jnp.sum(ds, axis=-1, keepdims=True)
                return (dk + _bdot_tn(ds, qh) * scale, dv + _bdot_tn(p, doh), dcs - jnp.sum(ds, axis=0, keepdims=True))

            dk, dv, dcs = lax.fori_loop(
                j, nq, step, (jnp.zeros((tq, hd), f32), jnp.zeros((tq, hd), f32), jnp.zeros((1, tq), f32)))
            dk_ref[:, lanes] = dk.astype(dk_ref.dtype)
            dv_ref[:, lanes] = dv.astype(dv_ref.dtype)
            dck_ref[0, 0, 0, hh:hh + 1, :] = dcs

    whole = lambda c0: pl.BlockSpec((t, LANES), lambda b, p, j: (b, c0 + p))
    tile = lambda c0: pl.BlockSpec((tq, LANES), lambda b, p, j: (b * nq + j, c0 + p))
    ckspec = pl.BlockSpec((1, 1, 1, hp, tq), lambda b, p, j: (b, p, j, 0, 0))
    cqspec = pl.BlockSpec((1, t, hp), lambda b, p, j: (p, b, 0))
    return pl.pallas_call(
        body, grid=(bl_, npair, nq), name=name,
        in_specs=[whole(0), tile(0), tile(npair), whole(0), whole(0), whole(0), cqspec, ckspec],
        out_specs=[whole(0), tile(0), tile(0), ckspec, cqspec],
        out_shape=[SDS((n, d), f32), SDS((n, d), bf16), SDS((n, d), bf16), SDS((bl_, npair, nq, hp, tq), f32),
                   SDS((npair, n, hp), f32)],
        compiler_params=_params(),
    )(q, kv, kv, o, do, lse, cq, ck)


def _loss_head(h, target, t, nm, tm, name):
    n, d = h.shape
    nt = t // tm

    def body(h_ref, t_ref, loss_ref, dh_ref):
        i = pl.program_id(0)

        @pl.when(i == 0)
        def _():
            loss_ref[...] = jnp.zeros_like(loss_ref)

        pos = (i % nt) * tm + _iota2((tm, d), 0)
        err = jnp.where(pos >= nm, h_ref[...] - t_ref[...], 0.0)
        dh_ref[...] = err * (1.0 / d)
        loss_ref[...] += 0.5 * jnp.sum(jnp.mean(err * err, axis=-1, keepdims=True))

    row = pl.BlockSpec((tm, d), lambda i: (i, 0))
    return pl.pallas_call(
        body, grid=(n // tm,), name=name, in_specs=[row, row],
        out_specs=[pl.BlockSpec((8, LANES), lambda i: (0, 0)), row],
        out_shape=[SDS((8, LANES), f32), SDS((n, d), f32)], compiler_params=_params(),
    )(h, target)


def _c_layouts(c, bl_, t, tq, bh, hp):
    n = c.shape[0]
    npair = bh // hp
    nk = t // tq
    cc = c[:, :bh]
    cq = cc.reshape(n, npair, hp).transpose(1, 0, 2)
    ck = cc.reshape(bl_, nk, tq, npair, hp).transpose(0, 3, 1, 4, 2)
    return cq, ck


def _dc_rows(dck, dcq, bl_, t, bh):
    d = dck.transpose(0, 2, 4, 1, 3).reshape(bl_ * t, bh) + dcq.transpose(1, 0, 2).reshape(bl_ * t, bh)
    return jnp.pad(d, ((0, 0), (0, LANES - bh)))


_ANY = pl.BlockSpec(memory_space=pl.ANY)


def _all_gather(xs, name):
    na = len(xs)

    def body(*refs):
        x_refs, out_refs = refs[:na], refs[na:2 * na]
        send_sems, recv_sems, local_sems = refs[2 * na:]
        mx, my, mc = lax.axis_index("x"), lax.axis_index("y"), lax.axis_index("c")
        me, sibling = (mx, my, mc), (mx, my, 1 - mc)
        chips = [(1 - mx, my), (mx, 1 - my), (1 - mx, 1 - my)]

        def copy(a, k, block, to, own=False):
            px, py, pc = block
            rows = out_refs[a].at[4 * px + 2 * py + pc]
            return pltpu.make_async_remote_copy(
                src_ref=x_refs[a] if own else rows, dst_ref=rows,
                send_sem=send_sems.at[a, k], recv_sem=recv_sems.at[a, k], device_id=to, device_id_type=MESH)

        arrays = range(na)
        mine = [pltpu.make_async_copy(x_refs[a], out_refs[a].at[4 * mx + 2 * my + mc], local_sems.at[a]) for a in arrays]
        for cp in mine:
            cp.start()
        first = [copy(a, 1 + j, me, (*chip, mc), own=True) for j, chip in enumerate(chips) for a in arrays]
        first += [copy(a, 0, me, sibling, own=True) for a in arrays]
        for cp in first:
            cp.start()
        passed = []
        for j, chip in enumerate(chips):
            for a in arrays:
                copy(a, 1 + j, (*chip, mc), me).wait_recv()
                cp = copy(a, 4 + j, (*chip, mc), sibling)
                cp.start()
                passed.append(cp)
        for a in arrays:
            copy(a, 0, sibling, me).wait_recv()
        for j, chip in enumerate(chips):
            for a in arrays:
                copy(a, 4 + j, (*chip, 1 - mc), me).wait_recv()
        for cp in first + passed:
            cp.wait_send()
        for cp in mine:
            cp.wait()

    return pl.pallas_call(
        body, name=name, out_shape=[SDS((N_DEV,) + x.shape, x.dtype) for x in xs],
        in_specs=[_ANY] * na, out_specs=[_ANY] * na,
        scratch_shapes=[pltpu.SemaphoreType.DMA((na, 7)), pltpu.SemaphoreType.DMA((na, 7)), pltpu.SemaphoreType.DMA((na,))],
    )(*xs)


def _exchange(bufs, group, name):
    na = len(bufs)
    g = 2 if group == "c" else 4
    assert all(b.shape[0] == g for b in bufs)

    def body(*refs):
        buf_refs, out_refs = refs[:na], refs[na:2 * na]
        send_sems, recv_sems, local_sems = refs[2 * na:]
        mx, my, mc = lax.axis_index("x"), lax.axis_index("y"), lax.axis_index("c")
        if group == "c":
            me = mc
            peers = [((mx, my, 1 - mc), 1 - mc)]
        else:
            me = 2 * mx + my
            peers = []
            for r in range(1, 4):
                px = 1 - mx if r & 2 else mx
                py = 1 - my if r & 1 else my
                peers.append(((px, py, mc), 2 * px + py))

        def copy(a, k, dev, src_row, dst_row):
            return pltpu.make_async_remote_copy(
                src_ref=buf_refs[a].at[src_row], dst_ref=out_refs[a].at[dst_row],
                send_sem=send_sems.at[a, k], recv_sem=recv_sems.at[a, k], device_id=dev, device_id_type=MESH)

        sends = [copy(a, k, dev, idx, me) for k, (dev, idx) in enumerate(peers) for a in range(na)]
        for cp in sends:
            cp.start()
        own = [pltpu.make_async_copy(buf_refs[a].at[me], out_refs[a].at[me], local_sems.at[a]) for a in range(na)]
        for cp in own:
            cp.start()
        for k, (dev, idx) in enumerate(peers):
            for a in range(na):
                copy(a, k, dev, idx, idx).wait_recv()
        for cp in sends:
            cp.wait_send()
        for cp in own:
            cp.wait()

    return pl.pallas_call(
        body, name=name, out_shape=[SDS(b.shape, b.dtype) for b in bufs], in_specs=[_ANY] * na, out_specs=[_ANY] * na,
        scratch_shapes=[pltpu.SemaphoreType.DMA((na, g - 1)), pltpu.SemaphoreType.DMA((na, g - 1)),
                        pltpu.SemaphoreType.DMA((na,))],
    )(*bufs)


def _sum_rows(parts, tr, name):
    g, r, c = parts.shape

    def body(p_ref, o_ref):
        acc = p_ref[0].astype(f32)
        for k in range(1, g):
            acc = acc + p_ref[k].astype(f32)
        o_ref[...] = acc.astype(o_ref.dtype)

    return pl.pallas_call(
        body, grid=(r // tr,), name=name, in_specs=[pl.BlockSpec((g, tr, c), lambda i: (0, i, 0))],
        out_specs=pl.BlockSpec((tr, c), lambda i: (i, 0)), out_shape=SDS((r, c), parts.dtype), compiler_params=_params(),
    )(parts)


def _adamw(parts, w, m, v, tr, name):
    g, r, c = parts.shape

    def body(p_ref, w_ref, m_ref, v_ref, g_ref, d_ref, m2_ref, v2_ref):
        gr = p_ref[0].astype(f32)
        for k in range(1, g):
            gr = gr + p_ref[k].astype(f32)
        m2 = ADAM_B1 * m_ref[...] + (1.0 - ADAM_B1) * gr
        v2 = ADAM_B2 * v_ref[...] + (1.0 - ADAM_B2) * (gr * gr)
        m_hat = m2 / (1.0 - ADAM_B1 ** ADAM_STEP)
        v_hat = v2 / (1.0 - ADAM_B2 ** ADAM_STEP)
        g_ref[...] = gr
        d_ref[...] = -ADAM_LR * (m_hat / (jnp.sqrt(v_hat) + ADAM_EPS) + ADAM_WD * w_ref[...])
        m2_ref[...] = m2
        v2_ref[...] = v2

    row = pl.BlockSpec((tr, c), lambda i: (i, 0))
    return pl.pallas_call(
        body, grid=(r // tr,), name=name, in_specs=[pl.BlockSpec((g, tr, c), lambda i: (0, i, 0)), row, row, row],
        out_specs=[row] * 4, out_shape=[SDS((r, c), f32)] * 4, compiler_params=_params(),
    )(parts, w, m, v)


_SHARD_AXIS = dict(meta_tokens=1, norm_gains=2, a_w_in=2, a_lb_logits=1, a_head_norm=1, a_w_out=1, kv_w=1,
                   b_w_q=1, b_w_out=1, ffn_w_up=2, ffn_conv=2, ffn_w_down=1)
_MATRICES = ("a_w_in", "a_w_out", "kv_w", "b_w_q", "b_w_out", "ffn_w_up", "ffn_w_down")
_VECTORS = ("meta_tokens", "norm_gains", "a_lb_logits", "a_head_norm", "ffn_conv")
_SHARDED = tuple(_SHARD_AXIS)
_REPLICATED = ("kv_norm", "fg_b")
_ROW_TILE_CAP = 512


def _pack(arrs, dtype, cols, row_mult):
    lead = arrs[0].shape[:-1] if arrs[0].ndim > 1 else ()
    flat = jnp.concatenate([a.astype(dtype) for a in arrs], axis=-1)
    size = flat.shape[-1]
    per = cols * row_mult
    total = -(-size // per) * per
    flat = jnp.pad(flat, [(0, 0)] * len(lead) + [(0, total - size)])
    return flat.reshape(lead + (total // cols, cols))


def _unpack(flat, shapes):
    out, off = [], 0
    lead = flat.shape[:-1]
    for shp in shapes:
        size = 1
        for s in shp:
            size *= s
        out.append(flat[..., off:off + size].reshape(lead + tuple(shp)))
        off += size
    return out


def _unshard(seg, axis):
    a = jnp.moveaxis(seg, 0, axis)
    shp = a.shape
    return a.reshape(shp[:axis] + (shp[axis] * shp[axis + 1],) + shp[axis + 2:])


def _shard8(full, axis):
    shp = full.shape
    a = full.reshape(shp[:axis] + (N_DEV, shp[axis] // N_DEV) + shp[axis + 1:])
    return jnp.moveaxis(a, axis, 0)


def _rows(a, lead=0):
    return a.reshape(a.shape[:lead] + (-1, a.shape[-1]))


def _core_major(a):
    _, r, c = a.shape
    return a.reshape(4, 2, r, c).transpose(1, 0, 2, 3).reshape(2, 4 * r, c)


def kernel(x, meta_tokens, norm_gains, a_w_in, a_lb_logits, a_head_norm, a_w_out, kv_norm, kv_w, fg_b, b_w_q, b_w_out, ffn_w_up, ffn_conv, ffn_w_down, loss_target, m_meta_tokens, m_norm_gains, m_a_w_in, m_a_lb_logits, m_a_head_norm, m_a_w_out, m_kv_norm, m_kv_w, m_fg_b, m_b_w_q, m_b_w_out, m_ffn_w_up, m_ffn_conv, m_ffn_w_down, v_meta_tokens, v_norm_gains, v_a_w_in, v_a_lb_logits, v_a_head_norm, v_a_w_out, v_kv_norm, v_kv_w, v_fg_b, v_b_w_q, v_b_w_out, v_ffn_w_up, v_ffn_conv, v_ffn_w_down):
    names = ("meta_tokens", "norm_gains", "a_w_in", "a_lb_logits", "a_head_norm", "a_w_out", "kv_norm", "kv_w", "fg_b",
             "b_w_q", "b_w_out", "ffn_w_up", "ffn_conv", "ffn_w_down")
    w = dict(zip(names, (meta_tokens, norm_gains, a_w_in, a_lb_logits, a_head_norm, a_w_out, kv_norm, kv_w, fg_b,
                         b_w_q, b_w_out, ffn_w_up, ffn_conv, ffn_w_down)))
    mom = dict(zip(names, (m_meta_tokens, m_norm_gains, m_a_w_in, m_a_lb_logits, m_a_head_norm, m_a_w_out, m_kv_norm,
                           m_kv_w, m_fg_b, m_b_w_q, m_b_w_out, m_ffn_w_up, m_ffn_conv, m_ffn_w_down)))
    var = dict(zip(names, (v_meta_tokens, v_norm_gains, v_a_w_in, v_a_lb_logits, v_a_head_norm, v_a_w_out, v_kv_norm,
                           v_kv_w, v_fg_b, v_b_w_q, v_b_w_out, v_ffn_w_up, v_ffn_conv, v_ffn_w_down)))

    bl_, seq, d = x.shape
    nm = meta_tokens.shape[0]
    t = nm + seq
    n = bl_ * t
    bh = fg_b.shape[0]
    hd = d // bh
    hp = LANES // hd
    ff = ffn_w_down.shape[1] * N_DEV
    tm = _div_tile(t, 1024)
    tc = _div_tile(t, 64)
    tn = 512

    vec_pack = _pack([w[k].reshape(-1) for k in _VECTORS], f32, LANES, 8)
    got = _all_gather([w[k].astype(bf16) for k in _MATRICES] + [vec_pack], "gather_weights")
    big = {k: _unshard(a, _SHARD_AXIS[k]) for k, a in zip(_MATRICES, got)}
    vec_segs = _unpack(got[-1].reshape(N_DEV, -1), [w[k].shape for k in _VECTORS])
    small = {k: _unshard(a, _SHARD_AXIS[k]) for k, a in zip(_VECTORS, vec_segs)}
    w_in, w_out_a = big["a_w_in"][0], big["a_w_out"][0]
    w_kv, w_zf = big["kv_w"][:, :2 * d], jnp.pad(big["kv_w"][:, 2 * d:], ((0, 0), (0, LANES - bh)))
    w_q, w_out_b = big["b_w_q"][0], big["b_w_out"][0]
    w_gate, w_val, w_down = big["ffn_w_up"][:, :, :ff], big["ffn_w_up"][:, :, ff:], big["ffn_w_down"]
    gains = small["norm_gains"]
    gain = lambda l, j: gains[l, j][None]
    cw_gate, cw_val = small["ffn_conv"][:, :, :ff], small["ffn_conv"][:, :, ff:]
    head_gain = small["a_head_norm"]
    lb = jax.nn.softmax(small["a_lb_logits"], axis=0)[0:1]
    kvn = kv_norm[None]
    fgb_pad = jnp.pad(fg_b, (0, LANES - bh))[None]

    h0 = jnp.concatenate([jnp.broadcast_to(small["meta_tokens"][None], (bl_, nm, d)), x], axis=1).reshape(n, d)

    def ffn_fwd(l, h_in):
        fi = _rms_fwd(h_in, gain(l, 2), tm, f"ffn{l}_norm")
        ug = _mm(fi, w_gate[l], bf16, tm, ff, f"ffn{l}_up_gate")
        uv = _mm(fi, w_val[l], bf16, tm, ff, f"ffn{l}_up_val")
        act = _conv_gate_fwd(ug, uv, cw_gate[l], cw_val[l], bl_, t, tc, f"ffn{l}_conv_gate")
        h_out, mix = _mm_norm_res(act, w_down[l], gain(l, 3), h_in, tm, f"ffn{l}_down")
        return h_out, (h_in, fi, ug, uv, act, mix)

    hn0 = _rms_fwd(h0, gain(0, 0), tm, "a_norm")
    pmat = _mm(hn0, w_in, f32, tm, tn, "a_in_proj")
    og, states = _gla_fwd(pmat, lb, head_gain, bl_, t, nm, "a_gla_fwd")
    h1, mix_a = _mm_norm_res(og, w_out_a, gain(0, 1), h0, tm, "a_out_proj")
    h2, ffn0 = ffn_fwd(0, h1)

    hk = _rms_fwd(h2, kvn, tm, "kv_norm")
    kvp = _mm(hk, w_kv, bf16, tm, tn, "kv_proj")
    zf, cum = _zf_c(hk, w_zf, fgb_pad, bl_, t, tm, "forget_cumsum")
    cq, ck = _c_layouts(cum, bl_, t, tm, bh, hp)
    hn1 = _rms_fwd(h2, gain(1, 0), tm, "b_norm")
    q = _mm(hn1, w_q, bf16, tm, tn, "b_q_proj")
    o, lse = _attn_fwd(q, kvp, cq, ck, bl_, t, tm, hd, "b_attn_fwd")
    h3, mix_b = _mm_norm_res(o, w_out_b, gain(1, 1), h2, tm, "b_out_proj")
    h4, ffn1 = ffn_fwd(1, h3)

    target = jnp.concatenate([jnp.zeros((bl_, nm, d), f32), loss_target], axis=1).reshape(n, d)
    loss8, dh = _loss_head(h4, target, t, nm, tm, "loss_head")
    loss = lax.psum(loss8[0, 0], ("x", "y", "c"))

    dgain = {}

    def ffn_bwd(l, saved, dh_out):
        h_in, fi, ug, uv, act, mix = saved
        dmix, dgain[l, 3] = _rms_bwd(mix, gain(l, 3), dh_out, None, bf16, tm, f"ffn{l}_down_norm_bwd")
        dact = _mm_nt([(dmix, w_down[l])], bf16, tm, ff, f"ffn{l}_down_dx")
        dw_down = _mm_tn(act, dmix, tm, ff, tn, f"ffn{l}_down_dw")
        dug, duv, dcg, dcv = _conv_gate_bwd(ug, uv, cw_gate[l], cw_val[l], dact, bl_, t, tc, f"ffn{l}_conv_gate_bwd")
        dfi = _mm_nt([(dug, w_gate[l]), (duv, w_val[l])], bf16, tm, 256, f"ffn{l}_up_dx")
        dw_up = jnp.concatenate([_mm_tn(fi, dug, tm, tn, ff, f"ffn{l}_up_gate_dw"),
                                 _mm_tn(fi, duv, tm, tn, ff, f"ffn{l}_up_val_dw")], axis=1)
        dh_in, dgain[l, 2] = _rms_bwd(h_in, gain(l, 2), dfi, dh_out, f32, tm, f"ffn{l}_norm_bwd")
        return dh_in, dw_up, jnp.concatenate([dcg, dcv], axis=1), dw_down

    dh, dw_up1, dconv1, dw_down1 = ffn_bwd(1, ffn1, dh)

    dmix, dgain[1, 1] = _rms_bwd(mix_b, gain(1, 1), dh, None, bf16, tm, "b_out_norm_bwd")
    do = _mm_nt([(dmix, w_out_b)], bf16, tm, tn, "b_out_dx")
    dw_out_b = _mm_tn(o, dmix, tm, tn, tn, "b_out_dw")
    dq, dk, dv, dck, dcq = _attn_bwd(q, kvp, o, do, lse, cq, ck, bl_, t, tm, hd, "b_attn_bwd")
    dhn1 = _mm_nt([(dq, w_q)], bf16, tm, tn, "b_q_dx")
    dw_q = _mm_tn(hn1, dq, tm, tn, tn, "b_q_dw")
    dh, dgain[1, 0] = _rms_bwd(h2, gain(1, 0), dhn1, dh, f32, tm, "b_norm_bwd")

    dzf, dfgb = _c_bwd(_dc_rows(dck, dcq, bl_, t, bh), zf, bl_, t, tm, "forget_cumsum_bwd")
    dhk = _mm_nt([(dk, w_kv[:, :d]), (dv, w_kv[:, d:]), (dzf, w_zf)], bf16, tm, tn, "kv_dx")
    dw_kv = jnp.concatenate([_mm_tn(hk, dk, tm, tn, tn, "k_dw"), _mm_tn(hk, dv, tm, tn, tn, "v_dw"),
                             _mm_tn(hk, dzf, tm, tn, LANES, "zf_dw")[:, :bh]], axis=1)
    dh, dkvn = _rms_bwd(h2, kvn, dhk, dh, f32, tm, "kv_norm_bwd")

    dh, dw_up0, dconv0, dw_down0 = ffn_bwd(0, ffn0, dh)

    dmix, dgain[0, 1] = _rms_bwd(mix_a, gain(0, 1), dh, None, bf16, tm, "a_out_norm_bwd")
    dog = _mm_nt([(dmix, w_out_a)], bf16, tm, tn, "a_out_dx")
    dw_out_a = _mm_tn(og, dmix, tm, tn, tn, "a_out_dw")
    dpq, dpf, dpi, dpg, dlb, dhg = _gla_bwd(pmat, states, dog, lb, head_gain, bl_, t, nm, "a_gla_bwd")
    dps = (dpq, dpf, dpi, dpg)
    dhn0 = _mm_nt([(dp, w_in[:, j * d:(j + 1) * d]) for j, dp in enumerate(dps)], bf16, tm, tn, "a_in_dx")
    dw_in = jnp.concatenate([_mm_tn(hn0, dp, tm, tn, tn, f"a_in_dw{j}") for j, dp in enumerate(dps)], axis=1)
    dh, dgain[0, 0] = _rms_bwd(h0, gain(0, 0), dhn0, dh, f32, tm, "a_norm_bwd")

    dh = dh.reshape(bl_, t, d)
    grad_x = dh[:, nm:]
    dl0 = dlb * lb * (1.0 - lb)
    grads = dict(
        meta_tokens=jnp.sum(dh[:, :nm], axis=0),
        norm_gains=jnp.stack([jnp.concatenate([dgain[l, j] for j in range(4)], axis=0) for l in range(2)]),
        a_w_in=dw_in[None], a_lb_logits=jnp.concatenate([dl0, -dl0], axis=0), a_head_norm=dhg, a_w_out=dw_out_a[None],
        kv_w=dw_kv, b_w_q=dw_q[None], b_w_out=dw_out_b[None],
        ffn_w_up=jnp.stack([dw_up0, dw_up1]), ffn_conv=jnp.stack([dconv0, dconv1]),
        ffn_w_down=jnp.stack([dw_down0, dw_down1]))

    send = [_rows(_shard8(grads[k], _SHARD_AXIS[k]), 1).astype(bf16) for k in _MATRICES]
    send.append(_pack([_shard8(grads[k], _SHARD_AXIS[k]).reshape(N_DEV, -1) for k in _VECTORS], bf16, LANES, BF16_ROWS))
    pair = _exchange([_core_major(a) for a in send], "c", "grad_exchange_cores")
    chip_sum = [_sum_rows(a, _div_tile(a.shape[1], _ROW_TILE_CAP), f"grad_sum_cores{i}") for i, a in enumerate(pair)]
    parts = _exchange([a.reshape((4, -1) + a.shape[1:]) for a in chip_sum], "xy", "grad_exchange_chips")
    g_s, d_s, m_s, v_s = {}, {}, {}, {}

    def update(part, srcs, label):
        return _adamw(part, *srcs, _div_tile(part.shape[1], _ROW_TILE_CAP), label)

    for k, part in zip(_MATRICES, parts):
        res = update(part, [_rows(src[k]) for src in (w, mom, var)], f"adamw_{k}")
        g_s[k], d_s[k], m_s[k], v_s[k] = (r.reshape(w[k].shape) for r in res)
    vec_packs = [_pack([src[k].reshape(-1) for k in _VECTORS], f32, LANES, BF16_ROWS) for src in (w, mom, var)]
    vec_shapes = [w[k].shape for k in _VECTORS]
    for dst, r in zip((g_s, d_s, m_s, v_s), update(parts[-1], vec_packs, "adamw_vectors")):
        dst.update(zip(_VECTORS, _unpack(r.reshape(-1), vec_shapes)))

    rep_local = _pack([dkvn.reshape(-1), dfgb[0, :bh]], f32, LANES, 8)
    rep_parts = _all_gather([rep_local], "gather_replicated_grads")[0]
    rpacks = [_pack([src[k].reshape(-1) for k in _REPLICATED], f32, LANES, 8) for src in (w, mom, var)]
    rres = _adamw(rep_parts, *rpacks, rep_local.shape[0], "adamw_replicated")
    rshapes = [w[k].shape for k in _REPLICATED]
    g_r, d_r, m_r, v_r = ({k: a for k, a in zip(_REPLICATED, _unpack(r.reshape(-1), rshapes))} for r in rres)

    out = [loss, grad_x]
    for sh, rp in ((g_s, g_r), (d_s, d_r), (m_s, m_r), (v_s, v_r)):
        out += [sh[k] if k in sh else rp[k] for k in names]
    return tuple(out)
```

```python
import functools

import jax
import jax.numpy as jnp
from jax import lax
from jax.experimental import pallas as pl
from jax.experimental.pallas import tpu as pltpu

f32 = jnp.float32
bf16 = jnp.bfloat16
SDS = jax.ShapeDtypeStruct

EPS = 1e-6
A_DK = 128
A_CHUNK = 64
LANES = 128
BF16_ROWS = 16
VMEM_LIMIT = 56 * 1024 * 1024
ADAM_LR, ADAM_B1, ADAM_B2, ADAM_EPS, ADAM_WD, ADAM_STEP = 0.001, 0.9, 0.999, 1e-08, 0.01, 10
N_DEV = 8
MESH = pl.DeviceIdType.MESH

_NT = (((1,), (1,)), ((), ()))
_TN = (((0,), (0,)), ((), ()))
_HI = lax.Precision.HIGHEST


def _params(**kw):
    return pltpu.CompilerParams(vmem_limit_bytes=VMEM_LIMIT, **kw)


def _div_tile(n, cap, mult=BF16_ROWS):
    best = None
    for t in range(mult, min(n, cap) + 1, mult):
        if n % t == 0:
            best = t
    assert best is not None, (n, cap, mult)
    return best


def _bdot(a, b):
    return jnp.dot(a.astype(bf16), b.astype(bf16), preferred_element_type=f32)


def _bdot_nt(a, b):
    return lax.dot_general(a.astype(bf16), b.astype(bf16), _NT, preferred_element_type=f32)


def _bdot_tn(a, b):
    return lax.dot_general(a.astype(bf16), b.astype(bf16), _TN, preferred_element_type=f32)


def _iota2(shape, axis):
    return lax.broadcasted_iota(jnp.int32, shape, axis)


def _cumsum_rows(x):
    n = x.shape[0]
    tri = (_iota2((n, n), 0) >= _iota2((n, n), 1)).astype(f32)
    return jnp.dot(tri, x, precision=_HI, preferred_element_type=f32)


def _revcumsum_rows(x):
    n = x.shape[0]
    tri = (_iota2((n, n), 1) >= _iota2((n, n), 0)).astype(f32)
    return jnp.dot(tri, x, precision=_HI, preferred_element_type=f32)


def _sigmoid(x):
    return 1.0 / (1.0 + jnp.exp(-x))


def _rms_fwd(x, g, tm, name):
    n, d = x.shape

    def body(x_ref, g_ref, o_ref):
        xv = x_ref[...]
        r = lax.rsqrt(jnp.mean(xv * xv, axis=-1, keepdims=True) + EPS)
        o_ref[...] = (xv * r * g_ref[...]).astype(o_ref.dtype)

    return pl.pallas_call(
        body, grid=(n // tm,), name=name,
        in_specs=[pl.BlockSpec((tm, d), lambda i: (i, 0)), pl.BlockSpec((1, d), lambda i: (0, 0))],
        out_specs=pl.BlockSpec((tm, d), lambda i: (i, 0)),
        out_shape=SDS((n, d), bf16), compiler_params=_params(),
    )(x, g)


def _mm(a, w, out_dtype, tm, tn, name):
    n, k = a.shape
    m = w.shape[1]

    def body(a_ref, w_ref, o_ref):
        o_ref[...] = _bdot(a_ref[...], w_ref[...]).astype(o_ref.dtype)

    return pl.pallas_call(
        body, grid=(m // tn, n // tm), name=name,
        in_specs=[pl.BlockSpec((tm, k), lambda j, i: (i, 0)), pl.BlockSpec((k, tn), lambda j, i: (0, j))],
        out_specs=pl.BlockSpec((tm, tn), lambda j, i: (i, j)),
        out_shape=SDS((n, m), out_dtype), compiler_params=_params(),
    )(a, w)


def _mm_norm_res(a, w, g, h, tm, name):
    n, k = a.shape
    d = w.shape[1]

    def body(a_ref, w_ref, g_ref, h_ref, hn_ref, mix_ref):
        mix = _bdot(a_ref[...], w_ref[...])
        r = lax.rsqrt(jnp.mean(mix * mix, axis=-1, keepdims=True) + EPS)
        mix_ref[...] = mix
        hn_ref[...] = h_ref[...] + mix * r * g_ref[...]

    return pl.pallas_call(
        body, grid=(n // tm,), name=name,
        in_specs=[pl.BlockSpec((tm, k), lambda i: (i, 0)), pl.BlockSpec((k, d), lambda i: (0, 0)),
                  pl.BlockSpec((1, d), lambda i: (0, 0)), pl.BlockSpec((tm, d), lambda i: (i, 0))],
        out_specs=[pl.BlockSpec((tm, d), lambda i: (i, 0)), pl.BlockSpec((tm, d), lambda i: (i, 0))],
        out_shape=[SDS((n, d), f32), SDS((n, d), f32)], compiler_params=_params(),
    )(a, w, g, h)


def _rms_bwd(x, g, dy, dh_in, out_dtype, tm, name):
    n, d = x.shape
    has_add = dh_in is not None

    def body(*refs):
        if has_add:
            x_ref, g_ref, dy_ref, dh_ref, o_ref, dg_ref = refs
        else:
            x_ref, g_ref, dy_ref, o_ref, dg_ref = refs
        xv = x_ref[...]
        dyv = dy_ref[...].astype(f32)
        r = lax.rsqrt(jnp.mean(xv * xv, axis=-1, keepdims=True) + EPS)
        xr = xv * r
        gdy = dyv * g_ref[...]
        dx = r * gdy - xr * (r * r) * jnp.mean(xv * gdy, axis=-1, keepdims=True)
        if has_add:
            dx = dx + dh_ref[...]
        o_ref[...] = dx.astype(o_ref.dtype)

        @pl.when(pl.program_id(0) == 0)
        def _():
            dg_ref[...] = jnp.zeros_like(dg_ref)

        dg_ref[...] += jnp.sum(dyv * xr, axis=0, keepdims=True)

    row = pl.BlockSpec((tm, d), lambda i: (i, 0))
    vec = pl.BlockSpec((1, d), lambda i: (0, 0))
    ins = [x, g, dy] + ([dh_in] if has_add else [])
    return pl.pallas_call(
        body, grid=(n // tm,), name=name,
        in_specs=[row, vec, row] + ([row] if has_add else []),
        out_specs=[row, vec],
        out_shape=[SDS((n, d), out_dtype), SDS((1, d), f32)], compiler_params=_params(),
    )(*ins)


def _mm_nt(pairs, out_dtype, tm, tk, name):
    n = pairs[0][0].shape[0]
    k = pairs[0][1].shape[0]
    np_ = len(pairs)

    def body(*refs):
        o_ref = refs[-1]
        acc = None
        for p in range(np_):
            t = _bdot_nt(refs[2 * p][...], refs[2 * p + 1][...])
            acc = t if acc is None else acc + t
        o_ref[...] = acc.astype(o_ref.dtype)

    in_specs, ins = [], []
    for dy, w in pairs:
        m = dy.shape[1]
        in_specs += [pl.BlockSpec((tm, m), lambda j, i: (i, 0)), pl.BlockSpec((tk, m), lambda j, i: (j, 0))]
        ins += [dy, w]
    return pl.pallas_call(
        body, grid=(k // tk, n // tm), name=name, in_specs=in_specs,
        out_specs=pl.BlockSpec((tm, tk), lambda j, i: (i, j)),
        out_shape=SDS((n, k), out_dtype), compiler_params=_params(),
    )(*ins)


def _mm_tn(x, dy, tm, tk, tn, name):
    n, k = x.shape
    m = dy.shape[1]

    def body(x_ref, dy_ref, o_ref):
        @pl.when(pl.program_id(2) == 0)
        def _():
            o_ref[...] = jnp.zeros_like(o_ref)

        o_ref[...] += _bdot_tn(x_ref[...], dy_ref[...])

    return pl.pallas_call(
        body, grid=(k // tk, m // tn, n // tm), name=name,
        in_specs=[pl.BlockSpec((tm, tk), lambda a, b, i: (i, a)), pl.BlockSpec((tm, tn), lambda a, b, i: (i, b))],
        out_specs=pl.BlockSpec((tk, tn), lambda a, b, i: (a, b)),
        out_shape=SDS((k, m), f32), compiler_params=_params(),
    )(x, dy)


def _gla_chunk_fwd(qc, fc, vc, lb, st):
    c = qc.shape[0]
    sg = _sigmoid(fc)
    f = lb + (1.0 - lb) * sg
    k = 1.0 - f
    b = _cumsum_rows(jnp.log(f))
    bl = b[c - 1:c, :]
    e = jnp.exp(b)
    ebl = jnp.exp(bl)
    qi = qc * e
    ki = k * jnp.exp(-b)
    ko = k * jnp.exp(bl - b)
    causal = _iota2((c, c), 0) >= _iota2((c, c), 1)
    att = jnp.where(causal, _bdot_nt(qi, ki), 0.0)
    o = _bdot(att, vc) + _bdot_nt(qi, st)
    st_new = st * ebl + _bdot_tn(vc, ko)
    return dict(sg=sg, f=f, k=k, b=b, bl=bl, e=e, ebl=ebl, qi=qi, ki=ki, ko=ko, att=att, o=o, st_new=st_new, causal=causal)


def _head_out(o, ggc, hg):
    r = lax.rsqrt(jnp.mean(o * o, axis=-1, keepdims=True) + EPS)
    return o * r * hg * (ggc * _sigmoid(ggc))


def _gla_fwd(pmat, lb, hg, bl_, t, nm, name):
    n, d4 = pmat.shape
    d = d4 // 4
    nh = d // A_DK
    nreal = (t - nm) // A_CHUNK
    nch = nreal + 1

    def body(q_ref, f_ref, i_ref, gg_ref, lb_ref, hg_ref, og_ref, ss_ref):
        lbv, hgv = lb_ref[...], hg_ref[...]

        def run(rows, st, idx):
            ss_ref[0, idx] = st
            w = _gla_chunk_fwd(q_ref[rows, :], f_ref[rows, :], i_ref[rows, :], lbv, st)
            og_ref[rows, :] = _head_out(w["o"], gg_ref[rows, :], hgv).astype(og_ref.dtype)
            return w["st_new"]

        st = run(pl.ds(0, nm), jnp.zeros((A_DK, A_DK), f32), 0)

        def step(c, st):
            return run(pl.ds(pl.multiple_of(nm + c * A_CHUNK, BF16_ROWS), A_CHUNK), st, c + 1)

        lax.fori_loop(0, nreal, step, st)

    col = lambda o: pl.BlockSpec((t, A_DK), lambda b, h: (b, o * nh + h))
    vec = pl.BlockSpec((1, A_DK), lambda b, h: (0, h))
    return pl.pallas_call(
        body, grid=(bl_, nh), name=name,
        in_specs=[col(0), col(1), col(2), col(3), vec, vec],
        out_specs=[pl.BlockSpec((t, A_DK), lambda b, h: (b, h)),
                   pl.BlockSpec((1, nch, A_DK, A_DK), lambda b, h: (b * nh + h, 0, 0, 0))],
        out_shape=[SDS((n, d), bf16), SDS((bl_ * nh, nch, A_DK, A_DK), f32)], compiler_params=_params(),
    )(pmat, pmat, pmat, pmat, lb, hg)


def _gla_bwd(pmat, ss, dog, lb, hg, bl_, t, nm, name):
    n, d4 = pmat.shape
    d = d4 // 4
    nh = d // A_DK
    nreal = (t - nm) // A_CHUNK
    nch = nreal + 1

    def body(q_ref, f_ref, i_ref, gg_ref, ss_ref, dog_ref, lb_ref, hg_ref,
             dq_ref, df_ref, di_ref, dgg_ref, dlb_ref, dhg_ref):
        lbv, hgv = lb_ref[...], hg_ref[...]

        def run(rows, idx, carry):
            dst, dlb, dhg = carry
            qc, fc, vc, ggc = q_ref[rows, :], f_ref[rows, :], i_ref[rows, :], gg_ref[rows, :]
            dogc = dog_ref[rows, :].astype(f32)
            st = ss_ref[0, idx]
            w = _gla_chunk_fwd(qc, fc, vc, lbv, st)
            c = qc.shape[0]
            o, qi, ki, ko = w["o"], w["qi"], w["ki"], w["ko"]
            r = lax.rsqrt(jnp.mean(o * o, axis=-1, keepdims=True) + EPS)
            sgg = _sigmoid(ggc)
            sil = ggc * sgg
            on = o * r
            dhg = dhg + jnp.sum(dogc * sil * on, axis=0, keepdims=True)
            dgg_ref[rows, :] = (dogc * on * hgv * (sgg * (1.0 + ggc * (1.0 - sgg)))).astype(dgg_ref.dtype)
            tt = dogc * sil * hgv
            do = r * tt - on * (r * r) * jnp.mean(o * tt, axis=-1, keepdims=True)
            dst_in = dst * w["ebl"] + _bdot_tn(do, qi)
            d_ebl = jnp.sum(dst * st, axis=0, keepdims=True)
            datt = jnp.where(w["causal"], _bdot_nt(do, vc), 0.0)
            dv = _bdot_tn(w["att"], do) + _bdot_nt(ko, dst)
            dko = _bdot(vc, dst)
            dqi = _bdot(datt, ki) + _bdot(do, st)
            dki = _bdot_tn(datt, qi)
            dk = dki * jnp.exp(-w["b"]) + dko * jnp.exp(w["bl"] - w["b"])
            db = dqi * qi - dki * ki - dko * ko
            dbl = jnp.sum(dko * ko, axis=0, keepdims=True) + d_ebl * w["ebl"]
            db = db + jnp.where(_iota2(db.shape, 0) == c - 1, dbl, 0.0)
            dlogf = _revcumsum_rows(db)
            df = dlogf / w["f"] - dk
            sg = w["sg"]
            dq_ref[rows, :] = (dqi * w["e"]).astype(dq_ref.dtype)
            df_ref[rows, :] = (df * (1.0 - lbv) * sg * (1.0 - sg)).astype(df_ref.dtype)
            di_ref[rows, :] = dv.astype(di_ref.dtype)
            dlb = dlb + jnp.sum(df * (1.0 - sg), axis=0, keepdims=True)
            return dst_in, dlb, dhg

        zero = jnp.zeros((1, A_DK), f32)

        def step(j, carry):
            c = nreal - 1 - j
            return run(pl.ds(pl.multiple_of(nm + c * A_CHUNK, BF16_ROWS), A_CHUNK), c + 1, carry)

        carry = lax.fori_loop(0, nreal, step, (jnp.zeros((A_DK, A_DK), f32), zero, zero))
        _, dlb, dhg = run(pl.ds(0, nm), 0, carry)

        @pl.when(pl.program_id(1) == 0)
        def _():
            dlb_ref[...] = jnp.zeros_like(dlb_ref)
            dhg_ref[...] = jnp.zeros_like(dhg_ref)

        dlb_ref[...] += dlb
        dhg_ref[...] += dhg

    col = lambda o: pl.BlockSpec((t, A_DK), lambda h, b: (b, o * nh + h))
    blk = pl.BlockSpec((t, A_DK), lambda h, b: (b, h))
    vec = pl.BlockSpec((1, A_DK), lambda h, b: (0, h))
    return pl.pallas_call(
        body, grid=(nh, bl_), name=name,
        in_specs=[col(0), col(1), col(2), col(3),
                  pl.BlockSpec((1, nch, A_DK, A_DK), lambda h, b: (b * nh + h, 0, 0, 0)), blk, vec, vec],
        out_specs=[blk, blk, blk, blk, vec, vec],
        out_shape=[SDS((n, d), bf16)] * 4 + [SDS((1, d), f32)] * 2, compiler_params=_params(),
    )(pmat, pmat, pmat, pmat, ss, dog, lb, hg)


def _shift_down(x, prev2, s):
    row = _iota2(x.shape, 0)
    y = pltpu.roll(x, s, 0)
    if s == 1:
        return jnp.where(row == 0, prev2[1:2, :], y)
    return jnp.where(row == 0, prev2[0:1, :], jnp.where(row == 1, prev2[1:2, :], y))


def _shift_up(x, next2, s):
    n = x.shape[0]
    row = _iota2(x.shape, 0)
    y = pltpu.roll(x, n - s, 0)
    if s == 1:
        return jnp.where(row == n - 1, next2[0:1, :], y)
    return jnp.where(row == n - 1, next2[1:2, :], jnp.where(row == n - 2, next2[0:1, :], y))


def _conv3(x, prev2, w):
    return w[0:1, :] * _shift_down(x, prev2, 2) + w[1:2, :] * _shift_down(x, prev2, 1) + w[2:3, :] * x


def _conv_gate_fwd(ug, uv, cwg, cwv, bl_, t, tc, name):
    n, ff = ug.shape
    nt = t // tc

    def body(ug_ref, uv_ref, wg_ref, wv_ref, a_ref, hg_ref, hv_ref):
        @pl.when(pl.program_id(1) == 0)
        def _():
            hg_ref[...] = jnp.zeros_like(hg_ref)
            hv_ref[...] = jnp.zeros_like(hv_ref)

        xg = ug_ref[...].astype(f32)
        xv = uv_ref[...].astype(f32)
        cg = _conv3(xg, hg_ref[...], wg_ref[...])
        cv = _conv3(xv, hv_ref[...], wv_ref[...])
        a_ref[...] = (cg * _sigmoid(cg) * cv).astype(a_ref.dtype)
        hg_ref[...] = xg[tc - 2:tc, :]
        hv_ref[...] = xv[tc - 2:tc, :]

    row = pl.BlockSpec((tc, ff), lambda b, i: (b * nt + i, 0))
    wsp = pl.BlockSpec((3, ff), lambda b, i: (0, 0))
    return pl.pallas_call(
        body, grid=(bl_, nt), name=name, in_specs=[row, row, wsp, wsp], out_specs=row,
        out_shape=SDS((n, ff), bf16),
        scratch_shapes=[pltpu.VMEM((2, ff), f32), pltpu.VMEM((2, ff), f32)], compiler_params=_params(),
    )(ug, uv, cwg, cwv)


def _conv_gate_bwd(ug, uv, cwg, cwv, da, bl_, t, tc, name):
    n, ff = ug.shape
    nt = t // tc
    per = tc // BF16_ROWS

    def body(ug_ref, uv_ref, pg_ref, pv_ref, wg_ref, wv_ref, da_ref, dug_ref, duv_ref, dwg_ref, dwv_ref, ng_ref, nv_ref):
        first = jnp.logical_and(pl.program_id(0) == 0, pl.program_id(1) == 0)

        @pl.when(first)
        def _():
            dwg_ref[...] = jnp.zeros_like(dwg_ref)
            dwv_ref[...] = jnp.zeros_like(dwv_ref)

        @pl.when(pl.program_id(1) == 0)
        def _():
            ng_ref[...] = jnp.zeros_like(ng_ref)
            nv_ref[...] = jnp.zeros_like(nv_ref)

        seq_start = pl.program_id(1) == nt - 1
        dav = da_ref[...].astype(f32)

        def half(u_ref, p_ref, w_ref):
            x = u_ref[...].astype(f32)
            prev2 = jnp.where(seq_start, 0.0, p_ref[BF16_ROWS - 2:BF16_ROWS, :].astype(f32))
            x1, x2 = _shift_down(x, prev2, 1), _shift_down(x, prev2, 2)
            w = w_ref[...]
            return x, x1, x2, w[0:1, :] * x2 + w[1:2, :] * x1 + w[2:3, :] * x

        xg, xg1, xg2, cg = half(ug_ref, pg_ref, wg_ref)
        xv, xv1, xv2, cv = half(uv_ref, pv_ref, wv_ref)
        sg = _sigmoid(cg)
        dcg = dav * cv * (sg * (1.0 + cg * (1.0 - sg)))
        dcv = dav * (cg * sg)

        def back(dc, x, x1, x2, w_ref, nx_ref, du_ref, dw_ref):
            w = w_ref[...]
            nx = nx_ref[...]
            du = w[2:3, :] * dc + w[1:2, :] * _shift_up(dc, nx, 1) + w[0:1, :] * _shift_up(dc, nx, 2)
            du_ref[...] = du.astype(du_ref.dtype)
            dw_ref[0:1, :] += jnp.sum(dc * x2, axis=0, keepdims=True)
            dw_ref[1:2, :] += jnp.sum(dc * x1, axis=0, keepdims=True)
            dw_ref[2:3, :] += jnp.sum(dc * x, axis=0, keepdims=True)
            nx_ref[...] = dc[0:2, :]

        back(dcg, xg, xg1, xg2, wg_ref, ng_ref, dug_ref, dwg_ref)
        back(dcv, xv, xv1, xv2, wv_ref, nv_ref, duv_ref, dwv_ref)

    row = pl.BlockSpec((tc, ff), lambda b, i: (b * nt + nt - 1 - i, 0))
    prev = pl.BlockSpec((BF16_ROWS, ff), lambda b, i: (jnp.maximum((b * nt + nt - 1 - i) * per - 1, 0), 0))
    wsp = pl.BlockSpec((3, ff), lambda b, i: (0, 0))
    return pl.pallas_call(
        body, grid=(bl_, nt), name=name, in_specs=[row, row, prev, prev, wsp, wsp, row],
        out_specs=[row, row, wsp, wsp],
        out_shape=[SDS((n, ff), bf16), SDS((n, ff), bf16), SDS((3, ff), f32), SDS((3, ff), f32)],
        scratch_shapes=[pltpu.VMEM((2, ff), f32), pltpu.VMEM((2, ff), f32)], compiler_params=_params(),
    )(ug, uv, ug, uv, cwg, cwv, da)


def _zf_c(hk, wzf, fgb, bl_, t, tm, name):
    n, d = hk.shape
    nt = t // tm

    def body(hk_ref, w_ref, b_ref, zf_ref, c_ref, carry_ref):
        @pl.when(pl.program_id(1) == 0)
        def _():
            carry_ref[...] = jnp.zeros_like(carry_ref)

        z = _bdot(hk_ref[...], w_ref[...]) + b_ref[...]
        ls = jnp.minimum(z, 0.0) - jnp.log(1.0 + jnp.exp(-jnp.abs(z)))
        c = _cumsum_rows(ls) + carry_ref[...]
        zf_ref[...] = z
        c_ref[...] = c
        carry_ref[...] = c[tm - 1:tm, :]

    row = lambda w: pl.BlockSpec((tm, w), lambda b, i: (b * nt + i, 0))
    return pl.pallas_call(
        body, grid=(bl_, nt), name=name,
        in_specs=[row(d), pl.BlockSpec((d, LANES), lambda b, i: (0, 0)), pl.BlockSpec((1, LANES), lambda b, i: (0, 0))],
        out_specs=[row(LANES), row(LANES)],
        out_shape=[SDS((n, LANES), f32), SDS((n, LANES), f32)],
        scratch_shapes=[pltpu.VMEM((1, LANES), f32)], compiler_params=_params(),
    )(hk, wzf, fgb)


def _c_bwd(dc, zf, bl_, t, tm, name):
    n = dc.shape[0]
    nt = t // tm

    def body(dc_ref, zf_ref, dzf_ref, dfg_ref, carry_ref):
        @pl.when(jnp.logical_and(pl.program_id(0) == 0, pl.program_id(1) == 0))
        def _():
            dfg_ref[...] = jnp.zeros_like(dfg_ref)

        @pl.when(pl.program_id(1) == 0)
        def _():
            carry_ref[...] = jnp.zeros_like(carry_ref)

        rc = _revcumsum_rows(dc_ref[...]) + carry_ref[...]
        dz = rc * _sigmoid(-zf_ref[...])
        dzf_ref[...] = dz.astype(dzf_ref.dtype)
        dfg_ref[...] += jnp.sum(dz, axis=0, keepdims=True)
        carry_ref[...] = rc[0:1, :]

    row = pl.BlockSpec((tm, LANES), lambda b, i: (b * nt + nt - 1 - i, 0))
    vec = pl.BlockSpec((1, LANES), lambda b, i: (0, 0))
    return pl.pallas_call(
        body, grid=(bl_, nt), name=name, in_specs=[row, row], out_specs=[row, vec],
        out_shape=[SDS((n, LANES), bf16), SDS((1, LANES), f32)],
        scratch_shapes=[pltpu.VMEM((1, LANES), f32)], compiler_params=_params(),
    )(dc, zf)


def _attn_fwd(q, kv, cq, ck, bl_, t, tq, hd, name):
    n, d = q.shape
    npair = d // LANES
    hp = LANES // hd
    nq = t // tq
    scale = 1.0 / (hd ** 0.5)

    def body(q_ref, k_ref, v_ref, cq_ref, ck_ref, o_ref, lse_ref):
        i = pl.program_id(2)
        rowg = i * tq + _iota2((tq, tq), 0)
        for hh in range(hp):
            lanes = slice(hh * hd, (hh + 1) * hd)
            qh = q_ref[:, lanes]
            ct = cq_ref[0, :, hh:hh + 1]

            def step(j, carry, lanes=lanes, qh=qh, ct=ct, hh=hh):
                m, l, acc = carry
                rows = pl.ds(pl.multiple_of(j * tq, BF16_ROWS), tq)
                s = _bdot_nt(qh, k_ref[rows, lanes]) * scale + (ct - ck_ref[0, 0, j, hh:hh + 1, :])
                s = jnp.where(rowg >= j * tq + _iota2((tq, tq), 1), s, -1e30)
                m2 = jnp.maximum(m, jnp.max(s, axis=-1, keepdims=True))
                p = jnp.exp(s - m2)
                a = jnp.exp(m - m2)
                return m2, a * l + jnp.sum(p, axis=-1, keepdims=True), a * acc + _bdot(p, v_ref[rows, lanes])

            m, l, acc = lax.fori_loop(
                0, i + 1, step, (jnp.full((tq, 1), -1e30, f32), jnp.zeros((tq, 1), f32), jnp.zeros((tq, hd), f32)))
            o_ref[:, lanes] = (acc / l).astype(o_ref.dtype)
            lse_ref[:, lanes] = jnp.broadcast_to(m + jnp.log(l), (tq, hd))

    nk = nq
    return pl.pallas_call(
        body, grid=(bl_, npair, nq), name=name,
        in_specs=[pl.BlockSpec((tq, LANES), lambda b, p, i: (b * nq + i, p)),
                  pl.BlockSpec((t, LANES), lambda b, p, i: (b, p)),
                  pl.BlockSpec((t, LANES), lambda b, p, i: (b, npair + p)),
                  pl.BlockSpec((1, tq, hp), lambda b, p, i: (p, b * nq + i, 0)),
                  pl.BlockSpec((1, 1, nk, hp, tq), lambda b, p, i: (b, p, 0, 0, 0))],
        out_specs=[pl.BlockSpec((tq, LANES), lambda b, p, i: (b * nq + i, p)),
                   pl.BlockSpec((tq, LANES), lambda b, p, i: (b * nq + i, p))],
        out_shape=[SDS((n, d), f32), SDS((n, d), f32)], compiler_params=_params(),
    )(q, kv, kv, cq, ck)


def _attn_bwd(q, kv, o, do, lse, cq, ck, bl_, t, tq, hd, name):
    n, d = q.shape
    npair = d // LANES
    hp = LANES // hd
    nq = t // tq
    scale = 1.0 / (hd ** 0.5)

    def body(q_ref, k_ref, v_ref, o_ref, do_ref, lse_ref, cq_ref, ck_ref, dq_ref, dk_ref, dv_ref, dck_ref, dcq_ref):
        j = pl.program_id(2)

        @pl.when(j == 0)
        def _():
            dq_ref[...] = jnp.zeros_like(dq_ref)
            dcq_ref[...] = jnp.zeros_like(dcq_ref)

        colg = j * tq + _iota2((tq, tq), 1)
        for hh in range(hp):
            lanes = slice(hh * hd, (hh + 1) * hd)
            kh = k_ref[:, lanes]
            vh = v_ref[:, lanes]
            cs = ck_ref[0, 0, 0, hh:hh + 1, :]

            def step(i, carry, lanes=lanes, kh=kh, vh=vh, cs=cs, hh=hh):
                dk, dv, dcs = carry
                rows = pl.ds(pl.multiple_of(i * tq, BF16_ROWS), tq)
                qh = q_ref[rows, lanes]
                doh = do_ref[rows, lanes]
                s = _bdot_nt(qh, kh) * scale + (cq_ref[0, rows, hh:hh + 1] - cs)
                s = jnp.where(i * tq + _iota2((tq, tq), 0) >= colg, s, -1e30)
                p = jnp.exp(s - lse_ref[rows, hh * hd:hh * hd + 1])
                delta = jnp.sum(doh.astype(f32) * o_ref[rows, lanes].astype(f32), axis=-1, keepdims=True)
                ds = p * (_bdot_nt(doh, vh) - delta)
                dq_ref[rows, lanes] += _bdot(ds, kh) * scale
                dcq_ref[0, rows, hh:hh + 1] += jnp.sum(ds, axis=-1, keepdims=True)
                return (dk + _bdot_tn(ds, qh) * scale, dv + _bdot_tn(p, doh), dcs - jnp.sum(ds, axis=0, keepdims=True))

            dk, dv, dcs = lax.fori_loop(
                j, nq, step, (jnp.zeros((tq, hd), f32), jnp.zeros((tq, hd), f32), jnp.zeros((1, tq), f32)))
            dk_ref[:, lanes] = dk.astype(dk_ref.dtype)
            dv_ref[:, lanes] = dv.astype(dv_ref.dtype)
            dck_ref[0, 0, 0, hh:hh + 1, :] = dcs

    whole = lambda c0: pl.BlockSpec((t, LANES), lambda b, p, j: (b, c0 + p))
    tile = lambda c0: pl.BlockSpec((tq, LANES), lambda b, p, j: (b * nq + j, c0 + p))
    ckspec = pl.BlockSpec((1, 1, 1, hp, tq), lambda b, p, j: (b, p, j, 0, 0))
    cqspec = pl.BlockSpec((1, t, hp), lambda b, p, j: (p, b, 0))
    return pl.pallas_call(
        body, grid=(bl_, npair, nq), name=name,
        in_specs=[whole(0), tile(0), tile(npair), whole(0), whole(0), whole(0), cqspec, ckspec],
        out_specs=[whole(0), tile(0), tile(0), ckspec, cqspec],
        out_shape=[SDS((n, d), f32), SDS((n, d), bf16), SDS((n, d), bf16), SDS((bl_, npair, nq, hp, tq), f32),
                   SDS((npair, n, hp), f32)],
        compiler_params=_params(),
    )(q, kv, kv, o, do, lse, cq, ck)


def _loss_head(h, target, t, nm, tm, name):
    n, d = h.shape
    nt = t // tm

    def body(h_ref, t_ref, loss_ref, dh_ref):
        i = pl.program_id(0)

        @pl.when(i == 0)
        def _():
            loss_ref[...] = jnp.zeros_like(loss_ref)

        pos = (i % nt) * tm + _iota2((tm, d), 0)
        err = jnp.where(pos >= nm, h_ref[...] - t_ref[...], 0.0)
        dh_ref[...] = err * (1.0 / d)
        loss_ref[...] += 0.5 * jnp.sum(jnp.mean(err * err, axis=-1, keepdims=True))

    row = pl.BlockSpec((tm, d), lambda i: (i, 0))
    return pl.pallas_call(
        body, grid=(n // tm,), name=name, in_specs=[row, row],
        out_specs=[pl.BlockSpec((8, LANES), lambda i: (0, 0)), row],
        out_shape=[SDS((8, LANES), f32), SDS((n, d), f32)], compiler_params=_params(),
    )(h, target)


def _c_layouts(c, bl_, t, tq, bh, hp):
    n = c.shape[0]
    npair = bh // hp
    nk = t // tq
    cc = c[:, :bh]
    cq = cc.reshape(n, npair, hp).transpose(1, 0, 2)
    ck = cc.reshape(bl_, nk, tq, npair, hp).transpose(0, 3, 1, 4, 2)
    return cq, ck


def _dc_rows(dck, dcq, bl_, t, bh):
    d = dck.transpose(0, 2, 4, 1, 3).reshape(bl_ * t, bh) + dcq.transpose(1, 0, 2).reshape(bl_ * t, bh)
    return jnp.pad(d, ((0, 0), (0, LANES - bh)))


_ANY = pl.BlockSpec(memory_space=pl.ANY)


def _all_gather(xs, name):
    na = len(xs)

    def body(*refs):
        x_refs, out_refs = refs[:na], refs[na:2 * na]
        send_sems, recv_sems, local_sems = refs[2 * na:]
        mx, my, mc = lax.axis_index("x"), lax.axis_index("y"), lax.axis_index("c")
        me, sibling = (mx, my, mc), (mx, my, 1 - mc)
        chips = [(1 - mx, my), (mx, 1 - my), (1 - mx, 1 - my)]

        def copy(a, k, block, to, own=False):
            px, py, pc = block
            rows = out_refs[a].at[4 * px + 2 * py + pc]
            return pltpu.make_async_remote_copy(
                src_ref=x_refs[a] if own else rows, dst_ref=rows,
                send_sem=send_sems.at[a, k], recv_sem=recv_sems.at[a, k], device_id=to, device_id_type=MESH)

        arrays = range(na)
        mine = [pltpu.make_async_copy(x_refs[a], out_refs[a].at[4 * mx + 2 * my + mc], local_sems.at[a]) for a in arrays]
        for cp in mine:
            cp.start()
        first = [copy(a, 1 + j, me, (*chip, mc), own=True) for j, chip in enumerate(chips) for a in arrays]
        first += [copy(a, 0, me, sibling, own=True) for a in arrays]
        for cp in first:
            cp.start()
        passed = []
        for j, chip in enumerate(chips):
            for a in arrays:
                copy(a, 1 + j, (*chip, mc), me).wait_recv()
                cp = copy(a, 4 + j, (*chip, mc), sibling)
                cp.start()
                passed.append(cp)
        for a in arrays:
            copy(a, 0, sibling, me).wait_recv()
        for j, chip in enumerate(chips):
            for a in arrays:
                copy(a, 4 + j, (*chip, 1 - mc), me).wait_recv()
        for cp in first + passed:
            cp.wait_send()
        for cp in mine:
            cp.wait()

    return pl.pallas_call(
        body, name=name, out_shape=[SDS((N_DEV,) + x.shape, x.dtype) for x in xs],
        in_specs=[_ANY] * na, out_specs=[_ANY] * na,
        scratch_shapes=[pltpu.SemaphoreType.DMA((na, 7)), pltpu.SemaphoreType.DMA((na, 7)), pltpu.SemaphoreType.DMA((na,))],
    )(*xs)


def _exchange(bufs, group, name):
    na = len(bufs)
    g = 2 if group == "c" else 4
    assert all(b.shape[0] == g for b in bufs)
    keep_own = g > 2

    def body(*refs):
        buf_refs, out_refs = refs[:na], refs[na:2 * na]
        send_sems, recv_sems, local_sems = refs[2 * na:]
        mx, my, mc = lax.axis_index("x"), lax.axis_index("y"), lax.axis_index("c")
        if group == "c":
            me = mc
            peers = [((mx, my, 1 - mc), 1 - mc)]
        else:
            me = 2 * mx + my
            peers = []
            for r in range(1, 4):
                px = 1 - mx if r & 2 else mx
                py = 1 - my if r & 1 else my
                peers.append(((px, py, mc), 2 * px + py))

        def copy(a, k, dev, src_row, dst_row):
            return pltpu.make_async_remote_copy(
                src_ref=buf_refs[a].at[src_row], dst_ref=out_refs[a].at[dst_row] if keep_own else out_refs[a],
                send_sem=send_sems.at[a, k], recv_sem=recv_sems.at[a, k], device_id=dev, device_id_type=MESH)

        sends = [copy(a, k, dev, idx, me) for k, (dev, idx) in enumerate(peers) for a in range(na)]
        for cp in sends:
            cp.start()
        own = []
        if keep_own:
            own = [pltpu.make_async_copy(buf_refs[a].at[me], out_refs[a].at[me], local_sems.at[a]) for a in range(na)]
        for cp in own:
            cp.start()
        for k, (dev, idx) in enumerate(peers):
            for a in range(na):
                copy(a, k, dev, idx, idx).wait_recv()
        for cp in sends:
            cp.wait_send()
        for cp in own:
            cp.wait()

    return pl.pallas_call(
        body, name=name, out_shape=[SDS(b.shape if keep_own else b.shape[1:], b.dtype) for b in bufs],
        in_specs=[_ANY] * na, out_specs=[_ANY] * na,
        scratch_shapes=[pltpu.SemaphoreType.DMA((na, g - 1)), pltpu.SemaphoreType.DMA((na, g - 1)),
                        pltpu.SemaphoreType.DMA((na,))],
    )(*bufs)


def _sum_own_recv(buf, recv, core, tr, name):
    _, r, c = buf.shape

    def body(core_ref, own_ref, recv_ref, o_ref):
        o_ref[...] = (own_ref[0].astype(f32) + recv_ref[...].astype(f32)).astype(o_ref.dtype)

    row = pl.BlockSpec((tr, c), lambda i, core_ref: (i, 0))
    return pl.pallas_call(
        body, name=name, out_shape=SDS((r, c), buf.dtype), compiler_params=_params(),
        grid_spec=pltpu.PrefetchScalarGridSpec(
            num_scalar_prefetch=1, grid=(r // tr,),
            in_specs=[pl.BlockSpec((1, tr, c), lambda i, core_ref: (core_ref[0], i, 0)), row], out_specs=row),
    )(core, buf, recv)


def _sum_rows(parts, tr, name):
    g, r, c = parts.shape

    def body(p_ref, o_ref):
        acc = p_ref[0].astype(f32)
        for k in range(1, g):
            acc = acc + p_ref[k].astype(f32)
        o_ref[...] = acc.astype(o_ref.dtype)

    return pl.pallas_call(
        body, grid=(r // tr,), name=name, in_specs=[pl.BlockSpec((g, tr, c), lambda i: (0, i, 0))],
        out_specs=pl.BlockSpec((tr, c), lambda i: (i, 0)), out_shape=SDS((r, c), parts.dtype), compiler_params=_params(),
    )(parts)


def _adamw(parts, w, m, v, tr, name):
    g, r, c = parts.shape

    def body(p_ref, w_ref, m_ref, v_ref, g_ref, d_ref, m2_ref, v2_ref):
        gr = p_ref[0].astype(f32)
        for k in range(1, g):
            gr = gr + p_ref[k].astype(f32)
        m2 = ADAM_B1 * m_ref[...] + (1.0 - ADAM_B1) * gr
        v2 = ADAM_B2 * v_ref[...] + (1.0 - ADAM_B2) * (gr * gr)
        m_hat = m2 / (1.0 - ADAM_B1 ** ADAM_STEP)
        v_hat = v2 / (1.0 - ADAM_B2 ** ADAM_STEP)
        g_ref[...] = gr
        d_ref[...] = -ADAM_LR * (m_hat / (jnp.sqrt(v_hat) + ADAM_EPS) + ADAM_WD * w_ref[...])
        m2_ref[...] = m2
        v2_ref[...] = v2

    row = pl.BlockSpec((tr, c), lambda i: (i, 0))
    return pl.pallas_call(
        body, grid=(r // tr,), name=name, in_specs=[pl.BlockSpec((g, tr, c), lambda i: (0, i, 0)), row, row, row],
        out_specs=[row] * 4, out_shape=[SDS((r, c), f32)] * 4, compiler_params=_params(),
    )(parts, w, m, v)


_SHARD_AXIS = dict(meta_tokens=1, norm_gains=2, a_w_in=2, a_lb_logits=1, a_head_norm=1, a_w_out=1, kv_w=1,
                   b_w_q=1, b_w_out=1, ffn_w_up=2, ffn_conv=2, ffn_w_down=1)
_MATRICES = ("a_w_in", "a_w_out", "kv_w", "b_w_q", "b_w_out", "ffn_w_up", "ffn_w_down")
_VECTORS = ("meta_tokens", "norm_gains", "a_lb_logits", "a_head_norm", "ffn_conv")
_SHARDED = tuple(_SHARD_AXIS)
_REPLICATED = ("kv_norm", "fg_b")
_ROW_TILE_CAP = 512


def _pack(arrs, dtype, cols, row_mult):
    lead = arrs[0].shape[:-1] if arrs[0].ndim > 1 else ()
    flat = jnp.concatenate([a.astype(dtype) for a in arrs], axis=-1)
    size = flat.shape[-1]
    per = cols * row_mult
    total = -(-size // per) * per
    flat = jnp.pad(flat, [(0, 0)] * len(lead) + [(0, total - size)])
    return flat.reshape(lead + (total // cols, cols))


def _unpack(flat, shapes):
    out, off = [], 0
    lead = flat.shape[:-1]
    for shp in shapes:
        size = 1
        for s in shp:
            size *= s
        out.append(flat[..., off:off + size].reshape(lead + tuple(shp)))
        off += size
    return out


def _unshard(seg, axis):
    a = jnp.moveaxis(seg, 0, axis)
    shp = a.shape
    return a.reshape(shp[:axis] + (shp[axis] * shp[axis + 1],) + shp[axis + 2:])


def _shard8(full, axis):
    shp = full.shape
    a = full.reshape(shp[:axis] + (N_DEV, shp[axis] // N_DEV) + shp[axis + 1:])
    return jnp.moveaxis(a, axis, 0)


def _rows(a, lead=0):
    return a.reshape(a.shape[:lead] + (-1, a.shape[-1]))


def _core_major(a):
    _, r, c = a.shape
    return a.reshape(4, 2, r, c).transpose(1, 0, 2, 3).reshape(2, 4 * r, c)


def kernel(x, meta_tokens, norm_gains, a_w_in, a_lb_logits, a_head_norm, a_w_out, kv_norm, kv_w, fg_b, b_w_q, b_w_out, ffn_w_up, ffn_conv, ffn_w_down, loss_target, m_meta_tokens, m_norm_gains, m_a_w_in, m_a_lb_logits, m_a_head_norm, m_a_w_out, m_kv_norm, m_kv_w, m_fg_b, m_b_w_q, m_b_w_out, m_ffn_w_up, m_ffn_conv, m_ffn_w_down, v_meta_tokens, v_norm_gains, v_a_w_in, v_a_lb_logits, v_a_head_norm, v_a_w_out, v_kv_norm, v_kv_w, v_fg_b, v_b_w_q, v_b_w_out, v_ffn_w_up, v_ffn_conv, v_ffn_w_down):
    names = ("meta_tokens", "norm_gains", "a_w_in", "a_lb_logits", "a_head_norm", "a_w_out", "kv_norm", "kv_w", "fg_b",
             "b_w_q", "b_w_out", "ffn_w_up", "ffn_conv", "ffn_w_down")
    w = dict(zip(names, (meta_tokens, norm_gains, a_w_in, a_lb_logits, a_head_norm, a_w_out, kv_norm, kv_w, fg_b,
                         b_w_q, b_w_out, ffn_w_up, ffn_conv, ffn_w_down)))
    mom = dict(zip(names, (m_meta_tokens, m_norm_gains, m_a_w_in, m_a_lb_logits, m_a_head_norm, m_a_w_out, m_kv_norm,
                           m_kv_w, m_fg_b, m_b_w_q, m_b_w_out, m_ffn_w_up, m_ffn_conv, m_ffn_w_down)))
    var = dict(zip(names, (v_meta_tokens, v_norm_gains, v_a_w_in, v_a_lb_logits, v_a_head_norm, v_a_w_out, v_kv_norm,
                           v_kv_w, v_fg_b, v_b_w_q, v_b_w_out, v_ffn_w_up, v_ffn_conv, v_ffn_w_down)))

    bl_, seq, d = x.shape
    nm = meta_tokens.shape[0]
    t = nm + seq
    n = bl_ * t
    bh = fg_b.shape[0]
    hd = d // bh
    hp = LANES // hd
    ff = ffn_w_down.shape[1] * N_DEV
    tm = _div_tile(t, 1024)
    tc = _div_tile(t, 64)
    tn = 512

    vec_pack = _pack([w[k].reshape(-1) for k in _VECTORS], f32, LANES, 8)
    got = _all_gather([w[k].astype(bf16) for k in _MATRICES] + [vec_pack], "gather_weights")
    big = {k: _unshard(a, _SHARD_AXIS[k]) for k, a in zip(_MATRICES, got)}
    vec_segs = _unpack(got[-1].reshape(N_DEV, -1), [w[k].shape for k in _VECTORS])
    small = {k: _unshard(a, _SHARD_AXIS[k]) for k, a in zip(_VECTORS, vec_segs)}
    w_in, w_out_a = big["a_w_in"][0], big["a_w_out"][0]
    w_kv, w_zf = big["kv_w"][:, :2 * d], jnp.pad(big["kv_w"][:, 2 * d:], ((0, 0), (0, LANES - bh)))
    w_q, w_out_b = big["b_w_q"][0], big["b_w_out"][0]
    w_gate, w_val, w_down = big["ffn_w_up"][:, :, :ff], big["ffn_w_up"][:, :, ff:], big["ffn_w_down"]
    gains = small["norm_gains"]
    gain = lambda l, j: gains[l, j][None]
    cw_gate, cw_val = small["ffn_conv"][:, :, :ff], small["ffn_conv"][:, :, ff:]
    head_gain = small["a_head_norm"]
    lb = jax.nn.softmax(small["a_lb_logits"], axis=0)[0:1]
    kvn = kv_norm[None]
    fgb_pad = jnp.pad(fg_b, (0, LANES - bh))[None]

    h0 = jnp.concatenate([jnp.broadcast_to(small["meta_tokens"][None], (bl_, nm, d)), x], axis=1).reshape(n, d)

    def ffn_fwd(l, h_in):
        fi = _rms_fwd(h_in, gain(l, 2), tm, f"ffn{l}_norm")
        ug = _mm(fi, w_gate[l], bf16, tm, ff, f"ffn{l}_up_gate")
        uv = _mm(fi, w_val[l], bf16, tm, ff, f"ffn{l}_up_val")
        act = _conv_gate_fwd(ug, uv, cw_gate[l], cw_val[l], bl_, t, tc, f"ffn{l}_conv_gate")
        h_out, mix = _mm_norm_res(act, w_down[l], gain(l, 3), h_in, tm, f"ffn{l}_down")
        return h_out, (h_in, fi, ug, uv, act, mix)

    hn0 = _rms_fwd(h0, gain(0, 0), tm, "a_norm")
    pmat = _mm(hn0, w_in, f32, tm, tn, "a_in_proj")
    og, states = _gla_fwd(pmat, lb, head_gain, bl_, t, nm, "a_gla_fwd")
    h1, mix_a = _mm_norm_res(og, w_out_a, gain(0, 1), h0, tm, "a_out_proj")
    h2, ffn0 = ffn_fwd(0, h1)

    hk = _rms_fwd(h2, kvn, tm, "kv_norm")
    kvp = _mm(hk, w_kv, bf16, tm, tn, "kv_proj")
    zf, cum = _zf_c(hk, w_zf, fgb_pad, bl_, t, tm, "forget_cumsum")
    cq, ck = _c_layouts(cum, bl_, t, tm, bh, hp)
    hn1 = _rms_fwd(h2, gain(1, 0), tm, "b_norm")
    q = _mm(hn1, w_q, bf16, tm, tn, "b_q_proj")
    o, lse = _attn_fwd(q, kvp, cq, ck, bl_, t, tm, hd, "b_attn_fwd")
    h3, mix_b = _mm_norm_res(o, w_out_b, gain(1, 1), h2, tm, "b_out_proj")
    h4, ffn1 = ffn_fwd(1, h3)

    target = jnp.concatenate([jnp.zeros((bl_, nm, d), f32), loss_target], axis=1).reshape(n, d)
    loss8, dh = _loss_head(h4, target, t, nm, tm, "loss_head")
    loss = lax.psum(loss8[0, 0], ("x", "y", "c"))

    dgain = {}

    def ffn_bwd(l, saved, dh_out):
        h_in, fi, ug, uv, act, mix = saved
        dmix, dgain[l, 3] = _rms_bwd(mix, gain(l, 3), dh_out, None, bf16, tm, f"ffn{l}_down_norm_bwd")
        dact = _mm_nt([(dmix, w_down[l])], bf16, tm, ff, f"ffn{l}_down_dx")
        dw_down = _mm_tn(act, dmix, tm, ff, tn, f"ffn{l}_down_dw")
        dug, duv, dcg, dcv = _conv_gate_bwd(ug, uv, cw_gate[l], cw_val[l], dact, bl_, t, tc, f"ffn{l}_conv_gate_bwd")
        dfi = _mm_nt([(dug, w_gate[l]), (duv, w_val[l])], bf16, tm, 256, f"ffn{l}_up_dx")
        dw_up = jnp.concatenate([_mm_tn(fi, dug, tm, tn, ff, f"ffn{l}_up_gate_dw"),
                                 _mm_tn(fi, duv, tm, tn, ff, f"ffn{l}_up_val_dw")], axis=1)
        dh_in, dgain[l, 2] = _rms_bwd(h_in, gain(l, 2), dfi, dh_out, f32, tm, f"ffn{l}_norm_bwd")
        return dh_in, dw_up, jnp.concatenate([dcg, dcv], axis=1), dw_down

    dh, dw_up1, dconv1, dw_down1 = ffn_bwd(1, ffn1, dh)

    dmix, dgain[1, 1] = _rms_bwd(mix_b, gain(1, 1), dh, None, bf16, tm, "b_out_norm_bwd")
    do = _mm_nt([(dmix, w_out_b)], bf16, tm, tn, "b_out_dx")
    dw_out_b = _mm_tn(o, dmix, tm, tn, tn, "b_out_dw")
    dq, dk, dv, dck, dcq = _attn_bwd(q, kvp, o, do, lse, cq, ck, bl_, t, tm, hd, "b_attn_bwd")
    dhn1 = _mm_nt([(dq, w_q)], bf16, tm, tn, "b_q_dx")
    dw_q = _mm_tn(hn1, dq, tm, tn, tn, "b_q_dw")
    dh, dgain[1, 0] = _rms_bwd(h2, gain(1, 0), dhn1, dh, f32, tm, "b_norm_bwd")

    dzf, dfgb = _c_bwd(_dc_rows(dck, dcq, bl_, t, bh), zf, bl_, t, tm, "forget_cumsum_bwd")
    dhk = _mm_nt([(dk, w_kv[:, :d]), (dv, w_kv[:, d:]), (dzf, w_zf)], bf16, tm, tn, "kv_dx")
    dw_kv = jnp.concatenate([_mm_tn(hk, dk, tm, tn, tn, "k_dw"), _mm_tn(hk, dv, tm, tn, tn, "v_dw"),
                             _mm_tn(hk, dzf, tm, tn, LANES, "zf_dw")[:, :bh]], axis=1)
    dh, dkvn = _rms_bwd(h2, kvn, dhk, dh, f32, tm, "kv_norm_bwd")

    dh, dw_up0, dconv0, dw_down0 = ffn_bwd(0, ffn0, dh)

    dmix, dgain[0, 1] = _rms_bwd(mix_a, gain(0, 1), dh, None, bf16, tm, "a_out_norm_bwd")
    dog = _mm_nt([(dmix, w_out_a)], bf16, tm, tn, "a_out_dx")
    dw_out_a = _mm_tn(og, dmix, tm, tn, tn, "a_out_dw")
    dpq, dpf, dpi, dpg, dlb, dhg = _gla_bwd(pmat, states, dog, lb, head_gain, bl_, t, nm, "a_gla_bwd")
    dps = (dpq, dpf, dpi, dpg)
    dhn0 = _mm_nt([(dp, w_in[:, j * d:(j + 1) * d]) for j, dp in enumerate(dps)], bf16, tm, tn, "a_in_dx")
    dw_in = jnp.concatenate([_mm_tn(hn0, dp, tm, tn, tn, f"a_in_dw{j}") for j, dp in enumerate(dps)], axis=1)
    dh, dgain[0, 0] = _rms_bwd(h0, gain(0, 0), dhn0, dh, f32, tm, "a_norm_bwd")

    dh = dh.reshape(bl_, t, d)
    grad_x = dh[:, nm:]
    dl0 = dlb * lb * (1.0 - lb)
    grads = dict(
        meta_tokens=jnp.sum(dh[:, :nm], axis=0),
        norm_gains=jnp.stack([jnp.concatenate([dgain[l, j] for j in range(4)], axis=0) for l in range(2)]),
        a_w_in=dw_in[None], a_lb_logits=jnp.concatenate([dl0, -dl0], axis=0), a_head_norm=dhg, a_w_out=dw_out_a[None],
        kv_w=dw_kv, b_w_q=dw_q[None], b_w_out=dw_out_b[None],
        ffn_w_up=jnp.stack([dw_up0, dw_up1]), ffn_conv=jnp.stack([dconv0, dconv1]),
        ffn_w_down=jnp.stack([dw_down0, dw_down1]))

    send = [_rows(_shard8(grads[k], _SHARD_AXIS[k]), 1).astype(bf16) for k in _MATRICES]
    send.append(_pack([_shard8(grads[k], _SHARD_AXIS[k]).reshape(N_DEV, -1) for k in _VECTORS], bf16, LANES, BF16_ROWS))
    send = [_core_major(a) for a in send]
    recv = _exchange(send, "c", "grad_exchange_cores")
    core = lax.axis_index("c").astype(jnp.int32).reshape(1)
    chip_sum = [_sum_own_recv(a, b, core, _div_tile(b.shape[0], _ROW_TILE_CAP), f"grad_sum_cores{i}")
                for i, (a, b) in enumerate(zip(send, recv))]
    parts = _exchange([a.reshape((4, -1) + a.shape[1:]) for a in chip_sum], "xy", "grad_exchange_chips")
    g_s, d_s, m_s, v_s = {}, {}, {}, {}

    def update(part, srcs, label):
        return _adamw(part, *srcs, _div_tile(part.shape[1], _ROW_TILE_CAP), label)

    for k, part in zip(_MATRICES, parts):
        res = update(part, [_rows(src[k]) for src in (w, mom, var)], f"adamw_{k}")
        g_s[k], d_s[k], m_s[k], v_s[k] = (r.reshape(w[k].shape) for r in res)
    vec_packs = [_pack([src[k].reshape(-1) for k in _VECTORS], f32, LANES, BF16_ROWS) for src in (w, mom, var)]
    vec_shapes = [w[k].shape for k in _VECTORS]
    for dst, r in zip((g_s, d_s, m_s, v_s), update(parts[-1], vec_packs, "adamw_vectors")):
        dst.update(zip(_VECTORS, _unpack(r.reshape(-1), vec_shapes)))

    rep_local = _pack([dkvn.reshape(-1), dfgb[0, :bh]], f32, LANES, 8)
    rep_parts = _all_gather([rep_local], "gather_replicated_grads")[0]
    rpacks = [_pack([src[k].reshape(-1) for k in _REPLICATED], f32, LANES, 8) for src in (w, mom, var)]
    rres = _adamw(rep_parts, *rpacks, rep_local.shape[0], "adamw_replicated")
    rshapes = [w[k].shape for k in _REPLICATED]
    g_r, d_r, m_r, v_r = ({k: a for k, a in zip(_REPLICATED, _unpack(r.reshape(-1), rshapes))} for r in rres)

    out = [loss, grad_x]
    for sh, rp in ((g_s, g_r), (d_s, d_r), (m_s, m_r), (v_s, v_r)):
        out += [sh[k] if k in sh else rp[k] for k in names]
    return tuple(out)
```

```python
import functools
import math

import jax
import jax.numpy as jnp
from jax import lax
from jax.experimental import pallas as pl
from jax.experimental.pallas import tpu as pltpu

f32 = jnp.float32
bf16 = jnp.bfloat16
SDS = jax.ShapeDtypeStruct

EPS = 1e-6
A_DK = 128
A_CHUNK = 64
GLA_GROUP = 4
TOKEN_TILE_CAP = 1024
LANES = 128
BF16_ROWS = 16
VMEM_LIMIT = 56 * 1024 * 1024
ADAM_LR, ADAM_B1, ADAM_B2, ADAM_EPS, ADAM_WD, ADAM_STEP = 0.001, 0.9, 0.999, 1e-08, 0.01, 10
N_DEV = 8
MESH = pl.DeviceIdType.MESH

_NT = (((1,), (1,)), ((), ()))
_TN = (((0,), (0,)), ((), ()))
_HI = lax.Precision.HIGHEST


def _params(**kw):
    return pltpu.CompilerParams(vmem_limit_bytes=VMEM_LIMIT, **kw)


def _div_tile(n, cap, mult=BF16_ROWS):
    best = None
    for t in range(mult, min(n, cap) + 1, mult):
        if n % t == 0:
            best = t
    assert best is not None, (n, cap, mult)
    return best


def _bdot(a, b):
    return jnp.dot(a.astype(bf16), b.astype(bf16), preferred_element_type=f32)


def _bdot_nt(a, b):
    return lax.dot_general(a.astype(bf16), b.astype(bf16), _NT, preferred_element_type=f32)


def _bdot_tn(a, b):
    return lax.dot_general(a.astype(bf16), b.astype(bf16), _TN, preferred_element_type=f32)


def _iota2(shape, axis):
    return lax.broadcasted_iota(jnp.int32, shape, axis)


def _cumsum_rows(x):
    n = x.shape[0]
    tri = (_iota2((n, n), 0) >= _iota2((n, n), 1)).astype(f32)
    return jnp.dot(tri, x, precision=_HI, preferred_element_type=f32)


def _revcumsum_rows(x):
    n = x.shape[0]
    tri = (_iota2((n, n), 1) >= _iota2((n, n), 0)).astype(f32)
    return jnp.dot(tri, x, precision=_HI, preferred_element_type=f32)


def _sigmoid(x):
    return 1.0 / (1.0 + jnp.exp(-x))


def _rms_fwd(x, g, tm, name):
    n, d = x.shape

    def body(x_ref, g_ref, o_ref):
        xv = x_ref[...]
        r = lax.rsqrt(jnp.mean(xv * xv, axis=-1, keepdims=True) + EPS)
        o_ref[...] = (xv * r * g_ref[...]).astype(o_ref.dtype)

    return pl.pallas_call(
        body, grid=(n // tm,), name=name,
        in_specs=[pl.BlockSpec((tm, d), lambda i: (i, 0)), pl.BlockSpec((1, d), lambda i: (0, 0))],
        out_specs=pl.BlockSpec((tm, d), lambda i: (i, 0)),
        out_shape=SDS((n, d), bf16), compiler_params=_params(),
    )(x, g)


def _mm(a, w, out_dtype, tm, tn, name):
    n, k = a.shape
    m = w.shape[1]

    def body(a_ref, w_ref, o_ref):
        o_ref[...] = _bdot(a_ref[...], w_ref[...]).astype(o_ref.dtype)

    return pl.pallas_call(
        body, grid=(m // tn, n // tm), name=name,
        in_specs=[pl.BlockSpec((tm, k), lambda j, i: (i, 0)), pl.BlockSpec((k, tn), lambda j, i: (0, j))],
        out_specs=pl.BlockSpec((tm, tn), lambda j, i: (i, j)),
        out_shape=SDS((n, m), out_dtype), compiler_params=_params(),
    )(a, w)


def _mm_norm_res(a, w, g, h, tm, name):
    n, k = a.shape
    d = w.shape[1]

    def body(a_ref, w_ref, g_ref, h_ref, hn_ref, mix_ref):
        mix = _bdot(a_ref[...], w_ref[...])
        r = lax.rsqrt(jnp.mean(mix * mix, axis=-1, keepdims=True) + EPS)
        mix_ref[...] = mix
        hn_ref[...] = h_ref[...] + mix * r * g_ref[...]

    return pl.pallas_call(
        body, grid=(n // tm,), name=name,
        in_specs=[pl.BlockSpec((tm, k), lambda i: (i, 0)), pl.BlockSpec((k, d), lambda i: (0, 0)),
                  pl.BlockSpec((1, d), lambda i: (0, 0)), pl.BlockSpec((tm, d), lambda i: (i, 0))],
        out_specs=[pl.BlockSpec((tm, d), lambda i: (i, 0)), pl.BlockSpec((tm, d), lambda i: (i, 0))],
        out_shape=[SDS((n, d), f32), SDS((n, d), f32)], compiler_params=_params(),
    )(a, w, g, h)


def _rms_bwd(x, g, dy, dh_in, out_dtype, tm, name):
    n, d = x.shape
    has_add = dh_in is not None

    def body(*refs):
        if has_add:
            x_ref, g_ref, dy_ref, dh_ref, o_ref, dg_ref = refs
        else:
            x_ref, g_ref, dy_ref, o_ref, dg_ref = refs
        xv = x_ref[...]
        dyv = dy_ref[...].astype(f32)
        r = lax.rsqrt(jnp.mean(xv * xv, axis=-1, keepdims=True) + EPS)
        xr = xv * r
        gdy = dyv * g_ref[...]
        dx = r * gdy - xr * (r * r) * jnp.mean(xv * gdy, axis=-1, keepdims=True)
        if has_add:
            dx = dx + dh_ref[...]
        o_ref[...] = dx.astype(o_ref.dtype)

        @pl.when(pl.program_id(0) == 0)
        def _():
            dg_ref[...] = jnp.zeros_like(dg_ref)

        dg_ref[...] += jnp.sum(dyv * xr, axis=0, keepdims=True)

    row = pl.BlockSpec((tm, d), lambda i: (i, 0))
    vec = pl.BlockSpec((1, d), lambda i: (0, 0))
    ins = [x, g, dy] + ([dh_in] if has_add else [])
    return pl.pallas_call(
        body, grid=(n // tm,), name=name,
        in_specs=[row, vec, row] + ([row] if has_add else []),
        out_specs=[row, vec],
        out_shape=[SDS((n, d), out_dtype), SDS((1, d), f32)], compiler_params=_params(),
    )(*ins)


def _mm_nt(pairs, out_dtype, tm, tk, name):
    n = pairs[0][0].shape[0]
    k = pairs[0][1].shape[0]
    np_ = len(pairs)

    def body(*refs):
        o_ref = refs[-1]
        acc = None
        for p in range(np_):
            t = _bdot_nt(refs[2 * p][...], refs[2 * p + 1][...])
            acc = t if acc is None else acc + t
        o_ref[...] = acc.astype(o_ref.dtype)

    in_specs, ins = [], []
    for dy, w in pairs:
        m = dy.shape[1]
        in_specs += [pl.BlockSpec((tm, m), lambda j, i: (i, 0)), pl.BlockSpec((tk, m), lambda j, i: (j, 0))]
        ins += [dy, w]
    return pl.pallas_call(
        body, grid=(k // tk, n // tm), name=name, in_specs=in_specs,
        out_specs=pl.BlockSpec((tm, tk), lambda j, i: (i, j)),
        out_shape=SDS((n, k), out_dtype), compiler_params=_params(),
    )(*ins)


def _mm_tn(x, dy, tm, tk, tn, name):
    n, k = x.shape
    m = dy.shape[1]

    def body(x_ref, dy_ref, o_ref):
        @pl.when(pl.program_id(2) == 0)
        def _():
            o_ref[...] = jnp.zeros_like(o_ref)

        o_ref[...] += _bdot_tn(x_ref[...], dy_ref[...])

    return pl.pallas_call(
        body, grid=(k // tk, m // tn, n // tm), name=name,
        in_specs=[pl.BlockSpec((tm, tk), lambda a, b, i: (i, a)), pl.BlockSpec((tm, tn), lambda a, b, i: (i, b))],
        out_specs=pl.BlockSpec((tk, tn), lambda a, b, i: (a, b)),
        out_shape=SDS((k, m), f32), compiler_params=_params(),
    )(x, dy)


def _split3(x):
    hi = x.astype(bf16)
    r = x - hi.astype(f32)
    mid = r.astype(bf16)
    return hi, mid, (r - mid.astype(f32)).astype(bf16)


def _mask_dot(mask, x):
    hi, mid, lo = _split3(x)
    dot = lambda p: jnp.dot(mask, p, preferred_element_type=f32)
    return dot(hi) + dot(mid) + dot(lo)


def _chunk_rows(parts, cl):
    tiles = [jnp.broadcast_to(p, (cl, p.shape[1])) for p in parts]
    return tiles[0] if len(tiles) == 1 else jnp.concatenate(tiles, axis=0)


def _cat(parts):
    return parts[0] if len(parts) == 1 else jnp.concatenate(parts, axis=0)


def _gla_group_fwd(qg, fg, vg, lb, st, nc, cl):
    g = nc * cl
    sg = _sigmoid(fg)
    f = lb + (1.0 - lb) * sg
    k = 1.0 - f
    row, col = _iota2((g, g), 0), _iota2((g, g), 1)
    chunk_of = lambda idx: sum((idx >= u * cl).astype(jnp.int32) for u in range(1, nc)) if nc > 1 else 0
    same = chunk_of(row) == chunk_of(col) if nc > 1 else None
    causal = row >= col if nc == 1 else jnp.logical_and(same, row >= col)
    anti = col >= row if nc == 1 else jnp.logical_and(same, col >= row)
    b = _mask_dot(causal.astype(bf16), jnp.log(f))
    bls = [b[(u + 1) * cl - 1:(u + 1) * cl, :] for u in range(nc)]
    ebls = [jnp.exp(x) for x in bls]
    e = jnp.exp(b)
    ei = jnp.exp(-b)
    eo = jnp.exp(_chunk_rows(bls, cl) - b)
    qi, ki, ko = qg * e, k * ei, k * eo
    att = jnp.where(causal, _bdot_nt(qi, ki), 0.0)
    o_intra = _bdot(att, vg)
    sl = [slice(u * cl, (u + 1) * cl) for u in range(nc)]
    ds = [_bdot_tn(vg[s], ko[s]) for s in sl]
    sts = [st]
    for u in range(nc):
        sts.append(sts[u] * ebls[u] + ds[u])
    o = o_intra + _cat([_bdot_nt(qi[sl[u]], sts[u]) for u in range(nc)])
    return dict(sg=sg, f=f, e=e, ei=ei, eo=eo, ebls=ebls, qi=qi, ki=ki, ko=ko, att=att, o=o, sts=sts, causal=causal,
                anti=anti, sl=sl)


def _gla_group(nreal, want):
    while nreal % want:
        want //= 2
    return max(want, 1)


def _head_out(o, ggc, hg):
    r = lax.rsqrt(jnp.mean(o * o, axis=-1, keepdims=True) + EPS)
    return o * r * hg * (ggc * _sigmoid(ggc))


def _gla_fwd(pmat, lb, hg, bl_, t, nm, name):
    n, d4 = pmat.shape
    d = d4 // 4
    nh = d // A_DK
    nreal = (t - nm) // A_CHUNK
    nch = nreal + 1
    un = _gla_group(nreal, GLA_GROUP)

    def body(q_ref, f_ref, i_ref, gg_ref, lb_ref, hg_ref, og_ref, ss_ref):
        lbv, hgv = lb_ref[...], hg_ref[...]

        def run(rows, st, idx, nc, cl):
            w = _gla_group_fwd(q_ref[rows, :], f_ref[rows, :], i_ref[rows, :], lbv, st, nc, cl)
            for u in range(nc):
                ss_ref[0, idx + u] = w["sts"][u]
            og_ref[rows, :] = _head_out(w["o"], gg_ref[rows, :], hgv).astype(og_ref.dtype)
            return w["sts"][nc]

        st = run(pl.ds(0, nm), jnp.zeros((A_DK, A_DK), f32), 0, 1, nm)

        def step(it, st):
            rows = pl.ds(pl.multiple_of(nm + it * (un * A_CHUNK), BF16_ROWS), un * A_CHUNK)
            return run(rows, st, 1 + it * un, un, A_CHUNK)

        lax.fori_loop(0, nreal // un, step, st)

    col = lambda o: pl.BlockSpec((t, A_DK), lambda b, h: (b, o * nh + h))
    vec = pl.BlockSpec((1, A_DK), lambda b, h: (0, h))
    return pl.pallas_call(
        body, grid=(bl_, nh), name=name,
        in_specs=[col(0), col(1), col(2), col(3), vec, vec],
        out_specs=[pl.BlockSpec((t, A_DK), lambda b, h: (b, h)),
                   pl.BlockSpec((1, nch, A_DK, A_DK), lambda b, h: (b * nh + h, 0, 0, 0))],
        out_shape=[SDS((n, d), bf16), SDS((bl_ * nh, nch, A_DK, A_DK), f32)], compiler_params=_params(),
    )(pmat, pmat, pmat, pmat, lb, hg)


def _gla_bwd(pmat, ss, dog, lb, hg, bl_, t, nm, name):
    n, d4 = pmat.shape
    d = d4 // 4
    nh = d // A_DK
    nreal = (t - nm) // A_CHUNK
    nch = nreal + 1
    un = _gla_group(nreal, GLA_GROUP)

    def body(q_ref, f_ref, i_ref, gg_ref, ss_ref, dog_ref, lb_ref, hg_ref,
             dq_ref, df_ref, di_ref, dgg_ref, dlb_ref, dhg_ref):
        lbv, hgv = lb_ref[...], hg_ref[...]

        def run(rows, idx, carry, nc, cl):
            dst, dlb, dhg = carry
            qg, fg, vg, ggc = q_ref[rows, :], f_ref[rows, :], i_ref[rows, :], gg_ref[rows, :]
            dogc = dog_ref[rows, :].astype(f32)
            w = _gla_group_fwd(qg, fg, vg, lbv, ss_ref[0, idx], nc, cl)
            o, qi, ki, ko, sl, sts, ebls = w["o"], w["qi"], w["ki"], w["ko"], w["sl"], w["sts"], w["ebls"]
            r = lax.rsqrt(jnp.mean(o * o, axis=-1, keepdims=True) + EPS)
            sgg = _sigmoid(ggc)
            sil = ggc * sgg
            on = o * r
            dhg = dhg + jnp.sum(dogc * sil * on, axis=0, keepdims=True)
            dgg_ref[rows, :] = (dogc * on * hgv * (sgg * (1.0 + ggc * (1.0 - sgg)))).astype(dgg_ref.dtype)
            tt = dogc * sil * hgv
            do = r * tt - on * (r * r) * jnp.mean(o * tt, axis=-1, keepdims=True)
            xs = [_bdot_tn(do[s], qi[s]) for s in sl]
            dsts = [None] * nc + [dst]
            for u in reversed(range(nc)):
                dsts[u] = dsts[u + 1] * ebls[u] + xs[u]
            datt = jnp.where(w["causal"], _bdot_nt(do, vg), 0.0)
            dv = _bdot_tn(w["att"], do) + _cat([_bdot_nt(ko[sl[u]], dsts[u + 1]) for u in range(nc)])
            dko = _cat([_bdot(vg[sl[u]], dsts[u + 1]) for u in range(nc)])
            dqi = _bdot(datt, ki) + _cat([_bdot(do[sl[u]], sts[u]) for u in range(nc)])
            dki = _bdot_tn(datt, qi)
            dk = dki * w["ei"] + dko * w["eo"]
            dkoko = dko * ko
            db = dqi * qi - dki * ki - dkoko
            rowi = _iota2(db.shape, 0)
            for u in range(nc):
                d_ebl = jnp.sum(dsts[u + 1] * sts[u], axis=0, keepdims=True)
                dbl = jnp.sum(dkoko[sl[u]], axis=0, keepdims=True) + d_ebl * ebls[u]
                db = db + jnp.where(rowi == (u + 1) * cl - 1, dbl, 0.0)
            dlogf = _mask_dot(w["anti"].astype(bf16), db)
            df = dlogf / w["f"] - dk
            sg = w["sg"]
            dq_ref[rows, :] = (dqi * w["e"]).astype(dq_ref.dtype)
            df_ref[rows, :] = (df * (1.0 - lbv) * sg * (1.0 - sg)).astype(df_ref.dtype)
            di_ref[rows, :] = dv.astype(di_ref.dtype)
            dlb = dlb + jnp.sum(df * (1.0 - sg), axis=0, keepdims=True)
            return dsts[0], dlb, dhg

        zero = jnp.zeros((1, A_DK), f32)
        ngroups = nreal // un

        def step(it, carry):
            grp = ngroups - 1 - it
            rows = pl.ds(pl.multiple_of(nm + grp * (un * A_CHUNK), BF16_ROWS), un * A_CHUNK)
            return run(rows, 1 + grp * un, carry, un, A_CHUNK)

        carry = lax.fori_loop(0, ngroups, step, (jnp.zeros((A_DK, A_DK), f32), zero, zero))
        _, dlb, dhg = run(pl.ds(0, nm), 0, carry, 1, nm)

        @pl.when(pl.program_id(1) == 0)
        def _():
            dlb_ref[...] = jnp.zeros_like(dlb_ref)
            dhg_ref[...] = jnp.zeros_like(dhg_ref)

        dlb_ref[...] += dlb
        dhg_ref[...] += dhg

    col = lambda o: pl.BlockSpec((t, A_DK), lambda h, b: (b, o * nh + h))
    blk = pl.BlockSpec((t, A_DK), lambda h, b: (b, h))
    vec = pl.BlockSpec((1, A_DK), lambda h, b: (0, h))
    return pl.pallas_call(
        body, grid=(nh, bl_), name=name,
        in_specs=[col(0), col(1), col(2), col(3),
                  pl.BlockSpec((1, nch, A_DK, A_DK), lambda h, b: (b * nh + h, 0, 0, 0)), blk, vec, vec],
        out_specs=[blk, blk, blk, blk, vec, vec],
        out_shape=[SDS((n, d), bf16)] * 4 + [SDS((1, d), f32)] * 2, compiler_params=_params(),
    )(pmat, pmat, pmat, pmat, ss, dog, lb, hg)


def _shift_down(x, prev2, s):
    row = _iota2(x.shape, 0)
    y = pltpu.roll(x, s, 0)
    if s == 1:
        return jnp.where(row == 0, prev2[1:2, :], y)
    return jnp.where(row == 0, prev2[0:1, :], jnp.where(row == 1, prev2[1:2, :], y))


def _shift_up(x, next2, s):
    n = x.shape[0]
    row = _iota2(x.shape, 0)
    y = pltpu.roll(x, n - s, 0)
    if s == 1:
        return jnp.where(row == n - 1, next2[0:1, :], y)
    return jnp.where(row == n - 1, next2[1:2, :], jnp.where(row == n - 2, next2[0:1, :], y))


def _conv3(x, prev2, w):
    return w[0:1, :] * _shift_down(x, prev2, 2) + w[1:2, :] * _shift_down(x, prev2, 1) + w[2:3, :] * x


def _conv_gate_fwd(ug, uv, cwg, cwv, bl_, t, tc, name):
    n, ff = ug.shape
    nt = t // tc

    def body(ug_ref, uv_ref, wg_ref, wv_ref, a_ref, hg_ref, hv_ref):
        @pl.when(pl.program_id(1) == 0)
        def _():
            hg_ref[...] = jnp.zeros_like(hg_ref)
            hv_ref[...] = jnp.zeros_like(hv_ref)

        xg = ug_ref[...].astype(f32)
        xv = uv_ref[...].astype(f32)
        cg = _conv3(xg, hg_ref[...], wg_ref[...])
        cv = _conv3(xv, hv_ref[...], wv_ref[...])
        a_ref[...] = (cg * _sigmoid(cg) * cv).astype(a_ref.dtype)
        hg_ref[...] = xg[tc - 2:tc, :]
        hv_ref[...] = xv[tc - 2:tc, :]

    row = pl.BlockSpec((tc, ff), lambda b, i: (b * nt + i, 0))
    wsp = pl.BlockSpec((3, ff), lambda b, i: (0, 0))
    return pl.pallas_call(
        body, grid=(bl_, nt), name=name, in_specs=[row, row, wsp, wsp], out_specs=row,
        out_shape=SDS((n, ff), bf16),
        scratch_shapes=[pltpu.VMEM((2, ff), f32), pltpu.VMEM((2, ff), f32)], compiler_params=_params(),
    )(ug, uv, cwg, cwv)


def _conv_gate_bwd(ug, uv, cwg, cwv, da, bl_, t, tc, name):
    n, ff = ug.shape
    nt = t // tc
    per = tc // BF16_ROWS

    def body(ug_ref, uv_ref, pg_ref, pv_ref, wg_ref, wv_ref, da_ref, dug_ref, duv_ref, dwg_ref, dwv_ref, ng_ref, nv_ref):
        first = jnp.logical_and(pl.program_id(0) == 0, pl.program_id(1) == 0)

        @pl.when(first)
        def _():
            dwg_ref[...] = jnp.zeros_like(dwg_ref)
            dwv_ref[...] = jnp.zeros_like(dwv_ref)

        @pl.when(pl.program_id(1) == 0)
        def _():
            ng_ref[...] = jnp.zeros_like(ng_ref)
            nv_ref[...] = jnp.zeros_like(nv_ref)

        seq_start = pl.program_id(1) == nt - 1
        dav = da_ref[...].astype(f32)

        def half(u_ref, p_ref, w_ref):
            x = u_ref[...].astype(f32)
            prev2 = jnp.where(seq_start, 0.0, p_ref[BF16_ROWS - 2:BF16_ROWS, :].astype(f32))
            x1, x2 = _shift_down(x, prev2, 1), _shift_down(x, prev2, 2)
            w = w_ref[...]
            return x, x1, x2, w[0:1, :] * x2 + w[1:2, :] * x1 + w[2:3, :] * x

        xg, xg1, xg2, cg = half(ug_ref, pg_ref, wg_ref)
        xv, xv1, xv2, cv = half(uv_ref, pv_ref, wv_ref)
        sg = _sigmoid(cg)
        dcg = dav * cv * (sg * (1.0 + cg * (1.0 - sg)))
        dcv = dav * (cg * sg)

        def back(dc, x, x1, x2, w_ref, nx_ref, du_ref, dw_ref):
            w = w_ref[...]
            nx = nx_ref[...]
            du = w[2:3, :] * dc + w[1:2, :] * _shift_up(dc, nx, 1) + w[0:1, :] * _shift_up(dc, nx, 2)
            du_ref[...] = du.astype(du_ref.dtype)
            dw_ref[0:1, :] += jnp.sum(dc * x2, axis=0, keepdims=True)
            dw_ref[1:2, :] += jnp.sum(dc * x1, axis=0, keepdims=True)
            dw_ref[2:3, :] += jnp.sum(dc * x, axis=0, keepdims=True)
            nx_ref[...] = dc[0:2, :]

        back(dcg, xg, xg1, xg2, wg_ref, ng_ref, dug_ref, dwg_ref)
        back(dcv, xv, xv1, xv2, wv_ref, nv_ref, duv_ref, dwv_ref)

    row = pl.BlockSpec((tc, ff), lambda b, i: (b * nt + nt - 1 - i, 0))
    prev = pl.BlockSpec((BF16_ROWS, ff), lambda b, i: (jnp.maximum((b * nt + nt - 1 - i) * per - 1, 0), 0))
    wsp = pl.BlockSpec((3, ff), lambda b, i: (0, 0))
    return pl.pallas_call(
        body, grid=(bl_, nt), name=name, in_specs=[row, row, prev, prev, wsp, wsp, row],
        out_specs=[row, row, wsp, wsp],
        out_shape=[SDS((n, ff), bf16), SDS((n, ff), bf16), SDS((3, ff), f32), SDS((3, ff), f32)],
        scratch_shapes=[pltpu.VMEM((2, ff), f32), pltpu.VMEM((2, ff), f32)], compiler_params=_params(),
    )(ug, uv, ug, uv, cwg, cwv, da)


def _zf_c(hk, wzf, fgb, bl_, t, tm, name):
    n, d = hk.shape
    nt = t // tm

    def body(hk_ref, w_ref, b_ref, zf_ref, c_ref, carry_ref):
        @pl.when(pl.program_id(1) == 0)
        def _():
            carry_ref[...] = jnp.zeros_like(carry_ref)

        z = _bdot(hk_ref[...], w_ref[...]) + b_ref[...]
        ls = jnp.minimum(z, 0.0) - jnp.log(1.0 + jnp.exp(-jnp.abs(z)))
        c = _cumsum_rows(ls) + carry_ref[...]
        zf_ref[...] = z
        c_ref[...] = c
        carry_ref[...] = c[tm - 1:tm, :]

    row = lambda w: pl.BlockSpec((tm, w), lambda b, i: (b * nt + i, 0))
    return pl.pallas_call(
        body, grid=(bl_, nt), name=name,
        in_specs=[row(d), pl.BlockSpec((d, LANES), lambda b, i: (0, 0)), pl.BlockSpec((1, LANES), lambda b, i: (0, 0))],
        out_specs=[row(LANES), row(LANES)],
        out_shape=[SDS((n, LANES), f32), SDS((n, LANES), f32)],
        scratch_shapes=[pltpu.VMEM((1, LANES), f32)], compiler_params=_params(),
    )(hk, wzf, fgb)


def _c_bwd(dc, zf, bl_, t, tm, name):
    n = dc.shape[0]
    nt = t // tm

    def body(dc_ref, zf_ref, dzf_ref, dfg_ref, carry_ref):
        @pl.when(jnp.logical_and(pl.program_id(0) == 0, pl.program_id(1) == 0))
        def _():
            dfg_ref[...] = jnp.zeros_like(dfg_ref)

        @pl.when(pl.program_id(1) == 0)
        def _():
            carry_ref[...] = jnp.zeros_like(carry_ref)

        rc = _revcumsum_rows(dc_ref[...]) + carry_ref[...]
        dz = rc * _sigmoid(-zf_ref[...])
        dzf_ref[...] = dz.astype(dzf_ref.dtype)
        dfg_ref[...] += jnp.sum(dz, axis=0, keepdims=True)
        carry_ref[...] = rc[0:1, :]

    row = pl.BlockSpec((tm, LANES), lambda b, i: (b * nt + nt - 1 - i, 0))
    vec = pl.BlockSpec((1, LANES), lambda b, i: (0, 0))
    return pl.pallas_call(
        body, grid=(bl_, nt), name=name, in_specs=[row, row], out_specs=[row, vec],
        out_shape=[SDS((n, LANES), bf16), SDS((1, LANES), f32)],
        scratch_shapes=[pltpu.VMEM((1, LANES), f32)], compiler_params=_params(),
    )(dc, zf)


def _is_pow2(x):
    m, _ = math.frexp(x)
    return m == 0.5


def _prescale(qh, scale):
    return (qh.astype(f32) * scale).astype(bf16)


def _attn_fwd(q, kv, ck, bl_, t, tq, hd, name):
    n, d = q.shape
    npair = d // LANES
    hp = LANES // hd
    nq = t // tq
    scale = 1.0 / (hd ** 0.5)

    pre = _is_pow2(scale)

    def body(q_ref, k_ref, v_ref, ck_ref, o_ref, lse_ref):
        i = pl.program_id(2)
        diag = _iota2((tq, tq), 0) >= _iota2((tq, tq), 1)
        for hh in range(hp):
            lanes = slice(hh * hd, (hh + 1) * hd)
            qh = _prescale(q_ref[:, lanes], scale) if pre else q_ref[:, lanes]

            def block(j, carry, masked, lanes=lanes, qh=qh, hh=hh):
                m, l, acc = carry
                rows = pl.ds(pl.multiple_of(j * tq, BF16_ROWS), tq)
                s = _bdot_nt(qh, k_ref[rows, lanes])
                s = (s if pre else s * scale) - ck_ref[0, 0, j, hh:hh + 1, :]
                if masked:
                    s = jnp.where(diag, s, -1e30)
                m2 = jnp.maximum(m, jnp.max(s, axis=-1, keepdims=True))
                p = jnp.exp(s - m2)
                a = jnp.exp(m - m2)
                return m2, a * l + jnp.sum(p, axis=-1, keepdims=True), a * acc + _bdot(p, v_ref[rows, lanes])

            init = (jnp.full((tq, 1), -1e30, f32), jnp.zeros((tq, 1), f32), jnp.zeros((tq, hd), f32))
            carry = lax.fori_loop(0, i, functools.partial(block, masked=False), init)
            m, l, acc = block(i, carry, True)
            o_ref[:, lanes] = (acc / l).astype(o_ref.dtype)
            lse_ref[:, lanes] = jnp.broadcast_to(m + jnp.log(l), (tq, hd))

    nk = nq
    return pl.pallas_call(
        body, grid=(bl_, npair, nq), name=name,
        in_specs=[pl.BlockSpec((tq, LANES), lambda b, p, i: (b * nq + i, p)),
                  pl.BlockSpec((t, LANES), lambda b, p, i: (b, p)),
                  pl.BlockSpec((t, LANES), lambda b, p, i: (b, npair + p)),
                  pl.BlockSpec((1, 1, nk, hp, tq), lambda b, p, i: (b, p, 0, 0, 0))],
        out_specs=[pl.BlockSpec((tq, LANES), lambda b, p, i: (b * nq + i, p)),
                   pl.BlockSpec((tq, LANES), lambda b, p, i: (b * nq + i, p))],
        out_shape=[SDS((n, d), f32), SDS((n, d), f32)], compiler_params=_params(),
    )(q, kv, kv, ck)


def _attn_bwd(q, kv, o, do, lse, ck, bl_, t, tq, hd, name):
    n, d = q.shape
    npair = d // LANES
    hp = LANES // hd
    nq = t // tq
    scale = 1.0 / (hd ** 0.5)

    pre = _is_pow2(scale)

    def body(q_ref, k_ref, v_ref, o_ref, do_ref, lse_ref, ck_ref, dq_ref, dk_ref, dv_ref, dck_ref, dcq_ref):
        j = pl.program_id(2)

        @pl.when(j == 0)
        def _():
            dq_ref[...] = jnp.zeros_like(dq_ref)
            dcq_ref[...] = jnp.zeros_like(dcq_ref)

        diag = _iota2((tq, tq), 0) >= _iota2((tq, tq), 1)
        for hh in range(hp):
            lanes = slice(hh * hd, (hh + 1) * hd)
            kh = k_ref[:, lanes]
            vh = v_ref[:, lanes]
            cs = ck_ref[0, 0, 0, hh:hh + 1, :]

            def block(i, carry, masked, lanes=lanes, kh=kh, vh=vh, cs=cs, hh=hh):
                dk, dv, dcs = carry
                rows = pl.ds(pl.multiple_of(i * tq, BF16_ROWS), tq)
                qh = _prescale(q_ref[rows, lanes], scale) if pre else q_ref[rows, lanes]
                doh = do_ref[rows, lanes]
                s = _bdot_nt(qh, kh)
                s = (s if pre else s * scale) - cs
                if masked:
                    s = jnp.where(diag, s, -1e30)
                p = jnp.exp(s - lse_ref[rows, hh * hd:hh * hd + 1])
                delta = jnp.sum(doh.astype(f32) * o_ref[rows, lanes].astype(f32), axis=-1, keepdims=True)
                ds = p * (_bdot_nt(doh, vh) - delta)
                dq_ref[rows, lanes] += _bdot(ds, kh) * scale
                dcq_ref[0, rows, hh:hh + 1] += jnp.sum(ds, axis=-1, keepdims=True)
                dkq = _bdot_tn(ds, qh)
                return (dk + (dkq if pre else dkq * scale), dv + _bdot_tn(p, doh), dcs - jnp.sum(ds, axis=0, keepdims=True))

            init = (jnp.zeros((tq, hd), f32), jnp.zeros((tq, hd), f32), jnp.zeros((1, tq), f32))
            dk, dv, dcs = lax.fori_loop(j + 1, nq, functools.partial(block, masked=False), block(j, init, True))
            dk_ref[:, lanes] = dk.astype(dk_ref.dtype)
            dv_ref[:, lanes] = dv.astype(dv_ref.dtype)
            dck_ref[0, 0, 0, hh:hh + 1, :] = dcs

    whole = lambda c0: pl.BlockSpec((t, LANES), lambda b, p, j: (b, c0 + p))
    tile = lambda c0: pl.BlockSpec((tq, LANES), lambda b, p, j: (b * nq + j, c0 + p))
    ckspec = pl.BlockSpec((1, 1, 1, hp, tq), lambda b, p, j: (b, p, j, 0, 0))
    cqspec = pl.BlockSpec((1, t, hp), lambda b, p, j: (p, b, 0))
    return pl.pallas_call(
        body, grid=(bl_, npair, nq), name=name,
        in_specs=[whole(0), tile(0), tile(npair), whole(0), whole(0), whole(0), ckspec],
        out_specs=[whole(0), tile(0), tile(0), ckspec, cqspec],
        out_shape=[SDS((n, d), f32), SDS((n, d), bf16), SDS((n, d), bf16), SDS((bl_, npair, nq, hp, tq), f32),
                   SDS((npair, n, hp), f32)],
        compiler_params=_params(),
    )(q, kv, kv, o, do, lse, ck)


def _loss_head(h, target, t, nm, tm, name):
    n, d = h.shape
    nt = t // tm

    def body(h_ref, t_ref, loss_ref, dh_ref):
        i = pl.program_id(0)

        @pl.when(i == 0)
        def _():
            loss_ref[...] = jnp.zeros_like(loss_ref)

        pos = (i % nt) * tm + _iota2((tm, d), 0)
        err = jnp.where(pos >= nm, h_ref[...] - t_ref[...], 0.0)
        dh_ref[...] = err * (1.0 / d)
        loss_ref[...] += 0.5 * jnp.sum(jnp.mean(err * err, axis=-1, keepdims=True))

    row = pl.BlockSpec((tm, d), lambda i: (i, 0))
    return pl.pallas_call(
        body, grid=(n // tm,), name=name, in_specs=[row, row],
        out_specs=[pl.BlockSpec((8, LANES), lambda i: (0, 0)), row],
        out_shape=[SDS((8, LANES), f32), SDS((n, d), f32)], compiler_params=_params(),
    )(h, target)


def _c_key_rows(c, bl_, t, tq, bh, hp):
    npair = bh // hp
    nk = t // tq
    return c[:, :bh].reshape(bl_, nk, tq, npair, hp).transpose(0, 3, 1, 4, 2)


def _dc_rows(dck, dcq, bl_, t, bh):
    d = dck.transpose(0, 2, 4, 1, 3).reshape(bl_ * t, bh) + dcq.transpose(1, 0, 2).reshape(bl_ * t, bh)
    return jnp.pad(d, ((0, 0), (0, LANES - bh)))


_ANY = pl.BlockSpec(memory_space=pl.ANY)


def _all_gather(xs, name):
    na = len(xs)

    def body(*refs):
        x_refs, out_refs = refs[:na], refs[na:2 * na]
        send_sems, recv_sems, local_sems = refs[2 * na:]
        mx, my, mc = lax.axis_index("x"), lax.axis_index("y"), lax.axis_index("c")
        me, sibling = (mx, my, mc), (mx, my, 1 - mc)
        chips = [(1 - mx, my), (mx, 1 - my), (1 - mx, 1 - my)]

        def copy(a, k, block, to, own=False):
            px, py, pc = block
            rows = out_refs[a].at[4 * px + 2 * py + pc]
            return pltpu.make_async_remote_copy(
                src_ref=x_refs[a] if own else rows, dst_ref=rows,
                send_sem=send_sems.at[a, k], recv_sem=recv_sems.at[a, k], device_id=to, device_id_type=MESH)

        arrays = range(na)
        mine = [pltpu.make_async_copy(x_refs[a], out_refs[a].at[4 * mx + 2 * my + mc], local_sems.at[a]) for a in arrays]
        for cp in mine:
            cp.start()
        first = [copy(a, 1 + j, me, (*chip, mc), own=True) for j, chip in enumerate(chips) for a in arrays]
        first += [copy(a, 0, me, sibling, own=True) for a in arrays]
        for cp in first:
            cp.start()
        passed = []
        for j, chip in enumerate(chips):
            for a in arrays:
                copy(a, 1 + j, (*chip, mc), me).wait_recv()
                cp = copy(a, 4 + j, (*chip, mc), sibling)
                cp.start()
                passed.append(cp)
        for a in arrays:
            copy(a, 0, sibling, me).wait_recv()
        for j, chip in enumerate(chips):
            for a in arrays:
                copy(a, 4 + j, (*chip, 1 - mc), me).wait_recv()
        for cp in first + passed:
            cp.wait_send()
        for cp in mine:
            cp.wait()

    return pl.pallas_call(
        body, name=name, out_shape=[SDS((N_DEV,) + x.shape, x.dtype) for x in xs],
        in_specs=[_ANY] * na, out_specs=[_ANY] * na,
        scratch_shapes=[pltpu.SemaphoreType.DMA((na, 7)), pltpu.SemaphoreType.DMA((na, 7)), pltpu.SemaphoreType.DMA((na,))],
    )(*xs)


def _exchange(bufs, group, name):
    na = len(bufs)
    g = 2 if group == "c" else 4
    assert all(b.shape[0] == g for b in bufs)
    keep_own = g > 2

    def body(*refs):
        buf_refs, out_refs = refs[:na], refs[na:2 * na]
        send_sems, recv_sems, local_sems = refs[2 * na:]
        mx, my, mc = lax.axis_index("x"), lax.axis_index("y"), lax.axis_index("c")
        if group == "c":
            me = mc
            peers = [((mx, my, 1 - mc), 1 - mc)]
        else:
            me = 2 * mx + my
            peers = []
            for r in range(1, 4):
                px = 1 - mx if r & 2 else mx
                py = 1 - my if r & 1 else my
                peers.append(((px, py, mc), 2 * px + py))

        def copy(a, k, dev, src_row, dst_row):
            return pltpu.make_async_remote_copy(
                src_ref=buf_refs[a].at[src_row], dst_ref=out_refs[a].at[dst_row] if keep_own else out_refs[a],
                send_sem=send_sems.at[a, k], recv_sem=recv_sems.at[a, k], device_id=dev, device_id_type=MESH)

        sends = [copy(a, k, dev, idx, me) for k, (dev, idx) in enumerate(peers) for a in range(na)]
        for cp in sends:
            cp.start()
        own = []
        if keep_own:
            own = [pltpu.make_async_copy(buf_refs[a].at[me], out_refs[a].at[me], local_sems.at[a]) for a in range(na)]
        for cp in own:
            cp.start()
        for k, (dev, idx) in enumerate(peers):
            for a in range(na):
                copy(a, k, dev, idx, idx).wait_recv()
        for cp in sends:
            cp.wait_send()
        for cp in own:
            cp.wait()

    return pl.pallas_call(
        body, name=name, out_shape=[SDS(b.shape if keep_own else b.shape[1:], b.dtype) for b in bufs],
        in_specs=[_ANY] * na, out_specs=[_ANY] * na,
        scratch_shapes=[pltpu.SemaphoreType.DMA((na, g - 1)), pltpu.SemaphoreType.DMA((na, g - 1)),
                        pltpu.SemaphoreType.DMA((na,))],
    )(*bufs)


def _sum_own_recv(buf, recv, core, tr, name):
    _, r, c = buf.shape

    def body(core_ref, own_ref, recv_ref, o_ref):
        o_ref[...] = (own_ref[0].astype(f32) + recv_ref[...].astype(f32)).astype(o_ref.dtype)

    row = pl.BlockSpec((tr, c), lambda i, core_ref: (i, 0))
    return pl.pallas_call(
        body, name=name, out_shape=SDS((r, c), buf.dtype), compiler_params=_params(),
        grid_spec=pltpu.PrefetchScalarGridSpec(
            num_scalar_prefetch=1, grid=(r // tr,),
            in_specs=[pl.BlockSpec((1, tr, c), lambda i, core_ref: (core_ref[0], i, 0)), row], out_specs=row),
    )(core, buf, recv)


def _sum_rows(parts, tr, name):
    g, r, c = parts.shape

    def body(p_ref, o_ref):
        acc = p_ref[0].astype(f32)
        for k in range(1, g):
            acc = acc + p_ref[k].astype(f32)
        o_ref[...] = acc.astype(o_ref.dtype)

    return pl.pallas_call(
        body, grid=(r // tr,), name=name, in_specs=[pl.BlockSpec((g, tr, c), lambda i: (0, i, 0))],
        out_specs=pl.BlockSpec((tr, c), lambda i: (i, 0)), out_shape=SDS((r, c), parts.dtype), compiler_params=_params(),
    )(parts)


def _adamw(parts, w, m, v, tr, name):
    g, r, c = parts.shape

    def body(p_ref, w_ref, m_ref, v_ref, g_ref, d_ref, m2_ref, v2_ref):
        gr = p_ref[0].astype(f32)
        for k in range(1, g):
            gr = gr + p_ref[k].astype(f32)
        m2 = ADAM_B1 * m_ref[...] + (1.0 - ADAM_B1) * gr
        v2 = ADAM_B2 * v_ref[...] + (1.0 - ADAM_B2) * (gr * gr)
        m_hat = m2 / (1.0 - ADAM_B1 ** ADAM_STEP)
        v_hat = v2 / (1.0 - ADAM_B2 ** ADAM_STEP)
        g_ref[...] = gr
        d_ref[...] = -ADAM_LR * (m_hat / (jnp.sqrt(v_hat) + ADAM_EPS) + ADAM_WD * w_ref[...])
        m2_ref[...] = m2
        v2_ref[...] = v2

    row = pl.BlockSpec((tr, c), lambda i: (i, 0))
    return pl.pallas_call(
        body, grid=(r // tr,), name=name, in_specs=[pl.BlockSpec((g, tr, c), lambda i: (0, i, 0)), row, row, row],
        out_specs=[row] * 4, out_shape=[SDS((r, c), f32)] * 4, compiler_params=_params(),
    )(parts, w, m, v)


_SHARD_AXIS = dict(meta_tokens=1, norm_gains=2, a_w_in=2, a_lb_logits=1, a_head_norm=1, a_w_out=1, kv_w=1,
                   b_w_q=1, b_w_out=1, ffn_w_up=2, ffn_conv=2, ffn_w_down=1)
_MATRICES = ("a_w_in", "a_w_out", "kv_w", "b_w_q", "b_w_out", "ffn_w_up", "ffn_w_down")
_VECTORS = ("meta_tokens", "norm_gains", "a_lb_logits", "a_head_norm", "ffn_conv")
_SHARDED = tuple(_SHARD_AXIS)
_REPLICATED = ("kv_norm", "fg_b")
_ROW_TILE_CAP = 512


def _pack(arrs, dtype, cols, row_mult):
    lead = arrs[0].shape[:-1] if arrs[0].ndim > 1 else ()
    flat = jnp.concatenate([a.astype(dtype) for a in arrs], axis=-1)
    size = flat.shape[-1]
    per = cols * row_mult
    total = -(-size // per) * per
    flat = jnp.pad(flat, [(0, 0)] * len(lead) + [(0, total - size)])
    return flat.reshape(lead + (total // cols, cols))


def _unpack(flat, shapes):
    out, off = [], 0
    lead = flat.shape[:-1]
    for shp in shapes:
        size = 1
        for s in shp:
            size *= s
        out.append(flat[..., off:off + size].reshape(lead + tuple(shp)))
        off += size
    return out


def _unshard(seg, axis):
    a = jnp.moveaxis(seg, 0, axis)
    shp = a.shape
    return a.reshape(shp[:axis] + (shp[axis] * shp[axis + 1],) + shp[axis + 2:])


def _shard8(full, axis):
    shp = full.shape
    a = full.reshape(shp[:axis] + (N_DEV, shp[axis] // N_DEV) + shp[axis + 1:])
    return jnp.moveaxis(a, axis, 0)


def _rows(a, lead=0):
    return a.reshape(a.shape[:lead] + (-1, a.shape[-1]))


def _core_major(a):
    _, r, c = a.shape
    return a.reshape(4, 2, r, c).transpose(1, 0, 2, 3).reshape(2, 4 * r, c)


def kernel(x, meta_tokens, norm_gains, a_w_in, a_lb_logits, a_head_norm, a_w_out, kv_norm, kv_w, fg_b, b_w_q, b_w_out, ffn_w_up, ffn_conv, ffn_w_down, loss_target, m_meta_tokens, m_norm_gains, m_a_w_in, m_a_lb_logits, m_a_head_norm, m_a_w_out, m_kv_norm, m_kv_w, m_fg_b, m_b_w_q, m_b_w_out, m_ffn_w_up, m_ffn_conv, m_ffn_w_down, v_meta_tokens, v_norm_gains, v_a_w_in, v_a_lb_logits, v_a_head_norm, v_a_w_out, v_kv_norm, v_kv_w, v_fg_b, v_b_w_q, v_b_w_out, v_ffn_w_up, v_ffn_conv, v_ffn_w_down):
    names = ("meta_tokens", "norm_gains", "a_w_in", "a_lb_logits", "a_head_norm", "a_w_out", "kv_norm", "kv_w", "fg_b",
             "b_w_q", "b_w_out", "ffn_w_up", "ffn_conv", "ffn_w_down")
    w = dict(zip(names, (meta_tokens, norm_gains, a_w_in, a_lb_logits, a_head_norm, a_w_out, kv_norm, kv_w, fg_b,
                         b_w_q, b_w_out, ffn_w_up, ffn_conv, ffn_w_down)))
    mom = dict(zip(names, (m_meta_tokens, m_norm_gains, m_a_w_in, m_a_lb_logits, m_a_head_norm, m_a_w_out, m_kv_norm,
                           m_kv_w, m_fg_b, m_b_w_q, m_b_w_out, m_ffn_w_up, m_ffn_conv, m_ffn_w_down)))
    var = dict(zip(names, (v_meta_tokens, v_norm_gains, v_a_w_in, v_a_lb_logits, v_a_head_norm, v_a_w_out, v_kv_norm,
                           v_kv_w, v_fg_b, v_b_w_q, v_b_w_out, v_ffn_w_up, v_ffn_conv, v_ffn_w_down)))

    bl_, seq, d = x.shape
    nm = meta_tokens.shape[0]
    t = nm + seq
    n = bl_ * t
    bh = fg_b.shape[0]
    hd = d // bh
    hp = LANES // hd
    ff = ffn_w_down.shape[1] * N_DEV
    tm = _div_tile(t, TOKEN_TILE_CAP)
    tc = _div_tile(t, 64)
    tn = 512

    vec_pack = _pack([w[k].reshape(-1) for k in _VECTORS], f32, LANES, 8)
    got = _all_gather([w[k].astype(bf16) for k in _MATRICES] + [vec_pack], "gather_weights")
    big = {k: _unshard(a, _SHARD_AXIS[k]) for k, a in zip(_MATRICES, got)}
    vec_segs = _unpack(got[-1].reshape(N_DEV, -1), [w[k].shape for k in _VECTORS])
    small = {k: _unshard(a, _SHARD_AXIS[k]) for k, a in zip(_VECTORS, vec_segs)}
    w_in, w_out_a = big["a_w_in"][0], big["a_w_out"][0]
    w_kv, w_zf = big["kv_w"][:, :2 * d], jnp.pad(big["kv_w"][:, 2 * d:], ((0, 0), (0, LANES - bh)))
    w_q, w_out_b = big["b_w_q"][0], big["b_w_out"][0]
    w_gate, w_val, w_down = big["ffn_w_up"][:, :, :ff], big["ffn_w_up"][:, :, ff:], big["ffn_w_down"]
    gains = small["norm_gains"]
    gain = lambda l, j: gains[l, j][None]
    cw_gate, cw_val = small["ffn_conv"][:, :, :ff], small["ffn_conv"][:, :, ff:]
    head_gain = small["a_head_norm"]
    lb = jax.nn.softmax(small["a_lb_logits"], axis=0)[0:1]
    kvn = kv_norm[None]
    fgb_pad = jnp.pad(fg_b, (0, LANES - bh))[None]

    h0 = jnp.concatenate([jnp.broadcast_to(small["meta_tokens"][None], (bl_, nm, d)), x], axis=1).reshape(n, d)

    def ffn_fwd(l, h_in):
        fi = _rms_fwd(h_in, gain(l, 2), tm, f"ffn{l}_norm")
        ug = _mm(fi, w_gate[l], bf16, tm, ff, f"ffn{l}_up_gate")
        uv = _mm(fi, w_val[l], bf16, tm, ff, f"ffn{l}_up_val")
        act = _conv_gate_fwd(ug, uv, cw_gate[l], cw_val[l], bl_, t, tc, f"ffn{l}_conv_gate")
        h_out, mix = _mm_norm_res(act, w_down[l], gain(l, 3), h_in, tm, f"ffn{l}_down")
        return h_out, (h_in, fi, ug, uv, act, mix)

    hn0 = _rms_fwd(h0, gain(0, 0), tm, "a_norm")
    pmat = _mm(hn0, w_in, f32, tm, tn, "a_in_proj")
    og, states = _gla_fwd(pmat, lb, head_gain, bl_, t, nm, "a_gla_fwd")
    h1, mix_a = _mm_norm_res(og, w_out_a, gain(0, 1), h0, tm, "a_out_proj")
    h2, ffn0 = ffn_fwd(0, h1)

    hk = _rms_fwd(h2, kvn, tm, "kv_norm")
    kvp = _mm(hk, w_kv, bf16, tm, tn, "kv_proj")
    zf, cum = _zf_c(hk, w_zf, fgb_pad, bl_, t, tm, "forget_cumsum")
    ck = _c_key_rows(cum, bl_, t, tm, bh, hp)
    hn1 = _rms_fwd(h2, gain(1, 0), tm, "b_norm")
    q = _mm(hn1, w_q, bf16, tm, tn, "b_q_proj")
    o, lse = _attn_fwd(q, kvp, ck, bl_, t, tm, hd, "b_attn_fwd")
    h3, mix_b = _mm_norm_res(o, w_out_b, gain(1, 1), h2, tm, "b_out_proj")
    h4, ffn1 = ffn_fwd(1, h3)

    target = jnp.concatenate([jnp.zeros((bl_, nm, d), f32), loss_target], axis=1).reshape(n, d)
    loss8, dh = _loss_head(h4, target, t, nm, tm, "loss_head")
    loss = lax.psum(loss8[0, 0], ("x", "y", "c"))

    dgain = {}

    def ffn_bwd(l, saved, dh_out):
        h_in, fi, ug, uv, act, mix = saved
        dmix, dgain[l, 3] = _rms_bwd(mix, gain(l, 3), dh_out, None, bf16, tm, f"ffn{l}_down_norm_bwd")
        dact = _mm_nt([(dmix, w_down[l])], bf16, tm, ff, f"ffn{l}_down_dx")
        dw_down = _mm_tn(act, dmix, tm, ff, tn, f"ffn{l}_down_dw")
        dug, duv, dcg, dcv = _conv_gate_bwd(ug, uv, cw_gate[l], cw_val[l], dact, bl_, t, tc, f"ffn{l}_conv_gate_bwd")
        dfi = _mm_nt([(dug, w_gate[l]), (duv, w_val[l])], bf16, tm, 256, f"ffn{l}_up_dx")
        dw_up = jnp.concatenate([_mm_tn(fi, dug, tm, tn, ff, f"ffn{l}_up_gate_dw"),
                                 _mm_tn(fi, duv, tm, tn, ff, f"ffn{l}_up_val_dw")], axis=1)
        dh_in, dgain[l, 2] = _rms_bwd(h_in, gain(l, 2), dfi, dh_out, f32, tm, f"ffn{l}_norm_bwd")
        return dh_in, dw_up, jnp.concatenate([dcg, dcv], axis=1), dw_down

    dh, dw_up1, dconv1, dw_down1 = ffn_bwd(1, ffn1, dh)

    dmix, dgain[1, 1] = _rms_bwd(mix_b, gain(1, 1), dh, None, bf16, tm, "b_out_norm_bwd")
    do = _mm_nt([(dmix, w_out_b)], bf16, tm, tn, "b_out_dx")
    dw_out_b = _mm_tn(o, dmix, tm, tn, tn, "b_out_dw")
    dq, dk, dv, dck, dcq = _attn_bwd(q, kvp, o, do, lse, ck, bl_, t, tm, hd, "b_attn_bwd")
    dhn1 = _mm_nt([(dq, w_q)], bf16, tm, tn, "b_q_dx")
    dw_q = _mm_tn(hn1, dq, tm, tn, tn, "b_q_dw")
    dh, dgain[1, 0] = _rms_bwd(h2, gain(1, 0), dhn1, dh, f32, tm, "b_norm_bwd")

    dzf, dfgb = _c_bwd(_dc_rows(dck, dcq, bl_, t, bh), zf, bl_, t, tm, "forget_cumsum_bwd")
    dhk = _mm_nt([(dk, w_kv[:, :d]), (dv, w_kv[:, d:]), (dzf, w_zf)], bf16, tm, tn, "kv_dx")
    dw_kv = jnp.concatenate([_mm_tn(hk, dk, tm, tn, tn, "k_dw"), _mm_tn(hk, dv, tm, tn, tn, "v_dw"),
                             _mm_tn(hk, dzf, tm, tn, LANES, "zf_dw")[:, :bh]], axis=1)
    dh, dkvn = _rms_bwd(h2, kvn, dhk, dh, f32, tm, "kv_norm_bwd")

    dh, dw_up0, dconv0, dw_down0 = ffn_bwd(0, ffn0, dh)

    dmix, dgain[0, 1] = _rms_bwd(mix_a, gain(0, 1), dh, None, bf16, tm, "a_out_norm_bwd")
    dog = _mm_nt([(dmix, w_out_a)], bf16, tm, tn, "a_out_dx")
    dw_out_a = _mm_tn(og, dmix, tm, tn, tn, "a_out_dw")
    dpq, dpf, dpi, dpg, dlb, dhg = _gla_bwd(pmat, states, dog, lb, head_gain, bl_, t, nm, "a_gla_bwd")
    dps = (dpq, dpf, dpi, dpg)
    dhn0 = _mm_nt([(dp, w_in[:, j * d:(j + 1) * d]) for j, dp in enumerate(dps)], bf16, tm, tn, "a_in_dx")
    dw_in = jnp.concatenate([_mm_tn(hn0, dp, tm, tn, tn, f"a_in_dw{j}") for j, dp in enumerate(dps)], axis=1)
    dh, dgain[0, 0] = _rms_bwd(h0, gain(0, 0), dhn0, dh, f32, tm, "a_norm_bwd")

    dh = dh.reshape(bl_, t, d)
    grad_x = dh[:, nm:]
    dl0 = dlb * lb * (1.0 - lb)
    grads = dict(
        meta_tokens=jnp.sum(dh[:, :nm], axis=0),
        norm_gains=jnp.stack([jnp.concatenate([dgain[l, j] for j in range(4)], axis=0) for l in range(2)]),
        a_w_in=dw_in[None], a_lb_logits=jnp.concatenate([dl0, -dl0], axis=0), a_head_norm=dhg, a_w_out=dw_out_a[None],
        kv_w=dw_kv, b_w_q=dw_q[None], b_w_out=dw_out_b[None],
        ffn_w_up=jnp.stack([dw_up0, dw_up1]), ffn_conv=jnp.stack([dconv0, dconv1]),
        ffn_w_down=jnp.stack([dw_down0, dw_down1]))

    send = [_rows(_shard8(grads[k], _SHARD_AXIS[k]), 1).astype(bf16) for k in _MATRICES]
    send.append(_pack([_shard8(grads[k], _SHARD_AXIS[k]).reshape(N_DEV, -1) for k in _VECTORS], bf16, LANES, BF16_ROWS))
    send = [_core_major(a) for a in send]
    recv = _exchange(send, "c", "grad_exchange_cores")
    core = lax.axis_index("c").astype(jnp.int32).reshape(1)
    chip_sum = [_sum_own_recv(a, b, core, _div_tile(b.shape[0], _ROW_TILE_CAP), f"grad_sum_cores{i}")
                for i, (a, b) in enumerate(zip(send, recv))]
    parts = _exchange([a.reshape((4, -1) + a.shape[1:]) for a in chip_sum], "xy", "grad_exchange_chips")
    g_s, d_s, m_s, v_s = {}, {}, {}, {}

    def update(part, srcs, label):
        return _adamw(part, *srcs, _div_tile(part.shape[1], _ROW_TILE_CAP), label)

    for k, part in zip(_MATRICES, parts):
        res = update(part, [_rows(src[k]) for src in (w, mom, var)], f"adamw_{k}")
        g_s[k], d_s[k], m_s[k], v_s[k] = (r.reshape(w[k].shape) for r in res)
    vec_packs = [_pack([src[k].reshape(-1) for k in _VECTORS], f32, LANES, BF16_ROWS) for src in (w, mom, var)]
    vec_shapes = [w[k].shape for k in _VECTORS]
    for dst, r in zip((g_s, d_s, m_s, v_s), update(parts[-1], vec_packs, "adamw_vectors")):
        dst.update(zip(_VECTORS, _unpack(r.reshape(-1), vec_shapes)))

    rep_local = _pack([dkvn.reshape(-1), dfgb[0, :bh]], f32, LANES, 8)
    rep_parts = _all_gather([rep_local], "gather_replicated_grads")[0]
    rpacks = [_pack([src[k].reshape(-1) for k in _REPLICATED], f32, LANES, 8) for src in (w, mom, var)]
    rres = _adamw(rep_parts, *rpacks, rep_local.shape[0], "adamw_replicated")
    rshapes = [w[k].shape for k in _REPLICATED]
    g_r, d_r, m_r, v_r = ({k: a for k, a in zip(_REPLICATED, _unpack(r.reshape(-1), rshapes))} for r in rres)

    out = [loss, grad_x]
    for sh, rp in ((g_s, g_r), (d_s, d_r), (m_s, m_r), (v_s, v_r)):
        out += [sh[k] if k in sh else rp[k] for k in names]
    return tuple(out)
```

```python
import functools
import math

import jax
import jax.numpy as jnp
from jax import lax
from jax.experimental import pallas as pl
from jax.experimental.pallas import tpu as pltpu

f32 = jnp.float32
bf16 = jnp.bfloat16
SDS = jax.ShapeDtypeStruct

EPS = 1e-6
A_DK = 128
A_CHUNK = 64
GLA_GROUP = 4
TOKEN_TILE_CAP = 1024
LANES = 128
BF16_ROWS = 16
VMEM_LIMIT = 56 * 1024 * 1024
ADAM_LR, ADAM_B1, ADAM_B2, ADAM_EPS, ADAM_WD, ADAM_STEP = 0.001, 0.9, 0.999, 1e-08, 0.01, 10
N_DEV = 8
MESH = pl.DeviceIdType.MESH

_NT = (((1,), (1,)), ((), ()))
_TN = (((0,), (0,)), ((), ()))
_HI = lax.Precision.HIGHEST


def _params(**kw):
    return pltpu.CompilerParams(vmem_limit_bytes=VMEM_LIMIT, **kw)


def _div_tile(n, cap, mult=BF16_ROWS):
    best = None
    for t in range(mult, min(n, cap) + 1, mult):
        if n % t == 0:
            best = t
    assert best is not None, (n, cap, mult)
    return best


def _bdot(a, b):
    return jnp.dot(a.astype(bf16), b.astype(bf16), preferred_element_type=f32)


def _bdot_nt(a, b):
    return lax.dot_general(a.astype(bf16), b.astype(bf16), _NT, preferred_element_type=f32)


def _bdot_tn(a, b):
    return lax.dot_general(a.astype(bf16), b.astype(bf16), _TN, preferred_element_type=f32)


def _iota2(shape, axis):
    return lax.broadcasted_iota(jnp.int32, shape, axis)


def _cumsum_rows(x):
    n = x.shape[0]
    tri = (_iota2((n, n), 0) >= _iota2((n, n), 1)).astype(f32)
    return jnp.dot(tri, x, precision=_HI, preferred_element_type=f32)


def _revcumsum_rows(x):
    n = x.shape[0]
    tri = (_iota2((n, n), 1) >= _iota2((n, n), 0)).astype(f32)
    return jnp.dot(tri, x, precision=_HI, preferred_element_type=f32)


def _sigmoid(x):
    return 1.0 / (1.0 + jnp.exp(-x))


def _rms_fwd(x, g, tm, name):
    n, d = x.shape

    def body(x_ref, g_ref, o_ref):
        xv = x_ref[...]
        r = lax.rsqrt(jnp.mean(xv * xv, axis=-1, keepdims=True) + EPS)
        o_ref[...] = (xv * r * g_ref[...]).astype(o_ref.dtype)

    return pl.pallas_call(
        body, grid=(n // tm,), name=name,
        in_specs=[pl.BlockSpec((tm, d), lambda i: (i, 0)), pl.BlockSpec((1, d), lambda i: (0, 0))],
        out_specs=pl.BlockSpec((tm, d), lambda i: (i, 0)),
        out_shape=SDS((n, d), bf16), compiler_params=_params(),
    )(x, g)


def _mm(a, w, out_dtype, tm, tn, name):
    n, k = a.shape
    m = w.shape[1]

    def body(a_ref, w_ref, o_ref):
        o_ref[...] = _bdot(a_ref[...], w_ref[...]).astype(o_ref.dtype)

    return pl.pallas_call(
        body, grid=(m // tn, n // tm), name=name,
        in_specs=[pl.BlockSpec((tm, k), lambda j, i: (i, 0)), pl.BlockSpec((k, tn), lambda j, i: (0, j))],
        out_specs=pl.BlockSpec((tm, tn), lambda j, i: (i, j)),
        out_shape=SDS((n, m), out_dtype), compiler_params=_params(),
    )(a, w)


def _mm_norm_res(a, w, g, h, tm, name):
    n, k = a.shape
    d = w.shape[1]

    def body(a_ref, w_ref, g_ref, h_ref, hn_ref, mix_ref):
        mix = _bdot(a_ref[...], w_ref[...])
        r = lax.rsqrt(jnp.mean(mix * mix, axis=-1, keepdims=True) + EPS)
        mix_ref[...] = mix
        hn_ref[...] = h_ref[...] + mix * r * g_ref[...]

    return pl.pallas_call(
        body, grid=(n // tm,), name=name,
        in_specs=[pl.BlockSpec((tm, k), lambda i: (i, 0)), pl.BlockSpec((k, d), lambda i: (0, 0)),
                  pl.BlockSpec((1, d), lambda i: (0, 0)), pl.BlockSpec((tm, d), lambda i: (i, 0))],
        out_specs=[pl.BlockSpec((tm, d), lambda i: (i, 0)), pl.BlockSpec((tm, d), lambda i: (i, 0))],
        out_shape=[SDS((n, d), f32), SDS((n, d), f32)], compiler_params=_params(),
    )(a, w, g, h)


def _rms_bwd(x, g, dy, dh_in, out_dtype, tm, name):
    n, d = x.shape
    has_add = dh_in is not None

    def body(*refs):
        if has_add:
            x_ref, g_ref, dy_ref, dh_ref, o_ref, dg_ref = refs
        else:
            x_ref, g_ref, dy_ref, o_ref, dg_ref = refs
        xv = x_ref[...]
        dyv = dy_ref[...].astype(f32)
        r = lax.rsqrt(jnp.mean(xv * xv, axis=-1, keepdims=True) + EPS)
        xr = xv * r
        gdy = dyv * g_ref[...]
        dx = r * gdy - xr * (r * r) * jnp.mean(xv * gdy, axis=-1, keepdims=True)
        if has_add:
            dx = dx + dh_ref[...]
        o_ref[...] = dx.astype(o_ref.dtype)

        @pl.when(pl.program_id(0) == 0)
        def _():
            dg_ref[...] = jnp.zeros_like(dg_ref)

        dg_ref[...] += jnp.sum(dyv * xr, axis=0, keepdims=True)

    row = pl.BlockSpec((tm, d), lambda i: (i, 0))
    vec = pl.BlockSpec((1, d), lambda i: (0, 0))
    ins = [x, g, dy] + ([dh_in] if has_add else [])
    return pl.pallas_call(
        body, grid=(n // tm,), name=name,
        in_specs=[row, vec, row] + ([row] if has_add else []),
        out_specs=[row, vec],
        out_shape=[SDS((n, d), out_dtype), SDS((1, d), f32)], compiler_params=_params(),
    )(*ins)


def _mm_nt(pairs, out_dtype, tm, tk, name):
    n = pairs[0][0].shape[0]
    k = pairs[0][1].shape[0]
    np_ = len(pairs)

    def body(*refs):
        o_ref = refs[-1]
        acc = None
        for p in range(np_):
            t = _bdot_nt(refs[2 * p][...], refs[2 * p + 1][...])
            acc = t if acc is None else acc + t
        o_ref[...] = acc.astype(o_ref.dtype)

    in_specs, ins = [], []
    for dy, w in pairs:
        m = dy.shape[1]
        in_specs += [pl.BlockSpec((tm, m), lambda j, i: (i, 0)), pl.BlockSpec((tk, m), lambda j, i: (j, 0))]
        ins += [dy, w]
    return pl.pallas_call(
        body, grid=(k // tk, n // tm), name=name, in_specs=in_specs,
        out_specs=pl.BlockSpec((tm, tk), lambda j, i: (i, j)),
        out_shape=SDS((n, k), out_dtype), compiler_params=_params(),
    )(*ins)


def _mm_tn(x, dy, tm, tk, tn, name):
    n, k = x.shape
    m = dy.shape[1]

    def body(x_ref, dy_ref, o_ref):
        @pl.when(pl.program_id(2) == 0)
        def _():
            o_ref[...] = jnp.zeros_like(o_ref)

        o_ref[...] += _bdot_tn(x_ref[...], dy_ref[...])

    return pl.pallas_call(
        body, grid=(k // tk, m // tn, n // tm), name=name,
        in_specs=[pl.BlockSpec((tm, tk), lambda a, b, i: (i, a)), pl.BlockSpec((tm, tn), lambda a, b, i: (i, b))],
        out_specs=pl.BlockSpec((tk, tn), lambda a, b, i: (a, b)),
        out_shape=SDS((k, m), f32), compiler_params=_params(),
    )(x, dy)


def _split3(x):
    hi = x.astype(bf16)
    r = x - hi.astype(f32)
    mid = r.astype(bf16)
    return hi, mid, (r - mid.astype(f32)).astype(bf16)


def _mask_dot(mask, x):
    hi, mid, lo = _split3(x)
    dot = lambda p: jnp.dot(mask, p, preferred_element_type=f32)
    return dot(hi) + dot(mid) + dot(lo)


def _chunk_rows(parts, cl):
    tiles = [jnp.broadcast_to(p, (cl, p.shape[1])) for p in parts]
    return tiles[0] if len(tiles) == 1 else jnp.concatenate(tiles, axis=0)


def _cat(parts):
    return parts[0] if len(parts) == 1 else jnp.concatenate(parts, axis=0)


def _gla_group_fwd(qg, fg, vg, lb, st, nc, cl):
    g = nc * cl
    sg = _sigmoid(fg)
    f = lb + (1.0 - lb) * sg
    k = 1.0 - f
    row, col = _iota2((g, g), 0), _iota2((g, g), 1)
    chunk_of = lambda idx: sum((idx >= u * cl).astype(jnp.int32) for u in range(1, nc)) if nc > 1 else 0
    same = chunk_of(row) == chunk_of(col) if nc > 1 else None
    causal = row >= col if nc == 1 else jnp.logical_and(same, row >= col)
    anti = col >= row if nc == 1 else jnp.logical_and(same, col >= row)
    b = _mask_dot(causal.astype(bf16), jnp.log(f))
    bls = [b[(u + 1) * cl - 1:(u + 1) * cl, :] for u in range(nc)]
    ebls = [jnp.exp(x) for x in bls]
    e = jnp.exp(b)
    ei = jnp.exp(-b)
    eo = jnp.exp(_chunk_rows(bls, cl) - b)
    qi, ki, ko = qg * e, k * ei, k * eo
    att = jnp.where(causal, _bdot_nt(qi, ki), 0.0)
    o_intra = _bdot(att, vg)
    sl = [slice(u * cl, (u + 1) * cl) for u in range(nc)]
    ds = [_bdot_tn(vg[s], ko[s]) for s in sl]
    sts = [st]
    for u in range(nc):
        sts.append(sts[u] * ebls[u] + ds[u])
    o = o_intra + _cat([_bdot_nt(qi[sl[u]], sts[u]) for u in range(nc)])
    return dict(sg=sg, f=f, e=e, ei=ei, eo=eo, ebls=ebls, qi=qi, ki=ki, ko=ko, att=att, o=o, sts=sts, causal=causal,
                anti=anti, sl=sl)


def _gla_group(nreal, want):
    while nreal % want:
        want //= 2
    return max(want, 1)


def _head_out(o, ggc, hg):
    r = lax.rsqrt(jnp.mean(o * o, axis=-1, keepdims=True) + EPS)
    return o * r * hg * (ggc * _sigmoid(ggc))


def _gla_fwd(pmat, lb, hg, bl_, t, nm, name):
    n, d4 = pmat.shape
    d = d4 // 4
    nh = d // A_DK
    nreal = (t - nm) // A_CHUNK
    nch = nreal + 1
    un = _gla_group(nreal, GLA_GROUP)

    def body(q_ref, f_ref, i_ref, gg_ref, lb_ref, hg_ref, og_ref, ss_ref):
        lbv, hgv = lb_ref[...], hg_ref[...]

        def run(rows, st, idx, nc, cl):
            w = _gla_group_fwd(q_ref[rows, :], f_ref[rows, :], i_ref[rows, :], lbv, st, nc, cl)
            for u in range(nc):
                ss_ref[0, idx + u] = w["sts"][u]
            og_ref[rows, :] = _head_out(w["o"], gg_ref[rows, :], hgv).astype(og_ref.dtype)
            return w["sts"][nc]

        st = run(pl.ds(0, nm), jnp.zeros((A_DK, A_DK), f32), 0, 1, nm)

        def step(it, st):
            rows = pl.ds(pl.multiple_of(nm + it * (un * A_CHUNK), BF16_ROWS), un * A_CHUNK)
            return run(rows, st, 1 + it * un, un, A_CHUNK)

        lax.fori_loop(0, nreal // un, step, st)

    col = lambda o: pl.BlockSpec((t, A_DK), lambda b, h: (b, o * nh + h))
    vec = pl.BlockSpec((1, A_DK), lambda b, h: (0, h))
    return pl.pallas_call(
        body, grid=(bl_, nh), name=name,
        in_specs=[col(0), col(1), col(2), col(3), vec, vec],
        out_specs=[pl.BlockSpec((t, A_DK), lambda b, h: (b, h)),
                   pl.BlockSpec((1, nch, A_DK, A_DK), lambda b, h: (b * nh + h, 0, 0, 0))],
        out_shape=[SDS((n, d), bf16), SDS((bl_ * nh, nch, A_DK, A_DK), f32)], compiler_params=_params(),
    )(pmat, pmat, pmat, pmat, lb, hg)


def _gla_bwd(pmat, ss, dog, lb, hg, bl_, t, nm, name):
    n, d4 = pmat.shape
    d = d4 // 4
    nh = d // A_DK
    nreal = (t - nm) // A_CHUNK
    nch = nreal + 1
    un = _gla_group(nreal, GLA_GROUP)

    def body(q_ref, f_ref, i_ref, gg_ref, ss_ref, dog_ref, lb_ref, hg_ref,
             dq_ref, df_ref, di_ref, dgg_ref, dlb_ref, dhg_ref):
        lbv, hgv = lb_ref[...], hg_ref[...]

        def run(rows, idx, carry, nc, cl):
            dst, dlb, dhg = carry
            qg, fg, vg, ggc = q_ref[rows, :], f_ref[rows, :], i_ref[rows, :], gg_ref[rows, :]
            dogc = dog_ref[rows, :].astype(f32)
            w = _gla_group_fwd(qg, fg, vg, lbv, ss_ref[0, idx], nc, cl)
            o, qi, ki, ko, sl, sts, ebls = w["o"], w["qi"], w["ki"], w["ko"], w["sl"], w["sts"], w["ebls"]
            r = lax.rsqrt(jnp.mean(o * o, axis=-1, keepdims=True) + EPS)
            sgg = _sigmoid(ggc)
            sil = ggc * sgg
            on = o * r
            dhg = dhg + jnp.sum(dogc * sil * on, axis=0, keepdims=True)
            dgg_ref[rows, :] = (dogc * on * hgv * (sgg * (1.0 + ggc * (1.0 - sgg)))).astype(dgg_ref.dtype)
            tt = dogc * sil * hgv
            do = r * tt - on * (r * r) * jnp.mean(o * tt, axis=-1, keepdims=True)
            xs = [_bdot_tn(do[s], qi[s]) for s in sl]
            dsts = [None] * nc + [dst]
            for u in reversed(range(nc)):
                dsts[u] = dsts[u + 1] * ebls[u] + xs[u]
            datt = jnp.where(w["causal"], _bdot_nt(do, vg), 0.0)
            dv = _bdot_tn(w["att"], do) + _cat([_bdot_nt(ko[sl[u]], dsts[u + 1]) for u in range(nc)])
            dko = _cat([_bdot(vg[sl[u]], dsts[u + 1]) for u in range(nc)])
            dqi = _bdot(datt, ki) + _cat([_bdot(do[sl[u]], sts[u]) for u in range(nc)])
            dki = _bdot_tn(datt, qi)
            dk = dki * w["ei"] + dko * w["eo"]
            dkoko = dko * ko
            db = dqi * qi - dki * ki - dkoko
            rowi = _iota2(db.shape, 0)
            for u in range(nc):
                d_ebl = jnp.sum(dsts[u + 1] * sts[u], axis=0, keepdims=True)
                dbl = jnp.sum(dkoko[sl[u]], axis=0, keepdims=True) + d_ebl * ebls[u]
                db = db + jnp.where(rowi == (u + 1) * cl - 1, dbl, 0.0)
            dlogf = _mask_dot(w["anti"].astype(bf16), db)
            df = dlogf / w["f"] - dk
            sg = w["sg"]
            dq_ref[rows, :] = (dqi * w["e"]).astype(dq_ref.dtype)
            df_ref[rows, :] = (df * (1.0 - lbv) * sg * (1.0 - sg)).astype(df_ref.dtype)
            di_ref[rows, :] = dv.astype(di_ref.dtype)
            dlb = dlb + jnp.sum(df * (1.0 - sg), axis=0, keepdims=True)
            return dsts[0], dlb, dhg

        zero = jnp.zeros((1, A_DK), f32)
        ngroups = nreal // un

        def step(it, carry):
            grp = ngroups - 1 - it
            rows = pl.ds(pl.multiple_of(nm + grp * (un * A_CHUNK), BF16_ROWS), un * A_CHUNK)
            return run(rows, 1 + grp * un, carry, un, A_CHUNK)

        carry = lax.fori_loop(0, ngroups, step, (jnp.zeros((A_DK, A_DK), f32), zero, zero))
        _, dlb, dhg = run(pl.ds(0, nm), 0, carry, 1, nm)

        @pl.when(pl.program_id(1) == 0)
        def _():
            dlb_ref[...] = jnp.zeros_like(dlb_ref)
            dhg_ref[...] = jnp.zeros_like(dhg_ref)

        dlb_ref[...] += dlb
        dhg_ref[...] += dhg

    col = lambda o: pl.BlockSpec((t, A_DK), lambda h, b: (b, o * nh + h))
    blk = pl.BlockSpec((t, A_DK), lambda h, b: (b, h))
    vec = pl.BlockSpec((1, A_DK), lambda h, b: (0, h))
    return pl.pallas_call(
        body, grid=(nh, bl_), name=name,
        in_specs=[col(0), col(1), col(2), col(3),
                  pl.BlockSpec((1, nch, A_DK, A_DK), lambda h, b: (b * nh + h, 0, 0, 0)), blk, vec, vec],
        out_specs=[blk, blk, blk, blk, vec, vec],
        out_shape=[SDS((n, d), bf16)] * 4 + [SDS((1, d), f32)] * 2, compiler_params=_params(),
    )(pmat, pmat, pmat, pmat, ss, dog, lb, hg)


def _shift_down(x, prev2, s):
    row = _iota2(x.shape, 0)
    y = pltpu.roll(x, s, 0)
    if s == 1:
        return jnp.where(row == 0, prev2[1:2, :], y)
    return jnp.where(row == 0, prev2[0:1, :], jnp.where(row == 1, prev2[1:2, :], y))


def _shift_up(x, next2, s):
    n = x.shape[0]
    row = _iota2(x.shape, 0)
    y = pltpu.roll(x, n - s, 0)
    if s == 1:
        return jnp.where(row == n - 1, next2[0:1, :], y)
    return jnp.where(row == n - 1, next2[1:2, :], jnp.where(row == n - 2, next2[0:1, :], y))


def _conv3(x, prev2, w):
    return w[0:1, :] * _shift_down(x, prev2, 2) + w[1:2, :] * _shift_down(x, prev2, 1) + w[2:3, :] * x


def _conv_gate_fwd(ug, uv, cwg, cwv, bl_, t, tc, name):
    n, ff = ug.shape
    nt = t // tc

    def body(ug_ref, uv_ref, wg_ref, wv_ref, a_ref, hg_ref, hv_ref):
        @pl.when(pl.program_id(1) == 0)
        def _():
            hg_ref[...] = jnp.zeros_like(hg_ref)
            hv_ref[...] = jnp.zeros_like(hv_ref)

        xg = ug_ref[...].astype(f32)
        xv = uv_ref[...].astype(f32)
        cg = _conv3(xg, hg_ref[...], wg_ref[...])
        cv = _conv3(xv, hv_ref[...], wv_ref[...])
        a_ref[...] = (cg * _sigmoid(cg) * cv).astype(a_ref.dtype)
        hg_ref[...] = xg[tc - 2:tc, :]
        hv_ref[...] = xv[tc - 2:tc, :]

    row = pl.BlockSpec((tc, ff), lambda b, i: (b * nt + i, 0))
    wsp = pl.BlockSpec((3, ff), lambda b, i: (0, 0))
    return pl.pallas_call(
        body, grid=(bl_, nt), name=name, in_specs=[row, row, wsp, wsp], out_specs=row,
        out_shape=SDS((n, ff), bf16),
        scratch_shapes=[pltpu.VMEM((2, ff), f32), pltpu.VMEM((2, ff), f32)], compiler_params=_params(),
    )(ug, uv, cwg, cwv)


def _conv_gate_bwd(ug, uv, cwg, cwv, da, bl_, t, tc, name):
    n, ff = ug.shape
    nt = t // tc
    per = tc // BF16_ROWS

    def body(ug_ref, uv_ref, pg_ref, pv_ref, wg_ref, wv_ref, da_ref, dug_ref, duv_ref, dwg_ref, dwv_ref, ng_ref, nv_ref):
        first = jnp.logical_and(pl.program_id(0) == 0, pl.program_id(1) == 0)

        @pl.when(first)
        def _():
            dwg_ref[...] = jnp.zeros_like(dwg_ref)
            dwv_ref[...] = jnp.zeros_like(dwv_ref)

        @pl.when(pl.program_id(1) == 0)
        def _():
            ng_ref[...] = jnp.zeros_like(ng_ref)
            nv_ref[...] = jnp.zeros_like(nv_ref)

        seq_start = pl.program_id(1) == nt - 1
        dav = da_ref[...].astype(f32)

        def half(u_ref, p_ref, w_ref):
            x = u_ref[...].astype(f32)
            prev2 = jnp.where(seq_start, 0.0, p_ref[BF16_ROWS - 2:BF16_ROWS, :].astype(f32))
            x1, x2 = _shift_down(x, prev2, 1), _shift_down(x, prev2, 2)
            w = w_ref[...]
            return x, x1, x2, w[0:1, :] * x2 + w[1:2, :] * x1 + w[2:3, :] * x

        xg, xg1, xg2, cg = half(ug_ref, pg_ref, wg_ref)
        xv, xv1, xv2, cv = half(uv_ref, pv_ref, wv_ref)
        sg = _sigmoid(cg)
        dcg = dav * cv * (sg * (1.0 + cg * (1.0 - sg)))
        dcv = dav * (cg * sg)

        def back(dc, x, x1, x2, w_ref, nx_ref, du_ref, dw_ref):
            w = w_ref[...]
            nx = nx_ref[...]
            du = w[2:3, :] * dc + w[1:2, :] * _shift_up(dc, nx, 1) + w[0:1, :] * _shift_up(dc, nx, 2)
            du_ref[...] = du.astype(du_ref.dtype)
            dw_ref[0:1, :] += jnp.sum(dc * x2, axis=0, keepdims=True)
            dw_ref[1:2, :] += jnp.sum(dc * x1, axis=0, keepdims=True)
            dw_ref[2:3, :] += jnp.sum(dc * x, axis=0, keepdims=True)
            nx_ref[...] = dc[0:2, :]

        back(dcg, xg, xg1, xg2, wg_ref, ng_ref, dug_ref, dwg_ref)
        back(dcv, xv, xv1, xv2, wv_ref, nv_ref, duv_ref, dwv_ref)

    row = pl.BlockSpec((tc, ff), lambda b, i: (b * nt + nt - 1 - i, 0))
    prev = pl.BlockSpec((BF16_ROWS, ff), lambda b, i: (jnp.maximum((b * nt + nt - 1 - i) * per - 1, 0), 0))
    wsp = pl.BlockSpec((3, ff), lambda b, i: (0, 0))
    return pl.pallas_call(
        body, grid=(bl_, nt), name=name, in_specs=[row, row, prev, prev, wsp, wsp, row],
        out_specs=[row, row, wsp, wsp],
        out_shape=[SDS((n, ff), bf16), SDS((n, ff), bf16), SDS((3, ff), f32), SDS((3, ff), f32)],
        scratch_shapes=[pltpu.VMEM((2, ff), f32), pltpu.VMEM((2, ff), f32)], compiler_params=_params(),
    )(ug, uv, ug, uv, cwg, cwv, da)


def _zf_c(hk, wzf, fgb, bl_, t, tm, name):
    n, d = hk.shape
    nt = t // tm

    def body(hk_ref, w_ref, b_ref, zf_ref, c_ref, carry_ref):
        @pl.when(pl.program_id(1) == 0)
        def _():
            carry_ref[...] = jnp.zeros_like(carry_ref)

        z = _bdot(hk_ref[...], w_ref[...]) + b_ref[...]
        ls = jnp.minimum(z, 0.0) - jnp.log(1.0 + jnp.exp(-jnp.abs(z)))
        c = _cumsum_rows(ls) + carry_ref[...]
        zf_ref[...] = z
        c_ref[...] = c
        carry_ref[...] = c[tm - 1:tm, :]

    row = lambda w: pl.BlockSpec((tm, w), lambda b, i: (b * nt + i, 0))
    return pl.pallas_call(
        body, grid=(bl_, nt), name=name,
        in_specs=[row(d), pl.BlockSpec((d, LANES), lambda b, i: (0, 0)), pl.BlockSpec((1, LANES), lambda b, i: (0, 0))],
        out_specs=[row(LANES), row(LANES)],
        out_shape=[SDS((n, LANES), f32), SDS((n, LANES), f32)],
        scratch_shapes=[pltpu.VMEM((1, LANES), f32)], compiler_params=_params(),
    )(hk, wzf, fgb)


def _c_bwd(dc, zf, bl_, t, tm, name):
    n = dc.shape[0]
    nt = t // tm

    def body(dc_ref, zf_ref, dzf_ref, dfg_ref, carry_ref):
        @pl.when(jnp.logical_and(pl.program_id(0) == 0, pl.program_id(1) == 0))
        def _():
            dfg_ref[...] = jnp.zeros_like(dfg_ref)

        @pl.when(pl.program_id(1) == 0)
        def _():
            carry_ref[...] = jnp.zeros_like(carry_ref)

        rc = _revcumsum_rows(dc_ref[...]) + carry_ref[...]
        dz = rc * _sigmoid(-zf_ref[...])
        dzf_ref[...] = dz.astype(dzf_ref.dtype)
        dfg_ref[...] += jnp.sum(dz, axis=0, keepdims=True)
        carry_ref[...] = rc[0:1, :]

    row = pl.BlockSpec((tm, LANES), lambda b, i: (b * nt + nt - 1 - i, 0))
    vec = pl.BlockSpec((1, LANES), lambda b, i: (0, 0))
    return pl.pallas_call(
        body, grid=(bl_, nt), name=name, in_specs=[row, row], out_specs=[row, vec],
        out_shape=[SDS((n, LANES), bf16), SDS((1, LANES), f32)],
        scratch_shapes=[pltpu.VMEM((1, LANES), f32)], compiler_params=_params(),
    )(dc, zf)


def _is_pow2(x):
    m, _ = math.frexp(x)
    return m == 0.5


def _prescale(qh, scale):
    return (qh.astype(f32) * scale).astype(bf16)


def _attn_fwd(q, kv, ck, bl_, t, tq, hd, name):
    n, d = q.shape
    npair = d // LANES
    hp = LANES // hd
    nq = t // tq
    scale = 1.0 / (hd ** 0.5)

    pre = _is_pow2(scale)

    def body(q_ref, k_ref, v_ref, ck_ref, o_ref, lse_ref):
        i = pl.program_id(2)
        diag = _iota2((tq, tq), 0) >= _iota2((tq, tq), 1)
        for hh in range(hp):
            lanes = slice(hh * hd, (hh + 1) * hd)
            qh = _prescale(q_ref[:, lanes], scale) if pre else q_ref[:, lanes]

            def block(j, carry, masked, lanes=lanes, qh=qh, hh=hh):
                m, l, acc = carry
                rows = pl.ds(pl.multiple_of(j * tq, BF16_ROWS), tq)
                s = _bdot_nt(qh, k_ref[rows, lanes])
                s = (s if pre else s * scale) - ck_ref[0, 0, j, hh:hh + 1, :]
                if masked:
                    s = jnp.where(diag, s, -1e30)
                m2 = jnp.maximum(m, jnp.max(s, axis=-1, keepdims=True))
                p = jnp.exp(s - m2)
                a = jnp.exp(m - m2)
                return m2, a * l + jnp.sum(p, axis=-1, keepdims=True), a * acc + _bdot(p, v_ref[rows, lanes])

            init = (jnp.full((tq, 1), -1e30, f32), jnp.zeros((tq, 1), f32), jnp.zeros((tq, hd), f32))
            carry = lax.fori_loop(0, i, functools.partial(block, masked=False), init)
            m, l, acc = block(i, carry, True)
            o_ref[:, lanes] = (acc / l).astype(o_ref.dtype)
            lse_ref[:, lanes] = jnp.broadcast_to(m + jnp.log(l), (tq, hd))

    nk = nq
    return pl.pallas_call(
        body, grid=(bl_, npair, nq), name=name,
        in_specs=[pl.BlockSpec((tq, LANES), lambda b, p, i: (b * nq + i, p)),
                  pl.BlockSpec((t, LANES), lambda b, p, i: (b, p)),
                  pl.BlockSpec((t, LANES), lambda b, p, i: (b, npair + p)),
                  pl.BlockSpec((1, 1, nk, hp, tq), lambda b, p, i: (b, p, 0, 0, 0))],
        out_specs=[pl.BlockSpec((tq, LANES), lambda b, p, i: (b * nq + i, p)),
                   pl.BlockSpec((tq, LANES), lambda b, p, i: (b * nq + i, p))],
        out_shape=[SDS((n, d), f32), SDS((n, d), f32)], compiler_params=_params(),
    )(q, kv, kv, ck)


def _attn_bwd(q, kv, o, do, lse, ck, bl_, t, tq, hd, name):
    n, d = q.shape
    npair = d // LANES
    hp = LANES // hd
    nq = t // tq
    scale = 1.0 / (hd ** 0.5)

    pre = _is_pow2(scale)

    def body(q_ref, k_ref, v_ref, o_ref, do_ref, lse_ref, ck_ref, dq_ref, dk_ref, dv_ref, dck_ref, dcq_ref):
        j = pl.program_id(2)

        @pl.when(j == 0)
        def _():
            dq_ref[...] = jnp.zeros_like(dq_ref)
            dcq_ref[...] = jnp.zeros_like(dcq_ref)

        diag = _iota2((tq, tq), 0) >= _iota2((tq, tq), 1)
        for hh in range(hp):
            lanes = slice(hh * hd, (hh + 1) * hd)
            kh = k_ref[:, lanes]
            vh = v_ref[:, lanes]
            cs = ck_ref[0, 0, 0, hh:hh + 1, :]

            def block(i, carry, masked, lanes=lanes, kh=kh, vh=vh, cs=cs, hh=hh):
                dk, dv, dcs = carry
                rows = pl.ds(pl.multiple_of(i * tq, BF16_ROWS), tq)
                qh = _prescale(q_ref[rows, lanes], scale) if pre else q_ref[rows, lanes]
                doh = do_ref[rows, lanes]
                s = _bdot_nt(qh, kh)
                s = (s if pre else s * scale) - cs
                if masked:
                    s = jnp.where(diag, s, -1e30)
                p = jnp.exp(s - lse_ref[rows, hh * hd:hh * hd + 1])
                delta = jnp.sum(doh.astype(f32) * o_ref[rows, lanes].astype(f32), axis=-1, keepdims=True)
                ds = p * (_bdot_nt(doh, vh) - delta)
                dq_ref[rows, lanes] += _bdot(ds, kh) * scale
                dcq_ref[0, rows, hh:hh + 1] += jnp.sum(ds, axis=-1, keepdims=True)
                dkq = _bdot_tn(ds, qh)
                return (dk + (dkq if pre else dkq * scale), dv + _bdot_tn(p, doh), dcs - jnp.sum(ds, axis=0, keepdims=True))

            init = (jnp.zeros((tq, hd), f32), jnp.zeros((tq, hd), f32), jnp.zeros((1, tq), f32))
            dk, dv, dcs = lax.fori_loop(j + 1, nq, functools.partial(block, masked=False), block(j, init, True))
            dk_ref[:, lanes] = dk.astype(dk_ref.dtype)
            dv_ref[:, lanes] = dv.astype(dv_ref.dtype)
            dck_ref[0, 0, 0, hh:hh + 1, :] = dcs

    whole = lambda c0: pl.BlockSpec((t, LANES), lambda b, p, j: (b, c0 + p))
    tile = lambda c0: pl.BlockSpec((tq, LANES), lambda b, p, j: (b * nq + j, c0 + p))
    ckspec = pl.BlockSpec((1, 1, 1, hp, tq), lambda b, p, j: (b, p, j, 0, 0))
    cqspec = pl.BlockSpec((1, t, hp), lambda b, p, j: (p, b, 0))
    return pl.pallas_call(
        body, grid=(bl_, npair, nq), name=name,
        in_specs=[whole(0), tile(0), tile(npair), whole(0), whole(0), whole(0), ckspec],
        out_specs=[whole(0), tile(0), tile(0), ckspec, cqspec],
        out_shape=[SDS((n, d), f32), SDS((n, d), bf16), SDS((n, d), bf16), SDS((bl_, npair, nq, hp, tq), f32),
                   SDS((npair, n, hp), f32)],
        compiler_params=_params(),
    )(q, kv, kv, o, do, lse, ck)


def _loss_head(h, target, t, nm, tm, name):
    n, d = h.shape
    nt = t // tm

    def body(h_ref, t_ref, loss_ref, dh_ref):
        i = pl.program_id(0)

        @pl.when(i == 0)
        def _():
            loss_ref[...] = jnp.zeros_like(loss_ref)

        pos = (i % nt) * tm + _iota2((tm, d), 0)
        err = jnp.where(pos >= nm, h_ref[...] - t_ref[...], 0.0)
        dh_ref[...] = err * (1.0 / d)
        loss_ref[...] += 0.5 * jnp.sum(jnp.mean(err * err, axis=-1, keepdims=True))

    row = pl.BlockSpec((tm, d), lambda i: (i, 0))
    return pl.pallas_call(
        body, grid=(n // tm,), name=name, in_specs=[row, row],
        out_specs=[pl.BlockSpec((8, LANES), lambda i: (0, 0)), row],
        out_shape=[SDS((8, LANES), f32), SDS((n, d), f32)], compiler_params=_params(),
    )(h, target)


def _c_key_rows(c, bl_, t, tq, bh, hp):
    npair = bh // hp
    nk = t // tq
    return c[:, :bh].reshape(bl_, nk, tq, npair, hp).transpose(0, 3, 1, 4, 2)


def _dc_rows(dck, dcq, bl_, t, bh):
    d = dck.transpose(0, 2, 4, 1, 3).reshape(bl_ * t, bh) + dcq.transpose(1, 0, 2).reshape(bl_ * t, bh)
    return jnp.pad(d, ((0, 0), (0, LANES - bh)))


_ANY = pl.BlockSpec(memory_space=pl.ANY)


def _all_gather(xs, name):
    na = len(xs)

    def body(*refs):
        x_refs, out_refs = refs[:na], refs[na:2 * na]
        send_sems, recv_sems, local_sems = refs[2 * na:]
        mx, my, mc = lax.axis_index("x"), lax.axis_index("y"), lax.axis_index("c")
        me, sibling = (mx, my, mc), (mx, my, 1 - mc)
        chips = [(1 - mx, my), (mx, 1 - my), (1 - mx, 1 - my)]

        def copy(a, k, block, to, own=False):
            px, py, pc = block
            rows = out_refs[a].at[4 * px + 2 * py + pc]
            return pltpu.make_async_remote_copy(
                src_ref=x_refs[a] if own else rows, dst_ref=rows,
                send_sem=send_sems.at[a, k], recv_sem=recv_sems.at[a, k], device_id=to, device_id_type=MESH)

        arrays = range(na)
        mine = [pltpu.make_async_copy(x_refs[a], out_refs[a].at[4 * mx + 2 * my + mc], local_sems.at[a]) for a in arrays]
        for cp in mine:
            cp.start()
        first = [copy(a, 1 + j, me, (*chip, mc), own=True) for j, chip in enumerate(chips) for a in arrays]
        first += [copy(a, 0, me, sibling, own=True) for a in arrays]
        for cp in first:
            cp.start()
        passed = []
        for j, chip in enumerate(chips):
            for a in arrays:
                copy(a, 1 + j, (*chip, mc), me).wait_recv()
                cp = copy(a, 4 + j, (*chip, mc), sibling)
                cp.start()
                passed.append(cp)
        for a in arrays:
            copy(a, 0, sibling, me).wait_recv()
        for j, chip in enumerate(chips):
            for a in arrays:
                copy(a, 4 + j, (*chip, 1 - mc), me).wait_recv()
        for cp in first + passed:
            cp.wait_send()
        for cp in mine:
            cp.wait()

    return pl.pallas_call(
        body, name=name, out_shape=[SDS((N_DEV,) + x.shape, x.dtype) for x in xs],
        in_specs=[_ANY] * na, out_specs=[_ANY] * na,
        scratch_shapes=[pltpu.SemaphoreType.DMA((na, 7)), pltpu.SemaphoreType.DMA((na, 7)), pltpu.SemaphoreType.DMA((na,))],
    )(*xs)


def _exchange(bufs, group, name):
    na = len(bufs)
    g = 2 if group == "c" else 4
    assert all(b.shape[0] == g for b in bufs)
    keep_own = g > 2

    def body(*refs):
        buf_refs, out_refs = refs[:na], refs[na:2 * na]
        send_sems, recv_sems, local_sems = refs[2 * na:]
        mx, my, mc = lax.axis_index("x"), lax.axis_index("y"), lax.axis_index("c")
        if group == "c":
            me = mc
            peers = [((mx, my, 1 - mc), 1 - mc)]
        else:
            me = 2 * mx + my
            peers = []
            for r in range(1, 4):
                px = 1 - mx if r & 2 else mx
                py = 1 - my if r & 1 else my
                peers.append(((px, py, mc), 2 * px + py))

        def copy(a, k, dev, src_row, dst_row):
            return pltpu.make_async_remote_copy(
                src_ref=buf_refs[a].at[src_row], dst_ref=out_refs[a].at[dst_row] if keep_own else out_refs[a],
                send_sem=send_sems.at[a, k], recv_sem=recv_sems.at[a, k], device_id=dev, device_id_type=MESH)

        sends = [copy(a, k, dev, idx, me) for k, (dev, idx) in enumerate(peers) for a in range(na)]
        for cp in sends:
            cp.start()
        own = []
        if keep_own:
            own = [pltpu.make_async_copy(buf_refs[a].at[me], out_refs[a].at[me], local_sems.at[a]) for a in range(na)]
        for cp in own:
            cp.start()
        for k, (dev, idx) in enumerate(peers):
            for a in range(na):
                copy(a, k, dev, idx, idx).wait_recv()
        for cp in sends:
            cp.wait_send()
        for cp in own:
            cp.wait()

    return pl.pallas_call(
        body, name=name, out_shape=[SDS(b.shape if keep_own else b.shape[1:], b.dtype) for b in bufs],
        in_specs=[_ANY] * na, out_specs=[_ANY] * na,
        scratch_shapes=[pltpu.SemaphoreType.DMA((na, g - 1)), pltpu.SemaphoreType.DMA((na, g - 1)),
                        pltpu.SemaphoreType.DMA((na,))],
    )(*bufs)


def _sum_own_recv(buf, recv, core, tr, name):
    _, r, c = buf.shape

    def body(core_ref, own_ref, recv_ref, o_ref):
        o_ref[...] = (own_ref[0].astype(f32) + recv_ref[...].astype(f32)).astype(o_ref.dtype)

    row = pl.BlockSpec((tr, c), lambda i, core_ref: (i, 0))
    return pl.pallas_call(
        body, name=name, out_shape=SDS((r, c), buf.dtype), compiler_params=_params(),
        grid_spec=pltpu.PrefetchScalarGridSpec(
            num_scalar_prefetch=1, grid=(r // tr,),
            in_specs=[pl.BlockSpec((1, tr, c), lambda i, core_ref: (core_ref[0], i, 0)), row], out_specs=row),
    )(core, buf, recv)


_HBM = pl.BlockSpec(memory_space=pltpu.HBM)
_SEM = pl.BlockSpec(memory_space=pltpu.SEMAPHORE)
_DATAFLOW = pltpu.SideEffectType.DATAFLOW_SIDE_EFFECTING
N_PEERS = N_DEV - 1


def _device_index():
    return 4 * lax.axis_index("x") + 2 * lax.axis_index("y") + lax.axis_index("c")


def _peers():
    mx, my, mc = lax.axis_index("x"), lax.axis_index("y"), lax.axis_index("c")
    peers = []
    for r in (2, 3, 4, 5, 6, 7, 1):
        px = 1 - mx if r & 4 else mx
        py = 1 - my if r & 2 else my
        pc = 1 - mc if r & 1 else mc
        peers.append(((px, py, pc), 4 * px + 2 * py + pc))
    return 4 * mx + 2 * my + mc, peers


def _push_copy(src_ref, land_ref, send_sems, recv_sems, a, k, dev, src_row, land_row, scatter):
    return pltpu.make_async_remote_copy(
        src_ref=src_ref.at[src_row] if scatter else src_ref, dst_ref=land_ref.at[land_row],
        send_sem=send_sems.at[a * N_PEERS + k], recv_sem=recv_sems.at[a * N_PEERS + k], device_id=dev, device_id_type=MESH)


def _landing(own, me):
    return lax.dynamic_update_index_in_dim(lax.empty((N_DEV,) + own.shape, own.dtype), own, me, 0)


def _push_start(srcs, lands, scatter, name):
    na = len(srcs)

    def body(*refs):
        src_refs, land_refs = refs[:na], refs[na:2 * na]
        send_sems, recv_sems = refs[2 * na], refs[2 * na + 1]
        token = refs[-1]
        me, peers = _peers()
        for a in range(na):
            for k, (dev, idx) in enumerate(peers):
                _push_copy(src_refs[a], land_refs[a], send_sems, recv_sems, a, k, dev, idx, me, scatter).start()
        token[...] = jnp.zeros_like(token)

    hbm = lambda arrs: [pltpu.HBM(a.shape, a.dtype) for a in arrs]
    out = pl.pallas_call(
        body, name=name,
        out_shape=(pltpu.SemaphoreType.DMA((na * N_PEERS,)), pltpu.SemaphoreType.DMA((na * N_PEERS,)), *hbm(srcs), *hbm(lands),
                   SDS((8, LANES), f32)),
        in_specs=[_HBM] * (2 * na),
        out_specs=(_SEM, _SEM, *([_HBM] * (2 * na)), pl.BlockSpec(memory_space=pltpu.VMEM)),
        input_output_aliases={i: 2 + i for i in range(2 * na)},
        compiler_params=pltpu.CompilerParams(has_side_effects=_DATAFLOW),
    )(*[pltpu.with_memory_space_constraint(a, pltpu.HBM) for a in list(srcs) + list(lands)])
    return out[0], out[1], list(out[2:2 + na]), list(out[2 + na:2 + 2 * na]), out[-1]


def _push_wait(send_sems, recv_sems, srcs, lands, which, after, scatter, name):
    nw = len(which)

    def body(*refs):
        src_refs, land_refs = refs[:nw], refs[nw:2 * nw]
        send_sems_, recv_sems_ = refs[2 * nw], refs[2 * nw + 1]
        _, peers = _peers()
        for j, a in enumerate(which):
            for k, (dev, idx) in enumerate(peers):
                cp = _push_copy(src_refs[j], land_refs[j], send_sems_, recv_sems_, a, k, dev, idx, idx, scatter)
                cp.wait_send()
                cp.wait_recv()

    hbm = lambda arrs: [pltpu.HBM(a.shape, a.dtype) for a in arrs]
    out = pl.pallas_call(
        body, name=name, out_shape=(*hbm(srcs), *hbm(lands)),
        in_specs=[_HBM] * (2 * nw) + [_SEM, _SEM, _ANY], out_specs=[_HBM] * (2 * nw),
        input_output_aliases={i: i for i in range(2 * nw)},
        compiler_params=pltpu.CompilerParams(has_side_effects=_DATAFLOW),
    )(*srcs, *lands, send_sems, recv_sems, after)
    return list(out[nw:])


def _sum_rows(parts, tr, name):
    g, r, c = parts.shape

    def body(p_ref, o_ref):
        acc = p_ref[0].astype(f32)
        for k in range(1, g):
            acc = acc + p_ref[k].astype(f32)
        o_ref[...] = acc.astype(o_ref.dtype)

    return pl.pallas_call(
        body, grid=(r // tr,), name=name, in_specs=[pl.BlockSpec((g, tr, c), lambda i: (0, i, 0))],
        out_specs=pl.BlockSpec((tr, c), lambda i: (i, 0)), out_shape=SDS((r, c), parts.dtype), compiler_params=_params(),
    )(parts)


def _adamw(parts, w, m, v, tr, name):
    g, r, c = parts.shape

    def body(p_ref, w_ref, m_ref, v_ref, g_ref, d_ref, m2_ref, v2_ref):
        gr = p_ref[0].astype(f32)
        for k in range(1, g):
            gr = gr + p_ref[k].astype(f32)
        m2 = ADAM_B1 * m_ref[...] + (1.0 - ADAM_B1) * gr
        v2 = ADAM_B2 * v_ref[...] + (1.0 - ADAM_B2) * (gr * gr)
        m_hat = m2 / (1.0 - ADAM_B1 ** ADAM_STEP)
        v_hat = v2 / (1.0 - ADAM_B2 ** ADAM_STEP)
        g_ref[...] = gr
        d_ref[...] = -ADAM_LR * (m_hat / (jnp.sqrt(v_hat) + ADAM_EPS) + ADAM_WD * w_ref[...])
        m2_ref[...] = m2
        v2_ref[...] = v2

    row = pl.BlockSpec((tr, c), lambda i: (i, 0))
    return pl.pallas_call(
        body, grid=(r // tr,), name=name, in_specs=[pl.BlockSpec((g, tr, c), lambda i: (0, i, 0)), row, row, row],
        out_specs=[row] * 4, out_shape=[SDS((r, c), f32)] * 4, compiler_params=_params(),
    )(parts, w, m, v)


_SHARD_AXIS = dict(meta_tokens=1, norm_gains=2, a_w_in=2, a_lb_logits=1, a_head_norm=1, a_w_out=1, kv_w=1,
                   b_w_q=1, b_w_out=1, ffn_w_up=2, ffn_conv=2, ffn_w_down=1)
_MATRICES = ("a_w_in", "a_w_out", "kv_w", "b_w_q", "b_w_out", "ffn_w_up", "ffn_w_down")
_VECTORS = ("meta_tokens", "norm_gains", "a_lb_logits", "a_head_norm", "ffn_conv")
_SHARDED = tuple(_SHARD_AXIS)
_REPLICATED = ("kv_norm", "fg_b")
_ROW_TILE_CAP = 512


def _pack(arrs, dtype, cols, row_mult):
    lead = arrs[0].shape[:-1] if arrs[0].ndim > 1 else ()
    flat = jnp.concatenate([a.astype(dtype) for a in arrs], axis=-1)
    size = flat.shape[-1]
    per = cols * row_mult
    total = -(-size // per) * per
    flat = jnp.pad(flat, [(0, 0)] * len(lead) + [(0, total - size)])
    return flat.reshape(lead + (total // cols, cols))


def _unpack(flat, shapes):
    out, off = [], 0
    lead = flat.shape[:-1]
    for shp in shapes:
        size = 1
        for s in shp:
            size *= s
        out.append(flat[..., off:off + size].reshape(lead + tuple(shp)))
        off += size
    return out


def _unshard(seg, axis):
    a = jnp.moveaxis(seg, 0, axis)
    shp = a.shape
    return a.reshape(shp[:axis] + (shp[axis] * shp[axis + 1],) + shp[axis + 2:])


def _shard8(full, axis):
    shp = full.shape
    a = full.reshape(shp[:axis] + (N_DEV, shp[axis] // N_DEV) + shp[axis + 1:])
    return jnp.moveaxis(a, axis, 0)


def _rows(a, lead=0):
    return a.reshape(a.shape[:lead] + (-1, a.shape[-1]))


def _core_major(a):
    _, r, c = a.shape
    return a.reshape(4, 2, r, c).transpose(1, 0, 2, 3).reshape(2, 4 * r, c)


def kernel(x, meta_tokens, norm_gains, a_w_in, a_lb_logits, a_head_norm, a_w_out, kv_norm, kv_w, fg_b, b_w_q, b_w_out, ffn_w_up, ffn_conv, ffn_w_down, loss_target, m_meta_tokens, m_norm_gains, m_a_w_in, m_a_lb_logits, m_a_head_norm, m_a_w_out, m_kv_norm, m_kv_w, m_fg_b, m_b_w_q, m_b_w_out, m_ffn_w_up, m_ffn_conv, m_ffn_w_down, v_meta_tokens, v_norm_gains, v_a_w_in, v_a_lb_logits, v_a_head_norm, v_a_w_out, v_kv_norm, v_kv_w, v_fg_b, v_b_w_q, v_b_w_out, v_ffn_w_up, v_ffn_conv, v_ffn_w_down):
    names = ("meta_tokens", "norm_gains", "a_w_in", "a_lb_logits", "a_head_norm", "a_w_out", "kv_norm", "kv_w", "fg_b",
             "b_w_q", "b_w_out", "ffn_w_up", "ffn_conv", "ffn_w_down")
    w = dict(zip(names, (meta_tokens, norm_gains, a_w_in, a_lb_logits, a_head_norm, a_w_out, kv_norm, kv_w, fg_b,
                         b_w_q, b_w_out, ffn_w_up, ffn_conv, ffn_w_down)))
    mom = dict(zip(names, (m_meta_tokens, m_norm_gains, m_a_w_in, m_a_lb_logits, m_a_head_norm, m_a_w_out, m_kv_norm,
                           m_kv_w, m_fg_b, m_b_w_q, m_b_w_out, m_ffn_w_up, m_ffn_conv, m_ffn_w_down)))
    var = dict(zip(names, (v_meta_tokens, v_norm_gains, v_a_w_in, v_a_lb_logits, v_a_head_norm, v_a_w_out, v_kv_norm,
                           v_kv_w, v_fg_b, v_b_w_q, v_b_w_out, v_ffn_w_up, v_ffn_conv, v_ffn_w_down)))

    bl_, seq, d = x.shape
    nm = meta_tokens.shape[0]
    t = nm + seq
    n = bl_ * t
    bh = fg_b.shape[0]
    hd = d // bh
    hp = LANES // hd
    ff = ffn_w_down.shape[1] * N_DEV
    tm = _div_tile(t, TOKEN_TILE_CAP)
    tc = _div_tile(t, 64)
    tn = 512

    vec_pack = _pack([w[k].reshape(-1) for k in _VECTORS], f32, LANES, 8)
    first = _all_gather([w["a_w_in"].astype(bf16), w["a_w_out"].astype(bf16), vec_pack], "gather_first")
    vec_segs = _unpack(first[2].reshape(N_DEV, -1), [w[k].shape for k in _VECTORS])
    small = {k: _unshard(a, _SHARD_AXIS[k]) for k, a in zip(_VECTORS, vec_segs)}
    w_in, w_out_a = _unshard(first[0], _SHARD_AXIS["a_w_in"])[0], _unshard(first[1], _SHARD_AXIS["a_w_out"])[0]
    me = _device_index()
    later_names = ("ffn_w_up", "ffn_w_down", "kv_w", "b_w_q", "b_w_out")
    later, _ = lax.optimization_barrier(([w[k].astype(bf16) for k in later_names], first[2]))
    g_send, g_recv, later_src, later_land, _ = _push_start(later, [_landing(a, me) for a in later], False, "gather_rest_start")

    def gathered(which, after, name):
        lands = _push_wait(g_send, g_recv, [later_src[i] for i in which], [later_land[i] for i in which], which, after,
                           False, name)
        return [_unshard(a, _SHARD_AXIS[later_names[i]]) for i, a in zip(which, lands)]

    gains_box = [small["norm_gains"]]
    gain = lambda l, j: gains_box[0][l, j][None]
    cw_gate, cw_val = small["ffn_conv"][:, :, :ff], small["ffn_conv"][:, :, ff:]
    head_gain = small["a_head_norm"]
    lb = jax.nn.softmax(small["a_lb_logits"], axis=0)[0:1]
    kvn = kv_norm[None]
    fgb_pad = jnp.pad(fg_b, (0, LANES - bh))[None]

    h0 = jnp.concatenate([jnp.broadcast_to(small["meta_tokens"][None], (bl_, nm, d)), x], axis=1).reshape(n, d)

    def ffn_fwd(l, h_in):
        fi = _rms_fwd(h_in, gain(l, 2), tm, f"ffn{l}_norm")
        ug = _mm(fi, w_gate[l], bf16, tm, ff, f"ffn{l}_up_gate")
        uv = _mm(fi, w_val[l], bf16, tm, ff, f"ffn{l}_up_val")
        act = _conv_gate_fwd(ug, uv, cw_gate[l], cw_val[l], bl_, t, tc, f"ffn{l}_conv_gate")
        h_out, mix = _mm_norm_res(act, w_down[l], gain(l, 3), h_in, tm, f"ffn{l}_down")
        return h_out, (h_in, fi, ug, uv, act, mix)

    hn0 = _rms_fwd(h0, gain(0, 0), tm, "a_norm")
    pmat = _mm(hn0, w_in, f32, tm, tn, "a_in_proj")
    og, states = _gla_fwd(pmat, lb, head_gain, bl_, t, nm, "a_gla_fwd")
    h1, mix_a = _mm_norm_res(og, w_out_a, gain(0, 1), h0, tm, "a_out_proj")
    w_up, w_down = gathered((0, 1), h1, "gather_wait_ffn")
    w_gate, w_val = w_up[:, :, :ff], w_up[:, :, ff:]
    h2, ffn0 = ffn_fwd(0, h1)

    w_kv_zf, w_q, w_out_b = gathered((2, 3, 4), h2, "gather_wait_b")
    w_kv, w_zf = w_kv_zf[:, :2 * d], jnp.pad(w_kv_zf[:, 2 * d:], ((0, 0), (0, LANES - bh)))
    w_q, w_out_b = w_q[0], w_out_b[0]
    hk = _rms_fwd(h2, kvn, tm, "kv_norm")
    kvp = _mm(hk, w_kv, bf16, tm, tn, "kv_proj")
    zf, cum = _zf_c(hk, w_zf, fgb_pad, bl_, t, tm, "forget_cumsum")
    ck = _c_key_rows(cum, bl_, t, tm, bh, hp)
    hn1 = _rms_fwd(h2, gain(1, 0), tm, "b_norm")
    q = _mm(hn1, w_q, bf16, tm, tn, "b_q_proj")
    o, lse = _attn_fwd(q, kvp, ck, bl_, t, tm, hd, "b_attn_fwd")
    h3, mix_b = _mm_norm_res(o, w_out_b, gain(1, 1), h2, tm, "b_out_proj")
    h4, ffn1 = ffn_fwd(1, h3)

    target = jnp.concatenate([jnp.zeros((bl_, nm, d), f32), loss_target], axis=1).reshape(n, d)
    loss8, dh = _loss_head(h4, target, t, nm, tm, "loss_head")
    loss = lax.psum(loss8[0, 0], ("x", "y", "c"))

    dgain = {}

    def ffn_bwd(l, saved, dh_out):
        h_in, fi, ug, uv, act, mix = saved
        dmix, dgain[l, 3] = _rms_bwd(mix, gain(l, 3), dh_out, None, bf16, tm, f"ffn{l}_down_norm_bwd")
        dact = _mm_nt([(dmix, w_down[l])], bf16, tm, ff, f"ffn{l}_down_dx")
        dw_down = _mm_tn(act, dmix, tm, ff, tn, f"ffn{l}_down_dw")
        dug, duv, dcg, dcv = _conv_gate_bwd(ug, uv, cw_gate[l], cw_val[l], dact, bl_, t, tc, f"ffn{l}_conv_gate_bwd")
        dfi = _mm_nt([(dug, w_gate[l]), (duv, w_val[l])], bf16, tm, 256, f"ffn{l}_up_dx")
        dw_up = jnp.concatenate([_mm_tn(fi, dug, tm, tn, ff, f"ffn{l}_up_gate_dw"),
                                 _mm_tn(fi, duv, tm, tn, ff, f"ffn{l}_up_val_dw")], axis=1)
        dh_in, dgain[l, 2] = _rms_bwd(h_in, gain(l, 2), dfi, dh_out, f32, tm, f"ffn{l}_norm_bwd")
        return dh_in, dw_up, jnp.concatenate([dcg, dcv], axis=1), dw_down

    def shards(full, axis):
        return _rows(_shard8(full, axis), 1).astype(bf16)

    def push_grads(bufs, name):
        lands = [_landing(lax.dynamic_index_in_dim(b, me, 0, keepdims=False), me) for b in bufs]
        s_sem, r_sem, srcs, lands, token = _push_start(bufs, lands, True, name)
        gains_box[0] = gains_box[0] + token[0, 0]
        return s_sem, r_sem, srcs, lands

    def landed(handle, after, name):
        s_sem, r_sem, srcs, lands = handle
        return _push_wait(s_sem, r_sem, srcs, lands, tuple(range(len(srcs))), after, True, name)

    dh, dw_up1, dconv1, dw_down1 = ffn_bwd(1, ffn1, dh)
    push1 = push_grads([shards(dw_up1, 1), shards(dw_down1, 0)], "grad_push_ffn1")

    dmix, dgain[1, 1] = _rms_bwd(mix_b, gain(1, 1), dh, None, bf16, tm, "b_out_norm_bwd")
    do = _mm_nt([(dmix, w_out_b)], bf16, tm, tn, "b_out_dx")
    dw_out_b = _mm_tn(o, dmix, tm, tn, tn, "b_out_dw")
    dq, dk, dv, dck, dcq = _attn_bwd(q, kvp, o, do, lse, ck, bl_, t, tm, hd, "b_attn_bwd")
    dhn1 = _mm_nt([(dq, w_q)], bf16, tm, tn, "b_q_dx")
    dw_q = _mm_tn(hn1, dq, tm, tn, tn, "b_q_dw")
    dh, dgain[1, 0] = _rms_bwd(h2, gain(1, 0), dhn1, dh, f32, tm, "b_norm_bwd")

    dzf, dfgb = _c_bwd(_dc_rows(dck, dcq, bl_, t, bh), zf, bl_, t, tm, "forget_cumsum_bwd")
    dhk = _mm_nt([(dk, w_kv[:, :d]), (dv, w_kv[:, d:]), (dzf, w_zf)], bf16, tm, tn, "kv_dx")
    dw_kv = jnp.concatenate([_mm_tn(hk, dk, tm, tn, tn, "k_dw"), _mm_tn(hk, dv, tm, tn, tn, "v_dw"),
                             _mm_tn(hk, dzf, tm, tn, LANES, "zf_dw")[:, :bh]], axis=1)
    dh, dkvn = _rms_bwd(h2, kvn, dhk, dh, f32, tm, "kv_norm_bwd")
    push2 = push_grads([shards(dw_out_b, 0), shards(dw_q, 0), shards(dw_kv, 1)], "grad_push_b")

    dh, dw_up0, dconv0, dw_down0 = ffn_bwd(0, ffn0, dh)
    push3 = push_grads([shards(dw_up0, 1), shards(dw_down0, 0)], "grad_push_ffn0")

    dmix, dgain[0, 1] = _rms_bwd(mix_a, gain(0, 1), dh, None, bf16, tm, "a_out_norm_bwd")
    dog = _mm_nt([(dmix, w_out_a)], bf16, tm, tn, "a_out_dx")
    dw_out_a = _mm_tn(og, dmix, tm, tn, tn, "a_out_dw")
    dpq, dpf, dpi, dpg, dlb, dhg = _gla_bwd(pmat, states, dog, lb, head_gain, bl_, t, nm, "a_gla_bwd")
    dps = (dpq, dpf, dpi, dpg)
    dhn0 = _mm_nt([(dp, w_in[:, j * d:(j + 1) * d]) for j, dp in enumerate(dps)], bf16, tm, tn, "a_in_dx")
    dw_in = jnp.concatenate([_mm_tn(hn0, dp, tm, tn, tn, f"a_in_dw{j}") for j, dp in enumerate(dps)], axis=1)
    dh, dgain[0, 0] = _rms_bwd(h0, gain(0, 0), dhn0, dh, f32, tm, "a_norm_bwd")

    dh = dh.reshape(bl_, t, d)
    grad_x = dh[:, nm:]
    dl0 = dlb * lb * (1.0 - lb)
    vec_grads = dict(
        meta_tokens=jnp.sum(dh[:, :nm], axis=0),
        norm_gains=jnp.stack([jnp.concatenate([dgain[l, j] for j in range(4)], axis=0) for l in range(2)]),
        a_lb_logits=jnp.concatenate([dl0, -dl0], axis=0), a_head_norm=dhg, ffn_conv=jnp.stack([dconv0, dconv1]))
    vec_send = _pack([_shard8(vec_grads[k], _SHARD_AXIS[k]).reshape(N_DEV, -1) for k in _VECTORS], bf16, LANES, BF16_ROWS)
    push4 = push_grads([shards(dw_out_a, 0), shards(dw_in, 1), vec_send], "grad_push_a")

    g_s, d_s, m_s, v_s = {}, {}, {}, {}
    outs = (g_s, d_s, m_s, v_s)

    def update(part, srcs, label):
        rows = part.shape[1]
        return _adamw(part, *srcs, rows if rows <= _ROW_TILE_CAP else _div_tile(rows, _ROW_TILE_CAP), label)

    def update_matrix(k, part, layer=None):
        pick = (lambda a: a) if layer is None else (lambda a: a[layer])
        label = f"adamw_{k}" if layer is None else f"adamw_{k}{layer}"
        res = update(part, [_rows(pick(src[k])) for src in (w, mom, var)], label)
        return [r.reshape(pick(w[k]).shape) for r in res]

    def put(k, res):
        for dst, r in zip(outs, res):
            dst[k] = r

    up1, down1 = (update_matrix(k, p, 1) for k, p in zip(("ffn_w_up", "ffn_w_down"), landed(push1, gains_box[0], "grad_wait_ffn1")))
    for k, p in zip(("b_w_out", "b_w_q", "kv_w"), landed(push2, up1[0], "grad_wait_b")):
        put(k, update_matrix(k, p))
    up0, down0 = (update_matrix(k, p, 0) for k, p in zip(("ffn_w_up", "ffn_w_down"), landed(push3, g_s["kv_w"], "grad_wait_ffn0")))
    put("ffn_w_up", [jnp.stack(pair) for pair in zip(up0, up1)])
    put("ffn_w_down", [jnp.stack(pair) for pair in zip(down0, down1)])
    part_out_a, part_in, part_vec = landed(push4, down0[0], "grad_wait_a")
    put("a_w_out", update_matrix("a_w_out", part_out_a))
    put("a_w_in", update_matrix("a_w_in", part_in))
    vec_packs = [_pack([src[k].reshape(-1) for k in _VECTORS], f32, LANES, BF16_ROWS) for src in (w, mom, var)]
    vec_shapes = [w[k].shape for k in _VECTORS]
    for dst, r in zip(outs, update(part_vec, vec_packs, "adamw_vectors")):
        dst.update(zip(_VECTORS, _unpack(r.reshape(-1), vec_shapes)))

    rep_local = _pack([dkvn.reshape(-1), dfgb[0, :bh]], f32, LANES, 8)
    rep_parts = _all_gather([rep_local], "gather_replicated_grads")[0]
    rpacks = [_pack([src[k].reshape(-1) for k in _REPLICATED], f32, LANES, 8) for src in (w, mom, var)]
    rres = _adamw(rep_parts, *rpacks, rep_local.shape[0], "adamw_replicated")
    rshapes = [w[k].shape for k in _REPLICATED]
    g_r, d_r, m_r, v_r = ({k: a for k, a in zip(_REPLICATED, _unpack(r.reshape(-1), rshapes))} for r in rres)

    out = [loss, grad_x]
    for sh, rp in ((g_s, g_r), (d_s, d_r), (m_s, m_r), (v_s, v_r)):
        out += [sh[k] if k in sh else rp[k] for k in names]
    return tuple(out)
```

```python
import functools
import math

import jax
import jax.numpy as jnp
from jax import lax
from jax.experimental import pallas as pl
from jax.experimental.pallas import tpu as pltpu

f32 = jnp.float32
bf16 = jnp.bfloat16
SDS = jax.ShapeDtypeStruct

EPS = 1e-6
A_DK = 128
A_CHUNK = 64
GLA_GROUP = 4
TOKEN_TILE_CAP = 1024
LANES = 128
BF16_ROWS = 16
VMEM_LIMIT = 56 * 1024 * 1024
ADAM_LR, ADAM_B1, ADAM_B2, ADAM_EPS, ADAM_WD, ADAM_STEP = 0.001, 0.9, 0.999, 1e-08, 0.01, 10
N_DEV = 8
MESH = pl.DeviceIdType.MESH

_NT = (((1,), (1,)), ((), ()))
_TN = (((0,), (0,)), ((), ()))
_HI = lax.Precision.HIGHEST


def _params(**kw):
    return pltpu.CompilerParams(vmem_limit_bytes=VMEM_LIMIT, **kw)


def _div_tile(n, cap, mult=BF16_ROWS):
    best = None
    for t in range(mult, min(n, cap) + 1, mult):
        if n % t == 0:
            best = t
    assert best is not None, (n, cap, mult)
    return best


def _bdot(a, b):
    return jnp.dot(a.astype(bf16), b.astype(bf16), preferred_element_type=f32)


def _bdot_nt(a, b):
    return lax.dot_general(a.astype(bf16), b.astype(bf16), _NT, preferred_element_type=f32)


def _bdot_tn(a, b):
    return lax.dot_general(a.astype(bf16), b.astype(bf16), _TN, preferred_element_type=f32)


def _iota2(shape, axis):
    return lax.broadcasted_iota(jnp.int32, shape, axis)


def _cumsum_rows(x):
    n = x.shape[0]
    tri = (_iota2((n, n), 0) >= _iota2((n, n), 1)).astype(f32)
    return jnp.dot(tri, x, precision=_HI, preferred_element_type=f32)


def _revcumsum_rows(x):
    n = x.shape[0]
    tri = (_iota2((n, n), 1) >= _iota2((n, n), 0)).astype(f32)
    return jnp.dot(tri, x, precision=_HI, preferred_element_type=f32)


def _sigmoid(x):
    return 1.0 / (1.0 + jnp.exp(-x))


def _rms_fwd(x, g, tm, name):
    n, d = x.shape

    def body(x_ref, g_ref, o_ref):
        xv = x_ref[...]
        r = lax.rsqrt(jnp.mean(xv * xv, axis=-1, keepdims=True) + EPS)
        o_ref[...] = (xv * r * g_ref[...]).astype(o_ref.dtype)

    return pl.pallas_call(
        body, grid=(n // tm,), name=name,
        in_specs=[pl.BlockSpec((tm, d), lambda i: (i, 0)), pl.BlockSpec((1, d), lambda i: (0, 0))],
        out_specs=pl.BlockSpec((tm, d), lambda i: (i, 0)),
        out_shape=SDS((n, d), bf16), compiler_params=_params(),
    )(x, g)


def _mm(a, w, out_dtype, tm, tn, name):
    n, k = a.shape
    m = w.shape[1]

    def body(a_ref, w_ref, o_ref):
        o_ref[...] = _bdot(a_ref[...], w_ref[...]).astype(o_ref.dtype)

    return pl.pallas_call(
        body, grid=(m // tn, n // tm), name=name,
        in_specs=[pl.BlockSpec((tm, k), lambda j, i: (i, 0)), pl.BlockSpec((k, tn), lambda j, i: (0, j))],
        out_specs=pl.BlockSpec((tm, tn), lambda j, i: (i, j)),
        out_shape=SDS((n, m), out_dtype), compiler_params=_params(),
    )(a, w)


def _mm_norm_res(a, w, g, h, tm, name):
    n, k = a.shape
    d = w.shape[1]

    def body(a_ref, w_ref, g_ref, h_ref, hn_ref, mix_ref):
        mix = _bdot(a_ref[...], w_ref[...])
        r = lax.rsqrt(jnp.mean(mix * mix, axis=-1, keepdims=True) + EPS)
        mix_ref[...] = mix
        hn_ref[...] = h_ref[...] + mix * r * g_ref[...]

    return pl.pallas_call(
        body, grid=(n // tm,), name=name,
        in_specs=[pl.BlockSpec((tm, k), lambda i: (i, 0)), pl.BlockSpec((k, d), lambda i: (0, 0)),
                  pl.BlockSpec((1, d), lambda i: (0, 0)), pl.BlockSpec((tm, d), lambda i: (i, 0))],
        out_specs=[pl.BlockSpec((tm, d), lambda i: (i, 0)), pl.BlockSpec((tm, d), lambda i: (i, 0))],
        out_shape=[SDS((n, d), f32), SDS((n, d), f32)], compiler_params=_params(),
    )(a, w, g, h)


def _rms_bwd(x, g, dy, dh_in, out_dtype, tm, name):
    n, d = x.shape
    has_add = dh_in is not None

    def body(*refs):
        if has_add:
            x_ref, g_ref, dy_ref, dh_ref, o_ref, dg_ref = refs
        else:
            x_ref, g_ref, dy_ref, o_ref, dg_ref = refs
        xv = x_ref[...]
        dyv = dy_ref[...].astype(f32)
        r = lax.rsqrt(jnp.mean(xv * xv, axis=-1, keepdims=True) + EPS)
        xr = xv * r
        gdy = dyv * g_ref[...]
        dx = r * gdy - xr * (r * r) * jnp.mean(xv * gdy, axis=-1, keepdims=True)
        if has_add:
            dx = dx + dh_ref[...]
        o_ref[...] = dx.astype(o_ref.dtype)

        @pl.when(pl.program_id(0) == 0)
        def _():
            dg_ref[...] = jnp.zeros_like(dg_ref)

        dg_ref[...] += jnp.sum(dyv * xr, axis=0, keepdims=True)

    row = pl.BlockSpec((tm, d), lambda i: (i, 0))
    vec = pl.BlockSpec((1, d), lambda i: (0, 0))
    ins = [x, g, dy] + ([dh_in] if has_add else [])
    return pl.pallas_call(
        body, grid=(n // tm,), name=name,
        in_specs=[row, vec, row] + ([row] if has_add else []),
        out_specs=[row, vec],
        out_shape=[SDS((n, d), out_dtype), SDS((1, d), f32)], compiler_params=_params(),
    )(*ins)


def _mm_nt(pairs, out_dtype, tm, tk, name):
    n = pairs[0][0].shape[0]
    k = pairs[0][1].shape[0]
    np_ = len(pairs)

    def body(*refs):
        o_ref = refs[-1]
        acc = None
        for p in range(np_):
            t = _bdot_nt(refs[2 * p][...], refs[2 * p + 1][...])
            acc = t if acc is None else acc + t
        o_ref[...] = acc.astype(o_ref.dtype)

    in_specs, ins = [], []
    for dy, w in pairs:
        m = dy.shape[1]
        in_specs += [pl.BlockSpec((tm, m), lambda j, i: (i, 0)), pl.BlockSpec((tk, m), lambda j, i: (j, 0))]
        ins += [dy, w]
    return pl.pallas_call(
        body, grid=(k // tk, n // tm), name=name, in_specs=in_specs,
        out_specs=pl.BlockSpec((tm, tk), lambda j, i: (i, j)),
        out_shape=SDS((n, k), out_dtype), compiler_params=_params(),
    )(*ins)


def _mm_tn(x, dy, tm, tk, tn, name):
    n, k = x.shape
    m = dy.shape[1]

    def body(x_ref, dy_ref, o_ref):
        @pl.when(pl.program_id(2) == 0)
        def _():
            o_ref[...] = jnp.zeros_like(o_ref)

        o_ref[...] += _bdot_tn(x_ref[...], dy_ref[...])

    return pl.pallas_call(
        body, grid=(k // tk, m // tn, n // tm), name=name,
        in_specs=[pl.BlockSpec((tm, tk), lambda a, b, i: (i, a)), pl.BlockSpec((tm, tn), lambda a, b, i: (i, b))],
        out_specs=pl.BlockSpec((tk, tn), lambda a, b, i: (a, b)),
        out_shape=SDS((k, m), f32), compiler_params=_params(),
    )(x, dy)


def _split3(x):
    hi = x.astype(bf16)
    r = x - hi.astype(f32)
    mid = r.astype(bf16)
    return hi, mid, (r - mid.astype(f32)).astype(bf16)


def _mask_dot(mask, x):
    hi, mid, lo = _split3(x)
    dot = lambda p: jnp.dot(mask, p, preferred_element_type=f32)
    return dot(hi) + dot(mid) + dot(lo)


def _chunk_rows(parts, cl):
    tiles = [jnp.broadcast_to(p, (cl, p.shape[1])) for p in parts]
    return tiles[0] if len(tiles) == 1 else jnp.concatenate(tiles, axis=0)


def _cat(parts):
    return parts[0] if len(parts) == 1 else jnp.concatenate(parts, axis=0)


def _gla_group_fwd(qg, fg, vg, lb, st, nc, cl):
    g = nc * cl
    sg = _sigmoid(fg)
    f = lb + (1.0 - lb) * sg
    k = 1.0 - f
    row, col = _iota2((g, g), 0), _iota2((g, g), 1)
    chunk_of = lambda idx: sum((idx >= u * cl).astype(jnp.int32) for u in range(1, nc)) if nc > 1 else 0
    same = chunk_of(row) == chunk_of(col) if nc > 1 else None
    causal = row >= col if nc == 1 else jnp.logical_and(same, row >= col)
    anti = col >= row if nc == 1 else jnp.logical_and(same, col >= row)
    b = _mask_dot(causal.astype(bf16), jnp.log(f))
    bls = [b[(u + 1) * cl - 1:(u + 1) * cl, :] for u in range(nc)]
    ebls = [jnp.exp(x) for x in bls]
    e = jnp.exp(b)
    ei = jnp.exp(-b)
    eo = jnp.exp(_chunk_rows(bls, cl) - b)
    qi, ki, ko = qg * e, k * ei, k * eo
    att = jnp.where(causal, _bdot_nt(qi, ki), 0.0)
    o_intra = _bdot(att, vg)
    sl = [slice(u * cl, (u + 1) * cl) for u in range(nc)]
    ds = [_bdot_tn(vg[s], ko[s]) for s in sl]
    sts = [st]
    for u in range(nc):
        sts.append(sts[u] * ebls[u] + ds[u])
    o = o_intra + _cat([_bdot_nt(qi[sl[u]], sts[u]) for u in range(nc)])
    return dict(sg=sg, f=f, e=e, ei=ei, eo=eo, ebls=ebls, qi=qi, ki=ki, ko=ko, att=att, o=o, sts=sts, causal=causal,
                anti=anti, sl=sl)


def _gla_group(nreal, want):
    while nreal % want:
        want //= 2
    return max(want, 1)


def _head_out(o, ggc, hg):
    r = lax.rsqrt(jnp.mean(o * o, axis=-1, keepdims=True) + EPS)
    return o * r * hg * (ggc * _sigmoid(ggc))


def _gla_fwd(pmat, lb, hg, bl_, t, nm, name):
    n, d4 = pmat.shape
    d = d4 // 4
    nh = d // A_DK
    nreal = (t - nm) // A_CHUNK
    nch = nreal + 1
    un = _gla_group(nreal, GLA_GROUP)

    def body(q_ref, f_ref, i_ref, gg_ref, lb_ref, hg_ref, og_ref, ss_ref):
        lbv, hgv = lb_ref[...], hg_ref[...]

        def run(rows, st, idx, nc, cl):
            w = _gla_group_fwd(q_ref[rows, :], f_ref[rows, :], i_ref[rows, :], lbv, st, nc, cl)
            for u in range(nc):
                ss_ref[0, idx + u] = w["sts"][u]
            og_ref[rows, :] = _head_out(w["o"], gg_ref[rows, :], hgv).astype(og_ref.dtype)
            return w["sts"][nc]

        st = run(pl.ds(0, nm), jnp.zeros((A_DK, A_DK), f32), 0, 1, nm)

        def step(it, st):
            rows = pl.ds(pl.multiple_of(nm + it * (un * A_CHUNK), BF16_ROWS), un * A_CHUNK)
            return run(rows, st, 1 + it * un, un, A_CHUNK)

        lax.fori_loop(0, nreal // un, step, st)

    col = lambda o: pl.BlockSpec((t, A_DK), lambda b, h: (b, o * nh + h))
    vec = pl.BlockSpec((1, A_DK), lambda b, h: (0, h))
    return pl.pallas_call(
        body, grid=(bl_, nh), name=name,
        in_specs=[col(0), col(1), col(2), col(3), vec, vec],
        out_specs=[pl.BlockSpec((t, A_DK), lambda b, h: (b, h)),
                   pl.BlockSpec((1, nch, A_DK, A_DK), lambda b, h: (b * nh + h, 0, 0, 0))],
        out_shape=[SDS((n, d), bf16), SDS((bl_ * nh, nch, A_DK, A_DK), f32)], compiler_params=_params(),
    )(pmat, pmat, pmat, pmat, lb, hg)


def _gla_bwd(pmat, ss, dog, lb, hg, bl_, t, nm, name):
    n, d4 = pmat.shape
    d = d4 // 4
    nh = d // A_DK
    nreal = (t - nm) // A_CHUNK
    nch = nreal + 1
    un = _gla_group(nreal, GLA_GROUP)

    def body(q_ref, f_ref, i_ref, gg_ref, ss_ref, dog_ref, lb_ref, hg_ref,
             dq_ref, df_ref, di_ref, dgg_ref, dlb_ref, dhg_ref):
        lbv, hgv = lb_ref[...], hg_ref[...]

        def run(rows, idx, carry, nc, cl):
            dst, dlb, dhg = carry
            qg, fg, vg, ggc = q_ref[rows, :], f_ref[rows, :], i_ref[rows, :], gg_ref[rows, :]
            dogc = dog_ref[rows, :].astype(f32)
            w = _gla_group_fwd(qg, fg, vg, lbv, ss_ref[0, idx], nc, cl)
            o, qi, ki, ko, sl, sts, ebls = w["o"], w["qi"], w["ki"], w["ko"], w["sl"], w["sts"], w["ebls"]
            r = lax.rsqrt(jnp.mean(o * o, axis=-1, keepdims=True) + EPS)
            sgg = _sigmoid(ggc)
            sil = ggc * sgg
            on = o * r
            dhg = dhg + jnp.sum(dogc * sil * on, axis=0, keepdims=True)
            dgg_ref[rows, :] = (dogc * on * hgv * (sgg * (1.0 + ggc * (1.0 - sgg)))).astype(dgg_ref.dtype)
            tt = dogc * sil * hgv
            do = r * tt - on * (r * r) * jnp.mean(o * tt, axis=-1, keepdims=True)
            xs = [_bdot_tn(do[s], qi[s]) for s in sl]
            dsts = [None] * nc + [dst]
            for u in reversed(range(nc)):
                dsts[u] = dsts[u + 1] * ebls[u] + xs[u]
            datt = jnp.where(w["causal"], _bdot_nt(do, vg), 0.0)
            dv = _bdot_tn(w["att"], do) + _cat([_bdot_nt(ko[sl[u]], dsts[u + 1]) for u in range(nc)])
            dko = _cat([_bdot(vg[sl[u]], dsts[u + 1]) for u in range(nc)])
            dqi = _bdot(datt, ki) + _cat([_bdot(do[sl[u]], sts[u]) for u in range(nc)])
            dki = _bdot_tn(datt, qi)
            dk = dki * w["ei"] + dko * w["eo"]
            dkoko = dko * ko
            db = dqi * qi - dki * ki - dkoko
            rowi = _iota2(db.shape, 0)
            for u in range(nc):
                d_ebl = jnp.sum(dsts[u + 1] * sts[u], axis=0, keepdims=True)
                dbl = jnp.sum(dkoko[sl[u]], axis=0, keepdims=True) + d_ebl * ebls[u]
                db = db + jnp.where(rowi == (u + 1) * cl - 1, dbl, 0.0)
            dlogf = _mask_dot(w["anti"].astype(bf16), db)
            df = dlogf / w["f"] - dk
            sg = w["sg"]
            dq_ref[rows, :] = (dqi * w["e"]).astype(dq_ref.dtype)
            df_ref[rows, :] = (df * (1.0 - lbv) * sg * (1.0 - sg)).astype(df_ref.dtype)
            di_ref[rows, :] = dv.astype(di_ref.dtype)
            dlb = dlb + jnp.sum(df * (1.0 - sg), axis=0, keepdims=True)
            return dsts[0], dlb, dhg

        zero = jnp.zeros((1, A_DK), f32)
        ngroups = nreal // un

        def step(it, carry):
            grp = ngroups - 1 - it
            rows = pl.ds(pl.multiple_of(nm + grp * (un * A_CHUNK), BF16_ROWS), un * A_CHUNK)
            return run(rows, 1 + grp * un, carry, un, A_CHUNK)

        carry = lax.fori_loop(0, ngroups, step, (jnp.zeros((A_DK, A_DK), f32), zero, zero))
        _, dlb, dhg = run(pl.ds(0, nm), 0, carry, 1, nm)

        @pl.when(pl.program_id(1) == 0)
        def _():
            dlb_ref[...] = jnp.zeros_like(dlb_ref)
            dhg_ref[...] = jnp.zeros_like(dhg_ref)

        dlb_ref[...] += dlb
        dhg_ref[...] += dhg

    col = lambda o: pl.BlockSpec((t, A_DK), lambda h, b: (b, o * nh + h))
    blk = pl.BlockSpec((t, A_DK), lambda h, b: (b, h))
    vec = pl.BlockSpec((1, A_DK), lambda h, b: (0, h))
    return pl.pallas_call(
        body, grid=(nh, bl_), name=name,
        in_specs=[col(0), col(1), col(2), col(3),
                  pl.BlockSpec((1, nch, A_DK, A_DK), lambda h, b: (b * nh + h, 0, 0, 0)), blk, vec, vec],
        out_specs=[blk, blk, blk, blk, vec, vec],
        out_shape=[SDS((n, d), bf16)] * 4 + [SDS((1, d), f32)] * 2, compiler_params=_params(),
    )(pmat, pmat, pmat, pmat, ss, dog, lb, hg)


def _shift_down(x, prev2, s):
    row = _iota2(x.shape, 0)
    y = pltpu.roll(x, s, 0)
    if s == 1:
        return jnp.where(row == 0, prev2[1:2, :], y)
    return jnp.where(row == 0, prev2[0:1, :], jnp.where(row == 1, prev2[1:2, :], y))


def _shift_up(x, next2, s):
    n = x.shape[0]
    row = _iota2(x.shape, 0)
    y = pltpu.roll(x, n - s, 0)
    if s == 1:
        return jnp.where(row == n - 1, next2[0:1, :], y)
    return jnp.where(row == n - 1, next2[1:2, :], jnp.where(row == n - 2, next2[0:1, :], y))


def _conv3(x, prev2, w):
    return w[0:1, :] * _shift_down(x, prev2, 2) + w[1:2, :] * _shift_down(x, prev2, 1) + w[2:3, :] * x


def _conv_gate_fwd(ug, uv, cwg, cwv, bl_, t, tc, name):
    n, ff = ug.shape
    nt = t // tc

    def body(ug_ref, uv_ref, wg_ref, wv_ref, a_ref, hg_ref, hv_ref):
        @pl.when(pl.program_id(1) == 0)
        def _():
            hg_ref[...] = jnp.zeros_like(hg_ref)
            hv_ref[...] = jnp.zeros_like(hv_ref)

        xg = ug_ref[...].astype(f32)
        xv = uv_ref[...].astype(f32)
        cg = _conv3(xg, hg_ref[...], wg_ref[...])
        cv = _conv3(xv, hv_ref[...], wv_ref[...])
        a_ref[...] = (cg * _sigmoid(cg) * cv).astype(a_ref.dtype)
        hg_ref[...] = xg[tc - 2:tc, :]
        hv_ref[...] = xv[tc - 2:tc, :]

    row = pl.BlockSpec((tc, ff), lambda b, i: (b * nt + i, 0))
    wsp = pl.BlockSpec((3, ff), lambda b, i: (0, 0))
    return pl.pallas_call(
        body, grid=(bl_, nt), name=name, in_specs=[row, row, wsp, wsp], out_specs=row,
        out_shape=SDS((n, ff), bf16),
        scratch_shapes=[pltpu.VMEM((2, ff), f32), pltpu.VMEM((2, ff), f32)], compiler_params=_params(),
    )(ug, uv, cwg, cwv)


def _conv_gate_bwd(ug, uv, cwg, cwv, da, bl_, t, tc, name):
    n, ff = ug.shape
    nt = t // tc
    per = tc // BF16_ROWS

    def body(ug_ref, uv_ref, pg_ref, pv_ref, wg_ref, wv_ref, da_ref, dug_ref, duv_ref, dwg_ref, dwv_ref, ng_ref, nv_ref):
        first = jnp.logical_and(pl.program_id(0) == 0, pl.program_id(1) == 0)

        @pl.when(first)
        def _():
            dwg_ref[...] = jnp.zeros_like(dwg_ref)
            dwv_ref[...] = jnp.zeros_like(dwv_ref)

        @pl.when(pl.program_id(1) == 0)
        def _():
            ng_ref[...] = jnp.zeros_like(ng_ref)
            nv_ref[...] = jnp.zeros_like(nv_ref)

        seq_start = pl.program_id(1) == nt - 1
        dav = da_ref[...].astype(f32)

        def half(u_ref, p_ref, w_ref):
            x = u_ref[...].astype(f32)
            prev2 = jnp.where(seq_start, 0.0, p_ref[BF16_ROWS - 2:BF16_ROWS, :].astype(f32))
            x1, x2 = _shift_down(x, prev2, 1), _shift_down(x, prev2, 2)
            w = w_ref[...]
            return x, x1, x2, w[0:1, :] * x2 + w[1:2, :] * x1 + w[2:3, :] * x

        xg, xg1, xg2, cg = half(ug_ref, pg_ref, wg_ref)
        xv, xv1, xv2, cv = half(uv_ref, pv_ref, wv_ref)
        sg = _sigmoid(cg)
        dcg = dav * cv * (sg * (1.0 + cg * (1.0 - sg)))
        dcv = dav * (cg * sg)

        def back(dc, x, x1, x2, w_ref, nx_ref, du_ref, dw_ref):
            w = w_ref[...]
            nx = nx_ref[...]
            du = w[2:3, :] * dc + w[1:2, :] * _shift_up(dc, nx, 1) + w[0:1, :] * _shift_up(dc, nx, 2)
            du_ref[...] = du.astype(du_ref.dtype)
            dw_ref[0:1, :] += jnp.sum(dc * x2, axis=0, keepdims=True)
            dw_ref[1:2, :] += jnp.sum(dc * x1, axis=0, keepdims=True)
            dw_ref[2:3, :] += jnp.sum(dc * x, axis=0, keepdims=True)
            nx_ref[...] = dc[0:2, :]

        back(dcg, xg, xg1, xg2, wg_ref, ng_ref, dug_ref, dwg_ref)
        back(dcv, xv, xv1, xv2, wv_ref, nv_ref, duv_ref, dwv_ref)

    row = pl.BlockSpec((tc, ff), lambda b, i: (b * nt + nt - 1 - i, 0))
    prev = pl.BlockSpec((BF16_ROWS, ff), lambda b, i: (jnp.maximum((b * nt + nt - 1 - i) * per - 1, 0), 0))
    wsp = pl.BlockSpec((3, ff), lambda b, i: (0, 0))
    return pl.pallas_call(
        body, grid=(bl_, nt), name=name, in_specs=[row, row, prev, prev, wsp, wsp, row],
        out_specs=[row, row, wsp, wsp],
        out_shape=[SDS((n, ff), bf16), SDS((n, ff), bf16), SDS((3, ff), f32), SDS((3, ff), f32)],
        scratch_shapes=[pltpu.VMEM((2, ff), f32), pltpu.VMEM((2, ff), f32)], compiler_params=_params(),
    )(ug, uv, ug, uv, cwg, cwv, da)


def _zf_c(hk, wzf, fgb, bl_, t, tm, name):
    n, d = hk.shape
    nt = t // tm

    def body(hk_ref, w_ref, b_ref, zf_ref, c_ref, carry_ref):
        @pl.when(pl.program_id(1) == 0)
        def _():
            carry_ref[...] = jnp.zeros_like(carry_ref)

        z = _bdot(hk_ref[...], w_ref[...]) + b_ref[...]
        ls = jnp.minimum(z, 0.0) - jnp.log(1.0 + jnp.exp(-jnp.abs(z)))
        c = _cumsum_rows(ls) + carry_ref[...]
        zf_ref[...] = z
        c_ref[...] = c
        carry_ref[...] = c[tm - 1:tm, :]

    row = lambda w: pl.BlockSpec((tm, w), lambda b, i: (b * nt + i, 0))
    return pl.pallas_call(
        body, grid=(bl_, nt), name=name,
        in_specs=[row(d), pl.BlockSpec((d, LANES), lambda b, i: (0, 0)), pl.BlockSpec((1, LANES), lambda b, i: (0, 0))],
        out_specs=[row(LANES), row(LANES)],
        out_shape=[SDS((n, LANES), f32), SDS((n, LANES), f32)],
        scratch_shapes=[pltpu.VMEM((1, LANES), f32)], compiler_params=_params(),
    )(hk, wzf, fgb)


def _c_bwd(dc, zf, bl_, t, tm, name):
    n = dc.shape[0]
    nt = t // tm

    def body(dc_ref, zf_ref, dzf_ref, dfg_ref, carry_ref):
        @pl.when(jnp.logical_and(pl.program_id(0) == 0, pl.program_id(1) == 0))
        def _():
            dfg_ref[...] = jnp.zeros_like(dfg_ref)

        @pl.when(pl.program_id(1) == 0)
        def _():
            carry_ref[...] = jnp.zeros_like(carry_ref)

        rc = _revcumsum_rows(dc_ref[...]) + carry_ref[...]
        dz = rc * _sigmoid(-zf_ref[...])
        dzf_ref[...] = dz.astype(dzf_ref.dtype)
        dfg_ref[...] += jnp.sum(dz, axis=0, keepdims=True)
        carry_ref[...] = rc[0:1, :]

    row = pl.BlockSpec((tm, LANES), lambda b, i: (b * nt + nt - 1 - i, 0))
    vec = pl.BlockSpec((1, LANES), lambda b, i: (0, 0))
    return pl.pallas_call(
        body, grid=(bl_, nt), name=name, in_specs=[row, row], out_specs=[row, vec],
        out_shape=[SDS((n, LANES), bf16), SDS((1, LANES), f32)],
        scratch_shapes=[pltpu.VMEM((1, LANES), f32)], compiler_params=_params(),
    )(dc, zf)


def _is_pow2(x):
    m, _ = math.frexp(x)
    return m == 0.5


def _prescale(qh, scale):
    return (qh.astype(f32) * scale).astype(bf16)


def _attn_fwd(q, kv, ck, bl_, t, tq, hd, name):
    n, d = q.shape
    npair = d // LANES
    hp = LANES // hd
    nq = t // tq
    scale = 1.0 / (hd ** 0.5)

    pre = _is_pow2(scale)

    def body(q_ref, k_ref, v_ref, ck_ref, o_ref, lse_ref):
        i = pl.program_id(2)
        diag = _iota2((tq, tq), 0) >= _iota2((tq, tq), 1)
        for hh in range(hp):
            lanes = slice(hh * hd, (hh + 1) * hd)
            qh = _prescale(q_ref[:, lanes], scale) if pre else q_ref[:, lanes]

            def block(j, carry, masked, lanes=lanes, qh=qh, hh=hh):
                m, l, acc = carry
                rows = pl.ds(pl.multiple_of(j * tq, BF16_ROWS), tq)
                s = _bdot_nt(qh, k_ref[rows, lanes])
                s = (s if pre else s * scale) - ck_ref[0, 0, j, hh:hh + 1, :]
                if masked:
                    s = jnp.where(diag, s, -1e30)
                m2 = jnp.maximum(m, jnp.max(s, axis=-1, keepdims=True))
                p = jnp.exp(s - m2)
                a = jnp.exp(m - m2)
                return m2, a * l + jnp.sum(p, axis=-1, keepdims=True), a * acc + _bdot(p, v_ref[rows, lanes])

            init = (jnp.full((tq, 1), -1e30, f32), jnp.zeros((tq, 1), f32), jnp.zeros((tq, hd), f32))
            carry = lax.fori_loop(0, i, functools.partial(block, masked=False), init)
            m, l, acc = block(i, carry, True)
            o_ref[:, lanes] = (acc / l).astype(o_ref.dtype)
            lse_ref[:, lanes] = jnp.broadcast_to(m + jnp.log(l), (tq, hd))

    nk = nq
    return pl.pallas_call(
        body, grid=(bl_, npair, nq), name=name,
        in_specs=[pl.BlockSpec((tq, LANES), lambda b, p, i: (b * nq + i, p)),
                  pl.BlockSpec((t, LANES), lambda b, p, i: (b, p)),
                  pl.BlockSpec((t, LANES), lambda b, p, i: (b, npair + p)),
                  pl.BlockSpec((1, 1, nk, hp, tq), lambda b, p, i: (b, p, 0, 0, 0))],
        out_specs=[pl.BlockSpec((tq, LANES), lambda b, p, i: (b * nq + i, p)),
                   pl.BlockSpec((tq, LANES), lambda b, p, i: (b * nq + i, p))],
        out_shape=[SDS((n, d), f32), SDS((n, d), f32)], compiler_params=_params(),
    )(q, kv, kv, ck)


def _attn_bwd(q, kv, o, do, lse, ck, bl_, t, tq, hd, name):
    n, d = q.shape
    npair = d // LANES
    hp = LANES // hd
    nq = t // tq
    scale = 1.0 / (hd ** 0.5)

    pre = _is_pow2(scale)

    def body(q_ref, k_ref, v_ref, o_ref, do_ref, lse_ref, ck_ref, dq_ref, dk_ref, dv_ref, dck_ref, dcq_ref):
        j = pl.program_id(2)

        @pl.when(j == 0)
        def _():
            dq_ref[...] = jnp.zeros_like(dq_ref)
            dcq_ref[...] = jnp.zeros_like(dcq_ref)

        diag = _iota2((tq, tq), 0) >= _iota2((tq, tq), 1)
        for hh in range(hp):
            lanes = slice(hh * hd, (hh + 1) * hd)
            kh = k_ref[:, lanes]
            vh = v_ref[:, lanes]
            kt = kh.astype(f32).T.astype(bf16)
            cs = ck_ref[0, 0, 0, hh:hh + 1, :]

            def block(i, carry, masked, lanes=lanes, kh=kh, vh=vh, kt=kt, cs=cs, hh=hh):
                dkt, dvt, dcs = carry
                rows = pl.ds(pl.multiple_of(i * tq, BF16_ROWS), tq)
                qh = _prescale(q_ref[rows, lanes], scale) if pre else q_ref[rows, lanes]
                doh = do_ref[rows, lanes]
                s = _bdot_nt(qh, kh)
                s = (s if pre else s * scale) - cs
                if masked:
                    s = jnp.where(diag, s, -1e30)
                p = jnp.exp(s - lse_ref[rows, hh * hd:hh * hd + 1])
                delta = jnp.sum(doh.astype(f32) * o_ref[rows, lanes].astype(f32), axis=-1, keepdims=True)
                ds = p * (_bdot_nt(doh, vh) - delta)
                dsb = ds.astype(bf16)
                dq_ref[rows, lanes] += _bdot_nt(kt, dsb).T * scale
                dcq_ref[0, rows, hh:hh + 1] += jnp.sum(ds, axis=-1, keepdims=True)
                dkq = _bdot_tn(qh, dsb)
                return (dkt + (dkq if pre else dkq * scale), dvt + _bdot_tn(doh, p), dcs - jnp.sum(ds, axis=0, keepdims=True))

            init = (jnp.zeros((hd, tq), f32), jnp.zeros((hd, tq), f32), jnp.zeros((1, tq), f32))
            dkt, dvt, dcs = lax.fori_loop(j + 1, nq, functools.partial(block, masked=False), block(j, init, True))
            dk_ref[:, lanes] = dkt.T.astype(dk_ref.dtype)
            dv_ref[:, lanes] = dvt.T.astype(dv_ref.dtype)
            dck_ref[0, 0, 0, hh:hh + 1, :] = dcs

    whole = lambda c0: pl.BlockSpec((t, LANES), lambda b, p, j: (b, c0 + p))
    tile = lambda c0: pl.BlockSpec((tq, LANES), lambda b, p, j: (b * nq + j, c0 + p))
    ckspec = pl.BlockSpec((1, 1, 1, hp, tq), lambda b, p, j: (b, p, j, 0, 0))
    cqspec = pl.BlockSpec((1, t, hp), lambda b, p, j: (p, b, 0))
    return pl.pallas_call(
        body, grid=(bl_, npair, nq), name=name,
        in_specs=[whole(0), tile(0), tile(npair), whole(0), whole(0), whole(0), ckspec],
        out_specs=[whole(0), tile(0), tile(0), ckspec, cqspec],
        out_shape=[SDS((n, d), f32), SDS((n, d), bf16), SDS((n, d), bf16), SDS((bl_, npair, nq, hp, tq), f32),
                   SDS((npair, n, hp), f32)],
        compiler_params=_params(),
    )(q, kv, kv, o, do, lse, ck)


def _loss_head(h, target, t, nm, tm, name):
    n, d = h.shape
    nt = t // tm

    def body(h_ref, t_ref, loss_ref, dh_ref):
        i = pl.program_id(0)

        @pl.when(i == 0)
        def _():
            loss_ref[...] = jnp.zeros_like(loss_ref)

        pos = (i % nt) * tm + _iota2((tm, d), 0)
        err = jnp.where(pos >= nm, h_ref[...] - t_ref[...], 0.0)
        dh_ref[...] = err * (1.0 / d)
        loss_ref[...] += 0.5 * jnp.sum(jnp.mean(err * err, axis=-1, keepdims=True))

    row = pl.BlockSpec((tm, d), lambda i: (i, 0))
    return pl.pallas_call(
        body, grid=(n // tm,), name=name, in_specs=[row, row],
        out_specs=[pl.BlockSpec((8, LANES), lambda i: (0, 0)), row],
        out_shape=[SDS((8, LANES), f32), SDS((n, d), f32)], compiler_params=_params(),
    )(h, target)


def _c_key_rows(c, bl_, t, tq, bh, hp):
    npair = bh // hp
    nk = t // tq
    return c[:, :bh].reshape(bl_, nk, tq, npair, hp).transpose(0, 3, 1, 4, 2)


def _dc_rows(dck, dcq, bl_, t, bh):
    d = dck.transpose(0, 2, 4, 1, 3).reshape(bl_ * t, bh) + dcq.transpose(1, 0, 2).reshape(bl_ * t, bh)
    return jnp.pad(d, ((0, 0), (0, LANES - bh)))


_ANY = pl.BlockSpec(memory_space=pl.ANY)


def _all_gather(xs, name):
    na = len(xs)

    def body(*refs):
        x_refs, out_refs = refs[:na], refs[na:2 * na]
        send_sems, recv_sems, local_sems = refs[2 * na:]
        mx, my, mc = lax.axis_index("x"), lax.axis_index("y"), lax.axis_index("c")
        me, sibling = (mx, my, mc), (mx, my, 1 - mc)
        chips = [(1 - mx, my), (mx, 1 - my), (1 - mx, 1 - my)]

        def copy(a, k, block, to, own=False):
            px, py, pc = block
            rows = out_refs[a].at[4 * px + 2 * py + pc]
            return pltpu.make_async_remote_copy(
                src_ref=x_refs[a] if own else rows, dst_ref=rows,
                send_sem=send_sems.at[a, k], recv_sem=recv_sems.at[a, k], device_id=to, device_id_type=MESH)

        arrays = range(na)
        mine = [pltpu.make_async_copy(x_refs[a], out_refs[a].at[4 * mx + 2 * my + mc], local_sems.at[a]) for a in arrays]
        for cp in mine:
            cp.start()
        first = [copy(a, 1 + j, me, (*chip, mc), own=True) for j, chip in enumerate(chips) for a in arrays]
        first += [copy(a, 0, me, sibling, own=True) for a in arrays]
        for cp in first:
            cp.start()
        passed = []
        for j, chip in enumerate(chips):
            for a in arrays:
                copy(a, 1 + j, (*chip, mc), me).wait_recv()
                cp = copy(a, 4 + j, (*chip, mc), sibling)
                cp.start()
                passed.append(cp)
        for a in arrays:
            copy(a, 0, sibling, me).wait_recv()
        for j, chip in enumerate(chips):
            for a in arrays:
                copy(a, 4 + j, (*chip, 1 - mc), me).wait_recv()
        for cp in first + passed:
            cp.wait_send()
        for cp in mine:
            cp.wait()

    return pl.pallas_call(
        body, name=name, out_shape=[SDS((N_DEV,) + x.shape, x.dtype) for x in xs],
        in_specs=[_ANY] * na, out_specs=[_ANY] * na,
        scratch_shapes=[pltpu.SemaphoreType.DMA((na, 7)), pltpu.SemaphoreType.DMA((na, 7)), pltpu.SemaphoreType.DMA((na,))],
    )(*xs)


def _exchange(bufs, group, name):
    na = len(bufs)
    g = 2 if group == "c" else 4
    assert all(b.shape[0] == g for b in bufs)
    keep_own = g > 2

    def body(*refs):
        buf_refs, out_refs = refs[:na], refs[na:2 * na]
        send_sems, recv_sems, local_sems = refs[2 * na:]
        mx, my, mc = lax.axis_index("x"), lax.axis_index("y"), lax.axis_index("c")
        if group == "c":
            me = mc
            peers = [((mx, my, 1 - mc), 1 - mc)]
        else:
            me = 2 * mx + my
            peers = []
            for r in range(1, 4):
                px = 1 - mx if r & 2 else mx
                py = 1 - my if r & 1 else my
                peers.append(((px, py, mc), 2 * px + py))

        def copy(a, k, dev, src_row, dst_row):
            return pltpu.make_async_remote_copy(
                src_ref=buf_refs[a].at[src_row], dst_ref=out_refs[a].at[dst_row] if keep_own else out_refs[a],
                send_sem=send_sems.at[a, k], recv_sem=recv_sems.at[a, k], device_id=dev, device_id_type=MESH)

        sends = [copy(a, k, dev, idx, me) for k, (dev, idx) in enumerate(peers) for a in range(na)]
        for cp in sends:
            cp.start()
        own = []
        if keep_own:
            own = [pltpu.make_async_copy(buf_refs[a].at[me], out_refs[a].at[me], local_sems.at[a]) for a in range(na)]
        for cp in own:
            cp.start()
        for k, (dev, idx) in enumerate(peers):
            for a in range(na):
                copy(a, k, dev, idx, idx).wait_recv()
        for cp in sends:
            cp.wait_send()
        for cp in own:
            cp.wait()

    return pl.pallas_call(
        body, name=name, out_shape=[SDS(b.shape if keep_own else b.shape[1:], b.dtype) for b in bufs],
        in_specs=[_ANY] * na, out_specs=[_ANY] * na,
        scratch_shapes=[pltpu.SemaphoreType.DMA((na, g - 1)), pltpu.SemaphoreType.DMA((na, g - 1)),
                        pltpu.SemaphoreType.DMA((na,))],
    )(*bufs)


def _sum_own_recv(buf, recv, core, tr, name):
    _, r, c = buf.shape

    def body(core_ref, own_ref, recv_ref, o_ref):
        o_ref[...] = (own_ref[0].astype(f32) + recv_ref[...].astype(f32)).astype(o_ref.dtype)

    row = pl.BlockSpec((tr, c), lambda i, core_ref: (i, 0))
    return pl.pallas_call(
        body, name=name, out_shape=SDS((r, c), buf.dtype), compiler_params=_params(),
        grid_spec=pltpu.PrefetchScalarGridSpec(
            num_scalar_prefetch=1, grid=(r // tr,),
            in_specs=[pl.BlockSpec((1, tr, c), lambda i, core_ref: (core_ref[0], i, 0)), row], out_specs=row),
    )(core, buf, recv)


_HBM = pl.BlockSpec(memory_space=pltpu.HBM)
_SEM = pl.BlockSpec(memory_space=pltpu.SEMAPHORE)
_DATAFLOW = pltpu.SideEffectType.DATAFLOW_SIDE_EFFECTING
N_PEERS = N_DEV - 1


def _device_index():
    return 4 * lax.axis_index("x") + 2 * lax.axis_index("y") + lax.axis_index("c")


def _peers():
    mx, my, mc = lax.axis_index("x"), lax.axis_index("y"), lax.axis_index("c")
    peers = []
    for r in (2, 3, 4, 5, 6, 7, 1):
        px = 1 - mx if r & 4 else mx
        py = 1 - my if r & 2 else my
        pc = 1 - mc if r & 1 else mc
        peers.append(((px, py, pc), 4 * px + 2 * py + pc))
    return 4 * mx + 2 * my + mc, peers


def _push_copy(src_ref, land_ref, send_sems, recv_sems, a, k, dev, src_row, land_row, scatter):
    return pltpu.make_async_remote_copy(
        src_ref=src_ref.at[src_row] if scatter else src_ref, dst_ref=land_ref.at[land_row],
        send_sem=send_sems.at[a * N_PEERS + k], recv_sem=recv_sems.at[a * N_PEERS + k], device_id=dev, device_id_type=MESH)


def _landing(own, me):
    return lax.dynamic_update_index_in_dim(lax.empty((N_DEV,) + own.shape, own.dtype), own, me, 0)


def _push_start(srcs, lands, scatter, name):
    na = len(srcs)

    def body(*refs):
        src_refs, land_refs = refs[:na], refs[na:2 * na]
        send_sems, recv_sems = refs[2 * na], refs[2 * na + 1]
        token = refs[-1]
        me, peers = _peers()
        for a in range(na):
            for k, (dev, idx) in enumerate(peers):
                _push_copy(src_refs[a], land_refs[a], send_sems, recv_sems, a, k, dev, idx, me, scatter).start()
        token[...] = jnp.zeros_like(token)

    hbm = lambda arrs: [pltpu.HBM(a.shape, a.dtype) for a in arrs]
    out = pl.pallas_call(
        body, name=name,
        out_shape=(pltpu.SemaphoreType.DMA((na * N_PEERS,)), pltpu.SemaphoreType.DMA((na * N_PEERS,)), *hbm(srcs), *hbm(lands),
                   SDS((8, LANES), f32)),
        in_specs=[_HBM] * (2 * na),
        out_specs=(_SEM, _SEM, *([_HBM] * (2 * na)), pl.BlockSpec(memory_space=pltpu.VMEM)),
        input_output_aliases={i: 2 + i for i in range(2 * na)},
        compiler_params=pltpu.CompilerParams(has_side_effects=_DATAFLOW),
    )(*[pltpu.with_memory_space_constraint(a, pltpu.HBM) for a in list(srcs) + list(lands)])
    return out[0], out[1], list(out[2:2 + na]), list(out[2 + na:2 + 2 * na]), out[-1]


def _push_wait(send_sems, recv_sems, srcs, lands, which, after, scatter, name):
    nw = len(which)

    def body(*refs):
        src_refs, land_refs = refs[:nw], refs[nw:2 * nw]
        send_sems_, recv_sems_ = refs[2 * nw], refs[2 * nw + 1]
        _, peers = _peers()
        for j, a in enumerate(which):
            for k, (dev, idx) in enumerate(peers):
                cp = _push_copy(src_refs[j], land_refs[j], send_sems_, recv_sems_, a, k, dev, idx, idx, scatter)
                cp.wait_send()
                cp.wait_recv()

    hbm = lambda arrs: [pltpu.HBM(a.shape, a.dtype) for a in arrs]
    out = pl.pallas_call(
        body, name=name, out_shape=(*hbm(srcs), *hbm(lands)),
        in_specs=[_HBM] * (2 * nw) + [_SEM, _SEM, _ANY], out_specs=[_HBM] * (2 * nw),
        input_output_aliases={i: i for i in range(2 * nw)},
        compiler_params=pltpu.CompilerParams(has_side_effects=_DATAFLOW),
    )(*srcs, *lands, send_sems, recv_sems, after)
    return list(out[nw:])


def _sum_rows(parts, tr, name):
    g, r, c = parts.shape

    def body(p_ref, o_ref):
        acc = p_ref[0].astype(f32)
        for k in range(1, g):
            acc = acc + p_ref[k].astype(f32)
        o_ref[...] = acc.astype(o_ref.dtype)

    return pl.pallas_call(
        body, grid=(r // tr,), name=name, in_specs=[pl.BlockSpec((g, tr, c), lambda i: (0, i, 0))],
        out_specs=pl.BlockSpec((tr, c), lambda i: (i, 0)), out_shape=SDS((r, c), parts.dtype), compiler_params=_params(),
    )(parts)


def _adamw(parts, w, m, v, tr, name):
    g, r, c = parts.shape

    def body(p_ref, w_ref, m_ref, v_ref, g_ref, d_ref, m2_ref, v2_ref):
        gr = p_ref[0].astype(f32)
        for k in range(1, g):
            gr = gr + p_ref[k].astype(f32)
        m2 = ADAM_B1 * m_ref[...] + (1.0 - ADAM_B1) * gr
        v2 = ADAM_B2 * v_ref[...] + (1.0 - ADAM_B2) * (gr * gr)
        m_hat = m2 / (1.0 - ADAM_B1 ** ADAM_STEP)
        v_hat = v2 / (1.0 - ADAM_B2 ** ADAM_STEP)
        g_ref[...] = gr
        d_ref[...] = -ADAM_LR * (m_hat / (jnp.sqrt(v_hat) + ADAM_EPS) + ADAM_WD * w_ref[...])
        m2_ref[...] = m2
        v2_ref[...] = v2

    row = pl.BlockSpec((tr, c), lambda i: (i, 0))
    return pl.pallas_call(
        body, grid=(r // tr,), name=name, in_specs=[pl.BlockSpec((g, tr, c), lambda i: (0, i, 0)), row, row, row],
        out_specs=[row] * 4, out_shape=[SDS((r, c), f32)] * 4, compiler_params=_params(),
    )(parts, w, m, v)


_SHARD_AXIS = dict(meta_tokens=1, norm_gains=2, a_w_in=2, a_lb_logits=1, a_head_norm=1, a_w_out=1, kv_w=1,
                   b_w_q=1, b_w_out=1, ffn_w_up=2, ffn_conv=2, ffn_w_down=1)
_MATRICES = ("a_w_in", "a_w_out", "kv_w", "b_w_q", "b_w_out", "ffn_w_up", "ffn_w_down")
_VECTORS = ("meta_tokens", "norm_gains", "a_lb_logits", "a_head_norm", "ffn_conv")
_SHARDED = tuple(_SHARD_AXIS)
_REPLICATED = ("kv_norm", "fg_b")
_ROW_TILE_CAP = 512


def _pack(arrs, dtype, cols, row_mult):
    lead = arrs[0].shape[:-1] if arrs[0].ndim > 1 else ()
    flat = jnp.concatenate([a.astype(dtype) for a in arrs], axis=-1)
    size = flat.shape[-1]
    per = cols * row_mult
    total = -(-size // per) * per
    flat = jnp.pad(flat, [(0, 0)] * len(lead) + [(0, total - size)])
    return flat.reshape(lead + (total // cols, cols))


def _unpack(flat, shapes):
    out, off = [], 0
    lead = flat.shape[:-1]
    for shp in shapes:
        size = 1
        for s in shp:
            size *= s
        out.append(flat[..., off:off + size].reshape(lead + tuple(shp)))
        off += size
    return out


def _unshard(seg, axis):
    a = jnp.moveaxis(seg, 0, axis)
    shp = a.shape
    return a.reshape(shp[:axis] + (shp[axis] * shp[axis + 1],) + shp[axis + 2:])


def _shard8(full, axis):
    shp = full.shape
    a = full.reshape(shp[:axis] + (N_DEV, shp[axis] // N_DEV) + shp[axis + 1:])
    return jnp.moveaxis(a, axis, 0)


def _rows(a, lead=0):
    return a.reshape(a.shape[:lead] + (-1, a.shape[-1]))


def _core_major(a):
    _, r, c = a.shape
    return a.reshape(4, 2, r, c).transpose(1, 0, 2, 3).reshape(2, 4 * r, c)


def kernel(x, meta_tokens, norm_gains, a_w_in, a_lb_logits, a_head_norm, a_w_out, kv_norm, kv_w, fg_b, b_w_q, b_w_out, ffn_w_up, ffn_conv, ffn_w_down, loss_target, m_meta_tokens, m_norm_gains, m_a_w_in, m_a_lb_logits, m_a_head_norm, m_a_w_out, m_kv_norm, m_kv_w, m_fg_b, m_b_w_q, m_b_w_out, m_ffn_w_up, m_ffn_conv, m_ffn_w_down, v_meta_tokens, v_norm_gains, v_a_w_in, v_a_lb_logits, v_a_head_norm, v_a_w_out, v_kv_norm, v_kv_w, v_fg_b, v_b_w_q, v_b_w_out, v_ffn_w_up, v_ffn_conv, v_ffn_w_down):
    names = ("meta_tokens", "norm_gains", "a_w_in", "a_lb_logits", "a_head_norm", "a_w_out", "kv_norm", "kv_w", "fg_b",
             "b_w_q", "b_w_out", "ffn_w_up", "ffn_conv", "ffn_w_down")
    w = dict(zip(names, (meta_tokens, norm_gains, a_w_in, a_lb_logits, a_head_norm, a_w_out, kv_norm, kv_w, fg_b,
                         b_w_q, b_w_out, ffn_w_up, ffn_conv, ffn_w_down)))
    mom = dict(zip(names, (m_meta_tokens, m_norm_gains, m_a_w_in, m_a_lb_logits, m_a_head_norm, m_a_w_out, m_kv_norm,
                           m_kv_w, m_fg_b, m_b_w_q, m_b_w_out, m_ffn_w_up, m_ffn_conv, m_ffn_w_down)))
    var = dict(zip(names, (v_meta_tokens, v_norm_gains, v_a_w_in, v_a_lb_logits, v_a_head_norm, v_a_w_out, v_kv_norm,
                           v_kv_w, v_fg_b, v_b_w_q, v_b_w_out, v_ffn_w_up, v_ffn_conv, v_ffn_w_down)))

    bl_, seq, d = x.shape
    nm = meta_tokens.shape[0]
    t = nm + seq
    n = bl_ * t
    bh = fg_b.shape[0]
    hd = d // bh
    hp = LANES // hd
    ff = ffn_w_down.shape[1] * N_DEV
    tm = _div_tile(t, TOKEN_TILE_CAP)
    tc = _div_tile(t, 64)
    tn = 512

    vec_pack = _pack([w[k].reshape(-1) for k in _VECTORS], f32, LANES, 8)
    first = _all_gather([w["a_w_in"].astype(bf16), w["a_w_out"].astype(bf16), vec_pack], "gather_first")
    vec_segs = _unpack(first[2].reshape(N_DEV, -1), [w[k].shape for k in _VECTORS])
    small = {k: _unshard(a, _SHARD_AXIS[k]) for k, a in zip(_VECTORS, vec_segs)}
    w_in, w_out_a = _unshard(first[0], _SHARD_AXIS["a_w_in"])[0], _unshard(first[1], _SHARD_AXIS["a_w_out"])[0]
    me = _device_index()
    later_names = ("ffn_w_up", "ffn_w_down", "kv_w", "b_w_q", "b_w_out", "ffn_w_up", "ffn_w_down")
    later_layer = (0, 0, None, None, None, 1, 1)
    later = [(w[k] if l is None else w[k][l]).astype(bf16) for k, l in zip(later_names, later_layer)]
    later, _ = lax.optimization_barrier((later, first[2]))
    g_send, g_recv, later_src, later_land, _ = _push_start(later, [_landing(a, me) for a in later], False, "gather_rest_start")

    def gathered(which, after, name):
        lands = _push_wait(g_send, g_recv, [later_src[i] for i in which], [later_land[i] for i in which], which, after,
                           False, name)
        return [_unshard(a, _SHARD_AXIS[later_names[i]] - (later_layer[i] is not None)) for i, a in zip(which, lands)]

    gains_box = [small["norm_gains"]]
    gain = lambda l, j: gains_box[0][l, j][None]
    cw_gate, cw_val = small["ffn_conv"][:, :, :ff], small["ffn_conv"][:, :, ff:]
    head_gain = small["a_head_norm"]
    lb = jax.nn.softmax(small["a_lb_logits"], axis=0)[0:1]
    kvn = kv_norm[None]
    fgb_pad = jnp.pad(fg_b, (0, LANES - bh))[None]

    h0 = jnp.concatenate([jnp.broadcast_to(small["meta_tokens"][None], (bl_, nm, d)), x], axis=1).reshape(n, d)

    def ffn_fwd(l, h_in):
        fi = _rms_fwd(h_in, gain(l, 2), tm, f"ffn{l}_norm")
        ug = _mm(fi, w_gate[l], bf16, tm, ff, f"ffn{l}_up_gate")
        uv = _mm(fi, w_val[l], bf16, tm, ff, f"ffn{l}_up_val")
        act = _conv_gate_fwd(ug, uv, cw_gate[l], cw_val[l], bl_, t, tc, f"ffn{l}_conv_gate")
        h_out, mix = _mm_norm_res(act, w_down[l], gain(l, 3), h_in, tm, f"ffn{l}_down")
        return h_out, (h_in, fi, ug, uv, act, mix)

    hn0 = _rms_fwd(h0, gain(0, 0), tm, "a_norm")
    pmat = _mm(hn0, w_in, f32, tm, tn, "a_in_proj")
    og, states = _gla_fwd(pmat, lb, head_gain, bl_, t, nm, "a_gla_fwd")
    h1, mix_a = _mm_norm_res(og, w_out_a, gain(0, 1), h0, tm, "a_out_proj")
    w_gate, w_val, w_down = {}, {}, {}

    def ffn_weights(l, which, after):
        w_up, w_down[l] = gathered(which, after, f"gather_wait_ffn{l}")
        w_gate[l], w_val[l] = w_up[:, :ff], w_up[:, ff:]

    ffn_weights(0, (0, 1), h1)
    h2, ffn0 = ffn_fwd(0, h1)

    w_kv_zf, w_q, w_out_b = gathered((2, 3, 4), h2, "gather_wait_b")
    w_kv, w_zf = w_kv_zf[:, :2 * d], jnp.pad(w_kv_zf[:, 2 * d:], ((0, 0), (0, LANES - bh)))
    w_q, w_out_b = w_q[0], w_out_b[0]
    hk = _rms_fwd(h2, kvn, tm, "kv_norm")
    kvp = _mm(hk, w_kv, bf16, tm, tn, "kv_proj")
    zf, cum = _zf_c(hk, w_zf, fgb_pad, bl_, t, tm, "forget_cumsum")
    ck = _c_key_rows(cum, bl_, t, tm, bh, hp)
    hn1 = _rms_fwd(h2, gain(1, 0), tm, "b_norm")
    q = _mm(hn1, w_q, bf16, tm, tn, "b_q_proj")
    o, lse = _attn_fwd(q, kvp, ck, bl_, t, tm, hd, "b_attn_fwd")
    h3, mix_b = _mm_norm_res(o, w_out_b, gain(1, 1), h2, tm, "b_out_proj")
    ffn_weights(1, (5, 6), h3)
    h4, ffn1 = ffn_fwd(1, h3)

    target = jnp.concatenate([jnp.zeros((bl_, nm, d), f32), loss_target], axis=1).reshape(n, d)
    loss8, dh = _loss_head(h4, target, t, nm, tm, "loss_head")
    loss = lax.psum(loss8[0, 0], ("x", "y", "c"))

    dgain = {}

    def ffn_bwd(l, saved, dh_out):
        h_in, fi, ug, uv, act, mix = saved
        dmix, dgain[l, 3] = _rms_bwd(mix, gain(l, 3), dh_out, None, bf16, tm, f"ffn{l}_down_norm_bwd")
        dact = _mm_nt([(dmix, w_down[l])], bf16, tm, ff, f"ffn{l}_down_dx")
        dw_down = _mm_tn(act, dmix, tm, ff, tn, f"ffn{l}_down_dw")
        dug, duv, dcg, dcv = _conv_gate_bwd(ug, uv, cw_gate[l], cw_val[l], dact, bl_, t, tc, f"ffn{l}_conv_gate_bwd")
        dfi = _mm_nt([(dug, w_gate[l]), (duv, w_val[l])], bf16, tm, 256, f"ffn{l}_up_dx")
        dw_up = jnp.concatenate([_mm_tn(fi, dug, tm, tn, ff, f"ffn{l}_up_gate_dw"),
                                 _mm_tn(fi, duv, tm, tn, ff, f"ffn{l}_up_val_dw")], axis=1)
        dh_in, dgain[l, 2] = _rms_bwd(h_in, gain(l, 2), dfi, dh_out, f32, tm, f"ffn{l}_norm_bwd")
        return dh_in, dw_up, jnp.concatenate([dcg, dcv], axis=1), dw_down

    def shards(full, axis):
        return _rows(_shard8(full, axis), 1).astype(bf16)

    def push_grads(bufs, name):
        lands = [_landing(lax.dynamic_index_in_dim(b, me, 0, keepdims=False), me) for b in bufs]
        s_sem, r_sem, srcs, lands, token = _push_start(bufs, lands, True, name)
        gains_box[0] = gains_box[0] + token[0, 0]
        return s_sem, r_sem, srcs, lands

    def landed(handle, after, name):
        s_sem, r_sem, srcs, lands = handle
        return _push_wait(s_sem, r_sem, srcs, lands, tuple(range(len(srcs))), after, True, name)

    dh, dw_up1, dconv1, dw_down1 = ffn_bwd(1, ffn1, dh)
    push1 = push_grads([shards(dw_up1, 1), shards(dw_down1, 0)], "grad_push_ffn1")

    dmix, dgain[1, 1] = _rms_bwd(mix_b, gain(1, 1), dh, None, bf16, tm, "b_out_norm_bwd")
    do = _mm_nt([(dmix, w_out_b)], bf16, tm, tn, "b_out_dx")
    dw_out_b = _mm_tn(o, dmix, tm, tn, tn, "b_out_dw")
    dq, dk, dv, dck, dcq = _attn_bwd(q, kvp, o, do, lse, ck, bl_, t, tm, hd, "b_attn_bwd")
    dhn1 = _mm_nt([(dq, w_q)], bf16, tm, tn, "b_q_dx")
    dw_q = _mm_tn(hn1, dq, tm, tn, tn, "b_q_dw")
    dh, dgain[1, 0] = _rms_bwd(h2, gain(1, 0), dhn1, dh, f32, tm, "b_norm_bwd")

    dzf, dfgb = _c_bwd(_dc_rows(dck, dcq, bl_, t, bh), zf, bl_, t, tm, "forget_cumsum_bwd")
    dhk = _mm_nt([(dk, w_kv[:, :d]), (dv, w_kv[:, d:]), (dzf, w_zf)], bf16, tm, tn, "kv_dx")
    dw_kv = jnp.concatenate([_mm_tn(hk, dk, tm, tn, tn, "k_dw"), _mm_tn(hk, dv, tm, tn, tn, "v_dw"),
                             _mm_tn(hk, dzf, tm, tn, LANES, "zf_dw")[:, :bh]], axis=1)
    dh, dkvn = _rms_bwd(h2, kvn, dhk, dh, f32, tm, "kv_norm_bwd")
    push2 = push_grads([shards(dw_out_b, 0), shards(dw_q, 0), shards(dw_kv, 1)], "grad_push_b")

    dh, dw_up0, dconv0, dw_down0 = ffn_bwd(0, ffn0, dh)
    push3 = push_grads([shards(dw_up0, 1), shards(dw_down0, 0)], "grad_push_ffn0")

    dmix, dgain[0, 1] = _rms_bwd(mix_a, gain(0, 1), dh, None, bf16, tm, "a_out_norm_bwd")
    dog = _mm_nt([(dmix, w_out_a)], bf16, tm, tn, "a_out_dx")
    dw_out_a = _mm_tn(og, dmix, tm, tn, tn, "a_out_dw")
    dpq, dpf, dpi, dpg, dlb, dhg = _gla_bwd(pmat, states, dog, lb, head_gain, bl_, t, nm, "a_gla_bwd")
    dps = (dpq, dpf, dpi, dpg)
    dhn0 = _mm_nt([(dp, w_in[:, j * d:(j + 1) * d]) for j, dp in enumerate(dps)], bf16, tm, tn, "a_in_dx")
    dw_in = jnp.concatenate([_mm_tn(hn0, dp, tm, tn, tn, f"a_in_dw{j}") for j, dp in enumerate(dps)], axis=1)
    dh, dgain[0, 0] = _rms_bwd(h0, gain(0, 0), dhn0, dh, f32, tm, "a_norm_bwd")

    dh = dh.reshape(bl_, t, d)
    grad_x = dh[:, nm:]
    dl0 = dlb * lb * (1.0 - lb)
    vec_grads = dict(
        meta_tokens=jnp.sum(dh[:, :nm], axis=0),
        norm_gains=jnp.stack([jnp.concatenate([dgain[l, j] for j in range(4)], axis=0) for l in range(2)]),
        a_lb_logits=jnp.concatenate([dl0, -dl0], axis=0), a_head_norm=dhg, ffn_conv=jnp.stack([dconv0, dconv1]))
    vec_send = _pack([_shard8(vec_grads[k], _SHARD_AXIS[k]).reshape(N_DEV, -1) for k in _VECTORS], bf16, LANES, BF16_ROWS)
    push4 = push_grads([shards(dw_out_a, 0), shards(dw_in, 1), vec_send], "grad_push_a")

    g_s, d_s, m_s, v_s = {}, {}, {}, {}
    outs = (g_s, d_s, m_s, v_s)

    def update(part, srcs, label):
        rows = part.shape[1]
        return _adamw(part, *srcs, rows if rows <= _ROW_TILE_CAP else _div_tile(rows, _ROW_TILE_CAP), label)

    def update_matrix(k, part, layer=None):
        pick = (lambda a: a) if layer is None else (lambda a: a[layer])
        label = f"adamw_{k}" if layer is None else f"adamw_{k}{layer}"
        res = update(part, [_rows(pick(src[k])) for src in (w, mom, var)], label)
        return [r.reshape(pick(w[k]).shape) for r in res]

    def put(k, res):
        for dst, r in zip(outs, res):
            dst[k] = r

    up1, down1 = (update_matrix(k, p, 1) for k, p in zip(("ffn_w_up", "ffn_w_down"), landed(push1, gains_box[0], "grad_wait_ffn1")))
    for k, p in zip(("b_w_out", "b_w_q", "kv_w"), landed(push2, up1[0], "grad_wait_b")):
        put(k, update_matrix(k, p))
    up0, down0 = (update_matrix(k, p, 0) for k, p in zip(("ffn_w_up", "ffn_w_down"), landed(push3, g_s["kv_w"], "grad_wait_ffn0")))
    put("ffn_w_up", [jnp.stack(pair) for pair in zip(up0, up1)])
    put("ffn_w_down", [jnp.stack(pair) for pair in zip(down0, down1)])
    part_out_a, part_in, part_vec = landed(push4, down0[0], "grad_wait_a")
    put("a_w_out", update_matrix("a_w_out", part_out_a))
    put("a_w_in", update_matrix("a_w_in", part_in))
    vec_packs = [_pack([src[k].reshape(-1) for k in _VECTORS], f32, LANES, BF16_ROWS) for src in (w, mom, var)]
    vec_shapes = [w[k].shape for k in _VECTORS]
    for dst, r in zip(outs, update(part_vec, vec_packs, "adamw_vectors")):
        dst.update(zip(_VECTORS, _unpack(r.reshape(-1), vec_shapes)))

    rep_local = _pack([dkvn.reshape(-1), dfgb[0, :bh]], f32, LANES, 8)
    rep_parts = _all_gather([rep_local], "gather_replicated_grads")[0]
    rpacks = [_pack([src[k].reshape(-1) for k in _REPLICATED], f32, LANES, 8) for src in (w, mom, var)]
    rres = _adamw(rep_parts, *rpacks, rep_local.shape[0], "adamw_replicated")
    rshapes = [w[k].shape for k in _REPLICATED]
    g_r, d_r, m_r, v_r = ({k: a for k, a in zip(_REPLICATED, _unpack(r.reshape(-1), rshapes))} for r in rres)

    out = [loss, grad_x]
    for sh, rp in ((g_s, g_r), (d_s, d_r), (m_s, m_r), (v_s, v_r)):
        out += [sh[k] if k in sh else rp[k] for k in names]
    return tuple(out)
```

```python
import functools
import math

import jax
import jax.numpy as jnp
from jax import lax
from jax.experimental import pallas as pl
from jax.experimental.pallas import tpu as pltpu

f32 = jnp.float32
bf16 = jnp.bfloat16
SDS = jax.ShapeDtypeStruct

EPS = 1e-6
A_DK = 128
A_CHUNK = 64
GLA_GROUP = 4
TOKEN_TILE_CAP = 1024
LANES = 128
SUBLANES = 8
BF16_ROWS = 16
VMEM_LIMIT = 56 * 1024 * 1024
ADAM_LR, ADAM_B1, ADAM_B2, ADAM_EPS, ADAM_WD, ADAM_STEP = 0.001, 0.9, 0.999, 1e-08, 0.01, 10
N_DEV = 8
MESH = pl.DeviceIdType.MESH

_NT = (((1,), (1,)), ((), ()))
_TN = (((0,), (0,)), ((), ()))
_HI = lax.Precision.HIGHEST


def _params(**kw):
    return pltpu.CompilerParams(vmem_limit_bytes=VMEM_LIMIT, **kw)


def _div_tile(n, cap, mult=BF16_ROWS):
    best = None
    for t in range(mult, min(n, cap) + 1, mult):
        if n % t == 0:
            best = t
    assert best is not None, (n, cap, mult)
    return best


def _bdot(a, b):
    return jnp.dot(a.astype(bf16), b.astype(bf16), preferred_element_type=f32)


def _bdot_nt(a, b):
    return lax.dot_general(a.astype(bf16), b.astype(bf16), _NT, preferred_element_type=f32)


def _bdot_tn(a, b):
    return lax.dot_general(a.astype(bf16), b.astype(bf16), _TN, preferred_element_type=f32)


def _iota2(shape, axis):
    return lax.broadcasted_iota(jnp.int32, shape, axis)


def _cumsum_rows(x):
    n = x.shape[0]
    tri = (_iota2((n, n), 0) >= _iota2((n, n), 1)).astype(f32)
    return jnp.dot(tri, x, precision=_HI, preferred_element_type=f32)


def _revcumsum_rows(x):
    n = x.shape[0]
    tri = (_iota2((n, n), 1) >= _iota2((n, n), 0)).astype(f32)
    return jnp.dot(tri, x, precision=_HI, preferred_element_type=f32)


def _sigmoid(x):
    return 1.0 / (1.0 + jnp.exp(-x))


def _rms_fwd(x, g, tm, name):
    n, d = x.shape

    def body(x_ref, g_ref, o_ref):
        xv = x_ref[...]
        r = lax.rsqrt(jnp.mean(xv * xv, axis=-1, keepdims=True) + EPS)
        o_ref[...] = (xv * r * g_ref[...]).astype(o_ref.dtype)

    return pl.pallas_call(
        body, grid=(n // tm,), name=name,
        in_specs=[pl.BlockSpec((tm, d), lambda i: (i, 0)), pl.BlockSpec((1, d), lambda i: (0, 0))],
        out_specs=pl.BlockSpec((tm, d), lambda i: (i, 0)),
        out_shape=SDS((n, d), bf16), compiler_params=_params(),
    )(x, g)


def _mm(a, w, out_dtype, tm, tn, name):
    n, k = a.shape
    m = w.shape[1]

    def body(a_ref, w_ref, o_ref):
        o_ref[...] = _bdot(a_ref[...], w_ref[...]).astype(o_ref.dtype)

    return pl.pallas_call(
        body, grid=(m // tn, n // tm), name=name,
        in_specs=[pl.BlockSpec((tm, k), lambda j, i: (i, 0)), pl.BlockSpec((k, tn), lambda j, i: (0, j))],
        out_specs=pl.BlockSpec((tm, tn), lambda j, i: (i, j)),
        out_shape=SDS((n, m), out_dtype), compiler_params=_params(),
    )(a, w)


def _mm_norm_res(a, w, g, h, tm, name):
    n, k = a.shape
    d = w.shape[1]

    def body(a_ref, w_ref, g_ref, h_ref, hn_ref, mix_ref):
        mix = _bdot(a_ref[...], w_ref[...])
        r = lax.rsqrt(jnp.mean(mix * mix, axis=-1, keepdims=True) + EPS)
        mix_ref[...] = mix
        hn_ref[...] = h_ref[...] + mix * r * g_ref[...]

    return pl.pallas_call(
        body, grid=(n // tm,), name=name,
        in_specs=[pl.BlockSpec((tm, k), lambda i: (i, 0)), pl.BlockSpec((k, d), lambda i: (0, 0)),
                  pl.BlockSpec((1, d), lambda i: (0, 0)), pl.BlockSpec((tm, d), lambda i: (i, 0))],
        out_specs=[pl.BlockSpec((tm, d), lambda i: (i, 0)), pl.BlockSpec((tm, d), lambda i: (i, 0))],
        out_shape=[SDS((n, d), f32), SDS((n, d), f32)], compiler_params=_params(),
    )(a, w, g, h)


def _rms_bwd(x, g, dy, dh_in, out_dtype, tm, name):
    n, d = x.shape
    has_add = dh_in is not None

    def body(*refs):
        if has_add:
            x_ref, g_ref, dy_ref, dh_ref, o_ref, dg_ref = refs
        else:
            x_ref, g_ref, dy_ref, o_ref, dg_ref = refs
        xv = x_ref[...]
        dyv = dy_ref[...].astype(f32)
        r = lax.rsqrt(jnp.mean(xv * xv, axis=-1, keepdims=True) + EPS)
        xr = xv * r
        gdy = dyv * g_ref[...]
        dx = r * gdy - xr * (r * r) * jnp.mean(xv * gdy, axis=-1, keepdims=True)
        if has_add:
            dx = dx + dh_ref[...]
        o_ref[...] = dx.astype(o_ref.dtype)

        @pl.when(pl.program_id(0) == 0)
        def _():
            dg_ref[...] = jnp.zeros_like(dg_ref)

        dg_ref[...] += jnp.sum(dyv * xr, axis=0, keepdims=True)

    row = pl.BlockSpec((tm, d), lambda i: (i, 0))
    vec = pl.BlockSpec((1, d), lambda i: (0, 0))
    ins = [x, g, dy] + ([dh_in] if has_add else [])
    return pl.pallas_call(
        body, grid=(n // tm,), name=name,
        in_specs=[row, vec, row] + ([row] if has_add else []),
        out_specs=[row, vec],
        out_shape=[SDS((n, d), out_dtype), SDS((1, d), f32)], compiler_params=_params(),
    )(*ins)


def _mm_nt(pairs, out_dtype, tm, tk, name):
    n = pairs[0][0].shape[0]
    k = pairs[0][1].shape[0]
    np_ = len(pairs)

    def body(*refs):
        o_ref = refs[-1]
        acc = None
        for p in range(np_):
            t = _bdot_nt(refs[2 * p][...], refs[2 * p + 1][...])
            acc = t if acc is None else acc + t
        o_ref[...] = acc.astype(o_ref.dtype)

    in_specs, ins = [], []
    for dy, w in pairs:
        m = dy.shape[1]
        in_specs += [pl.BlockSpec((tm, m), lambda j, i: (i, 0)), pl.BlockSpec((tk, m), lambda j, i: (j, 0))]
        ins += [dy, w]
    return pl.pallas_call(
        body, grid=(k // tk, n // tm), name=name, in_specs=in_specs,
        out_specs=pl.BlockSpec((tm, tk), lambda j, i: (i, j)),
        out_shape=SDS((n, k), out_dtype), compiler_params=_params(),
    )(*ins)


def _mm_tn(x, dy, tm, tk, tn, name):
    n, k = x.shape
    m = dy.shape[1]

    def body(x_ref, dy_ref, o_ref):
        @pl.when(pl.program_id(2) == 0)
        def _():
            o_ref[...] = jnp.zeros_like(o_ref)

        o_ref[...] += _bdot_tn(x_ref[...], dy_ref[...])

    return pl.pallas_call(
        body, grid=(k // tk, m // tn, n // tm), name=name,
        in_specs=[pl.BlockSpec((tm, tk), lambda a, b, i: (i, a)), pl.BlockSpec((tm, tn), lambda a, b, i: (i, b))],
        out_specs=pl.BlockSpec((tk, tn), lambda a, b, i: (a, b)),
        out_shape=SDS((k, m), f32), compiler_params=_params(),
    )(x, dy)


def _split3(x):
    hi = x.astype(bf16)
    r = x - hi.astype(f32)
    mid = r.astype(bf16)
    return hi, mid, (r - mid.astype(f32)).astype(bf16)


def _mask_dot(mask, x):
    hi, mid, lo = _split3(x)
    dot = lambda p: jnp.dot(mask, p, preferred_element_type=f32)
    return dot(hi) + dot(mid) + dot(lo)


def _chunk_rows(parts, cl):
    tiles = [jnp.broadcast_to(p, (cl, p.shape[1])) for p in parts]
    return tiles[0] if len(tiles) == 1 else jnp.concatenate(tiles, axis=0)


def _cat(parts):
    return parts[0] if len(parts) == 1 else jnp.concatenate(parts, axis=0)


def _gla_group_fwd(qg, fg, vg, lb, st, nc, cl):
    g = nc * cl
    sg = _sigmoid(fg)
    f = lb + (1.0 - lb) * sg
    k = 1.0 - f
    row, col = _iota2((g, g), 0), _iota2((g, g), 1)
    chunk_of = lambda idx: sum((idx >= u * cl).astype(jnp.int32) for u in range(1, nc)) if nc > 1 else 0
    same = chunk_of(row) == chunk_of(col) if nc > 1 else None
    causal = row >= col if nc == 1 else jnp.logical_and(same, row >= col)
    anti = col >= row if nc == 1 else jnp.logical_and(same, col >= row)
    b = _mask_dot(causal.astype(bf16), jnp.log(f))
    bls = [b[(u + 1) * cl - 1:(u + 1) * cl, :] for u in range(nc)]
    ebls = [jnp.exp(x) for x in bls]
    e = jnp.exp(b)
    ei = jnp.exp(-b)
    eo = jnp.exp(_chunk_rows(bls, cl) - b)
    qi, ki, ko = qg * e, k * ei, k * eo
    att = jnp.where(causal, _bdot_nt(qi, ki), 0.0)
    o_intra = _bdot(att, vg)
    sl = [slice(u * cl, (u + 1) * cl) for u in range(nc)]
    ds = [_bdot_tn(vg[s], ko[s]) for s in sl]
    sts = [st]
    for u in range(nc):
        sts.append(sts[u] * ebls[u] + ds[u])
    o = o_intra + _cat([_bdot_nt(qi[sl[u]], sts[u]) for u in range(nc)])
    return dict(sg=sg, f=f, e=e, ei=ei, eo=eo, ebls=ebls, qi=qi, ki=ki, ko=ko, att=att, o=o, sts=sts, causal=causal,
                anti=anti, sl=sl)


def _gla_group(nreal, want):
    while nreal % want:
        want //= 2
    return max(want, 1)


def _head_out(o, ggc, hg):
    r = lax.rsqrt(jnp.mean(o * o, axis=-1, keepdims=True) + EPS)
    return o * r * hg * (ggc * _sigmoid(ggc))


def _gla_fwd(pmat, lb, hg, bl_, t, nm, name):
    n, d4 = pmat.shape
    d = d4 // 4
    nh = d // A_DK
    nreal = (t - nm) // A_CHUNK
    nch = nreal + 1
    un = _gla_group(nreal, GLA_GROUP)

    def body(q_ref, f_ref, i_ref, gg_ref, lb_ref, hg_ref, og_ref, ss_ref):
        lbv, hgv = lb_ref[...], hg_ref[...]

        def run(rows, st, idx, nc, cl):
            w = _gla_group_fwd(q_ref[rows, :], f_ref[rows, :], i_ref[rows, :], lbv, st, nc, cl)
            for u in range(nc):
                ss_ref[0, idx + u] = w["sts"][u]
            og_ref[rows, :] = _head_out(w["o"], gg_ref[rows, :], hgv).astype(og_ref.dtype)
            return w["sts"][nc]

        st = run(pl.ds(0, nm), jnp.zeros((A_DK, A_DK), f32), 0, 1, nm)

        def step(it, st):
            rows = pl.ds(pl.multiple_of(nm + it * (un * A_CHUNK), BF16_ROWS), un * A_CHUNK)
            return run(rows, st, 1 + it * un, un, A_CHUNK)

        lax.fori_loop(0, nreal // un, step, st)

    col = lambda o: pl.BlockSpec((t, A_DK), lambda b, h: (b, o * nh + h))
    vec = pl.BlockSpec((1, A_DK), lambda b, h: (0, h))
    return pl.pallas_call(
        body, grid=(bl_, nh), name=name,
        in_specs=[col(0), col(1), col(2), col(3), vec, vec],
        out_specs=[pl.BlockSpec((t, A_DK), lambda b, h: (b, h)),
                   pl.BlockSpec((1, nch, A_DK, A_DK), lambda b, h: (b * nh + h, 0, 0, 0))],
        out_shape=[SDS((n, d), bf16), SDS((bl_ * nh, nch, A_DK, A_DK), f32)], compiler_params=_params(),
    )(pmat, pmat, pmat, pmat, lb, hg)


def _gla_bwd(pmat, ss, dog, lb, hg, bl_, t, nm, name):
    n, d4 = pmat.shape
    d = d4 // 4
    nh = d // A_DK
    nreal = (t - nm) // A_CHUNK
    nch = nreal + 1
    un = _gla_group(nreal, GLA_GROUP)

    def body(q_ref, f_ref, i_ref, gg_ref, ss_ref, dog_ref, lb_ref, hg_ref,
             dq_ref, df_ref, di_ref, dgg_ref, dlb_ref, dhg_ref):
        lbv, hgv = lb_ref[...], hg_ref[...]

        def run(rows, idx, carry, nc, cl):
            dst, dlb, dhg = carry
            qg, fg, vg, ggc = q_ref[rows, :], f_ref[rows, :], i_ref[rows, :], gg_ref[rows, :]
            dogc = dog_ref[rows, :].astype(f32)
            w = _gla_group_fwd(qg, fg, vg, lbv, ss_ref[0, idx], nc, cl)
            o, qi, ki, ko, sl, sts, ebls = w["o"], w["qi"], w["ki"], w["ko"], w["sl"], w["sts"], w["ebls"]
            r = lax.rsqrt(jnp.mean(o * o, axis=-1, keepdims=True) + EPS)
            sgg = _sigmoid(ggc)
            sil = ggc * sgg
            on = o * r
            dhg = dhg + jnp.sum(dogc * sil * on, axis=0, keepdims=True)
            dgg_ref[rows, :] = (dogc * on * hgv * (sgg * (1.0 + ggc * (1.0 - sgg)))).astype(dgg_ref.dtype)
            tt = dogc * sil * hgv
            do = r * tt - on * (r * r) * jnp.mean(o * tt, axis=-1, keepdims=True)
            xs = [_bdot_tn(do[s], qi[s]) for s in sl]
            dsts = [None] * nc + [dst]
            for u in reversed(range(nc)):
                dsts[u] = dsts[u + 1] * ebls[u] + xs[u]
            datt = jnp.where(w["causal"], _bdot_nt(do, vg), 0.0)
            dv = _bdot_tn(w["att"], do) + _cat([_bdot_nt(ko[sl[u]], dsts[u + 1]) for u in range(nc)])
            dko = _cat([_bdot(vg[sl[u]], dsts[u + 1]) for u in range(nc)])
            dqi = _bdot(datt, ki) + _cat([_bdot(do[sl[u]], sts[u]) for u in range(nc)])
            dki = _bdot_tn(datt, qi)
            dk = dki * w["ei"] + dko * w["eo"]
            dkoko = dko * ko
            db = dqi * qi - dki * ki - dkoko
            rowi = _iota2(db.shape, 0)
            for u in range(nc):
                d_ebl = jnp.sum(dsts[u + 1] * sts[u], axis=0, keepdims=True)
                dbl = jnp.sum(dkoko[sl[u]], axis=0, keepdims=True) + d_ebl * ebls[u]
                db = db + jnp.where(rowi == (u + 1) * cl - 1, dbl, 0.0)
            dlogf = _mask_dot(w["anti"].astype(bf16), db)
            df = dlogf / w["f"] - dk
            sg = w["sg"]
            dq_ref[rows, :] = (dqi * w["e"]).astype(dq_ref.dtype)
            df_ref[rows, :] = (df * (1.0 - lbv) * sg * (1.0 - sg)).astype(df_ref.dtype)
            di_ref[rows, :] = dv.astype(di_ref.dtype)
            dlb = dlb + jnp.sum(df * (1.0 - sg), axis=0, keepdims=True)
            return dsts[0], dlb, dhg

        zero = jnp.zeros((1, A_DK), f32)
        ngroups = nreal // un

        def step(it, carry):
            grp = ngroups - 1 - it
            rows = pl.ds(pl.multiple_of(nm + grp * (un * A_CHUNK), BF16_ROWS), un * A_CHUNK)
            return run(rows, 1 + grp * un, carry, un, A_CHUNK)

        carry = lax.fori_loop(0, ngroups, step, (jnp.zeros((A_DK, A_DK), f32), zero, zero))
        _, dlb, dhg = run(pl.ds(0, nm), 0, carry, 1, nm)

        @pl.when(pl.program_id(1) == 0)
        def _():
            dlb_ref[...] = jnp.zeros_like(dlb_ref)
            dhg_ref[...] = jnp.zeros_like(dhg_ref)

        dlb_ref[...] += dlb
        dhg_ref[...] += dhg

    col = lambda o: pl.BlockSpec((t, A_DK), lambda h, b: (b, o * nh + h))
    blk = pl.BlockSpec((t, A_DK), lambda h, b: (b, h))
    vec = pl.BlockSpec((1, A_DK), lambda h, b: (0, h))
    return pl.pallas_call(
        body, grid=(nh, bl_), name=name,
        in_specs=[col(0), col(1), col(2), col(3),
                  pl.BlockSpec((1, nch, A_DK, A_DK), lambda h, b: (b * nh + h, 0, 0, 0)), blk, vec, vec],
        out_specs=[blk, blk, blk, blk, vec, vec],
        out_shape=[SDS((n, d), bf16)] * 4 + [SDS((1, d), f32)] * 2, compiler_params=_params(),
    )(pmat, pmat, pmat, pmat, ss, dog, lb, hg)


def _shifted(x, halo, before):
    n = x.shape[0]
    both = jnp.concatenate([halo, x] if before else [x, halo], axis=0)
    row, col = _iota2((n, n + BF16_ROWS), 0), _iota2((n, n + BF16_ROWS), 1)
    src = row + BF16_ROWS if before else row
    step = -1 if before else 1
    pick = lambda s: jnp.dot((col == src + step * s).astype(bf16), both, preferred_element_type=f32)
    return pick(1), pick(2)


def _conv3(xb, halo, w):
    x = xb.astype(f32)
    x1, x2 = _shifted(xb, halo, True)
    return x, x1, x2, w[0:1, :] * x2 + w[1:2, :] * x1 + w[2:3, :] * x


def _conv_gate_fwd(ug, uv, cwg, cwv, bl_, t, tc, name):
    n, ff = ug.shape
    nt = t // tc

    def body(ug_ref, uv_ref, wg_ref, wv_ref, a_ref, hg_ref, hv_ref):
        @pl.when(pl.program_id(1) == 0)
        def _():
            hg_ref[...] = jnp.zeros_like(hg_ref)
            hv_ref[...] = jnp.zeros_like(hv_ref)

        xg, xv = ug_ref[...], uv_ref[...]
        cg = _conv3(xg, hg_ref[...], wg_ref[...])[3]
        cv = _conv3(xv, hv_ref[...], wv_ref[...])[3]
        a_ref[...] = (cg * _sigmoid(cg) * cv).astype(a_ref.dtype)
        hg_ref[...] = xg[tc - BF16_ROWS:tc, :].astype(hg_ref.dtype)
        hv_ref[...] = xv[tc - BF16_ROWS:tc, :].astype(hv_ref.dtype)

    row = pl.BlockSpec((tc, ff), lambda b, i: (b * nt + i, 0))
    wsp = pl.BlockSpec((3, ff), lambda b, i: (0, 0))
    return pl.pallas_call(
        body, grid=(bl_, nt), name=name, in_specs=[row, row, wsp, wsp], out_specs=row,
        out_shape=SDS((n, ff), bf16),
        scratch_shapes=[pltpu.VMEM((BF16_ROWS, ff), bf16), pltpu.VMEM((BF16_ROWS, ff), bf16)], compiler_params=_params(),
    )(ug, uv, cwg, cwv)


def _conv_gate_bwd(ug, uv, cwg, cwv, da, bl_, t, tc, name):
    n, ff = ug.shape
    nt = t // tc
    per = tc // BF16_ROWS

    def body(ug_ref, uv_ref, pg_ref, pv_ref, wg_ref, wv_ref, da_ref, dug_ref, duv_ref, dwg_ref, dwv_ref, ng_ref, nv_ref):
        first = jnp.logical_and(pl.program_id(0) == 0, pl.program_id(1) == 0)

        @pl.when(first)
        def _():
            dwg_ref[...] = jnp.zeros_like(dwg_ref)
            dwv_ref[...] = jnp.zeros_like(dwv_ref)

        @pl.when(pl.program_id(1) == 0)
        def _():
            ng_ref[...] = jnp.zeros_like(ng_ref)
            nv_ref[...] = jnp.zeros_like(nv_ref)

        seq_start = pl.program_id(1) == nt - 1
        dav = da_ref[...].astype(f32)

        def half(u_ref, p_ref, w_ref):
            halo = p_ref[...]
            return _conv3(u_ref[...], jnp.where(seq_start, jnp.zeros_like(halo), halo), w_ref[...])

        xg, xg1, xg2, cg = half(ug_ref, pg_ref, wg_ref)
        xv, xv1, xv2, cv = half(uv_ref, pv_ref, wv_ref)
        sg = _sigmoid(cg)
        dcg = dav * cv * (sg * (1.0 + cg * (1.0 - sg)))
        dcv = dav * (cg * sg)

        def back(dc, x, x1, x2, w_ref, nx_ref, du_ref, dw_ref):
            w = w_ref[...]
            dcb = dc.astype(bf16)
            dc1, dc2 = _shifted(dcb, nx_ref[...], False)
            du = w[2:3, :] * dc + w[1:2, :] * dc1 + w[0:1, :] * dc2
            du_ref[...] = du.astype(du_ref.dtype)
            dw_ref[0:1, :] += jnp.sum(dc * x2, axis=0, keepdims=True)
            dw_ref[1:2, :] += jnp.sum(dc * x1, axis=0, keepdims=True)
            dw_ref[2:3, :] += jnp.sum(dc * x, axis=0, keepdims=True)
            nx_ref[...] = dcb[0:BF16_ROWS, :]

        back(dcg, xg, xg1, xg2, wg_ref, ng_ref, dug_ref, dwg_ref)
        back(dcv, xv, xv1, xv2, wv_ref, nv_ref, duv_ref, dwv_ref)

    row = pl.BlockSpec((tc, ff), lambda b, i: (b * nt + nt - 1 - i, 0))
    prev = pl.BlockSpec((BF16_ROWS, ff), lambda b, i: (jnp.maximum((b * nt + nt - 1 - i) * per - 1, 0), 0))
    wsp = pl.BlockSpec((3, ff), lambda b, i: (0, 0))
    return pl.pallas_call(
        body, grid=(bl_, nt), name=name, in_specs=[row, row, prev, prev, wsp, wsp, row],
        out_specs=[row, row, wsp, wsp],
        out_shape=[SDS((n, ff), bf16), SDS((n, ff), bf16), SDS((3, ff), f32), SDS((3, ff), f32)],
        scratch_shapes=[pltpu.VMEM((BF16_ROWS, ff), bf16), pltpu.VMEM((BF16_ROWS, ff), bf16)], compiler_params=_params(),
    )(ug, uv, ug, uv, cwg, cwv, da)


def _zf_c(hk, wzf, fgb, bl_, t, tm, name):
    n, d = hk.shape
    nt = t // tm

    def body(hk_ref, w_ref, b_ref, zf_ref, c_ref, carry_ref):
        @pl.when(pl.program_id(1) == 0)
        def _():
            carry_ref[...] = jnp.zeros_like(carry_ref)

        z = _bdot(hk_ref[...], w_ref[...]) + b_ref[...]
        ls = jnp.minimum(z, 0.0) - jnp.log(1.0 + jnp.exp(-jnp.abs(z)))
        c = _cumsum_rows(ls) + carry_ref[...]
        zf_ref[...] = z
        c_ref[...] = c
        carry_ref[...] = c[tm - 1:tm, :]

    row = lambda w: pl.BlockSpec((tm, w), lambda b, i: (b * nt + i, 0))
    return pl.pallas_call(
        body, grid=(bl_, nt), name=name,
        in_specs=[row(d), pl.BlockSpec((d, LANES), lambda b, i: (0, 0)), pl.BlockSpec((1, LANES), lambda b, i: (0, 0))],
        out_specs=[row(LANES), row(LANES)],
        out_shape=[SDS((n, LANES), f32), SDS((n, LANES), f32)],
        scratch_shapes=[pltpu.VMEM((1, LANES), f32)], compiler_params=_params(),
    )(hk, wzf, fgb)


def _c_bwd(dc, zf, bl_, t, tm, name):
    n = dc.shape[0]
    nt = t // tm

    def body(dc_ref, zf_ref, dzf_ref, dfg_ref, carry_ref):
        @pl.when(jnp.logical_and(pl.program_id(0) == 0, pl.program_id(1) == 0))
        def _():
            dfg_ref[...] = jnp.zeros_like(dfg_ref)

        @pl.when(pl.program_id(1) == 0)
        def _():
            carry_ref[...] = jnp.zeros_like(carry_ref)

        rc = _revcumsum_rows(dc_ref[...]) + carry_ref[...]
        dz = rc * _sigmoid(-zf_ref[...])
        dzf_ref[...] = dz.astype(dzf_ref.dtype)
        dfg_ref[...] += jnp.sum(dz, axis=0, keepdims=True)
        carry_ref[...] = rc[0:1, :]

    row = pl.BlockSpec((tm, LANES), lambda b, i: (b * nt + nt - 1 - i, 0))
    vec = pl.BlockSpec((1, LANES), lambda b, i: (0, 0))
    return pl.pallas_call(
        body, grid=(bl_, nt), name=name, in_specs=[row, row], out_specs=[row, vec],
        out_shape=[SDS((n, LANES), bf16), SDS((1, LANES), f32)],
        scratch_shapes=[pltpu.VMEM((1, LANES), f32)], compiler_params=_params(),
    )(dc, zf)


def _is_pow2(x):
    m, _ = math.frexp(x)
    return m == 0.5


def _prescale(qh, scale):
    return (qh.astype(f32) * scale).astype(bf16)


def _attn_fwd(q, kv, ck, bl_, t, tq, hd, name):
    n, d = q.shape
    npair = d // LANES
    hp = LANES // hd
    nq = t // tq
    scale = 1.0 / (hd ** 0.5)

    pre = _is_pow2(scale)

    def body(q_ref, k_ref, v_ref, ck_ref, o_ref, lse_ref):
        i = pl.program_id(2)
        diag = _iota2((tq, tq), 0) >= _iota2((tq, tq), 1)
        for hh in range(hp):
            lanes = slice(hh * hd, (hh + 1) * hd)
            qh = _prescale(q_ref[:, lanes], scale) if pre else q_ref[:, lanes]

            def block(j, carry, masked, lanes=lanes, qh=qh, hh=hh):
                m, l, acc = carry
                rows = pl.ds(pl.multiple_of(j * tq, BF16_ROWS), tq)
                s = _bdot_nt(qh, k_ref[rows, lanes])
                s = (s if pre else s * scale) - ck_ref[0, 0, j, hh:hh + 1, :]
                if masked:
                    s = jnp.where(diag, s, -1e30)
                m2 = jnp.maximum(m, jnp.max(s, axis=-1, keepdims=True))
                p = jnp.exp(s - m2)
                a = jnp.exp(m - m2)
                return m2, a * l + jnp.sum(p, axis=-1, keepdims=True), a * acc + _bdot(p, v_ref[rows, lanes])

            init = (jnp.full((tq, 1), -1e30, f32), jnp.zeros((tq, 1), f32), jnp.zeros((tq, hd), f32))
            carry = lax.fori_loop(0, i, functools.partial(block, masked=False), init)
            m, l, acc = block(i, carry, True)
            o_ref[:, lanes] = (acc / l).astype(o_ref.dtype)
            lse_ref[:, lanes] = jnp.broadcast_to(m + jnp.log(l), (tq, hd))

    nk = nq
    return pl.pallas_call(
        body, grid=(bl_, npair, nq), name=name,
        in_specs=[pl.BlockSpec((tq, LANES), lambda b, p, i: (b * nq + i, p)),
                  pl.BlockSpec((t, LANES), lambda b, p, i: (b, p)),
                  pl.BlockSpec((t, LANES), lambda b, p, i: (b, npair + p)),
                  pl.BlockSpec((1, 1, nk, hp, tq), lambda b, p, i: (b, p, 0, 0, 0))],
        out_specs=[pl.BlockSpec((tq, LANES), lambda b, p, i: (b * nq + i, p)),
                   pl.BlockSpec((tq, LANES), lambda b, p, i: (b * nq + i, p))],
        out_shape=[SDS((n, d), f32), SDS((n, d), f32)], compiler_params=_params(),
    )(q, kv, kv, ck)


def _attn_bwd(q, kv, o, do, lse, ck, bl_, t, tq, hd, name):
    n, d = q.shape
    npair = d // LANES
    hp = LANES // hd
    nq = t // tq
    scale = 1.0 / (hd ** 0.5)

    pre = _is_pow2(scale)

    def body(q_ref, k_ref, v_ref, o_ref, do_ref, lse_ref, ck_ref, dq_ref, dk_ref, dv_ref, dck_ref, dcq_ref):
        j = pl.program_id(2)

        @pl.when(j == 0)
        def _():
            dq_ref[...] = jnp.zeros_like(dq_ref)
            dcq_ref[...] = jnp.zeros_like(dcq_ref)

        diag = _iota2((tq, tq), 0) >= _iota2((tq, tq), 1)
        for hh in range(hp):
            lanes = slice(hh * hd, (hh + 1) * hd)
            kh = k_ref[:, lanes]
            vh = v_ref[:, lanes]
            kt = kh.astype(f32).T.astype(bf16)
            cs = ck_ref[0, 0, 0, hh:hh + 1, :]

            def block(i, carry, masked, lanes=lanes, kh=kh, vh=vh, kt=kt, cs=cs, hh=hh):
                dkt, dvt, dcs = carry
                rows = pl.ds(pl.multiple_of(i * tq, BF16_ROWS), tq)
                qh = _prescale(q_ref[rows, lanes], scale) if pre else q_ref[rows, lanes]
                doh = do_ref[rows, lanes]
                s = _bdot_nt(qh, kh)
                s = (s if pre else s * scale) - cs
                if masked:
                    s = jnp.where(diag, s, -1e30)
                p = jnp.exp(s - lse_ref[rows, hh * hd:hh * hd + 1])
                delta = jnp.sum(doh.astype(f32) * o_ref[rows, lanes].astype(f32), axis=-1, keepdims=True)
                ds = p * (_bdot_nt(doh, vh) - delta)
                dsb = ds.astype(bf16)
                dq_ref[rows, lanes] += _bdot_nt(kt, dsb).T * scale
                dcq_ref[0, rows, hh:hh + 1] += jnp.sum(ds, axis=-1, keepdims=True)
                dkq = _bdot_tn(qh, dsb)
                return (dkt + (dkq if pre else dkq * scale), dvt + _bdot_tn(doh, p), dcs - jnp.sum(ds, axis=0, keepdims=True))

            init = (jnp.zeros((hd, tq), f32), jnp.zeros((hd, tq), f32), jnp.zeros((1, tq), f32))
            dkt, dvt, dcs = lax.fori_loop(j + 1, nq, functools.partial(block, masked=False), block(j, init, True))
            dk_ref[:, lanes] = dkt.T.astype(dk_ref.dtype)
            dv_ref[:, lanes] = dvt.T.astype(dv_ref.dtype)
            dck_ref[0, 0, 0, hh:hh + 1, :] = dcs

    whole = lambda c0: pl.BlockSpec((t, LANES), lambda b, p, j: (b, c0 + p))
    tile = lambda c0: pl.BlockSpec((tq, LANES), lambda b, p, j: (b * nq + j, c0 + p))
    ckspec = pl.BlockSpec((1, 1, 1, hp, tq), lambda b, p, j: (b, p, j, 0, 0))
    cqspec = pl.BlockSpec((1, t, hp), lambda b, p, j: (p, b, 0))
    return pl.pallas_call(
        body, grid=(bl_, npair, nq), name=name,
        in_specs=[whole(0), tile(0), tile(npair), whole(0), whole(0), whole(0), ckspec],
        out_specs=[whole(0), tile(0), tile(0), ckspec, cqspec],
        out_shape=[SDS((n, d), f32), SDS((n, d), bf16), SDS((n, d), bf16), SDS((bl_, npair, nq, hp, tq), f32),
                   SDS((npair, n, hp), f32)],
        compiler_params=_params(),
    )(q, kv, kv, o, do, lse, ck)


def _loss_head(h, target, t, nm, tm, name):
    n, d = h.shape
    nt = t // tm

    def body(h_ref, t_ref, loss_ref, dh_ref):
        i = pl.program_id(0)

        @pl.when(i == 0)
        def _():
            loss_ref[...] = jnp.zeros_like(loss_ref)

        pos = (i % nt) * tm + _iota2((tm, d), 0)
        err = jnp.where(pos >= nm, h_ref[...] - t_ref[...], 0.0)
        dh_ref[...] = err * (1.0 / d)
        loss_ref[...] += 0.5 * jnp.sum(jnp.mean(err * err, axis=-1, keepdims=True))

    row = pl.BlockSpec((tm, d), lambda i: (i, 0))
    return pl.pallas_call(
        body, grid=(n // tm,), name=name, in_specs=[row, row],
        out_specs=[pl.BlockSpec((8, LANES), lambda i: (0, 0)), row],
        out_shape=[SDS((8, LANES), f32), SDS((n, d), f32)], compiler_params=_params(),
    )(h, target)


def _c_key_rows(c, bl_, t, tq, bh, hp):
    npair = bh // hp
    nk = t // tq
    return c[:, :bh].reshape(bl_, nk, tq, npair, hp).transpose(0, 3, 1, 4, 2)


def _dc_rows(dck, dcq, bl_, t, bh):
    d = dck.transpose(0, 2, 4, 1, 3).reshape(bl_ * t, bh) + dcq.transpose(1, 0, 2).reshape(bl_ * t, bh)
    return jnp.pad(d, ((0, 0), (0, LANES - bh)))


_ANY = pl.BlockSpec(memory_space=pl.ANY)


def _all_gather(xs, name):
    na = len(xs)

    def body(*refs):
        x_refs, out_refs = refs[:na], refs[na:2 * na]
        send_sems, recv_sems, local_sems = refs[2 * na:]
        mx, my, mc = lax.axis_index("x"), lax.axis_index("y"), lax.axis_index("c")
        me, sibling = (mx, my, mc), (mx, my, 1 - mc)
        chips = [(1 - mx, my), (mx, 1 - my), (1 - mx, 1 - my)]

        def copy(a, k, block, to, own=False):
            px, py, pc = block
            rows = out_refs[a].at[4 * px + 2 * py + pc]
            return pltpu.make_async_remote_copy(
                src_ref=x_refs[a] if own else rows, dst_ref=rows,
                send_sem=send_sems.at[a, k], recv_sem=recv_sems.at[a, k], device_id=to, device_id_type=MESH)

        arrays = range(na)
        mine = [pltpu.make_async_copy(x_refs[a], out_refs[a].at[4 * mx + 2 * my + mc], local_sems.at[a]) for a in arrays]
        for cp in mine:
            cp.start()
        first = [copy(a, 1 + j, me, (*chip, mc), own=True) for j, chip in enumerate(chips) for a in arrays]
        first += [copy(a, 0, me, sibling, own=True) for a in arrays]
        for cp in first:
            cp.start()
        passed = []
        for j, chip in enumerate(chips):
            for a in arrays:
                copy(a, 1 + j, (*chip, mc), me).wait_recv()
                cp = copy(a, 4 + j, (*chip, mc), sibling)
                cp.start()
                passed.append(cp)
        for a in arrays:
            copy(a, 0, sibling, me).wait_recv()
        for j, chip in enumerate(chips):
            for a in arrays:
                copy(a, 4 + j, (*chip, 1 - mc), me).wait_recv()
        for cp in first + passed:
            cp.wait_send()
        for cp in mine:
            cp.wait()

    return pl.pallas_call(
        body, name=name, out_shape=[SDS((N_DEV,) + x.shape, x.dtype) for x in xs],
        in_specs=[_ANY] * na, out_specs=[_ANY] * na,
        scratch_shapes=[pltpu.SemaphoreType.DMA((na, 7)), pltpu.SemaphoreType.DMA((na, 7)), pltpu.SemaphoreType.DMA((na,))],
    )(*xs)


_HBM = pl.BlockSpec(memory_space=pltpu.HBM)
_SEM = pl.BlockSpec(memory_space=pltpu.SEMAPHORE)
_DATAFLOW = pltpu.SideEffectType.DATAFLOW_SIDE_EFFECTING
N_PEERS = N_DEV - 1


def _device_index():
    return 4 * lax.axis_index("x") + 2 * lax.axis_index("y") + lax.axis_index("c")


def _peers():
    mx, my, mc = lax.axis_index("x"), lax.axis_index("y"), lax.axis_index("c")
    peers = []
    for r in (2, 3, 4, 5, 6, 7, 1):
        px = 1 - mx if r & 4 else mx
        py = 1 - my if r & 2 else my
        pc = 1 - mc if r & 1 else mc
        peers.append(((px, py, pc), 4 * px + 2 * py + pc))
    return 4 * mx + 2 * my + mc, peers


def _push_copy(src_ref, land_ref, send_sems, recv_sems, a, k, dev, src_row, land_row, scatter):
    return pltpu.make_async_remote_copy(
        src_ref=src_ref.at[src_row] if scatter else src_ref, dst_ref=land_ref.at[land_row],
        send_sem=send_sems.at[a * N_PEERS + k], recv_sem=recv_sems.at[a * N_PEERS + k], device_id=dev, device_id_type=MESH)


def _landing(own, me):
    return lax.dynamic_update_index_in_dim(lax.empty((N_DEV,) + own.shape, own.dtype), own, me, 0)


def _push_start(srcs, lands, scatter, name):
    na = len(srcs)

    def body(*refs):
        src_refs, land_refs = refs[:na], refs[na:2 * na]
        send_sems, recv_sems = refs[2 * na], refs[2 * na + 1]
        token = refs[-1]
        me, peers = _peers()
        for a in range(na):
            for k, (dev, idx) in enumerate(peers):
                _push_copy(src_refs[a], land_refs[a], send_sems, recv_sems, a, k, dev, idx, me, scatter).start()
        token[...] = jnp.zeros_like(token)

    hbm = lambda arrs: [pltpu.HBM(a.shape, a.dtype) for a in arrs]
    out = pl.pallas_call(
        body, name=name,
        out_shape=(pltpu.SemaphoreType.DMA((na * N_PEERS,)), pltpu.SemaphoreType.DMA((na * N_PEERS,)), *hbm(srcs), *hbm(lands),
                   SDS((8, LANES), f32)),
        in_specs=[_HBM] * (2 * na),
        out_specs=(_SEM, _SEM, *([_HBM] * (2 * na)), pl.BlockSpec(memory_space=pltpu.VMEM)),
        input_output_aliases={i: 2 + i for i in range(2 * na)},
        compiler_params=pltpu.CompilerParams(has_side_effects=_DATAFLOW),
    )(*[pltpu.with_memory_space_constraint(a, pltpu.HBM) for a in list(srcs) + list(lands)])
    return out[0], out[1], list(out[2:2 + na]), list(out[2 + na:2 + 2 * na]), out[-1]


def _push_wait(send_sems, recv_sems, srcs, lands, which, after, scatter, name):
    nw = len(which)

    def body(*refs):
        src_refs, land_refs = refs[:nw], refs[nw:2 * nw]
        send_sems_, recv_sems_ = refs[2 * nw], refs[2 * nw + 1]
        _, peers = _peers()
        for j, a in enumerate(which):
            for k, (dev, idx) in enumerate(peers):
                cp = _push_copy(src_refs[j], land_refs[j], send_sems_, recv_sems_, a, k, dev, idx, idx, scatter)
                cp.wait_send()
                cp.wait_recv()

    hbm = lambda arrs: [pltpu.HBM(a.shape, a.dtype) for a in arrs]
    out = pl.pallas_call(
        body, name=name, out_shape=(*hbm(srcs), *hbm(lands)),
        in_specs=[_HBM] * (2 * nw) + [_SEM, _SEM, _ANY], out_specs=[_HBM] * (2 * nw),
        input_output_aliases={i: i for i in range(2 * nw)},
        compiler_params=pltpu.CompilerParams(has_side_effects=_DATAFLOW),
    )(*srcs, *lands, send_sems, recv_sems, after)
    return list(out[nw:])


def _adamw(parts, w, m, v, tr, name):
    g, r, c = parts.shape

    def body(p_ref, w_ref, m_ref, v_ref, g_ref, d_ref, m2_ref, v2_ref):
        gr = p_ref[0].astype(f32)
        for k in range(1, g):
            gr = gr + p_ref[k].astype(f32)
        m2 = ADAM_B1 * m_ref[...] + (1.0 - ADAM_B1) * gr
        v2 = ADAM_B2 * v_ref[...] + (1.0 - ADAM_B2) * (gr * gr)
        m_hat = m2 / (1.0 - ADAM_B1 ** ADAM_STEP)
        v_hat = v2 / (1.0 - ADAM_B2 ** ADAM_STEP)
        g_ref[...] = gr
        d_ref[...] = -ADAM_LR * (m_hat / (jnp.sqrt(v_hat) + ADAM_EPS) + ADAM_WD * w_ref[...])
        m2_ref[...] = m2
        v2_ref[...] = v2

    row = pl.BlockSpec((tr, c), lambda i: (i, 0))
    return pl.pallas_call(
        body, grid=(r // tr,), name=name, in_specs=[pl.BlockSpec((g, tr, c), lambda i: (0, i, 0)), row, row, row],
        out_specs=[row] * 4, out_shape=[SDS((r, c), f32)] * 4, compiler_params=_params(),
    )(parts, w, m, v)


_SHARD_AXIS = dict(meta_tokens=1, norm_gains=2, a_w_in=2, a_lb_logits=1, a_head_norm=1, a_w_out=1, kv_w=1,
                   b_w_q=1, b_w_out=1, ffn_w_up=2, ffn_conv=2, ffn_w_down=1)
_VECTORS = ("meta_tokens", "norm_gains", "a_lb_logits", "a_head_norm", "ffn_conv")
_REPLICATED = ("kv_norm", "fg_b")
_ROW_TILE_CAP = 512


def _pack(arrs, dtype, cols, row_mult):
    lead = arrs[0].shape[:-1] if arrs[0].ndim > 1 else ()
    flat = jnp.concatenate([a.astype(dtype) for a in arrs], axis=-1)
    size = flat.shape[-1]
    per = cols * row_mult
    total = -(-size // per) * per
    flat = jnp.pad(flat, [(0, 0)] * len(lead) + [(0, total - size)])
    return flat.reshape(lead + (total // cols, cols))


def _unpack(flat, shapes):
    out, off = [], 0
    lead = flat.shape[:-1]
    for shp in shapes:
        size = 1
        for s in shp:
            size *= s
        out.append(flat[..., off:off + size].reshape(lead + tuple(shp)))
        off += size
    return out


def _unshard(seg, axis):
    a = jnp.moveaxis(seg, 0, axis)
    shp = a.shape
    return a.reshape(shp[:axis] + (shp[axis] * shp[axis + 1],) + shp[axis + 2:])


def _shard8(full, axis):
    shp = full.shape
    a = full.reshape(shp[:axis] + (N_DEV, shp[axis] // N_DEV) + shp[axis + 1:])
    return jnp.moveaxis(a, axis, 0)


def _rows(a, lead=0):
    return a.reshape(a.shape[:lead] + (-1, a.shape[-1]))


def kernel(x, meta_tokens, norm_gains, a_w_in, a_lb_logits, a_head_norm, a_w_out, kv_norm, kv_w, fg_b, b_w_q, b_w_out, ffn_w_up, ffn_conv, ffn_w_down, loss_target, m_meta_tokens, m_norm_gains, m_a_w_in, m_a_lb_logits, m_a_head_norm, m_a_w_out, m_kv_norm, m_kv_w, m_fg_b, m_b_w_q, m_b_w_out, m_ffn_w_up, m_ffn_conv, m_ffn_w_down, v_meta_tokens, v_norm_gains, v_a_w_in, v_a_lb_logits, v_a_head_norm, v_a_w_out, v_kv_norm, v_kv_w, v_fg_b, v_b_w_q, v_b_w_out, v_ffn_w_up, v_ffn_conv, v_ffn_w_down):
    names = ("meta_tokens", "norm_gains", "a_w_in", "a_lb_logits", "a_head_norm", "a_w_out", "kv_norm", "kv_w", "fg_b",
             "b_w_q", "b_w_out", "ffn_w_up", "ffn_conv", "ffn_w_down")
    w = dict(zip(names, (meta_tokens, norm_gains, a_w_in, a_lb_logits, a_head_norm, a_w_out, kv_norm, kv_w, fg_b,
                         b_w_q, b_w_out, ffn_w_up, ffn_conv, ffn_w_down)))
    mom = dict(zip(names, (m_meta_tokens, m_norm_gains, m_a_w_in, m_a_lb_logits, m_a_head_norm, m_a_w_out, m_kv_norm,
                           m_kv_w, m_fg_b, m_b_w_q, m_b_w_out, m_ffn_w_up, m_ffn_conv, m_ffn_w_down)))
    var = dict(zip(names, (v_meta_tokens, v_norm_gains, v_a_w_in, v_a_lb_logits, v_a_head_norm, v_a_w_out, v_kv_norm,
                           v_kv_w, v_fg_b, v_b_w_q, v_b_w_out, v_ffn_w_up, v_ffn_conv, v_ffn_w_down)))

    bl_, seq, d = x.shape
    nm = meta_tokens.shape[0]
    t = nm + seq
    n = bl_ * t
    bh = fg_b.shape[0]
    hd = d // bh
    hp = LANES // hd
    ff = ffn_w_down.shape[1] * N_DEV
    tm = _div_tile(t, TOKEN_TILE_CAP)
    tc = _div_tile(t, 64)
    tn = 512

    vec_pack = _pack([w[k].reshape(-1) for k in _VECTORS], f32, LANES, 8)
    first = _all_gather([w["a_w_in"].astype(bf16), w["a_w_out"].astype(bf16), vec_pack], "gather_first")
    vec_segs = _unpack(first[2].reshape(N_DEV, -1), [w[k].shape for k in _VECTORS])
    small = {k: _unshard(a, _SHARD_AXIS[k]) for k, a in zip(_VECTORS, vec_segs)}
    w_in, w_out_a = _unshard(first[0], _SHARD_AXIS["a_w_in"])[0], _unshard(first[1], _SHARD_AXIS["a_w_out"])[0]
    me = _device_index()
    later_names = ("ffn_w_up", "ffn_w_down", "kv_w", "b_w_q", "b_w_out", "ffn_w_up", "ffn_w_down")
    later_layer = (0, 0, None, None, None, 1, 1)
    later = [(w[k] if l is None else w[k][l]).astype(bf16) for k, l in zip(later_names, later_layer)]
    later, _ = lax.optimization_barrier((later, first[2]))
    g_send, g_recv, later_src, later_land, _ = _push_start(later, [_landing(a, me) for a in later], False, "gather_rest_start")

    def gathered(which, after, name):
        lands = _push_wait(g_send, g_recv, [later_src[i] for i in which], [later_land[i] for i in which], which, after,
                           False, name)
        return [_unshard(a, _SHARD_AXIS[later_names[i]] - (later_layer[i] is not None)) for i, a in zip(which, lands)]

    gains_box = [small["norm_gains"]]
    gain = lambda l, j: gains_box[0][l, j][None]
    cw_gate, cw_val = small["ffn_conv"][:, :, :ff], small["ffn_conv"][:, :, ff:]
    head_gain = small["a_head_norm"]
    lb = jax.nn.softmax(small["a_lb_logits"], axis=0)[0:1]
    kvn = kv_norm[None]
    fgb_pad = jnp.pad(fg_b, (0, LANES - bh))[None]

    h0 = jnp.concatenate([jnp.broadcast_to(small["meta_tokens"][None], (bl_, nm, d)), x], axis=1).reshape(n, d)

    def ffn_fwd(l, h_in):
        fi = _rms_fwd(h_in, gain(l, 2), tm, f"ffn{l}_norm")
        ug = _mm(fi, w_gate[l], bf16, tm, ff, f"ffn{l}_up_gate")
        uv = _mm(fi, w_val[l], bf16, tm, ff, f"ffn{l}_up_val")
        act = _conv_gate_fwd(ug, uv, cw_gate[l], cw_val[l], bl_, t, tc, f"ffn{l}_conv_gate")
        h_out, mix = _mm_norm_res(act, w_down[l], gain(l, 3), h_in, tm, f"ffn{l}_down")
        return h_out, (h_in, fi, ug, uv, act, mix)

    hn0 = _rms_fwd(h0, gain(0, 0), tm, "a_norm")
    pmat = _mm(hn0, w_in, f32, tm, tn, "a_in_proj")
    og, states = _gla_fwd(pmat, lb, head_gain, bl_, t, nm, "a_gla_fwd")
    h1, mix_a = _mm_norm_res(og, w_out_a, gain(0, 1), h0, tm, "a_out_proj")
    w_gate, w_val, w_down = {}, {}, {}

    def ffn_weights(l, which, after):
        w_up, w_down[l] = gathered(which, after, f"gather_wait_ffn{l}")
        w_gate[l], w_val[l] = w_up[:, :ff], w_up[:, ff:]

    ffn_weights(0, (0, 1), h1)
    h2, ffn0 = ffn_fwd(0, h1)

    w_kv_zf, w_q, w_out_b = gathered((2, 3, 4), h2, "gather_wait_b")
    w_kv, w_zf = w_kv_zf[:, :2 * d], jnp.pad(w_kv_zf[:, 2 * d:], ((0, 0), (0, LANES - bh)))
    w_q, w_out_b = w_q[0], w_out_b[0]
    hk = _rms_fwd(h2, kvn, tm, "kv_norm")
    kvp = _mm(hk, w_kv, bf16, tm, tn, "kv_proj")
    zf, cum = _zf_c(hk, w_zf, fgb_pad, bl_, t, tm, "forget_cumsum")
    ck = _c_key_rows(cum, bl_, t, tm, bh, hp)
    hn1 = _rms_fwd(h2, gain(1, 0), tm, "b_norm")
    q = _mm(hn1, w_q, bf16, tm, tn, "b_q_proj")
    o, lse = _attn_fwd(q, kvp, ck, bl_, t, tm, hd, "b_attn_fwd")
    h3, mix_b = _mm_norm_res(o, w_out_b, gain(1, 1), h2, tm, "b_out_proj")
    ffn_weights(1, (5, 6), h3)
    h4, ffn1 = ffn_fwd(1, h3)

    target = jnp.concatenate([jnp.zeros((bl_, nm, d), f32), loss_target], axis=1).reshape(n, d)
    loss8, dh = _loss_head(h4, target, t, nm, tm, "loss_head")
    loss = lax.psum(loss8[0, 0], ("x", "y", "c"))

    dgain = {}

    def ffn_bwd(l, saved, dh_out):
        h_in, fi, ug, uv, act, mix = saved
        dmix, dgain[l, 3] = _rms_bwd(mix, gain(l, 3), dh_out, None, bf16, tm, f"ffn{l}_down_norm_bwd")
        dact = _mm_nt([(dmix, w_down[l])], bf16, tm, ff, f"ffn{l}_down_dx")
        dw_down = _mm_tn(act, dmix, tm, ff, tn, f"ffn{l}_down_dw")
        dug, duv, dcg, dcv = _conv_gate_bwd(ug, uv, cw_gate[l], cw_val[l], dact, bl_, t, tc, f"ffn{l}_conv_gate_bwd")
        dfi = _mm_nt([(dug, w_gate[l]), (duv, w_val[l])], bf16, tm, 256, f"ffn{l}_up_dx")
        dw_up = jnp.concatenate([_mm_tn(fi, dug, tm, tn, ff, f"ffn{l}_up_gate_dw"),
                                 _mm_tn(fi, duv, tm, tn, ff, f"ffn{l}_up_val_dw")], axis=1)
        dh_in, dgain[l, 2] = _rms_bwd(h_in, gain(l, 2), dfi, dh_out, f32, tm, f"ffn{l}_norm_bwd")
        return dh_in, dw_up, jnp.concatenate([dcg, dcv], axis=1), dw_down

    def shards(full, axis):
        return _rows(_shard8(full, axis), 1).astype(bf16)

    def push_grads(bufs, name):
        lands = [_landing(lax.dynamic_index_in_dim(b, me, 0, keepdims=False), me) for b in bufs]
        s_sem, r_sem, srcs, lands, token = _push_start(bufs, lands, True, name)
        gains_box[0] = gains_box[0] + token[0, 0]
        return s_sem, r_sem, srcs, lands

    def landed(handle, after, name):
        s_sem, r_sem, srcs, lands = handle
        return _push_wait(s_sem, r_sem, srcs, lands, tuple(range(len(srcs))), after, True, name)

    dh, dw_up1, dconv1, dw_down1 = ffn_bwd(1, ffn1, dh)
    push1 = push_grads([shards(dw_up1, 1), shards(dw_down1, 0)], "grad_push_ffn1")

    dmix, dgain[1, 1] = _rms_bwd(mix_b, gain(1, 1), dh, None, bf16, tm, "b_out_norm_bwd")
    do = _mm_nt([(dmix, w_out_b)], bf16, tm, tn, "b_out_dx")
    dw_out_b = _mm_tn(o, dmix, tm, tn, tn, "b_out_dw")
    dq, dk, dv, dck, dcq = _attn_bwd(q, kvp, o, do, lse, ck, bl_, t, tm, hd, "b_attn_bwd")
    dhn1 = _mm_nt([(dq, w_q)], bf16, tm, tn, "b_q_dx")
    dw_q = _mm_tn(hn1, dq, tm, tn, tn, "b_q_dw")
    dh, dgain[1, 0] = _rms_bwd(h2, gain(1, 0), dhn1, dh, f32, tm, "b_norm_bwd")

    dzf, dfgb = _c_bwd(_dc_rows(dck, dcq, bl_, t, bh), zf, bl_, t, tm, "forget_cumsum_bwd")
    dhk = _mm_nt([(dk, w_kv[:, :d]), (dv, w_kv[:, d:]), (dzf, w_zf)], bf16, tm, tn, "kv_dx")
    dw_kv = jnp.concatenate([_mm_tn(hk, dk, tm, tn, tn, "k_dw"), _mm_tn(hk, dv, tm, tn, tn, "v_dw"),
                             _mm_tn(hk, dzf, tm, tn, LANES, "zf_dw")[:, :bh]], axis=1)
    dh, dkvn = _rms_bwd(h2, kvn, dhk, dh, f32, tm, "kv_norm_bwd")
    push2 = push_grads([shards(dw_out_b, 0), shards(dw_q, 0), shards(dw_kv, 1)], "grad_push_b")

    dh, dw_up0, dconv0, dw_down0 = ffn_bwd(0, ffn0, dh)
    push3 = push_grads([shards(dw_up0, 1), shards(dw_down0, 0)], "grad_push_ffn0")

    dmix, dgain[0, 1] = _rms_bwd(mix_a, gain(0, 1), dh, None, bf16, tm, "a_out_norm_bwd")
    dog = _mm_nt([(dmix, w_out_a)], bf16, tm, tn, "a_out_dx")
    dw_out_a = _mm_tn(og, dmix, tm, tn, tn, "a_out_dw")
    dpq, dpf, dpi, dpg, dlb, dhg = _gla_bwd(pmat, states, dog, lb, head_gain, bl_, t, nm, "a_gla_bwd")
    dps = (dpq, dpf, dpi, dpg)
    dhn0 = _mm_nt([(dp, w_in[:, j * d:(j + 1) * d]) for j, dp in enumerate(dps)], bf16, tm, tn, "a_in_dx")
    dw_in = jnp.concatenate([_mm_tn(hn0, dp, tm, tn, tn, f"a_in_dw{j}") for j, dp in enumerate(dps)], axis=1)
    dh, dgain[0, 0] = _rms_bwd(h0, gain(0, 0), dhn0, dh, f32, tm, "a_norm_bwd")

    dh = dh.reshape(bl_, t, d)
    grad_x = dh[:, nm:]
    dl0 = dlb * lb * (1.0 - lb)
    vec_grads = dict(
        meta_tokens=jnp.sum(dh[:, :nm], axis=0),
        norm_gains=jnp.stack([jnp.concatenate([dgain[l, j] for j in range(4)], axis=0) for l in range(2)]),
        a_lb_logits=jnp.concatenate([dl0, -dl0], axis=0), a_head_norm=dhg, ffn_conv=jnp.stack([dconv0, dconv1]))
    vec_send = _pack([_shard8(vec_grads[k], _SHARD_AXIS[k]).reshape(N_DEV, -1) for k in _VECTORS], bf16, LANES, BF16_ROWS)
    push4 = push_grads([shards(dw_out_a, 0), shards(dw_in, 1), vec_send], "grad_push_a")

    g_s, d_s, m_s, v_s = {}, {}, {}, {}
    outs = (g_s, d_s, m_s, v_s)

    def update(part, srcs, label):
        rows = part.shape[1]
        return _adamw(part, *srcs, rows if rows <= _ROW_TILE_CAP else _div_tile(rows, _ROW_TILE_CAP), label)

    def update_matrix(k, part, layer=None):
        pick = (lambda a: a) if layer is None else (lambda a: a[layer])
        label = f"adamw_{k}" if layer is None else f"adamw_{k}{layer}"
        res = update(part, [_rows(pick(src[k])) for src in (w, mom, var)], label)
        return [r.reshape(pick(w[k]).shape) for r in res]

    def put(k, res):
        for dst, r in zip(outs, res):
            dst[k] = r

    up1, down1 = (update_matrix(k, p, 1) for k, p in zip(("ffn_w_up", "ffn_w_down"), landed(push1, gains_box[0], "grad_wait_ffn1")))
    for k, p in zip(("b_w_out", "b_w_q", "kv_w"), landed(push2, up1[0], "grad_wait_b")):
        put(k, update_matrix(k, p))
    up0, down0 = (update_matrix(k, p, 0) for k, p in zip(("ffn_w_up", "ffn_w_down"), landed(push3, g_s["kv_w"], "grad_wait_ffn0")))
    put("ffn_w_up", [jnp.stack(pair) for pair in zip(up0, up1)])
    put("ffn_w_down", [jnp.stack(pair) for pair in zip(down0, down1)])
    part_out_a, part_in, part_vec = landed(push4, down0[0], "grad_wait_a")
    put("a_w_out", update_matrix("a_w_out", part_out_a))
    put("a_w_in", update_matrix("a_w_in", part_in))
    vec_packs = [_pack([src[k].reshape(-1) for k in _VECTORS], f32, LANES, BF16_ROWS) for src in (w, mom, var)]
    vec_shapes = [w[k].shape for k in _VECTORS]
    for dst, r in zip(outs, update(part_vec, vec_packs, "adamw_vectors")):
        dst.update(zip(_VECTORS, _unpack(r.reshape(-1), vec_shapes)))

    rep_local = _pack([dkvn.reshape(-1), dfgb[0, :bh]], f32, LANES, 8)
    rep_parts = _all_gather([rep_local], "gather_replicated_grads")[0]
    rpacks = [_pack([src[k].reshape(-1) for k in _REPLICATED], f32, LANES, 8) for src in (w, mom, var)]
    rres = _adamw(rep_parts, *rpacks, rep_local.shape[0], "adamw_replicated")
    rshapes = [w[k].shape for k in _REPLICATED]
    g_r, d_r, m_r, v_r = ({k: a for k, a in zip(_REPLICATED, _unpack(r.reshape(-1), rshapes))} for r in rres)

    out = [loss, grad_x]
    for sh, rp in ((g_s, g_r), (d_s, d_r), (m_s, m_r), (v_s, v_r)):
        out += [sh[k] if k in sh else rp[k] for k in names]
    return tuple(out)
```

```python
import functools
import math

import jax
import jax.numpy as jnp
from jax import lax
from jax.experimental import pallas as pl
from jax.experimental.pallas import tpu as pltpu

f32 = jnp.float32
bf16 = jnp.bfloat16
SDS = jax.ShapeDtypeStruct

EPS = 1e-6
A_DK = 128
A_CHUNK = 64
GLA_GROUP = 4
TOKEN_TILE_CAP = 1024
MODEL_TILE_CAP = 1024
LANES = 128
SUBLANES = 8
BF16_ROWS = 16
VMEM_LIMIT = 56 * 1024 * 1024
ADAM_LR, ADAM_B1, ADAM_B2, ADAM_EPS, ADAM_WD, ADAM_STEP = 0.001, 0.9, 0.999, 1e-08, 0.01, 10
N_DEV = 8
MESH = pl.DeviceIdType.MESH

_NT = (((1,), (1,)), ((), ()))
_TN = (((0,), (0,)), ((), ()))
_HI = lax.Precision.HIGHEST


def _params(**kw):
    return pltpu.CompilerParams(vmem_limit_bytes=VMEM_LIMIT, **kw)


def _div_tile(n, cap, mult=BF16_ROWS):
    best = None
    for t in range(mult, min(n, cap) + 1, mult):
        if n % t == 0:
            best = t
    assert best is not None, (n, cap, mult)
    return best


def _bdot(a, b):
    return jnp.dot(a.astype(bf16), b.astype(bf16), preferred_element_type=f32)


def _bdot_nt(a, b):
    return lax.dot_general(a.astype(bf16), b.astype(bf16), _NT, preferred_element_type=f32)


def _bdot_tn(a, b):
    return lax.dot_general(a.astype(bf16), b.astype(bf16), _TN, preferred_element_type=f32)


def _iota2(shape, axis):
    return lax.broadcasted_iota(jnp.int32, shape, axis)


def _cumsum_rows(x):
    n = x.shape[0]
    tri = (_iota2((n, n), 0) >= _iota2((n, n), 1)).astype(f32)
    return jnp.dot(tri, x, precision=_HI, preferred_element_type=f32)


def _revcumsum_rows(x):
    n = x.shape[0]
    tri = (_iota2((n, n), 1) >= _iota2((n, n), 0)).astype(f32)
    return jnp.dot(tri, x, precision=_HI, preferred_element_type=f32)


def _sigmoid(x):
    return 1.0 / (1.0 + jnp.exp(-x))


def _rms_fwd(x, g, tm, name):
    n, d = x.shape

    def body(x_ref, g_ref, o_ref):
        xv = x_ref[...]
        r = lax.rsqrt(jnp.mean(xv * xv, axis=-1, keepdims=True) + EPS)
        o_ref[...] = (xv * r * g_ref[...]).astype(o_ref.dtype)

    return pl.pallas_call(
        body, grid=(n // tm,), name=name,
        in_specs=[pl.BlockSpec((tm, d), lambda i: (i, 0)), pl.BlockSpec((1, d), lambda i: (0, 0))],
        out_specs=pl.BlockSpec((tm, d), lambda i: (i, 0)),
        out_shape=SDS((n, d), bf16), compiler_params=_params(),
    )(x, g)


def _mm(a, w, out_dtype, tm, tn, name):
    n, k = a.shape
    m = w.shape[1]

    def body(a_ref, w_ref, o_ref):
        o_ref[...] = _bdot(a_ref[...], w_ref[...]).astype(o_ref.dtype)

    return pl.pallas_call(
        body, grid=(m // tn, n // tm), name=name,
        in_specs=[pl.BlockSpec((tm, k), lambda j, i: (i, 0)), pl.BlockSpec((k, tn), lambda j, i: (0, j))],
        out_specs=pl.BlockSpec((tm, tn), lambda j, i: (i, j)),
        out_shape=SDS((n, m), out_dtype), compiler_params=_params(),
    )(a, w)


def _mm_norm_res(a, w, g, h, tm, name):
    n, k = a.shape
    d = w.shape[1]

    def body(a_ref, w_ref, g_ref, h_ref, hn_ref, mix_ref):
        mix = _bdot(a_ref[...], w_ref[...])
        r = lax.rsqrt(jnp.mean(mix * mix, axis=-1, keepdims=True) + EPS)
        mix_ref[...] = mix
        hn_ref[...] = h_ref[...] + mix * r * g_ref[...]

    return pl.pallas_call(
        body, grid=(n // tm,), name=name,
        in_specs=[pl.BlockSpec((tm, k), lambda i: (i, 0)), pl.BlockSpec((k, d), lambda i: (0, 0)),
                  pl.BlockSpec((1, d), lambda i: (0, 0)), pl.BlockSpec((tm, d), lambda i: (i, 0))],
        out_specs=[pl.BlockSpec((tm, d), lambda i: (i, 0)), pl.BlockSpec((tm, d), lambda i: (i, 0))],
        out_shape=[SDS((n, d), f32), SDS((n, d), f32)], compiler_params=_params(),
    )(a, w, g, h)


def _rms_bwd(x, g, dy, dh_in, out_dtype, tm, name):
    n, d = x.shape
    has_add = dh_in is not None

    def body(*refs):
        if has_add:
            x_ref, g_ref, dy_ref, dh_ref, o_ref, dg_ref = refs
        else:
            x_ref, g_ref, dy_ref, o_ref, dg_ref = refs
        xv = x_ref[...]
        dyv = dy_ref[...].astype(f32)
        r = lax.rsqrt(jnp.mean(xv * xv, axis=-1, keepdims=True) + EPS)
        xr = xv * r
        gdy = dyv * g_ref[...]
        dx = r * gdy - xr * (r * r) * jnp.mean(xv * gdy, axis=-1, keepdims=True)
        if has_add:
            dx = dx + dh_ref[...]
        o_ref[...] = dx.astype(o_ref.dtype)

        @pl.when(pl.program_id(0) == 0)
        def _():
            dg_ref[...] = jnp.zeros_like(dg_ref)

        dg_ref[...] += jnp.sum(dyv * xr, axis=0, keepdims=True)

    row = pl.BlockSpec((tm, d), lambda i: (i, 0))
    vec = pl.BlockSpec((1, d), lambda i: (0, 0))
    ins = [x, g, dy] + ([dh_in] if has_add else [])
    return pl.pallas_call(
        body, grid=(n // tm,), name=name,
        in_specs=[row, vec, row] + ([row] if has_add else []),
        out_specs=[row, vec],
        out_shape=[SDS((n, d), out_dtype), SDS((1, d), f32)], compiler_params=_params(),
    )(*ins)


def _mm_nt(pairs, out_dtype, tm, tk, name):
    n = pairs[0][0].shape[0]
    k = pairs[0][1].shape[0]
    np_ = len(pairs)

    def body(*refs):
        o_ref = refs[-1]
        acc = None
        for p in range(np_):
            t = _bdot_nt(refs[2 * p][...], refs[2 * p + 1][...])
            acc = t if acc is None else acc + t
        o_ref[...] = acc.astype(o_ref.dtype)

    in_specs, ins = [], []
    for dy, w in pairs:
        m = dy.shape[1]
        in_specs += [pl.BlockSpec((tm, m), lambda j, i: (i, 0)), pl.BlockSpec((tk, m), lambda j, i: (j, 0))]
        ins += [dy, w]
    return pl.pallas_call(
        body, grid=(k // tk, n // tm), name=name, in_specs=in_specs,
        out_specs=pl.BlockSpec((tm, tk), lambda j, i: (i, j)),
        out_shape=SDS((n, k), out_dtype), compiler_params=_params(),
    )(*ins)


def _mm_tn(x, dy, tm, tk, tn, name):
    n, k = x.shape
    m = dy.shape[1]

    def body(x_ref, dy_ref, o_ref):
        @pl.when(pl.program_id(2) == 0)
        def _():
            o_ref[...] = jnp.zeros_like(o_ref)

        o_ref[...] += _bdot_tn(x_ref[...], dy_ref[...])

    return pl.pallas_call(
        body, grid=(k // tk, m // tn, n // tm), name=name,
        in_specs=[pl.BlockSpec((tm, tk), lambda a, b, i: (i, a)), pl.BlockSpec((tm, tn), lambda a, b, i: (i, b))],
        out_specs=pl.BlockSpec((tk, tn), lambda a, b, i: (a, b)),
        out_shape=SDS((k, m), f32), compiler_params=_params(),
    )(x, dy)


def _split3(x):
    hi = x.astype(bf16)
    r = x - hi.astype(f32)
    mid = r.astype(bf16)
    return hi, mid, (r - mid.astype(f32)).astype(bf16)


def _mask_dot(mask, x):
    hi, mid, lo = _split3(x)
    dot = lambda p: jnp.dot(mask, p, preferred_element_type=f32)
    return dot(hi) + dot(mid) + dot(lo)


def _chunk_rows(parts, cl):
    tiles = [jnp.broadcast_to(p, (cl, p.shape[1])) for p in parts]
    return tiles[0] if len(tiles) == 1 else jnp.concatenate(tiles, axis=0)


def _cat(parts):
    return parts[0] if len(parts) == 1 else jnp.concatenate(parts, axis=0)


def _gla_group_fwd(qg, fg, vg, lb, st, nc, cl):
    g = nc * cl
    sg = _sigmoid(fg)
    f = lb + (1.0 - lb) * sg
    k = 1.0 - f
    row, col = _iota2((g, g), 0), _iota2((g, g), 1)
    chunk_of = lambda idx: sum((idx >= u * cl).astype(jnp.int32) for u in range(1, nc)) if nc > 1 else 0
    same = chunk_of(row) == chunk_of(col) if nc > 1 else None
    causal = row >= col if nc == 1 else jnp.logical_and(same, row >= col)
    anti = col >= row if nc == 1 else jnp.logical_and(same, col >= row)
    b = _mask_dot(causal.astype(bf16), jnp.log(f))
    bls = [b[(u + 1) * cl - 1:(u + 1) * cl, :] for u in range(nc)]
    ebls = [jnp.exp(x) for x in bls]
    e = jnp.exp(b)
    ei = jnp.exp(-b)
    eo = jnp.exp(_chunk_rows(bls, cl) - b)
    qi, ki, ko = qg * e, k * ei, k * eo
    att = jnp.where(causal, _bdot_nt(qi, ki), 0.0)
    o_intra = _bdot(att, vg)
    sl = [slice(u * cl, (u + 1) * cl) for u in range(nc)]
    ds = [_bdot_tn(vg[s], ko[s]) for s in sl]
    sts = [st]
    for u in range(nc):
        sts.append(sts[u] * ebls[u] + ds[u])
    o = o_intra + _cat([_bdot_nt(qi[sl[u]], sts[u]) for u in range(nc)])
    return dict(sg=sg, f=f, e=e, ei=ei, eo=eo, ebls=ebls, qi=qi, ki=ki, ko=ko, att=att, o=o, sts=sts, causal=causal,
                anti=anti, sl=sl)


def _gla_group(nreal, want):
    while nreal % want:
        want //= 2
    return max(want, 1)


def _head_out(o, ggc, hg):
    r = lax.rsqrt(jnp.mean(o * o, axis=-1, keepdims=True) + EPS)
    return o * r * hg * (ggc * _sigmoid(ggc))


def _gla_fwd(pmat, lb, hg, bl_, t, nm, name):
    n, d4 = pmat.shape
    d = d4 // 4
    nh = d // A_DK
    nreal = (t - nm) // A_CHUNK
    nch = nreal + 1
    un = _gla_group(nreal, GLA_GROUP)

    def body(q_ref, f_ref, i_ref, gg_ref, lb_ref, hg_ref, og_ref, ss_ref):
        lbv, hgv = lb_ref[...], hg_ref[...]

        def run(rows, st, idx, nc, cl):
            w = _gla_group_fwd(q_ref[rows, :], f_ref[rows, :], i_ref[rows, :], lbv, st, nc, cl)
            for u in range(nc):
                ss_ref[0, idx + u] = w["sts"][u]
            og_ref[rows, :] = _head_out(w["o"], gg_ref[rows, :], hgv).astype(og_ref.dtype)
            return w["sts"][nc]

        st = run(pl.ds(0, nm), jnp.zeros((A_DK, A_DK), f32), 0, 1, nm)

        def step(it, st):
            rows = pl.ds(pl.multiple_of(nm + it * (un * A_CHUNK), BF16_ROWS), un * A_CHUNK)
            return run(rows, st, 1 + it * un, un, A_CHUNK)

        lax.fori_loop(0, nreal // un, step, st)

    col = lambda o: pl.BlockSpec((t, A_DK), lambda b, h: (b, o * nh + h))
    vec = pl.BlockSpec((1, A_DK), lambda b, h: (0, h))
    return pl.pallas_call(
        body, grid=(bl_, nh), name=name,
        in_specs=[col(0), col(1), col(2), col(3), vec, vec],
        out_specs=[pl.BlockSpec((t, A_DK), lambda b, h: (b, h)),
                   pl.BlockSpec((1, nch, A_DK, A_DK), lambda b, h: (b * nh + h, 0, 0, 0))],
        out_shape=[SDS((n, d), bf16), SDS((bl_ * nh, nch, A_DK, A_DK), f32)], compiler_params=_params(),
    )(pmat, pmat, pmat, pmat, lb, hg)


def _gla_bwd(pmat, ss, dog, lb, hg, bl_, t, nm, name):
    n, d4 = pmat.shape
    d = d4 // 4
    nh = d // A_DK
    nreal = (t - nm) // A_CHUNK
    nch = nreal + 1
    un = _gla_group(nreal, GLA_GROUP)

    def body(q_ref, f_ref, i_ref, gg_ref, ss_ref, dog_ref, lb_ref, hg_ref,
             dq_ref, df_ref, di_ref, dgg_ref, dlb_ref, dhg_ref):
        lbv, hgv = lb_ref[...], hg_ref[...]

        def run(rows, idx, carry, nc, cl):
            dst, dlb, dhg = carry
            qg, fg, vg, ggc = q_ref[rows, :], f_ref[rows, :], i_ref[rows, :], gg_ref[rows, :]
            dogc = dog_ref[rows, :].astype(f32)
            w = _gla_group_fwd(qg, fg, vg, lbv, ss_ref[0, idx], nc, cl)
            o, qi, ki, ko, sl, sts, ebls = w["o"], w["qi"], w["ki"], w["ko"], w["sl"], w["sts"], w["ebls"]
            r = lax.rsqrt(jnp.mean(o * o, axis=-1, keepdims=True) + EPS)
            sgg = _sigmoid(ggc)
            sil = ggc * sgg
            on = o * r
            dhg = dhg + jnp.sum(dogc * sil * on, axis=0, keepdims=True)
            dgg_ref[rows, :] = (dogc * on * hgv * (sgg * (1.0 + ggc * (1.0 - sgg)))).astype(dgg_ref.dtype)
            tt = dogc * sil * hgv
            do = r * tt - on * (r * r) * jnp.mean(o * tt, axis=-1, keepdims=True)
            xs = [_bdot_tn(do[s], qi[s]) for s in sl]
            dsts = [None] * nc + [dst]
            for u in reversed(range(nc)):
                dsts[u] = dsts[u + 1] * ebls[u] + xs[u]
            datt = jnp.where(w["causal"], _bdot_nt(do, vg), 0.0)
            dv = _bdot_tn(w["att"], do) + _cat([_bdot_nt(ko[sl[u]], dsts[u + 1]) for u in range(nc)])
            dko = _cat([_bdot(vg[sl[u]], dsts[u + 1]) for u in range(nc)])
            dqi = _bdot(datt, ki) + _cat([_bdot(do[sl[u]], sts[u]) for u in range(nc)])
            dki = _bdot_tn(datt, qi)
            dk = dki * w["ei"] + dko * w["eo"]
            dkoko = dko * ko
            db = dqi * qi - dki * ki - dkoko
            rowi = _iota2(db.shape, 0)
            for u in range(nc):
                d_ebl = jnp.sum(dsts[u + 1] * sts[u], axis=0, keepdims=True)
                dbl = jnp.sum(dkoko[sl[u]], axis=0, keepdims=True) + d_ebl * ebls[u]
                db = db + jnp.where(rowi == (u + 1) * cl - 1, dbl, 0.0)
            dlogf = _mask_dot(w["anti"].astype(bf16), db)
            df = dlogf / w["f"] - dk
            sg = w["sg"]
            dq_ref[rows, :] = (dqi * w["e"]).astype(dq_ref.dtype)
            df_ref[rows, :] = (df * (1.0 - lbv) * sg * (1.0 - sg)).astype(df_ref.dtype)
            di_ref[rows, :] = dv.astype(di_ref.dtype)
            dlb = dlb + jnp.sum(df * (1.0 - sg), axis=0, keepdims=True)
            return dsts[0], dlb, dhg

        zero = jnp.zeros((1, A_DK), f32)
        ngroups = nreal // un

        def step(it, carry):
            grp = ngroups - 1 - it
            rows = pl.ds(pl.multiple_of(nm + grp * (un * A_CHUNK), BF16_ROWS), un * A_CHUNK)
            return run(rows, 1 + grp * un, carry, un, A_CHUNK)

        carry = lax.fori_loop(0, ngroups, step, (jnp.zeros((A_DK, A_DK), f32), zero, zero))
        _, dlb, dhg = run(pl.ds(0, nm), 0, carry, 1, nm)

        @pl.when(pl.program_id(1) == 0)
        def _():
            dlb_ref[...] = jnp.zeros_like(dlb_ref)
            dhg_ref[...] = jnp.zeros_like(dhg_ref)

        dlb_ref[...] += dlb
        dhg_ref[...] += dhg

    col = lambda o: pl.BlockSpec((t, A_DK), lambda h, b: (b, o * nh + h))
    blk = pl.BlockSpec((t, A_DK), lambda h, b: (b, h))
    vec = pl.BlockSpec((1, A_DK), lambda h, b: (0, h))
    return pl.pallas_call(
        body, grid=(nh, bl_), name=name,
        in_specs=[col(0), col(1), col(2), col(3),
                  pl.BlockSpec((1, nch, A_DK, A_DK), lambda h, b: (b * nh + h, 0, 0, 0)), blk, vec, vec],
        out_specs=[blk, blk, blk, blk, vec, vec],
        out_shape=[SDS((n, d), bf16)] * 4 + [SDS((1, d), f32)] * 2, compiler_params=_params(),
    )(pmat, pmat, pmat, pmat, ss, dog, lb, hg)


def _shifted(x, halo, before):
    n = x.shape[0]
    both = jnp.concatenate([halo, x] if before else [x, halo], axis=0)
    row, col = _iota2((n, n + BF16_ROWS), 0), _iota2((n, n + BF16_ROWS), 1)
    src = row + BF16_ROWS if before else row
    step = -1 if before else 1
    pick = lambda s: jnp.dot((col == src + step * s).astype(bf16), both, preferred_element_type=f32)
    return pick(1), pick(2)


def _conv3(xb, halo, w):
    x = xb.astype(f32)
    x1, x2 = _shifted(xb, halo, True)
    return x, x1, x2, w[0:1, :] * x2 + w[1:2, :] * x1 + w[2:3, :] * x


def _conv_gate_fwd(ug, uv, cwg, cwv, bl_, t, tc, name):
    n, ff = ug.shape
    nt = t // tc

    def body(ug_ref, uv_ref, wg_ref, wv_ref, a_ref, hg_ref, hv_ref):
        @pl.when(pl.program_id(1) == 0)
        def _():
            hg_ref[...] = jnp.zeros_like(hg_ref)
            hv_ref[...] = jnp.zeros_like(hv_ref)

        xg, xv = ug_ref[...], uv_ref[...]
        cg = _conv3(xg, hg_ref[...], wg_ref[...])[3]
        cv = _conv3(xv, hv_ref[...], wv_ref[...])[3]
        a_ref[...] = (cg * _sigmoid(cg) * cv).astype(a_ref.dtype)
        hg_ref[...] = xg[tc - BF16_ROWS:tc, :].astype(hg_ref.dtype)
        hv_ref[...] = xv[tc - BF16_ROWS:tc, :].astype(hv_ref.dtype)

    row = pl.BlockSpec((tc, ff), lambda b, i: (b * nt + i, 0))
    wsp = pl.BlockSpec((3, ff), lambda b, i: (0, 0))
    return pl.pallas_call(
        body, grid=(bl_, nt), name=name, in_specs=[row, row, wsp, wsp], out_specs=row,
        out_shape=SDS((n, ff), bf16),
        scratch_shapes=[pltpu.VMEM((BF16_ROWS, ff), bf16), pltpu.VMEM((BF16_ROWS, ff), bf16)], compiler_params=_params(),
    )(ug, uv, cwg, cwv)


def _conv_gate_bwd(ug, uv, cwg, cwv, da, bl_, t, tc, name):
    n, ff = ug.shape
    nt = t // tc
    per = tc // BF16_ROWS

    def body(ug_ref, uv_ref, pg_ref, pv_ref, wg_ref, wv_ref, da_ref, dug_ref, duv_ref, dwg_ref, dwv_ref, ng_ref, nv_ref):
        first = jnp.logical_and(pl.program_id(0) == 0, pl.program_id(1) == 0)

        @pl.when(first)
        def _():
            dwg_ref[...] = jnp.zeros_like(dwg_ref)
            dwv_ref[...] = jnp.zeros_like(dwv_ref)

        @pl.when(pl.program_id(1) == 0)
        def _():
            ng_ref[...] = jnp.zeros_like(ng_ref)
            nv_ref[...] = jnp.zeros_like(nv_ref)

        seq_start = pl.program_id(1) == nt - 1
        dav = da_ref[...].astype(f32)

        def half(u_ref, p_ref, w_ref):
            halo = p_ref[...]
            return _conv3(u_ref[...], jnp.where(seq_start, jnp.zeros_like(halo), halo), w_ref[...])

        xg, xg1, xg2, cg = half(ug_ref, pg_ref, wg_ref)
        xv, xv1, xv2, cv = half(uv_ref, pv_ref, wv_ref)
        sg = _sigmoid(cg)
        dcg = dav * cv * (sg * (1.0 + cg * (1.0 - sg)))
        dcv = dav * (cg * sg)

        def back(dc, x, x1, x2, w_ref, nx_ref, du_ref, dw_ref):
            w = w_ref[...]
            dcb = dc.astype(bf16)
            dc1, dc2 = _shifted(dcb, nx_ref[...], False)
            du = w[2:3, :] * dc + w[1:2, :] * dc1 + w[0:1, :] * dc2
            du_ref[...] = du.astype(du_ref.dtype)
            dw_ref[0:1, :] += jnp.sum(dc * x2, axis=0, keepdims=True)
            dw_ref[1:2, :] += jnp.sum(dc * x1, axis=0, keepdims=True)
            dw_ref[2:3, :] += jnp.sum(dc * x, axis=0, keepdims=True)
            nx_ref[...] = dcb[0:BF16_ROWS, :]

        back(dcg, xg, xg1, xg2, wg_ref, ng_ref, dug_ref, dwg_ref)
        back(dcv, xv, xv1, xv2, wv_ref, nv_ref, duv_ref, dwv_ref)

    row = pl.BlockSpec((tc, ff), lambda b, i: (b * nt + nt - 1 - i, 0))
    prev = pl.BlockSpec((BF16_ROWS, ff), lambda b, i: (jnp.maximum((b * nt + nt - 1 - i) * per - 1, 0), 0))
    wsp = pl.BlockSpec((3, ff), lambda b, i: (0, 0))
    return pl.pallas_call(
        body, grid=(bl_, nt), name=name, in_specs=[row, row, prev, prev, wsp, wsp, row],
        out_specs=[row, row, wsp, wsp],
        out_shape=[SDS((n, ff), bf16), SDS((n, ff), bf16), SDS((3, ff), f32), SDS((3, ff), f32)],
        scratch_shapes=[pltpu.VMEM((BF16_ROWS, ff), bf16), pltpu.VMEM((BF16_ROWS, ff), bf16)], compiler_params=_params(),
    )(ug, uv, ug, uv, cwg, cwv, da)


def _zf_c(hk, wzf, fgb, bl_, t, tm, name):
    n, d = hk.shape
    nt = t // tm

    def body(hk_ref, w_ref, b_ref, zf_ref, c_ref, carry_ref):
        @pl.when(pl.program_id(1) == 0)
        def _():
            carry_ref[...] = jnp.zeros_like(carry_ref)

        z = _bdot(hk_ref[...], w_ref[...]) + b_ref[...]
        ls = jnp.minimum(z, 0.0) - jnp.log(1.0 + jnp.exp(-jnp.abs(z)))
        c = _cumsum_rows(ls) + carry_ref[...]
        zf_ref[...] = z
        c_ref[...] = c
        carry_ref[...] = c[tm - 1:tm, :]

    row = lambda w: pl.BlockSpec((tm, w), lambda b, i: (b * nt + i, 0))
    return pl.pallas_call(
        body, grid=(bl_, nt), name=name,
        in_specs=[row(d), pl.BlockSpec((d, LANES), lambda b, i: (0, 0)), pl.BlockSpec((1, LANES), lambda b, i: (0, 0))],
        out_specs=[row(LANES), row(LANES)],
        out_shape=[SDS((n, LANES), f32), SDS((n, LANES), f32)],
        scratch_shapes=[pltpu.VMEM((1, LANES), f32)], compiler_params=_params(),
    )(hk, wzf, fgb)


def _c_bwd(dc, zf, bl_, t, tm, name):
    n = dc.shape[0]
    nt = t // tm

    def body(dc_ref, zf_ref, dzf_ref, dfg_ref, carry_ref):
        @pl.when(jnp.logical_and(pl.program_id(0) == 0, pl.program_id(1) == 0))
        def _():
            dfg_ref[...] = jnp.zeros_like(dfg_ref)

        @pl.when(pl.program_id(1) == 0)
        def _():
            carry_ref[...] = jnp.zeros_like(carry_ref)

        rc = _revcumsum_rows(dc_ref[...]) + carry_ref[...]
        dz = rc * _sigmoid(-zf_ref[...])
        dzf_ref[...] = dz.astype(dzf_ref.dtype)
        dfg_ref[...] += jnp.sum(dz, axis=0, keepdims=True)
        carry_ref[...] = rc[0:1, :]

    row = pl.BlockSpec((tm, LANES), lambda b, i: (b * nt + nt - 1 - i, 0))
    vec = pl.BlockSpec((1, LANES), lambda b, i: (0, 0))
    return pl.pallas_call(
        body, grid=(bl_, nt), name=name, in_specs=[row, row], out_specs=[row, vec],
        out_shape=[SDS((n, LANES), bf16), SDS((1, LANES), f32)],
        scratch_shapes=[pltpu.VMEM((1, LANES), f32)], compiler_params=_params(),
    )(dc, zf)


def _is_pow2(x):
    m, _ = math.frexp(x)
    return m == 0.5


def _prescale(qh, scale):
    return (qh.astype(f32) * scale).astype(bf16)


def _attn_fwd(q, kv, ck, bl_, t, tq, hd, name):
    n, d = q.shape
    npair = d // LANES
    hp = LANES // hd
    nq = t // tq
    scale = 1.0 / (hd ** 0.5)

    pre = _is_pow2(scale)

    def body(q_ref, k_ref, v_ref, ck_ref, o_ref, lse_ref):
        i = pl.program_id(2)
        diag = _iota2((tq, tq), 0) >= _iota2((tq, tq), 1)
        for hh in range(hp):
            lanes = slice(hh * hd, (hh + 1) * hd)
            qh = _prescale(q_ref[:, lanes], scale) if pre else q_ref[:, lanes]

            def block(j, carry, masked, lanes=lanes, qh=qh, hh=hh):
                m, l, acc = carry
                rows = pl.ds(pl.multiple_of(j * tq, BF16_ROWS), tq)
                s = _bdot_nt(qh, k_ref[rows, lanes])
                s = (s if pre else s * scale) - ck_ref[0, 0, j, hh:hh + 1, :]
                if masked:
                    s = jnp.where(diag, s, -1e30)
                m2 = jnp.maximum(m, jnp.max(s, axis=-1, keepdims=True))
                p = jnp.exp(s - m2)
                a = jnp.exp(m - m2)
                return m2, a * l + jnp.sum(p, axis=-1, keepdims=True), a * acc + _bdot(p, v_ref[rows, lanes])

            init = (jnp.full((tq, 1), -1e30, f32), jnp.zeros((tq, 1), f32), jnp.zeros((tq, hd), f32))
            carry = lax.fori_loop(0, i, functools.partial(block, masked=False), init)
            m, l, acc = block(i, carry, True)
            o_ref[:, lanes] = (acc / l).astype(o_ref.dtype)
            lse_ref[:, lanes] = jnp.broadcast_to(m + jnp.log(l), (tq, hd))

    nk = nq
    return pl.pallas_call(
        body, grid=(bl_, npair, nq), name=name,
        in_specs=[pl.BlockSpec((tq, LANES), lambda b, p, i: (b * nq + i, p)),
                  pl.BlockSpec((t, LANES), lambda b, p, i: (b, p)),
                  pl.BlockSpec((t, LANES), lambda b, p, i: (b, npair + p)),
                  pl.BlockSpec((1, 1, nk, hp, tq), lambda b, p, i: (b, p, 0, 0, 0))],
        out_specs=[pl.BlockSpec((tq, LANES), lambda b, p, i: (b * nq + i, p)),
                   pl.BlockSpec((tq, LANES), lambda b, p, i: (b * nq + i, p))],
        out_shape=[SDS((n, d), f32), SDS((n, d), f32)], compiler_params=_params(),
    )(q, kv, kv, ck)


def _attn_bwd(q, kv, o, do, lse, ck, bl_, t, tq, hd, name):
    n, d = q.shape
    npair = d // LANES
    hp = LANES // hd
    nq = t // tq
    scale = 1.0 / (hd ** 0.5)

    pre = _is_pow2(scale)

    def body(q_ref, k_ref, v_ref, o_ref, do_ref, lse_ref, ck_ref, dq_ref, dk_ref, dv_ref, dck_ref, dcq_ref):
        j = pl.program_id(2)

        @pl.when(j == 0)
        def _():
            dq_ref[...] = jnp.zeros_like(dq_ref)
            dcq_ref[...] = jnp.zeros_like(dcq_ref)

        diag = _iota2((tq, tq), 0) >= _iota2((tq, tq), 1)
        for hh in range(hp):
            lanes = slice(hh * hd, (hh + 1) * hd)
            kh = k_ref[:, lanes]
            vh = v_ref[:, lanes]
            kt = kh.astype(f32).T.astype(bf16)
            cs = ck_ref[0, 0, 0, hh:hh + 1, :]

            def block(i, carry, masked, lanes=lanes, kh=kh, vh=vh, kt=kt, cs=cs, hh=hh):
                dkt, dvt, dcs = carry
                rows = pl.ds(pl.multiple_of(i * tq, BF16_ROWS), tq)
                qh = _prescale(q_ref[rows, lanes], scale) if pre else q_ref[rows, lanes]
                doh = do_ref[rows, lanes]
                s = _bdot_nt(qh, kh)
                s = (s if pre else s * scale) - cs
                if masked:
                    s = jnp.where(diag, s, -1e30)
                p = jnp.exp(s - lse_ref[rows, hh * hd:hh * hd + 1])
                delta = jnp.sum(doh.astype(f32) * o_ref[rows, lanes].astype(f32), axis=-1, keepdims=True)
                ds = p * (_bdot_nt(doh, vh) - delta)
                dsb = ds.astype(bf16)
                dq_ref[rows, lanes] += _bdot_nt(kt, dsb).T * scale
                dcq_ref[0, rows, hh:hh + 1] += jnp.sum(ds, axis=-1, keepdims=True)
                dkq = _bdot_tn(qh, dsb)
                return (dkt + (dkq if pre else dkq * scale), dvt + _bdot_tn(doh, p), dcs - jnp.sum(ds, axis=0, keepdims=True))

            init = (jnp.zeros((hd, tq), f32), jnp.zeros((hd, tq), f32), jnp.zeros((1, tq), f32))
            dkt, dvt, dcs = lax.fori_loop(j + 1, nq, functools.partial(block, masked=False), block(j, init, True))
            dk_ref[:, lanes] = dkt.T.astype(dk_ref.dtype)
            dv_ref[:, lanes] = dvt.T.astype(dv_ref.dtype)
            dck_ref[0, 0, 0, hh:hh + 1, :] = dcs

    whole = lambda c0: pl.BlockSpec((t, LANES), lambda b, p, j: (b, c0 + p))
    tile = lambda c0: pl.BlockSpec((tq, LANES), lambda b, p, j: (b * nq + j, c0 + p))
    ckspec = pl.BlockSpec((1, 1, 1, hp, tq), lambda b, p, j: (b, p, j, 0, 0))
    cqspec = pl.BlockSpec((1, t, hp), lambda b, p, j: (p, b, 0))
    return pl.pallas_call(
        body, grid=(bl_, npair, nq), name=name,
        in_specs=[whole(0), tile(0), tile(npair), whole(0), whole(0), whole(0), ckspec],
        out_specs=[whole(0), tile(0), tile(0), ckspec, cqspec],
        out_shape=[SDS((n, d), f32), SDS((n, d), bf16), SDS((n, d), bf16), SDS((bl_, npair, nq, hp, tq), f32),
                   SDS((npair, n, hp), f32)],
        compiler_params=_params(),
    )(q, kv, kv, o, do, lse, ck)


def _loss_head(h, target, t, nm, tm, name):
    n, d = h.shape
    nt = t // tm

    def body(h_ref, t_ref, loss_ref, dh_ref):
        i = pl.program_id(0)

        @pl.when(i == 0)
        def _():
            loss_ref[...] = jnp.zeros_like(loss_ref)

        pos = (i % nt) * tm + _iota2((tm, d), 0)
        err = jnp.where(pos >= nm, h_ref[...] - t_ref[...], 0.0)
        dh_ref[...] = err * (1.0 / d)
        loss_ref[...] += 0.5 * jnp.sum(jnp.mean(err * err, axis=-1, keepdims=True))

    row = pl.BlockSpec((tm, d), lambda i: (i, 0))
    return pl.pallas_call(
        body, grid=(n // tm,), name=name, in_specs=[row, row],
        out_specs=[pl.BlockSpec((8, LANES), lambda i: (0, 0)), row],
        out_shape=[SDS((8, LANES), f32), SDS((n, d), f32)], compiler_params=_params(),
    )(h, target)


def _c_key_rows(c, bl_, t, tq, bh, hp):
    npair = bh // hp
    nk = t // tq
    return c[:, :bh].reshape(bl_, nk, tq, npair, hp).transpose(0, 3, 1, 4, 2)


def _dc_rows(dck, dcq, bl_, t, bh):
    d = dck.transpose(0, 2, 4, 1, 3).reshape(bl_ * t, bh) + dcq.transpose(1, 0, 2).reshape(bl_ * t, bh)
    return jnp.pad(d, ((0, 0), (0, LANES - bh)))


_ANY = pl.BlockSpec(memory_space=pl.ANY)


def _all_gather(xs, name):
    na = len(xs)

    def body(*refs):
        x_refs, out_refs = refs[:na], refs[na:2 * na]
        send_sems, recv_sems, local_sems = refs[2 * na:]
        mx, my, mc = lax.axis_index("x"), lax.axis_index("y"), lax.axis_index("c")
        me, sibling = (mx, my, mc), (mx, my, 1 - mc)
        chips = [(1 - mx, my), (mx, 1 - my), (1 - mx, 1 - my)]

        def copy(a, k, block, to, own=False):
            px, py, pc = block
            rows = out_refs[a].at[4 * px + 2 * py + pc]
            return pltpu.make_async_remote_copy(
                src_ref=x_refs[a] if own else rows, dst_ref=rows,
                send_sem=send_sems.at[a, k], recv_sem=recv_sems.at[a, k], device_id=to, device_id_type=MESH)

        arrays = range(na)
        mine = [pltpu.make_async_copy(x_refs[a], out_refs[a].at[4 * mx + 2 * my + mc], local_sems.at[a]) for a in arrays]
        for cp in mine:
            cp.start()
        first = [copy(a, 1 + j, me, (*chip, mc), own=True) for j, chip in enumerate(chips) for a in arrays]
        first += [copy(a, 0, me, sibling, own=True) for a in arrays]
        for cp in first:
            cp.start()
        passed = []
        for j, chip in enumerate(chips):
            for a in arrays:
                copy(a, 1 + j, (*chip, mc), me).wait_recv()
                cp = copy(a, 4 + j, (*chip, mc), sibling)
                cp.start()
                passed.append(cp)
        for a in arrays:
            copy(a, 0, sibling, me).wait_recv()
        for j, chip in enumerate(chips):
            for a in arrays:
                copy(a, 4 + j, (*chip, 1 - mc), me).wait_recv()
        for cp in first + passed:
            cp.wait_send()
        for cp in mine:
            cp.wait()

    return pl.pallas_call(
        body, name=name, out_shape=[SDS((N_DEV,) + x.shape, x.dtype) for x in xs],
        in_specs=[_ANY] * na, out_specs=[_ANY] * na,
        scratch_shapes=[pltpu.SemaphoreType.DMA((na, 7)), pltpu.SemaphoreType.DMA((na, 7)), pltpu.SemaphoreType.DMA((na,))],
    )(*xs)


_HBM = pl.BlockSpec(memory_space=pltpu.HBM)
_SEM = pl.BlockSpec(memory_space=pltpu.SEMAPHORE)
_DATAFLOW = pltpu.SideEffectType.DATAFLOW_SIDE_EFFECTING
N_PEERS = N_DEV - 1


def _device_index():
    return 4 * lax.axis_index("x") + 2 * lax.axis_index("y") + lax.axis_index("c")


def _peers():
    mx, my, mc = lax.axis_index("x"), lax.axis_index("y"), lax.axis_index("c")
    peers = []
    for r in (2, 3, 4, 5, 6, 7, 1):
        px = 1 - mx if r & 4 else mx
        py = 1 - my if r & 2 else my
        pc = 1 - mc if r & 1 else mc
        peers.append(((px, py, pc), 4 * px + 2 * py + pc))
    return 4 * mx + 2 * my + mc, peers


def _push_copy(src_ref, land_ref, send_sems, recv_sems, a, k, dev, src_row, land_row, scatter):
    return pltpu.make_async_remote_copy(
        src_ref=src_ref.at[src_row] if scatter else src_ref, dst_ref=land_ref.at[land_row],
        send_sem=send_sems.at[a * N_PEERS + k], recv_sem=recv_sems.at[a * N_PEERS + k], device_id=dev, device_id_type=MESH)


def _landing(own, me):
    return lax.dynamic_update_index_in_dim(lax.empty((N_DEV,) + own.shape, own.dtype), own, me, 0)


def _push_start(srcs, lands, scatter, name):
    na = len(srcs)

    def body(*refs):
        src_refs, land_refs = refs[:na], refs[na:2 * na]
        send_sems, recv_sems = refs[2 * na], refs[2 * na + 1]
        token = refs[-1]
        me, peers = _peers()
        for a in range(na):
            for k, (dev, idx) in enumerate(peers):
                _push_copy(src_refs[a], land_refs[a], send_sems, recv_sems, a, k, dev, idx, me, scatter).start()
        token[...] = jnp.zeros_like(token)

    hbm = lambda arrs: [pltpu.HBM(a.shape, a.dtype) for a in arrs]
    out = pl.pallas_call(
        body, name=name,
        out_shape=(pltpu.SemaphoreType.DMA((na * N_PEERS,)), pltpu.SemaphoreType.DMA((na * N_PEERS,)), *hbm(srcs), *hbm(lands),
                   SDS((8, LANES), f32)),
        in_specs=[_HBM] * (2 * na),
        out_specs=(_SEM, _SEM, *([_HBM] * (2 * na)), pl.BlockSpec(memory_space=pltpu.VMEM)),
        input_output_aliases={i: 2 + i for i in range(2 * na)},
        compiler_params=pltpu.CompilerParams(has_side_effects=_DATAFLOW),
    )(*[pltpu.with_memory_space_constraint(a, pltpu.HBM) for a in list(srcs) + list(lands)])
    return out[0], out[1], list(out[2:2 + na]), list(out[2 + na:2 + 2 * na]), out[-1]


def _push_wait(send_sems, recv_sems, srcs, lands, which, after, scatter, name):
    nw = len(which)

    def body(*refs):
        src_refs, land_refs = refs[:nw], refs[nw:2 * nw]
        send_sems_, recv_sems_ = refs[2 * nw], refs[2 * nw + 1]
        _, peers = _peers()
        for j, a in enumerate(which):
            for k, (dev, idx) in enumerate(peers):
                cp = _push_copy(src_refs[j], land_refs[j], send_sems_, recv_sems_, a, k, dev, idx, idx, scatter)
                cp.wait_send()
                cp.wait_recv()

    hbm = lambda arrs: [pltpu.HBM(a.shape, a.dtype) for a in arrs]
    out = pl.pallas_call(
        body, name=name, out_shape=(*hbm(srcs), *hbm(lands)),
        in_specs=[_HBM] * (2 * nw) + [_SEM, _SEM, _ANY], out_specs=[_HBM] * (2 * nw),
        input_output_aliases={i: i for i in range(2 * nw)},
        compiler_params=pltpu.CompilerParams(has_side_effects=_DATAFLOW),
    )(*srcs, *lands, send_sems, recv_sems, after)
    return list(out[nw:])


def _adamw(parts, w, m, v, tr, name):
    g, r, c = parts.shape

    def body(p_ref, w_ref, m_ref, v_ref, g_ref, d_ref, m2_ref, v2_ref):
        gr = p_ref[0].astype(f32)
        for k in range(1, g):
            gr = gr + p_ref[k].astype(f32)
        m2 = ADAM_B1 * m_ref[...] + (1.0 - ADAM_B1) * gr
        v2 = ADAM_B2 * v_ref[...] + (1.0 - ADAM_B2) * (gr * gr)
        m_hat = m2 / (1.0 - ADAM_B1 ** ADAM_STEP)
        v_hat = v2 / (1.0 - ADAM_B2 ** ADAM_STEP)
        g_ref[...] = gr
        d_ref[...] = -ADAM_LR * (m_hat / (jnp.sqrt(v_hat) + ADAM_EPS) + ADAM_WD * w_ref[...])
        m2_ref[...] = m2
        v2_ref[...] = v2

    row = pl.BlockSpec((tr, c), lambda i: (i, 0))
    return pl.pallas_call(
        body, grid=(r // tr,), name=name, in_specs=[pl.BlockSpec((g, tr, c), lambda i: (0, i, 0)), row, row, row],
        out_specs=[row] * 4, out_shape=[SDS((r, c), f32)] * 4, compiler_params=_params(),
    )(parts, w, m, v)


_SHARD_AXIS = dict(meta_tokens=1, norm_gains=2, a_w_in=2, a_lb_logits=1, a_head_norm=1, a_w_out=1, kv_w=1,
                   b_w_q=1, b_w_out=1, ffn_w_up=2, ffn_conv=2, ffn_w_down=1)
_VECTORS = ("meta_tokens", "norm_gains", "a_lb_logits", "a_head_norm", "ffn_conv")
_REPLICATED = ("kv_norm", "fg_b")
_ROW_TILE_CAP = 512


def _pack(arrs, dtype, cols, row_mult):
    lead = arrs[0].shape[:-1] if arrs[0].ndim > 1 else ()
    flat = jnp.concatenate([a.astype(dtype) for a in arrs], axis=-1)
    size = flat.shape[-1]
    per = cols * row_mult
    total = -(-size // per) * per
    flat = jnp.pad(flat, [(0, 0)] * len(lead) + [(0, total - size)])
    return flat.reshape(lead + (total // cols, cols))


def _unpack(flat, shapes):
    out, off = [], 0
    lead = flat.shape[:-1]
    for shp in shapes:
        size = 1
        for s in shp:
            size *= s
        out.append(flat[..., off:off + size].reshape(lead + tuple(shp)))
        off += size
    return out


def _unshard(seg, axis):
    a = jnp.moveaxis(seg, 0, axis)
    shp = a.shape
    return a.reshape(shp[:axis] + (shp[axis] * shp[axis + 1],) + shp[axis + 2:])


def _shard8(full, axis):
    shp = full.shape
    a = full.reshape(shp[:axis] + (N_DEV, shp[axis] // N_DEV) + shp[axis + 1:])
    return jnp.moveaxis(a, axis, 0)


def _rows(a, lead=0):
    return a.reshape(a.shape[:lead] + (-1, a.shape[-1]))


def kernel(x, meta_tokens, norm_gains, a_w_in, a_lb_logits, a_head_norm, a_w_out, kv_norm, kv_w, fg_b, b_w_q, b_w_out, ffn_w_up, ffn_conv, ffn_w_down, loss_target, m_meta_tokens, m_norm_gains, m_a_w_in, m_a_lb_logits, m_a_head_norm, m_a_w_out, m_kv_norm, m_kv_w, m_fg_b, m_b_w_q, m_b_w_out, m_ffn_w_up, m_ffn_conv, m_ffn_w_down, v_meta_tokens, v_norm_gains, v_a_w_in, v_a_lb_logits, v_a_head_norm, v_a_w_out, v_kv_norm, v_kv_w, v_fg_b, v_b_w_q, v_b_w_out, v_ffn_w_up, v_ffn_conv, v_ffn_w_down):
    names = ("meta_tokens", "norm_gains", "a_w_in", "a_lb_logits", "a_head_norm", "a_w_out", "kv_norm", "kv_w", "fg_b",
             "b_w_q", "b_w_out", "ffn_w_up", "ffn_conv", "ffn_w_down")
    w = dict(zip(names, (meta_tokens, norm_gains, a_w_in, a_lb_logits, a_head_norm, a_w_out, kv_norm, kv_w, fg_b,
                         b_w_q, b_w_out, ffn_w_up, ffn_conv, ffn_w_down)))
    mom = dict(zip(names, (m_meta_tokens, m_norm_gains, m_a_w_in, m_a_lb_logits, m_a_head_norm, m_a_w_out, m_kv_norm,
                           m_kv_w, m_fg_b, m_b_w_q, m_b_w_out, m_ffn_w_up, m_ffn_conv, m_ffn_w_down)))
    var = dict(zip(names, (v_meta_tokens, v_norm_gains, v_a_w_in, v_a_lb_logits, v_a_head_norm, v_a_w_out, v_kv_norm,
                           v_kv_w, v_fg_b, v_b_w_q, v_b_w_out, v_ffn_w_up, v_ffn_conv, v_ffn_w_down)))

    bl_, seq, d = x.shape
    nm = meta_tokens.shape[0]
    t = nm + seq
    n = bl_ * t
    bh = fg_b.shape[0]
    hd = d // bh
    hp = LANES // hd
    ff = ffn_w_down.shape[1] * N_DEV
    tm = _div_tile(t, TOKEN_TILE_CAP)
    tc = _div_tile(t, 64)
    tn = min(d, MODEL_TILE_CAP)

    vec_pack = _pack([w[k].reshape(-1) for k in _VECTORS], f32, LANES, 8)
    first = _all_gather([w["a_w_in"].astype(bf16), w["a_w_out"].astype(bf16), vec_pack], "gather_first")
    vec_segs = _unpack(first[2].reshape(N_DEV, -1), [w[k].shape for k in _VECTORS])
    small = {k: _unshard(a, _SHARD_AXIS[k]) for k, a in zip(_VECTORS, vec_segs)}
    w_in, w_out_a = _unshard(first[0], _SHARD_AXIS["a_w_in"])[0], _unshard(first[1], _SHARD_AXIS["a_w_out"])[0]
    me = _device_index()
    later_names = ("ffn_w_up", "ffn_w_down", "kv_w", "b_w_q", "b_w_out", "ffn_w_up", "ffn_w_down")
    later_layer = (0, 0, None, None, None, 1, 1)
    later = [(w[k] if l is None else w[k][l]).astype(bf16) for k, l in zip(later_names, later_layer)]
    later, _ = lax.optimization_barrier((later, first[2]))
    g_send, g_recv, later_src, later_land, _ = _push_start(later, [_landing(a, me) for a in later], False, "gather_rest_start")

    def gathered(which, after, name):
        lands = _push_wait(g_send, g_recv, [later_src[i] for i in which], [later_land[i] for i in which], which, after,
                           False, name)
        return [_unshard(a, _SHARD_AXIS[later_names[i]] - (later_layer[i] is not None)) for i, a in zip(which, lands)]

    gains_box = [small["norm_gains"]]
    gain = lambda l, j: gains_box[0][l, j][None]
    cw_gate, cw_val = small["ffn_conv"][:, :, :ff], small["ffn_conv"][:, :, ff:]
    head_gain = small["a_head_norm"]
    lb = jax.nn.softmax(small["a_lb_logits"], axis=0)[0:1]
    kvn = kv_norm[None]
    fgb_pad = jnp.pad(fg_b, (0, LANES - bh))[None]

    h0 = jnp.concatenate([jnp.broadcast_to(small["meta_tokens"][None], (bl_, nm, d)), x], axis=1).reshape(n, d)

    def ffn_fwd(l, h_in):
        fi = _rms_fwd(h_in, gain(l, 2), tm, f"ffn{l}_norm")
        ug = _mm(fi, w_gate[l], bf16, tm, ff, f"ffn{l}_up_gate")
        uv = _mm(fi, w_val[l], bf16, tm, ff, f"ffn{l}_up_val")
        act = _conv_gate_fwd(ug, uv, cw_gate[l], cw_val[l], bl_, t, tc, f"ffn{l}_conv_gate")
        h_out, mix = _mm_norm_res(act, w_down[l], gain(l, 3), h_in, tm, f"ffn{l}_down")
        return h_out, (h_in, fi, ug, uv, act, mix)

    hn0 = _rms_fwd(h0, gain(0, 0), tm, "a_norm")
    pmat = _mm(hn0, w_in, f32, tm, tn, "a_in_proj")
    og, states = _gla_fwd(pmat, lb, head_gain, bl_, t, nm, "a_gla_fwd")
    h1, mix_a = _mm_norm_res(og, w_out_a, gain(0, 1), h0, tm, "a_out_proj")
    w_gate, w_val, w_down = {}, {}, {}

    def ffn_weights(l, which, after):
        w_up, w_down[l] = gathered(which, after, f"gather_wait_ffn{l}")
        w_gate[l], w_val[l] = w_up[:, :ff], w_up[:, ff:]

    ffn_weights(0, (0, 1), h1)
    h2, ffn0 = ffn_fwd(0, h1)

    w_kv_zf, w_q, w_out_b = gathered((2, 3, 4), h2, "gather_wait_b")
    w_kv, w_zf = w_kv_zf[:, :2 * d], jnp.pad(w_kv_zf[:, 2 * d:], ((0, 0), (0, LANES - bh)))
    w_q, w_out_b = w_q[0], w_out_b[0]
    hk = _rms_fwd(h2, kvn, tm, "kv_norm")
    kvp = _mm(hk, w_kv, bf16, tm, tn, "kv_proj")
    zf, cum = _zf_c(hk, w_zf, fgb_pad, bl_, t, tm, "forget_cumsum")
    ck = _c_key_rows(cum, bl_, t, tm, bh, hp)
    hn1 = _rms_fwd(h2, gain(1, 0), tm, "b_norm")
    q = _mm(hn1, w_q, bf16, tm, tn, "b_q_proj")
    o, lse = _attn_fwd(q, kvp, ck, bl_, t, tm, hd, "b_attn_fwd")
    h3, mix_b = _mm_norm_res(o, w_out_b, gain(1, 1), h2, tm, "b_out_proj")
    ffn_weights(1, (5, 6), h3)
    h4, ffn1 = ffn_fwd(1, h3)

    target = jnp.concatenate([jnp.zeros((bl_, nm, d), f32), loss_target], axis=1).reshape(n, d)
    loss8, dh = _loss_head(h4, target, t, nm, tm, "loss_head")
    loss = lax.psum(loss8[0, 0], ("x", "y", "c"))

    dgain = {}

    def ffn_bwd(l, saved, dh_out):
        h_in, fi, ug, uv, act, mix = saved
        dmix, dgain[l, 3] = _rms_bwd(mix, gain(l, 3), dh_out, None, bf16, tm, f"ffn{l}_down_norm_bwd")
        dact = _mm_nt([(dmix, w_down[l])], bf16, tm, ff, f"ffn{l}_down_dx")
        dw_down = _mm_tn(act, dmix, tm, ff, tn, f"ffn{l}_down_dw")
        dug, duv, dcg, dcv = _conv_gate_bwd(ug, uv, cw_gate[l], cw_val[l], dact, bl_, t, tc, f"ffn{l}_conv_gate_bwd")
        dfi = _mm_nt([(dug, w_gate[l]), (duv, w_val[l])], bf16, tm, tn // 2, f"ffn{l}_up_dx")
        dw_up = jnp.concatenate([_mm_tn(fi, dug, tm, tn, ff, f"ffn{l}_up_gate_dw"),
                                 _mm_tn(fi, duv, tm, tn, ff, f"ffn{l}_up_val_dw")], axis=1)
        dh_in, dgain[l, 2] = _rms_bwd(h_in, gain(l, 2), dfi, dh_out, f32, tm, f"ffn{l}_norm_bwd")
        return dh_in, dw_up, jnp.concatenate([dcg, dcv], axis=1), dw_down

    def shards(full, axis):
        return _rows(_shard8(full, axis), 1).astype(bf16)

    def push_grads(bufs, name):
        lands = [_landing(lax.dynamic_index_in_dim(b, me, 0, keepdims=False), me) for b in bufs]
        s_sem, r_sem, srcs, lands, token = _push_start(bufs, lands, True, name)
        gains_box[0] = gains_box[0] + token[0, 0]
        return s_sem, r_sem, srcs, lands

    def landed(handle, after, name):
        s_sem, r_sem, srcs, lands = handle
        return _push_wait(s_sem, r_sem, srcs, lands, tuple(range(len(srcs))), after, True, name)

    dh, dw_up1, dconv1, dw_down1 = ffn_bwd(1, ffn1, dh)
    push1 = push_grads([shards(dw_up1, 1), shards(dw_down1, 0)], "grad_push_ffn1")

    dmix, dgain[1, 1] = _rms_bwd(mix_b, gain(1, 1), dh, None, bf16, tm, "b_out_norm_bwd")
    do = _mm_nt([(dmix, w_out_b)], bf16, tm, tn, "b_out_dx")
    dw_out_b = _mm_tn(o, dmix, tm, tn, tn, "b_out_dw")
    dq, dk, dv, dck, dcq = _attn_bwd(q, kvp, o, do, lse, ck, bl_, t, tm, hd, "b_attn_bwd")
    dhn1 = _mm_nt([(dq, w_q)], bf16, tm, tn, "b_q_dx")
    dw_q = _mm_tn(hn1, dq, tm, tn, tn, "b_q_dw")
    dh, dgain[1, 0] = _rms_bwd(h2, gain(1, 0), dhn1, dh, f32, tm, "b_norm_bwd")

    dzf, dfgb = _c_bwd(_dc_rows(dck, dcq, bl_, t, bh), zf, bl_, t, tm, "forget_cumsum_bwd")
    dhk = _mm_nt([(dk, w_kv[:, :d]), (dv, w_kv[:, d:]), (dzf, w_zf)], bf16, tm, tn, "kv_dx")
    dw_kv = jnp.concatenate([_mm_tn(hk, dk, tm, tn, tn, "k_dw"), _mm_tn(hk, dv, tm, tn, tn, "v_dw"),
                             _mm_tn(hk, dzf, tm, tn, LANES, "zf_dw")[:, :bh]], axis=1)
    dh, dkvn = _rms_bwd(h2, kvn, dhk, dh, f32, tm, "kv_norm_bwd")
    push2 = push_grads([shards(dw_out_b, 0), shards(dw_q, 0), shards(dw_kv, 1)], "grad_push_b")

    dh, dw_up0, dconv0, dw_down0 = ffn_bwd(0, ffn0, dh)
    push3 = push_grads([shards(dw_up0, 1), shards(dw_down0, 0)], "grad_push_ffn0")

    dmix, dgain[0, 1] = _rms_bwd(mix_a, gain(0, 1), dh, None, bf16, tm, "a_out_norm_bwd")
    dog = _mm_nt([(dmix, w_out_a)], bf16, tm, tn, "a_out_dx")
    dw_out_a = _mm_tn(og, dmix, tm, tn, tn, "a_out_dw")
    dpq, dpf, dpi, dpg, dlb, dhg = _gla_bwd(pmat, states, dog, lb, head_gain, bl_, t, nm, "a_gla_bwd")
    dps = (dpq, dpf, dpi, dpg)
    dhn0 = _mm_nt([(dp, w_in[:, j * d:(j + 1) * d]) for j, dp in enumerate(dps)], bf16, tm, tn, "a_in_dx")
    dw_in = jnp.concatenate([_mm_tn(hn0, dp, tm, tn, tn, f"a_in_dw{j}") for j, dp in enumerate(dps)], axis=1)
    dh, dgain[0, 0] = _rms_bwd(h0, gain(0, 0), dhn0, dh, f32, tm, "a_norm_bwd")

    dh = dh.reshape(bl_, t, d)
    grad_x = dh[:, nm:]
    dl0 = dlb * lb * (1.0 - lb)
    vec_grads = dict(
        meta_tokens=jnp.sum(dh[:, :nm], axis=0),
        norm_gains=jnp.stack([jnp.concatenate([dgain[l, j] for j in range(4)], axis=0) for l in range(2)]),
        a_lb_logits=jnp.concatenate([dl0, -dl0], axis=0), a_head_norm=dhg, ffn_conv=jnp.stack([dconv0, dconv1]))
    vec_send = _pack([_shard8(vec_grads[k], _SHARD_AXIS[k]).reshape(N_DEV, -1) for k in _VECTORS], bf16, LANES, BF16_ROWS)
    push4 = push_grads([shards(dw_out_a, 0), shards(dw_in, 1), vec_send], "grad_push_a")

    g_s, d_s, m_s, v_s = {}, {}, {}, {}
    outs = (g_s, d_s, m_s, v_s)

    def update(part, srcs, label):
        rows = part.shape[1]
        return _adamw(part, *srcs, rows if rows <= _ROW_TILE_CAP else _div_tile(rows, _ROW_TILE_CAP), label)

    def update_matrix(k, part, layer=None):
        pick = (lambda a: a) if layer is None else (lambda a: a[layer])
        label = f"adamw_{k}" if layer is None else f"adamw_{k}{layer}"
        res = update(part, [_rows(pick(src[k])) for src in (w, mom, var)], label)
        return [r.reshape(pick(w[k]).shape) for r in res]

    def put(k, res):
        for dst, r in zip(outs, res):
            dst[k] = r

    up1, down1 = (update_matrix(k, p, 1) for k, p in zip(("ffn_w_up", "ffn_w_down"), landed(push1, gains_box[0], "grad_wait_ffn1")))
    for k, p in zip(("b_w_out", "b_w_q", "kv_w"), landed(push2, up1[0], "grad_wait_b")):
        put(k, update_matrix(k, p))
    up0, down0 = (update_matrix(k, p, 0) for k, p in zip(("ffn_w_up", "ffn_w_down"), landed(push3, g_s["kv_w"], "grad_wait_ffn0")))
    put("ffn_w_up", [jnp.stack(pair) for pair in zip(up0, up1)])
    put("ffn_w_down", [jnp.stack(pair) for pair in zip(down0, down1)])
    part_out_a, part_in, part_vec = landed(push4, down0[0], "grad_wait_a")
    put("a_w_out", update_matrix("a_w_out", part_out_a))
    put("a_w_in", update_matrix("a_w_in", part_in))
    vec_packs = [_pack([src[k].reshape(-1) for k in _VECTORS], f32, LANES, BF16_ROWS) for src in (w, mom, var)]
    vec_shapes = [w[k].shape for k in _VECTORS]
    for dst, r in zip(outs, update(part_vec, vec_packs, "adamw_vectors")):
        dst.update(zip(_VECTORS, _unpack(r.reshape(-1), vec_shapes)))

    rep_local = _pack([dkvn.reshape(-1), dfgb[0, :bh]], f32, LANES, 8)
    rep_parts = _all_gather([rep_local], "gather_replicated_grads")[0]
    rpacks = [_pack([src[k].reshape(-1) for k in _REPLICATED], f32, LANES, 8) for src in (w, mom, var)]
    rres = _adamw(rep_parts, *rpacks, rep_local.shape[0], "adamw_replicated")
    rshapes = [w[k].shape for k in _REPLICATED]
    g_r, d_r, m_r, v_r = ({k: a for k, a in zip(_REPLICATED, _unpack(r.reshape(-1), rshapes))} for r in rres)

    out = [loss, grad_x]
    for sh, rp in ((g_s, g_r), (d_s, d_r), (m_s, m_r), (v_s, v_r)):
        out += [sh[k] if k in sh else rp[k] for k in names]
    return tuple(out)
```

```python
import functools
import math

import jax
import jax.numpy as jnp
from jax import lax
from jax.experimental import pallas as pl
from jax.experimental.pallas import tpu as pltpu

f32 = jnp.float32
bf16 = jnp.bfloat16
SDS = jax.ShapeDtypeStruct

EPS = 1e-6
A_DK = 128
A_CHUNK = 64
GLA_GROUP = 4
GLA_HEADS = 2
TOKEN_TILE_CAP = 1024
MODEL_TILE_CAP = 1024
LANES = 128
SUBLANES = 8
BF16_ROWS = 16
VMEM_LIMIT = 56 * 1024 * 1024
ADAM_LR, ADAM_B1, ADAM_B2, ADAM_EPS, ADAM_WD, ADAM_STEP = 0.001, 0.9, 0.999, 1e-08, 0.01, 10
N_DEV = 8
MESH = pl.DeviceIdType.MESH

_NT = (((1,), (1,)), ((), ()))
_TN = (((0,), (0,)), ((), ()))
_HI = lax.Precision.HIGHEST


def _params(**kw):
    return pltpu.CompilerParams(vmem_limit_bytes=VMEM_LIMIT, **kw)


def _div_tile(n, cap, mult=BF16_ROWS):
    best = None
    for t in range(mult, min(n, cap) + 1, mult):
        if n % t == 0:
            best = t
    assert best is not None, (n, cap, mult)
    return best


def _bdot(a, b):
    return jnp.dot(a.astype(bf16), b.astype(bf16), preferred_element_type=f32)


def _bdot_nt(a, b):
    return lax.dot_general(a.astype(bf16), b.astype(bf16), _NT, preferred_element_type=f32)


def _bdot_tn(a, b):
    return lax.dot_general(a.astype(bf16), b.astype(bf16), _TN, preferred_element_type=f32)


def _iota2(shape, axis):
    return lax.broadcasted_iota(jnp.int32, shape, axis)


def _cumsum_rows(x):
    n = x.shape[0]
    tri = (_iota2((n, n), 0) >= _iota2((n, n), 1)).astype(f32)
    return jnp.dot(tri, x, precision=_HI, preferred_element_type=f32)


def _revcumsum_rows(x):
    n = x.shape[0]
    tri = (_iota2((n, n), 1) >= _iota2((n, n), 0)).astype(f32)
    return jnp.dot(tri, x, precision=_HI, preferred_element_type=f32)


def _sigmoid(x):
    return 1.0 / (1.0 + jnp.exp(-x))


def _rms_fwd(x, g, tm, name):
    n, d = x.shape

    def body(x_ref, g_ref, o_ref):
        xv = x_ref[...]
        r = lax.rsqrt(jnp.mean(xv * xv, axis=-1, keepdims=True) + EPS)
        o_ref[...] = (xv * r * g_ref[...]).astype(o_ref.dtype)

    return pl.pallas_call(
        body, grid=(n // tm,), name=name,
        in_specs=[pl.BlockSpec((tm, d), lambda i: (i, 0)), pl.BlockSpec((1, d), lambda i: (0, 0))],
        out_specs=pl.BlockSpec((tm, d), lambda i: (i, 0)),
        out_shape=SDS((n, d), bf16), compiler_params=_params(),
    )(x, g)


def _mm(a, w, out_dtype, tm, tn, name):
    n, k = a.shape
    m = w.shape[1]

    def body(a_ref, w_ref, o_ref):
        o_ref[...] = _bdot(a_ref[...], w_ref[...]).astype(o_ref.dtype)

    return pl.pallas_call(
        body, grid=(m // tn, n // tm), name=name,
        in_specs=[pl.BlockSpec((tm, k), lambda j, i: (i, 0)), pl.BlockSpec((k, tn), lambda j, i: (0, j))],
        out_specs=pl.BlockSpec((tm, tn), lambda j, i: (i, j)),
        out_shape=SDS((n, m), out_dtype), compiler_params=_params(),
    )(a, w)


def _mm_norm_res(a, w, g, h, tm, name):
    n, k = a.shape
    d = w.shape[1]

    def body(a_ref, w_ref, g_ref, h_ref, hn_ref, mix_ref):
        mix = _bdot(a_ref[...], w_ref[...])
        r = lax.rsqrt(jnp.mean(mix * mix, axis=-1, keepdims=True) + EPS)
        mix_ref[...] = mix
        hn_ref[...] = h_ref[...] + mix * r * g_ref[...]

    return pl.pallas_call(
        body, grid=(n // tm,), name=name,
        in_specs=[pl.BlockSpec((tm, k), lambda i: (i, 0)), pl.BlockSpec((k, d), lambda i: (0, 0)),
                  pl.BlockSpec((1, d), lambda i: (0, 0)), pl.BlockSpec((tm, d), lambda i: (i, 0))],
        out_specs=[pl.BlockSpec((tm, d), lambda i: (i, 0)), pl.BlockSpec((tm, d), lambda i: (i, 0))],
        out_shape=[SDS((n, d), f32), SDS((n, d), f32)], compiler_params=_params(),
    )(a, w, g, h)


def _rms_bwd(x, g, dy, dh_in, out_dtype, tm, name):
    n, d = x.shape
    has_add = dh_in is not None

    def body(*refs):
        if has_add:
            x_ref, g_ref, dy_ref, dh_ref, o_ref, dg_ref = refs
        else:
            x_ref, g_ref, dy_ref, o_ref, dg_ref = refs
        xv = x_ref[...]
        dyv = dy_ref[...].astype(f32)
        r = lax.rsqrt(jnp.mean(xv * xv, axis=-1, keepdims=True) + EPS)
        xr = xv * r
        gdy = dyv * g_ref[...]
        dx = r * gdy - xr * (r * r) * jnp.mean(xv * gdy, axis=-1, keepdims=True)
        if has_add:
            dx = dx + dh_ref[...]
        o_ref[...] = dx.astype(o_ref.dtype)

        @pl.when(pl.program_id(0) == 0)
        def _():
            dg_ref[...] = jnp.zeros_like(dg_ref)

        dg_ref[...] += jnp.sum(dyv * xr, axis=0, keepdims=True)

    row = pl.BlockSpec((tm, d), lambda i: (i, 0))
    vec = pl.BlockSpec((1, d), lambda i: (0, 0))
    ins = [x, g, dy] + ([dh_in] if has_add else [])
    return pl.pallas_call(
        body, grid=(n // tm,), name=name,
        in_specs=[row, vec, row] + ([row] if has_add else []),
        out_specs=[row, vec],
        out_shape=[SDS((n, d), out_dtype), SDS((1, d), f32)], compiler_params=_params(),
    )(*ins)


def _mm_nt(pairs, out_dtype, tm, tk, name):
    n = pairs[0][0].shape[0]
    k = pairs[0][1].shape[0]
    np_ = len(pairs)

    def body(*refs):
        o_ref = refs[-1]
        acc = None
        for p in range(np_):
            t = _bdot_nt(refs[2 * p][...], refs[2 * p + 1][...])
            acc = t if acc is None else acc + t
        o_ref[...] = acc.astype(o_ref.dtype)

    in_specs, ins = [], []
    for dy, w in pairs:
        m = dy.shape[1]
        in_specs += [pl.BlockSpec((tm, m), lambda j, i: (i, 0)), pl.BlockSpec((tk, m), lambda j, i: (j, 0))]
        ins += [dy, w]
    return pl.pallas_call(
        body, grid=(k // tk, n // tm), name=name, in_specs=in_specs,
        out_specs=pl.BlockSpec((tm, tk), lambda j, i: (i, j)),
        out_shape=SDS((n, k), out_dtype), compiler_params=_params(),
    )(*ins)


def _mm_tn(x, dy, tm, tk, tn, name):
    n, k = x.shape
    m = dy.shape[1]

    def body(x_ref, dy_ref, o_ref):
        @pl.when(pl.program_id(2) == 0)
        def _():
            o_ref[...] = jnp.zeros_like(o_ref)

        o_ref[...] += _bdot_tn(x_ref[...], dy_ref[...])

    return pl.pallas_call(
        body, grid=(k // tk, m // tn, n // tm), name=name,
        in_specs=[pl.BlockSpec((tm, tk), lambda a, b, i: (i, a)), pl.BlockSpec((tm, tn), lambda a, b, i: (i, b))],
        out_specs=pl.BlockSpec((tk, tn), lambda a, b, i: (a, b)),
        out_shape=SDS((k, m), f32), compiler_params=_params(),
    )(x, dy)


def _split3(x):
    hi = x.astype(bf16)
    r = x - hi.astype(f32)
    mid = r.astype(bf16)
    return hi, mid, (r - mid.astype(f32)).astype(bf16)


def _mask_dot(mask, x):
    hi, mid, lo = _split3(x)
    dot = lambda p: jnp.dot(mask, p, preferred_element_type=f32)
    return dot(hi) + dot(mid) + dot(lo)


_BNN = (((2,), (1,)), ((0,), (0,)))
_BNT = (((2,), (2,)), ((0,), (0,)))
_BTN = (((1,), (1,)), ((0,), (0,)))


def _hdot(a, b, dims):
    return lax.dot_general(a.astype(bf16), b.astype(bf16), dims, preferred_element_type=f32)


def _heads(x, nhb):
    return jnp.stack([x[:, h * A_DK:(h + 1) * A_DK] for h in range(nhb)])


def _mask_dot_heads(mask, x):
    return jnp.stack([_mask_dot(mask, x[h]) for h in range(x.shape[0])])


def _chunk_rows(parts, cl):
    tiles = [jnp.broadcast_to(p, (p.shape[0], cl, p.shape[2])) for p in parts]
    return tiles[0] if len(tiles) == 1 else jnp.concatenate(tiles, axis=1)


def _cat(parts):
    return parts[0] if len(parts) == 1 else jnp.concatenate(parts, axis=1)


def _gla_group_fwd(qg, fg, vg, lb, st, nc, cl):
    g = nc * cl
    sg = _sigmoid(fg)
    f = lb + (1.0 - lb) * sg
    k = 1.0 - f
    row, col = _iota2((g, g), 0), _iota2((g, g), 1)
    chunk_of = lambda idx: sum((idx >= u * cl).astype(jnp.int32) for u in range(1, nc)) if nc > 1 else 0
    same = chunk_of(row) == chunk_of(col) if nc > 1 else None
    causal = row >= col if nc == 1 else jnp.logical_and(same, row >= col)
    anti = col >= row if nc == 1 else jnp.logical_and(same, col >= row)
    b = _mask_dot_heads(causal.astype(bf16), jnp.log(f))
    bls = [b[:, (u + 1) * cl - 1:(u + 1) * cl, :] for u in range(nc)]
    ebls = [jnp.exp(x) for x in bls]
    e = jnp.exp(b)
    ei = jnp.exp(-b)
    eo = jnp.exp(_chunk_rows(bls, cl) - b)
    qi, ki, ko = qg * e, k * ei, k * eo
    att = jnp.where(causal[None], _hdot(qi, ki, _BNT), 0.0)
    o_intra = _hdot(att, vg, _BNN)
    sl = [slice(u * cl, (u + 1) * cl) for u in range(nc)]
    ds = [_hdot(vg[:, s], ko[:, s], _BTN) for s in sl]
    sts = [st]
    for u in range(nc):
        sts.append(sts[u] * ebls[u] + ds[u])
    o = o_intra + _cat([_hdot(qi[:, sl[u]], sts[u], _BNT) for u in range(nc)])
    return dict(sg=sg, f=f, e=e, ei=ei, eo=eo, ebls=ebls, qi=qi, ki=ki, ko=ko, att=att, o=o, sts=sts, causal=causal,
                anti=anti, sl=sl)


def _gla_group(nreal, want):
    while nreal % want:
        want //= 2
    return max(want, 1)


def _head_out(o, ggc, hg):
    r = lax.rsqrt(jnp.mean(o * o, axis=-1, keepdims=True) + EPS)
    return o * r * hg * (ggc * _sigmoid(ggc))


def _gla_fwd(pmat, lb, hg, bl_, t, nm, name):
    n, d4 = pmat.shape
    d = d4 // 4
    nh = d // A_DK
    nreal = (t - nm) // A_CHUNK
    nch = nreal + 1
    un = _gla_group(nreal, GLA_GROUP)
    hb = _gla_group(nh, GLA_HEADS)
    ng = nh // hb
    wide = hb * A_DK

    def body(q_ref, f_ref, i_ref, gg_ref, lb_ref, hg_ref, og_ref, ss_ref):
        lbv, hgv = _heads(lb_ref[...], hb), _heads(hg_ref[...], hb)
        take = lambda ref, rows: _heads(ref[rows, :], hb)

        def run(rows, st, idx, nc, cl):
            w = _gla_group_fwd(take(q_ref, rows), take(f_ref, rows), take(i_ref, rows), lbv, st, nc, cl)
            out = _head_out(w["o"], take(gg_ref, rows), hgv)
            for h in range(hb):
                for u in range(nc):
                    ss_ref[h, idx + u] = w["sts"][u][h]
                og_ref[rows, h * A_DK:(h + 1) * A_DK] = out[h].astype(og_ref.dtype)
            return w["sts"][nc]

        st = run(pl.ds(0, nm), jnp.zeros((hb, A_DK, A_DK), f32), 0, 1, nm)

        def step(it, st):
            rows = pl.ds(pl.multiple_of(nm + it * (un * A_CHUNK), BF16_ROWS), un * A_CHUNK)
            return run(rows, st, 1 + it * un, un, A_CHUNK)

        lax.fori_loop(0, nreal // un, step, st)

    col = lambda o: pl.BlockSpec((t, wide), lambda b, h: (b, o * ng + h))
    vec = pl.BlockSpec((1, wide), lambda b, h: (0, h))
    return pl.pallas_call(
        body, grid=(bl_, ng), name=name,
        in_specs=[col(0), col(1), col(2), col(3), vec, vec],
        out_specs=[pl.BlockSpec((t, wide), lambda b, h: (b, h)),
                   pl.BlockSpec((hb, nch, A_DK, A_DK), lambda b, h: (b * ng + h, 0, 0, 0))],
        out_shape=[SDS((n, d), bf16), SDS((bl_ * nh, nch, A_DK, A_DK), f32)], compiler_params=_params(),
    )(pmat, pmat, pmat, pmat, lb, hg)


def _gla_bwd(pmat, ss, dog, lb, hg, bl_, t, nm, name):
    n, d4 = pmat.shape
    d = d4 // 4
    nh = d // A_DK
    nreal = (t - nm) // A_CHUNK
    nch = nreal + 1
    un = _gla_group(nreal, GLA_GROUP)
    hb = _gla_group(nh, GLA_HEADS)
    ng = nh // hb
    wide = hb * A_DK

    def body(q_ref, f_ref, i_ref, gg_ref, ss_ref, dog_ref, lb_ref, hg_ref,
             dq_ref, df_ref, di_ref, dgg_ref, dlb_ref, dhg_ref):
        lbv, hgv = _heads(lb_ref[...], hb), _heads(hg_ref[...], hb)
        take = lambda ref, rows: _heads(ref[rows, :], hb)

        def put(ref, rows, val):
            for h in range(hb):
                ref[rows, h * A_DK:(h + 1) * A_DK] = val[h].astype(ref.dtype)

        def run(rows, idx, carry, nc, cl):
            dst, dlb, dhg = carry
            qg, fg, vg, ggc = take(q_ref, rows), take(f_ref, rows), take(i_ref, rows), take(gg_ref, rows)
            dogc = take(dog_ref, rows).astype(f32)
            st_in = jnp.stack([ss_ref[h, idx] for h in range(hb)])
            w = _gla_group_fwd(qg, fg, vg, lbv, st_in, nc, cl)
            o, qi, ki, ko, sl, sts, ebls = w["o"], w["qi"], w["ki"], w["ko"], w["sl"], w["sts"], w["ebls"]
            r = lax.rsqrt(jnp.mean(o * o, axis=-1, keepdims=True) + EPS)
            sgg = _sigmoid(ggc)
            sil = ggc * sgg
            on = o * r
            dhg = dhg + jnp.sum(dogc * sil * on, axis=1, keepdims=True)
            put(dgg_ref, rows, dogc * on * hgv * (sgg * (1.0 + ggc * (1.0 - sgg))))
            tt = dogc * sil * hgv
            do = r * tt - on * (r * r) * jnp.mean(o * tt, axis=-1, keepdims=True)
            xs = [_hdot(do[:, s], qi[:, s], _BTN) for s in sl]
            dsts = [None] * nc + [dst]
            for u in reversed(range(nc)):
                dsts[u] = dsts[u + 1] * ebls[u] + xs[u]
            datt = jnp.where(w["causal"][None], _hdot(do, vg, _BNT), 0.0)
            dv = _hdot(w["att"], do, _BTN) + _cat([_hdot(ko[:, sl[u]], dsts[u + 1], _BNT) for u in range(nc)])
            dko = _cat([_hdot(vg[:, sl[u]], dsts[u + 1], _BNN) for u in range(nc)])
            dqi = _hdot(datt, ki, _BNN) + _cat([_hdot(do[:, sl[u]], sts[u], _BNN) for u in range(nc)])
            dki = _hdot(datt, qi, _BTN)
            dk = dki * w["ei"] + dko * w["eo"]
            dkoko = dko * ko
            db = dqi * qi - dki * ki - dkoko
            rowi = lax.broadcasted_iota(jnp.int32, db.shape, 1)
            for u in range(nc):
                d_ebl = jnp.sum(dsts[u + 1] * sts[u], axis=1, keepdims=True)
                dbl = jnp.sum(dkoko[:, sl[u]], axis=1, keepdims=True) + d_ebl * ebls[u]
                db = db + jnp.where(rowi == (u + 1) * cl - 1, dbl, 0.0)
            dlogf = _mask_dot_heads(w["anti"].astype(bf16), db)
            df = dlogf / w["f"] - dk
            sg = w["sg"]
            put(dq_ref, rows, dqi * w["e"])
            put(df_ref, rows, df * (1.0 - lbv) * sg * (1.0 - sg))
            put(di_ref, rows, dv)
            dlb = dlb + jnp.sum(df * (1.0 - sg), axis=1, keepdims=True)
            return dsts[0], dlb, dhg

        zero = jnp.zeros((hb, 1, A_DK), f32)
        ngroups = nreal // un

        def step(it, carry):
            grp = ngroups - 1 - it
            rows = pl.ds(pl.multiple_of(nm + grp * (un * A_CHUNK), BF16_ROWS), un * A_CHUNK)
            return run(rows, 1 + grp * un, carry, un, A_CHUNK)

        carry = lax.fori_loop(0, ngroups, step, (jnp.zeros((hb, A_DK, A_DK), f32), zero, zero))
        _, dlb, dhg = run(pl.ds(0, nm), 0, carry, 1, nm)

        @pl.when(pl.program_id(1) == 0)
        def _():
            dlb_ref[...] = jnp.zeros_like(dlb_ref)
            dhg_ref[...] = jnp.zeros_like(dhg_ref)

        for h in range(hb):
            dlb_ref[:, h * A_DK:(h + 1) * A_DK] += dlb[h]
            dhg_ref[:, h * A_DK:(h + 1) * A_DK] += dhg[h]

    col = lambda o: pl.BlockSpec((t, wide), lambda h, b: (b, o * ng + h))
    blk = pl.BlockSpec((t, wide), lambda h, b: (b, h))
    vec = pl.BlockSpec((1, wide), lambda h, b: (0, h))
    return pl.pallas_call(
        body, grid=(ng, bl_), name=name,
        in_specs=[col(0), col(1), col(2), col(3),
                  pl.BlockSpec((hb, nch, A_DK, A_DK), lambda h, b: (b * ng + h, 0, 0, 0)), blk, vec, vec],
        out_specs=[blk, blk, blk, blk, vec, vec],
        out_shape=[SDS((n, d), bf16)] * 4 + [SDS((1, d), f32)] * 2, compiler_params=_params(),
    )(pmat, pmat, pmat, pmat, ss, dog, lb, hg)


def _shifted(x, halo, before):
    n = x.shape[0]
    both = jnp.concatenate([halo, x] if before else [x, halo], axis=0)
    row, col = _iota2((n, n + BF16_ROWS), 0), _iota2((n, n + BF16_ROWS), 1)
    src = row + BF16_ROWS if before else row
    step = -1 if before else 1
    pick = lambda s: jnp.dot((col == src + step * s).astype(bf16), both, preferred_element_type=f32)
    return pick(1), pick(2)


def _conv3(xb, halo, w):
    x = xb.astype(f32)
    x1, x2 = _shifted(xb, halo, True)
    return x, x1, x2, w[0:1, :] * x2 + w[1:2, :] * x1 + w[2:3, :] * x


def _conv_gate_fwd(ug, uv, cwg, cwv, bl_, t, tc, name):
    n, ff = ug.shape
    nt = t // tc

    def body(ug_ref, uv_ref, wg_ref, wv_ref, a_ref, hg_ref, hv_ref):
        @pl.when(pl.program_id(1) == 0)
        def _():
            hg_ref[...] = jnp.zeros_like(hg_ref)
            hv_ref[...] = jnp.zeros_like(hv_ref)

        xg, xv = ug_ref[...], uv_ref[...]
        cg = _conv3(xg, hg_ref[...], wg_ref[...])[3]
        cv = _conv3(xv, hv_ref[...], wv_ref[...])[3]
        a_ref[...] = (cg * _sigmoid(cg) * cv).astype(a_ref.dtype)
        hg_ref[...] = xg[tc - BF16_ROWS:tc, :].astype(hg_ref.dtype)
        hv_ref[...] = xv[tc - BF16_ROWS:tc, :].astype(hv_ref.dtype)

    row = pl.BlockSpec((tc, ff), lambda b, i: (b * nt + i, 0))
    wsp = pl.BlockSpec((3, ff), lambda b, i: (0, 0))
    return pl.pallas_call(
        body, grid=(bl_, nt), name=name, in_specs=[row, row, wsp, wsp], out_specs=row,
        out_shape=SDS((n, ff), bf16),
        scratch_shapes=[pltpu.VMEM((BF16_ROWS, ff), bf16), pltpu.VMEM((BF16_ROWS, ff), bf16)], compiler_params=_params(),
    )(ug, uv, cwg, cwv)


def _conv_gate_bwd(ug, uv, cwg, cwv, da, bl_, t, tc, name):
    n, ff = ug.shape
    nt = t // tc
    per = tc // BF16_ROWS

    def body(ug_ref, uv_ref, pg_ref, pv_ref, wg_ref, wv_ref, da_ref, dug_ref, duv_ref, dwg_ref, dwv_ref, ng_ref, nv_ref):
        first = jnp.logical_and(pl.program_id(0) == 0, pl.program_id(1) == 0)

        @pl.when(first)
        def _():
            dwg_ref[...] = jnp.zeros_like(dwg_ref)
            dwv_ref[...] = jnp.zeros_like(dwv_ref)

        @pl.when(pl.program_id(1) == 0)
        def _():
            ng_ref[...] = jnp.zeros_like(ng_ref)
            nv_ref[...] = jnp.zeros_like(nv_ref)

        seq_start = pl.program_id(1) == nt - 1
        dav = da_ref[...].astype(f32)

        def half(u_ref, p_ref, w_ref):
            halo = p_ref[...]
            return _conv3(u_ref[...], jnp.where(seq_start, jnp.zeros_like(halo), halo), w_ref[...])

        xg, xg1, xg2, cg = half(ug_ref, pg_ref, wg_ref)
        xv, xv1, xv2, cv = half(uv_ref, pv_ref, wv_ref)
        sg = _sigmoid(cg)
        dcg = dav * cv * (sg * (1.0 + cg * (1.0 - sg)))
        dcv = dav * (cg * sg)

        def back(dc, x, x1, x2, w_ref, nx_ref, du_ref, dw_ref):
            w = w_ref[...]
            dcb = dc.astype(bf16)
            dc1, dc2 = _shifted(dcb, nx_ref[...], False)
            du = w[2:3, :] * dc + w[1:2, :] * dc1 + w[0:1, :] * dc2
            du_ref[...] = du.astype(du_ref.dtype)
            dw_ref[0:1, :] += jnp.sum(dc * x2, axis=0, keepdims=True)
            dw_ref[1:2, :] += jnp.sum(dc * x1, axis=0, keepdims=True)
            dw_ref[2:3, :] += jnp.sum(dc * x, axis=0, keepdims=True)
            nx_ref[...] = dcb[0:BF16_ROWS, :].astype(nx_ref.dtype)

        back(dcg, xg, xg1, xg2, wg_ref, ng_ref, dug_ref, dwg_ref)
        back(dcv, xv, xv1, xv2, wv_ref, nv_ref, duv_ref, dwv_ref)

    row = pl.BlockSpec((tc, ff), lambda b, i: (b * nt + nt - 1 - i, 0))
    prev = pl.BlockSpec((BF16_ROWS, ff), lambda b, i: (jnp.maximum((b * nt + nt - 1 - i) * per - 1, 0), 0))
    wsp = pl.BlockSpec((3, ff), lambda b, i: (0, 0))
    return pl.pallas_call(
        body, grid=(bl_, nt), name=name, in_specs=[row, row, prev, prev, wsp, wsp, row],
        out_specs=[row, row, wsp, wsp],
        out_shape=[SDS((n, ff), bf16), SDS((n, ff), bf16), SDS((3, ff), f32), SDS((3, ff), f32)],
        scratch_shapes=[pltpu.VMEM((BF16_ROWS, ff), bf16), pltpu.VMEM((BF16_ROWS, ff), bf16)], compiler_params=_params(),
    )(ug, uv, ug, uv, cwg, cwv, da)


def _zf_c(hk, wzf, fgb, bl_, t, tm, name):
    n, d = hk.shape
    nt = t // tm

    def body(hk_ref, w_ref, b_ref, zf_ref, c_ref, carry_ref):
        @pl.when(pl.program_id(1) == 0)
        def _():
            carry_ref[...] = jnp.zeros_like(carry_ref)

        z = _bdot(hk_ref[...], w_ref[...]) + b_ref[...]
        ls = jnp.minimum(z, 0.0) - jnp.log(1.0 + jnp.exp(-jnp.abs(z)))
        c = _cumsum_rows(ls) + carry_ref[...]
        zf_ref[...] = z
        c_ref[...] = c
        carry_ref[...] = c[tm - 1:tm, :]

    row = lambda w: pl.BlockSpec((tm, w), lambda b, i: (b * nt + i, 0))
    return pl.pallas_call(
        body, grid=(bl_, nt), name=name,
        in_specs=[row(d), pl.BlockSpec((d, LANES), lambda b, i: (0, 0)), pl.BlockSpec((1, LANES), lambda b, i: (0, 0))],
        out_specs=[row(LANES), row(LANES)],
        out_shape=[SDS((n, LANES), f32), SDS((n, LANES), f32)],
        scratch_shapes=[pltpu.VMEM((1, LANES), f32)], compiler_params=_params(),
    )(hk, wzf, fgb)


def _c_bwd(dc, zf, bl_, t, tm, name):
    n = dc.shape[0]
    nt = t // tm

    def body(dc_ref, zf_ref, dzf_ref, dfg_ref, carry_ref):
        @pl.when(jnp.logical_and(pl.program_id(0) == 0, pl.program_id(1) == 0))
        def _():
            dfg_ref[...] = jnp.zeros_like(dfg_ref)

        @pl.when(pl.program_id(1) == 0)
        def _():
            carry_ref[...] = jnp.zeros_like(carry_ref)

        rc = _revcumsum_rows(dc_ref[...]) + carry_ref[...]
        dz = rc * _sigmoid(-zf_ref[...])
        dzf_ref[...] = dz.astype(dzf_ref.dtype)
        dfg_ref[...] += jnp.sum(dz, axis=0, keepdims=True)
        carry_ref[...] = rc[0:1, :]

    row = pl.BlockSpec((tm, LANES), lambda b, i: (b * nt + nt - 1 - i, 0))
    vec = pl.BlockSpec((1, LANES), lambda b, i: (0, 0))
    return pl.pallas_call(
        body, grid=(bl_, nt), name=name, in_specs=[row, row], out_specs=[row, vec],
        out_shape=[SDS((n, LANES), bf16), SDS((1, LANES), f32)],
        scratch_shapes=[pltpu.VMEM((1, LANES), f32)], compiler_params=_params(),
    )(dc, zf)


def _is_pow2(x):
    m, _ = math.frexp(x)
    return m == 0.5


def _prescale(qh, scale):
    return (qh.astype(f32) * scale).astype(bf16)


def _attn_fwd(q, kv, ck, bl_, t, tq, hd, name):
    n, d = q.shape
    npair = d // LANES
    hp = LANES // hd
    nq = t // tq
    scale = 1.0 / (hd ** 0.5)

    pre = _is_pow2(scale)

    def body(q_ref, k_ref, v_ref, ck_ref, o_ref, lse_ref):
        i = pl.program_id(2)
        diag = _iota2((tq, tq), 0) >= _iota2((tq, tq), 1)
        for hh in range(hp):
            lanes = slice(hh * hd, (hh + 1) * hd)
            qh = _prescale(q_ref[:, lanes], scale) if pre else q_ref[:, lanes]

            def block(j, carry, masked, lanes=lanes, qh=qh, hh=hh):
                m, l, acc = carry
                rows = pl.ds(pl.multiple_of(j * tq, BF16_ROWS), tq)
                s = _bdot_nt(qh, k_ref[rows, lanes])
                s = (s if pre else s * scale) - ck_ref[0, 0, j, hh:hh + 1, :]
                if masked:
                    s = jnp.where(diag, s, -1e30)
                m2 = jnp.maximum(m, jnp.max(s, axis=-1, keepdims=True))
                p = jnp.exp(s - m2)
                a = jnp.exp(m - m2)
                return m2, a * l + jnp.sum(p, axis=-1, keepdims=True), a * acc + _bdot(p, v_ref[rows, lanes])

            init = (jnp.full((tq, 1), -1e30, f32), jnp.zeros((tq, 1), f32), jnp.zeros((tq, hd), f32))
            carry = lax.fori_loop(0, i, functools.partial(block, masked=False), init)
            m, l, acc = block(i, carry, True)
            o_ref[:, lanes] = (acc / l).astype(o_ref.dtype)
            lse_ref[:, lanes] = jnp.broadcast_to(m + jnp.log(l), (tq, hd))

    nk = nq
    return pl.pallas_call(
        body, grid=(bl_, npair, nq), name=name,
        in_specs=[pl.BlockSpec((tq, LANES), lambda b, p, i: (b * nq + i, p)),
                  pl.BlockSpec((t, LANES), lambda b, p, i: (b, p)),
                  pl.BlockSpec((t, LANES), lambda b, p, i: (b, npair + p)),
                  pl.BlockSpec((1, 1, nk, hp, tq), lambda b, p, i: (b, p, 0, 0, 0))],
        out_specs=[pl.BlockSpec((tq, LANES), lambda b, p, i: (b * nq + i, p)),
                   pl.BlockSpec((tq, LANES), lambda b, p, i: (b * nq + i, p))],
        out_shape=[SDS((n, d), f32), SDS((n, d), f32)], compiler_params=_params(),
    )(q, kv, kv, ck)


def _attn_bwd(q, kv, o, do, lse, ck, bl_, t, tq, hd, name):
    n, d = q.shape
    npair = d // LANES
    hp = LANES // hd
    nq = t // tq
    scale = 1.0 / (hd ** 0.5)

    pre = _is_pow2(scale)

    def body(q_ref, k_ref, v_ref, o_ref, do_ref, lse_ref, ck_ref, dq_ref, dk_ref, dv_ref, dck_ref, dcq_ref):
        j = pl.program_id(2)

        @pl.when(j == 0)
        def _():
            dq_ref[...] = jnp.zeros_like(dq_ref)
            dcq_ref[...] = jnp.zeros_like(dcq_ref)

        diag = _iota2((tq, tq), 0) >= _iota2((tq, tq), 1)
        for hh in range(hp):
            lanes = slice(hh * hd, (hh + 1) * hd)
            kh = k_ref[:, lanes]
            vh = v_ref[:, lanes]
            kt = kh.astype(f32).T.astype(bf16)
            cs = ck_ref[0, 0, 0, hh:hh + 1, :]

            def block(i, carry, masked, lanes=lanes, kh=kh, vh=vh, kt=kt, cs=cs, hh=hh):
                dkt, dvt, dcs = carry
                rows = pl.ds(pl.multiple_of(i * tq, BF16_ROWS), tq)
                qh = _prescale(q_ref[rows, lanes], scale) if pre else q_ref[rows, lanes]
                doh = do_ref[rows, lanes]
                s = _bdot_nt(qh, kh)
                s = (s if pre else s * scale) - cs
                if masked:
                    s = jnp.where(diag, s, -1e30)
                p = jnp.exp(s - lse_ref[rows, hh * hd:hh * hd + 1])
                delta = jnp.sum(doh.astype(f32) * o_ref[rows, lanes].astype(f32), axis=-1, keepdims=True)
                ds = p * (_bdot_nt(doh, vh) - delta)
                dsb = ds.astype(bf16)
                dq_ref[rows, lanes] += _bdot_nt(kt, dsb).T * scale
                dcq_ref[0, rows, hh:hh + 1] += jnp.sum(ds, axis=-1, keepdims=True)
                dkq = _bdot_tn(qh, dsb)
                return (dkt + (dkq if pre else dkq * scale), dvt + _bdot_tn(doh, p), dcs - jnp.sum(ds, axis=0, keepdims=True))

            init = (jnp.zeros((hd, tq), f32), jnp.zeros((hd, tq), f32), jnp.zeros((1, tq), f32))
            dkt, dvt, dcs = lax.fori_loop(j + 1, nq, functools.partial(block, masked=False), block(j, init, True))
            dk_ref[:, lanes] = dkt.T.astype(dk_ref.dtype)
            dv_ref[:, lanes] = dvt.T.astype(dv_ref.dtype)
            dck_ref[0, 0, 0, hh:hh + 1, :] = dcs

    whole = lambda c0: pl.BlockSpec((t, LANES), lambda b, p, j: (b, c0 + p))
    tile = lambda c0: pl.BlockSpec((tq, LANES), lambda b, p, j: (b * nq + j, c0 + p))
    ckspec = pl.BlockSpec((1, 1, 1, hp, tq), lambda b, p, j: (b, p, j, 0, 0))
    cqspec = pl.BlockSpec((1, t, hp), lambda b, p, j: (p, b, 0))
    return pl.pallas_call(
        body, grid=(bl_, npair, nq), name=name,
        in_specs=[whole(0), tile(0), tile(npair), whole(0), whole(0), whole(0), ckspec],
        out_specs=[whole(0), tile(0), tile(0), ckspec, cqspec],
        out_shape=[SDS((n, d), f32), SDS((n, d), bf16), SDS((n, d), bf16), SDS((bl_, npair, nq, hp, tq), f32),
                   SDS((npair, n, hp), f32)],
        compiler_params=_params(),
    )(q, kv, kv, o, do, lse, ck)


def _loss_head(h, target, t, nm, tm, name):
    n, d = h.shape
    nt = t // tm

    def body(h_ref, t_ref, loss_ref, dh_ref):
        i = pl.program_id(0)

        @pl.when(i == 0)
        def _():
            loss_ref[...] = jnp.zeros_like(loss_ref)

        pos = (i % nt) * tm + _iota2((tm, d), 0)
        err = jnp.where(pos >= nm, h_ref[...] - t_ref[...], 0.0)
        dh_ref[...] = err * (1.0 / d)
        loss_ref[...] += 0.5 * jnp.sum(jnp.mean(err * err, axis=-1, keepdims=True))

    row = pl.BlockSpec((tm, d), lambda i: (i, 0))
    return pl.pallas_call(
        body, grid=(n // tm,), name=name, in_specs=[row, row],
        out_specs=[pl.BlockSpec((8, LANES), lambda i: (0, 0)), row],
        out_shape=[SDS((8, LANES), f32), SDS((n, d), f32)], compiler_params=_params(),
    )(h, target)


def _c_key_rows(c, bl_, t, tq, bh, hp):
    npair = bh // hp
    nk = t // tq
    return c[:, :bh].reshape(bl_, nk, tq, npair, hp).transpose(0, 3, 1, 4, 2)


def _dc_rows(dck, dcq, bl_, t, bh):
    d = dck.transpose(0, 2, 4, 1, 3).reshape(bl_ * t, bh) + dcq.transpose(1, 0, 2).reshape(bl_ * t, bh)
    return jnp.pad(d, ((0, 0), (0, LANES - bh)))


_ANY = pl.BlockSpec(memory_space=pl.ANY)


def _all_gather(xs, name):
    na = len(xs)

    def body(*refs):
        x_refs, out_refs = refs[:na], refs[na:2 * na]
        send_sems, recv_sems, local_sems = refs[2 * na:]
        mx, my, mc = lax.axis_index("x"), lax.axis_index("y"), lax.axis_index("c")
        me, sibling = (mx, my, mc), (mx, my, 1 - mc)
        chips = [(1 - mx, my), (mx, 1 - my), (1 - mx, 1 - my)]

        def copy(a, k, block, to, own=False):
            px, py, pc = block
            rows = out_refs[a].at[4 * px + 2 * py + pc]
            return pltpu.make_async_remote_copy(
                src_ref=x_refs[a] if own else rows, dst_ref=rows,
                send_sem=send_sems.at[a, k], recv_sem=recv_sems.at[a, k], device_id=to, device_id_type=MESH)

        arrays = range(na)
        mine = [pltpu.make_async_copy(x_refs[a], out_refs[a].at[4 * mx + 2 * my + mc], local_sems.at[a]) for a in arrays]
        for cp in mine:
            cp.start()
        first = [copy(a, 1 + j, me, (*chip, mc), own=True) for j, chip in enumerate(chips) for a in arrays]
        first += [copy(a, 0, me, sibling, own=True) for a in arrays]
        for cp in first:
            cp.start()
        passed = []
        for j, chip in enumerate(chips):
            for a in arrays:
                copy(a, 1 + j, (*chip, mc), me).wait_recv()
                cp = copy(a, 4 + j, (*chip, mc), sibling)
                cp.start()
                passed.append(cp)
        for a in arrays:
            copy(a, 0, sibling, me).wait_recv()
        for j, chip in enumerate(chips):
            for a in arrays:
                copy(a, 4 + j, (*chip, 1 - mc), me).wait_recv()
        for cp in first + passed:
            cp.wait_send()
        for cp in mine:
            cp.wait()

    return pl.pallas_call(
        body, name=name, out_shape=[SDS((N_DEV,) + x.shape, x.dtype) for x in xs],
        in_specs=[_ANY] * na, out_specs=[_ANY] * na,
        scratch_shapes=[pltpu.SemaphoreType.DMA((na, 7)), pltpu.SemaphoreType.DMA((na, 7)), pltpu.SemaphoreType.DMA((na,))],
    )(*xs)


_HBM = pl.BlockSpec(memory_space=pltpu.HBM)
_SEM = pl.BlockSpec(memory_space=pltpu.SEMAPHORE)
_DATAFLOW = pltpu.SideEffectType.DATAFLOW_SIDE_EFFECTING
N_PEERS = N_DEV - 1


def _device_index():
    return 4 * lax.axis_index("x") + 2 * lax.axis_index("y") + lax.axis_index("c")


def _peers():
    mx, my, mc = lax.axis_index("x"), lax.axis_index("y"), lax.axis_index("c")
    peers = []
    for r in (2, 3, 4, 5, 6, 7, 1):
        px = 1 - mx if r & 4 else mx
        py = 1 - my if r & 2 else my
        pc = 1 - mc if r & 1 else mc
        peers.append(((px, py, pc), 4 * px + 2 * py + pc))
    return 4 * mx + 2 * my + mc, peers


def _push_copy(src_ref, land_ref, send_sems, recv_sems, a, k, dev, src_row, land_row, scatter):
    return pltpu.make_async_remote_copy(
        src_ref=src_ref.at[src_row] if scatter else src_ref, dst_ref=land_ref.at[land_row],
        send_sem=send_sems.at[a * N_PEERS + k], recv_sem=recv_sems.at[a * N_PEERS + k], device_id=dev, device_id_type=MESH)


def _landing(own, me):
    return lax.dynamic_update_index_in_dim(lax.empty((N_DEV,) + own.shape, own.dtype), own, me, 0)


def _push_start(srcs, lands, scatter, name):
    na = len(srcs)

    def body(*refs):
        src_refs, land_refs = refs[:na], refs[na:2 * na]
        send_sems, recv_sems = refs[2 * na], refs[2 * na + 1]
        token = refs[-1]
        me, peers = _peers()
        for a in range(na):
            for k, (dev, idx) in enumerate(peers):
                _push_copy(src_refs[a], land_refs[a], send_sems, recv_sems, a, k, dev, idx, me, scatter).start()
        token[...] = jnp.zeros_like(token)

    hbm = lambda arrs: [pltpu.HBM(a.shape, a.dtype) for a in arrs]
    out = pl.pallas_call(
        body, name=name,
        out_shape=(pltpu.SemaphoreType.DMA((na * N_PEERS,)), pltpu.SemaphoreType.DMA((na * N_PEERS,)), *hbm(srcs), *hbm(lands),
                   SDS((8, LANES), f32)),
        in_specs=[_HBM] * (2 * na),
        out_specs=(_SEM, _SEM, *([_HBM] * (2 * na)), pl.BlockSpec(memory_space=pltpu.VMEM)),
        input_output_aliases={i: 2 + i for i in range(2 * na)},
        compiler_params=pltpu.CompilerParams(has_side_effects=_DATAFLOW),
    )(*[pltpu.with_memory_space_constraint(a, pltpu.HBM) for a in list(srcs) + list(lands)])
    return out[0], out[1], list(out[2:2 + na]), list(out[2 + na:2 + 2 * na]), out[-1]


def _push_wait(send_sems, recv_sems, srcs, lands, which, after, scatter, name):
    nw = len(which)

    def body(*refs):
        src_refs, land_refs = refs[:nw], refs[nw:2 * nw]
        send_sems_, recv_sems_ = refs[2 * nw], refs[2 * nw + 1]
        _, peers = _peers()
        for j, a in enumerate(which):
            for k, (dev, idx) in enumerate(peers):
                cp = _push_copy(src_refs[j], land_refs[j], send_sems_, recv_sems_, a, k, dev, idx, idx, scatter)
                cp.wait_send()
                cp.wait_recv()

    hbm = lambda arrs: [pltpu.HBM(a.shape, a.dtype) for a in arrs]
    out = pl.pallas_call(
        body, name=name, out_shape=(*hbm(srcs), *hbm(lands)),
        in_specs=[_HBM] * (2 * nw) + [_SEM, _SEM, _ANY], out_specs=[_HBM] * (2 * nw),
        input_output_aliases={i: i for i in range(2 * nw)},
        compiler_params=pltpu.CompilerParams(has_side_effects=_DATAFLOW),
    )(*srcs, *lands, send_sems, recv_sems, after)
    return list(out[nw:])


def _adamw(parts, w, m, v, tr, name):
    g, r, c = parts.shape

    def body(p_ref, w_ref, m_ref, v_ref, g_ref, d_ref, m2_ref, v2_ref):
        gr = p_ref[0].astype(f32)
        for k in range(1, g):
            gr = gr + p_ref[k].astype(f32)
        m2 = ADAM_B1 * m_ref[...] + (1.0 - ADAM_B1) * gr
        v2 = ADAM_B2 * v_ref[...] + (1.0 - ADAM_B2) * (gr * gr)
        m_hat = m2 / (1.0 - ADAM_B1 ** ADAM_STEP)
        v_hat = v2 / (1.0 - ADAM_B2 ** ADAM_STEP)
        g_ref[...] = gr
        d_ref[...] = -ADAM_LR * (m_hat / (jnp.sqrt(v_hat) + ADAM_EPS) + ADAM_WD * w_ref[...])
        m2_ref[...] = m2
        v2_ref[...] = v2

    row = pl.BlockSpec((tr, c), lambda i: (i, 0))
    return pl.pallas_call(
        body, grid=(r // tr,), name=name, in_specs=[pl.BlockSpec((g, tr, c), lambda i: (0, i, 0)), row, row, row],
        out_specs=[row] * 4, out_shape=[SDS((r, c), f32)] * 4, compiler_params=_params(),
    )(parts, w, m, v)


_SHARD_AXIS = dict(meta_tokens=1, norm_gains=2, a_w_in=2, a_lb_logits=1, a_head_norm=1, a_w_out=1, kv_w=1,
                   b_w_q=1, b_w_out=1, ffn_w_up=2, ffn_conv=2, ffn_w_down=1)
_VECTORS = ("meta_tokens", "norm_gains", "a_lb_logits", "a_head_norm", "ffn_conv")
_REPLICATED = ("kv_norm", "fg_b")
_ROW_TILE_CAP = 512


def _pack(arrs, dtype, cols, row_mult):
    lead = arrs[0].shape[:-1] if arrs[0].ndim > 1 else ()
    flat = jnp.concatenate([a.astype(dtype) for a in arrs], axis=-1)
    size = flat.shape[-1]
    per = cols * row_mult
    total = -(-size // per) * per
    flat = jnp.pad(flat, [(0, 0)] * len(lead) + [(0, total - size)])
    return flat.reshape(lead + (total // cols, cols))


def _unpack(flat, shapes):
    out, off = [], 0
    lead = flat.shape[:-1]
    for shp in shapes:
        size = 1
        for s in shp:
            size *= s
        out.append(flat[..., off:off + size].reshape(lead + tuple(shp)))
        off += size
    return out


def _unshard(seg, axis):
    a = jnp.moveaxis(seg, 0, axis)
    shp = a.shape
    return a.reshape(shp[:axis] + (shp[axis] * shp[axis + 1],) + shp[axis + 2:])


def _shard8(full, axis):
    shp = full.shape
    a = full.reshape(shp[:axis] + (N_DEV, shp[axis] // N_DEV) + shp[axis + 1:])
    return jnp.moveaxis(a, axis, 0)


def _rows(a, lead=0):
    return a.reshape(a.shape[:lead] + (-1, a.shape[-1]))


def kernel(x, meta_tokens, norm_gains, a_w_in, a_lb_logits, a_head_norm, a_w_out, kv_norm, kv_w, fg_b, b_w_q, b_w_out, ffn_w_up, ffn_conv, ffn_w_down, loss_target, m_meta_tokens, m_norm_gains, m_a_w_in, m_a_lb_logits, m_a_head_norm, m_a_w_out, m_kv_norm, m_kv_w, m_fg_b, m_b_w_q, m_b_w_out, m_ffn_w_up, m_ffn_conv, m_ffn_w_down, v_meta_tokens, v_norm_gains, v_a_w_in, v_a_lb_logits, v_a_head_norm, v_a_w_out, v_kv_norm, v_kv_w, v_fg_b, v_b_w_q, v_b_w_out, v_ffn_w_up, v_ffn_conv, v_ffn_w_down):
    names = ("meta_tokens", "norm_gains", "a_w_in", "a_lb_logits", "a_head_norm", "a_w_out", "kv_norm", "kv_w", "fg_b",
             "b_w_q", "b_w_out", "ffn_w_up", "ffn_conv", "ffn_w_down")
    w = dict(zip(names, (meta_tokens, norm_gains, a_w_in, a_lb_logits, a_head_norm, a_w_out, kv_norm, kv_w, fg_b,
                         b_w_q, b_w_out, ffn_w_up, ffn_conv, ffn_w_down)))
    mom = dict(zip(names, (m_meta_tokens, m_norm_gains, m_a_w_in, m_a_lb_logits, m_a_head_norm, m_a_w_out, m_kv_norm,
                           m_kv_w, m_fg_b, m_b_w_q, m_b_w_out, m_ffn_w_up, m_ffn_conv, m_ffn_w_down)))
    var = dict(zip(names, (v_meta_tokens, v_norm_gains, v_a_w_in, v_a_lb_logits, v_a_head_norm, v_a_w_out, v_kv_norm,
                           v_kv_w, v_fg_b, v_b_w_q, v_b_w_out, v_ffn_w_up, v_ffn_conv, v_ffn_w_down)))

    bl_, seq, d = x.shape
    nm = meta_tokens.shape[0]
    t = nm + seq
    n = bl_ * t
    bh = fg_b.shape[0]
    hd = d // bh
    hp = LANES // hd
    ff = ffn_w_down.shape[1] * N_DEV
    tm = _div_tile(t, TOKEN_TILE_CAP)
    tc = _div_tile(t, 64)
    tn = min(d, MODEL_TILE_CAP)

    vec_pack = _pack([w[k].reshape(-1) for k in _VECTORS], f32, LANES, 8)
    first = _all_gather([w["a_w_in"].astype(bf16), w["a_w_out"].astype(bf16), vec_pack], "gather_first")
    vec_segs = _unpack(first[2].reshape(N_DEV, -1), [w[k].shape for k in _VECTORS])
    small = {k: _unshard(a, _SHARD_AXIS[k]) for k, a in zip(_VECTORS, vec_segs)}
    w_in, w_out_a = _unshard(first[0], _SHARD_AXIS["a_w_in"])[0], _unshard(first[1], _SHARD_AXIS["a_w_out"])[0]
    me = _device_index()
    later_names = ("ffn_w_up", "ffn_w_down", "kv_w", "b_w_q", "b_w_out", "ffn_w_up", "ffn_w_down")
    later_layer = (0, 0, None, None, None, 1, 1)
    later = [(w[k] if l is None else w[k][l]).astype(bf16) for k, l in zip(later_names, later_layer)]
    later, _ = lax.optimization_barrier((later, first[2]))
    g_send, g_recv, later_src, later_land, _ = _push_start(later, [_landing(a, me) for a in later], False, "gather_rest_start")

    def gathered(which, after, name):
        lands = _push_wait(g_send, g_recv, [later_src[i] for i in which], [later_land[i] for i in which], which, after,
                           False, name)
        return [_unshard(a, _SHARD_AXIS[later_names[i]] - (later_layer[i] is not None)) for i, a in zip(which, lands)]

    gains_box = [small["norm_gains"]]
    gain = lambda l, j: gains_box[0][l, j][None]
    cw_gate, cw_val = small["ffn_conv"][:, :, :ff], small["ffn_conv"][:, :, ff:]
    head_gain = small["a_head_norm"]
    lb = jax.nn.softmax(small["a_lb_logits"], axis=0)[0:1]
    kvn = kv_norm[None]
    fgb_pad = jnp.pad(fg_b, (0, LANES - bh))[None]

    h0 = jnp.concatenate([jnp.broadcast_to(small["meta_tokens"][None], (bl_, nm, d)), x], axis=1).reshape(n, d)

    def ffn_fwd(l, h_in):
        fi = _rms_fwd(h_in, gain(l, 2), tm, f"ffn{l}_norm")
        ug = _mm(fi, w_gate[l], bf16, tm, ff, f"ffn{l}_up_gate")
        uv = _mm(fi, w_val[l], bf16, tm, ff, f"ffn{l}_up_val")
        act = _conv_gate_fwd(ug, uv, cw_gate[l], cw_val[l], bl_, t, tc, f"ffn{l}_conv_gate")
        h_out, mix = _mm_norm_res(act, w_down[l], gain(l, 3), h_in, tm, f"ffn{l}_down")
        return h_out, (h_in, fi, ug, uv, act, mix)

    hn0 = _rms_fwd(h0, gain(0, 0), tm, "a_norm")
    pmat = _mm(hn0, w_in, f32, tm, tn, "a_in_proj")
    og, states = _gla_fwd(pmat, lb, head_gain, bl_, t, nm, "a_gla_fwd")
    h1, mix_a = _mm_norm_res(og, w_out_a, gain(0, 1), h0, tm, "a_out_proj")
    w_gate, w_val, w_down = {}, {}, {}

    def ffn_weights(l, which, after):
        w_up, w_down[l] = gathered(which, after, f"gather_wait_ffn{l}")
        w_gate[l], w_val[l] = w_up[:, :ff], w_up[:, ff:]

    ffn_weights(0, (0, 1), h1)
    h2, ffn0 = ffn_fwd(0, h1)

    w_kv_zf, w_q, w_out_b = gathered((2, 3, 4), h2, "gather_wait_b")
    w_kv, w_zf = w_kv_zf[:, :2 * d], jnp.pad(w_kv_zf[:, 2 * d:], ((0, 0), (0, LANES - bh)))
    w_q, w_out_b = w_q[0], w_out_b[0]
    hk = _rms_fwd(h2, kvn, tm, "kv_norm")
    kvp = _mm(hk, w_kv, bf16, tm, tn, "kv_proj")
    zf, cum = _zf_c(hk, w_zf, fgb_pad, bl_, t, tm, "forget_cumsum")
    ck = _c_key_rows(cum, bl_, t, tm, bh, hp)
    hn1 = _rms_fwd(h2, gain(1, 0), tm, "b_norm")
    q = _mm(hn1, w_q, bf16, tm, tn, "b_q_proj")
    o, lse = _attn_fwd(q, kvp, ck, bl_, t, tm, hd, "b_attn_fwd")
    h3, mix_b = _mm_norm_res(o, w_out_b, gain(1, 1), h2, tm, "b_out_proj")
    ffn_weights(1, (5, 6), h3)
    h4, ffn1 = ffn_fwd(1, h3)

    target = jnp.concatenate([jnp.zeros((bl_, nm, d), f32), loss_target], axis=1).reshape(n, d)
    loss8, dh = _loss_head(h4, target, t, nm, tm, "loss_head")
    loss = lax.psum(loss8[0, 0], ("x", "y", "c"))

    dgain = {}

    def ffn_bwd(l, saved, dh_out):
        h_in, fi, ug, uv, act, mix = saved
        dmix, dgain[l, 3] = _rms_bwd(mix, gain(l, 3), dh_out, None, bf16, tm, f"ffn{l}_down_norm_bwd")
        dact = _mm_nt([(dmix, w_down[l])], bf16, tm, ff, f"ffn{l}_down_dx")
        dw_down = _mm_tn(act, dmix, tm, ff, tn, f"ffn{l}_down_dw")
        dug, duv, dcg, dcv = _conv_gate_bwd(ug, uv, cw_gate[l], cw_val[l], dact, bl_, t, tc, f"ffn{l}_conv_gate_bwd")
        dfi = _mm_nt([(dug, w_gate[l]), (duv, w_val[l])], bf16, tm, tn // 2, f"ffn{l}_up_dx")
        dw_up = jnp.concatenate([_mm_tn(fi, dug, tm, tn, ff, f"ffn{l}_up_gate_dw"),
                                 _mm_tn(fi, duv, tm, tn, ff, f"ffn{l}_up_val_dw")], axis=1)
        dh_in, dgain[l, 2] = _rms_bwd(h_in, gain(l, 2), dfi, dh_out, f32, tm, f"ffn{l}_norm_bwd")
        return dh_in, dw_up, jnp.concatenate([dcg, dcv], axis=1), dw_down

    def shards(full, axis):
        return _rows(_shard8(full, axis), 1).astype(bf16)

    def push_grads(bufs, name):
        lands = [_landing(lax.dynamic_index_in_dim(b, me, 0, keepdims=False), me) for b in bufs]
        s_sem, r_sem, srcs, lands, token = _push_start(bufs, lands, True, name)
        gains_box[0] = gains_box[0] + token[0, 0]
        return s_sem, r_sem, srcs, lands

    def landed(handle, after, name):
        s_sem, r_sem, srcs, lands = handle
        return _push_wait(s_sem, r_sem, srcs, lands, tuple(range(len(srcs))), after, True, name)

    dh, dw_up1, dconv1, dw_down1 = ffn_bwd(1, ffn1, dh)
    push1 = push_grads([shards(dw_up1, 1), shards(dw_down1, 0)], "grad_push_ffn1")

    dmix, dgain[1, 1] = _rms_bwd(mix_b, gain(1, 1), dh, None, bf16, tm, "b_out_norm_bwd")
    do = _mm_nt([(dmix, w_out_b)], bf16, tm, tn, "b_out_dx")
    dw_out_b = _mm_tn(o, dmix, tm, tn, tn, "b_out_dw")
    dq, dk, dv, dck, dcq = _attn_bwd(q, kvp, o, do, lse, ck, bl_, t, tm, hd, "b_attn_bwd")
    dhn1 = _mm_nt([(dq, w_q)], bf16, tm, tn, "b_q_dx")
    dw_q = _mm_tn(hn1, dq, tm, tn, tn, "b_q_dw")
    dh, dgain[1, 0] = _rms_bwd(h2, gain(1, 0), dhn1, dh, f32, tm, "b_norm_bwd")

    dzf, dfgb = _c_bwd(_dc_rows(dck, dcq, bl_, t, bh), zf, bl_, t, tm, "forget_cumsum_bwd")
    dhk = _mm_nt([(dk, w_kv[:, :d]), (dv, w_kv[:, d:]), (dzf, w_zf)], bf16, tm, tn, "kv_dx")
    dw_kv = jnp.concatenate([_mm_tn(hk, dk, tm, tn, tn, "k_dw"), _mm_tn(hk, dv, tm, tn, tn, "v_dw"),
                             _mm_tn(hk, dzf, tm, tn, LANES, "zf_dw")[:, :bh]], axis=1)
    dh, dkvn = _rms_bwd(h2, kvn, dhk, dh, f32, tm, "kv_norm_bwd")
    push2 = push_grads([shards(dw_out_b, 0), shards(dw_q, 0), shards(dw_kv, 1)], "grad_push_b")

    dh, dw_up0, dconv0, dw_down0 = ffn_bwd(0, ffn0, dh)
    push3 = push_grads([shards(dw_up0, 1), shards(dw_down0, 0)], "grad_push_ffn0")

    dmix, dgain[0, 1] = _rms_bwd(mix_a, gain(0, 1), dh, None, bf16, tm, "a_out_norm_bwd")
    dog = _mm_nt([(dmix, w_out_a)], bf16, tm, tn, "a_out_dx")
    dw_out_a = _mm_tn(og, dmix, tm, tn, tn, "a_out_dw")
    dpq, dpf, dpi, dpg, dlb, dhg = _gla_bwd(pmat, states, dog, lb, head_gain, bl_, t, nm, "a_gla_bwd")
    dps = (dpq, dpf, dpi, dpg)
    dhn0 = _mm_nt([(dp, w_in[:, j * d:(j + 1) * d]) for j, dp in enumerate(dps)], bf16, tm, tn, "a_in_dx")
    dw_in = jnp.concatenate([_mm_tn(hn0, dp, tm, tn, tn, f"a_in_dw{j}") for j, dp in enumerate(dps)], axis=1)
    dh, dgain[0, 0] = _rms_bwd(h0, gain(0, 0), dhn0, dh, f32, tm, "a_norm_bwd")

    dh = dh.reshape(bl_, t, d)
    grad_x = dh[:, nm:]
    dl0 = dlb * lb * (1.0 - lb)
    vec_grads = dict(
        meta_tokens=jnp.sum(dh[:, :nm], axis=0),
        norm_gains=jnp.stack([jnp.concatenate([dgain[l, j] for j in range(4)], axis=0) for l in range(2)]),
        a_lb_logits=jnp.concatenate([dl0, -dl0], axis=0), a_head_norm=dhg, ffn_conv=jnp.stack([dconv0, dconv1]))
    vec_send = _pack([_shard8(vec_grads[k], _SHARD_AXIS[k]).reshape(N_DEV, -1) for k in _VECTORS], bf16, LANES, BF16_ROWS)
    push4 = push_grads([shards(dw_out_a, 0), shards(dw_in, 1), vec_send], "grad_push_a")

    g_s, d_s, m_s, v_s = {}, {}, {}, {}
    outs = (g_s, d_s, m_s, v_s)

    def update(part, srcs, label):
        rows = part.shape[1]
        return _adamw(part, *srcs, rows if rows <= _ROW_TILE_CAP else _div_tile(rows, _ROW_TILE_CAP), label)

    def update_matrix(k, part, layer=None):
        pick = (lambda a: a) if layer is None else (lambda a: a[layer])
        label = f"adamw_{k}" if layer is None else f"adamw_{k}{layer}"
        res = update(part, [_rows(pick(src[k])) for src in (w, mom, var)], label)
        return [r.reshape(pick(w[k]).shape) for r in res]

    def put(k, res):
        for dst, r in zip(outs, res):
            dst[k] = r

    up1, down1 = (update_matrix(k, p, 1) for k, p in zip(("ffn_w_up", "ffn_w_down"), landed(push1, gains_box[0], "grad_wait_ffn1")))
    for k, p in zip(("b_w_out", "b_w_q", "kv_w"), landed(push2, up1[0], "grad_wait_b")):
        put(k, update_matrix(k, p))
    up0, down0 = (update_matrix(k, p, 0) for k, p in zip(("ffn_w_up", "ffn_w_down"), landed(push3, g_s["kv_w"], "grad_wait_ffn0")))
    put("ffn_w_up", [jnp.stack(pair) for pair in zip(up0, up1)])
    put("ffn_w_down", [jnp.stack(pair) for pair in zip(down0, down1)])
    part_out_a, part_in, part_vec = landed(push4, down0[0], "grad_wait_a")
    put("a_w_out", update_matrix("a_w_out", part_out_a))
    put("a_w_in", update_matrix("a_w_in", part_in))
    vec_packs = [_pack([src[k].reshape(-1) for k in _VECTORS], f32, LANES, BF16_ROWS) for src in (w, mom, var)]
    vec_shapes = [w[k].shape for k in _VECTORS]
    for dst, r in zip(outs, update(part_vec, vec_packs, "adamw_vectors")):
        dst.update(zip(_VECTORS, _unpack(r.reshape(-1), vec_shapes)))

    rep_local = _pack([dkvn.reshape(-1), dfgb[0, :bh]], f32, LANES, 8)
    rep_parts = _all_gather([rep_local], "gather_replicated_grads")[0]
    rpacks = [_pack([src[k].reshape(-1) for k in _REPLICATED], f32, LANES, 8) for src in (w, mom, var)]
    rres = _adamw(rep_parts, *rpacks, rep_local.shape[0], "adamw_replicated")
    rshapes = [w[k].shape for k in _REPLICATED]
    g_r, d_r, m_r, v_r = ({k: a for k, a in zip(_REPLICATED, _unpack(r.reshape(-1), rshapes))} for r in rres)

    out = [loss, grad_x]
    for sh, rp in ((g_s, g_r), (d_s, d_r), (m_s, m_r), (v_s, v_r)):
        out += [sh[k] if k in sh else rp[k] for k in names]
    return tuple(out)
```

```python
import functools
import math

import jax
import jax.numpy as jnp
from jax import lax
from jax.experimental import pallas as pl
from jax.experimental.pallas import tpu as pltpu

f32 = jnp.float32
bf16 = jnp.bfloat16
SDS = jax.ShapeDtypeStruct

EPS = 1e-6
A_DK = 128
A_CHUNK = 64
GLA_GROUP = 4
GLA_HEADS = 2
TOKEN_TILE_CAP = 1024
MODEL_TILE_CAP = 1024
LANES = 128
SUBLANES = 8
BF16_ROWS = 16
VMEM_LIMIT = 56 * 1024 * 1024
ADAM_LR, ADAM_B1, ADAM_B2, ADAM_EPS, ADAM_WD, ADAM_STEP = 0.001, 0.9, 0.999, 1e-08, 0.01, 10
N_DEV = 8
MESH = pl.DeviceIdType.MESH

_NT = (((1,), (1,)), ((), ()))
_TN = (((0,), (0,)), ((), ()))
_HI = lax.Precision.HIGHEST


def _params(**kw):
    return pltpu.CompilerParams(vmem_limit_bytes=VMEM_LIMIT, **kw)


def _div_tile(n, cap, mult=BF16_ROWS):
    best = None
    for t in range(mult, min(n, cap) + 1, mult):
        if n % t == 0:
            best = t
    assert best is not None, (n, cap, mult)
    return best


def _bdot(a, b):
    return jnp.dot(a.astype(bf16), b.astype(bf16), preferred_element_type=f32)


def _bdot_nt(a, b):
    return lax.dot_general(a.astype(bf16), b.astype(bf16), _NT, preferred_element_type=f32)


def _bdot_tn(a, b):
    return lax.dot_general(a.astype(bf16), b.astype(bf16), _TN, preferred_element_type=f32)


def _iota2(shape, axis):
    return lax.broadcasted_iota(jnp.int32, shape, axis)


def _cumsum_rows(x):
    n = x.shape[0]
    tri = (_iota2((n, n), 0) >= _iota2((n, n), 1)).astype(f32)
    return jnp.dot(tri, x, precision=_HI, preferred_element_type=f32)


def _revcumsum_rows(x):
    n = x.shape[0]
    tri = (_iota2((n, n), 1) >= _iota2((n, n), 0)).astype(f32)
    return jnp.dot(tri, x, precision=_HI, preferred_element_type=f32)


def _sigmoid(x):
    return 1.0 / (1.0 + jnp.exp(-x))


def _rms_fwd(x, g, tm, name):
    n, d = x.shape

    def body(x_ref, g_ref, o_ref):
        xv = x_ref[...]
        r = lax.rsqrt(jnp.mean(xv * xv, axis=-1, keepdims=True) + EPS)
        o_ref[...] = (xv * r * g_ref[...]).astype(o_ref.dtype)

    return pl.pallas_call(
        body, grid=(n // tm,), name=name,
        in_specs=[pl.BlockSpec((tm, d), lambda i: (i, 0)), pl.BlockSpec((1, d), lambda i: (0, 0))],
        out_specs=pl.BlockSpec((tm, d), lambda i: (i, 0)),
        out_shape=SDS((n, d), bf16), compiler_params=_params(),
    )(x, g)


def _mm(a, w, out_dtype, tm, tn, name):
    n, k = a.shape
    m = w.shape[1]

    def body(a_ref, w_ref, o_ref):
        o_ref[...] = _bdot(a_ref[...], w_ref[...]).astype(o_ref.dtype)

    return pl.pallas_call(
        body, grid=(m // tn, n // tm), name=name,
        in_specs=[pl.BlockSpec((tm, k), lambda j, i: (i, 0)), pl.BlockSpec((k, tn), lambda j, i: (0, j))],
        out_specs=pl.BlockSpec((tm, tn), lambda j, i: (i, j)),
        out_shape=SDS((n, m), out_dtype), compiler_params=_params(),
    )(a, w)


def _mm_norm_res(a, w, g, h, tm, name):
    n, k = a.shape
    d = w.shape[1]

    def body(a_ref, w_ref, g_ref, h_ref, hn_ref, mix_ref):
        mix = _bdot(a_ref[...], w_ref[...])
        r = lax.rsqrt(jnp.mean(mix * mix, axis=-1, keepdims=True) + EPS)
        mix_ref[...] = mix
        hn_ref[...] = h_ref[...] + mix * r * g_ref[...]

    return pl.pallas_call(
        body, grid=(n // tm,), name=name,
        in_specs=[pl.BlockSpec((tm, k), lambda i: (i, 0)), pl.BlockSpec((k, d), lambda i: (0, 0)),
                  pl.BlockSpec((1, d), lambda i: (0, 0)), pl.BlockSpec((tm, d), lambda i: (i, 0))],
        out_specs=[pl.BlockSpec((tm, d), lambda i: (i, 0)), pl.BlockSpec((tm, d), lambda i: (i, 0))],
        out_shape=[SDS((n, d), f32), SDS((n, d), f32)], compiler_params=_params(),
    )(a, w, g, h)


def _rms_bwd(x, g, dy, dh_in, out_dtype, tm, name):
    n, d = x.shape
    has_add = dh_in is not None

    def body(*refs):
        if has_add:
            x_ref, g_ref, dy_ref, dh_ref, o_ref, dg_ref = refs
        else:
            x_ref, g_ref, dy_ref, o_ref, dg_ref = refs
        xv = x_ref[...]
        dyv = dy_ref[...].astype(f32)
        r = lax.rsqrt(jnp.mean(xv * xv, axis=-1, keepdims=True) + EPS)
        xr = xv * r
        gdy = dyv * g_ref[...]
        dx = r * gdy - xr * (r * r) * jnp.mean(xv * gdy, axis=-1, keepdims=True)
        if has_add:
            dx = dx + dh_ref[...]
        o_ref[...] = dx.astype(o_ref.dtype)

        @pl.when(pl.program_id(0) == 0)
        def _():
            dg_ref[...] = jnp.zeros_like(dg_ref)

        dg_ref[...] += jnp.sum(dyv * xr, axis=0, keepdims=True)

    row = pl.BlockSpec((tm, d), lambda i: (i, 0))
    vec = pl.BlockSpec((1, d), lambda i: (0, 0))
    ins = [x, g, dy] + ([dh_in] if has_add else [])
    return pl.pallas_call(
        body, grid=(n // tm,), name=name,
        in_specs=[row, vec, row] + ([row] if has_add else []),
        out_specs=[row, vec],
        out_shape=[SDS((n, d), out_dtype), SDS((1, d), f32)], compiler_params=_params(),
    )(*ins)


def _mm_nt(pairs, out_dtype, tm, tk, name):
    n = pairs[0][0].shape[0]
    k = pairs[0][1].shape[0]
    np_ = len(pairs)

    def body(*refs):
        o_ref = refs[-1]
        acc = None
        for p in range(np_):
            t = _bdot_nt(refs[2 * p][...], refs[2 * p + 1][...])
            acc = t if acc is None else acc + t
        o_ref[...] = acc.astype(o_ref.dtype)

    in_specs, ins = [], []
    for dy, w in pairs:
        m = dy.shape[1]
        in_specs += [pl.BlockSpec((tm, m), lambda j, i: (i, 0)), pl.BlockSpec((tk, m), lambda j, i: (j, 0))]
        ins += [dy, w]
    return pl.pallas_call(
        body, grid=(k // tk, n // tm), name=name, in_specs=in_specs,
        out_specs=pl.BlockSpec((tm, tk), lambda j, i: (i, j)),
        out_shape=SDS((n, k), out_dtype), compiler_params=_params(),
    )(*ins)


def _mm_tn(x, dy, tm, tk, tn, name):
    n, k = x.shape
    m = dy.shape[1]

    def body(x_ref, dy_ref, o_ref):
        @pl.when(pl.program_id(2) == 0)
        def _():
            o_ref[...] = jnp.zeros_like(o_ref)

        o_ref[...] += _bdot_tn(x_ref[...], dy_ref[...])

    return pl.pallas_call(
        body, grid=(k // tk, m // tn, n // tm), name=name,
        in_specs=[pl.BlockSpec((tm, tk), lambda a, b, i: (i, a)), pl.BlockSpec((tm, tn), lambda a, b, i: (i, b))],
        out_specs=pl.BlockSpec((tk, tn), lambda a, b, i: (a, b)),
        out_shape=SDS((k, m), f32), compiler_params=_params(),
    )(x, dy)


def _split3(x):
    hi = x.astype(bf16)
    r = x - hi.astype(f32)
    mid = r.astype(bf16)
    return hi, mid, (r - mid.astype(f32)).astype(bf16)


def _mask_dot(mask, x):
    hi, mid, lo = _split3(x)
    dot = lambda p: jnp.dot(mask, p, preferred_element_type=f32)
    return dot(hi) + dot(mid) + dot(lo)


_BNN = (((2,), (1,)), ((0,), (0,)))
_BNT = (((2,), (2,)), ((0,), (0,)))
_BTN = (((1,), (1,)), ((0,), (0,)))


def _hdot(a, b, dims):
    return lax.dot_general(a.astype(bf16), b.astype(bf16), dims, preferred_element_type=f32)


def _heads(x, nhb):
    return jnp.stack([x[:, h * A_DK:(h + 1) * A_DK] for h in range(nhb)])


def _mask_dot_heads(mask, x):
    return jnp.stack([_mask_dot(mask, x[h]) for h in range(x.shape[0])])


def _chunk_rows(parts, cl):
    tiles = [jnp.broadcast_to(p, (p.shape[0], cl, p.shape[2])) for p in parts]
    return tiles[0] if len(tiles) == 1 else jnp.concatenate(tiles, axis=1)


def _cat(parts):
    return parts[0] if len(parts) == 1 else jnp.concatenate(parts, axis=1)


def _gla_group_fwd(qg, fg, vg, lb, st, nc, cl):
    g = nc * cl
    sg = _sigmoid(fg)
    f = lb + (1.0 - lb) * sg
    k = 1.0 - f
    row, col = _iota2((g, g), 0), _iota2((g, g), 1)
    chunk_of = lambda idx: sum((idx >= u * cl).astype(jnp.int32) for u in range(1, nc)) if nc > 1 else 0
    same = chunk_of(row) == chunk_of(col) if nc > 1 else None
    causal = row >= col if nc == 1 else jnp.logical_and(same, row >= col)
    anti = col >= row if nc == 1 else jnp.logical_and(same, col >= row)
    b = _mask_dot_heads(causal.astype(bf16), jnp.log(f))
    bls = [b[:, (u + 1) * cl - 1:(u + 1) * cl, :] for u in range(nc)]
    ebls = [jnp.exp(x) for x in bls]
    e = jnp.exp(b)
    ei = jnp.exp(-b)
    eo = jnp.exp(_chunk_rows(bls, cl) - b)
    qi, ki, ko = qg * e, k * ei, k * eo
    att = jnp.where(causal[None], _hdot(qi, ki, _BNT), 0.0)
    o_intra = _hdot(att, vg, _BNN)
    sl = [slice(u * cl, (u + 1) * cl) for u in range(nc)]
    ds = [_hdot(vg[:, s], ko[:, s], _BTN) for s in sl]
    sts = [st]
    for u in range(nc):
        sts.append(sts[u] * ebls[u] + ds[u])
    o = o_intra + _cat([_hdot(qi[:, sl[u]], sts[u], _BNT) for u in range(nc)])
    return dict(sg=sg, f=f, e=e, ei=ei, eo=eo, ebls=ebls, qi=qi, ki=ki, ko=ko, att=att, o=o, sts=sts, causal=causal,
                anti=anti, sl=sl)


def _gla_group(nreal, want):
    while nreal % want:
        want //= 2
    return max(want, 1)


def _head_out(o, ggc, hg):
    r = lax.rsqrt(jnp.mean(o * o, axis=-1, keepdims=True) + EPS)
    return o * r * hg * (ggc * _sigmoid(ggc))


def _gla_fwd(pmat, lb, hg, bl_, t, nm, name):
    n, d4 = pmat.shape
    d = d4 // 4
    nh = d // A_DK
    nreal = (t - nm) // A_CHUNK
    nch = nreal + 1
    un = _gla_group(nreal, GLA_GROUP)
    hb = _gla_group(nh, GLA_HEADS)
    ng = nh // hb
    wide = hb * A_DK

    def body(q_ref, f_ref, i_ref, gg_ref, lb_ref, hg_ref, og_ref, ss_ref):
        lbv, hgv = _heads(lb_ref[...], hb), _heads(hg_ref[...], hb)
        take = lambda ref, rows: _heads(ref[rows, :], hb)

        def run(rows, st, idx, nc, cl):
            w = _gla_group_fwd(take(q_ref, rows), take(f_ref, rows), take(i_ref, rows), lbv, st, nc, cl)
            out = _head_out(w["o"], take(gg_ref, rows), hgv)
            for h in range(hb):
                for u in range(nc):
                    ss_ref[h, idx + u] = w["sts"][u][h]
                og_ref[rows, h * A_DK:(h + 1) * A_DK] = out[h].astype(og_ref.dtype)
            return w["sts"][nc]

        st = run(pl.ds(0, nm), jnp.zeros((hb, A_DK, A_DK), f32), 0, 1, nm)

        def step(it, st):
            rows = pl.ds(pl.multiple_of(nm + it * (un * A_CHUNK), BF16_ROWS), un * A_CHUNK)
            return run(rows, st, 1 + it * un, un, A_CHUNK)

        lax.fori_loop(0, nreal // un, step, st)

    col = lambda o: pl.BlockSpec((t, wide), lambda b, h: (b, o * ng + h))
    vec = pl.BlockSpec((1, wide), lambda b, h: (0, h))
    return pl.pallas_call(
        body, grid=(bl_, ng), name=name,
        in_specs=[col(0), col(1), col(2), col(3), vec, vec],
        out_specs=[pl.BlockSpec((t, wide), lambda b, h: (b, h)),
                   pl.BlockSpec((hb, nch, A_DK, A_DK), lambda b, h: (b * ng + h, 0, 0, 0))],
        out_shape=[SDS((n, d), bf16), SDS((bl_ * nh, nch, A_DK, A_DK), f32)], compiler_params=_params(),
    )(pmat, pmat, pmat, pmat, lb, hg)


def _gla_bwd(pmat, ss, dog, lb, hg, bl_, t, nm, name):
    n, d4 = pmat.shape
    d = d4 // 4
    nh = d // A_DK
    nreal = (t - nm) // A_CHUNK
    nch = nreal + 1
    un = _gla_group(nreal, GLA_GROUP)
    hb = _gla_group(nh, GLA_HEADS)
    ng = nh // hb
    wide = hb * A_DK

    def body(q_ref, f_ref, i_ref, gg_ref, ss_ref, dog_ref, lb_ref, hg_ref,
             dq_ref, df_ref, di_ref, dgg_ref, dlb_ref, dhg_ref):
        lbv, hgv = _heads(lb_ref[...], hb), _heads(hg_ref[...], hb)
        take = lambda ref, rows: _heads(ref[rows, :], hb)

        def put(ref, rows, val):
            for h in range(hb):
                ref[rows, h * A_DK:(h + 1) * A_DK] = val[h].astype(ref.dtype)

        def run(rows, idx, carry, nc, cl):
            dst, dlb, dhg = carry
            qg, fg, vg, ggc = take(q_ref, rows), take(f_ref, rows), take(i_ref, rows), take(gg_ref, rows)
            dogc = take(dog_ref, rows).astype(f32)
            st_in = jnp.stack([ss_ref[h, idx] for h in range(hb)])
            w = _gla_group_fwd(qg, fg, vg, lbv, st_in, nc, cl)
            o, qi, ki, ko, sl, sts, ebls = w["o"], w["qi"], w["ki"], w["ko"], w["sl"], w["sts"], w["ebls"]
            r = lax.rsqrt(jnp.mean(o * o, axis=-1, keepdims=True) + EPS)
            sgg = _sigmoid(ggc)
            sil = ggc * sgg
            on = o * r
            dhg = dhg + jnp.sum(dogc * sil * on, axis=1, keepdims=True)
            put(dgg_ref, rows, dogc * on * hgv * (sgg * (1.0 + ggc * (1.0 - sgg))))
            tt = dogc * sil * hgv
            do = r * tt - on * (r * r) * jnp.mean(o * tt, axis=-1, keepdims=True)
            xs = [_hdot(do[:, s], qi[:, s], _BTN) for s in sl]
            dsts = [None] * nc + [dst]
            for u in reversed(range(nc)):
                dsts[u] = dsts[u + 1] * ebls[u] + xs[u]
            datt = jnp.where(w["causal"][None], _hdot(do, vg, _BNT), 0.0)
            dv = _hdot(w["att"], do, _BTN) + _cat([_hdot(ko[:, sl[u]], dsts[u + 1], _BNT) for u in range(nc)])
            dko = _cat([_hdot(vg[:, sl[u]], dsts[u + 1], _BNN) for u in range(nc)])
            dqi = _hdot(datt, ki, _BNN) + _cat([_hdot(do[:, sl[u]], sts[u], _BNN) for u in range(nc)])
            dki = _hdot(datt, qi, _BTN)
            dk = dki * w["ei"] + dko * w["eo"]
            dkoko = dko * ko
            db = dqi * qi - dki * ki - dkoko
            rowi = lax.broadcasted_iota(jnp.int32, db.shape, 1)
            for u in range(nc):
                d_ebl = jnp.sum(dsts[u + 1] * sts[u], axis=1, keepdims=True)
                dbl = jnp.sum(dkoko[:, sl[u]], axis=1, keepdims=True) + d_ebl * ebls[u]
                db = db + jnp.where(rowi == (u + 1) * cl - 1, dbl, 0.0)
            dlogf = _mask_dot_heads(w["anti"].astype(bf16), db)
            df = dlogf / w["f"] - dk
            sg = w["sg"]
            put(dq_ref, rows, dqi * w["e"])
            put(df_ref, rows, df * (1.0 - lbv) * sg * (1.0 - sg))
            put(di_ref, rows, dv)
            dlb = dlb + jnp.sum(df * (1.0 - sg), axis=1, keepdims=True)
            return dsts[0], dlb, dhg

        zero = jnp.zeros((hb, 1, A_DK), f32)
        ngroups = nreal // un

        def step(it, carry):
            grp = ngroups - 1 - it
            rows = pl.ds(pl.multiple_of(nm + grp * (un * A_CHUNK), BF16_ROWS), un * A_CHUNK)
            return run(rows, 1 + grp * un, carry, un, A_CHUNK)

        carry = lax.fori_loop(0, ngroups, step, (jnp.zeros((hb, A_DK, A_DK), f32), zero, zero))
        _, dlb, dhg = run(pl.ds(0, nm), 0, carry, 1, nm)

        @pl.when(pl.program_id(1) == 0)
        def _():
            dlb_ref[...] = jnp.zeros_like(dlb_ref)
            dhg_ref[...] = jnp.zeros_like(dhg_ref)

        for h in range(hb):
            dlb_ref[:, h * A_DK:(h + 1) * A_DK] += dlb[h]
            dhg_ref[:, h * A_DK:(h + 1) * A_DK] += dhg[h]

    col = lambda o: pl.BlockSpec((t, wide), lambda h, b: (b, o * ng + h))
    blk = pl.BlockSpec((t, wide), lambda h, b: (b, h))
    vec = pl.BlockSpec((1, wide), lambda h, b: (0, h))
    return pl.pallas_call(
        body, grid=(ng, bl_), name=name,
        in_specs=[col(0), col(1), col(2), col(3),
                  pl.BlockSpec((hb, nch, A_DK, A_DK), lambda h, b: (b * ng + h, 0, 0, 0)), blk, vec, vec],
        out_specs=[blk, blk, blk, blk, vec, vec],
        out_shape=[SDS((n, d), bf16)] * 4 + [SDS((1, d), f32)] * 2, compiler_params=_params(),
    )(pmat, pmat, pmat, pmat, ss, dog, lb, hg)


def _shifted(x, halo, before):
    n = x.shape[0]
    both = jnp.concatenate([halo, x] if before else [x, halo], axis=0)
    row, col = _iota2((n, n + BF16_ROWS), 0), _iota2((n, n + BF16_ROWS), 1)
    src = row + BF16_ROWS if before else row
    step = -1 if before else 1
    pick = lambda s: jnp.dot((col == src + step * s).astype(bf16), both, preferred_element_type=f32)
    return pick(1), pick(2)


def _conv3(xb, halo, w):
    x = xb.astype(f32)
    x1, x2 = _shifted(xb, halo, True)
    return x, x1, x2, w[0:1, :] * x2 + w[1:2, :] * x1 + w[2:3, :] * x


def _conv_gate_fwd(ug, uv, cwg, cwv, bl_, t, tc, name):
    n, ff = ug.shape
    nt = t // tc

    def body(ug_ref, uv_ref, wg_ref, wv_ref, a_ref, hg_ref, hv_ref):
        @pl.when(pl.program_id(1) == 0)
        def _():
            hg_ref[...] = jnp.zeros_like(hg_ref)
            hv_ref[...] = jnp.zeros_like(hv_ref)

        xg, xv = ug_ref[...], uv_ref[...]
        cg = _conv3(xg, hg_ref[...], wg_ref[...])[3]
        cv = _conv3(xv, hv_ref[...], wv_ref[...])[3]
        a_ref[...] = (cg * _sigmoid(cg) * cv).astype(a_ref.dtype)
        hg_ref[...] = xg[tc - BF16_ROWS:tc, :].astype(hg_ref.dtype)
        hv_ref[...] = xv[tc - BF16_ROWS:tc, :].astype(hv_ref.dtype)

    row = pl.BlockSpec((tc, ff), lambda b, i: (b * nt + i, 0))
    wsp = pl.BlockSpec((3, ff), lambda b, i: (0, 0))
    return pl.pallas_call(
        body, grid=(bl_, nt), name=name, in_specs=[row, row, wsp, wsp], out_specs=row,
        out_shape=SDS((n, ff), bf16),
        scratch_shapes=[pltpu.VMEM((BF16_ROWS, ff), bf16), pltpu.VMEM((BF16_ROWS, ff), bf16)], compiler_params=_params(),
    )(ug, uv, cwg, cwv)


def _conv_gate_bwd(ug, uv, cwg, cwv, da, bl_, t, tc, name):
    n, ff = ug.shape
    nt = t // tc
    per = tc // BF16_ROWS

    def body(ug_ref, uv_ref, pg_ref, pv_ref, wg_ref, wv_ref, da_ref, dug_ref, duv_ref, dwg_ref, dwv_ref, ng_ref, nv_ref):
        first = jnp.logical_and(pl.program_id(0) == 0, pl.program_id(1) == 0)

        @pl.when(first)
        def _():
            dwg_ref[...] = jnp.zeros_like(dwg_ref)
            dwv_ref[...] = jnp.zeros_like(dwv_ref)

        @pl.when(pl.program_id(1) == 0)
        def _():
            ng_ref[...] = jnp.zeros_like(ng_ref)
            nv_ref[...] = jnp.zeros_like(nv_ref)

        seq_start = pl.program_id(1) == nt - 1
        dav = da_ref[...].astype(f32)

        def half(u_ref, p_ref, w_ref):
            halo = p_ref[...]
            return _conv3(u_ref[...], jnp.where(seq_start, jnp.zeros_like(halo), halo), w_ref[...])

        xg, xg1, xg2, cg = half(ug_ref, pg_ref, wg_ref)
        xv, xv1, xv2, cv = half(uv_ref, pv_ref, wv_ref)
        sg = _sigmoid(cg)
        dcg = dav * cv * (sg * (1.0 + cg * (1.0 - sg)))
        dcv = dav * (cg * sg)

        def back(dc, x, x1, x2, w_ref, nx_ref, du_ref, dw_ref):
            w = w_ref[...]
            dcb = dc.astype(bf16)
            dc1, dc2 = _shifted(dcb, nx_ref[...], False)
            du = w[2:3, :] * dc + w[1:2, :] * dc1 + w[0:1, :] * dc2
            du_ref[...] = du.astype(du_ref.dtype)
            dw_ref[0:1, :] += jnp.sum(dc * x2, axis=0, keepdims=True)
            dw_ref[1:2, :] += jnp.sum(dc * x1, axis=0, keepdims=True)
            dw_ref[2:3, :] += jnp.sum(dc * x, axis=0, keepdims=True)
            nx_ref[...] = dcb[0:BF16_ROWS, :].astype(nx_ref.dtype)

        back(dcg, xg, xg1, xg2, wg_ref, ng_ref, dug_ref, dwg_ref)
        back(dcv, xv, xv1, xv2, wv_ref, nv_ref, duv_ref, dwv_ref)

    row = pl.BlockSpec((tc, ff), lambda b, i: (b * nt + nt - 1 - i, 0))
    prev = pl.BlockSpec((BF16_ROWS, ff), lambda b, i: (jnp.maximum((b * nt + nt - 1 - i) * per - 1, 0), 0))
    wsp = pl.BlockSpec((3, ff), lambda b, i: (0, 0))
    return pl.pallas_call(
        body, grid=(bl_, nt), name=name, in_specs=[row, row, prev, prev, wsp, wsp, row],
        out_specs=[row, row, wsp, wsp],
        out_shape=[SDS((n, ff), bf16), SDS((n, ff), bf16), SDS((3, ff), f32), SDS((3, ff), f32)],
        scratch_shapes=[pltpu.VMEM((BF16_ROWS, ff), bf16), pltpu.VMEM((BF16_ROWS, ff), bf16)], compiler_params=_params(),
    )(ug, uv, ug, uv, cwg, cwv, da)


def _zf_c(hk, wzf, fgb, bl_, t, tm, name):
    n, d = hk.shape
    nt = t // tm

    def body(hk_ref, w_ref, b_ref, zf_ref, c_ref, carry_ref):
        @pl.when(pl.program_id(1) == 0)
        def _():
            carry_ref[...] = jnp.zeros_like(carry_ref)

        z = _bdot(hk_ref[...], w_ref[...]) + b_ref[...]
        ls = jnp.minimum(z, 0.0) - jnp.log(1.0 + jnp.exp(-jnp.abs(z)))
        c = _cumsum_rows(ls) + carry_ref[...]
        zf_ref[...] = z
        c_ref[...] = c
        carry_ref[...] = c[tm - 1:tm, :]

    row = lambda w: pl.BlockSpec((tm, w), lambda b, i: (b * nt + i, 0))
    return pl.pallas_call(
        body, grid=(bl_, nt), name=name,
        in_specs=[row(d), pl.BlockSpec((d, LANES), lambda b, i: (0, 0)), pl.BlockSpec((1, LANES), lambda b, i: (0, 0))],
        out_specs=[row(LANES), row(LANES)],
        out_shape=[SDS((n, LANES), f32), SDS((n, LANES), f32)],
        scratch_shapes=[pltpu.VMEM((1, LANES), f32)], compiler_params=_params(),
    )(hk, wzf, fgb)


def _c_bwd(dc, zf, bl_, t, tm, name):
    n = dc.shape[0]
    nt = t // tm

    def body(dc_ref, zf_ref, dzf_ref, dfg_ref, carry_ref):
        @pl.when(jnp.logical_and(pl.program_id(0) == 0, pl.program_id(1) == 0))
        def _():
            dfg_ref[...] = jnp.zeros_like(dfg_ref)

        @pl.when(pl.program_id(1) == 0)
        def _():
            carry_ref[...] = jnp.zeros_like(carry_ref)

        rc = _revcumsum_rows(dc_ref[...]) + carry_ref[...]
        dz = rc * _sigmoid(-zf_ref[...])
        dzf_ref[...] = dz.astype(dzf_ref.dtype)
        dfg_ref[...] += jnp.sum(dz, axis=0, keepdims=True)
        carry_ref[...] = rc[0:1, :]

    row = pl.BlockSpec((tm, LANES), lambda b, i: (b * nt + nt - 1 - i, 0))
    vec = pl.BlockSpec((1, LANES), lambda b, i: (0, 0))
    return pl.pallas_call(
        body, grid=(bl_, nt), name=name, in_specs=[row, row], out_specs=[row, vec],
        out_shape=[SDS((n, LANES), bf16), SDS((1, LANES), f32)],
        scratch_shapes=[pltpu.VMEM((1, LANES), f32)], compiler_params=_params(),
    )(dc, zf)


def _is_pow2(x):
    m, _ = math.frexp(x)
    return m == 0.5


def _prescale(qh, scale):
    return (qh.astype(f32) * scale).astype(bf16)


def _attn_fwd(q, kv, ck, bl_, t, tq, hd, name):
    n, d = q.shape
    npair = d // LANES
    hp = LANES // hd
    nq = t // tq
    scale = 1.0 / (hd ** 0.5)

    pre = _is_pow2(scale)

    def body(q_ref, k_ref, v_ref, ck_ref, o_ref, lse_ref):
        i = pl.program_id(2)
        diag = _iota2((tq, tq), 0) >= _iota2((tq, tq), 1)
        for hh in range(hp):
            lanes = slice(hh * hd, (hh + 1) * hd)
            qh = _prescale(q_ref[:, lanes], scale) if pre else q_ref[:, lanes]

            def block(j, carry, masked, lanes=lanes, qh=qh, hh=hh):
                m, l, acc = carry
                rows = pl.ds(pl.multiple_of(j * tq, BF16_ROWS), tq)
                s = _bdot_nt(qh, k_ref[rows, lanes])
                s = (s if pre else s * scale) - ck_ref[0, 0, j, hh:hh + 1, :]
                if masked:
                    s = jnp.where(diag, s, -1e30)
                m2 = jnp.maximum(m, jnp.max(s, axis=-1, keepdims=True))
                p = jnp.exp(s - m2)
                a = jnp.exp(m - m2)
                return m2, a * l + jnp.sum(p, axis=-1, keepdims=True), a * acc + _bdot(p, v_ref[rows, lanes])

            init = (jnp.full((tq, 1), -1e30, f32), jnp.zeros((tq, 1), f32), jnp.zeros((tq, hd), f32))
            carry = lax.fori_loop(0, i, functools.partial(block, masked=False), init)
            m, l, acc = block(i, carry, True)
            o_ref[:, lanes] = (acc / l).astype(o_ref.dtype)
            lse_ref[:, lanes] = jnp.broadcast_to(m + jnp.log(l), (tq, hd))

    nk = nq
    return pl.pallas_call(
        body, grid=(bl_, npair, nq), name=name,
        in_specs=[pl.BlockSpec((tq, LANES), lambda b, p, i: (b * nq + i, p)),
                  pl.BlockSpec((t, LANES), lambda b, p, i: (b, p)),
                  pl.BlockSpec((t, LANES), lambda b, p, i: (b, npair + p)),
                  pl.BlockSpec((1, 1, nk, hp, tq), lambda b, p, i: (b, p, 0, 0, 0))],
        out_specs=[pl.BlockSpec((tq, LANES), lambda b, p, i: (b * nq + i, p)),
                   pl.BlockSpec((tq, LANES), lambda b, p, i: (b * nq + i, p))],
        out_shape=[SDS((n, d), f32), SDS((n, d), f32)], compiler_params=_params(),
    )(q, kv, kv, ck)


def _attn_bwd(q, kv, o, do, lse, ck, bl_, t, tq, hd, name):
    n, d = q.shape
    npair = d // LANES
    hp = LANES // hd
    nq = t // tq
    scale = 1.0 / (hd ** 0.5)

    pre = _is_pow2(scale)

    def body(q_ref, k_ref, v_ref, o_ref, do_ref, lse_ref, ck_ref, dq_ref, dk_ref, dv_ref, dck_ref, dcq_ref):
        j = pl.program_id(2)

        @pl.when(j == 0)
        def _():
            dq_ref[...] = jnp.zeros_like(dq_ref)
            dcq_ref[...] = jnp.zeros_like(dcq_ref)

        diag = _iota2((tq, tq), 0) >= _iota2((tq, tq), 1)
        for hh in range(hp):
            lanes = slice(hh * hd, (hh + 1) * hd)
            kh = k_ref[:, lanes]
            vh = v_ref[:, lanes]
            kt = kh.astype(f32).T.astype(bf16)
            cs = ck_ref[0, 0, 0, hh:hh + 1, :]

            def block(i, carry, masked, lanes=lanes, kh=kh, vh=vh, kt=kt, cs=cs, hh=hh):
                dkt, dvt, dcs = carry
                rows = pl.ds(pl.multiple_of(i * tq, BF16_ROWS), tq)
                qh = _prescale(q_ref[rows, lanes], scale) if pre else q_ref[rows, lanes]
                doh = do_ref[rows, lanes]
                s = _bdot_nt(qh, kh)
                s = (s if pre else s * scale) - cs
                if masked:
                    s = jnp.where(diag, s, -1e30)
                p = jnp.exp(s - lse_ref[rows, hh * hd:hh * hd + 1])
                delta = jnp.sum(doh.astype(f32) * o_ref[rows, lanes].astype(f32), axis=-1, keepdims=True)
                ds = p * (_bdot_nt(doh, vh) - delta)
                dsb = ds.astype(bf16)
                dq_ref[rows, lanes] += _bdot_nt(kt, dsb).T * scale
                dcq_ref[0, rows, hh:hh + 1] += jnp.sum(ds, axis=-1, keepdims=True)
                dkq = _bdot_tn(qh, dsb)
                return (dkt + (dkq if pre else dkq * scale), dvt + _bdot_tn(doh, p), dcs - jnp.sum(ds, axis=0, keepdims=True))

            init = (jnp.zeros((hd, tq), f32), jnp.zeros((hd, tq), f32), jnp.zeros((1, tq), f32))
            dkt, dvt, dcs = lax.fori_loop(j + 1, nq, functools.partial(block, masked=False), block(j, init, True))
            dk_ref[:, lanes] = dkt.T.astype(dk_ref.dtype)
            dv_ref[:, lanes] = dvt.T.astype(dv_ref.dtype)
            dck_ref[0, 0, 0, hh:hh + 1, :] = dcs

    whole = lambda c0: pl.BlockSpec((t, LANES), lambda b, p, j: (b, c0 + p))
    tile = lambda c0: pl.BlockSpec((tq, LANES), lambda b, p, j: (b * nq + j, c0 + p))
    ckspec = pl.BlockSpec((1, 1, 1, hp, tq), lambda b, p, j: (b, p, j, 0, 0))
    cqspec = pl.BlockSpec((1, t, hp), lambda b, p, j: (p, b, 0))
    return pl.pallas_call(
        body, grid=(bl_, npair, nq), name=name,
        in_specs=[whole(0), tile(0), tile(npair), whole(0), whole(0), whole(0), ckspec],
        out_specs=[whole(0), tile(0), tile(0), ckspec, cqspec],
        out_shape=[SDS((n, d), f32), SDS((n, d), bf16), SDS((n, d), bf16), SDS((bl_, npair, nq, hp, tq), f32),
                   SDS((npair, n, hp), f32)],
        compiler_params=_params(),
    )(q, kv, kv, o, do, lse, ck)


def _loss_head(h, target, t, nm, tm, name):
    n, d = h.shape
    nt = t // tm

    def body(h_ref, t_ref, loss_ref, dh_ref):
        i = pl.program_id(0)

        @pl.when(i == 0)
        def _():
            loss_ref[...] = jnp.zeros_like(loss_ref)

        pos = (i % nt) * tm + _iota2((tm, d), 0)
        err = jnp.where(pos >= nm, h_ref[...] - t_ref[...], 0.0)
        dh_ref[...] = err * (1.0 / d)
        loss_ref[...] += 0.5 * jnp.sum(jnp.mean(err * err, axis=-1, keepdims=True))

    row = pl.BlockSpec((tm, d), lambda i: (i, 0))
    return pl.pallas_call(
        body, grid=(n // tm,), name=name, in_specs=[row, row],
        out_specs=[pl.BlockSpec((8, LANES), lambda i: (0, 0)), row],
        out_shape=[SDS((8, LANES), f32), SDS((n, d), f32)], compiler_params=_params(),
    )(h, target)


def _c_key_rows(c, bl_, t, tq, bh, hp):
    npair = bh // hp
    nk = t // tq
    return c[:, :bh].reshape(bl_, nk, tq, npair, hp).transpose(0, 3, 1, 4, 2)


def _dc_rows(dck, dcq, bl_, t, bh):
    d = dck.transpose(0, 2, 4, 1, 3).reshape(bl_ * t, bh) + dcq.transpose(1, 0, 2).reshape(bl_ * t, bh)
    return jnp.pad(d, ((0, 0), (0, LANES - bh)))


_ANY = pl.BlockSpec(memory_space=pl.ANY)


def _all_gather(xs, name):
    na = len(xs)

    def body(*refs):
        x_refs, out_refs = refs[:na], refs[na:2 * na]
        send_sems, recv_sems, local_sems = refs[2 * na:]
        mx, my, mc = lax.axis_index("x"), lax.axis_index("y"), lax.axis_index("c")
        me, sibling = (mx, my, mc), (mx, my, 1 - mc)
        chips = [(1 - mx, my), (mx, 1 - my), (1 - mx, 1 - my)]

        def copy(a, k, block, to, own=False):
            px, py, pc = block
            rows = out_refs[a].at[4 * px + 2 * py + pc]
            return pltpu.make_async_remote_copy(
                src_ref=x_refs[a] if own else rows, dst_ref=rows,
                send_sem=send_sems.at[a, k], recv_sem=recv_sems.at[a, k], device_id=to, device_id_type=MESH)

        arrays = range(na)
        mine = [pltpu.make_async_copy(x_refs[a], out_refs[a].at[4 * mx + 2 * my + mc], local_sems.at[a]) for a in arrays]
        for cp in mine:
            cp.start()
        first = [copy(a, 1 + j, me, (*chip, mc), own=True) for j, chip in enumerate(chips) for a in arrays]
        first += [copy(a, 0, me, sibling, own=True) for a in arrays]
        for cp in first:
            cp.start()
        passed = []
        for j, chip in enumerate(chips):
            for a in arrays:
                copy(a, 1 + j, (*chip, mc), me).wait_recv()
                cp = copy(a, 4 + j, (*chip, mc), sibling)
                cp.start()
                passed.append(cp)
        for a in arrays:
            copy(a, 0, sibling, me).wait_recv()
        for j, chip in enumerate(chips):
            for a in arrays:
                copy(a, 4 + j, (*chip, 1 - mc), me).wait_recv()
        for cp in first + passed:
            cp.wait_send()
        for cp in mine:
            cp.wait()

    return pl.pallas_call(
        body, name=name, out_shape=[SDS((N_DEV,) + x.shape, x.dtype) for x in xs],
        in_specs=[_ANY] * na, out_specs=[_ANY] * na,
        scratch_shapes=[pltpu.SemaphoreType.DMA((na, 7)), pltpu.SemaphoreType.DMA((na, 7)), pltpu.SemaphoreType.DMA((na,))],
    )(*xs)


_HBM = pl.BlockSpec(memory_space=pltpu.HBM)
_SEM = pl.BlockSpec(memory_space=pltpu.SEMAPHORE)
_DATAFLOW = pltpu.SideEffectType.DATAFLOW_SIDE_EFFECTING
N_PEERS = N_DEV - 1


def _device_index():
    return 4 * lax.axis_index("x") + 2 * lax.axis_index("y") + lax.axis_index("c")


def _peers():
    mx, my, mc = lax.axis_index("x"), lax.axis_index("y"), lax.axis_index("c")
    peers = []
    for r in (2, 3, 4, 5, 6, 7, 1):
        px = 1 - mx if r & 4 else mx
        py = 1 - my if r & 2 else my
        pc = 1 - mc if r & 1 else mc
        peers.append(((px, py, pc), 4 * px + 2 * py + pc))
    return 4 * mx + 2 * my + mc, peers


def _push_copy(src_ref, land_ref, send_sems, recv_sems, a, k, dev, src_row, land_row, scatter):
    return pltpu.make_async_remote_copy(
        src_ref=src_ref.at[src_row] if scatter else src_ref, dst_ref=land_ref.at[land_row],
        send_sem=send_sems.at[a * N_PEERS + k], recv_sem=recv_sems.at[a * N_PEERS + k], device_id=dev, device_id_type=MESH)


def _landing(own, me):
    return lax.dynamic_update_index_in_dim(lax.empty((N_DEV,) + own.shape, own.dtype), own, me, 0)


def _push_start(srcs, lands, scatter, name):
    na = len(srcs)

    def body(*refs):
        src_refs, land_refs = refs[:na], refs[na:2 * na]
        send_sems, recv_sems = refs[2 * na], refs[2 * na + 1]
        token = refs[-1]
        me, peers = _peers()
        for a in range(na):
            for k, (dev, idx) in enumerate(peers):
                _push_copy(src_refs[a], land_refs[a], send_sems, recv_sems, a, k, dev, idx, me, scatter).start()
        token[...] = jnp.zeros_like(token)

    hbm = lambda arrs: [pltpu.HBM(a.shape, a.dtype) for a in arrs]
    out = pl.pallas_call(
        body, name=name,
        out_shape=(pltpu.SemaphoreType.DMA((na * N_PEERS,)), pltpu.SemaphoreType.DMA((na * N_PEERS,)), *hbm(srcs), *hbm(lands),
                   SDS((8, LANES), f32)),
        in_specs=[_HBM] * (2 * na),
        out_specs=(_SEM, _SEM, *([_HBM] * (2 * na)), pl.BlockSpec(memory_space=pltpu.VMEM)),
        input_output_aliases={i: 2 + i for i in range(2 * na)},
        compiler_params=pltpu.CompilerParams(has_side_effects=_DATAFLOW),
    )(*[pltpu.with_memory_space_constraint(a, pltpu.HBM) for a in list(srcs) + list(lands)])
    return out[0], out[1], list(out[2:2 + na]), list(out[2 + na:2 + 2 * na]), out[-1]


def _push_wait(send_sems, recv_sems, srcs, lands, which, after, scatter, name):
    nw = len(which)

    def body(*refs):
        src_refs, land_refs = refs[:nw], refs[nw:2 * nw]
        send_sems_, recv_sems_ = refs[2 * nw], refs[2 * nw + 1]
        _, peers = _peers()
        for j, a in enumerate(which):
            for k, (dev, idx) in enumerate(peers):
                cp = _push_copy(src_refs[j], land_refs[j], send_sems_, recv_sems_, a, k, dev, idx, idx, scatter)
                cp.wait_send()
                cp.wait_recv()

    hbm = lambda arrs: [pltpu.HBM(a.shape, a.dtype) for a in arrs]
    out = pl.pallas_call(
        body, name=name, out_shape=(*hbm(srcs), *hbm(lands)),
        in_specs=[_HBM] * (2 * nw) + [_SEM, _SEM, _ANY], out_specs=[_HBM] * (2 * nw),
        input_output_aliases={i: i for i in range(2 * nw)},
        compiler_params=pltpu.CompilerParams(has_side_effects=_DATAFLOW),
    )(*srcs, *lands, send_sems, recv_sems, after)
    return list(out[nw:])


def _adamw(parts, w, m, v, tr, name):
    g, r, c = parts.shape

    def body(p_ref, w_ref, m_ref, v_ref, g_ref, d_ref, m2_ref, v2_ref):
        gr = p_ref[0].astype(f32)
        for k in range(1, g):
            gr = gr + p_ref[k].astype(f32)
        m2 = ADAM_B1 * m_ref[...] + (1.0 - ADAM_B1) * gr
        v2 = ADAM_B2 * v_ref[...] + (1.0 - ADAM_B2) * (gr * gr)
        m_hat = m2 / (1.0 - ADAM_B1 ** ADAM_STEP)
        v_hat = v2 / (1.0 - ADAM_B2 ** ADAM_STEP)
        g_ref[...] = gr
        d_ref[...] = -ADAM_LR * (m_hat / (jnp.sqrt(v_hat) + ADAM_EPS) + ADAM_WD * w_ref[...])
        m2_ref[...] = m2
        v2_ref[...] = v2

    row = pl.BlockSpec((tr, c), lambda i: (i, 0))
    return pl.pallas_call(
        body, grid=(r // tr,), name=name, in_specs=[pl.BlockSpec((g, tr, c), lambda i: (0, i, 0)), row, row, row],
        out_specs=[row] * 4, out_shape=[SDS((r, c), f32)] * 4, compiler_params=_params(),
    )(parts, w, m, v)


_SHARD_AXIS = dict(meta_tokens=1, norm_gains=2, a_w_in=2, a_lb_logits=1, a_head_norm=1, a_w_out=1, kv_w=1,
                   b_w_q=1, b_w_out=1, ffn_w_up=2, ffn_conv=2, ffn_w_down=1)
_VECTORS = ("meta_tokens", "norm_gains", "a_lb_logits", "a_head_norm", "ffn_conv")
_REPLICATED = ("kv_norm", "fg_b")
_ROW_TILE_CAP = 512


def _pack(arrs, dtype, cols, row_mult):
    lead = arrs[0].shape[:-1] if arrs[0].ndim > 1 else ()
    flat = jnp.concatenate([a.astype(dtype) for a in arrs], axis=-1)
    size = flat.shape[-1]
    per = cols * row_mult
    total = -(-size // per) * per
    flat = jnp.pad(flat, [(0, 0)] * len(lead) + [(0, total - size)])
    return flat.reshape(lead + (total // cols, cols))


def _unpack(flat, shapes):
    out, off = [], 0
    lead = flat.shape[:-1]
    for shp in shapes:
        size = 1
        for s in shp:
            size *= s
        out.append(flat[..., off:off + size].reshape(lead + tuple(shp)))
        off += size
    return out


def _unshard(seg, axis):
    a = jnp.moveaxis(seg, 0, axis)
    shp = a.shape
    return a.reshape(shp[:axis] + (shp[axis] * shp[axis + 1],) + shp[axis + 2:])


def _shard8(full, axis):
    shp = full.shape
    a = full.reshape(shp[:axis] + (N_DEV, shp[axis] // N_DEV) + shp[axis + 1:])
    return jnp.moveaxis(a, axis, 0)


def _rows(a, lead=0):
    return a.reshape(a.shape[:lead] + (-1, a.shape[-1]))


def kernel(x, meta_tokens, norm_gains, a_w_in, a_lb_logits, a_head_norm, a_w_out, kv_norm, kv_w, fg_b, b_w_q, b_w_out, ffn_w_up, ffn_conv, ffn_w_down, loss_target, m_meta_tokens, m_norm_gains, m_a_w_in, m_a_lb_logits, m_a_head_norm, m_a_w_out, m_kv_norm, m_kv_w, m_fg_b, m_b_w_q, m_b_w_out, m_ffn_w_up, m_ffn_conv, m_ffn_w_down, v_meta_tokens, v_norm_gains, v_a_w_in, v_a_lb_logits, v_a_head_norm, v_a_w_out, v_kv_norm, v_kv_w, v_fg_b, v_b_w_q, v_b_w_out, v_ffn_w_up, v_ffn_conv, v_ffn_w_down):
    names = ("meta_tokens", "norm_gains", "a_w_in", "a_lb_logits", "a_head_norm", "a_w_out", "kv_norm", "kv_w", "fg_b",
             "b_w_q", "b_w_out", "ffn_w_up", "ffn_conv", "ffn_w_down")
    w = dict(zip(names, (meta_tokens, norm_gains, a_w_in, a_lb_logits, a_head_norm, a_w_out, kv_norm, kv_w, fg_b,
                         b_w_q, b_w_out, ffn_w_up, ffn_conv, ffn_w_down)))
    mom = dict(zip(names, (m_meta_tokens, m_norm_gains, m_a_w_in, m_a_lb_logits, m_a_head_norm, m_a_w_out, m_kv_norm,
                           m_kv_w, m_fg_b, m_b_w_q, m_b_w_out, m_ffn_w_up, m_ffn_conv, m_ffn_w_down)))
    var = dict(zip(names, (v_meta_tokens, v_norm_gains, v_a_w_in, v_a_lb_logits, v_a_head_norm, v_a_w_out, v_kv_norm,
                           v_kv_w, v_fg_b, v_b_w_q, v_b_w_out, v_ffn_w_up, v_ffn_conv, v_ffn_w_down)))

    bl_, seq, d = x.shape
    nm = meta_tokens.shape[0]
    t = nm + seq
    n = bl_ * t
    bh = fg_b.shape[0]
    hd = d // bh
    hp = LANES // hd
    ff = ffn_w_down.shape[1] * N_DEV
    tm = _div_tile(t, TOKEN_TILE_CAP)
    tc = _div_tile(t, 64)
    tn = min(d, MODEL_TILE_CAP)

    vec_pack = _pack([w[k].reshape(-1) for k in _VECTORS], f32, LANES, 8)
    first = _all_gather([w["a_w_in"].astype(bf16), vec_pack], "gather_first")
    vec_segs = _unpack(first[1].reshape(N_DEV, -1), [w[k].shape for k in _VECTORS])
    small = {k: _unshard(a, _SHARD_AXIS[k]) for k, a in zip(_VECTORS, vec_segs)}
    w_in = _unshard(first[0], _SHARD_AXIS["a_w_in"])[0]
    me = _device_index()
    later_names = ("a_w_out", "ffn_w_up", "ffn_w_down", "kv_w", "b_w_q", "b_w_out", "ffn_w_up", "ffn_w_down")
    later_layer = (None, 0, 0, None, None, None, 1, 1)
    later = [(w[k] if l is None else w[k][l]).astype(bf16) for k, l in zip(later_names, later_layer)]
    later, _ = lax.optimization_barrier((later, first[1]))
    g_send, g_recv, later_src, later_land, _ = _push_start(later, [_landing(a, me) for a in later], False, "gather_rest_start")

    def gathered(which, after, name):
        lands = _push_wait(g_send, g_recv, [later_src[i] for i in which], [later_land[i] for i in which], which, after,
                           False, name)
        return [_unshard(a, _SHARD_AXIS[later_names[i]] - (later_layer[i] is not None)) for i, a in zip(which, lands)]

    gains_box = [small["norm_gains"]]
    gain = lambda l, j: gains_box[0][l, j][None]
    cw_gate, cw_val = small["ffn_conv"][:, :, :ff], small["ffn_conv"][:, :, ff:]
    head_gain = small["a_head_norm"]
    lb = jax.nn.softmax(small["a_lb_logits"], axis=0)[0:1]
    kvn = kv_norm[None]
    fgb_pad = jnp.pad(fg_b, (0, LANES - bh))[None]

    h0 = jnp.concatenate([jnp.broadcast_to(small["meta_tokens"][None], (bl_, nm, d)), x], axis=1).reshape(n, d)

    def ffn_fwd(l, h_in):
        fi = _rms_fwd(h_in, gain(l, 2), tm, f"ffn{l}_norm")
        ug = _mm(fi, w_gate[l], bf16, tm, ff, f"ffn{l}_up_gate")
        uv = _mm(fi, w_val[l], bf16, tm, ff, f"ffn{l}_up_val")
        act = _conv_gate_fwd(ug, uv, cw_gate[l], cw_val[l], bl_, t, tc, f"ffn{l}_conv_gate")
        h_out, mix = _mm_norm_res(act, w_down[l], gain(l, 3), h_in, tm, f"ffn{l}_down")
        return h_out, (h_in, fi, ug, uv, act, mix)

    hn0 = _rms_fwd(h0, gain(0, 0), tm, "a_norm")
    pmat = _mm(hn0, w_in, f32, tm, tn, "a_in_proj")
    og, states = _gla_fwd(pmat, lb, head_gain, bl_, t, nm, "a_gla_fwd")
    w_out_a = gathered((0,), og, "gather_wait_a")[0][0]
    h1, mix_a = _mm_norm_res(og, w_out_a, gain(0, 1), h0, tm, "a_out_proj")
    w_gate, w_val, w_down = {}, {}, {}

    def ffn_weights(l, which, after):
        w_up, w_down[l] = gathered(which, after, f"gather_wait_ffn{l}")
        w_gate[l], w_val[l] = w_up[:, :ff], w_up[:, ff:]

    ffn_weights(0, (1, 2), h1)
    h2, ffn0 = ffn_fwd(0, h1)

    w_kv_zf, w_q, w_out_b = gathered((3, 4, 5), h2, "gather_wait_b")
    w_kv, w_zf = w_kv_zf[:, :2 * d], jnp.pad(w_kv_zf[:, 2 * d:], ((0, 0), (0, LANES - bh)))
    w_q, w_out_b = w_q[0], w_out_b[0]
    hk = _rms_fwd(h2, kvn, tm, "kv_norm")
    kvp = _mm(hk, w_kv, bf16, tm, tn, "kv_proj")
    zf, cum = _zf_c(hk, w_zf, fgb_pad, bl_, t, tm, "forget_cumsum")
    ck = _c_key_rows(cum, bl_, t, tm, bh, hp)
    hn1 = _rms_fwd(h2, gain(1, 0), tm, "b_norm")
    q = _mm(hn1, w_q, bf16, tm, tn, "b_q_proj")
    o, lse = _attn_fwd(q, kvp, ck, bl_, t, tm, hd, "b_attn_fwd")
    h3, mix_b = _mm_norm_res(o, w_out_b, gain(1, 1), h2, tm, "b_out_proj")
    ffn_weights(1, (6, 7), h3)
    h4, ffn1 = ffn_fwd(1, h3)

    target = jnp.concatenate([jnp.zeros((bl_, nm, d), f32), loss_target], axis=1).reshape(n, d)
    loss8, dh = _loss_head(h4, target, t, nm, tm, "loss_head")
    loss = lax.psum(loss8[0, 0], ("x", "y", "c"))

    dgain = {}

    def ffn_bwd(l, saved, dh_out):
        h_in, fi, ug, uv, act, mix = saved
        dmix, dgain[l, 3] = _rms_bwd(mix, gain(l, 3), dh_out, None, bf16, tm, f"ffn{l}_down_norm_bwd")
        dact = _mm_nt([(dmix, w_down[l])], bf16, tm, ff, f"ffn{l}_down_dx")
        dw_down = _mm_tn(act, dmix, tm, ff, tn, f"ffn{l}_down_dw")
        dug, duv, dcg, dcv = _conv_gate_bwd(ug, uv, cw_gate[l], cw_val[l], dact, bl_, t, tc, f"ffn{l}_conv_gate_bwd")
        dfi = _mm_nt([(dug, w_gate[l]), (duv, w_val[l])], bf16, tm, tn // 2, f"ffn{l}_up_dx")
        dw_up = jnp.concatenate([_mm_tn(fi, dug, tm, tn, ff, f"ffn{l}_up_gate_dw"),
                                 _mm_tn(fi, duv, tm, tn, ff, f"ffn{l}_up_val_dw")], axis=1)
        dh_in, dgain[l, 2] = _rms_bwd(h_in, gain(l, 2), dfi, dh_out, f32, tm, f"ffn{l}_norm_bwd")
        return dh_in, dw_up, jnp.concatenate([dcg, dcv], axis=1), dw_down

    def shards(full, axis):
        return _rows(_shard8(full, axis), 1).astype(bf16)

    def push_grads(bufs, name):
        lands = [_landing(lax.dynamic_index_in_dim(b, me, 0, keepdims=False), me) for b in bufs]
        s_sem, r_sem, srcs, lands, token = _push_start(bufs, lands, True, name)
        gains_box[0] = gains_box[0] + token[0, 0]
        return s_sem, r_sem, srcs, lands

    def landed(handle, after, name):
        s_sem, r_sem, srcs, lands = handle
        return _push_wait(s_sem, r_sem, srcs, lands, tuple(range(len(srcs))), after, True, name)

    dh, dw_up1, dconv1, dw_down1 = ffn_bwd(1, ffn1, dh)
    push1 = push_grads([shards(dw_up1, 1), shards(dw_down1, 0)], "grad_push_ffn1")

    dmix, dgain[1, 1] = _rms_bwd(mix_b, gain(1, 1), dh, None, bf16, tm, "b_out_norm_bwd")
    do = _mm_nt([(dmix, w_out_b)], bf16, tm, tn, "b_out_dx")
    dw_out_b = _mm_tn(o, dmix, tm, tn, tn, "b_out_dw")
    dq, dk, dv, dck, dcq = _attn_bwd(q, kvp, o, do, lse, ck, bl_, t, tm, hd, "b_attn_bwd")
    dhn1 = _mm_nt([(dq, w_q)], bf16, tm, tn, "b_q_dx")
    dw_q = _mm_tn(hn1, dq, tm, tn, tn, "b_q_dw")
    dh, dgain[1, 0] = _rms_bwd(h2, gain(1, 0), dhn1, dh, f32, tm, "b_norm_bwd")

    dzf, dfgb = _c_bwd(_dc_rows(dck, dcq, bl_, t, bh), zf, bl_, t, tm, "forget_cumsum_bwd")
    dhk = _mm_nt([(dk, w_kv[:, :d]), (dv, w_kv[:, d:]), (dzf, w_zf)], bf16, tm, tn, "kv_dx")
    dw_kv = jnp.concatenate([_mm_tn(hk, dk, tm, tn, tn, "k_dw"), _mm_tn(hk, dv, tm, tn, tn, "v_dw"),
                             _mm_tn(hk, dzf, tm, tn, LANES, "zf_dw")[:, :bh]], axis=1)
    dh, dkvn = _rms_bwd(h2, kvn, dhk, dh, f32, tm, "kv_norm_bwd")
    push2 = push_grads([shards(dw_out_b, 0), shards(dw_q, 0), shards(dw_kv, 1)], "grad_push_b")

    dh, dw_up0, dconv0, dw_down0 = ffn_bwd(0, ffn0, dh)
    push3 = push_grads([shards(dw_up0, 1), shards(dw_down0, 0)], "grad_push_ffn0")

    dmix, dgain[0, 1] = _rms_bwd(mix_a, gain(0, 1), dh, None, bf16, tm, "a_out_norm_bwd")
    dog = _mm_nt([(dmix, w_out_a)], bf16, tm, tn, "a_out_dx")
    dw_out_a = _mm_tn(og, dmix, tm, tn, tn, "a_out_dw")
    dpq, dpf, dpi, dpg, dlb, dhg = _gla_bwd(pmat, states, dog, lb, head_gain, bl_, t, nm, "a_gla_bwd")
    dps = (dpq, dpf, dpi, dpg)
    dw_in = jnp.concatenate([_mm_tn(hn0, dp, tm, tn, tn, f"a_in_dw{j}") for j, dp in enumerate(dps)], axis=1)
    push4 = push_grads([shards(dw_out_a, 0), shards(dw_in, 1)], "grad_push_a")
    dhn0 = _mm_nt([(dp, w_in[:, j * d:(j + 1) * d]) for j, dp in enumerate(dps)], bf16, tm, tn, "a_in_dx")
    dh, dgain[0, 0] = _rms_bwd(h0, gain(0, 0), dhn0, dh, f32, tm, "a_norm_bwd")

    dh = dh.reshape(bl_, t, d)
    grad_x = dh[:, nm:]
    dl0 = dlb * lb * (1.0 - lb)
    vec_grads = dict(
        meta_tokens=jnp.sum(dh[:, :nm], axis=0),
        norm_gains=jnp.stack([jnp.concatenate([dgain[l, j] for j in range(4)], axis=0) for l in range(2)]),
        a_lb_logits=jnp.concatenate([dl0, -dl0], axis=0), a_head_norm=dhg, ffn_conv=jnp.stack([dconv0, dconv1]))
    vec_send = _pack([_shard8(vec_grads[k], _SHARD_AXIS[k]).reshape(N_DEV, -1) for k in _VECTORS], bf16, LANES, BF16_ROWS)
    push5 = push_grads([vec_send], "grad_push_vectors")

    g_s, d_s, m_s, v_s = {}, {}, {}, {}
    outs = (g_s, d_s, m_s, v_s)

    def update(part, srcs, label):
        rows = part.shape[1]
        return _adamw(part, *srcs, rows if rows <= _ROW_TILE_CAP else _div_tile(rows, _ROW_TILE_CAP), label)

    def update_matrix(k, part, layer=None):
        pick = (lambda a: a) if layer is None else (lambda a: a[layer])
        label = f"adamw_{k}" if layer is None else f"adamw_{k}{layer}"
        res = update(part, [_rows(pick(src[k])) for src in (w, mom, var)], label)
        return [r.reshape(pick(w[k]).shape) for r in res]

    def put(k, res):
        for dst, r in zip(outs, res):
            dst[k] = r

    up1, down1 = (update_matrix(k, p, 1) for k, p in zip(("ffn_w_up", "ffn_w_down"), landed(push1, gains_box[0], "grad_wait_ffn1")))
    for k, p in zip(("b_w_out", "b_w_q", "kv_w"), landed(push2, up1[0], "grad_wait_b")):
        put(k, update_matrix(k, p))
    up0, down0 = (update_matrix(k, p, 0) for k, p in zip(("ffn_w_up", "ffn_w_down"), landed(push3, g_s["kv_w"], "grad_wait_ffn0")))
    put("ffn_w_up", [jnp.stack(pair) for pair in zip(up0, up1)])
    put("ffn_w_down", [jnp.stack(pair) for pair in zip(down0, down1)])
    part_out_a, part_in = landed(push4, down0[0], "grad_wait_a")
    put("a_w_out", update_matrix("a_w_out", part_out_a))
    put("a_w_in", update_matrix("a_w_in", part_in))
    part_vec, = landed(push5, g_s["a_w_in"], "grad_wait_vectors")
    vec_packs = [_pack([src[k].reshape(-1) for k in _VECTORS], f32, LANES, BF16_ROWS) for src in (w, mom, var)]
    vec_shapes = [w[k].shape for k in _VECTORS]
    for dst, r in zip(outs, update(part_vec, vec_packs, "adamw_vectors")):
        dst.update(zip(_VECTORS, _unpack(r.reshape(-1), vec_shapes)))

    rep_local = _pack([dkvn.reshape(-1), dfgb[0, :bh]], f32, LANES, 8)
    rep_parts = _all_gather([rep_local], "gather_replicated_grads")[0]
    rpacks = [_pack([src[k].reshape(-1) for k in _REPLICATED], f32, LANES, 8) for src in (w, mom, var)]
    rres = _adamw(rep_parts, *rpacks, rep_local.shape[0], "adamw_replicated")
    rshapes = [w[k].shape for k in _REPLICATED]
    g_r, d_r, m_r, v_r = ({k: a for k, a in zip(_REPLICATED, _unpack(r.reshape(-1), rshapes))} for r in rres)

    out = [loss, grad_x]
    for sh, rp in ((g_s, g_r), (d_s, d_r), (m_s, m_r), (v_s, v_r)):
        out += [sh[k] if k in sh else rp[k] for k in names]
    return tuple(out)
```

```python
import functools
import math

import jax
import jax.numpy as jnp
from jax import lax
from jax.experimental import pallas as pl
from jax.experimental.pallas import tpu as pltpu

f32 = jnp.float32
bf16 = jnp.bfloat16
SDS = jax.ShapeDtypeStruct

EPS = 1e-6
A_DK = 128
A_CHUNK = 64
GLA_GROUP = 4
GLA_HEADS = 2
TOKEN_TILE_CAP = 1024
MODEL_TILE_CAP = 1024
LANES = 128
SUBLANES = 8
BF16_ROWS = 16
VMEM_LIMIT = 56 * 1024 * 1024
ADAM_LR, ADAM_B1, ADAM_B2, ADAM_EPS, ADAM_WD, ADAM_STEP = 0.001, 0.9, 0.999, 1e-08, 0.01, 10
N_DEV = 8
MESH = pl.DeviceIdType.MESH

_NT = (((1,), (1,)), ((), ()))
_TN = (((0,), (0,)), ((), ()))
_HI = lax.Precision.HIGHEST


def _params(**kw):
    return pltpu.CompilerParams(vmem_limit_bytes=VMEM_LIMIT, **kw)


def _div_tile(n, cap, mult=BF16_ROWS):
    best = None
    for t in range(mult, min(n, cap) + 1, mult):
        if n % t == 0:
            best = t
    assert best is not None, (n, cap, mult)
    return best


def _bdot(a, b):
    return jnp.dot(a.astype(bf16), b.astype(bf16), preferred_element_type=f32)


def _bdot_nt(a, b):
    return lax.dot_general(a.astype(bf16), b.astype(bf16), _NT, preferred_element_type=f32)


def _bdot_tn(a, b):
    return lax.dot_general(a.astype(bf16), b.astype(bf16), _TN, preferred_element_type=f32)


def _iota2(shape, axis):
    return lax.broadcasted_iota(jnp.int32, shape, axis)


def _cumsum_rows(x):
    n = x.shape[0]
    tri = (_iota2((n, n), 0) >= _iota2((n, n), 1)).astype(f32)
    return jnp.dot(tri, x, precision=_HI, preferred_element_type=f32)


def _revcumsum_rows(x):
    n = x.shape[0]
    tri = (_iota2((n, n), 1) >= _iota2((n, n), 0)).astype(f32)
    return jnp.dot(tri, x, precision=_HI, preferred_element_type=f32)


def _sigmoid(x):
    return 1.0 / (1.0 + jnp.exp(-x))


def _rms_fwd(x, g, tm, name):
    n, d = x.shape

    def body(x_ref, g_ref, o_ref):
        xv = x_ref[...]
        r = lax.rsqrt(jnp.mean(xv * xv, axis=-1, keepdims=True) + EPS)
        o_ref[...] = (xv * r * g_ref[...]).astype(o_ref.dtype)

    return pl.pallas_call(
        body, grid=(n // tm,), name=name,
        in_specs=[pl.BlockSpec((tm, d), lambda i: (i, 0)), pl.BlockSpec((1, d), lambda i: (0, 0))],
        out_specs=pl.BlockSpec((tm, d), lambda i: (i, 0)),
        out_shape=SDS((n, d), bf16), compiler_params=_params(),
    )(x, g)


def _mm(a, w, out_dtype, tm, tn, name):
    n, k = a.shape
    m = w.shape[1]

    def body(a_ref, w_ref, o_ref):
        o_ref[...] = _bdot(a_ref[...], w_ref[...]).astype(o_ref.dtype)

    return pl.pallas_call(
        body, grid=(m // tn, n // tm), name=name,
        in_specs=[pl.BlockSpec((tm, k), lambda j, i: (i, 0)), pl.BlockSpec((k, tn), lambda j, i: (0, j))],
        out_specs=pl.BlockSpec((tm, tn), lambda j, i: (i, j)),
        out_shape=SDS((n, m), out_dtype), compiler_params=_params(),
    )(a, w)


def _mm_norm_res(a, w, g, h, tm, name):
    n, k = a.shape
    d = w.shape[1]

    def body(a_ref, w_ref, g_ref, h_ref, hn_ref, mix_ref):
        mix = _bdot(a_ref[...], w_ref[...])
        r = lax.rsqrt(jnp.mean(mix * mix, axis=-1, keepdims=True) + EPS)
        mix_ref[...] = mix
        hn_ref[...] = h_ref[...] + mix * r * g_ref[...]

    return pl.pallas_call(
        body, grid=(n // tm,), name=name,
        in_specs=[pl.BlockSpec((tm, k), lambda i: (i, 0)), pl.BlockSpec((k, d), lambda i: (0, 0)),
                  pl.BlockSpec((1, d), lambda i: (0, 0)), pl.BlockSpec((tm, d), lambda i: (i, 0))],
        out_specs=[pl.BlockSpec((tm, d), lambda i: (i, 0)), pl.BlockSpec((tm, d), lambda i: (i, 0))],
        out_shape=[SDS((n, d), f32), SDS((n, d), f32)], compiler_params=_params(),
    )(a, w, g, h)


def _rms_bwd(x, g, dy, dh_in, out_dtype, tm, name):
    n, d = x.shape
    has_add = dh_in is not None

    def body(*refs):
        if has_add:
            x_ref, g_ref, dy_ref, dh_ref, o_ref, dg_ref = refs
        else:
            x_ref, g_ref, dy_ref, o_ref, dg_ref = refs
        xv = x_ref[...]
        dyv = dy_ref[...].astype(f32)
        r = lax.rsqrt(jnp.mean(xv * xv, axis=-1, keepdims=True) + EPS)
        xr = xv * r
        gdy = dyv * g_ref[...]
        dx = r * gdy - xr * (r * r) * jnp.mean(xv * gdy, axis=-1, keepdims=True)
        if has_add:
            dx = dx + dh_ref[...]
        o_ref[...] = dx.astype(o_ref.dtype)

        @pl.when(pl.program_id(0) == 0)
        def _():
            dg_ref[...] = jnp.zeros_like(dg_ref)

        dg_ref[...] += jnp.sum(dyv * xr, axis=0, keepdims=True)

    row = pl.BlockSpec((tm, d), lambda i: (i, 0))
    vec = pl.BlockSpec((1, d), lambda i: (0, 0))
    ins = [x, g, dy] + ([dh_in] if has_add else [])
    return pl.pallas_call(
        body, grid=(n // tm,), name=name,
        in_specs=[row, vec, row] + ([row] if has_add else []),
        out_specs=[row, vec],
        out_shape=[SDS((n, d), out_dtype), SDS((1, d), f32)], compiler_params=_params(),
    )(*ins)


def _mm_nt(pairs, out_dtype, tm, tk, name):
    n = pairs[0][0].shape[0]
    k = pairs[0][1].shape[0]
    np_ = len(pairs)

    def body(*refs):
        o_ref = refs[-1]
        acc = None
        for p in range(np_):
            t = _bdot_nt(refs[2 * p][...], refs[2 * p + 1][...])
            acc = t if acc is None else acc + t
        o_ref[...] = acc.astype(o_ref.dtype)

    in_specs, ins = [], []
    for dy, w in pairs:
        m = dy.shape[1]
        in_specs += [pl.BlockSpec((tm, m), lambda j, i: (i, 0)), pl.BlockSpec((tk, m), lambda j, i: (j, 0))]
        ins += [dy, w]
    return pl.pallas_call(
        body, grid=(k // tk, n // tm), name=name, in_specs=in_specs,
        out_specs=pl.BlockSpec((tm, tk), lambda j, i: (i, j)),
        out_shape=SDS((n, k), out_dtype), compiler_params=_params(),
    )(*ins)


def _mm_tn(x, dy, tm, tk, tn, name):
    n, k = x.shape
    m = dy.shape[1]

    def body(x_ref, dy_ref, o_ref):
        @pl.when(pl.program_id(2) == 0)
        def _():
            o_ref[...] = jnp.zeros_like(o_ref)

        o_ref[...] += _bdot_tn(x_ref[...], dy_ref[...])

    return pl.pallas_call(
        body, grid=(k // tk, m // tn, n // tm), name=name,
        in_specs=[pl.BlockSpec((tm, tk), lambda a, b, i: (i, a)), pl.BlockSpec((tm, tn), lambda a, b, i: (i, b))],
        out_specs=pl.BlockSpec((tk, tn), lambda a, b, i: (a, b)),
        out_shape=SDS((k, m), f32), compiler_params=_params(),
    )(x, dy)


def _split3(x):
    hi = x.astype(bf16)
    r = x - hi.astype(f32)
    mid = r.astype(bf16)
    return hi, mid, (r - mid.astype(f32)).astype(bf16)


def _mask_dot(mask, x):
    hi, mid, lo = _split3(x)
    dot = lambda p: jnp.dot(mask, p, preferred_element_type=f32)
    return dot(hi) + dot(mid) + dot(lo)


_BNN = (((2,), (1,)), ((0,), (0,)))
_BNT = (((2,), (2,)), ((0,), (0,)))
_BTN = (((1,), (1,)), ((0,), (0,)))


def _hdot(a, b, dims):
    return lax.dot_general(a.astype(bf16), b.astype(bf16), dims, preferred_element_type=f32)


def _heads(x, nhb):
    return jnp.stack([x[:, h * A_DK:(h + 1) * A_DK] for h in range(nhb)])


def _mask_dot_heads(mask, x):
    return jnp.stack([_mask_dot(mask, x[h]) for h in range(x.shape[0])])


def _chunk_rows(parts, cl):
    tiles = [jnp.broadcast_to(p, (p.shape[0], cl, p.shape[2])) for p in parts]
    return tiles[0] if len(tiles) == 1 else jnp.concatenate(tiles, axis=1)


def _cat(parts):
    return parts[0] if len(parts) == 1 else jnp.concatenate(parts, axis=1)


def _gla_group_fwd(qg, fg, vg, lb, st, nc, cl):
    g = nc * cl
    sg = _sigmoid(fg)
    f = lb + (1.0 - lb) * sg
    k = 1.0 - f
    row, col = _iota2((g, g), 0), _iota2((g, g), 1)
    chunk_of = lambda idx: sum((idx >= u * cl).astype(jnp.int32) for u in range(1, nc)) if nc > 1 else 0
    same = chunk_of(row) == chunk_of(col) if nc > 1 else None
    causal = row >= col if nc == 1 else jnp.logical_and(same, row >= col)
    anti = col >= row if nc == 1 else jnp.logical_and(same, col >= row)
    b = _mask_dot_heads(causal.astype(bf16), jnp.log(f))
    bls = [b[:, (u + 1) * cl - 1:(u + 1) * cl, :] for u in range(nc)]
    ebls = [jnp.exp(x) for x in bls]
    e = jnp.exp(b)
    ei = jnp.exp(-b)
    eo = jnp.exp(_chunk_rows(bls, cl) - b)
    qi, ki, ko = qg * e, k * ei, k * eo
    att = jnp.where(causal[None], _hdot(qi, ki, _BNT), 0.0)
    o_intra = _hdot(att, vg, _BNN)
    sl = [slice(u * cl, (u + 1) * cl) for u in range(nc)]
    ds = [_hdot(vg[:, s], ko[:, s], _BTN) for s in sl]
    sts = [st]
    for u in range(nc):
        sts.append(sts[u] * ebls[u] + ds[u])
    o = o_intra + _cat([_hdot(qi[:, sl[u]], sts[u], _BNT) for u in range(nc)])
    return dict(sg=sg, f=f, e=e, ei=ei, eo=eo, ebls=ebls, qi=qi, ki=ki, ko=ko, att=att, o=o, sts=sts, causal=causal,
                anti=anti, sl=sl)


def _gla_group(nreal, want):
    while nreal % want:
        want //= 2
    return max(want, 1)


def _head_out(o, ggc, hg):
    r = lax.rsqrt(jnp.mean(o * o, axis=-1, keepdims=True) + EPS)
    return o * r * hg * (ggc * _sigmoid(ggc))


def _gla_fwd(pmat, lb, hg, bl_, t, nm, name):
    n, d4 = pmat.shape
    d = d4 // 4
    nh = d // A_DK
    nreal = (t - nm) // A_CHUNK
    nch = nreal + 1
    un = _gla_group(nreal, GLA_GROUP)
    hb = _gla_group(nh, GLA_HEADS)
    ng = nh // hb
    wide = hb * A_DK

    def body(q_ref, f_ref, i_ref, gg_ref, lb_ref, hg_ref, og_ref, ss_ref):
        lbv, hgv = _heads(lb_ref[...], hb), _heads(hg_ref[...], hb)
        take = lambda ref, rows: _heads(ref[rows, :], hb)

        def run(rows, st, idx, nc, cl):
            w = _gla_group_fwd(take(q_ref, rows), take(f_ref, rows), take(i_ref, rows), lbv, st, nc, cl)
            out = _head_out(w["o"], take(gg_ref, rows), hgv)
            for h in range(hb):
                for u in range(nc):
                    ss_ref[h, idx + u] = w["sts"][u][h]
                og_ref[rows, h * A_DK:(h + 1) * A_DK] = out[h].astype(og_ref.dtype)
            return w["sts"][nc]

        st = run(pl.ds(0, nm), jnp.zeros((hb, A_DK, A_DK), f32), 0, 1, nm)

        def step(it, st):
            rows = pl.ds(pl.multiple_of(nm + it * (un * A_CHUNK), BF16_ROWS), un * A_CHUNK)
            return run(rows, st, 1 + it * un, un, A_CHUNK)

        lax.fori_loop(0, nreal // un, step, st)

    col = lambda o: pl.BlockSpec((t, wide), lambda b, h: (b, o * ng + h))
    vec = pl.BlockSpec((1, wide), lambda b, h: (0, h))
    return pl.pallas_call(
        body, grid=(bl_, ng), name=name,
        in_specs=[col(0), col(1), col(2), col(3), vec, vec],
        out_specs=[pl.BlockSpec((t, wide), lambda b, h: (b, h)),
                   pl.BlockSpec((hb, nch, A_DK, A_DK), lambda b, h: (b * ng + h, 0, 0, 0))],
        out_shape=[SDS((n, d), bf16), SDS((bl_ * nh, nch, A_DK, A_DK), f32)], compiler_params=_params(),
    )(pmat, pmat, pmat, pmat, lb, hg)


def _gla_bwd(pmat, ss, dog, lb, hg, bl_, t, nm, name):
    n, d4 = pmat.shape
    d = d4 // 4
    nh = d // A_DK
    nreal = (t - nm) // A_CHUNK
    nch = nreal + 1
    un = _gla_group(nreal, GLA_GROUP)
    hb = _gla_group(nh, GLA_HEADS)
    ng = nh // hb
    wide = hb * A_DK

    def body(q_ref, f_ref, i_ref, gg_ref, ss_ref, dog_ref, lb_ref, hg_ref,
             dq_ref, df_ref, di_ref, dgg_ref, dlb_ref, dhg_ref):
        lbv, hgv = _heads(lb_ref[...], hb), _heads(hg_ref[...], hb)
        take = lambda ref, rows: _heads(ref[rows, :], hb)

        def put(ref, rows, val):
            for h in range(hb):
                ref[rows, h * A_DK:(h + 1) * A_DK] = val[h].astype(ref.dtype)

        def run(rows, idx, carry, nc, cl):
            dst, dlb, dhg = carry
            qg, fg, vg, ggc = take(q_ref, rows), take(f_ref, rows), take(i_ref, rows), take(gg_ref, rows)
            dogc = take(dog_ref, rows).astype(f32)
            st_in = jnp.stack([ss_ref[h, idx] for h in range(hb)])
            w = _gla_group_fwd(qg, fg, vg, lbv, st_in, nc, cl)
            o, qi, ki, ko, sl, sts, ebls = w["o"], w["qi"], w["ki"], w["ko"], w["sl"], w["sts"], w["ebls"]
            r = lax.rsqrt(jnp.mean(o * o, axis=-1, keepdims=True) + EPS)
            sgg = _sigmoid(ggc)
            sil = ggc * sgg
            on = o * r
            dhg = dhg + jnp.sum(dogc * sil * on, axis=1, keepdims=True)
            put(dgg_ref, rows, dogc * on * hgv * (sgg * (1.0 + ggc * (1.0 - sgg))))
            tt = dogc * sil * hgv
            do = r * tt - on * (r * r) * jnp.mean(o * tt, axis=-1, keepdims=True)
            xs = [_hdot(do[:, s], qi[:, s], _BTN) for s in sl]
            dsts = [None] * nc + [dst]
            for u in reversed(range(nc)):
                dsts[u] = dsts[u + 1] * ebls[u] + xs[u]
            datt = jnp.where(w["causal"][None], _hdot(do, vg, _BNT), 0.0)
            dv = _hdot(w["att"], do, _BTN) + _cat([_hdot(ko[:, sl[u]], dsts[u + 1], _BNT) for u in range(nc)])
            dko = _cat([_hdot(vg[:, sl[u]], dsts[u + 1], _BNN) for u in range(nc)])
            dqi = _hdot(datt, ki, _BNN) + _cat([_hdot(do[:, sl[u]], sts[u], _BNN) for u in range(nc)])
            dki = _hdot(datt, qi, _BTN)
            dk = dki * w["ei"] + dko * w["eo"]
            dkoko = dko * ko
            db = dqi * qi - dki * ki - dkoko
            rowi = lax.broadcasted_iota(jnp.int32, db.shape, 1)
            for u in range(nc):
                d_ebl = jnp.sum(dsts[u + 1] * sts[u], axis=1, keepdims=True)
                dbl = jnp.sum(dkoko[:, sl[u]], axis=1, keepdims=True) + d_ebl * ebls[u]
                db = db + jnp.where(rowi == (u + 1) * cl - 1, dbl, 0.0)
            dlogf = _mask_dot_heads(w["anti"].astype(bf16), db)
            df = dlogf / w["f"] - dk
            sg = w["sg"]
            put(dq_ref, rows, dqi * w["e"])
            put(df_ref, rows, df * (1.0 - lbv) * sg * (1.0 - sg))
            put(di_ref, rows, dv)
            dlb = dlb + jnp.sum(df * (1.0 - sg), axis=1, keepdims=True)
            return dsts[0], dlb, dhg

        zero = jnp.zeros((hb, 1, A_DK), f32)
        ngroups = nreal // un

        def step(it, carry):
            grp = ngroups - 1 - it
            rows = pl.ds(pl.multiple_of(nm + grp * (un * A_CHUNK), BF16_ROWS), un * A_CHUNK)
            return run(rows, 1 + grp * un, carry, un, A_CHUNK)

        carry = lax.fori_loop(0, ngroups, step, (jnp.zeros((hb, A_DK, A_DK), f32), zero, zero))
        _, dlb, dhg = run(pl.ds(0, nm), 0, carry, 1, nm)

        @pl.when(pl.program_id(1) == 0)
        def _():
            dlb_ref[...] = jnp.zeros_like(dlb_ref)
            dhg_ref[...] = jnp.zeros_like(dhg_ref)

        for h in range(hb):
            dlb_ref[:, h * A_DK:(h + 1) * A_DK] += dlb[h]
            dhg_ref[:, h * A_DK:(h + 1) * A_DK] += dhg[h]

    col = lambda o: pl.BlockSpec((t, wide), lambda h, b: (b, o * ng + h))
    blk = pl.BlockSpec((t, wide), lambda h, b: (b, h))
    vec = pl.BlockSpec((1, wide), lambda h, b: (0, h))
    return pl.pallas_call(
        body, grid=(ng, bl_), name=name,
        in_specs=[col(0), col(1), col(2), col(3),
                  pl.BlockSpec((hb, nch, A_DK, A_DK), lambda h, b: (b * ng + h, 0, 0, 0)), blk, vec, vec],
        out_specs=[blk, blk, blk, blk, vec, vec],
        out_shape=[SDS((n, d), bf16)] * 4 + [SDS((1, d), f32)] * 2, compiler_params=_params(),
    )(pmat, pmat, pmat, pmat, ss, dog, lb, hg)


def _shifted(x, halo, before):
    n = x.shape[0]
    both = jnp.concatenate([halo, x] if before else [x, halo], axis=0)
    row, col = _iota2((n, n + BF16_ROWS), 0), _iota2((n, n + BF16_ROWS), 1)
    src = row + BF16_ROWS if before else row
    step = -1 if before else 1
    pick = lambda s: jnp.dot((col == src + step * s).astype(bf16), both, preferred_element_type=f32)
    return pick(1), pick(2)


def _conv3(xb, halo, w):
    x = xb.astype(f32)
    x1, x2 = _shifted(xb, halo, True)
    return x, x1, x2, w[0:1, :] * x2 + w[1:2, :] * x1 + w[2:3, :] * x


def _conv_gate_fwd(ug, uv, cwg, cwv, bl_, t, tc, name):
    n, ff = ug.shape
    nt = t // tc

    def body(ug_ref, uv_ref, wg_ref, wv_ref, a_ref, hg_ref, hv_ref):
        @pl.when(pl.program_id(1) == 0)
        def _():
            hg_ref[...] = jnp.zeros_like(hg_ref)
            hv_ref[...] = jnp.zeros_like(hv_ref)

        xg, xv = ug_ref[...], uv_ref[...]
        cg = _conv3(xg, hg_ref[...], wg_ref[...])[3]
        cv = _conv3(xv, hv_ref[...], wv_ref[...])[3]
        a_ref[...] = (cg * _sigmoid(cg) * cv).astype(a_ref.dtype)
        hg_ref[...] = xg[tc - BF16_ROWS:tc, :].astype(hg_ref.dtype)
        hv_ref[...] = xv[tc - BF16_ROWS:tc, :].astype(hv_ref.dtype)

    row = pl.BlockSpec((tc, ff), lambda b, i: (b * nt + i, 0))
    wsp = pl.BlockSpec((3, ff), lambda b, i: (0, 0))
    return pl.pallas_call(
        body, grid=(bl_, nt), name=name, in_specs=[row, row, wsp, wsp], out_specs=row,
        out_shape=SDS((n, ff), bf16),
        scratch_shapes=[pltpu.VMEM((BF16_ROWS, ff), bf16), pltpu.VMEM((BF16_ROWS, ff), bf16)], compiler_params=_params(),
    )(ug, uv, cwg, cwv)


def _conv_gate_bwd(ug, uv, cwg, cwv, da, bl_, t, tc, name):
    n, ff = ug.shape
    nt = t // tc
    per = tc // BF16_ROWS

    def body(ug_ref, uv_ref, pg_ref, pv_ref, wg_ref, wv_ref, da_ref, dug_ref, duv_ref, dwg_ref, dwv_ref, ng_ref, nv_ref):
        first = jnp.logical_and(pl.program_id(0) == 0, pl.program_id(1) == 0)

        @pl.when(first)
        def _():
            dwg_ref[...] = jnp.zeros_like(dwg_ref)
            dwv_ref[...] = jnp.zeros_like(dwv_ref)

        @pl.when(pl.program_id(1) == 0)
        def _():
            ng_ref[...] = jnp.zeros_like(ng_ref)
            nv_ref[...] = jnp.zeros_like(nv_ref)

        seq_start = pl.program_id(1) == nt - 1
        dav = da_ref[...].astype(f32)

        def half(u_ref, p_ref, w_ref):
            halo = p_ref[...]
            return _conv3(u_ref[...], jnp.where(seq_start, jnp.zeros_like(halo), halo), w_ref[...])

        xg, xg1, xg2, cg = half(ug_ref, pg_ref, wg_ref)
        xv, xv1, xv2, cv = half(uv_ref, pv_ref, wv_ref)
        sg = _sigmoid(cg)
        dcg = dav * cv * (sg * (1.0 + cg * (1.0 - sg)))
        dcv = dav * (cg * sg)

        def back(dc, x, x1, x2, w_ref, nx_ref, du_ref, dw_ref):
            w = w_ref[...]
            dcb = dc.astype(bf16)
            dc1, dc2 = _shifted(dcb, nx_ref[...], False)
            du = w[2:3, :] * dc + w[1:2, :] * dc1 + w[0:1, :] * dc2
            du_ref[...] = du.astype(du_ref.dtype)
            dw_ref[0:1, :] += jnp.sum(dc * x2, axis=0, keepdims=True)
            dw_ref[1:2, :] += jnp.sum(dc * x1, axis=0, keepdims=True)
            dw_ref[2:3, :] += jnp.sum(dc * x, axis=0, keepdims=True)
            nx_ref[...] = dcb[0:BF16_ROWS, :].astype(nx_ref.dtype)

        back(dcg, xg, xg1, xg2, wg_ref, ng_ref, dug_ref, dwg_ref)
        back(dcv, xv, xv1, xv2, wv_ref, nv_ref, duv_ref, dwv_ref)

    row = pl.BlockSpec((tc, ff), lambda b, i: (b * nt + nt - 1 - i, 0))
    prev = pl.BlockSpec((BF16_ROWS, ff), lambda b, i: (jnp.maximum((b * nt + nt - 1 - i) * per - 1, 0), 0))
    wsp = pl.BlockSpec((3, ff), lambda b, i: (0, 0))
    return pl.pallas_call(
        body, grid=(bl_, nt), name=name, in_specs=[row, row, prev, prev, wsp, wsp, row],
        out_specs=[row, row, wsp, wsp],
        out_shape=[SDS((n, ff), bf16), SDS((n, ff), bf16), SDS((3, ff), f32), SDS((3, ff), f32)],
        scratch_shapes=[pltpu.VMEM((BF16_ROWS, ff), bf16), pltpu.VMEM((BF16_ROWS, ff), bf16)], compiler_params=_params(),
    )(ug, uv, ug, uv, cwg, cwv, da)


def _zf_c(hk, wzf, fgb, bl_, t, tm, name):
    n, d = hk.shape
    nt = t // tm

    def body(hk_ref, w_ref, b_ref, zf_ref, c_ref, carry_ref):
        @pl.when(pl.program_id(1) == 0)
        def _():
            carry_ref[...] = jnp.zeros_like(carry_ref)

        z = _bdot(hk_ref[...], w_ref[...]) + b_ref[...]
        ls = jnp.minimum(z, 0.0) - jnp.log(1.0 + jnp.exp(-jnp.abs(z)))
        c = _cumsum_rows(ls) + carry_ref[...]
        zf_ref[...] = z
        c_ref[...] = c
        carry_ref[...] = c[tm - 1:tm, :]

    row = lambda w: pl.BlockSpec((tm, w), lambda b, i: (b * nt + i, 0))
    return pl.pallas_call(
        body, grid=(bl_, nt), name=name,
        in_specs=[row(d), pl.BlockSpec((d, LANES), lambda b, i: (0, 0)), pl.BlockSpec((1, LANES), lambda b, i: (0, 0))],
        out_specs=[row(LANES), row(LANES)],
        out_shape=[SDS((n, LANES), f32), SDS((n, LANES), f32)],
        scratch_shapes=[pltpu.VMEM((1, LANES), f32)], compiler_params=_params(),
    )(hk, wzf, fgb)


def _c_bwd(dc, zf, bl_, t, tm, name):
    n = dc.shape[0]
    nt = t // tm

    def body(dc_ref, zf_ref, dzf_ref, dfg_ref, carry_ref):
        @pl.when(jnp.logical_and(pl.program_id(0) == 0, pl.program_id(1) == 0))
        def _():
            dfg_ref[...] = jnp.zeros_like(dfg_ref)

        @pl.when(pl.program_id(1) == 0)
        def _():
            carry_ref[...] = jnp.zeros_like(carry_ref)

        rc = _revcumsum_rows(dc_ref[...]) + carry_ref[...]
        dz = rc * _sigmoid(-zf_ref[...])
        dzf_ref[...] = dz.astype(dzf_ref.dtype)
        dfg_ref[...] += jnp.sum(dz, axis=0, keepdims=True)
        carry_ref[...] = rc[0:1, :]

    row = pl.BlockSpec((tm, LANES), lambda b, i: (b * nt + nt - 1 - i, 0))
    vec = pl.BlockSpec((1, LANES), lambda b, i: (0, 0))
    return pl.pallas_call(
        body, grid=(bl_, nt), name=name, in_specs=[row, row], out_specs=[row, vec],
        out_shape=[SDS((n, LANES), bf16), SDS((1, LANES), f32)],
        scratch_shapes=[pltpu.VMEM((1, LANES), f32)], compiler_params=_params(),
    )(dc, zf)


def _is_pow2(x):
    m, _ = math.frexp(x)
    return m == 0.5


def _prescale(qh, scale):
    return (qh.astype(f32) * scale).astype(bf16)


def _attn_fwd(q, kv, ck, bl_, t, tq, hd, name):
    n, d = q.shape
    npair = d // LANES
    hp = LANES // hd
    nq = t // tq
    scale = 1.0 / (hd ** 0.5)

    pre = _is_pow2(scale)

    def body(q_ref, k_ref, v_ref, ck_ref, o_ref, lse_ref):
        i = pl.program_id(2)
        diag = _iota2((tq, tq), 0) >= _iota2((tq, tq), 1)
        for hh in range(hp):
            lanes = slice(hh * hd, (hh + 1) * hd)
            qh = _prescale(q_ref[:, lanes], scale) if pre else q_ref[:, lanes]

            def block(j, carry, masked, lanes=lanes, qh=qh, hh=hh):
                m, l, acc = carry
                rows = pl.ds(pl.multiple_of(j * tq, BF16_ROWS), tq)
                s = _bdot_nt(qh, k_ref[rows, lanes])
                s = (s if pre else s * scale) - ck_ref[0, 0, j, hh:hh + 1, :]
                if masked:
                    s = jnp.where(diag, s, -1e30)
                m2 = jnp.maximum(m, jnp.max(s, axis=-1, keepdims=True))
                p = jnp.exp(s - m2)
                a = jnp.exp(m - m2)
                return m2, a * l + jnp.sum(p, axis=-1, keepdims=True), a * acc + _bdot(p, v_ref[rows, lanes])

            init = (jnp.full((tq, 1), -1e30, f32), jnp.zeros((tq, 1), f32), jnp.zeros((tq, hd), f32))
            carry = lax.fori_loop(0, i, functools.partial(block, masked=False), init)
            m, l, acc = block(i, carry, True)
            o_ref[:, lanes] = (acc / l).astype(o_ref.dtype)
            lse_ref[:, lanes] = jnp.broadcast_to(m + jnp.log(l), (tq, hd))

    nk = nq
    return pl.pallas_call(
        body, grid=(bl_, npair, nq), name=name,
        in_specs=[pl.BlockSpec((tq, LANES), lambda b, p, i: (b * nq + i, p)),
                  pl.BlockSpec((t, LANES), lambda b, p, i: (b, p)),
                  pl.BlockSpec((t, LANES), lambda b, p, i: (b, npair + p)),
                  pl.BlockSpec((1, 1, nk, hp, tq), lambda b, p, i: (b, p, 0, 0, 0))],
        out_specs=[pl.BlockSpec((tq, LANES), lambda b, p, i: (b * nq + i, p)),
                   pl.BlockSpec((tq, LANES), lambda b, p, i: (b * nq + i, p))],
        out_shape=[SDS((n, d), f32), SDS((n, d), f32)], compiler_params=_params(),
    )(q, kv, kv, ck)


def _attn_bwd(q, kv, o, do, lse, ck, bl_, t, tq, hd, name):
    n, d = q.shape
    npair = d // LANES
    hp = LANES // hd
    nq = t // tq
    scale = 1.0 / (hd ** 0.5)

    pre = _is_pow2(scale)

    def body(q_ref, k_ref, v_ref, o_ref, do_ref, lse_ref, ck_ref, dq_ref, dk_ref, dv_ref, dck_ref, dcq_ref):
        j = pl.program_id(2)

        @pl.when(j == 0)
        def _():
            dq_ref[...] = jnp.zeros_like(dq_ref)
            dcq_ref[...] = jnp.zeros_like(dcq_ref)

        diag = _iota2((tq, tq), 0) >= _iota2((tq, tq), 1)
        for hh in range(hp):
            lanes = slice(hh * hd, (hh + 1) * hd)
            kh = k_ref[:, lanes]
            vh = v_ref[:, lanes]
            kt = kh.astype(f32).T.astype(bf16)
            cs = ck_ref[0, 0, 0, hh:hh + 1, :]

            def block(i, carry, masked, lanes=lanes, kh=kh, vh=vh, kt=kt, cs=cs, hh=hh):
                dkt, dvt, dcs = carry
                rows = pl.ds(pl.multiple_of(i * tq, BF16_ROWS), tq)
                qh = _prescale(q_ref[rows, lanes], scale) if pre else q_ref[rows, lanes]
                doh = do_ref[rows, lanes]
                s = _bdot_nt(qh, kh)
                s = (s if pre else s * scale) - cs
                if masked:
                    s = jnp.where(diag, s, -1e30)
                p = jnp.exp(s - lse_ref[rows, hh * hd:hh * hd + 1])
                delta = jnp.sum(doh.astype(f32) * o_ref[rows, lanes].astype(f32), axis=-1, keepdims=True)
                ds = p * (_bdot_nt(doh, vh) - delta)
                dsb = ds.astype(bf16)
                dq_ref[rows, lanes] += _bdot_nt(kt, dsb).T * scale
                dcq_ref[0, rows, hh:hh + 1] += jnp.sum(ds, axis=-1, keepdims=True)
                dkq = _bdot_tn(qh, dsb)
                return (dkt + (dkq if pre else dkq * scale), dvt + _bdot_tn(doh, p), dcs - jnp.sum(ds, axis=0, keepdims=True))

            init = (jnp.zeros((hd, tq), f32), jnp.zeros((hd, tq), f32), jnp.zeros((1, tq), f32))
            dkt, dvt, dcs = lax.fori_loop(j + 1, nq, functools.partial(block, masked=False), block(j, init, True))
            dk_ref[:, lanes] = dkt.T.astype(dk_ref.dtype)
            dv_ref[:, lanes] = dvt.T.astype(dv_ref.dtype)
            dck_ref[0, 0, 0, hh:hh + 1, :] = dcs

    whole = lambda c0: pl.BlockSpec((t, LANES), lambda b, p, j: (b, c0 + p))
    tile = lambda c0: pl.BlockSpec((tq, LANES), lambda b, p, j: (b * nq + j, c0 + p))
    ckspec = pl.BlockSpec((1, 1, 1, hp, tq), lambda b, p, j: (b, p, j, 0, 0))
    cqspec = pl.BlockSpec((1, t, hp), lambda b, p, j: (p, b, 0))
    return pl.pallas_call(
        body, grid=(bl_, npair, nq), name=name,
        in_specs=[whole(0), tile(0), tile(npair), whole(0), whole(0), whole(0), ckspec],
        out_specs=[whole(0), tile(0), tile(0), ckspec, cqspec],
        out_shape=[SDS((n, d), f32), SDS((n, d), bf16), SDS((n, d), bf16), SDS((bl_, npair, nq, hp, tq), f32),
                   SDS((npair, n, hp), f32)],
        compiler_params=_params(),
    )(q, kv, kv, o, do, lse, ck)


def _loss_head(h, target, t, nm, tm, name):
    n, d = h.shape
    nt = t // tm

    def body(h_ref, t_ref, loss_ref, dh_ref):
        i = pl.program_id(0)

        @pl.when(i == 0)
        def _():
            loss_ref[...] = jnp.zeros_like(loss_ref)

        pos = (i % nt) * tm + _iota2((tm, d), 0)
        err = jnp.where(pos >= nm, h_ref[...] - t_ref[...], 0.0)
        dh_ref[...] = err * (1.0 / d)
        loss_ref[...] += 0.5 * jnp.sum(jnp.mean(err * err, axis=-1, keepdims=True))

    row = pl.BlockSpec((tm, d), lambda i: (i, 0))
    return pl.pallas_call(
        body, grid=(n // tm,), name=name, in_specs=[row, row],
        out_specs=[pl.BlockSpec((8, LANES), lambda i: (0, 0)), row],
        out_shape=[SDS((8, LANES), f32), SDS((n, d), f32)], compiler_params=_params(),
    )(h, target)


def _c_key_rows(c, bl_, t, tq, bh, hp):
    npair = bh // hp
    nk = t // tq
    return c[:, :bh].reshape(bl_, nk, tq, npair, hp).transpose(0, 3, 1, 4, 2)


def _dc_rows(dck, dcq, bl_, t, bh):
    d = dck.transpose(0, 2, 4, 1, 3).reshape(bl_ * t, bh) + dcq.transpose(1, 0, 2).reshape(bl_ * t, bh)
    return jnp.pad(d, ((0, 0), (0, LANES - bh)))


_ANY = pl.BlockSpec(memory_space=pl.ANY)


def _all_gather(xs, name):
    na = len(xs)

    def body(*refs):
        x_refs, out_refs = refs[:na], refs[na:2 * na]
        send_sems, recv_sems, local_sems = refs[2 * na:]
        mx, my, mc = lax.axis_index("x"), lax.axis_index("y"), lax.axis_index("c")
        me, sibling = (mx, my, mc), (mx, my, 1 - mc)
        chips = [(1 - mx, my), (mx, 1 - my), (1 - mx, 1 - my)]

        def copy(a, k, block, to, own=False):
            px, py, pc = block
            rows = out_refs[a].at[4 * px + 2 * py + pc]
            return pltpu.make_async_remote_copy(
                src_ref=x_refs[a] if own else rows, dst_ref=rows,
                send_sem=send_sems.at[a, k], recv_sem=recv_sems.at[a, k], device_id=to, device_id_type=MESH)

        arrays = range(na)
        mine = [pltpu.make_async_copy(x_refs[a], out_refs[a].at[4 * mx + 2 * my + mc], local_sems.at[a]) for a in arrays]
        for cp in mine:
            cp.start()
        first = [copy(a, 1 + j, me, (*chip, mc), own=True) for j, chip in enumerate(chips) for a in arrays]
        first += [copy(a, 0, me, sibling, own=True) for a in arrays]
        for cp in first:
            cp.start()
        passed = []
        for j, chip in enumerate(chips):
            for a in arrays:
                copy(a, 1 + j, (*chip, mc), me).wait_recv()
                cp = copy(a, 4 + j, (*chip, mc), sibling)
                cp.start()
                passed.append(cp)
        for a in arrays:
            copy(a, 0, sibling, me).wait_recv()
        for j, chip in enumerate(chips):
            for a in arrays:
                copy(a, 4 + j, (*chip, 1 - mc), me).wait_recv()
        for cp in first + passed:
            cp.wait_send()
        for cp in mine:
            cp.wait()

    return pl.pallas_call(
        body, name=name, out_shape=[SDS((N_DEV,) + x.shape, x.dtype) for x in xs],
        in_specs=[_ANY] * na, out_specs=[_ANY] * na,
        scratch_shapes=[pltpu.SemaphoreType.DMA((na, 7)), pltpu.SemaphoreType.DMA((na, 7)), pltpu.SemaphoreType.DMA((na,))],
    )(*xs)


_HBM = pl.BlockSpec(memory_space=pltpu.HBM)
_SEM = pl.BlockSpec(memory_space=pltpu.SEMAPHORE)
_DATAFLOW = pltpu.SideEffectType.DATAFLOW_SIDE_EFFECTING
N_PEERS = N_DEV - 1


def _device_index():
    return 4 * lax.axis_index("x") + 2 * lax.axis_index("y") + lax.axis_index("c")


def _peers():
    mx, my, mc = lax.axis_index("x"), lax.axis_index("y"), lax.axis_index("c")
    peers = []
    for r in (2, 3, 4, 5, 6, 7, 1):
        px = 1 - mx if r & 4 else mx
        py = 1 - my if r & 2 else my
        pc = 1 - mc if r & 1 else mc
        peers.append(((px, py, pc), 4 * px + 2 * py + pc))
    return 4 * mx + 2 * my + mc, peers


def _push_copy(src_ref, land_ref, send_sems, recv_sems, a, k, dev, src_row, land_row, scatter):
    return pltpu.make_async_remote_copy(
        src_ref=src_ref.at[src_row] if scatter else src_ref, dst_ref=land_ref.at[land_row],
        send_sem=send_sems.at[a * N_PEERS + k], recv_sem=recv_sems.at[a * N_PEERS + k], device_id=dev, device_id_type=MESH)


def _landing(own, me):
    return lax.dynamic_update_index_in_dim(lax.empty((N_DEV,) + own.shape, own.dtype), own, me, 0)


def _push_start(srcs, lands, scatter, name):
    na = len(srcs)

    def body(*refs):
        src_refs, land_refs = refs[:na], refs[na:2 * na]
        send_sems, recv_sems = refs[2 * na], refs[2 * na + 1]
        token = refs[-1]
        me, peers = _peers()
        for a in range(na):
            for k, (dev, idx) in enumerate(peers):
                _push_copy(src_refs[a], land_refs[a], send_sems, recv_sems, a, k, dev, idx, me, scatter).start()
        token[...] = jnp.zeros_like(token)

    hbm = lambda arrs: [pltpu.HBM(a.shape, a.dtype) for a in arrs]
    out = pl.pallas_call(
        body, name=name,
        out_shape=(pltpu.SemaphoreType.DMA((na * N_PEERS,)), pltpu.SemaphoreType.DMA((na * N_PEERS,)), *hbm(srcs), *hbm(lands),
                   SDS((8, LANES), f32)),
        in_specs=[_HBM] * (2 * na),
        out_specs=(_SEM, _SEM, *([_HBM] * (2 * na)), pl.BlockSpec(memory_space=pltpu.VMEM)),
        input_output_aliases={i: 2 + i for i in range(2 * na)},
        compiler_params=pltpu.CompilerParams(has_side_effects=_DATAFLOW),
    )(*[pltpu.with_memory_space_constraint(a, pltpu.HBM) for a in list(srcs) + list(lands)])
    return out[0], out[1], list(out[2:2 + na]), list(out[2 + na:2 + 2 * na]), out[-1]


def _push_wait(send_sems, recv_sems, srcs, lands, which, after, scatter, name):
    nw = len(which)

    def body(*refs):
        src_refs, land_refs = refs[:nw], refs[nw:2 * nw]
        send_sems_, recv_sems_ = refs[2 * nw], refs[2 * nw + 1]
        _, peers = _peers()
        for j, a in enumerate(which):
            for k, (dev, idx) in enumerate(peers):
                cp = _push_copy(src_refs[j], land_refs[j], send_sems_, recv_sems_, a, k, dev, idx, idx, scatter)
                cp.wait_send()
                cp.wait_recv()

    hbm = lambda arrs: [pltpu.HBM(a.shape, a.dtype) for a in arrs]
    out = pl.pallas_call(
        body, name=name, out_shape=(*hbm(srcs), *hbm(lands)),
        in_specs=[_HBM] * (2 * nw) + [_SEM, _SEM, _ANY], out_specs=[_HBM] * (2 * nw),
        input_output_aliases={i: i for i in range(2 * nw)},
        compiler_params=pltpu.CompilerParams(has_side_effects=_DATAFLOW),
    )(*srcs, *lands, send_sems, recv_sems, after)
    return list(out[nw:])


def _adamw(parts, w, m, v, tr, name, layer=None, earlier=None):
    g, r, c = parts.shape

    def body(p_ref, w_ref, m_ref, v_ref, *rest):
        g_ref, d_ref, m2_ref, v2_ref = rest[-4:]
        gr = p_ref[0].astype(f32)
        for k in range(1, g):
            gr = gr + p_ref[k].astype(f32)
        m2 = ADAM_B1 * m_ref[...] + (1.0 - ADAM_B1) * gr
        v2 = ADAM_B2 * v_ref[...] + (1.0 - ADAM_B2) * (gr * gr)
        m_hat = m2 / (1.0 - ADAM_B1 ** ADAM_STEP)
        v_hat = v2 / (1.0 - ADAM_B2 ** ADAM_STEP)
        g_ref[...] = gr
        d_ref[...] = -ADAM_LR * (m_hat / (jnp.sqrt(v_hat) + ADAM_EPS) + ADAM_WD * w_ref[...])
        m2_ref[...] = m2
        v2_ref[...] = v2

    if layer is None:
        row, shape = pl.BlockSpec((tr, c), lambda i: (i, 0)), (r, c)
    else:
        row, shape = pl.BlockSpec((None, tr, c), lambda i: (layer, i, 0)), w.shape
    done = list(earlier or ())
    return pl.pallas_call(
        body, grid=(r // tr,), name=name,
        in_specs=[pl.BlockSpec((g, tr, c), lambda i: (0, i, 0)), row, row, row] + [_ANY] * len(done),
        out_specs=[row] * 4, out_shape=[SDS(shape, f32)] * 4,
        input_output_aliases={4 + k: k for k in range(len(done))}, compiler_params=_params(),
    )(parts, w, m, v, *done)


_SHARD_AXIS = dict(meta_tokens=1, norm_gains=2, a_w_in=2, a_lb_logits=1, a_head_norm=1, a_w_out=1, kv_w=1,
                   b_w_q=1, b_w_out=1, ffn_w_up=2, ffn_conv=2, ffn_w_down=1)
_VECTORS = ("meta_tokens", "norm_gains", "a_lb_logits", "a_head_norm", "ffn_conv")
_REPLICATED = ("kv_norm", "fg_b")
_ROW_TILE_CAP = 512


def _pack(arrs, dtype, cols, row_mult):
    lead = arrs[0].shape[:-1] if arrs[0].ndim > 1 else ()
    flat = jnp.concatenate([a.astype(dtype) for a in arrs], axis=-1)
    size = flat.shape[-1]
    per = cols * row_mult
    total = -(-size // per) * per
    flat = jnp.pad(flat, [(0, 0)] * len(lead) + [(0, total - size)])
    return flat.reshape(lead + (total // cols, cols))


def _unpack(flat, shapes):
    out, off = [], 0
    lead = flat.shape[:-1]
    for shp in shapes:
        size = 1
        for s in shp:
            size *= s
        out.append(flat[..., off:off + size].reshape(lead + tuple(shp)))
        off += size
    return out


def _unshard(seg, axis):
    a = jnp.moveaxis(seg, 0, axis)
    shp = a.shape
    return a.reshape(shp[:axis] + (shp[axis] * shp[axis + 1],) + shp[axis + 2:])


def _shard8(full, axis):
    shp = full.shape
    a = full.reshape(shp[:axis] + (N_DEV, shp[axis] // N_DEV) + shp[axis + 1:])
    return jnp.moveaxis(a, axis, 0)


def _rows(a, lead=0):
    return a.reshape(a.shape[:lead] + (-1, a.shape[-1]))


def kernel(x, meta_tokens, norm_gains, a_w_in, a_lb_logits, a_head_norm, a_w_out, kv_norm, kv_w, fg_b, b_w_q, b_w_out, ffn_w_up, ffn_conv, ffn_w_down, loss_target, m_meta_tokens, m_norm_gains, m_a_w_in, m_a_lb_logits, m_a_head_norm, m_a_w_out, m_kv_norm, m_kv_w, m_fg_b, m_b_w_q, m_b_w_out, m_ffn_w_up, m_ffn_conv, m_ffn_w_down, v_meta_tokens, v_norm_gains, v_a_w_in, v_a_lb_logits, v_a_head_norm, v_a_w_out, v_kv_norm, v_kv_w, v_fg_b, v_b_w_q, v_b_w_out, v_ffn_w_up, v_ffn_conv, v_ffn_w_down):
    names = ("meta_tokens", "norm_gains", "a_w_in", "a_lb_logits", "a_head_norm", "a_w_out", "kv_norm", "kv_w", "fg_b",
             "b_w_q", "b_w_out", "ffn_w_up", "ffn_conv", "ffn_w_down")
    w = dict(zip(names, (meta_tokens, norm_gains, a_w_in, a_lb_logits, a_head_norm, a_w_out, kv_norm, kv_w, fg_b,
                         b_w_q, b_w_out, ffn_w_up, ffn_conv, ffn_w_down)))
    mom = dict(zip(names, (m_meta_tokens, m_norm_gains, m_a_w_in, m_a_lb_logits, m_a_head_norm, m_a_w_out, m_kv_norm,
                           m_kv_w, m_fg_b, m_b_w_q, m_b_w_out, m_ffn_w_up, m_ffn_conv, m_ffn_w_down)))
    var = dict(zip(names, (v_meta_tokens, v_norm_gains, v_a_w_in, v_a_lb_logits, v_a_head_norm, v_a_w_out, v_kv_norm,
                           v_kv_w, v_fg_b, v_b_w_q, v_b_w_out, v_ffn_w_up, v_ffn_conv, v_ffn_w_down)))

    bl_, seq, d = x.shape
    nm = meta_tokens.shape[0]
    t = nm + seq
    n = bl_ * t
    bh = fg_b.shape[0]
    hd = d // bh
    hp = LANES // hd
    ff = ffn_w_down.shape[1] * N_DEV
    tm = _div_tile(t, TOKEN_TILE_CAP)
    tc = _div_tile(t, 64)
    tn = min(d, MODEL_TILE_CAP)

    vec_pack = _pack([w[k].reshape(-1) for k in _VECTORS], f32, LANES, 8)
    first = _all_gather([w["a_w_in"].astype(bf16), vec_pack], "gather_first")
    vec_segs = _unpack(first[1].reshape(N_DEV, -1), [w[k].shape for k in _VECTORS])
    small = {k: _unshard(a, _SHARD_AXIS[k]) for k, a in zip(_VECTORS, vec_segs)}
    w_in = _unshard(first[0], _SHARD_AXIS["a_w_in"])[0]
    me = _device_index()
    later_names = ("a_w_out", "ffn_w_up", "ffn_w_down", "kv_w", "b_w_q", "b_w_out", "ffn_w_up", "ffn_w_down")
    later_layer = (None, 0, 0, None, None, None, 1, 1)
    later = [(w[k] if l is None else w[k][l]).astype(bf16) for k, l in zip(later_names, later_layer)]
    later, _ = lax.optimization_barrier((later, first[1]))
    g_send, g_recv, later_src, later_land, _ = _push_start(later, [_landing(a, me) for a in later], False, "gather_rest_start")

    def gathered(which, after, name):
        lands = _push_wait(g_send, g_recv, [later_src[i] for i in which], [later_land[i] for i in which], which, after,
                           False, name)
        return [_unshard(a, _SHARD_AXIS[later_names[i]] - (later_layer[i] is not None)) for i, a in zip(which, lands)]

    gains_box = [small["norm_gains"]]
    gain = lambda l, j: gains_box[0][l, j][None]
    cw_gate, cw_val = small["ffn_conv"][:, :, :ff], small["ffn_conv"][:, :, ff:]
    head_gain = small["a_head_norm"]
    lb = jax.nn.softmax(small["a_lb_logits"], axis=0)[0:1]
    kvn = kv_norm[None]
    fgb_pad = jnp.pad(fg_b, (0, LANES - bh))[None]

    h0 = jnp.concatenate([jnp.broadcast_to(small["meta_tokens"][None], (bl_, nm, d)), x], axis=1).reshape(n, d)

    def ffn_fwd(l, h_in):
        fi = _rms_fwd(h_in, gain(l, 2), tm, f"ffn{l}_norm")
        ug = _mm(fi, w_gate[l], bf16, tm, ff, f"ffn{l}_up_gate")
        uv = _mm(fi, w_val[l], bf16, tm, ff, f"ffn{l}_up_val")
        act = _conv_gate_fwd(ug, uv, cw_gate[l], cw_val[l], bl_, t, tc, f"ffn{l}_conv_gate")
        h_out, mix = _mm_norm_res(act, w_down[l], gain(l, 3), h_in, tm, f"ffn{l}_down")
        return h_out, (h_in, fi, ug, uv, act, mix)

    hn0 = _rms_fwd(h0, gain(0, 0), tm, "a_norm")
    pmat = _mm(hn0, w_in, f32, tm, tn, "a_in_proj")
    og, states = _gla_fwd(pmat, lb, head_gain, bl_, t, nm, "a_gla_fwd")
    w_out_a = gathered((0,), og, "gather_wait_a")[0][0]
    h1, mix_a = _mm_norm_res(og, w_out_a, gain(0, 1), h0, tm, "a_out_proj")
    w_gate, w_val, w_down = {}, {}, {}

    def ffn_weights(l, which, after):
        w_up, w_down[l] = gathered(which, after, f"gather_wait_ffn{l}")
        w_gate[l], w_val[l] = w_up[:, :ff], w_up[:, ff:]

    ffn_weights(0, (1, 2), h1)
    h2, ffn0 = ffn_fwd(0, h1)

    w_kv_zf, w_q, w_out_b = gathered((3, 4, 5), h2, "gather_wait_b")
    w_kv, w_zf = w_kv_zf[:, :2 * d], jnp.pad(w_kv_zf[:, 2 * d:], ((0, 0), (0, LANES - bh)))
    w_q, w_out_b = w_q[0], w_out_b[0]
    hk = _rms_fwd(h2, kvn, tm, "kv_norm")
    kvp = _mm(hk, w_kv, bf16, tm, tn, "kv_proj")
    zf, cum = _zf_c(hk, w_zf, fgb_pad, bl_, t, tm, "forget_cumsum")
    ck = _c_key_rows(cum, bl_, t, tm, bh, hp)
    hn1 = _rms_fwd(h2, gain(1, 0), tm, "b_norm")
    q = _mm(hn1, w_q, bf16, tm, tn, "b_q_proj")
    o, lse = _attn_fwd(q, kvp, ck, bl_, t, tm, hd, "b_attn_fwd")
    h3, mix_b = _mm_norm_res(o, w_out_b, gain(1, 1), h2, tm, "b_out_proj")
    ffn_weights(1, (6, 7), h3)
    h4, ffn1 = ffn_fwd(1, h3)

    target = jnp.concatenate([jnp.zeros((bl_, nm, d), f32), loss_target], axis=1).reshape(n, d)
    loss8, dh = _loss_head(h4, target, t, nm, tm, "loss_head")
    loss = lax.psum(loss8[0, 0], ("x", "y", "c"))

    dgain = {}

    def ffn_bwd(l, saved, dh_out):
        h_in, fi, ug, uv, act, mix = saved
        dmix, dgain[l, 3] = _rms_bwd(mix, gain(l, 3), dh_out, None, bf16, tm, f"ffn{l}_down_norm_bwd")
        dact = _mm_nt([(dmix, w_down[l])], bf16, tm, ff, f"ffn{l}_down_dx")
        dw_down = _mm_tn(act, dmix, tm, ff, tn, f"ffn{l}_down_dw")
        dug, duv, dcg, dcv = _conv_gate_bwd(ug, uv, cw_gate[l], cw_val[l], dact, bl_, t, tc, f"ffn{l}_conv_gate_bwd")
        dfi = _mm_nt([(dug, w_gate[l]), (duv, w_val[l])], bf16, tm, tn // 2, f"ffn{l}_up_dx")
        dw_up = jnp.concatenate([_mm_tn(fi, dug, tm, tn, ff, f"ffn{l}_up_gate_dw"),
                                 _mm_tn(fi, duv, tm, tn, ff, f"ffn{l}_up_val_dw")], axis=1)
        dh_in, dgain[l, 2] = _rms_bwd(h_in, gain(l, 2), dfi, dh_out, f32, tm, f"ffn{l}_norm_bwd")
        return dh_in, dw_up, jnp.concatenate([dcg, dcv], axis=1), dw_down

    def shards(full, axis):
        return _rows(_shard8(full, axis), 1).astype(bf16)

    def push_grads(bufs, name):
        lands = [_landing(lax.dynamic_index_in_dim(b, me, 0, keepdims=False), me) for b in bufs]
        s_sem, r_sem, srcs, lands, token = _push_start(bufs, lands, True, name)
        gains_box[0] = gains_box[0] + token[0, 0]
        return s_sem, r_sem, srcs, lands

    def landed(handle, after, name):
        s_sem, r_sem, srcs, lands = handle
        return _push_wait(s_sem, r_sem, srcs, lands, tuple(range(len(srcs))), after, True, name)

    dh, dw_up1, dconv1, dw_down1 = ffn_bwd(1, ffn1, dh)
    push1 = push_grads([shards(dw_up1, 1), shards(dw_down1, 0)], "grad_push_ffn1")

    dmix, dgain[1, 1] = _rms_bwd(mix_b, gain(1, 1), dh, None, bf16, tm, "b_out_norm_bwd")
    do = _mm_nt([(dmix, w_out_b)], bf16, tm, tn, "b_out_dx")
    dw_out_b = _mm_tn(o, dmix, tm, tn, tn, "b_out_dw")
    dq, dk, dv, dck, dcq = _attn_bwd(q, kvp, o, do, lse, ck, bl_, t, tm, hd, "b_attn_bwd")
    dhn1 = _mm_nt([(dq, w_q)], bf16, tm, tn, "b_q_dx")
    dw_q = _mm_tn(hn1, dq, tm, tn, tn, "b_q_dw")
    dh, dgain[1, 0] = _rms_bwd(h2, gain(1, 0), dhn1, dh, f32, tm, "b_norm_bwd")

    dzf, dfgb = _c_bwd(_dc_rows(dck, dcq, bl_, t, bh), zf, bl_, t, tm, "forget_cumsum_bwd")
    dhk = _mm_nt([(dk, w_kv[:, :d]), (dv, w_kv[:, d:]), (dzf, w_zf)], bf16, tm, tn, "kv_dx")
    dw_kv = jnp.concatenate([_mm_tn(hk, dk, tm, tn, tn, "k_dw"), _mm_tn(hk, dv, tm, tn, tn, "v_dw"),
                             _mm_tn(hk, dzf, tm, tn, LANES, "zf_dw")[:, :bh]], axis=1)
    dh, dkvn = _rms_bwd(h2, kvn, dhk, dh, f32, tm, "kv_norm_bwd")
    push2 = push_grads([shards(dw_out_b, 0), shards(dw_q, 0), shards(dw_kv, 1)], "grad_push_b")

    dh, dw_up0, dconv0, dw_down0 = ffn_bwd(0, ffn0, dh)
    push3 = push_grads([shards(dw_up0, 1), shards(dw_down0, 0)], "grad_push_ffn0")

    dmix, dgain[0, 1] = _rms_bwd(mix_a, gain(0, 1), dh, None, bf16, tm, "a_out_norm_bwd")
    dog = _mm_nt([(dmix, w_out_a)], bf16, tm, tn, "a_out_dx")
    dw_out_a = _mm_tn(og, dmix, tm, tn, tn, "a_out_dw")
    dpq, dpf, dpi, dpg, dlb, dhg = _gla_bwd(pmat, states, dog, lb, head_gain, bl_, t, nm, "a_gla_bwd")
    dps = (dpq, dpf, dpi, dpg)
    dw_in = jnp.concatenate([_mm_tn(hn0, dp, tm, tn, tn, f"a_in_dw{j}") for j, dp in enumerate(dps)], axis=1)
    push4 = push_grads([shards(dw_out_a, 0), shards(dw_in, 1)], "grad_push_a")
    dhn0 = _mm_nt([(dp, w_in[:, j * d:(j + 1) * d]) for j, dp in enumerate(dps)], bf16, tm, tn, "a_in_dx")
    dh, dgain[0, 0] = _rms_bwd(h0, gain(0, 0), dhn0, dh, f32, tm, "a_norm_bwd")

    dh = dh.reshape(bl_, t, d)
    grad_x = dh[:, nm:]
    dl0 = dlb * lb * (1.0 - lb)
    vec_grads = dict(
        meta_tokens=jnp.sum(dh[:, :nm], axis=0),
        norm_gains=jnp.stack([jnp.concatenate([dgain[l, j] for j in range(4)], axis=0) for l in range(2)]),
        a_lb_logits=jnp.concatenate([dl0, -dl0], axis=0), a_head_norm=dhg, ffn_conv=jnp.stack([dconv0, dconv1]))
    vec_send = _pack([_shard8(vec_grads[k], _SHARD_AXIS[k]).reshape(N_DEV, -1) for k in _VECTORS], bf16, LANES, BF16_ROWS)
    push5 = push_grads([vec_send], "grad_push_vectors")

    g_s, d_s, m_s, v_s = {}, {}, {}, {}
    outs = (g_s, d_s, m_s, v_s)

    def update(part, srcs, label, layer=None, earlier=None):
        rows = part.shape[1]
        tile = rows if rows <= _ROW_TILE_CAP else _div_tile(rows, _ROW_TILE_CAP)
        return _adamw(part, *srcs, tile, label, layer, earlier)

    def update_matrix(k, part, layer=None, earlier=None):
        label = f"adamw_{k}" if layer is None else f"adamw_{k}{layer}"
        res = update(part, [_rows(src[k], 0 if layer is None else 1) for src in (w, mom, var)], label, layer, earlier)
        return [r.reshape(w[k].shape) for r in res]

    def put(k, res):
        for dst, r in zip(outs, res):
            dst[k] = r

    up1, down1 = (update_matrix(k, p, 1) for k, p in zip(("ffn_w_up", "ffn_w_down"), landed(push1, gains_box[0], "grad_wait_ffn1")))
    for k, p in zip(("b_w_out", "b_w_q", "kv_w"), landed(push2, up1[0], "grad_wait_b")):
        put(k, update_matrix(k, p))
    part_up0, part_down0 = landed(push3, g_s["kv_w"], "grad_wait_ffn0")
    up0 = update_matrix("ffn_w_up", part_up0, 0, up1)
    down0 = update_matrix("ffn_w_down", part_down0, 0, down1)
    put("ffn_w_up", up0)
    put("ffn_w_down", down0)
    part_out_a, part_in = landed(push4, down0[0], "grad_wait_a")
    put("a_w_out", update_matrix("a_w_out", part_out_a))
    put("a_w_in", update_matrix("a_w_in", part_in))
    part_vec, = landed(push5, g_s["a_w_in"], "grad_wait_vectors")
    vec_packs = [_pack([src[k].reshape(-1) for k in _VECTORS], f32, LANES, BF16_ROWS) for src in (w, mom, var)]
    vec_shapes = [w[k].shape for k in _VECTORS]
    for dst, r in zip(outs, update(part_vec, vec_packs, "adamw_vectors")):
        dst.update(zip(_VECTORS, _unpack(r.reshape(-1), vec_shapes)))

    rep_local = _pack([dkvn.reshape(-1), dfgb[0, :bh]], f32, LANES, 8)
    rep_parts = _all_gather([rep_local], "gather_replicated_grads")[0]
    rpacks = [_pack([src[k].reshape(-1) for k in _REPLICATED], f32, LANES, 8) for src in (w, mom, var)]
    rres = _adamw(rep_parts, *rpacks, rep_local.shape[0], "adamw_replicated")
    rshapes = [w[k].shape for k in _REPLICATED]
    g_r, d_r, m_r, v_r = ({k: a for k, a in zip(_REPLICATED, _unpack(r.reshape(-1), rshapes))} for r in rres)

    out = [loss, grad_x]
    for sh, rp in ((g_s, g_r), (d_s, d_r), (m_s, m_r), (v_s, v_r)):
        out += [sh[k] if k in sh else rp[k] for k in names]
    return tuple(out)
```

```python
import functools
import math

import jax
import jax.numpy as jnp
from jax import lax
from jax.experimental import pallas as pl
from jax.experimental.pallas import tpu as pltpu

f32 = jnp.float32
bf16 = jnp.bfloat16
SDS = jax.ShapeDtypeStruct

EPS = 1e-6
A_DK = 128
A_CHUNK = 64
GLA_GROUP = 4
GLA_HEADS = 2
TOKEN_TILE_CAP = 1024
MODEL_TILE_CAP = 1024
LANES = 128
SUBLANES = 8
BF16_ROWS = 16
VMEM_LIMIT = 56 * 1024 * 1024
ADAM_LR, ADAM_B1, ADAM_B2, ADAM_EPS, ADAM_WD, ADAM_STEP = 0.001, 0.9, 0.999, 1e-08, 0.01, 10
N_DEV = 8
MESH = pl.DeviceIdType.MESH

_NT = (((1,), (1,)), ((), ()))
_TN = (((0,), (0,)), ((), ()))
_HI = lax.Precision.HIGHEST


def _params(**kw):
    return pltpu.CompilerParams(vmem_limit_bytes=VMEM_LIMIT, **kw)


def _div_tile(n, cap, mult=BF16_ROWS):
    best = None
    for t in range(mult, min(n, cap) + 1, mult):
        if n % t == 0:
            best = t
    assert best is not None, (n, cap, mult)
    return best


def _bdot(a, b):
    return jnp.dot(a.astype(bf16), b.astype(bf16), preferred_element_type=f32)


def _bdot_nt(a, b):
    return lax.dot_general(a.astype(bf16), b.astype(bf16), _NT, preferred_element_type=f32)


def _bdot_tn(a, b):
    return lax.dot_general(a.astype(bf16), b.astype(bf16), _TN, preferred_element_type=f32)


def _iota2(shape, axis):
    return lax.broadcasted_iota(jnp.int32, shape, axis)


def _cumsum_rows(x):
    n = x.shape[0]
    tri = (_iota2((n, n), 0) >= _iota2((n, n), 1)).astype(f32)
    return jnp.dot(tri, x, precision=_HI, preferred_element_type=f32)


def _revcumsum_rows(x):
    n = x.shape[0]
    tri = (_iota2((n, n), 1) >= _iota2((n, n), 0)).astype(f32)
    return jnp.dot(tri, x, precision=_HI, preferred_element_type=f32)


def _sigmoid(x):
    return 1.0 / (1.0 + jnp.exp(-x))


def _rms_fwd(x, g, tm, name):
    n, d = x.shape

    def body(x_ref, g_ref, o_ref):
        xv = x_ref[...]
        r = lax.rsqrt(jnp.mean(xv * xv, axis=-1, keepdims=True) + EPS)
        o_ref[...] = (xv * r * g_ref[...]).astype(o_ref.dtype)

    return pl.pallas_call(
        body, grid=(n // tm,), name=name,
        in_specs=[pl.BlockSpec((tm, d), lambda i: (i, 0)), pl.BlockSpec((1, d), lambda i: (0, 0))],
        out_specs=pl.BlockSpec((tm, d), lambda i: (i, 0)),
        out_shape=SDS((n, d), bf16), compiler_params=_params(),
    )(x, g)


def _mm(a, w, out_dtype, tm, tn, name):
    n, k = a.shape
    m = w.shape[1]

    def body(a_ref, w_ref, o_ref):
        o_ref[...] = _bdot(a_ref[...], w_ref[...]).astype(o_ref.dtype)

    return pl.pallas_call(
        body, grid=(m // tn, n // tm), name=name,
        in_specs=[pl.BlockSpec((tm, k), lambda j, i: (i, 0)), pl.BlockSpec((k, tn), lambda j, i: (0, j))],
        out_specs=pl.BlockSpec((tm, tn), lambda j, i: (i, j)),
        out_shape=SDS((n, m), out_dtype), compiler_params=_params(),
    )(a, w)


def _mm_norm_res(a, w, g, h, next_gains, tm, name):
    n, k = a.shape
    d = w.shape[1]
    nn = len(next_gains)

    def body(a_ref, w_ref, g_ref, h_ref, *rest):
        ng_refs, (hn_ref, mix_ref), out_refs = rest[:nn], rest[nn:nn + 2], rest[nn + 2:]
        mix = _bdot(a_ref[...], w_ref[...])
        r = lax.rsqrt(jnp.mean(mix * mix, axis=-1, keepdims=True) + EPS)
        mix_ref[...] = mix
        hn = h_ref[...] + mix * r * g_ref[...]
        hn_ref[...] = hn
        if nn:
            rn = lax.rsqrt(jnp.mean(hn * hn, axis=-1, keepdims=True) + EPS)
            for ng_ref, o_ref in zip(ng_refs, out_refs):
                o_ref[...] = (hn * rn * ng_ref[...]).astype(o_ref.dtype)

    row = pl.BlockSpec((tm, d), lambda i: (i, 0))
    vec = pl.BlockSpec((1, d), lambda i: (0, 0))
    return pl.pallas_call(
        body, grid=(n // tm,), name=name,
        in_specs=[pl.BlockSpec((tm, k), lambda i: (i, 0)), pl.BlockSpec((k, d), lambda i: (0, 0)), vec, row] + [vec] * nn,
        out_specs=[row] * (2 + nn),
        out_shape=[SDS((n, d), f32), SDS((n, d), f32)] + [SDS((n, d), bf16)] * nn, compiler_params=_params(),
    )(a, w, g, h, *next_gains)


def _rms_bwd(x, g, dy, dh_in, out_dtype, tm, name, then=None):
    n, d = x.shape
    has_add = dh_in is not None
    has_next = then is not None

    def norm_bwd(xv, gv, dyv):
        r = lax.rsqrt(jnp.mean(xv * xv, axis=-1, keepdims=True) + EPS)
        xr = xv * r
        gdy = dyv * gv
        return r * gdy - xr * (r * r) * jnp.mean(xv * gdy, axis=-1, keepdims=True), jnp.sum(dyv * xr, axis=0, keepdims=True)

    def body(*refs):
        refs = list(refs)
        x_ref, g_ref, dy_ref = refs[:3]
        dh_ref = refs[3] if has_add else None
        ins_end = 3 + has_add + 2 * has_next
        o_ref, dg_ref = refs[ins_end:ins_end + 2]
        dx, dg = norm_bwd(x_ref[...], g_ref[...], dy_ref[...].astype(f32))
        if has_add:
            dx = dx + dh_ref[...]
        o_ref[...] = dx.astype(o_ref.dtype)

        @pl.when(pl.program_id(0) == 0)
        def _():
            for ref in refs[ins_end + 1::2]:
                ref[...] = jnp.zeros_like(ref)

        dg_ref[...] += dg
        if has_next:
            x2_ref, g2_ref = refs[ins_end - 2:ins_end]
            o2_ref, dg2_ref = refs[ins_end + 2:]
            dx2, dg2 = norm_bwd(x2_ref[...], g2_ref[...], dx)
            o2_ref[...] = dx2.astype(o2_ref.dtype)
            dg2_ref[...] += dg2

    row = pl.BlockSpec((tm, d), lambda i: (i, 0))
    vec = pl.BlockSpec((1, d), lambda i: (0, 0))
    ins = [x, g, dy] + ([dh_in] if has_add else []) + (list(then) if has_next else [])
    return pl.pallas_call(
        body, grid=(n // tm,), name=name,
        in_specs=[row, vec, row] + ([row] if has_add else []) + ([row, vec] if has_next else []),
        out_specs=[row, vec] + ([row, vec] if has_next else []),
        out_shape=[SDS((n, d), out_dtype), SDS((1, d), f32)] + ([SDS((n, d), bf16), SDS((1, d), f32)] if has_next else []),
        compiler_params=_params(),
    )(*ins)


def _mm_nt(pairs, out_dtype, tm, tk, name):
    n = pairs[0][0].shape[0]
    k = pairs[0][1].shape[0]
    np_ = len(pairs)

    def body(*refs):
        o_ref = refs[-1]
        acc = None
        for p in range(np_):
            t = _bdot_nt(refs[2 * p][...], refs[2 * p + 1][...])
            acc = t if acc is None else acc + t
        o_ref[...] = acc.astype(o_ref.dtype)

    in_specs, ins = [], []
    for dy, w in pairs:
        m = dy.shape[1]
        in_specs += [pl.BlockSpec((tm, m), lambda j, i: (i, 0)), pl.BlockSpec((tk, m), lambda j, i: (j, 0))]
        ins += [dy, w]
    return pl.pallas_call(
        body, grid=(k // tk, n // tm), name=name, in_specs=in_specs,
        out_specs=pl.BlockSpec((tm, tk), lambda j, i: (i, j)),
        out_shape=SDS((n, k), out_dtype), compiler_params=_params(),
    )(*ins)


def _mm_tn(x, dy, tm, tk, tn, name):
    n, k = x.shape
    m = dy.shape[1]

    def body(x_ref, dy_ref, o_ref):
        @pl.when(pl.program_id(2) == 0)
        def _():
            o_ref[...] = jnp.zeros_like(o_ref)

        o_ref[...] += _bdot_tn(x_ref[...], dy_ref[...])

    return pl.pallas_call(
        body, grid=(k // tk, m // tn, n // tm), name=name,
        in_specs=[pl.BlockSpec((tm, tk), lambda a, b, i: (i, a)), pl.BlockSpec((tm, tn), lambda a, b, i: (i, b))],
        out_specs=pl.BlockSpec((tk, tn), lambda a, b, i: (a, b)),
        out_shape=SDS((k, m), f32), compiler_params=_params(),
    )(x, dy)


def _split3(x):
    hi = x.astype(bf16)
    r = x - hi.astype(f32)
    mid = r.astype(bf16)
    return hi, mid, (r - mid.astype(f32)).astype(bf16)


def _mask_dot(mask, x):
    hi, mid, lo = _split3(x)
    dot = lambda p: jnp.dot(mask, p, preferred_element_type=f32)
    return dot(hi) + dot(mid) + dot(lo)


_BNN = (((2,), (1,)), ((0,), (0,)))
_BNT = (((2,), (2,)), ((0,), (0,)))
_BTN = (((1,), (1,)), ((0,), (0,)))


def _hdot(a, b, dims):
    return lax.dot_general(a.astype(bf16), b.astype(bf16), dims, preferred_element_type=f32)


def _heads(x, nhb):
    return jnp.stack([x[:, h * A_DK:(h + 1) * A_DK] for h in range(nhb)])


def _mask_dot_heads(mask, x):
    return jnp.stack([_mask_dot(mask, x[h]) for h in range(x.shape[0])])


def _chunk_rows(parts, cl):
    tiles = [jnp.broadcast_to(p, (p.shape[0], cl, p.shape[2])) for p in parts]
    return tiles[0] if len(tiles) == 1 else jnp.concatenate(tiles, axis=1)


def _cat(parts):
    return parts[0] if len(parts) == 1 else jnp.concatenate(parts, axis=1)


def _gla_group_fwd(qg, fg, vg, lb, st, nc, cl):
    g = nc * cl
    sg = _sigmoid(fg)
    f = lb + (1.0 - lb) * sg
    k = 1.0 - f
    row, col = _iota2((g, g), 0), _iota2((g, g), 1)
    chunk_of = lambda idx: sum((idx >= u * cl).astype(jnp.int32) for u in range(1, nc)) if nc > 1 else 0
    same = chunk_of(row) == chunk_of(col) if nc > 1 else None
    causal = row >= col if nc == 1 else jnp.logical_and(same, row >= col)
    anti = col >= row if nc == 1 else jnp.logical_and(same, col >= row)
    b = _mask_dot_heads(causal.astype(bf16), jnp.log(f))
    bls = [b[:, (u + 1) * cl - 1:(u + 1) * cl, :] for u in range(nc)]
    ebls = [jnp.exp(x) for x in bls]
    e = jnp.exp(b)
    ei = jnp.exp(-b)
    eo = jnp.exp(_chunk_rows(bls, cl) - b)
    qi, ki, ko = qg * e, k * ei, k * eo
    att = jnp.where(causal[None], _hdot(qi, ki, _BNT), 0.0)
    o_intra = _hdot(att, vg, _BNN)
    sl = [slice(u * cl, (u + 1) * cl) for u in range(nc)]
    ds = [_hdot(vg[:, s], ko[:, s], _BTN) for s in sl]
    sts = [st]
    for u in range(nc):
        sts.append(sts[u] * ebls[u] + ds[u])
    o = o_intra + _cat([_hdot(qi[:, sl[u]], sts[u], _BNT) for u in range(nc)])
    return dict(sg=sg, f=f, e=e, ei=ei, eo=eo, ebls=ebls, qi=qi, ki=ki, ko=ko, att=att, o=o, sts=sts, causal=causal,
                anti=anti, sl=sl)


def _gla_group(nreal, want):
    while nreal % want:
        want //= 2
    return max(want, 1)


def _head_out(o, ggc, hg):
    r = lax.rsqrt(jnp.mean(o * o, axis=-1, keepdims=True) + EPS)
    return o * r * hg * (ggc * _sigmoid(ggc))


def _gla_fwd(pmat, lb, hg, bl_, t, nm, name):
    n, d4 = pmat.shape
    d = d4 // 4
    nh = d // A_DK
    nreal = (t - nm) // A_CHUNK
    nch = nreal + 1
    un = _gla_group(nreal, GLA_GROUP)
    hb = _gla_group(nh, GLA_HEADS)
    ng = nh // hb
    wide = hb * A_DK

    def body(q_ref, f_ref, i_ref, gg_ref, lb_ref, hg_ref, og_ref, ss_ref):
        lbv, hgv = _heads(lb_ref[...], hb), _heads(hg_ref[...], hb)
        take = lambda ref, rows: _heads(ref[rows, :], hb)

        def run(rows, st, idx, nc, cl):
            w = _gla_group_fwd(take(q_ref, rows), take(f_ref, rows), take(i_ref, rows), lbv, st, nc, cl)
            out = _head_out(w["o"], take(gg_ref, rows), hgv)
            for h in range(hb):
                for u in range(nc):
                    ss_ref[h, idx + u] = w["sts"][u][h]
                og_ref[rows, h * A_DK:(h + 1) * A_DK] = out[h].astype(og_ref.dtype)
            return w["sts"][nc]

        st = run(pl.ds(0, nm), jnp.zeros((hb, A_DK, A_DK), f32), 0, 1, nm)

        def step(it, st):
            rows = pl.ds(pl.multiple_of(nm + it * (un * A_CHUNK), BF16_ROWS), un * A_CHUNK)
            return run(rows, st, 1 + it * un, un, A_CHUNK)

        lax.fori_loop(0, nreal // un, step, st)

    col = lambda o: pl.BlockSpec((t, wide), lambda b, h: (b, o * ng + h))
    vec = pl.BlockSpec((1, wide), lambda b, h: (0, h))
    return pl.pallas_call(
        body, grid=(bl_, ng), name=name,
        in_specs=[col(0), col(1), col(2), col(3), vec, vec],
        out_specs=[pl.BlockSpec((t, wide), lambda b, h: (b, h)),
                   pl.BlockSpec((hb, nch, A_DK, A_DK), lambda b, h: (b * ng + h, 0, 0, 0))],
        out_shape=[SDS((n, d), bf16), SDS((bl_ * nh, nch, A_DK, A_DK), f32)], compiler_params=_params(),
    )(pmat, pmat, pmat, pmat, lb, hg)


def _gla_bwd(pmat, ss, dog, lb, hg, bl_, t, nm, name):
    n, d4 = pmat.shape
    d = d4 // 4
    nh = d // A_DK
    nreal = (t - nm) // A_CHUNK
    nch = nreal + 1
    un = _gla_group(nreal, GLA_GROUP)
    hb = _gla_group(nh, GLA_HEADS)
    ng = nh // hb
    wide = hb * A_DK

    def body(q_ref, f_ref, i_ref, gg_ref, ss_ref, dog_ref, lb_ref, hg_ref,
             dq_ref, df_ref, di_ref, dgg_ref, dlb_ref, dhg_ref):
        lbv, hgv = _heads(lb_ref[...], hb), _heads(hg_ref[...], hb)
        take = lambda ref, rows: _heads(ref[rows, :], hb)

        def put(ref, rows, val):
            for h in range(hb):
                ref[rows, h * A_DK:(h + 1) * A_DK] = val[h].astype(ref.dtype)

        def run(rows, idx, carry, nc, cl):
            dst, dlb, dhg = carry
            qg, fg, vg, ggc = take(q_ref, rows), take(f_ref, rows), take(i_ref, rows), take(gg_ref, rows)
            dogc = take(dog_ref, rows).astype(f32)
            st_in = jnp.stack([ss_ref[h, idx] for h in range(hb)])
            w = _gla_group_fwd(qg, fg, vg, lbv, st_in, nc, cl)
            o, qi, ki, ko, sl, sts, ebls = w["o"], w["qi"], w["ki"], w["ko"], w["sl"], w["sts"], w["ebls"]
            r = lax.rsqrt(jnp.mean(o * o, axis=-1, keepdims=True) + EPS)
            sgg = _sigmoid(ggc)
            sil = ggc * sgg
            on = o * r
            dhg = dhg + jnp.sum(dogc * sil * on, axis=1, keepdims=True)
            put(dgg_ref, rows, dogc * on * hgv * (sgg * (1.0 + ggc * (1.0 - sgg))))
            tt = dogc * sil * hgv
            do = r * tt - on * (r * r) * jnp.mean(o * tt, axis=-1, keepdims=True)
            xs = [_hdot(do[:, s], qi[:, s], _BTN) for s in sl]
            dsts = [None] * nc + [dst]
            for u in reversed(range(nc)):
                dsts[u] = dsts[u + 1] * ebls[u] + xs[u]
            datt = jnp.where(w["causal"][None], _hdot(do, vg, _BNT), 0.0)
            dv = _hdot(w["att"], do, _BTN) + _cat([_hdot(ko[:, sl[u]], dsts[u + 1], _BNT) for u in range(nc)])
            dko = _cat([_hdot(vg[:, sl[u]], dsts[u + 1], _BNN) for u in range(nc)])
            dqi = _hdot(datt, ki, _BNN) + _cat([_hdot(do[:, sl[u]], sts[u], _BNN) for u in range(nc)])
            dki = _hdot(datt, qi, _BTN)
            dk = dki * w["ei"] + dko * w["eo"]
            dkoko = dko * ko
            db = dqi * qi - dki * ki - dkoko
            rowi = lax.broadcasted_iota(jnp.int32, db.shape, 1)
            for u in range(nc):
                d_ebl = jnp.sum(dsts[u + 1] * sts[u], axis=1, keepdims=True)
                dbl = jnp.sum(dkoko[:, sl[u]], axis=1, keepdims=True) + d_ebl * ebls[u]
                db = db + jnp.where(rowi == (u + 1) * cl - 1, dbl, 0.0)
            dlogf = _mask_dot_heads(w["anti"].astype(bf16), db)
            df = dlogf / w["f"] - dk
            sg = w["sg"]
            put(dq_ref, rows, dqi * w["e"])
            put(df_ref, rows, df * (1.0 - lbv) * sg * (1.0 - sg))
            put(di_ref, rows, dv)
            dlb = dlb + jnp.sum(df * (1.0 - sg), axis=1, keepdims=True)
            return dsts[0], dlb, dhg

        zero = jnp.zeros((hb, 1, A_DK), f32)
        ngroups = nreal // un

        def step(it, carry):
            grp = ngroups - 1 - it
            rows = pl.ds(pl.multiple_of(nm + grp * (un * A_CHUNK), BF16_ROWS), un * A_CHUNK)
            return run(rows, 1 + grp * un, carry, un, A_CHUNK)

        carry = lax.fori_loop(0, ngroups, step, (jnp.zeros((hb, A_DK, A_DK), f32), zero, zero))
        _, dlb, dhg = run(pl.ds(0, nm), 0, carry, 1, nm)

        @pl.when(pl.program_id(1) == 0)
        def _():
            dlb_ref[...] = jnp.zeros_like(dlb_ref)
            dhg_ref[...] = jnp.zeros_like(dhg_ref)

        for h in range(hb):
            dlb_ref[:, h * A_DK:(h + 1) * A_DK] += dlb[h]
            dhg_ref[:, h * A_DK:(h + 1) * A_DK] += dhg[h]

    col = lambda o: pl.BlockSpec((t, wide), lambda h, b: (b, o * ng + h))
    blk = pl.BlockSpec((t, wide), lambda h, b: (b, h))
    vec = pl.BlockSpec((1, wide), lambda h, b: (0, h))
    return pl.pallas_call(
        body, grid=(ng, bl_), name=name,
        in_specs=[col(0), col(1), col(2), col(3),
                  pl.BlockSpec((hb, nch, A_DK, A_DK), lambda h, b: (b * ng + h, 0, 0, 0)), blk, vec, vec],
        out_specs=[blk, blk, blk, blk, vec, vec],
        out_shape=[SDS((n, d), bf16)] * 4 + [SDS((1, d), f32)] * 2, compiler_params=_params(),
    )(pmat, pmat, pmat, pmat, ss, dog, lb, hg)


def _shifted(x, halo, before):
    n = x.shape[0]
    both = jnp.concatenate([halo, x] if before else [x, halo], axis=0)
    row, col = _iota2((n, n + BF16_ROWS), 0), _iota2((n, n + BF16_ROWS), 1)
    src = row + BF16_ROWS if before else row
    step = -1 if before else 1
    pick = lambda s: jnp.dot((col == src + step * s).astype(bf16), both, preferred_element_type=f32)
    return pick(1), pick(2)


def _conv3(xb, halo, w):
    x = xb.astype(f32)
    x1, x2 = _shifted(xb, halo, True)
    return x, x1, x2, w[0:1, :] * x2 + w[1:2, :] * x1 + w[2:3, :] * x


def _conv_gate_fwd(ug, uv, cwg, cwv, bl_, t, tc, name):
    n, ff = ug.shape
    nt = t // tc

    def body(ug_ref, uv_ref, wg_ref, wv_ref, a_ref, hg_ref, hv_ref):
        @pl.when(pl.program_id(1) == 0)
        def _():
            hg_ref[...] = jnp.zeros_like(hg_ref)
            hv_ref[...] = jnp.zeros_like(hv_ref)

        xg, xv = ug_ref[...], uv_ref[...]
        cg = _conv3(xg, hg_ref[...], wg_ref[...])[3]
        cv = _conv3(xv, hv_ref[...], wv_ref[...])[3]
        a_ref[...] = (cg * _sigmoid(cg) * cv).astype(a_ref.dtype)
        hg_ref[...] = xg[tc - BF16_ROWS:tc, :].astype(hg_ref.dtype)
        hv_ref[...] = xv[tc - BF16_ROWS:tc, :].astype(hv_ref.dtype)

    row = pl.BlockSpec((tc, ff), lambda b, i: (b * nt + i, 0))
    wsp = pl.BlockSpec((3, ff), lambda b, i: (0, 0))
    return pl.pallas_call(
        body, grid=(bl_, nt), name=name, in_specs=[row, row, wsp, wsp], out_specs=row,
        out_shape=SDS((n, ff), bf16),
        scratch_shapes=[pltpu.VMEM((BF16_ROWS, ff), bf16), pltpu.VMEM((BF16_ROWS, ff), bf16)], compiler_params=_params(),
    )(ug, uv, cwg, cwv)


def _conv_gate_bwd(ug, uv, cwg, cwv, da, bl_, t, tc, name):
    n, ff = ug.shape
    nt = t // tc
    per = tc // BF16_ROWS

    def body(ug_ref, uv_ref, pg_ref, pv_ref, wg_ref, wv_ref, da_ref, dug_ref, duv_ref, dwg_ref, dwv_ref, ng_ref, nv_ref):
        first = jnp.logical_and(pl.program_id(0) == 0, pl.program_id(1) == 0)

        @pl.when(first)
        def _():
            dwg_ref[...] = jnp.zeros_like(dwg_ref)
            dwv_ref[...] = jnp.zeros_like(dwv_ref)

        @pl.when(pl.program_id(1) == 0)
        def _():
            ng_ref[...] = jnp.zeros_like(ng_ref)
            nv_ref[...] = jnp.zeros_like(nv_ref)

        seq_start = pl.program_id(1) == nt - 1
        dav = da_ref[...].astype(f32)

        def half(u_ref, p_ref, w_ref):
            halo = p_ref[...]
            return _conv3(u_ref[...], jnp.where(seq_start, jnp.zeros_like(halo), halo), w_ref[...])

        xg, xg1, xg2, cg = half(ug_ref, pg_ref, wg_ref)
        xv, xv1, xv2, cv = half(uv_ref, pv_ref, wv_ref)
        sg = _sigmoid(cg)
        dcg = dav * cv * (sg * (1.0 + cg * (1.0 - sg)))
        dcv = dav * (cg * sg)

        def back(dc, x, x1, x2, w_ref, nx_ref, du_ref, dw_ref):
            w = w_ref[...]
            dcb = dc.astype(bf16)
            dc1, dc2 = _shifted(dcb, nx_ref[...], False)
            du = w[2:3, :] * dc + w[1:2, :] * dc1 + w[0:1, :] * dc2
            du_ref[...] = du.astype(du_ref.dtype)
            dw_ref[0:1, :] += jnp.sum(dc * x2, axis=0, keepdims=True)
            dw_ref[1:2, :] += jnp.sum(dc * x1, axis=0, keepdims=True)
            dw_ref[2:3, :] += jnp.sum(dc * x, axis=0, keepdims=True)
            nx_ref[...] = dcb[0:BF16_ROWS, :].astype(nx_ref.dtype)

        back(dcg, xg, xg1, xg2, wg_ref, ng_ref, dug_ref, dwg_ref)
        back(dcv, xv, xv1, xv2, wv_ref, nv_ref, duv_ref, dwv_ref)

    row = pl.BlockSpec((tc, ff), lambda b, i: (b * nt + nt - 1 - i, 0))
    prev = pl.BlockSpec((BF16_ROWS, ff), lambda b, i: (jnp.maximum((b * nt + nt - 1 - i) * per - 1, 0), 0))
    wsp = pl.BlockSpec((3, ff), lambda b, i: (0, 0))
    return pl.pallas_call(
        body, grid=(bl_, nt), name=name, in_specs=[row, row, prev, prev, wsp, wsp, row],
        out_specs=[row, row, wsp, wsp],
        out_shape=[SDS((n, ff), bf16), SDS((n, ff), bf16), SDS((3, ff), f32), SDS((3, ff), f32)],
        scratch_shapes=[pltpu.VMEM((BF16_ROWS, ff), bf16), pltpu.VMEM((BF16_ROWS, ff), bf16)], compiler_params=_params(),
    )(ug, uv, ug, uv, cwg, cwv, da)


def _zf_c(hk, wzf, fgb, bl_, t, tm, name):
    n, d = hk.shape
    nt = t // tm

    def body(hk_ref, w_ref, b_ref, zf_ref, c_ref, carry_ref):
        @pl.when(pl.program_id(1) == 0)
        def _():
            carry_ref[...] = jnp.zeros_like(carry_ref)

        z = _bdot(hk_ref[...], w_ref[...]) + b_ref[...]
        ls = jnp.minimum(z, 0.0) - jnp.log(1.0 + jnp.exp(-jnp.abs(z)))
        c = _cumsum_rows(ls) + carry_ref[...]
        zf_ref[...] = z
        c_ref[...] = c
        carry_ref[...] = c[tm - 1:tm, :]

    row = lambda w: pl.BlockSpec((tm, w), lambda b, i: (b * nt + i, 0))
    return pl.pallas_call(
        body, grid=(bl_, nt), name=name,
        in_specs=[row(d), pl.BlockSpec((d, LANES), lambda b, i: (0, 0)), pl.BlockSpec((1, LANES), lambda b, i: (0, 0))],
        out_specs=[row(LANES), row(LANES)],
        out_shape=[SDS((n, LANES), f32), SDS((n, LANES), f32)],
        scratch_shapes=[pltpu.VMEM((1, LANES), f32)], compiler_params=_params(),
    )(hk, wzf, fgb)


def _c_bwd(dc, zf, bl_, t, tm, name):
    n = dc.shape[0]
    nt = t // tm

    def body(dc_ref, zf_ref, dzf_ref, dfg_ref, carry_ref):
        @pl.when(jnp.logical_and(pl.program_id(0) == 0, pl.program_id(1) == 0))
        def _():
            dfg_ref[...] = jnp.zeros_like(dfg_ref)

        @pl.when(pl.program_id(1) == 0)
        def _():
            carry_ref[...] = jnp.zeros_like(carry_ref)

        rc = _revcumsum_rows(dc_ref[...]) + carry_ref[...]
        dz = rc * _sigmoid(-zf_ref[...])
        dzf_ref[...] = dz.astype(dzf_ref.dtype)
        dfg_ref[...] += jnp.sum(dz, axis=0, keepdims=True)
        carry_ref[...] = rc[0:1, :]

    row = pl.BlockSpec((tm, LANES), lambda b, i: (b * nt + nt - 1 - i, 0))
    vec = pl.BlockSpec((1, LANES), lambda b, i: (0, 0))
    return pl.pallas_call(
        body, grid=(bl_, nt), name=name, in_specs=[row, row], out_specs=[row, vec],
        out_shape=[SDS((n, LANES), bf16), SDS((1, LANES), f32)],
        scratch_shapes=[pltpu.VMEM((1, LANES), f32)], compiler_params=_params(),
    )(dc, zf)


def _is_pow2(x):
    m, _ = math.frexp(x)
    return m == 0.5


def _prescale(qh, scale):
    return (qh.astype(f32) * scale).astype(bf16)


def _attn_fwd(q, kv, ck, bl_, t, tq, hd, name):
    n, d = q.shape
    npair = d // LANES
    hp = LANES // hd
    nq = t // tq
    scale = 1.0 / (hd ** 0.5)

    pre = _is_pow2(scale)

    def body(q_ref, k_ref, v_ref, ck_ref, o_ref, lse_ref):
        i = pl.program_id(2)
        diag = _iota2((tq, tq), 0) >= _iota2((tq, tq), 1)
        for hh in range(hp):
            lanes = slice(hh * hd, (hh + 1) * hd)
            qh = _prescale(q_ref[:, lanes], scale) if pre else q_ref[:, lanes]

            def block(j, carry, masked, lanes=lanes, qh=qh, hh=hh):
                m, l, acc = carry
                rows = pl.ds(pl.multiple_of(j * tq, BF16_ROWS), tq)
                s = _bdot_nt(qh, k_ref[rows, lanes])
                s = (s if pre else s * scale) - ck_ref[0, 0, j, hh:hh + 1, :]
                if masked:
                    s = jnp.where(diag, s, -1e30)
                m2 = jnp.maximum(m, jnp.max(s, axis=-1, keepdims=True))
                p = jnp.exp(s - m2)
                a = jnp.exp(m - m2)
                return m2, a * l + jnp.sum(p, axis=-1, keepdims=True), a * acc + _bdot(p, v_ref[rows, lanes])

            init = (jnp.full((tq, 1), -1e30, f32), jnp.zeros((tq, 1), f32), jnp.zeros((tq, hd), f32))
            carry = lax.fori_loop(0, i, functools.partial(block, masked=False), init)
            m, l, acc = block(i, carry, True)
            o_ref[:, lanes] = (acc / l).astype(o_ref.dtype)
            lse_ref[:, lanes] = jnp.broadcast_to(m + jnp.log(l), (tq, hd))

    nk = nq
    return pl.pallas_call(
        body, grid=(bl_, npair, nq), name=name,
        in_specs=[pl.BlockSpec((tq, LANES), lambda b, p, i: (b * nq + i, p)),
                  pl.BlockSpec((t, LANES), lambda b, p, i: (b, p)),
                  pl.BlockSpec((t, LANES), lambda b, p, i: (b, npair + p)),
                  pl.BlockSpec((1, 1, nk, hp, tq), lambda b, p, i: (b, p, 0, 0, 0))],
        out_specs=[pl.BlockSpec((tq, LANES), lambda b, p, i: (b * nq + i, p)),
                   pl.BlockSpec((tq, LANES), lambda b, p, i: (b * nq + i, p))],
        out_shape=[SDS((n, d), f32), SDS((n, d), f32)], compiler_params=_params(),
    )(q, kv, kv, ck)


def _attn_bwd(q, kv, o, do, lse, ck, bl_, t, tq, hd, name):
    n, d = q.shape
    npair = d // LANES
    hp = LANES // hd
    nq = t // tq
    scale = 1.0 / (hd ** 0.5)

    pre = _is_pow2(scale)

    def body(q_ref, k_ref, v_ref, o_ref, do_ref, lse_ref, ck_ref, dq_ref, dk_ref, dv_ref, dck_ref, dcq_ref):
        j = pl.program_id(2)

        @pl.when(j == 0)
        def _():
            dq_ref[...] = jnp.zeros_like(dq_ref)
            dcq_ref[...] = jnp.zeros_like(dcq_ref)

        diag = _iota2((tq, tq), 0) >= _iota2((tq, tq), 1)
        for hh in range(hp):
            lanes = slice(hh * hd, (hh + 1) * hd)
            kh = k_ref[:, lanes]
            vh = v_ref[:, lanes]
            kt = kh.astype(f32).T.astype(bf16)
            cs = ck_ref[0, 0, 0, hh:hh + 1, :]

            def block(i, carry, masked, lanes=lanes, kh=kh, vh=vh, kt=kt, cs=cs, hh=hh):
                dkt, dvt, dcs = carry
                rows = pl.ds(pl.multiple_of(i * tq, BF16_ROWS), tq)
                qh = _prescale(q_ref[rows, lanes], scale) if pre else q_ref[rows, lanes]
                doh = do_ref[rows, lanes]
                s = _bdot_nt(qh, kh)
                s = (s if pre else s * scale) - cs
                if masked:
                    s = jnp.where(diag, s, -1e30)
                p = jnp.exp(s - lse_ref[rows, hh * hd:hh * hd + 1])
                delta = jnp.sum(doh.astype(f32) * o_ref[rows, lanes].astype(f32), axis=-1, keepdims=True)
                ds = p * (_bdot_nt(doh, vh) - delta)
                dsb = ds.astype(bf16)
                dq_ref[rows, lanes] += _bdot_nt(kt, dsb).T * scale
                dcq_ref[0, rows, hh:hh + 1] += jnp.sum(ds, axis=-1, keepdims=True)
                dkq = _bdot_tn(qh, dsb)
                return (dkt + (dkq if pre else dkq * scale), dvt + _bdot_tn(doh, p), dcs - jnp.sum(ds, axis=0, keepdims=True))

            init = (jnp.zeros((hd, tq), f32), jnp.zeros((hd, tq), f32), jnp.zeros((1, tq), f32))
            dkt, dvt, dcs = lax.fori_loop(j + 1, nq, functools.partial(block, masked=False), block(j, init, True))
            dk_ref[:, lanes] = dkt.T.astype(dk_ref.dtype)
            dv_ref[:, lanes] = dvt.T.astype(dv_ref.dtype)
            dck_ref[0, 0, 0, hh:hh + 1, :] = dcs

    whole = lambda c0: pl.BlockSpec((t, LANES), lambda b, p, j: (b, c0 + p))
    tile = lambda c0: pl.BlockSpec((tq, LANES), lambda b, p, j: (b * nq + j, c0 + p))
    ckspec = pl.BlockSpec((1, 1, 1, hp, tq), lambda b, p, j: (b, p, j, 0, 0))
    cqspec = pl.BlockSpec((1, t, hp), lambda b, p, j: (p, b, 0))
    return pl.pallas_call(
        body, grid=(bl_, npair, nq), name=name,
        in_specs=[whole(0), tile(0), tile(npair), whole(0), whole(0), whole(0), ckspec],
        out_specs=[whole(0), tile(0), tile(0), ckspec, cqspec],
        out_shape=[SDS((n, d), f32), SDS((n, d), bf16), SDS((n, d), bf16), SDS((bl_, npair, nq, hp, tq), f32),
                   SDS((npair, n, hp), f32)],
        compiler_params=_params(),
    )(q, kv, kv, o, do, lse, ck)


def _loss_head(h, target, t, nm, tm, name):
    n, d = h.shape
    nt = t // tm

    def body(h_ref, t_ref, loss_ref, dh_ref):
        i = pl.program_id(0)

        @pl.when(i == 0)
        def _():
            loss_ref[...] = jnp.zeros_like(loss_ref)

        pos = (i % nt) * tm + _iota2((tm, d), 0)
        err = jnp.where(pos >= nm, h_ref[...] - t_ref[...], 0.0)
        dh_ref[...] = err * (1.0 / d)
        loss_ref[...] += 0.5 * jnp.sum(jnp.mean(err * err, axis=-1, keepdims=True))

    row = pl.BlockSpec((tm, d), lambda i: (i, 0))
    return pl.pallas_call(
        body, grid=(n // tm,), name=name, in_specs=[row, row],
        out_specs=[pl.BlockSpec((8, LANES), lambda i: (0, 0)), row],
        out_shape=[SDS((8, LANES), f32), SDS((n, d), f32)], compiler_params=_params(),
    )(h, target)


def _c_key_rows(c, bl_, t, tq, bh, hp):
    npair = bh // hp
    nk = t // tq
    return c[:, :bh].reshape(bl_, nk, tq, npair, hp).transpose(0, 3, 1, 4, 2)


def _dc_rows(dck, dcq, bl_, t, bh):
    d = dck.transpose(0, 2, 4, 1, 3).reshape(bl_ * t, bh) + dcq.transpose(1, 0, 2).reshape(bl_ * t, bh)
    return jnp.pad(d, ((0, 0), (0, LANES - bh)))


_ANY = pl.BlockSpec(memory_space=pl.ANY)


def _all_gather(xs, name):
    na = len(xs)

    def body(*refs):
        x_refs, out_refs = refs[:na], refs[na:2 * na]
        send_sems, recv_sems, local_sems = refs[2 * na:]
        mx, my, mc = lax.axis_index("x"), lax.axis_index("y"), lax.axis_index("c")
        me, sibling = (mx, my, mc), (mx, my, 1 - mc)
        chips = [(1 - mx, my), (mx, 1 - my), (1 - mx, 1 - my)]

        def copy(a, k, block, to, own=False):
            px, py, pc = block
            rows = out_refs[a].at[4 * px + 2 * py + pc]
            return pltpu.make_async_remote_copy(
                src_ref=x_refs[a] if own else rows, dst_ref=rows,
                send_sem=send_sems.at[a, k], recv_sem=recv_sems.at[a, k], device_id=to, device_id_type=MESH)

        arrays = range(na)
        mine = [pltpu.make_async_copy(x_refs[a], out_refs[a].at[4 * mx + 2 * my + mc], local_sems.at[a]) for a in arrays]
        for cp in mine:
            cp.start()
        first = [copy(a, 1 + j, me, (*chip, mc), own=True) for j, chip in enumerate(chips) for a in arrays]
        first += [copy(a, 0, me, sibling, own=True) for a in arrays]
        for cp in first:
            cp.start()
        passed = []
        for j, chip in enumerate(chips):
            for a in arrays:
                copy(a, 1 + j, (*chip, mc), me).wait_recv()
                cp = copy(a, 4 + j, (*chip, mc), sibling)
                cp.start()
                passed.append(cp)
        for a in arrays:
            copy(a, 0, sibling, me).wait_recv()
        for j, chip in enumerate(chips):
            for a in arrays:
                copy(a, 4 + j, (*chip, 1 - mc), me).wait_recv()
        for cp in first + passed:
            cp.wait_send()
        for cp in mine:
            cp.wait()

    return pl.pallas_call(
        body, name=name, out_shape=[SDS((N_DEV,) + x.shape, x.dtype) for x in xs],
        in_specs=[_ANY] * na, out_specs=[_ANY] * na,
        scratch_shapes=[pltpu.SemaphoreType.DMA((na, 7)), pltpu.SemaphoreType.DMA((na, 7)), pltpu.SemaphoreType.DMA((na,))],
    )(*xs)


_HBM = pl.BlockSpec(memory_space=pltpu.HBM)
_SEM = pl.BlockSpec(memory_space=pltpu.SEMAPHORE)
_DATAFLOW = pltpu.SideEffectType.DATAFLOW_SIDE_EFFECTING
N_PEERS = N_DEV - 1


def _device_index():
    return 4 * lax.axis_index("x") + 2 * lax.axis_index("y") + lax.axis_index("c")


def _peers():
    mx, my, mc = lax.axis_index("x"), lax.axis_index("y"), lax.axis_index("c")
    peers = []
    for r in (2, 3, 4, 5, 6, 7, 1):
        px = 1 - mx if r & 4 else mx
        py = 1 - my if r & 2 else my
        pc = 1 - mc if r & 1 else mc
        peers.append(((px, py, pc), 4 * px + 2 * py + pc))
    return 4 * mx + 2 * my + mc, peers


def _push_copy(src_ref, land_ref, send_sems, recv_sems, a, k, dev, src_row, land_row, scatter):
    return pltpu.make_async_remote_copy(
        src_ref=src_ref.at[src_row] if scatter else src_ref, dst_ref=land_ref.at[land_row],
        send_sem=send_sems.at[a * N_PEERS + k], recv_sem=recv_sems.at[a * N_PEERS + k], device_id=dev, device_id_type=MESH)


def _landing(own, me):
    return lax.dynamic_update_index_in_dim(lax.empty((N_DEV,) + own.shape, own.dtype), own, me, 0)


def _push_start(srcs, lands, scatter, name):
    na = len(srcs)

    def body(*refs):
        src_refs, land_refs = refs[:na], refs[na:2 * na]
        send_sems, recv_sems = refs[2 * na], refs[2 * na + 1]
        token = refs[-1]
        me, peers = _peers()
        for a in range(na):
            for k, (dev, idx) in enumerate(peers):
                _push_copy(src_refs[a], land_refs[a], send_sems, recv_sems, a, k, dev, idx, me, scatter).start()
        token[...] = jnp.zeros_like(token)

    hbm = lambda arrs: [pltpu.HBM(a.shape, a.dtype) for a in arrs]
    out = pl.pallas_call(
        body, name=name,
        out_shape=(pltpu.SemaphoreType.DMA((na * N_PEERS,)), pltpu.SemaphoreType.DMA((na * N_PEERS,)), *hbm(srcs), *hbm(lands),
                   SDS((8, LANES), f32)),
        in_specs=[_HBM] * (2 * na),
        out_specs=(_SEM, _SEM, *([_HBM] * (2 * na)), pl.BlockSpec(memory_space=pltpu.VMEM)),
        input_output_aliases={i: 2 + i for i in range(2 * na)},
        compiler_params=pltpu.CompilerParams(has_side_effects=_DATAFLOW),
    )(*[pltpu.with_memory_space_constraint(a, pltpu.HBM) for a in list(srcs) + list(lands)])
    return out[0], out[1], list(out[2:2 + na]), list(out[2 + na:2 + 2 * na]), out[-1]


def _push_wait(send_sems, recv_sems, srcs, lands, which, after, scatter, name):
    nw = len(which)

    def body(*refs):
        src_refs, land_refs = refs[:nw], refs[nw:2 * nw]
        send_sems_, recv_sems_ = refs[2 * nw], refs[2 * nw + 1]
        _, peers = _peers()
        for j, a in enumerate(which):
            for k, (dev, idx) in enumerate(peers):
                cp = _push_copy(src_refs[j], land_refs[j], send_sems_, recv_sems_, a, k, dev, idx, idx, scatter)
                cp.wait_send()
                cp.wait_recv()

    hbm = lambda arrs: [pltpu.HBM(a.shape, a.dtype) for a in arrs]
    out = pl.pallas_call(
        body, name=name, out_shape=(*hbm(srcs), *hbm(lands)),
        in_specs=[_HBM] * (2 * nw) + [_SEM, _SEM, _ANY], out_specs=[_HBM] * (2 * nw),
        input_output_aliases={i: i for i in range(2 * nw)},
        compiler_params=pltpu.CompilerParams(has_side_effects=_DATAFLOW),
    )(*srcs, *lands, send_sems, recv_sems, after)
    return list(out[nw:])


def _adamw(parts, w, m, v, tr, name):
    g, r, c = parts.shape

    def body(p_ref, w_ref, m_ref, v_ref, g_ref, d_ref, m2_ref, v2_ref):
        gr = p_ref[0].astype(f32)
        for k in range(1, g):
            gr = gr + p_ref[k].astype(f32)
        m2 = ADAM_B1 * m_ref[...] + (1.0 - ADAM_B1) * gr
        v2 = ADAM_B2 * v_ref[...] + (1.0 - ADAM_B2) * (gr * gr)
        m_hat = m2 / (1.0 - ADAM_B1 ** ADAM_STEP)
        v_hat = v2 / (1.0 - ADAM_B2 ** ADAM_STEP)
        g_ref[...] = gr
        d_ref[...] = -ADAM_LR * (m_hat / (jnp.sqrt(v_hat) + ADAM_EPS) + ADAM_WD * w_ref[...])
        m2_ref[...] = m2
        v2_ref[...] = v2

    row = pl.BlockSpec((tr, c), lambda i: (i, 0))
    return pl.pallas_call(
        body, grid=(r // tr,), name=name, in_specs=[pl.BlockSpec((g, tr, c), lambda i: (0, i, 0)), row, row, row],
        out_specs=[row] * 4, out_shape=[SDS((r, c), f32)] * 4, compiler_params=_params(),
    )(parts, w, m, v)


_SHARD_AXIS = dict(meta_tokens=1, norm_gains=2, a_w_in=2, a_lb_logits=1, a_head_norm=1, a_w_out=1, kv_w=1,
                   b_w_q=1, b_w_out=1, ffn_w_up=2, ffn_conv=2, ffn_w_down=1)
_VECTORS = ("meta_tokens", "norm_gains", "a_lb_logits", "a_head_norm", "ffn_conv")
_REPLICATED = ("kv_norm", "fg_b")
_ROW_TILE_CAP = 512


def _pack(arrs, dtype, cols, row_mult):
    lead = arrs[0].shape[:-1] if arrs[0].ndim > 1 else ()
    flat = jnp.concatenate([a.astype(dtype) for a in arrs], axis=-1)
    size = flat.shape[-1]
    per = cols * row_mult
    total = -(-size // per) * per
    flat = jnp.pad(flat, [(0, 0)] * len(lead) + [(0, total - size)])
    return flat.reshape(lead + (total // cols, cols))


def _unpack(flat, shapes):
    out, off = [], 0
    lead = flat.shape[:-1]
    for shp in shapes:
        size = 1
        for s in shp:
            size *= s
        out.append(flat[..., off:off + size].reshape(lead + tuple(shp)))
        off += size
    return out


def _unshard(seg, axis):
    a = jnp.moveaxis(seg, 0, axis)
    shp = a.shape
    return a.reshape(shp[:axis] + (shp[axis] * shp[axis + 1],) + shp[axis + 2:])


def _shard8(full, axis):
    shp = full.shape
    a = full.reshape(shp[:axis] + (N_DEV, shp[axis] // N_DEV) + shp[axis + 1:])
    return jnp.moveaxis(a, axis, 0)


def _rows(a, lead=0):
    return a.reshape(a.shape[:lead] + (-1, a.shape[-1]))


def kernel(x, meta_tokens, norm_gains, a_w_in, a_lb_logits, a_head_norm, a_w_out, kv_norm, kv_w, fg_b, b_w_q, b_w_out, ffn_w_up, ffn_conv, ffn_w_down, loss_target, m_meta_tokens, m_norm_gains, m_a_w_in, m_a_lb_logits, m_a_head_norm, m_a_w_out, m_kv_norm, m_kv_w, m_fg_b, m_b_w_q, m_b_w_out, m_ffn_w_up, m_ffn_conv, m_ffn_w_down, v_meta_tokens, v_norm_gains, v_a_w_in, v_a_lb_logits, v_a_head_norm, v_a_w_out, v_kv_norm, v_kv_w, v_fg_b, v_b_w_q, v_b_w_out, v_ffn_w_up, v_ffn_conv, v_ffn_w_down):
    names = ("meta_tokens", "norm_gains", "a_w_in", "a_lb_logits", "a_head_norm", "a_w_out", "kv_norm", "kv_w", "fg_b",
             "b_w_q", "b_w_out", "ffn_w_up", "ffn_conv", "ffn_w_down")
    w = dict(zip(names, (meta_tokens, norm_gains, a_w_in, a_lb_logits, a_head_norm, a_w_out, kv_norm, kv_w, fg_b,
                         b_w_q, b_w_out, ffn_w_up, ffn_conv, ffn_w_down)))
    mom = dict(zip(names, (m_meta_tokens, m_norm_gains, m_a_w_in, m_a_lb_logits, m_a_head_norm, m_a_w_out, m_kv_norm,
                           m_kv_w, m_fg_b, m_b_w_q, m_b_w_out, m_ffn_w_up, m_ffn_conv, m_ffn_w_down)))
    var = dict(zip(names, (v_meta_tokens, v_norm_gains, v_a_w_in, v_a_lb_logits, v_a_head_norm, v_a_w_out, v_kv_norm,
                           v_kv_w, v_fg_b, v_b_w_q, v_b_w_out, v_ffn_w_up, v_ffn_conv, v_ffn_w_down)))

    bl_, seq, d = x.shape
    nm = meta_tokens.shape[0]
    t = nm + seq
    n = bl_ * t
    bh = fg_b.shape[0]
    hd = d // bh
    hp = LANES // hd
    ff = ffn_w_down.shape[1] * N_DEV
    tm = _div_tile(t, TOKEN_TILE_CAP)
    tc = _div_tile(t, 64)
    tn = min(d, MODEL_TILE_CAP)

    vec_pack = _pack([w[k].reshape(-1) for k in _VECTORS], f32, LANES, 8)
    first = _all_gather([w["a_w_in"].astype(bf16), vec_pack], "gather_first")
    vec_segs = _unpack(first[1].reshape(N_DEV, -1), [w[k].shape for k in _VECTORS])
    small = {k: _unshard(a, _SHARD_AXIS[k]) for k, a in zip(_VECTORS, vec_segs)}
    w_in = _unshard(first[0], _SHARD_AXIS["a_w_in"])[0]
    me = _device_index()
    later_names = ("a_w_out", "ffn_w_up", "ffn_w_down", "kv_w", "b_w_q", "b_w_out", "ffn_w_up", "ffn_w_down")
    later_layer = (None, 0, 0, None, None, None, 1, 1)
    later = [(w[k] if l is None else w[k][l]).astype(bf16) for k, l in zip(later_names, later_layer)]
    later, _ = lax.optimization_barrier((later, first[1]))
    g_send, g_recv, later_src, later_land, _ = _push_start(later, [_landing(a, me) for a in later], False, "gather_rest_start")

    def gathered(which, after, name):
        lands = _push_wait(g_send, g_recv, [later_src[i] for i in which], [later_land[i] for i in which], which, after,
                           False, name)
        return [_unshard(a, _SHARD_AXIS[later_names[i]] - (later_layer[i] is not None)) for i, a in zip(which, lands)]

    gains_box = [small["norm_gains"]]
    gain = lambda l, j: gains_box[0][l, j][None]
    cw_gate, cw_val = small["ffn_conv"][:, :, :ff], small["ffn_conv"][:, :, ff:]
    head_gain = small["a_head_norm"]
    lb = jax.nn.softmax(small["a_lb_logits"], axis=0)[0:1]
    kvn = kv_norm[None]
    fgb_pad = jnp.pad(fg_b, (0, LANES - bh))[None]

    h0 = jnp.concatenate([jnp.broadcast_to(small["meta_tokens"][None], (bl_, nm, d)), x], axis=1).reshape(n, d)

    def ffn_fwd(l, h_in, fi, next_gains):
        ug = _mm(fi, w_gate[l], bf16, tm, ff, f"ffn{l}_up_gate")
        uv = _mm(fi, w_val[l], bf16, tm, ff, f"ffn{l}_up_val")
        act = _conv_gate_fwd(ug, uv, cw_gate[l], cw_val[l], bl_, t, tc, f"ffn{l}_conv_gate")
        h_out, mix, *normed = _mm_norm_res(act, w_down[l], gain(l, 3), h_in, next_gains, tm, f"ffn{l}_down")
        return h_out, (h_in, fi, ug, uv, act, mix), normed

    hn0 = _rms_fwd(h0, gain(0, 0), tm, "a_norm")
    pmat = _mm(hn0, w_in, f32, tm, tn, "a_in_proj")
    og, states = _gla_fwd(pmat, lb, head_gain, bl_, t, nm, "a_gla_fwd")
    w_out_a = gathered((0,), og, "gather_wait_a")[0][0]
    h1, mix_a, fi0 = _mm_norm_res(og, w_out_a, gain(0, 1), h0, [gain(0, 2)], tm, "a_out_proj")
    w_gate, w_val, w_down = {}, {}, {}

    def ffn_weights(l, which, after):
        w_up, w_down[l] = gathered(which, after, f"gather_wait_ffn{l}")
        w_gate[l], w_val[l] = w_up[:, :ff], w_up[:, ff:]

    ffn_weights(0, (1, 2), h1)
    h2, ffn0, (hk, hn1) = ffn_fwd(0, h1, fi0, [kvn, gain(1, 0)])

    w_kv_zf, w_q, w_out_b = gathered((3, 4, 5), h2, "gather_wait_b")
    w_kv, w_zf = w_kv_zf[:, :2 * d], jnp.pad(w_kv_zf[:, 2 * d:], ((0, 0), (0, LANES - bh)))
    w_q, w_out_b = w_q[0], w_out_b[0]
    kvp = _mm(hk, w_kv, bf16, tm, tn, "kv_proj")
    zf, cum = _zf_c(hk, w_zf, fgb_pad, bl_, t, tm, "forget_cumsum")
    ck = _c_key_rows(cum, bl_, t, tm, bh, hp)
    q = _mm(hn1, w_q, bf16, tm, tn, "b_q_proj")
    o, lse = _attn_fwd(q, kvp, ck, bl_, t, tm, hd, "b_attn_fwd")
    h3, mix_b, fi1 = _mm_norm_res(o, w_out_b, gain(1, 1), h2, [gain(1, 2)], tm, "b_out_proj")
    ffn_weights(1, (6, 7), h3)
    h4, ffn1, _ = ffn_fwd(1, h3, fi1, [])

    target = jnp.concatenate([jnp.zeros((bl_, nm, d), f32), loss_target], axis=1).reshape(n, d)
    loss8, dh = _loss_head(h4, target, t, nm, tm, "loss_head")
    loss = lax.psum(loss8[0, 0], ("x", "y", "c"))

    dgain = {}

    def ffn_bwd(l, saved, dh_out, dmix, below):
        h_in, fi, ug, uv, act, mix = saved
        if dmix is None:
            dmix, dgain[l, 3] = _rms_bwd(mix, gain(l, 3), dh_out, None, bf16, tm, f"ffn{l}_down_norm_bwd")
        dact = _mm_nt([(dmix, w_down[l])], bf16, tm, ff, f"ffn{l}_down_dx")
        dw_down = _mm_tn(act, dmix, tm, ff, tn, f"ffn{l}_down_dw")
        dug, duv, dcg, dcv = _conv_gate_bwd(ug, uv, cw_gate[l], cw_val[l], dact, bl_, t, tc, f"ffn{l}_conv_gate_bwd")
        dfi = _mm_nt([(dug, w_gate[l]), (duv, w_val[l])], bf16, tm, tn // 2, f"ffn{l}_up_dx")
        dw_up = jnp.concatenate([_mm_tn(fi, dug, tm, tn, ff, f"ffn{l}_up_gate_dw"),
                                 _mm_tn(fi, duv, tm, tn, ff, f"ffn{l}_up_val_dw")], axis=1)
        mix_below, gain_below, key_below = below
        dh_in, dgain[l, 2], dmix_below, dgain[key_below] = _rms_bwd(
            h_in, gain(l, 2), dfi, dh_out, f32, tm, f"ffn{l}_norm_bwd", then=(mix_below, gain_below))
        return dh_in, dmix_below, dw_up, jnp.concatenate([dcg, dcv], axis=1), dw_down

    def shards(full, axis):
        return _rows(_shard8(full, axis), 1).astype(bf16)

    def push_grads(bufs, name):
        lands = [_landing(lax.dynamic_index_in_dim(b, me, 0, keepdims=False), me) for b in bufs]
        s_sem, r_sem, srcs, lands, token = _push_start(bufs, lands, True, name)
        gains_box[0] = gains_box[0] + token[0, 0]
        return s_sem, r_sem, srcs, lands

    def landed(handle, after, name):
        s_sem, r_sem, srcs, lands = handle
        return _push_wait(s_sem, r_sem, srcs, lands, tuple(range(len(srcs))), after, True, name)

    dh, dmix, dw_up1, dconv1, dw_down1 = ffn_bwd(1, ffn1, dh, None, (mix_b, gain(1, 1), (1, 1)))
    push1 = push_grads([shards(dw_up1, 1), shards(dw_down1, 0)], "grad_push_ffn1")

    do = _mm_nt([(dmix, w_out_b)], bf16, tm, tn, "b_out_dx")
    dw_out_b = _mm_tn(o, dmix, tm, tn, tn, "b_out_dw")
    dq, dk, dv, dck, dcq = _attn_bwd(q, kvp, o, do, lse, ck, bl_, t, tm, hd, "b_attn_bwd")
    dhn1 = _mm_nt([(dq, w_q)], bf16, tm, tn, "b_q_dx")
    dw_q = _mm_tn(hn1, dq, tm, tn, tn, "b_q_dw")
    dh, dgain[1, 0] = _rms_bwd(h2, gain(1, 0), dhn1, dh, f32, tm, "b_norm_bwd")

    dzf, dfgb = _c_bwd(_dc_rows(dck, dcq, bl_, t, bh), zf, bl_, t, tm, "forget_cumsum_bwd")
    dhk = _mm_nt([(dk, w_kv[:, :d]), (dv, w_kv[:, d:]), (dzf, w_zf)], bf16, tm, tn, "kv_dx")
    dw_kv = jnp.concatenate([_mm_tn(hk, dk, tm, tn, tn, "k_dw"), _mm_tn(hk, dv, tm, tn, tn, "v_dw"),
                             _mm_tn(hk, dzf, tm, tn, LANES, "zf_dw")[:, :bh]], axis=1)
    dh, dkvn, dmix, dgain[0, 3] = _rms_bwd(h2, kvn, dhk, dh, f32, tm, "kv_norm_bwd", then=(ffn0[5], gain(0, 3)))
    push2 = push_grads([shards(dw_out_b, 0), shards(dw_q, 0), shards(dw_kv, 1)], "grad_push_b")

    dh, dmix, dw_up0, dconv0, dw_down0 = ffn_bwd(0, ffn0, dh, dmix, (mix_a, gain(0, 1), (0, 1)))
    push3 = push_grads([shards(dw_up0, 1), shards(dw_down0, 0)], "grad_push_ffn0")

    dog = _mm_nt([(dmix, w_out_a)], bf16, tm, tn, "a_out_dx")
    dw_out_a = _mm_tn(og, dmix, tm, tn, tn, "a_out_dw")
    dpq, dpf, dpi, dpg, dlb, dhg = _gla_bwd(pmat, states, dog, lb, head_gain, bl_, t, nm, "a_gla_bwd")
    dps = (dpq, dpf, dpi, dpg)
    dw_in = jnp.concatenate([_mm_tn(hn0, dp, tm, tn, tn, f"a_in_dw{j}") for j, dp in enumerate(dps)], axis=1)
    push4 = push_grads([shards(dw_out_a, 0), shards(dw_in, 1)], "grad_push_a")
    dhn0 = _mm_nt([(dp, w_in[:, j * d:(j + 1) * d]) for j, dp in enumerate(dps)], bf16, tm, tn, "a_in_dx")
    dh, dgain[0, 0] = _rms_bwd(h0, gain(0, 0), dhn0, dh, f32, tm, "a_norm_bwd")

    dh = dh.reshape(bl_, t, d)
    grad_x = dh[:, nm:]
    dl0 = dlb * lb * (1.0 - lb)
    vec_grads = dict(
        meta_tokens=jnp.sum(dh[:, :nm], axis=0),
        norm_gains=jnp.stack([jnp.concatenate([dgain[l, j] for j in range(4)], axis=0) for l in range(2)]),
        a_lb_logits=jnp.concatenate([dl0, -dl0], axis=0), a_head_norm=dhg, ffn_conv=jnp.stack([dconv0, dconv1]))
    vec_send = _pack([_shard8(vec_grads[k], _SHARD_AXIS[k]).reshape(N_DEV, -1) for k in _VECTORS], bf16, LANES, BF16_ROWS)
    push5 = push_grads([vec_send], "grad_push_vectors")

    g_s, d_s, m_s, v_s = {}, {}, {}, {}
    outs = (g_s, d_s, m_s, v_s)

    def update(part, srcs, label):
        rows = part.shape[1]
        return _adamw(part, *srcs, rows if rows <= _ROW_TILE_CAP else _div_tile(rows, _ROW_TILE_CAP), label)

    def update_matrix(k, part, layer=None):
        pick = (lambda a: a) if layer is None else (lambda a: a[layer])
        label = f"adamw_{k}" if layer is None else f"adamw_{k}{layer}"
        res = update(part, [_rows(pick(src[k])) for src in (w, mom, var)], label)
        return [r.reshape(pick(w[k]).shape) for r in res]

    def put(k, res):
        for dst, r in zip(outs, res):
            dst[k] = r

    up1, down1 = (update_matrix(k, p, 1) for k, p in zip(("ffn_w_up", "ffn_w_down"), landed(push1, gains_box[0], "grad_wait_ffn1")))
    for k, p in zip(("b_w_out", "b_w_q", "kv_w"), landed(push2, up1[0], "grad_wait_b")):
        put(k, update_matrix(k, p))
    up0, down0 = (update_matrix(k, p, 0) for k, p in zip(("ffn_w_up", "ffn_w_down"), landed(push3, g_s["kv_w"], "grad_wait_ffn0")))
    put("ffn_w_up", [jnp.stack(pair) for pair in zip(up0, up1)])
    put("ffn_w_down", [jnp.stack(pair) for pair in zip(down0, down1)])
    part_out_a, part_in = landed(push4, down0[0], "grad_wait_a")
    put("a_w_out", update_matrix("a_w_out", part_out_a))
    put("a_w_in", update_matrix("a_w_in", part_in))
    part_vec, = landed(push5, g_s["a_w_in"], "grad_wait_vectors")
    vec_packs = [_pack([src[k].reshape(-1) for k in _VECTORS], f32, LANES, BF16_ROWS) for src in (w, mom, var)]
    vec_shapes = [w[k].shape for k in _VECTORS]
    for dst, r in zip(outs, update(part_vec, vec_packs, "adamw_vectors")):
        dst.update(zip(_VECTORS, _unpack(r.reshape(-1), vec_shapes)))

    rep_local = _pack([dkvn.reshape(-1), dfgb[0, :bh]], f32, LANES, 8)
    rep_parts = _all_gather([rep_local], "gather_replicated_grads")[0]
    rpacks = [_pack([src[k].reshape(-1) for k in _REPLICATED], f32, LANES, 8) for src in (w, mom, var)]
    rres = _adamw(rep_parts, *rpacks, rep_local.shape[0], "adamw_replicated")
    rshapes = [w[k].shape for k in _REPLICATED]
    g_r, d_r, m_r, v_r = ({k: a for k, a in zip(_REPLICATED, _unpack(r.reshape(-1), rshapes))} for r in rres)

    out = [loss, grad_x]
    for sh, rp in ((g_s, g_r), (d_s, d_r), (m_s, m_r), (v_s, v_r)):
        out += [sh[k] if k in sh else rp[k] for k in names]
    return tuple(out)
```

```python
import functools
import math

import jax
import jax.numpy as jnp
from jax import lax
from jax.experimental import pallas as pl
from jax.experimental.pallas import tpu as pltpu

f32 = jnp.float32
bf16 = jnp.bfloat16
SDS = jax.ShapeDtypeStruct

EPS = 1e-6
A_DK = 128
A_CHUNK = 64
GLA_GROUP = 4
GLA_HEADS = 2
TOKEN_TILE_CAP = 1024
MODEL_TILE_CAP = 1024
LANES = 128
SUBLANES = 8
BF16_ROWS = 16
VMEM_LIMIT = 56 * 1024 * 1024
ADAM_LR, ADAM_B1, ADAM_B2, ADAM_EPS, ADAM_WD, ADAM_STEP = 0.001, 0.9, 0.999, 1e-08, 0.01, 10
N_DEV = 8
MESH = pl.DeviceIdType.MESH

_NT = (((1,), (1,)), ((), ()))
_TN = (((0,), (0,)), ((), ()))
_HI = lax.Precision.HIGHEST


def _params(**kw):
    return pltpu.CompilerParams(vmem_limit_bytes=VMEM_LIMIT, **kw)


def _div_tile(n, cap, mult=BF16_ROWS):
    best = None
    for t in range(mult, min(n, cap) + 1, mult):
        if n % t == 0:
            best = t
    assert best is not None, (n, cap, mult)
    return best


def _bdot(a, b):
    return jnp.dot(a.astype(bf16), b.astype(bf16), preferred_element_type=f32)


def _bdot_nt(a, b):
    return lax.dot_general(a.astype(bf16), b.astype(bf16), _NT, preferred_element_type=f32)


def _bdot_tn(a, b):
    return lax.dot_general(a.astype(bf16), b.astype(bf16), _TN, preferred_element_type=f32)


def _iota2(shape, axis):
    return lax.broadcasted_iota(jnp.int32, shape, axis)


def _cumsum_rows(x):
    n = x.shape[0]
    tri = (_iota2((n, n), 0) >= _iota2((n, n), 1)).astype(f32)
    return jnp.dot(tri, x, precision=_HI, preferred_element_type=f32)


def _revcumsum_rows(x):
    n = x.shape[0]
    tri = (_iota2((n, n), 1) >= _iota2((n, n), 0)).astype(f32)
    return jnp.dot(tri, x, precision=_HI, preferred_element_type=f32)


def _sigmoid(x):
    return 1.0 / (1.0 + jnp.exp(-x))


def _rms_fwd(x, g, tm, name):
    n, d = x.shape

    def body(x_ref, g_ref, o_ref):
        xv = x_ref[...]
        r = lax.rsqrt(jnp.mean(xv * xv, axis=-1, keepdims=True) + EPS)
        o_ref[...] = (xv * r * g_ref[...]).astype(o_ref.dtype)

    return pl.pallas_call(
        body, grid=(n // tm,), name=name,
        in_specs=[pl.BlockSpec((tm, d), lambda i: (i, 0)), pl.BlockSpec((1, d), lambda i: (0, 0))],
        out_specs=pl.BlockSpec((tm, d), lambda i: (i, 0)),
        out_shape=SDS((n, d), bf16), compiler_params=_params(),
    )(x, g)


def _mm(a, w, out_dtype, tm, tn, name):
    n, k = a.shape
    m = w.shape[1]

    def body(a_ref, w_ref, o_ref):
        o_ref[...] = _bdot(a_ref[...], w_ref[...]).astype(o_ref.dtype)

    return pl.pallas_call(
        body, grid=(m // tn, n // tm), name=name,
        in_specs=[pl.BlockSpec((tm, k), lambda j, i: (i, 0)), pl.BlockSpec((k, tn), lambda j, i: (0, j))],
        out_specs=pl.BlockSpec((tm, tn), lambda j, i: (i, j)),
        out_shape=SDS((n, m), out_dtype), compiler_params=_params(),
    )(a, w)


def _mm_norm_res(a, w, g, h, next_gains, tm, name):
    n, k = a.shape
    d = w.shape[1]
    nn = len(next_gains)

    def body(a_ref, w_ref, g_ref, h_ref, *rest):
        ng_refs, (hn_ref, mix_ref), out_refs = rest[:nn], rest[nn:nn + 2], rest[nn + 2:]
        mix = _bdot(a_ref[...], w_ref[...])
        r = lax.rsqrt(jnp.mean(mix * mix, axis=-1, keepdims=True) + EPS)
        mix_ref[...] = mix
        hn = h_ref[...] + mix * r * g_ref[...]
        hn_ref[...] = hn
        if nn:
            rn = lax.rsqrt(jnp.mean(hn * hn, axis=-1, keepdims=True) + EPS)
            for ng_ref, o_ref in zip(ng_refs, out_refs):
                o_ref[...] = (hn * rn * ng_ref[...]).astype(o_ref.dtype)

    row = pl.BlockSpec((tm, d), lambda i: (i, 0))
    vec = pl.BlockSpec((1, d), lambda i: (0, 0))
    return pl.pallas_call(
        body, grid=(n // tm,), name=name,
        in_specs=[pl.BlockSpec((tm, k), lambda i: (i, 0)), pl.BlockSpec((k, d), lambda i: (0, 0)), vec, row] + [vec] * nn,
        out_specs=[row] * (2 + nn),
        out_shape=[SDS((n, d), f32), SDS((n, d), f32)] + [SDS((n, d), bf16)] * nn, compiler_params=_params(),
    )(a, w, g, h, *next_gains)


def _rms_bwd(x, g, dy, dh_in, out_dtype, tm, name, then=None):
    n, d = x.shape
    has_add = dh_in is not None
    has_next = then is not None

    def norm_bwd(xv, gv, dyv):
        r = lax.rsqrt(jnp.mean(xv * xv, axis=-1, keepdims=True) + EPS)
        xr = xv * r
        gdy = dyv * gv
        return r * gdy - xr * (r * r) * jnp.mean(xv * gdy, axis=-1, keepdims=True), jnp.sum(dyv * xr, axis=0, keepdims=True)

    def body(*refs):
        refs = list(refs)
        x_ref, g_ref, dy_ref = refs[:3]
        dh_ref = refs[3] if has_add else None
        ins_end = 3 + has_add + 2 * has_next
        o_ref, dg_ref = refs[ins_end:ins_end + 2]
        dx, dg = norm_bwd(x_ref[...], g_ref[...], dy_ref[...].astype(f32))
        if has_add:
            dx = dx + dh_ref[...]
        o_ref[...] = dx.astype(o_ref.dtype)

        @pl.when(pl.program_id(0) == 0)
        def _():
            for ref in refs[ins_end + 1::2]:
                ref[...] = jnp.zeros_like(ref)

        dg_ref[...] += dg
        if has_next:
            x2_ref, g2_ref = refs[ins_end - 2:ins_end]
            o2_ref, dg2_ref = refs[ins_end + 2:]
            dx2, dg2 = norm_bwd(x2_ref[...], g2_ref[...], dx)
            o2_ref[...] = dx2.astype(o2_ref.dtype)
            dg2_ref[...] += dg2

    row = pl.BlockSpec((tm, d), lambda i: (i, 0))
    vec = pl.BlockSpec((1, d), lambda i: (0, 0))
    ins = [x, g, dy] + ([dh_in] if has_add else []) + (list(then) if has_next else [])
    return pl.pallas_call(
        body, grid=(n // tm,), name=name,
        in_specs=[row, vec, row] + ([row] if has_add else []) + ([row, vec] if has_next else []),
        out_specs=[row, vec] + ([row, vec] if has_next else []),
        out_shape=[SDS((n, d), out_dtype), SDS((1, d), f32)] + ([SDS((n, d), bf16), SDS((1, d), f32)] if has_next else []),
        compiler_params=_params(),
    )(*ins)


def _mm_nt(pairs, out_dtype, tm, tk, name):
    n = pairs[0][0].shape[0]
    k = pairs[0][1].shape[0]
    np_ = len(pairs)

    def body(*refs):
        o_ref = refs[-1]
        acc = None
        for p in range(np_):
            t = _bdot_nt(refs[2 * p][...], refs[2 * p + 1][...])
            acc = t if acc is None else acc + t
        o_ref[...] = acc.astype(o_ref.dtype)

    in_specs, ins = [], []
    for dy, w in pairs:
        m = dy.shape[1]
        in_specs += [pl.BlockSpec((tm, m), lambda j, i: (i, 0)), pl.BlockSpec((tk, m), lambda j, i: (j, 0))]
        ins += [dy, w]
    return pl.pallas_call(
        body, grid=(k // tk, n // tm), name=name, in_specs=in_specs,
        out_specs=pl.BlockSpec((tm, tk), lambda j, i: (i, j)),
        out_shape=SDS((n, k), out_dtype), compiler_params=_params(),
    )(*ins)


def _mm_tn(x, dy, tm, tk, tn, name):
    n, k = x.shape
    m = dy.shape[1]

    def body(x_ref, dy_ref, o_ref):
        @pl.when(pl.program_id(2) == 0)
        def _():
            o_ref[...] = jnp.zeros_like(o_ref)

        o_ref[...] += _bdot_tn(x_ref[...], dy_ref[...])

    return pl.pallas_call(
        body, grid=(k // tk, m // tn, n // tm), name=name,
        in_specs=[pl.BlockSpec((tm, tk), lambda a, b, i: (i, a)), pl.BlockSpec((tm, tn), lambda a, b, i: (i, b))],
        out_specs=pl.BlockSpec((tk, tn), lambda a, b, i: (a, b)),
        out_shape=SDS((k, m), f32), compiler_params=_params(),
    )(x, dy)


def _split3(x):
    hi = x.astype(bf16)
    r = x - hi.astype(f32)
    mid = r.astype(bf16)
    return hi, mid, (r - mid.astype(f32)).astype(bf16)


def _mask_dot(mask, x):
    hi, mid, lo = _split3(x)
    dot = lambda p: jnp.dot(mask, p, preferred_element_type=f32)
    return dot(hi) + dot(mid) + dot(lo)


_BNN = (((2,), (1,)), ((0,), (0,)))
_BNT = (((2,), (2,)), ((0,), (0,)))
_BTN = (((1,), (1,)), ((0,), (0,)))


def _hdot(a, b, dims):
    return lax.dot_general(a.astype(bf16), b.astype(bf16), dims, preferred_element_type=f32)


def _heads(x, nhb):
    return jnp.stack([x[:, h * A_DK:(h + 1) * A_DK] for h in range(nhb)])


def _mask_dot_heads(mask, x):
    return jnp.stack([_mask_dot(mask, x[h]) for h in range(x.shape[0])])


def _chunk_rows(parts, cl):
    tiles = [jnp.broadcast_to(p, (p.shape[0], cl, p.shape[2])) for p in parts]
    return tiles[0] if len(tiles) == 1 else jnp.concatenate(tiles, axis=1)


def _cat(parts):
    return parts[0] if len(parts) == 1 else jnp.concatenate(parts, axis=1)


def _gla_group_fwd(qg, fg, vg, lb, st, nc, cl):
    g = nc * cl
    sg = _sigmoid(fg)
    f = lb + (1.0 - lb) * sg
    k = 1.0 - f
    row, col = _iota2((g, g), 0), _iota2((g, g), 1)
    chunk_of = lambda idx: sum((idx >= u * cl).astype(jnp.int32) for u in range(1, nc)) if nc > 1 else 0
    same = chunk_of(row) == chunk_of(col) if nc > 1 else None
    causal = row >= col if nc == 1 else jnp.logical_and(same, row >= col)
    anti = col >= row if nc == 1 else jnp.logical_and(same, col >= row)
    b = _mask_dot_heads(causal.astype(bf16), jnp.log(f))
    bls = [b[:, (u + 1) * cl - 1:(u + 1) * cl, :] for u in range(nc)]
    ebls = [jnp.exp(x) for x in bls]
    e = jnp.exp(b)
    ei = jnp.exp(-b)
    eo = jnp.exp(_chunk_rows(bls, cl) - b)
    qi, ki, ko = qg * e, k * ei, k * eo
    att = jnp.where(causal[None], _hdot(qi, ki, _BNT), 0.0)
    o_intra = _hdot(att, vg, _BNN)
    sl = [slice(u * cl, (u + 1) * cl) for u in range(nc)]
    ds = [_hdot(vg[:, s], ko[:, s], _BTN) for s in sl]
    sts = [st]
    for u in range(nc):
        sts.append(sts[u] * ebls[u] + ds[u])
    o = o_intra + _cat([_hdot(qi[:, sl[u]], sts[u], _BNT) for u in range(nc)])
    return dict(sg=sg, f=f, e=e, ei=ei, eo=eo, ebls=ebls, qi=qi, ki=ki, ko=ko, att=att, o=o, sts=sts, causal=causal,
                anti=anti, sl=sl)


def _gla_group(nreal, want):
    while nreal % want:
        want //= 2
    return max(want, 1)


def _head_out(o, ggc, hg):
    r = lax.rsqrt(jnp.mean(o * o, axis=-1, keepdims=True) + EPS)
    return o * r * hg * (ggc * _sigmoid(ggc))


def _gla_fwd(pmat, lb, hg, bl_, t, nm, name):
    n, d4 = pmat.shape
    d = d4 // 4
    nh = d // A_DK
    nreal = (t - nm) // A_CHUNK
    nch = nreal + 1
    un = _gla_group(nreal, GLA_GROUP)
    hb = _gla_group(nh, GLA_HEADS)
    ng = nh // hb
    wide = hb * A_DK

    def body(q_ref, f_ref, i_ref, gg_ref, lb_ref, hg_ref, og_ref, ss_ref):
        lbv, hgv = _heads(lb_ref[...], hb), _heads(hg_ref[...], hb)
        take = lambda ref, rows: _heads(ref[rows, :], hb)

        def run(rows, st, idx, nc, cl):
            w = _gla_group_fwd(take(q_ref, rows), take(f_ref, rows), take(i_ref, rows), lbv, st, nc, cl)
            out = _head_out(w["o"], take(gg_ref, rows), hgv)
            for h in range(hb):
                for u in range(nc):
                    ss_ref[h, idx + u] = w["sts"][u][h]
                og_ref[rows, h * A_DK:(h + 1) * A_DK] = out[h].astype(og_ref.dtype)
            return w["sts"][nc]

        st = run(pl.ds(0, nm), jnp.zeros((hb, A_DK, A_DK), f32), 0, 1, nm)

        def step(it, st):
            rows = pl.ds(pl.multiple_of(nm + it * (un * A_CHUNK), BF16_ROWS), un * A_CHUNK)
            return run(rows, st, 1 + it * un, un, A_CHUNK)

        lax.fori_loop(0, nreal // un, step, st)

    col = lambda o: pl.BlockSpec((t, wide), lambda b, h: (b, o * ng + h))
    vec = pl.BlockSpec((1, wide), lambda b, h: (0, h))
    return pl.pallas_call(
        body, grid=(bl_, ng), name=name,
        in_specs=[col(0), col(1), col(2), col(3), vec, vec],
        out_specs=[pl.BlockSpec((t, wide), lambda b, h: (b, h)),
                   pl.BlockSpec((hb, nch, A_DK, A_DK), lambda b, h: (b * ng + h, 0, 0, 0))],
        out_shape=[SDS((n, d), bf16), SDS((bl_ * nh, nch, A_DK, A_DK), f32)], compiler_params=_params(),
    )(pmat, pmat, pmat, pmat, lb, hg)


def _gla_bwd(pmat, ss, dog, lb, hg, bl_, t, nm, name):
    n, d4 = pmat.shape
    d = d4 // 4
    nh = d // A_DK
    nreal = (t - nm) // A_CHUNK
    nch = nreal + 1
    un = _gla_group(nreal, GLA_GROUP)
    hb = _gla_group(nh, GLA_HEADS)
    ng = nh // hb
    wide = hb * A_DK

    def body(q_ref, f_ref, i_ref, gg_ref, ss_ref, dog_ref, lb_ref, hg_ref,
             dq_ref, df_ref, di_ref, dgg_ref, dlb_ref, dhg_ref):
        lbv, hgv = _heads(lb_ref[...], hb), _heads(hg_ref[...], hb)
        take = lambda ref, rows: _heads(ref[rows, :], hb)

        def put(ref, rows, val):
            for h in range(hb):
                ref[rows, h * A_DK:(h + 1) * A_DK] = val[h].astype(ref.dtype)

        def run(rows, idx, carry, nc, cl):
            dst, dlb, dhg = carry
            qg, fg, vg, ggc = take(q_ref, rows), take(f_ref, rows), take(i_ref, rows), take(gg_ref, rows)
            dogc = take(dog_ref, rows).astype(f32)
            st_in = jnp.stack([ss_ref[h, idx] for h in range(hb)])
            w = _gla_group_fwd(qg, fg, vg, lbv, st_in, nc, cl)
            o, qi, ki, ko, sl, sts, ebls = w["o"], w["qi"], w["ki"], w["ko"], w["sl"], w["sts"], w["ebls"]
            r = lax.rsqrt(jnp.mean(o * o, axis=-1, keepdims=True) + EPS)
            sgg = _sigmoid(ggc)
            sil = ggc * sgg
            on = o * r
            dhg = dhg + jnp.sum(dogc * sil * on, axis=1, keepdims=True)
            put(dgg_ref, rows, dogc * on * hgv * (sgg * (1.0 + ggc * (1.0 - sgg))))
            tt = dogc * sil * hgv
            do = r * tt - on * (r * r) * jnp.mean(o * tt, axis=-1, keepdims=True)
            xs = [_hdot(do[:, s], qi[:, s], _BTN) for s in sl]
            dsts = [None] * nc + [dst]
            for u in reversed(range(nc)):
                dsts[u] = dsts[u + 1] * ebls[u] + xs[u]
            datt = jnp.where(w["causal"][None], _hdot(do, vg, _BNT), 0.0)
            dv = _hdot(w["att"], do, _BTN) + _cat([_hdot(ko[:, sl[u]], dsts[u + 1], _BNT) for u in range(nc)])
            dko = _cat([_hdot(vg[:, sl[u]], dsts[u + 1], _BNN) for u in range(nc)])
            dqi = _hdot(datt, ki, _BNN) + _cat([_hdot(do[:, sl[u]], sts[u], _BNN) for u in range(nc)])
            dki = _hdot(datt, qi, _BTN)
            dk = dki * w["ei"] + dko * w["eo"]
            dkoko = dko * ko
            db = dqi * qi - dki * ki - dkoko
            rowi = lax.broadcasted_iota(jnp.int32, db.shape, 1)
            for u in range(nc):
                d_ebl = jnp.sum(dsts[u + 1] * sts[u], axis=1, keepdims=True)
                dbl = jnp.sum(dkoko[:, sl[u]], axis=1, keepdims=True) + d_ebl * ebls[u]
                db = db + jnp.where(rowi == (u + 1) * cl - 1, dbl, 0.0)
            dlogf = _mask_dot_heads(w["anti"].astype(bf16), db)
            df = dlogf / w["f"] - dk
            sg = w["sg"]
            put(dq_ref, rows, dqi * w["e"])
            put(df_ref, rows, df * (1.0 - lbv) * sg * (1.0 - sg))
            put(di_ref, rows, dv)
            dlb = dlb + jnp.sum(df * (1.0 - sg), axis=1, keepdims=True)
            return dsts[0], dlb, dhg

        zero = jnp.zeros((hb, 1, A_DK), f32)
        ngroups = nreal // un

        def step(it, carry):
            grp = ngroups - 1 - it
            rows = pl.ds(pl.multiple_of(nm + grp * (un * A_CHUNK), BF16_ROWS), un * A_CHUNK)
            return run(rows, 1 + grp * un, carry, un, A_CHUNK)

        carry = lax.fori_loop(0, ngroups, step, (jnp.zeros((hb, A_DK, A_DK), f32), zero, zero))
        _, dlb, dhg = run(pl.ds(0, nm), 0, carry, 1, nm)

        @pl.when(pl.program_id(1) == 0)
        def _():
            dlb_ref[...] = jnp.zeros_like(dlb_ref)
            dhg_ref[...] = jnp.zeros_like(dhg_ref)

        for h in range(hb):
            dlb_ref[:, h * A_DK:(h + 1) * A_DK] += dlb[h]
            dhg_ref[:, h * A_DK:(h + 1) * A_DK] += dhg[h]

    col = lambda o: pl.BlockSpec((t, wide), lambda h, b: (b, o * ng + h))
    blk = pl.BlockSpec((t, wide), lambda h, b: (b, h))
    vec = pl.BlockSpec((1, wide), lambda h, b: (0, h))
    return pl.pallas_call(
        body, grid=(ng, bl_), name=name,
        in_specs=[col(0), col(1), col(2), col(3),
                  pl.BlockSpec((hb, nch, A_DK, A_DK), lambda h, b: (b * ng + h, 0, 0, 0)), blk, vec, vec],
        out_specs=[blk, blk, blk, blk, vec, vec],
        out_shape=[SDS((n, d), bf16)] * 4 + [SDS((1, d), f32)] * 2, compiler_params=_params(),
    )(pmat, pmat, pmat, pmat, ss, dog, lb, hg)


def _shifted(x, halo, before):
    n = x.shape[0]
    both = jnp.concatenate([halo, x] if before else [x, halo], axis=0)
    row, col = _iota2((n, n + BF16_ROWS), 0), _iota2((n, n + BF16_ROWS), 1)
    src = row + BF16_ROWS if before else row
    step = -1 if before else 1
    pick = lambda s: jnp.dot((col == src + step * s).astype(bf16), both, preferred_element_type=f32)
    return pick(1), pick(2)


def _conv3(xb, halo, w):
    x = xb.astype(f32)
    x1, x2 = _shifted(xb, halo, True)
    return x, x1, x2, w[0:1, :] * x2 + w[1:2, :] * x1 + w[2:3, :] * x


def _conv_gate_fwd(ug, uv, cwg, cwv, bl_, t, tc, name):
    n, ff = ug.shape
    nt = t // tc

    def body(ug_ref, uv_ref, wg_ref, wv_ref, a_ref, hg_ref, hv_ref):
        @pl.when(pl.program_id(1) == 0)
        def _():
            hg_ref[...] = jnp.zeros_like(hg_ref)
            hv_ref[...] = jnp.zeros_like(hv_ref)

        xg, xv = ug_ref[...], uv_ref[...]
        cg = _conv3(xg, hg_ref[...], wg_ref[...])[3]
        cv = _conv3(xv, hv_ref[...], wv_ref[...])[3]
        a_ref[...] = (cg * _sigmoid(cg) * cv).astype(a_ref.dtype)
        hg_ref[...] = xg[tc - BF16_ROWS:tc, :].astype(hg_ref.dtype)
        hv_ref[...] = xv[tc - BF16_ROWS:tc, :].astype(hv_ref.dtype)

    row = pl.BlockSpec((tc, ff), lambda b, i: (b * nt + i, 0))
    wsp = pl.BlockSpec((3, ff), lambda b, i: (0, 0))
    return pl.pallas_call(
        body, grid=(bl_, nt), name=name, in_specs=[row, row, wsp, wsp], out_specs=row,
        out_shape=SDS((n, ff), bf16),
        scratch_shapes=[pltpu.VMEM((BF16_ROWS, ff), bf16), pltpu.VMEM((BF16_ROWS, ff), bf16)], compiler_params=_params(),
    )(ug, uv, cwg, cwv)


def _conv_gate_bwd(ug, uv, cwg, cwv, da, bl_, t, tc, name):
    n, ff = ug.shape
    nt = t // tc
    per = tc // BF16_ROWS

    def body(ug_ref, uv_ref, pg_ref, pv_ref, wg_ref, wv_ref, da_ref, dug_ref, duv_ref, dwg_ref, dwv_ref, ng_ref, nv_ref):
        first = jnp.logical_and(pl.program_id(0) == 0, pl.program_id(1) == 0)

        @pl.when(first)
        def _():
            dwg_ref[...] = jnp.zeros_like(dwg_ref)
            dwv_ref[...] = jnp.zeros_like(dwv_ref)

        @pl.when(pl.program_id(1) == 0)
        def _():
            ng_ref[...] = jnp.zeros_like(ng_ref)
            nv_ref[...] = jnp.zeros_like(nv_ref)

        seq_start = pl.program_id(1) == nt - 1
        dav = da_ref[...].astype(f32)

        def half(u_ref, p_ref, w_ref):
            halo = p_ref[...]
            return _conv3(u_ref[...], jnp.where(seq_start, jnp.zeros_like(halo), halo), w_ref[...])

        xg, xg1, xg2, cg = half(ug_ref, pg_ref, wg_ref)
        xv, xv1, xv2, cv = half(uv_ref, pv_ref, wv_ref)
        sg = _sigmoid(cg)
        dcg = dav * cv * (sg * (1.0 + cg * (1.0 - sg)))
        dcv = dav * (cg * sg)

        def back(dc, x, x1, x2, w_ref, nx_ref, du_ref, dw_ref):
            w = w_ref[...]
            dcb = dc.astype(bf16)
            dc1, dc2 = _shifted(dcb, nx_ref[...], False)
            du = w[2:3, :] * dc + w[1:2, :] * dc1 + w[0:1, :] * dc2
            du_ref[...] = du.astype(du_ref.dtype)
            dw_ref[0:1, :] += jnp.sum(dc * x2, axis=0, keepdims=True)
            dw_ref[1:2, :] += jnp.sum(dc * x1, axis=0, keepdims=True)
            dw_ref[2:3, :] += jnp.sum(dc * x, axis=0, keepdims=True)
            nx_ref[...] = dcb[0:BF16_ROWS, :].astype(nx_ref.dtype)

        back(dcg, xg, xg1, xg2, wg_ref, ng_ref, dug_ref, dwg_ref)
        back(dcv, xv, xv1, xv2, wv_ref, nv_ref, duv_ref, dwv_ref)

    row = pl.BlockSpec((tc, ff), lambda b, i: (b * nt + nt - 1 - i, 0))
    prev = pl.BlockSpec((BF16_ROWS, ff), lambda b, i: (jnp.maximum((b * nt + nt - 1 - i) * per - 1, 0), 0))
    wsp = pl.BlockSpec((3, ff), lambda b, i: (0, 0))
    return pl.pallas_call(
        body, grid=(bl_, nt), name=name, in_specs=[row, row, prev, prev, wsp, wsp, row],
        out_specs=[row, row, wsp, wsp],
        out_shape=[SDS((n, ff), bf16), SDS((n, ff), bf16), SDS((3, ff), f32), SDS((3, ff), f32)],
        scratch_shapes=[pltpu.VMEM((BF16_ROWS, ff), bf16), pltpu.VMEM((BF16_ROWS, ff), bf16)], compiler_params=_params(),
    )(ug, uv, ug, uv, cwg, cwv, da)


def _zf_c(hk, wzf, fgb, bl_, t, tm, name):
    n, d = hk.shape
    nt = t // tm

    def body(hk_ref, w_ref, b_ref, zf_ref, c_ref, carry_ref):
        @pl.when(pl.program_id(1) == 0)
        def _():
            carry_ref[...] = jnp.zeros_like(carry_ref)

        z = _bdot(hk_ref[...], w_ref[...]) + b_ref[...]
        ls = jnp.minimum(z, 0.0) - jnp.log(1.0 + jnp.exp(-jnp.abs(z)))
        c = _cumsum_rows(ls) + carry_ref[...]
        zf_ref[...] = z
        c_ref[...] = c
        carry_ref[...] = c[tm - 1:tm, :]

    row = lambda w: pl.BlockSpec((tm, w), lambda b, i: (b * nt + i, 0))
    return pl.pallas_call(
        body, grid=(bl_, nt), name=name,
        in_specs=[row(d), pl.BlockSpec((d, LANES), lambda b, i: (0, 0)), pl.BlockSpec((1, LANES), lambda b, i: (0, 0))],
        out_specs=[row(LANES), row(LANES)],
        out_shape=[SDS((n, LANES), f32), SDS((n, LANES), f32)],
        scratch_shapes=[pltpu.VMEM((1, LANES), f32)], compiler_params=_params(),
    )(hk, wzf, fgb)


def _c_bwd(dc, zf, bl_, t, tm, name):
    n = dc.shape[0]
    nt = t // tm

    def body(dc_ref, zf_ref, dzf_ref, dfg_ref, carry_ref):
        @pl.when(jnp.logical_and(pl.program_id(0) == 0, pl.program_id(1) == 0))
        def _():
            dfg_ref[...] = jnp.zeros_like(dfg_ref)

        @pl.when(pl.program_id(1) == 0)
        def _():
            carry_ref[...] = jnp.zeros_like(carry_ref)

        rc = _revcumsum_rows(dc_ref[...]) + carry_ref[...]
        dz = rc * _sigmoid(-zf_ref[...])
        dzf_ref[...] = dz.astype(dzf_ref.dtype)
        dfg_ref[...] += jnp.sum(dz, axis=0, keepdims=True)
        carry_ref[...] = rc[0:1, :]

    row = pl.BlockSpec((tm, LANES), lambda b, i: (b * nt + nt - 1 - i, 0))
    vec = pl.BlockSpec((1, LANES), lambda b, i: (0, 0))
    return pl.pallas_call(
        body, grid=(bl_, nt), name=name, in_specs=[row, row], out_specs=[row, vec],
        out_shape=[SDS((n, LANES), bf16), SDS((1, LANES), f32)],
        scratch_shapes=[pltpu.VMEM((1, LANES), f32)], compiler_params=_params(),
    )(dc, zf)


def _is_pow2(x):
    m, _ = math.frexp(x)
    return m == 0.5


def _prescale(qh, scale):
    return (qh.astype(f32) * scale).astype(bf16)


def _attn_fwd(q, kv, ck, bl_, t, tq, hd, name):
    n, d = q.shape
    npair = d // LANES
    hp = LANES // hd
    nq = t // tq
    scale = 1.0 / (hd ** 0.5)

    pre = _is_pow2(scale)

    def body(q_ref, k_ref, v_ref, ck_ref, o_ref, lse_ref):
        i = pl.program_id(2)
        diag = _iota2((tq, tq), 0) >= _iota2((tq, tq), 1)
        for hh in range(hp):
            lanes = slice(hh * hd, (hh + 1) * hd)
            qh = _prescale(q_ref[:, lanes], scale) if pre else q_ref[:, lanes]

            def block(j, carry, masked, lanes=lanes, qh=qh, hh=hh):
                m, l, acc = carry
                rows = pl.ds(pl.multiple_of(j * tq, BF16_ROWS), tq)
                s = _bdot_nt(qh, k_ref[rows, lanes])
                s = (s if pre else s * scale) - ck_ref[0, 0, j, hh:hh + 1, :]
                if masked:
                    s = jnp.where(diag, s, -1e30)
                m2 = jnp.maximum(m, jnp.max(s, axis=-1, keepdims=True))
                p = jnp.exp(s - m2)
                a = jnp.exp(m - m2)
                return m2, a * l + jnp.sum(p, axis=-1, keepdims=True), a * acc + _bdot(p, v_ref[rows, lanes])

            init = (jnp.full((tq, 1), -1e30, f32), jnp.zeros((tq, 1), f32), jnp.zeros((tq, hd), f32))
            carry = lax.fori_loop(0, i, functools.partial(block, masked=False), init)
            m, l, acc = block(i, carry, True)
            o_ref[:, lanes] = (acc / l).astype(o_ref.dtype)
            lse_ref[:, lanes] = jnp.broadcast_to(m + jnp.log(l), (tq, hd))

    nk = nq
    return pl.pallas_call(
        body, grid=(bl_, npair, nq), name=name,
        in_specs=[pl.BlockSpec((tq, LANES), lambda b, p, i: (b * nq + i, p)),
                  pl.BlockSpec((t, LANES), lambda b, p, i: (b, p)),
                  pl.BlockSpec((t, LANES), lambda b, p, i: (b, npair + p)),
                  pl.BlockSpec((1, 1, nk, hp, tq), lambda b, p, i: (b, p, 0, 0, 0))],
        out_specs=[pl.BlockSpec((tq, LANES), lambda b, p, i: (b * nq + i, p)),
                   pl.BlockSpec((tq, LANES), lambda b, p, i: (b * nq + i, p))],
        out_shape=[SDS((n, d), f32), SDS((n, d), f32)], compiler_params=_params(),
    )(q, kv, kv, ck)


def _attn_bwd(q, kv, o, do, lse, ck, bl_, t, tq, hd, name):
    n, d = q.shape
    npair = d // LANES
    hp = LANES // hd
    nq = t // tq
    scale = 1.0 / (hd ** 0.5)

    pre = _is_pow2(scale)

    def body(q_ref, k_ref, v_ref, o_ref, do_ref, lse_ref, ck_ref, dq_ref, dk_ref, dv_ref, dck_ref, dcq_ref):
        j = pl.program_id(2)

        @pl.when(j == 0)
        def _():
            dq_ref[...] = jnp.zeros_like(dq_ref)
            dcq_ref[...] = jnp.zeros_like(dcq_ref)

        diag = _iota2((tq, tq), 0) >= _iota2((tq, tq), 1)
        for hh in range(hp):
            lanes = slice(hh * hd, (hh + 1) * hd)
            kh = k_ref[:, lanes]
            vh = v_ref[:, lanes]
            kt = kh.astype(f32).T.astype(bf16)
            cs = ck_ref[0, 0, 0, hh:hh + 1, :]

            def block(i, carry, masked, lanes=lanes, kh=kh, vh=vh, kt=kt, cs=cs, hh=hh):
                dkt, dvt, dcs = carry
                rows = pl.ds(pl.multiple_of(i * tq, BF16_ROWS), tq)
                qh = _prescale(q_ref[rows, lanes], scale) if pre else q_ref[rows, lanes]
                doh = do_ref[rows, lanes]
                s = _bdot_nt(qh, kh)
                s = (s if pre else s * scale) - cs
                if masked:
                    s = jnp.where(diag, s, -1e30)
                p = jnp.exp(s - lse_ref[rows, hh * hd:hh * hd + 1])
                delta = jnp.sum(doh.astype(f32) * o_ref[rows, lanes].astype(f32), axis=-1, keepdims=True)
                ds = p * (_bdot_nt(doh, vh) - delta)
                dsb = ds.astype(bf16)
                dq_ref[rows, lanes] += _bdot_nt(kt, dsb).T * scale
                dcq_ref[0, rows, hh:hh + 1] += jnp.sum(ds, axis=-1, keepdims=True)
                dkq = _bdot_tn(qh, dsb)
                return (dkt + (dkq if pre else dkq * scale), dvt + _bdot_tn(doh, p), dcs - jnp.sum(ds, axis=0, keepdims=True))

            init = (jnp.zeros((hd, tq), f32), jnp.zeros((hd, tq), f32), jnp.zeros((1, tq), f32))
            dkt, dvt, dcs = lax.fori_loop(j + 1, nq, functools.partial(block, masked=False), block(j, init, True))
            dk_ref[:, lanes] = dkt.T.astype(dk_ref.dtype)
            dv_ref[:, lanes] = dvt.T.astype(dv_ref.dtype)
            dck_ref[0, 0, 0, hh:hh + 1, :] = dcs

    whole = lambda c0: pl.BlockSpec((t, LANES), lambda b, p, j: (b, c0 + p))
    tile = lambda c0: pl.BlockSpec((tq, LANES), lambda b, p, j: (b * nq + j, c0 + p))
    ckspec = pl.BlockSpec((1, 1, 1, hp, tq), lambda b, p, j: (b, p, j, 0, 0))
    cqspec = pl.BlockSpec((1, t, hp), lambda b, p, j: (p, b, 0))
    return pl.pallas_call(
        body, grid=(bl_, npair, nq), name=name,
        in_specs=[whole(0), tile(0), tile(npair), whole(0), whole(0), whole(0), ckspec],
        out_specs=[whole(0), tile(0), tile(0), ckspec, cqspec],
        out_shape=[SDS((n, d), f32), SDS((n, d), bf16), SDS((n, d), bf16), SDS((bl_, npair, nq, hp, tq), f32),
                   SDS((npair, n, hp), f32)],
        compiler_params=_params(),
    )(q, kv, kv, o, do, lse, ck)


def _loss_head(h, target, t, nm, tm, name):
    n, d = h.shape
    nt = t // tm

    def body(h_ref, t_ref, loss_ref, dh_ref):
        i = pl.program_id(0)

        @pl.when(i == 0)
        def _():
            loss_ref[...] = jnp.zeros_like(loss_ref)

        pos = (i % nt) * tm + _iota2((tm, d), 0)
        err = jnp.where(pos >= nm, h_ref[...] - t_ref[...], 0.0)
        dh_ref[...] = err * (1.0 / d)
        loss_ref[...] += 0.5 * jnp.sum(jnp.mean(err * err, axis=-1, keepdims=True))

    row = pl.BlockSpec((tm, d), lambda i: (i, 0))
    return pl.pallas_call(
        body, grid=(n // tm,), name=name, in_specs=[row, row],
        out_specs=[pl.BlockSpec((8, LANES), lambda i: (0, 0)), row],
        out_shape=[SDS((8, LANES), f32), SDS((n, d), f32)], compiler_params=_params(),
    )(h, target)


def _c_key_rows(c, bl_, t, tq, bh, hp):
    npair = bh // hp
    nk = t // tq
    return c[:, :bh].reshape(bl_, nk, tq, npair, hp).transpose(0, 3, 1, 4, 2)


def _dc_rows(dck, dcq, bl_, t, bh):
    d = dck.transpose(0, 2, 4, 1, 3).reshape(bl_ * t, bh) + dcq.transpose(1, 0, 2).reshape(bl_ * t, bh)
    return jnp.pad(d, ((0, 0), (0, LANES - bh)))


_ANY = pl.BlockSpec(memory_space=pl.ANY)


def _all_gather(xs, name):
    na = len(xs)

    def body(*refs):
        x_refs, out_refs = refs[:na], refs[na:2 * na]
        send_sems, recv_sems, local_sems = refs[2 * na:]
        mx, my, mc = lax.axis_index("x"), lax.axis_index("y"), lax.axis_index("c")
        me, sibling = (mx, my, mc), (mx, my, 1 - mc)
        chips = [(1 - mx, my), (mx, 1 - my), (1 - mx, 1 - my)]

        def copy(a, k, block, to, own=False):
            px, py, pc = block
            rows = out_refs[a].at[4 * px + 2 * py + pc]
            return pltpu.make_async_remote_copy(
                src_ref=x_refs[a] if own else rows, dst_ref=rows,
                send_sem=send_sems.at[a, k], recv_sem=recv_sems.at[a, k], device_id=to, device_id_type=MESH)

        arrays = range(na)
        mine = [pltpu.make_async_copy(x_refs[a], out_refs[a].at[4 * mx + 2 * my + mc], local_sems.at[a]) for a in arrays]
        for cp in mine:
            cp.start()
        first = [copy(a, 1 + j, me, (*chip, mc), own=True) for j, chip in enumerate(chips) for a in arrays]
        first += [copy(a, 0, me, sibling, own=True) for a in arrays]
        for cp in first:
            cp.start()
        passed = []
        for j, chip in enumerate(chips):
            for a in arrays:
                copy(a, 1 + j, (*chip, mc), me).wait_recv()
                cp = copy(a, 4 + j, (*chip, mc), sibling)
                cp.start()
                passed.append(cp)
        for a in arrays:
            copy(a, 0, sibling, me).wait_recv()
        for j, chip in enumerate(chips):
            for a in arrays:
                copy(a, 4 + j, (*chip, 1 - mc), me).wait_recv()
        for cp in first + passed:
            cp.wait_send()
        for cp in mine:
            cp.wait()

    return pl.pallas_call(
        body, name=name, out_shape=[SDS((N_DEV,) + x.shape, x.dtype) for x in xs],
        in_specs=[_ANY] * na, out_specs=[_ANY] * na,
        scratch_shapes=[pltpu.SemaphoreType.DMA((na, 7)), pltpu.SemaphoreType.DMA((na, 7)), pltpu.SemaphoreType.DMA((na,))],
    )(*xs)


_HBM = pl.BlockSpec(memory_space=pltpu.HBM)
_SEM = pl.BlockSpec(memory_space=pltpu.SEMAPHORE)
_DATAFLOW = pltpu.SideEffectType.DATAFLOW_SIDE_EFFECTING
N_PEERS = N_DEV - 1


def _device_index():
    return 4 * lax.axis_index("x") + 2 * lax.axis_index("y") + lax.axis_index("c")


def _peers():
    mx, my, mc = lax.axis_index("x"), lax.axis_index("y"), lax.axis_index("c")
    peers = []
    for r in (2, 3, 4, 5, 6, 7, 1):
        px = 1 - mx if r & 4 else mx
        py = 1 - my if r & 2 else my
        pc = 1 - mc if r & 1 else mc
        peers.append(((px, py, pc), 4 * px + 2 * py + pc))
    return 4 * mx + 2 * my + mc, peers


def _push_copy(src_ref, land_ref, send_sems, recv_sems, a, k, dev, src_row, land_row, scatter):
    return pltpu.make_async_remote_copy(
        src_ref=src_ref.at[src_row] if scatter else src_ref, dst_ref=land_ref.at[land_row],
        send_sem=send_sems.at[a * N_PEERS + k], recv_sem=recv_sems.at[a * N_PEERS + k], device_id=dev, device_id_type=MESH)


def _landing(own, me):
    return lax.dynamic_update_index_in_dim(lax.empty((N_DEV,) + own.shape, own.dtype), own, me, 0)


def _push_start(srcs, lands, scatter, name):
    na = len(srcs)

    def body(*refs):
        src_refs, land_refs = refs[:na], refs[na:2 * na]
        send_sems, recv_sems = refs[2 * na], refs[2 * na + 1]
        token = refs[-1]
        me, peers = _peers()
        for a in range(na):
            for k, (dev, idx) in enumerate(peers):
                _push_copy(src_refs[a], land_refs[a], send_sems, recv_sems, a, k, dev, idx, me, scatter).start()
        token[...] = jnp.zeros_like(token)

    hbm = lambda arrs: [pltpu.HBM(a.shape, a.dtype) for a in arrs]
    out = pl.pallas_call(
        body, name=name,
        out_shape=(pltpu.SemaphoreType.DMA((na * N_PEERS,)), pltpu.SemaphoreType.DMA((na * N_PEERS,)), *hbm(srcs), *hbm(lands),
                   SDS((8, LANES), f32)),
        in_specs=[_HBM] * (2 * na),
        out_specs=(_SEM, _SEM, *([_HBM] * (2 * na)), pl.BlockSpec(memory_space=pltpu.VMEM)),
        input_output_aliases={i: 2 + i for i in range(2 * na)},
        compiler_params=pltpu.CompilerParams(has_side_effects=_DATAFLOW),
    )(*[pltpu.with_memory_space_constraint(a, pltpu.HBM) for a in list(srcs) + list(lands)])
    return out[0], out[1], list(out[2:2 + na]), list(out[2 + na:2 + 2 * na]), out[-1]


def _push_wait(send_sems, recv_sems, srcs, lands, which, after, scatter, name):
    nw = len(which)

    def body(*refs):
        src_refs, land_refs = refs[:nw], refs[nw:2 * nw]
        send_sems_, recv_sems_ = refs[2 * nw], refs[2 * nw + 1]
        _, peers = _peers()
        for j, a in enumerate(which):
            for k, (dev, idx) in enumerate(peers):
                cp = _push_copy(src_refs[j], land_refs[j], send_sems_, recv_sems_, a, k, dev, idx, idx, scatter)
                cp.wait_send()
                cp.wait_recv()

    hbm = lambda arrs: [pltpu.HBM(a.shape, a.dtype) for a in arrs]
    out = pl.pallas_call(
        body, name=name, out_shape=(*hbm(srcs), *hbm(lands)),
        in_specs=[_HBM] * (2 * nw) + [_SEM, _SEM, _ANY], out_specs=[_HBM] * (2 * nw),
        input_output_aliases={i: i for i in range(2 * nw)},
        compiler_params=pltpu.CompilerParams(has_side_effects=_DATAFLOW),
    )(*srcs, *lands, send_sems, recv_sems, after)
    return list(out[nw:])


def _adamw(parts, w, m, v, tr, name):
    g, r, c = parts.shape

    def body(p_ref, w_ref, m_ref, v_ref, g_ref, d_ref, m2_ref, v2_ref):
        gr = p_ref[0].astype(f32)
        for k in range(1, g):
            gr = gr + p_ref[k].astype(f32)
        m2 = ADAM_B1 * m_ref[...] + (1.0 - ADAM_B1) * gr
        v2 = ADAM_B2 * v_ref[...] + (1.0 - ADAM_B2) * (gr * gr)
        m_hat = m2 / (1.0 - ADAM_B1 ** ADAM_STEP)
        v_hat = v2 / (1.0 - ADAM_B2 ** ADAM_STEP)
        g_ref[...] = gr
        d_ref[...] = -ADAM_LR * (m_hat / (jnp.sqrt(v_hat) + ADAM_EPS) + ADAM_WD * w_ref[...])
        m2_ref[...] = m2
        v2_ref[...] = v2

    row = pl.BlockSpec((tr, c), lambda i: (i, 0))
    return pl.pallas_call(
        body, grid=(r // tr,), name=name, in_specs=[pl.BlockSpec((g, tr, c), lambda i: (0, i, 0)), row, row, row],
        out_specs=[row] * 4, out_shape=[SDS((r, c), f32)] * 4, compiler_params=_params(),
    )(parts, w, m, v)


_SHARD_AXIS = dict(meta_tokens=1, norm_gains=2, a_w_in=2, a_lb_logits=1, a_head_norm=1, a_w_out=1, kv_w=1,
                   b_w_q=1, b_w_out=1, ffn_w_up=2, ffn_conv=2, ffn_w_down=1)
_VECTORS = ("meta_tokens", "norm_gains", "a_lb_logits", "a_head_norm", "ffn_conv")
_REPLICATED = ("kv_norm", "fg_b")
_ROW_TILE_CAP = 512


def _pack(arrs, dtype, cols, row_mult):
    lead = arrs[0].shape[:-1] if arrs[0].ndim > 1 else ()
    flat = jnp.concatenate([a.astype(dtype) for a in arrs], axis=-1)
    size = flat.shape[-1]
    per = cols * row_mult
    total = -(-size // per) * per
    flat = jnp.pad(flat, [(0, 0)] * len(lead) + [(0, total - size)])
    return flat.reshape(lead + (total // cols, cols))


def _unpack(flat, shapes):
    out, off = [], 0
    lead = flat.shape[:-1]
    for shp in shapes:
        size = 1
        for s in shp:
            size *= s
        out.append(flat[..., off:off + size].reshape(lead + tuple(shp)))
        off += size
    return out


def _unshard(seg, axis):
    a = jnp.moveaxis(seg, 0, axis)
    shp = a.shape
    return a.reshape(shp[:axis] + (shp[axis] * shp[axis + 1],) + shp[axis + 2:])


def _shard8(full, axis):
    shp = full.shape
    a = full.reshape(shp[:axis] + (N_DEV, shp[axis] // N_DEV) + shp[axis + 1:])
    return jnp.moveaxis(a, axis, 0)


def _rows(a, lead=0):
    return a.reshape(a.shape[:lead] + (-1, a.shape[-1]))


def kernel(x, meta_tokens, norm_gains, a_w_in, a_lb_logits, a_head_norm, a_w_out, kv_norm, kv_w, fg_b, b_w_q, b_w_out, ffn_w_up, ffn_conv, ffn_w_down, loss_target, m_meta_tokens, m_norm_gains, m_a_w_in, m_a_lb_logits, m_a_head_norm, m_a_w_out, m_kv_norm, m_kv_w, m_fg_b, m_b_w_q, m_b_w_out, m_ffn_w_up, m_ffn_conv, m_ffn_w_down, v_meta_tokens, v_norm_gains, v_a_w_in, v_a_lb_logits, v_a_head_norm, v_a_w_out, v_kv_norm, v_kv_w, v_fg_b, v_b_w_q, v_b_w_out, v_ffn_w_up, v_ffn_conv, v_ffn_w_down):
    names = ("meta_tokens", "norm_gains", "a_w_in", "a_lb_logits", "a_head_norm", "a_w_out", "kv_norm", "kv_w", "fg_b",
             "b_w_q", "b_w_out", "ffn_w_up", "ffn_conv", "ffn_w_down")
    w = dict(zip(names, (meta_tokens, norm_gains, a_w_in, a_lb_logits, a_head_norm, a_w_out, kv_norm, kv_w, fg_b,
                         b_w_q, b_w_out, ffn_w_up, ffn_conv, ffn_w_down)))
    mom = dict(zip(names, (m_meta_tokens, m_norm_gains, m_a_w_in, m_a_lb_logits, m_a_head_norm, m_a_w_out, m_kv_norm,
                           m_kv_w, m_fg_b, m_b_w_q, m_b_w_out, m_ffn_w_up, m_ffn_conv, m_ffn_w_down)))
    var = dict(zip(names, (v_meta_tokens, v_norm_gains, v_a_w_in, v_a_lb_logits, v_a_head_norm, v_a_w_out, v_kv_norm,
                           v_kv_w, v_fg_b, v_b_w_q, v_b_w_out, v_ffn_w_up, v_ffn_conv, v_ffn_w_down)))

    bl_, seq, d = x.shape
    nm = meta_tokens.shape[0]
    t = nm + seq
    n = bl_ * t
    bh = fg_b.shape[0]
    hd = d // bh
    hp = LANES // hd
    ff = ffn_w_down.shape[1] * N_DEV
    tm = _div_tile(t, TOKEN_TILE_CAP)
    tc = _div_tile(t, 64)
    tn = min(d, MODEL_TILE_CAP)

    vec_pack = _pack([w[k].reshape(-1) for k in _VECTORS], f32, LANES, 8)
    first = _all_gather([w["a_w_in"].astype(bf16), vec_pack], "gather_first")
    vec_segs = _unpack(first[1].reshape(N_DEV, -1), [w[k].shape for k in _VECTORS])
    small = {k: _unshard(a, _SHARD_AXIS[k]) for k, a in zip(_VECTORS, vec_segs)}
    w_in = _unshard(first[0], _SHARD_AXIS["a_w_in"])[0]
    me = _device_index()
    later_names = ("a_w_out", "ffn_w_up", "ffn_w_down", "kv_w", "b_w_q", "b_w_out", "ffn_w_up", "ffn_w_down")
    later_layer = (None, 0, 0, None, None, None, 1, 1)
    later = [(w[k] if l is None else w[k][l]).astype(bf16) for k, l in zip(later_names, later_layer)]
    later, _ = lax.optimization_barrier((later, first[1]))
    g_send, g_recv, later_src, later_land, _ = _push_start(later, [_landing(a, me) for a in later], False, "gather_rest_start")

    def gathered(which, after, name):
        lands = _push_wait(g_send, g_recv, [later_src[i] for i in which], [later_land[i] for i in which], which, after,
                           False, name)
        return [_unshard(a, _SHARD_AXIS[later_names[i]] - (later_layer[i] is not None)) for i, a in zip(which, lands)]

    gains_box = [small["norm_gains"]]
    gain = lambda l, j: gains_box[0][l, j][None]
    cw_gate, cw_val = small["ffn_conv"][:, :, :ff], small["ffn_conv"][:, :, ff:]
    head_gain = small["a_head_norm"]
    lb = jax.nn.softmax(small["a_lb_logits"], axis=0)[0:1]
    kvn = kv_norm[None]
    fgb_pad = jnp.pad(fg_b, (0, LANES - bh))[None]

    h0 = jnp.concatenate([jnp.broadcast_to(small["meta_tokens"][None], (bl_, nm, d)), x], axis=1).reshape(n, d)

    def ffn_fwd(l, h_in, fi, next_gains):
        ug = _mm(fi, w_gate[l], bf16, tm, ff, f"ffn{l}_up_gate")
        uv = _mm(fi, w_val[l], bf16, tm, ff, f"ffn{l}_up_val")
        act = _conv_gate_fwd(ug, uv, cw_gate[l], cw_val[l], bl_, t, tc, f"ffn{l}_conv_gate")
        h_out, mix, *normed = _mm_norm_res(act, w_down[l], gain(l, 3), h_in, next_gains, tm, f"ffn{l}_down")
        return h_out, (h_in, fi, ug, uv, act, mix), normed

    hn0 = _rms_fwd(h0, gain(0, 0), tm, "a_norm")
    pmat = _mm(hn0, w_in, f32, tm, tn, "a_in_proj")
    og, states = _gla_fwd(pmat, lb, head_gain, bl_, t, nm, "a_gla_fwd")
    w_out_a = gathered((0,), og, "gather_wait_a")[0][0]
    h1, mix_a, fi0 = _mm_norm_res(og, w_out_a, gain(0, 1), h0, [gain(0, 2)], tm, "a_out_proj")
    w_gate, w_val, w_down = {}, {}, {}

    def ffn_weights(l, which, after):
        w_up, w_down[l] = gathered(which, after, f"gather_wait_ffn{l}")
        w_gate[l], w_val[l] = w_up[:, :ff], w_up[:, ff:]

    ffn_weights(0, (1, 2), h1)
    h2, ffn0, (hk, hn1) = ffn_fwd(0, h1, fi0, [kvn, gain(1, 0)])

    w_kv_zf, w_q, w_out_b = gathered((3, 4, 5), h2, "gather_wait_b")
    w_kv, w_zf = w_kv_zf[:, :2 * d], jnp.pad(w_kv_zf[:, 2 * d:], ((0, 0), (0, LANES - bh)))
    w_q, w_out_b = w_q[0], w_out_b[0]
    kvp = _mm(hk, w_kv, bf16, tm, tn, "kv_proj")
    zf, cum = _zf_c(hk, w_zf, fgb_pad, bl_, t, tm, "forget_cumsum")
    ck = _c_key_rows(cum, bl_, t, tm, bh, hp)
    q = _mm(hn1, w_q, bf16, tm, tn, "b_q_proj")
    o, lse = _attn_fwd(q, kvp, ck, bl_, t, tm, hd, "b_attn_fwd")
    h3, mix_b, fi1 = _mm_norm_res(o, w_out_b, gain(1, 1), h2, [gain(1, 2)], tm, "b_out_proj")
    ffn_weights(1, (6, 7), h3)
    h4, ffn1, _ = ffn_fwd(1, h3, fi1, [])

    target = jnp.concatenate([jnp.zeros((bl_, nm, d), f32), loss_target], axis=1).reshape(n, d)
    loss8, dh = _loss_head(h4, target, t, nm, tm, "loss_head")
    loss = lax.psum(loss8[0, 0], ("x", "y", "c"))

    dgain = {}

    def ffn_bwd(l, saved, dh_out, dmix, below):
        h_in, fi, ug, uv, act, mix = saved
        if dmix is None:
            dmix, dgain[l, 3] = _rms_bwd(mix, gain(l, 3), dh_out, None, bf16, tm, f"ffn{l}_down_norm_bwd")
        dact = _mm_nt([(dmix, w_down[l])], bf16, tm, ff, f"ffn{l}_down_dx")
        dw_down = _mm_tn(act, dmix, tm, ff, tn, f"ffn{l}_down_dw")
        dug, duv, dcg, dcv = _conv_gate_bwd(ug, uv, cw_gate[l], cw_val[l], dact, bl_, t, tc, f"ffn{l}_conv_gate_bwd")
        dfi = _mm_nt([(dug, w_gate[l]), (duv, w_val[l])], bf16, tm, tn // 2, f"ffn{l}_up_dx")
        dw_up = jnp.concatenate([_mm_tn(fi, dug, tm, tn, ff, f"ffn{l}_up_gate_dw"),
                                 _mm_tn(fi, duv, tm, tn, ff, f"ffn{l}_up_val_dw")], axis=1)
        mix_below, gain_below, key_below = below
        dh_in, dgain[l, 2], dmix_below, dgain[key_below] = _rms_bwd(
            h_in, gain(l, 2), dfi, dh_out, f32, tm, f"ffn{l}_norm_bwd", then=(mix_below, gain_below))
        return dh_in, dmix_below, dw_up, jnp.concatenate([dcg, dcv], axis=1), dw_down

    def shards(full, axis):
        return _rows(_shard8(full, axis), 1).astype(bf16)

    def push_grads(bufs, name, tie=None):
        lands = [_landing(lax.dynamic_index_in_dim(b, me, 0, keepdims=False), me) for b in bufs]
        s_sem, r_sem, srcs, lands, token = _push_start(bufs, lands, True, name)
        gains_box[0] = gains_box[0] + token[0, 0]
        return (s_sem, r_sem, srcs, lands), (None if tie is None else tie + token[0, 0])

    def landed(handle, after, name):
        s_sem, r_sem, srcs, lands = handle
        return _push_wait(s_sem, r_sem, srcs, lands, tuple(range(len(srcs))), after, True, name)

    dh, dmix, dw_up1, dconv1, dw_down1 = ffn_bwd(1, ffn1, dh, None, (mix_b, gain(1, 1), (1, 1)))
    push1, ck = push_grads([shards(dw_up1, 1), shards(dw_down1, 0)], "grad_push_ffn1", ck)

    do = _mm_nt([(dmix, w_out_b)], bf16, tm, tn, "b_out_dx")
    dw_out_b = _mm_tn(o, dmix, tm, tn, tn, "b_out_dw")
    dq, dk, dv, dck, dcq = _attn_bwd(q, kvp, o, do, lse, ck, bl_, t, tm, hd, "b_attn_bwd")
    dhn1 = _mm_nt([(dq, w_q)], bf16, tm, tn, "b_q_dx")
    dw_q = _mm_tn(hn1, dq, tm, tn, tn, "b_q_dw")
    dh, dgain[1, 0] = _rms_bwd(h2, gain(1, 0), dhn1, dh, f32, tm, "b_norm_bwd")

    dzf, dfgb = _c_bwd(_dc_rows(dck, dcq, bl_, t, bh), zf, bl_, t, tm, "forget_cumsum_bwd")
    dhk = _mm_nt([(dk, w_kv[:, :d]), (dv, w_kv[:, d:]), (dzf, w_zf)], bf16, tm, tn, "kv_dx")
    dw_kv = jnp.concatenate([_mm_tn(hk, dk, tm, tn, tn, "k_dw"), _mm_tn(hk, dv, tm, tn, tn, "v_dw"),
                             _mm_tn(hk, dzf, tm, tn, LANES, "zf_dw")[:, :bh]], axis=1)
    dh, dkvn, dmix, dgain[0, 3] = _rms_bwd(h2, kvn, dhk, dh, f32, tm, "kv_norm_bwd", then=(ffn0[5], gain(0, 3)))
    push2, cw_gate = push_grads([shards(dw_out_b, 0), shards(dw_q, 0), shards(dw_kv, 1)], "grad_push_b", cw_gate)

    dh, dmix, dw_up0, dconv0, dw_down0 = ffn_bwd(0, ffn0, dh, dmix, (mix_a, gain(0, 1), (0, 1)))
    push3, head_gain = push_grads([shards(dw_up0, 1), shards(dw_down0, 0)], "grad_push_ffn0", head_gain)

    dog = _mm_nt([(dmix, w_out_a)], bf16, tm, tn, "a_out_dx")
    dw_out_a = _mm_tn(og, dmix, tm, tn, tn, "a_out_dw")
    dpq, dpf, dpi, dpg, dlb, dhg = _gla_bwd(pmat, states, dog, lb, head_gain, bl_, t, nm, "a_gla_bwd")
    dps = (dpq, dpf, dpi, dpg)
    dw_in = jnp.concatenate([_mm_tn(hn0, dp, tm, tn, tn, f"a_in_dw{j}") for j, dp in enumerate(dps)], axis=1)
    push4, _ = push_grads([shards(dw_out_a, 0), shards(dw_in, 1)], "grad_push_a")
    dhn0 = _mm_nt([(dp, w_in[:, j * d:(j + 1) * d]) for j, dp in enumerate(dps)], bf16, tm, tn, "a_in_dx")
    dh, dgain[0, 0] = _rms_bwd(h0, gain(0, 0), dhn0, dh, f32, tm, "a_norm_bwd")

    dh = dh.reshape(bl_, t, d)
    grad_x = dh[:, nm:]
    dl0 = dlb * lb * (1.0 - lb)
    vec_grads = dict(
        meta_tokens=jnp.sum(dh[:, :nm], axis=0),
        norm_gains=jnp.stack([jnp.concatenate([dgain[l, j] for j in range(4)], axis=0) for l in range(2)]),
        a_lb_logits=jnp.concatenate([dl0, -dl0], axis=0), a_head_norm=dhg, ffn_conv=jnp.stack([dconv0, dconv1]))
    vec_send = _pack([_shard8(vec_grads[k], _SHARD_AXIS[k]).reshape(N_DEV, -1) for k in _VECTORS], bf16, LANES, BF16_ROWS)
    push5, _ = push_grads([vec_send], "grad_push_vectors")

    g_s, d_s, m_s, v_s = {}, {}, {}, {}
    outs = (g_s, d_s, m_s, v_s)

    def update(part, srcs, label):
        rows = part.shape[1]
        return _adamw(part, *srcs, rows if rows <= _ROW_TILE_CAP else _div_tile(rows, _ROW_TILE_CAP), label)

    def update_matrix(k, part, layer=None):
        pick = (lambda a: a) if layer is None else (lambda a: a[layer])
        label = f"adamw_{k}" if layer is None else f"adamw_{k}{layer}"
        res = update(part, [_rows(pick(src[k])) for src in (w, mom, var)], label)
        return [r.reshape(pick(w[k]).shape) for r in res]

    def put(k, res):
        for dst, r in zip(outs, res):
            dst[k] = r

    up1, down1 = (update_matrix(k, p, 1) for k, p in zip(("ffn_w_up", "ffn_w_down"), landed(push1, gains_box[0], "grad_wait_ffn1")))
    for k, p in zip(("b_w_out", "b_w_q", "kv_w"), landed(push2, up1[0], "grad_wait_b")):
        put(k, update_matrix(k, p))
    up0, down0 = (update_matrix(k, p, 0) for k, p in zip(("ffn_w_up", "ffn_w_down"), landed(push3, g_s["kv_w"], "grad_wait_ffn0")))
    put("ffn_w_up", [jnp.stack(pair) for pair in zip(up0, up1)])
    put("ffn_w_down", [jnp.stack(pair) for pair in zip(down0, down1)])
    part_out_a, part_in = landed(push4, down0[0], "grad_wait_a")
    put("a_w_out", update_matrix("a_w_out", part_out_a))
    put("a_w_in", update_matrix("a_w_in", part_in))
    part_vec, = landed(push5, g_s["a_w_in"], "grad_wait_vectors")
    vec_packs = [_pack([src[k].reshape(-1) for k in _VECTORS], f32, LANES, BF16_ROWS) for src in (w, mom, var)]
    vec_shapes = [w[k].shape for k in _VECTORS]
    for dst, r in zip(outs, update(part_vec, vec_packs, "adamw_vectors")):
        dst.update(zip(_VECTORS, _unpack(r.reshape(-1), vec_shapes)))

    rep_local = _pack([dkvn.reshape(-1), dfgb[0, :bh]], f32, LANES, 8)
    rep_parts = _all_gather([rep_local], "gather_replicated_grads")[0]
    rpacks = [_pack([src[k].reshape(-1) for k in _REPLICATED], f32, LANES, 8) for src in (w, mom, var)]
    rres = _adamw(rep_parts, *rpacks, rep_local.shape[0], "adamw_replicated")
    rshapes = [w[k].shape for k in _REPLICATED]
    g_r, d_r, m_r, v_r = ({k: a for k, a in zip(_REPLICATED, _unpack(r.reshape(-1), rshapes))} for r in rres)

    out = [loss, grad_x]
    for sh, rp in ((g_s, g_r), (d_s, d_r), (m_s, m_r), (v_s, v_r)):
        out += [sh[k] if k in sh else rp[k] for k in names]
    return tuple(out)
```

```python
import functools
import math

import jax
import jax.numpy as jnp
from jax import lax
from jax.experimental import pallas as pl
from jax.experimental.pallas import tpu as pltpu

f32 = jnp.float32
bf16 = jnp.bfloat16
SDS = jax.ShapeDtypeStruct

EPS = 1e-6
A_DK = 128
A_CHUNK = 64
GLA_GROUP = 4
GLA_HEADS = 2
TOKEN_TILE_CAP = 1024
MODEL_TILE_CAP = 1024
LANES = 128
SUBLANES = 8
BF16_ROWS = 16
VMEM_LIMIT = 56 * 1024 * 1024
ADAM_LR, ADAM_B1, ADAM_B2, ADAM_EPS, ADAM_WD, ADAM_STEP = 0.001, 0.9, 0.999, 1e-08, 0.01, 10
N_DEV = 8
MESH = pl.DeviceIdType.MESH

_NT = (((1,), (1,)), ((), ()))
_TN = (((0,), (0,)), ((), ()))
_HI = lax.Precision.HIGHEST


def _params(**kw):
    return pltpu.CompilerParams(vmem_limit_bytes=VMEM_LIMIT, **kw)


def _div_tile(n, cap, mult=BF16_ROWS):
    best = None
    for t in range(mult, min(n, cap) + 1, mult):
        if n % t == 0:
            best = t
    assert best is not None, (n, cap, mult)
    return best


def _bdot(a, b):
    return jnp.dot(a.astype(bf16), b.astype(bf16), preferred_element_type=f32)


def _bdot_nt(a, b):
    return lax.dot_general(a.astype(bf16), b.astype(bf16), _NT, preferred_element_type=f32)


def _bdot_tn(a, b):
    return lax.dot_general(a.astype(bf16), b.astype(bf16), _TN, preferred_element_type=f32)


def _iota2(shape, axis):
    return lax.broadcasted_iota(jnp.int32, shape, axis)


def _cumsum_rows(x):
    n = x.shape[0]
    tri = (_iota2((n, n), 0) >= _iota2((n, n), 1)).astype(f32)
    return jnp.dot(tri, x, precision=_HI, preferred_element_type=f32)


def _revcumsum_rows(x):
    n = x.shape[0]
    tri = (_iota2((n, n), 1) >= _iota2((n, n), 0)).astype(f32)
    return jnp.dot(tri, x, precision=_HI, preferred_element_type=f32)


def _sigmoid(x):
    return 1.0 / (1.0 + jnp.exp(-x))


def _rms_fwd(x, g, tm, name):
    n, d = x.shape

    def body(x_ref, g_ref, o_ref):
        xv = x_ref[...]
        r = lax.rsqrt(jnp.mean(xv * xv, axis=-1, keepdims=True) + EPS)
        o_ref[...] = (xv * r * g_ref[...]).astype(o_ref.dtype)

    return pl.pallas_call(
        body, grid=(n // tm,), name=name,
        in_specs=[pl.BlockSpec((tm, d), lambda i: (i, 0)), pl.BlockSpec((1, d), lambda i: (0, 0))],
        out_specs=pl.BlockSpec((tm, d), lambda i: (i, 0)),
        out_shape=SDS((n, d), bf16), compiler_params=_params(),
    )(x, g)


def _mm(a, w, out_dtype, tm, tn, name):
    n, k = a.shape
    m = w.shape[1]

    def body(a_ref, w_ref, o_ref):
        o_ref[...] = _bdot(a_ref[...], w_ref[...]).astype(o_ref.dtype)

    return pl.pallas_call(
        body, grid=(m // tn, n // tm), name=name,
        in_specs=[pl.BlockSpec((tm, k), lambda j, i: (i, 0)), pl.BlockSpec((k, tn), lambda j, i: (0, j))],
        out_specs=pl.BlockSpec((tm, tn), lambda j, i: (i, j)),
        out_shape=SDS((n, m), out_dtype), compiler_params=_params(),
    )(a, w)


def _mm_norm_res(a, w, g, h, next_gains, tm, name):
    n, k = a.shape
    d = w.shape[1]
    nn = len(next_gains)

    def body(a_ref, w_ref, g_ref, h_ref, *rest):
        ng_refs, (hn_ref, mix_ref), out_refs = rest[:nn], rest[nn:nn + 2], rest[nn + 2:]
        mix = _bdot(a_ref[...], w_ref[...])
        r = lax.rsqrt(jnp.mean(mix * mix, axis=-1, keepdims=True) + EPS)
        mix_ref[...] = mix
        hn = h_ref[...] + mix * r * g_ref[...]
        hn_ref[...] = hn
        if nn:
            rn = lax.rsqrt(jnp.mean(hn * hn, axis=-1, keepdims=True) + EPS)
            for ng_ref, o_ref in zip(ng_refs, out_refs):
                o_ref[...] = (hn * rn * ng_ref[...]).astype(o_ref.dtype)

    row = pl.BlockSpec((tm, d), lambda i: (i, 0))
    vec = pl.BlockSpec((1, d), lambda i: (0, 0))
    return pl.pallas_call(
        body, grid=(n // tm,), name=name,
        in_specs=[pl.BlockSpec((tm, k), lambda i: (i, 0)), pl.BlockSpec((k, d), lambda i: (0, 0)), vec, row] + [vec] * nn,
        out_specs=[row] * (2 + nn),
        out_shape=[SDS((n, d), f32), SDS((n, d), f32)] + [SDS((n, d), bf16)] * nn, compiler_params=_params(),
    )(a, w, g, h, *next_gains)


def _norm_bwd_tile(xv, gv, dyv):
    r = lax.rsqrt(jnp.mean(xv * xv, axis=-1, keepdims=True) + EPS)
    xr = xv * r
    gdy = dyv * gv
    return r * gdy - xr * (r * r) * jnp.mean(xv * gdy, axis=-1, keepdims=True), jnp.sum(dyv * xr, axis=0, keepdims=True)


def _rms_bwd(x, g, dy, dh_in, out_dtype, tm, name, then=None):
    n, d = x.shape
    has_add = dh_in is not None
    has_next = then is not None

    norm_bwd = _norm_bwd_tile

    def body(*refs):
        refs = list(refs)
        x_ref, g_ref, dy_ref = refs[:3]
        dh_ref = refs[3] if has_add else None
        ins_end = 3 + has_add + 2 * has_next
        o_ref, dg_ref = refs[ins_end:ins_end + 2]
        dx, dg = norm_bwd(x_ref[...], g_ref[...], dy_ref[...].astype(f32))
        if has_add:
            dx = dx + dh_ref[...]
        o_ref[...] = dx.astype(o_ref.dtype)

        @pl.when(pl.program_id(0) == 0)
        def _():
            for ref in refs[ins_end + 1::2]:
                ref[...] = jnp.zeros_like(ref)

        dg_ref[...] += dg
        if has_next:
            x2_ref, g2_ref = refs[ins_end - 2:ins_end]
            o2_ref, dg2_ref = refs[ins_end + 2:]
            dx2, dg2 = norm_bwd(x2_ref[...], g2_ref[...], dx)
            o2_ref[...] = dx2.astype(o2_ref.dtype)
            dg2_ref[...] += dg2

    row = pl.BlockSpec((tm, d), lambda i: (i, 0))
    vec = pl.BlockSpec((1, d), lambda i: (0, 0))
    ins = [x, g, dy] + ([dh_in] if has_add else []) + (list(then) if has_next else [])
    return pl.pallas_call(
        body, grid=(n // tm,), name=name,
        in_specs=[row, vec, row] + ([row] if has_add else []) + ([row, vec] if has_next else []),
        out_specs=[row, vec] + ([row, vec] if has_next else []),
        out_shape=[SDS((n, d), out_dtype), SDS((1, d), f32)] + ([SDS((n, d), bf16), SDS((1, d), f32)] if has_next else []),
        compiler_params=_params(),
    )(*ins)


def _mm_nt(pairs, out_dtype, tm, tk, name):
    n = pairs[0][0].shape[0]
    k = pairs[0][1].shape[0]
    np_ = len(pairs)

    def body(*refs):
        o_ref = refs[-1]
        acc = None
        for p in range(np_):
            t = _bdot_nt(refs[2 * p][...], refs[2 * p + 1][...])
            acc = t if acc is None else acc + t
        o_ref[...] = acc.astype(o_ref.dtype)

    in_specs, ins = [], []
    for dy, w in pairs:
        m = dy.shape[1]
        in_specs += [pl.BlockSpec((tm, m), lambda j, i: (i, 0)), pl.BlockSpec((tk, m), lambda j, i: (j, 0))]
        ins += [dy, w]
    return pl.pallas_call(
        body, grid=(k // tk, n // tm), name=name, in_specs=in_specs,
        out_specs=pl.BlockSpec((tm, tk), lambda j, i: (i, j)),
        out_shape=SDS((n, k), out_dtype), compiler_params=_params(),
    )(*ins)


def _mm_tn(x, dy, tm, tk, tn, name):
    n, k = x.shape
    m = dy.shape[1]

    def body(x_ref, dy_ref, o_ref):
        @pl.when(pl.program_id(2) == 0)
        def _():
            o_ref[...] = jnp.zeros_like(o_ref)

        o_ref[...] += _bdot_tn(x_ref[...], dy_ref[...])

    return pl.pallas_call(
        body, grid=(k // tk, m // tn, n // tm), name=name,
        in_specs=[pl.BlockSpec((tm, tk), lambda a, b, i: (i, a)), pl.BlockSpec((tm, tn), lambda a, b, i: (i, b))],
        out_specs=pl.BlockSpec((tk, tn), lambda a, b, i: (a, b)),
        out_shape=SDS((k, m), f32), compiler_params=_params(),
    )(x, dy)


def _split3(x):
    hi = x.astype(bf16)
    r = x - hi.astype(f32)
    mid = r.astype(bf16)
    return hi, mid, (r - mid.astype(f32)).astype(bf16)


def _mask_dot(mask, x):
    hi, mid, lo = _split3(x)
    dot = lambda p: jnp.dot(mask, p, preferred_element_type=f32)
    return dot(hi) + dot(mid) + dot(lo)


_BNN = (((2,), (1,)), ((0,), (0,)))
_BNT = (((2,), (2,)), ((0,), (0,)))
_BTN = (((1,), (1,)), ((0,), (0,)))


def _hdot(a, b, dims):
    return lax.dot_general(a.astype(bf16), b.astype(bf16), dims, preferred_element_type=f32)


def _heads(x, nhb):
    return jnp.stack([x[:, h * A_DK:(h + 1) * A_DK] for h in range(nhb)])


def _mask_dot_heads(mask, x):
    return jnp.stack([_mask_dot(mask, x[h]) for h in range(x.shape[0])])


def _chunk_rows(parts, cl):
    tiles = [jnp.broadcast_to(p, (p.shape[0], cl, p.shape[2])) for p in parts]
    return tiles[0] if len(tiles) == 1 else jnp.concatenate(tiles, axis=1)


def _cat(parts):
    return parts[0] if len(parts) == 1 else jnp.concatenate(parts, axis=1)


def _gla_group_fwd(qg, fg, vg, lb, st, nc, cl):
    g = nc * cl
    sg = _sigmoid(fg)
    f = lb + (1.0 - lb) * sg
    k = 1.0 - f
    row, col = _iota2((g, g), 0), _iota2((g, g), 1)
    chunk_of = lambda idx: sum((idx >= u * cl).astype(jnp.int32) for u in range(1, nc)) if nc > 1 else 0
    same = chunk_of(row) == chunk_of(col) if nc > 1 else None
    causal = row >= col if nc == 1 else jnp.logical_and(same, row >= col)
    anti = col >= row if nc == 1 else jnp.logical_and(same, col >= row)
    b = _mask_dot_heads(causal.astype(bf16), jnp.log(f))
    bls = [b[:, (u + 1) * cl - 1:(u + 1) * cl, :] for u in range(nc)]
    ebls = [jnp.exp(x) for x in bls]
    e = jnp.exp(b)
    ei = jnp.exp(-b)
    eo = jnp.exp(_chunk_rows(bls, cl) - b)
    qi, ki, ko = qg * e, k * ei, k * eo
    att = jnp.where(causal[None], _hdot(qi, ki, _BNT), 0.0)
    o_intra = _hdot(att, vg, _BNN)
    sl = [slice(u * cl, (u + 1) * cl) for u in range(nc)]
    ds = [_hdot(vg[:, s], ko[:, s], _BTN) for s in sl]
    sts = [st]
    for u in range(nc):
        sts.append(sts[u] * ebls[u] + ds[u])
    o = o_intra + _cat([_hdot(qi[:, sl[u]], sts[u], _BNT) for u in range(nc)])
    return dict(sg=sg, f=f, e=e, ei=ei, eo=eo, ebls=ebls, qi=qi, ki=ki, ko=ko, att=att, o=o, sts=sts, causal=causal,
                anti=anti, sl=sl)


def _gla_group(nreal, want):
    while nreal % want:
        want //= 2
    return max(want, 1)


def _head_out(o, ggc, hg):
    r = lax.rsqrt(jnp.mean(o * o, axis=-1, keepdims=True) + EPS)
    return o * r * hg * (ggc * _sigmoid(ggc))


def _gla_fwd(pmat, lb, hg, bl_, t, nm, name):
    n, d4 = pmat.shape
    d = d4 // 4
    nh = d // A_DK
    nreal = (t - nm) // A_CHUNK
    nch = nreal + 1
    un = _gla_group(nreal, GLA_GROUP)
    hb = _gla_group(nh, GLA_HEADS)
    ng = nh // hb
    wide = hb * A_DK

    def body(q_ref, f_ref, i_ref, gg_ref, lb_ref, hg_ref, og_ref, ss_ref):
        lbv, hgv = _heads(lb_ref[...], hb), _heads(hg_ref[...], hb)
        take = lambda ref, rows: _heads(ref[rows, :], hb)

        def run(rows, st, idx, nc, cl):
            w = _gla_group_fwd(take(q_ref, rows), take(f_ref, rows), take(i_ref, rows), lbv, st, nc, cl)
            out = _head_out(w["o"], take(gg_ref, rows), hgv)
            for h in range(hb):
                for u in range(nc):
                    ss_ref[h, idx + u] = w["sts"][u][h]
                og_ref[rows, h * A_DK:(h + 1) * A_DK] = out[h].astype(og_ref.dtype)
            return w["sts"][nc]

        st = run(pl.ds(0, nm), jnp.zeros((hb, A_DK, A_DK), f32), 0, 1, nm)

        def step(it, st):
            rows = pl.ds(pl.multiple_of(nm + it * (un * A_CHUNK), BF16_ROWS), un * A_CHUNK)
            return run(rows, st, 1 + it * un, un, A_CHUNK)

        lax.fori_loop(0, nreal // un, step, st)

    col = lambda o: pl.BlockSpec((t, wide), lambda b, h: (b, o * ng + h))
    vec = pl.BlockSpec((1, wide), lambda b, h: (0, h))
    return pl.pallas_call(
        body, grid=(bl_, ng), name=name,
        in_specs=[col(0), col(1), col(2), col(3), vec, vec],
        out_specs=[pl.BlockSpec((t, wide), lambda b, h: (b, h)),
                   pl.BlockSpec((hb, nch, A_DK, A_DK), lambda b, h: (b * ng + h, 0, 0, 0))],
        out_shape=[SDS((n, d), bf16), SDS((bl_ * nh, nch, A_DK, A_DK), f32)], compiler_params=_params(),
    )(pmat, pmat, pmat, pmat, lb, hg)


def _gla_bwd(pmat, ss, dog, lb, hg, bl_, t, nm, name):
    n, d4 = pmat.shape
    d = d4 // 4
    nh = d // A_DK
    nreal = (t - nm) // A_CHUNK
    nch = nreal + 1
    un = _gla_group(nreal, GLA_GROUP)
    hb = _gla_group(nh, GLA_HEADS)
    ng = nh // hb
    wide = hb * A_DK

    def body(q_ref, f_ref, i_ref, gg_ref, ss_ref, dog_ref, lb_ref, hg_ref,
             dq_ref, df_ref, di_ref, dgg_ref, dlb_ref, dhg_ref):
        lbv, hgv = _heads(lb_ref[...], hb), _heads(hg_ref[...], hb)
        take = lambda ref, rows: _heads(ref[rows, :], hb)

        def put(ref, rows, val):
            for h in range(hb):
                ref[rows, h * A_DK:(h + 1) * A_DK] = val[h].astype(ref.dtype)

        def run(rows, idx, carry, nc, cl):
            dst, dlb, dhg = carry
            qg, fg, vg, ggc = take(q_ref, rows), take(f_ref, rows), take(i_ref, rows), take(gg_ref, rows)
            dogc = take(dog_ref, rows).astype(f32)
            st_in = jnp.stack([ss_ref[h, idx] for h in range(hb)])
            w = _gla_group_fwd(qg, fg, vg, lbv, st_in, nc, cl)
            o, qi, ki, ko, sl, sts, ebls = w["o"], w["qi"], w["ki"], w["ko"], w["sl"], w["sts"], w["ebls"]
            r = lax.rsqrt(jnp.mean(o * o, axis=-1, keepdims=True) + EPS)
            sgg = _sigmoid(ggc)
            sil = ggc * sgg
            on = o * r
            dhg = dhg + jnp.sum(dogc * sil * on, axis=1, keepdims=True)
            put(dgg_ref, rows, dogc * on * hgv * (sgg * (1.0 + ggc * (1.0 - sgg))))
            tt = dogc * sil * hgv
            do = r * tt - on * (r * r) * jnp.mean(o * tt, axis=-1, keepdims=True)
            xs = [_hdot(do[:, s], qi[:, s], _BTN) for s in sl]
            dsts = [None] * nc + [dst]
            for u in reversed(range(nc)):
                dsts[u] = dsts[u + 1] * ebls[u] + xs[u]
            datt = jnp.where(w["causal"][None], _hdot(do, vg, _BNT), 0.0)
            dv = _hdot(w["att"], do, _BTN) + _cat([_hdot(ko[:, sl[u]], dsts[u + 1], _BNT) for u in range(nc)])
            dko = _cat([_hdot(vg[:, sl[u]], dsts[u + 1], _BNN) for u in range(nc)])
            dqi = _hdot(datt, ki, _BNN) + _cat([_hdot(do[:, sl[u]], sts[u], _BNN) for u in range(nc)])
            dki = _hdot(datt, qi, _BTN)
            dk = dki * w["ei"] + dko * w["eo"]
            dkoko = dko * ko
            db = dqi * qi - dki * ki - dkoko
            rowi = lax.broadcasted_iota(jnp.int32, db.shape, 1)
            for u in range(nc):
                d_ebl = jnp.sum(dsts[u + 1] * sts[u], axis=1, keepdims=True)
                dbl = jnp.sum(dkoko[:, sl[u]], axis=1, keepdims=True) + d_ebl * ebls[u]
                db = db + jnp.where(rowi == (u + 1) * cl - 1, dbl, 0.0)
            dlogf = _mask_dot_heads(w["anti"].astype(bf16), db)
            df = dlogf / w["f"] - dk
            sg = w["sg"]
            put(dq_ref, rows, dqi * w["e"])
            put(df_ref, rows, df * (1.0 - lbv) * sg * (1.0 - sg))
            put(di_ref, rows, dv)
            dlb = dlb + jnp.sum(df * (1.0 - sg), axis=1, keepdims=True)
            return dsts[0], dlb, dhg

        zero = jnp.zeros((hb, 1, A_DK), f32)
        ngroups = nreal // un

        def step(it, carry):
            grp = ngroups - 1 - it
            rows = pl.ds(pl.multiple_of(nm + grp * (un * A_CHUNK), BF16_ROWS), un * A_CHUNK)
            return run(rows, 1 + grp * un, carry, un, A_CHUNK)

        carry = lax.fori_loop(0, ngroups, step, (jnp.zeros((hb, A_DK, A_DK), f32), zero, zero))
        _, dlb, dhg = run(pl.ds(0, nm), 0, carry, 1, nm)

        @pl.when(pl.program_id(1) == 0)
        def _():
            dlb_ref[...] = jnp.zeros_like(dlb_ref)
            dhg_ref[...] = jnp.zeros_like(dhg_ref)

        for h in range(hb):
            dlb_ref[:, h * A_DK:(h + 1) * A_DK] += dlb[h]
            dhg_ref[:, h * A_DK:(h + 1) * A_DK] += dhg[h]

    col = lambda o: pl.BlockSpec((t, wide), lambda h, b: (b, o * ng + h))
    blk = pl.BlockSpec((t, wide), lambda h, b: (b, h))
    vec = pl.BlockSpec((1, wide), lambda h, b: (0, h))
    return pl.pallas_call(
        body, grid=(ng, bl_), name=name,
        in_specs=[col(0), col(1), col(2), col(3),
                  pl.BlockSpec((hb, nch, A_DK, A_DK), lambda h, b: (b * ng + h, 0, 0, 0)), blk, vec, vec],
        out_specs=[blk, blk, blk, blk, vec, vec],
        out_shape=[SDS((n, d), bf16)] * 4 + [SDS((1, d), f32)] * 2, compiler_params=_params(),
    )(pmat, pmat, pmat, pmat, ss, dog, lb, hg)


def _shifted(x, halo, before):
    n = x.shape[0]
    both = jnp.concatenate([halo, x] if before else [x, halo], axis=0)
    row, col = _iota2((n, n + BF16_ROWS), 0), _iota2((n, n + BF16_ROWS), 1)
    src = row + BF16_ROWS if before else row
    step = -1 if before else 1
    pick = lambda s: jnp.dot((col == src + step * s).astype(bf16), both, preferred_element_type=f32)
    return pick(1), pick(2)


def _conv3(xb, halo, w):
    x = xb.astype(f32)
    x1, x2 = _shifted(xb, halo, True)
    return x, x1, x2, w[0:1, :] * x2 + w[1:2, :] * x1 + w[2:3, :] * x


def _conv_gate_fwd(ug, uv, cwg, cwv, bl_, t, tc, name):
    n, ff = ug.shape
    nt = t // tc

    def body(ug_ref, uv_ref, wg_ref, wv_ref, a_ref, hg_ref, hv_ref):
        @pl.when(pl.program_id(1) == 0)
        def _():
            hg_ref[...] = jnp.zeros_like(hg_ref)
            hv_ref[...] = jnp.zeros_like(hv_ref)

        xg, xv = ug_ref[...], uv_ref[...]
        cg = _conv3(xg, hg_ref[...], wg_ref[...])[3]
        cv = _conv3(xv, hv_ref[...], wv_ref[...])[3]
        a_ref[...] = (cg * _sigmoid(cg) * cv).astype(a_ref.dtype)
        hg_ref[...] = xg[tc - BF16_ROWS:tc, :].astype(hg_ref.dtype)
        hv_ref[...] = xv[tc - BF16_ROWS:tc, :].astype(hv_ref.dtype)

    row = pl.BlockSpec((tc, ff), lambda b, i: (b * nt + i, 0))
    wsp = pl.BlockSpec((3, ff), lambda b, i: (0, 0))
    return pl.pallas_call(
        body, grid=(bl_, nt), name=name, in_specs=[row, row, wsp, wsp], out_specs=row,
        out_shape=SDS((n, ff), bf16),
        scratch_shapes=[pltpu.VMEM((BF16_ROWS, ff), bf16), pltpu.VMEM((BF16_ROWS, ff), bf16)], compiler_params=_params(),
    )(ug, uv, cwg, cwv)


def _conv_gate_bwd(ug, uv, cwg, cwv, da, bl_, t, tc, name):
    n, ff = ug.shape
    nt = t // tc
    per = tc // BF16_ROWS

    def body(ug_ref, uv_ref, pg_ref, pv_ref, wg_ref, wv_ref, da_ref, dug_ref, duv_ref, dwg_ref, dwv_ref, ng_ref, nv_ref):
        first = jnp.logical_and(pl.program_id(0) == 0, pl.program_id(1) == 0)

        @pl.when(first)
        def _():
            dwg_ref[...] = jnp.zeros_like(dwg_ref)
            dwv_ref[...] = jnp.zeros_like(dwv_ref)

        @pl.when(pl.program_id(1) == 0)
        def _():
            ng_ref[...] = jnp.zeros_like(ng_ref)
            nv_ref[...] = jnp.zeros_like(nv_ref)

        seq_start = pl.program_id(1) == nt - 1
        dav = da_ref[...].astype(f32)

        def half(u_ref, p_ref, w_ref):
            halo = p_ref[...]
            return _conv3(u_ref[...], jnp.where(seq_start, jnp.zeros_like(halo), halo), w_ref[...])

        xg, xg1, xg2, cg = half(ug_ref, pg_ref, wg_ref)
        xv, xv1, xv2, cv = half(uv_ref, pv_ref, wv_ref)
        sg = _sigmoid(cg)
        dcg = dav * cv * (sg * (1.0 + cg * (1.0 - sg)))
        dcv = dav * (cg * sg)

        def back(dc, x, x1, x2, w_ref, nx_ref, du_ref, dw_ref):
            w = w_ref[...]
            dcb = dc.astype(bf16)
            dc1, dc2 = _shifted(dcb, nx_ref[...], False)
            du = w[2:3, :] * dc + w[1:2, :] * dc1 + w[0:1, :] * dc2
            du_ref[...] = du.astype(du_ref.dtype)
            dw_ref[0:1, :] += jnp.sum(dc * x2, axis=0, keepdims=True)
            dw_ref[1:2, :] += jnp.sum(dc * x1, axis=0, keepdims=True)
            dw_ref[2:3, :] += jnp.sum(dc * x, axis=0, keepdims=True)
            nx_ref[...] = dcb[0:BF16_ROWS, :].astype(nx_ref.dtype)

        back(dcg, xg, xg1, xg2, wg_ref, ng_ref, dug_ref, dwg_ref)
        back(dcv, xv, xv1, xv2, wv_ref, nv_ref, duv_ref, dwv_ref)

    row = pl.BlockSpec((tc, ff), lambda b, i: (b * nt + nt - 1 - i, 0))
    prev = pl.BlockSpec((BF16_ROWS, ff), lambda b, i: (jnp.maximum((b * nt + nt - 1 - i) * per - 1, 0), 0))
    wsp = pl.BlockSpec((3, ff), lambda b, i: (0, 0))
    return pl.pallas_call(
        body, grid=(bl_, nt), name=name, in_specs=[row, row, prev, prev, wsp, wsp, row],
        out_specs=[row, row, wsp, wsp],
        out_shape=[SDS((n, ff), bf16), SDS((n, ff), bf16), SDS((3, ff), f32), SDS((3, ff), f32)],
        scratch_shapes=[pltpu.VMEM((BF16_ROWS, ff), bf16), pltpu.VMEM((BF16_ROWS, ff), bf16)], compiler_params=_params(),
    )(ug, uv, ug, uv, cwg, cwv, da)


def _zf_c(hk, wzf, fgb, bl_, t, tm, name):
    n, d = hk.shape
    nt = t // tm

    def body(hk_ref, w_ref, b_ref, zf_ref, c_ref, carry_ref):
        @pl.when(pl.program_id(1) == 0)
        def _():
            carry_ref[...] = jnp.zeros_like(carry_ref)

        z = _bdot(hk_ref[...], w_ref[...]) + b_ref[...]
        ls = jnp.minimum(z, 0.0) - jnp.log(1.0 + jnp.exp(-jnp.abs(z)))
        c = _cumsum_rows(ls) + carry_ref[...]
        zf_ref[...] = z
        c_ref[...] = c
        carry_ref[...] = c[tm - 1:tm, :]

    row = lambda w: pl.BlockSpec((tm, w), lambda b, i: (b * nt + i, 0))
    return pl.pallas_call(
        body, grid=(bl_, nt), name=name,
        in_specs=[row(d), pl.BlockSpec((d, LANES), lambda b, i: (0, 0)), pl.BlockSpec((1, LANES), lambda b, i: (0, 0))],
        out_specs=[row(LANES), row(LANES)],
        out_shape=[SDS((n, LANES), f32), SDS((n, LANES), f32)],
        scratch_shapes=[pltpu.VMEM((1, LANES), f32)], compiler_params=_params(),
    )(hk, wzf, fgb)


def _c_bwd(dc, zf, bl_, t, tm, name):
    n = dc.shape[0]
    nt = t // tm

    def body(dc_ref, zf_ref, dzf_ref, dfg_ref, carry_ref):
        @pl.when(jnp.logical_and(pl.program_id(0) == 0, pl.program_id(1) == 0))
        def _():
            dfg_ref[...] = jnp.zeros_like(dfg_ref)

        @pl.when(pl.program_id(1) == 0)
        def _():
            carry_ref[...] = jnp.zeros_like(carry_ref)

        rc = _revcumsum_rows(dc_ref[...]) + carry_ref[...]
        dz = rc * _sigmoid(-zf_ref[...])
        dzf_ref[...] = dz.astype(dzf_ref.dtype)
        dfg_ref[...] += jnp.sum(dz, axis=0, keepdims=True)
        carry_ref[...] = rc[0:1, :]

    row = pl.BlockSpec((tm, LANES), lambda b, i: (b * nt + nt - 1 - i, 0))
    vec = pl.BlockSpec((1, LANES), lambda b, i: (0, 0))
    return pl.pallas_call(
        body, grid=(bl_, nt), name=name, in_specs=[row, row], out_specs=[row, vec],
        out_shape=[SDS((n, LANES), bf16), SDS((1, LANES), f32)],
        scratch_shapes=[pltpu.VMEM((1, LANES), f32)], compiler_params=_params(),
    )(dc, zf)


def _is_pow2(x):
    m, _ = math.frexp(x)
    return m == 0.5


def _prescale(qh, scale):
    return (qh.astype(f32) * scale).astype(bf16)


def _attn_fwd(q, kv, ck, bl_, t, tq, hd, name):
    n, d = q.shape
    npair = d // LANES
    hp = LANES // hd
    nq = t // tq
    scale = 1.0 / (hd ** 0.5)

    pre = _is_pow2(scale)

    def body(q_ref, k_ref, v_ref, ck_ref, o_ref, lse_ref):
        i = pl.program_id(2)
        diag = _iota2((tq, tq), 0) >= _iota2((tq, tq), 1)
        for hh in range(hp):
            lanes = slice(hh * hd, (hh + 1) * hd)
            qh = _prescale(q_ref[:, lanes], scale) if pre else q_ref[:, lanes]

            def block(j, carry, masked, lanes=lanes, qh=qh, hh=hh):
                m, l, acc = carry
                rows = pl.ds(pl.multiple_of(j * tq, BF16_ROWS), tq)
                s = _bdot_nt(qh, k_ref[rows, lanes])
                s = (s if pre else s * scale) - ck_ref[0, 0, j, hh:hh + 1, :]
                if masked:
                    s = jnp.where(diag, s, -1e30)
                m2 = jnp.maximum(m, jnp.max(s, axis=-1, keepdims=True))
                p = jnp.exp(s - m2)
                a = jnp.exp(m - m2)
                return m2, a * l + jnp.sum(p, axis=-1, keepdims=True), a * acc + _bdot(p, v_ref[rows, lanes])

            init = (jnp.full((tq, 1), -1e30, f32), jnp.zeros((tq, 1), f32), jnp.zeros((tq, hd), f32))
            carry = lax.fori_loop(0, i, functools.partial(block, masked=False), init)
            m, l, acc = block(i, carry, True)
            o_ref[:, lanes] = (acc / l).astype(o_ref.dtype)
            lse_ref[:, lanes] = jnp.broadcast_to(m + jnp.log(l), (tq, hd))

    nk = nq
    return pl.pallas_call(
        body, grid=(bl_, npair, nq), name=name,
        in_specs=[pl.BlockSpec((tq, LANES), lambda b, p, i: (b * nq + i, p)),
                  pl.BlockSpec((t, LANES), lambda b, p, i: (b, p)),
                  pl.BlockSpec((t, LANES), lambda b, p, i: (b, npair + p)),
                  pl.BlockSpec((1, 1, nk, hp, tq), lambda b, p, i: (b, p, 0, 0, 0))],
        out_specs=[pl.BlockSpec((tq, LANES), lambda b, p, i: (b * nq + i, p)),
                   pl.BlockSpec((tq, LANES), lambda b, p, i: (b * nq + i, p))],
        out_shape=[SDS((n, d), f32), SDS((n, d), f32)], compiler_params=_params(),
    )(q, kv, kv, ck)


def _attn_bwd(q, kv, o, do, lse, ck, bl_, t, tq, hd, name):
    n, d = q.shape
    npair = d // LANES
    hp = LANES // hd
    nq = t // tq
    scale = 1.0 / (hd ** 0.5)

    pre = _is_pow2(scale)

    def body(q_ref, k_ref, v_ref, o_ref, do_ref, lse_ref, ck_ref, dq_ref, dk_ref, dv_ref, dck_ref, dcq_ref):
        j = pl.program_id(2)

        @pl.when(j == 0)
        def _():
            dq_ref[...] = jnp.zeros_like(dq_ref)
            dcq_ref[...] = jnp.zeros_like(dcq_ref)

        diag = _iota2((tq, tq), 0) >= _iota2((tq, tq), 1)
        for hh in range(hp):
            lanes = slice(hh * hd, (hh + 1) * hd)
            kh = k_ref[:, lanes]
            vh = v_ref[:, lanes]
            kt = kh.astype(f32).T.astype(bf16)
            cs = ck_ref[0, 0, 0, hh:hh + 1, :]

            def block(i, carry, masked, lanes=lanes, kh=kh, vh=vh, kt=kt, cs=cs, hh=hh):
                dkt, dvt, dcs = carry
                rows = pl.ds(pl.multiple_of(i * tq, BF16_ROWS), tq)
                qh = _prescale(q_ref[rows, lanes], scale) if pre else q_ref[rows, lanes]
                doh = do_ref[rows, lanes]
                s = _bdot_nt(qh, kh)
                s = (s if pre else s * scale) - cs
                if masked:
                    s = jnp.where(diag, s, -1e30)
                p = jnp.exp(s - lse_ref[rows, hh * hd:hh * hd + 1])
                delta = jnp.sum(doh.astype(f32) * o_ref[rows, lanes].astype(f32), axis=-1, keepdims=True)
                ds = p * (_bdot_nt(doh, vh) - delta)
                dsb = ds.astype(bf16)
                dq_ref[rows, lanes] += _bdot_nt(kt, dsb).T * scale
                dcq_ref[0, rows, hh:hh + 1] += jnp.sum(ds, axis=-1, keepdims=True)
                dkq = _bdot_tn(qh, dsb)
                return (dkt + (dkq if pre else dkq * scale), dvt + _bdot_tn(doh, p), dcs - jnp.sum(ds, axis=0, keepdims=True))

            init = (jnp.zeros((hd, tq), f32), jnp.zeros((hd, tq), f32), jnp.zeros((1, tq), f32))
            dkt, dvt, dcs = lax.fori_loop(j + 1, nq, functools.partial(block, masked=False), block(j, init, True))
            dk_ref[:, lanes] = dkt.T.astype(dk_ref.dtype)
            dv_ref[:, lanes] = dvt.T.astype(dv_ref.dtype)
            dck_ref[0, 0, 0, hh:hh + 1, :] = dcs

    whole = lambda c0: pl.BlockSpec((t, LANES), lambda b, p, j: (b, c0 + p))
    tile = lambda c0: pl.BlockSpec((tq, LANES), lambda b, p, j: (b * nq + j, c0 + p))
    ckspec = pl.BlockSpec((1, 1, 1, hp, tq), lambda b, p, j: (b, p, j, 0, 0))
    cqspec = pl.BlockSpec((1, t, hp), lambda b, p, j: (p, b, 0))
    return pl.pallas_call(
        body, grid=(bl_, npair, nq), name=name,
        in_specs=[whole(0), tile(0), tile(npair), whole(0), whole(0), whole(0), ckspec],
        out_specs=[whole(0), tile(0), tile(0), ckspec, cqspec],
        out_shape=[SDS((n, d), f32), SDS((n, d), bf16), SDS((n, d), bf16), SDS((bl_, npair, nq, hp, tq), f32),
                   SDS((npair, n, hp), f32)],
        compiler_params=_params(),
    )(q, kv, kv, o, do, lse, ck)


def _loss_head(h, target, mix, g, t, nm, tm, name):
    n, d = h.shape
    nt = t // tm

    def body(h_ref, t_ref, mix_ref, g_ref, loss_ref, dh_ref, dmix_ref, dg_ref):
        i = pl.program_id(0)

        @pl.when(i == 0)
        def _():
            loss_ref[...] = jnp.zeros_like(loss_ref)
            dg_ref[...] = jnp.zeros_like(dg_ref)

        pos = (i % nt) * tm + _iota2((tm, d), 0)
        err = jnp.where(pos >= nm, h_ref[...] - t_ref[...], 0.0)
        dh = err * (1.0 / d)
        dh_ref[...] = dh
        loss_ref[...] += 0.5 * jnp.sum(jnp.mean(err * err, axis=-1, keepdims=True))
        dmix, dg = _norm_bwd_tile(mix_ref[...], g_ref[...], dh)
        dmix_ref[...] = dmix.astype(dmix_ref.dtype)
        dg_ref[...] += dg

    row = pl.BlockSpec((tm, d), lambda i: (i, 0))
    vec = pl.BlockSpec((1, d), lambda i: (0, 0))
    return pl.pallas_call(
        body, grid=(n // tm,), name=name, in_specs=[row, row, row, vec],
        out_specs=[pl.BlockSpec((8, LANES), lambda i: (0, 0)), row, row, vec],
        out_shape=[SDS((8, LANES), f32), SDS((n, d), f32), SDS((n, d), bf16), SDS((1, d), f32)], compiler_params=_params(),
    )(h, target, mix, g)


def _c_key_rows(c, bl_, t, tq, bh, hp):
    npair = bh // hp
    nk = t // tq
    return c[:, :bh].reshape(bl_, nk, tq, npair, hp).transpose(0, 3, 1, 4, 2)


def _dc_rows(dck, dcq, bl_, t, bh):
    d = dck.transpose(0, 2, 4, 1, 3).reshape(bl_ * t, bh) + dcq.transpose(1, 0, 2).reshape(bl_ * t, bh)
    return jnp.pad(d, ((0, 0), (0, LANES - bh)))


_ANY = pl.BlockSpec(memory_space=pl.ANY)


def _all_gather(xs, name):
    na = len(xs)

    def body(*refs):
        x_refs, out_refs = refs[:na], refs[na:2 * na]
        send_sems, recv_sems, local_sems = refs[2 * na:]
        mx, my, mc = lax.axis_index("x"), lax.axis_index("y"), lax.axis_index("c")
        me, sibling = (mx, my, mc), (mx, my, 1 - mc)
        chips = [(1 - mx, my), (mx, 1 - my), (1 - mx, 1 - my)]

        def copy(a, k, block, to, own=False):
            px, py, pc = block
            rows = out_refs[a].at[4 * px + 2 * py + pc]
            return pltpu.make_async_remote_copy(
                src_ref=x_refs[a] if own else rows, dst_ref=rows,
                send_sem=send_sems.at[a, k], recv_sem=recv_sems.at[a, k], device_id=to, device_id_type=MESH)

        arrays = range(na)
        mine = [pltpu.make_async_copy(x_refs[a], out_refs[a].at[4 * mx + 2 * my + mc], local_sems.at[a]) for a in arrays]
        for cp in mine:
            cp.start()
        first = [copy(a, 1 + j, me, (*chip, mc), own=True) for j, chip in enumerate(chips) for a in arrays]
        first += [copy(a, 0, me, sibling, own=True) for a in arrays]
        for cp in first:
            cp.start()
        passed = []
        for j, chip in enumerate(chips):
            for a in arrays:
                copy(a, 1 + j, (*chip, mc), me).wait_recv()
                cp = copy(a, 4 + j, (*chip, mc), sibling)
                cp.start()
                passed.append(cp)
        for a in arrays:
            copy(a, 0, sibling, me).wait_recv()
        for j, chip in enumerate(chips):
            for a in arrays:
                copy(a, 4 + j, (*chip, 1 - mc), me).wait_recv()
        for cp in first + passed:
            cp.wait_send()
        for cp in mine:
            cp.wait()

    return pl.pallas_call(
        body, name=name, out_shape=[SDS((N_DEV,) + x.shape, x.dtype) for x in xs],
        in_specs=[_ANY] * na, out_specs=[_ANY] * na,
        scratch_shapes=[pltpu.SemaphoreType.DMA((na, 7)), pltpu.SemaphoreType.DMA((na, 7)), pltpu.SemaphoreType.DMA((na,))],
    )(*xs)


_HBM = pl.BlockSpec(memory_space=pltpu.HBM)
_SEM = pl.BlockSpec(memory_space=pltpu.SEMAPHORE)
_DATAFLOW = pltpu.SideEffectType.DATAFLOW_SIDE_EFFECTING
N_PEERS = N_DEV - 1


def _device_index():
    return 4 * lax.axis_index("x") + 2 * lax.axis_index("y") + lax.axis_index("c")


def _peers():
    mx, my, mc = lax.axis_index("x"), lax.axis_index("y"), lax.axis_index("c")
    peers = []
    for r in (2, 3, 4, 5, 6, 7, 1):
        px = 1 - mx if r & 4 else mx
        py = 1 - my if r & 2 else my
        pc = 1 - mc if r & 1 else mc
        peers.append(((px, py, pc), 4 * px + 2 * py + pc))
    return 4 * mx + 2 * my + mc, peers


def _push_copy(src_ref, land_ref, send_sems, recv_sems, a, k, dev, src_row, land_row, scatter):
    return pltpu.make_async_remote_copy(
        src_ref=src_ref.at[src_row] if scatter else src_ref, dst_ref=land_ref.at[land_row],
        send_sem=send_sems.at[a * N_PEERS + k], recv_sem=recv_sems.at[a * N_PEERS + k], device_id=dev, device_id_type=MESH)


def _landing(own, me):
    return lax.dynamic_update_index_in_dim(lax.empty((N_DEV,) + own.shape, own.dtype), own, me, 0)


def _push_start(srcs, lands, scatter, name):
    na = len(srcs)

    def body(*refs):
        src_refs, land_refs = refs[:na], refs[na:2 * na]
        send_sems, recv_sems = refs[2 * na], refs[2 * na + 1]
        token = refs[-1]
        me, peers = _peers()
        for a in range(na):
            for k, (dev, idx) in enumerate(peers):
                _push_copy(src_refs[a], land_refs[a], send_sems, recv_sems, a, k, dev, idx, me, scatter).start()
        token[...] = jnp.zeros_like(token)

    hbm = lambda arrs: [pltpu.HBM(a.shape, a.dtype) for a in arrs]
    out = pl.pallas_call(
        body, name=name,
        out_shape=(pltpu.SemaphoreType.DMA((na * N_PEERS,)), pltpu.SemaphoreType.DMA((na * N_PEERS,)), *hbm(srcs), *hbm(lands),
                   SDS((8, LANES), f32)),
        in_specs=[_HBM] * (2 * na),
        out_specs=(_SEM, _SEM, *([_HBM] * (2 * na)), pl.BlockSpec(memory_space=pltpu.VMEM)),
        input_output_aliases={i: 2 + i for i in range(2 * na)},
        compiler_params=pltpu.CompilerParams(has_side_effects=_DATAFLOW),
    )(*[pltpu.with_memory_space_constraint(a, pltpu.HBM) for a in list(srcs) + list(lands)])
    return out[0], out[1], list(out[2:2 + na]), list(out[2 + na:2 + 2 * na]), out[-1]


def _push_wait(send_sems, recv_sems, srcs, lands, which, after, scatter, name):
    nw = len(which)

    def body(*refs):
        src_refs, land_refs = refs[:nw], refs[nw:2 * nw]
        send_sems_, recv_sems_ = refs[2 * nw], refs[2 * nw + 1]
        _, peers = _peers()
        for j, a in enumerate(which):
            for k, (dev, idx) in enumerate(peers):
                cp = _push_copy(src_refs[j], land_refs[j], send_sems_, recv_sems_, a, k, dev, idx, idx, scatter)
                cp.wait_send()
                cp.wait_recv()

    hbm = lambda arrs: [pltpu.HBM(a.shape, a.dtype) for a in arrs]
    out = pl.pallas_call(
        body, name=name, out_shape=(*hbm(srcs), *hbm(lands)),
        in_specs=[_HBM] * (2 * nw) + [_SEM, _SEM, _ANY], out_specs=[_HBM] * (2 * nw),
        input_output_aliases={i: i for i in range(2 * nw)},
        compiler_params=pltpu.CompilerParams(has_side_effects=_DATAFLOW),
    )(*srcs, *lands, send_sems, recv_sems, after)
    return list(out[nw:])


def _adamw(parts, w, m, v, tr, name):
    g, r, c = parts.shape

    def body(p_ref, w_ref, m_ref, v_ref, g_ref, d_ref, m2_ref, v2_ref):
        gr = p_ref[0].astype(f32)
        for k in range(1, g):
            gr = gr + p_ref[k].astype(f32)
        m2 = ADAM_B1 * m_ref[...] + (1.0 - ADAM_B1) * gr
        v2 = ADAM_B2 * v_ref[...] + (1.0 - ADAM_B2) * (gr * gr)
        m_hat = m2 / (1.0 - ADAM_B1 ** ADAM_STEP)
        v_hat = v2 / (1.0 - ADAM_B2 ** ADAM_STEP)
        g_ref[...] = gr
        d_ref[...] = -ADAM_LR * (m_hat / (jnp.sqrt(v_hat) + ADAM_EPS) + ADAM_WD * w_ref[...])
        m2_ref[...] = m2
        v2_ref[...] = v2

    row = pl.BlockSpec((tr, c), lambda i: (i, 0))
    return pl.pallas_call(
        body, grid=(r // tr,), name=name, in_specs=[pl.BlockSpec((g, tr, c), lambda i: (0, i, 0)), row, row, row],
        out_specs=[row] * 4, out_shape=[SDS((r, c), f32)] * 4, compiler_params=_params(),
    )(parts, w, m, v)


_SHARD_AXIS = dict(meta_tokens=1, norm_gains=2, a_w_in=2, a_lb_logits=1, a_head_norm=1, a_w_out=1, kv_w=1,
                   b_w_q=1, b_w_out=1, ffn_w_up=2, ffn_conv=2, ffn_w_down=1)
_VECTORS = ("meta_tokens", "norm_gains", "a_lb_logits", "a_head_norm", "ffn_conv")
_REPLICATED = ("kv_norm", "fg_b")
_ROW_TILE_CAP = 512


def _pack(arrs, dtype, cols, row_mult):
    lead = arrs[0].shape[:-1] if arrs[0].ndim > 1 else ()
    flat = jnp.concatenate([a.astype(dtype) for a in arrs], axis=-1)
    size = flat.shape[-1]
    per = cols * row_mult
    total = -(-size // per) * per
    flat = jnp.pad(flat, [(0, 0)] * len(lead) + [(0, total - size)])
    return flat.reshape(lead + (total // cols, cols))


def _unpack(flat, shapes):
    out, off = [], 0
    lead = flat.shape[:-1]
    for shp in shapes:
        size = 1
        for s in shp:
            size *= s
        out.append(flat[..., off:off + size].reshape(lead + tuple(shp)))
        off += size
    return out


def _unshard(seg, axis):
    a = jnp.moveaxis(seg, 0, axis)
    shp = a.shape
    return a.reshape(shp[:axis] + (shp[axis] * shp[axis + 1],) + shp[axis + 2:])


def _shard8(full, axis):
    shp = full.shape
    a = full.reshape(shp[:axis] + (N_DEV, shp[axis] // N_DEV) + shp[axis + 1:])
    return jnp.moveaxis(a, axis, 0)


def _rows(a, lead=0):
    return a.reshape(a.shape[:lead] + (-1, a.shape[-1]))


def kernel(x, meta_tokens, norm_gains, a_w_in, a_lb_logits, a_head_norm, a_w_out, kv_norm, kv_w, fg_b, b_w_q, b_w_out, ffn_w_up, ffn_conv, ffn_w_down, loss_target, m_meta_tokens, m_norm_gains, m_a_w_in, m_a_lb_logits, m_a_head_norm, m_a_w_out, m_kv_norm, m_kv_w, m_fg_b, m_b_w_q, m_b_w_out, m_ffn_w_up, m_ffn_conv, m_ffn_w_down, v_meta_tokens, v_norm_gains, v_a_w_in, v_a_lb_logits, v_a_head_norm, v_a_w_out, v_kv_norm, v_kv_w, v_fg_b, v_b_w_q, v_b_w_out, v_ffn_w_up, v_ffn_conv, v_ffn_w_down):
    names = ("meta_tokens", "norm_gains", "a_w_in", "a_lb_logits", "a_head_norm", "a_w_out", "kv_norm", "kv_w", "fg_b",
             "b_w_q", "b_w_out", "ffn_w_up", "ffn_conv", "ffn_w_down")
    w = dict(zip(names, (meta_tokens, norm_gains, a_w_in, a_lb_logits, a_head_norm, a_w_out, kv_norm, kv_w, fg_b,
                         b_w_q, b_w_out, ffn_w_up, ffn_conv, ffn_w_down)))
    mom = dict(zip(names, (m_meta_tokens, m_norm_gains, m_a_w_in, m_a_lb_logits, m_a_head_norm, m_a_w_out, m_kv_norm,
                           m_kv_w, m_fg_b, m_b_w_q, m_b_w_out, m_ffn_w_up, m_ffn_conv, m_ffn_w_down)))
    var = dict(zip(names, (v_meta_tokens, v_norm_gains, v_a_w_in, v_a_lb_logits, v_a_head_norm, v_a_w_out, v_kv_norm,
                           v_kv_w, v_fg_b, v_b_w_q, v_b_w_out, v_ffn_w_up, v_ffn_conv, v_ffn_w_down)))

    bl_, seq, d = x.shape
    nm = meta_tokens.shape[0]
    t = nm + seq
    n = bl_ * t
    bh = fg_b.shape[0]
    hd = d // bh
    hp = LANES // hd
    ff = ffn_w_down.shape[1] * N_DEV
    tm = _div_tile(t, TOKEN_TILE_CAP)
    tc = _div_tile(t, 64)
    tn = min(d, MODEL_TILE_CAP)

    vec_pack = _pack([w[k].reshape(-1) for k in _VECTORS], f32, LANES, 8)
    first = _all_gather([w["a_w_in"].astype(bf16), vec_pack], "gather_first")
    vec_segs = _unpack(first[1].reshape(N_DEV, -1), [w[k].shape for k in _VECTORS])
    small = {k: _unshard(a, _SHARD_AXIS[k]) for k, a in zip(_VECTORS, vec_segs)}
    w_in = _unshard(first[0], _SHARD_AXIS["a_w_in"])[0]
    me = _device_index()
    later_names = ("a_w_out", "ffn_w_up", "ffn_w_down", "kv_w", "b_w_q", "b_w_out", "ffn_w_up", "ffn_w_down")
    later_layer = (None, 0, 0, None, None, None, 1, 1)
    later = [(w[k] if l is None else w[k][l]).astype(bf16) for k, l in zip(later_names, later_layer)]
    later, _ = lax.optimization_barrier((later, first[1]))
    g_send, g_recv, later_src, later_land, _ = _push_start(later, [_landing(a, me) for a in later], False, "gather_rest_start")

    def gathered(which, after, name):
        lands = _push_wait(g_send, g_recv, [later_src[i] for i in which], [later_land[i] for i in which], which, after,
                           False, name)
        return [_unshard(a, _SHARD_AXIS[later_names[i]] - (later_layer[i] is not None)) for i, a in zip(which, lands)]

    gains_box = [small["norm_gains"]]
    gain = lambda l, j: gains_box[0][l, j][None]
    cw_gate, cw_val = small["ffn_conv"][:, :, :ff], small["ffn_conv"][:, :, ff:]
    head_gain = small["a_head_norm"]
    lb = jax.nn.softmax(small["a_lb_logits"], axis=0)[0:1]
    kvn = kv_norm[None]
    fgb_pad = jnp.pad(fg_b, (0, LANES - bh))[None]

    h0 = jnp.concatenate([jnp.broadcast_to(small["meta_tokens"][None], (bl_, nm, d)), x], axis=1).reshape(n, d)

    def ffn_fwd(l, h_in, fi, next_gains, which):
        w_up, = gathered(which[:1], h_in, f"gather_wait_ffn{l}_up")
        w_gate[l], w_val[l] = w_up[:, :ff], w_up[:, ff:]
        ug = _mm(fi, w_gate[l], bf16, tm, ff, f"ffn{l}_up_gate")
        uv = _mm(fi, w_val[l], bf16, tm, ff, f"ffn{l}_up_val")
        w_down[l], = gathered(which[1:], uv, f"gather_wait_ffn{l}_down")
        act = _conv_gate_fwd(ug, uv, cw_gate[l], cw_val[l], bl_, t, tc, f"ffn{l}_conv_gate")
        h_out, mix, *normed = _mm_norm_res(act, w_down[l], gain(l, 3), h_in, next_gains, tm, f"ffn{l}_down")
        return h_out, (h_in, fi, ug, uv, act, mix), normed

    hn0 = _rms_fwd(h0, gain(0, 0), tm, "a_norm")
    pmat = _mm(hn0, w_in, f32, tm, tn, "a_in_proj")
    og, states = _gla_fwd(pmat, lb, head_gain, bl_, t, nm, "a_gla_fwd")
    w_out_a = gathered((0,), og, "gather_wait_a")[0][0]
    h1, mix_a, fi0 = _mm_norm_res(og, w_out_a, gain(0, 1), h0, [gain(0, 2)], tm, "a_out_proj")
    w_gate, w_val, w_down = {}, {}, {}
    h2, ffn0, (hk, hn1) = ffn_fwd(0, h1, fi0, [kvn, gain(1, 0)], (1, 2))

    w_kv_zf, w_q, w_out_b = gathered((3, 4, 5), h2, "gather_wait_b")
    w_kv, w_zf = w_kv_zf[:, :2 * d], jnp.pad(w_kv_zf[:, 2 * d:], ((0, 0), (0, LANES - bh)))
    w_q, w_out_b = w_q[0], w_out_b[0]
    kvp = _mm(hk, w_kv, bf16, tm, tn, "kv_proj")
    zf, cum = _zf_c(hk, w_zf, fgb_pad, bl_, t, tm, "forget_cumsum")
    ck = _c_key_rows(cum, bl_, t, tm, bh, hp)
    q = _mm(hn1, w_q, bf16, tm, tn, "b_q_proj")
    o, lse = _attn_fwd(q, kvp, ck, bl_, t, tm, hd, "b_attn_fwd")
    h3, mix_b, fi1 = _mm_norm_res(o, w_out_b, gain(1, 1), h2, [gain(1, 2)], tm, "b_out_proj")
    h4, ffn1, _ = ffn_fwd(1, h3, fi1, [], (6, 7))

    dgain = {}
    target = jnp.concatenate([jnp.zeros((bl_, nm, d), f32), loss_target], axis=1).reshape(n, d)
    loss8, dh, dmix, dgain[1, 3] = _loss_head(h4, target, ffn1[5], gain(1, 3), t, nm, tm, "loss_head")
    loss = lax.psum(loss8[0, 0], ("x", "y", "c"))

    def ffn_bwd(l, saved, dh_out, dmix, below):
        h_in, fi, ug, uv, act, mix = saved
        dact = _mm_nt([(dmix, w_down[l])], bf16, tm, ff, f"ffn{l}_down_dx")
        dw_down = _mm_tn(act, dmix, tm, ff, tn, f"ffn{l}_down_dw")
        dug, duv, dcg, dcv = _conv_gate_bwd(ug, uv, cw_gate[l], cw_val[l], dact, bl_, t, tc, f"ffn{l}_conv_gate_bwd")
        dfi = _mm_nt([(dug, w_gate[l]), (duv, w_val[l])], bf16, tm, tn // 2, f"ffn{l}_up_dx")
        dw_up = jnp.concatenate([_mm_tn(fi, dug, tm, tn, ff, f"ffn{l}_up_gate_dw"),
                                 _mm_tn(fi, duv, tm, tn, ff, f"ffn{l}_up_val_dw")], axis=1)
        mix_below, gain_below, key_below = below
        dh_in, dgain[l, 2], dmix_below, dgain[key_below] = _rms_bwd(
            h_in, gain(l, 2), dfi, dh_out, f32, tm, f"ffn{l}_norm_bwd", then=(mix_below, gain_below))
        return dh_in, dmix_below, dw_up, jnp.concatenate([dcg, dcv], axis=1), dw_down

    def shards(full, axis):
        return _rows(_shard8(full, axis), 1).astype(bf16)

    def push_grads(bufs, name, tie=None):
        lands = [_landing(lax.dynamic_index_in_dim(b, me, 0, keepdims=False), me) for b in bufs]
        s_sem, r_sem, srcs, lands, token = _push_start(bufs, lands, True, name)
        gains_box[0] = gains_box[0] + token[0, 0]
        return (s_sem, r_sem, srcs, lands), (None if tie is None else tie + token[0, 0])

    def landed(handle, after, name):
        s_sem, r_sem, srcs, lands = handle
        return _push_wait(s_sem, r_sem, srcs, lands, tuple(range(len(srcs))), after, True, name)

    dh, dmix, dw_up1, dconv1, dw_down1 = ffn_bwd(1, ffn1, dh, dmix, (mix_b, gain(1, 1), (1, 1)))
    push1, ck = push_grads([shards(dw_up1, 1), shards(dw_down1, 0)], "grad_push_ffn1", ck)

    do = _mm_nt([(dmix, w_out_b)], bf16, tm, tn, "b_out_dx")
    dw_out_b = _mm_tn(o, dmix, tm, tn, tn, "b_out_dw")
    dq, dk, dv, dck, dcq = _attn_bwd(q, kvp, o, do, lse, ck, bl_, t, tm, hd, "b_attn_bwd")
    dhn1 = _mm_nt([(dq, w_q)], bf16, tm, tn, "b_q_dx")
    dw_q = _mm_tn(hn1, dq, tm, tn, tn, "b_q_dw")
    dh, dgain[1, 0] = _rms_bwd(h2, gain(1, 0), dhn1, dh, f32, tm, "b_norm_bwd")

    dzf, dfgb = _c_bwd(_dc_rows(dck, dcq, bl_, t, bh), zf, bl_, t, tm, "forget_cumsum_bwd")
    dhk = _mm_nt([(dk, w_kv[:, :d]), (dv, w_kv[:, d:]), (dzf, w_zf)], bf16, tm, tn, "kv_dx")
    dw_kv = jnp.concatenate([_mm_tn(hk, dk, tm, tn, tn, "k_dw"), _mm_tn(hk, dv, tm, tn, tn, "v_dw"),
                             _mm_tn(hk, dzf, tm, tn, LANES, "zf_dw")[:, :bh]], axis=1)
    dh, dkvn, dmix, dgain[0, 3] = _rms_bwd(h2, kvn, dhk, dh, f32, tm, "kv_norm_bwd", then=(ffn0[5], gain(0, 3)))
    push2, cw_gate = push_grads([shards(dw_out_b, 0), shards(dw_q, 0), shards(dw_kv, 1)], "grad_push_b", cw_gate)

    dh, dmix, dw_up0, dconv0, dw_down0 = ffn_bwd(0, ffn0, dh, dmix, (mix_a, gain(0, 1), (0, 1)))
    push3, head_gain = push_grads([shards(dw_up0, 1), shards(dw_down0, 0)], "grad_push_ffn0", head_gain)

    dog = _mm_nt([(dmix, w_out_a)], bf16, tm, tn, "a_out_dx")
    dw_out_a = _mm_tn(og, dmix, tm, tn, tn, "a_out_dw")
    dpq, dpf, dpi, dpg, dlb, dhg = _gla_bwd(pmat, states, dog, lb, head_gain, bl_, t, nm, "a_gla_bwd")
    dps = (dpq, dpf, dpi, dpg)
    dw_in = jnp.concatenate([_mm_tn(hn0, dp, tm, tn, tn, f"a_in_dw{j}") for j, dp in enumerate(dps)], axis=1)
    push4, _ = push_grads([shards(dw_out_a, 0), shards(dw_in, 1)], "grad_push_a")
    dhn0 = _mm_nt([(dp, w_in[:, j * d:(j + 1) * d]) for j, dp in enumerate(dps)], bf16, tm, tn, "a_in_dx")
    dh, dgain[0, 0] = _rms_bwd(h0, gain(0, 0), dhn0, dh, f32, tm, "a_norm_bwd")

    dh = dh.reshape(bl_, t, d)
    grad_x = dh[:, nm:]
    dl0 = dlb * lb * (1.0 - lb)
    vec_grads = dict(
        meta_tokens=jnp.sum(dh[:, :nm], axis=0),
        norm_gains=jnp.stack([jnp.concatenate([dgain[l, j] for j in range(4)], axis=0) for l in range(2)]),
        a_lb_logits=jnp.concatenate([dl0, -dl0], axis=0), a_head_norm=dhg, ffn_conv=jnp.stack([dconv0, dconv1]))
    vec_send = _pack([_shard8(vec_grads[k], _SHARD_AXIS[k]).reshape(N_DEV, -1) for k in _VECTORS], bf16, LANES, BF16_ROWS)
    push5, _ = push_grads([vec_send], "grad_push_vectors")

    g_s, d_s, m_s, v_s = {}, {}, {}, {}
    outs = (g_s, d_s, m_s, v_s)

    def update(part, srcs, label):
        rows = part.shape[1]
        return _adamw(part, *srcs, rows if rows <= _ROW_TILE_CAP else _div_tile(rows, _ROW_TILE_CAP), label)

    def update_matrix(k, part, layer=None):
        pick = (lambda a: a) if layer is None else (lambda a: a[layer])
        label = f"adamw_{k}" if layer is None else f"adamw_{k}{layer}"
        res = update(part, [_rows(pick(src[k])) for src in (w, mom, var)], label)
        return [r.reshape(pick(w[k]).shape) for r in res]

    def put(k, res):
        for dst, r in zip(outs, res):
            dst[k] = r

    up1, down1 = (update_matrix(k, p, 1) for k, p in zip(("ffn_w_up", "ffn_w_down"), landed(push1, gains_box[0], "grad_wait_ffn1")))
    for k, p in zip(("b_w_out", "b_w_q", "kv_w"), landed(push2, up1[0], "grad_wait_b")):
        put(k, update_matrix(k, p))
    up0, down0 = (update_matrix(k, p, 0) for k, p in zip(("ffn_w_up", "ffn_w_down"), landed(push3, g_s["kv_w"], "grad_wait_ffn0")))
    put("ffn_w_up", [jnp.stack(pair) for pair in zip(up0, up1)])
    put("ffn_w_down", [jnp.stack(pair) for pair in zip(down0, down1)])
    part_out_a, part_in = landed(push4, down0[0], "grad_wait_a")
    put("a_w_out", update_matrix("a_w_out", part_out_a))
    put("a_w_in", update_matrix("a_w_in", part_in))
    part_vec, = landed(push5, g_s["a_w_in"], "grad_wait_vectors")
    vec_packs = [_pack([src[k].reshape(-1) for k in _VECTORS], f32, LANES, BF16_ROWS) for src in (w, mom, var)]
    vec_shapes = [w[k].shape for k in _VECTORS]
    for dst, r in zip(outs, update(part_vec, vec_packs, "adamw_vectors")):
        dst.update(zip(_VECTORS, _unpack(r.reshape(-1), vec_shapes)))

    rep_local = _pack([dkvn.reshape(-1), dfgb[0, :bh]], f32, LANES, 8)
    rep_parts = _all_gather([rep_local], "gather_replicated_grads")[0]
    rpacks = [_pack([src[k].reshape(-1) for k in _REPLICATED], f32, LANES, 8) for src in (w, mom, var)]
    rres = _adamw(rep_parts, *rpacks, rep_local.shape[0], "adamw_replicated")
    rshapes = [w[k].shape for k in _REPLICATED]
    g_r, d_r, m_r, v_r = ({k: a for k, a in zip(_REPLICATED, _unpack(r.reshape(-1), rshapes))} for r in rres)

    out = [loss, grad_x]
    for sh, rp in ((g_s, g_r), (d_s, d_r), (m_s, m_r), (v_s, v_r)):
        out += [sh[k] if k in sh else rp[k] for k in names]
    return tuple(out)
```

```python
import functools
import math

import jax
import jax.numpy as jnp
from jax import lax
from jax.experimental import pallas as pl
from jax.experimental.pallas import tpu as pltpu

f32 = jnp.float32
bf16 = jnp.bfloat16
SDS = jax.ShapeDtypeStruct

EPS = 1e-6
A_DK = 128
A_CHUNK = 64
GLA_GROUP = 4
GLA_HEADS = 2
TOKEN_TILE_CAP = 1024
MODEL_TILE_CAP = 1024
LANES = 128
SUBLANES = 8
BF16_ROWS = 16
VMEM_LIMIT = 56 * 1024 * 1024
ADAM_LR, ADAM_B1, ADAM_B2, ADAM_EPS, ADAM_WD, ADAM_STEP = 0.001, 0.9, 0.999, 1e-08, 0.01, 10
N_DEV = 8
MESH = pl.DeviceIdType.MESH

_NT = (((1,), (1,)), ((), ()))
_TN = (((0,), (0,)), ((), ()))
_HI = lax.Precision.HIGHEST


def _params(**kw):
    return pltpu.CompilerParams(vmem_limit_bytes=VMEM_LIMIT, **kw)


def _div_tile(n, cap, mult=BF16_ROWS):
    best = None
    for t in range(mult, min(n, cap) + 1, mult):
        if n % t == 0:
            best = t
    assert best is not None, (n, cap, mult)
    return best


def _bdot(a, b):
    return jnp.dot(a.astype(bf16), b.astype(bf16), preferred_element_type=f32)


def _bdot_nt(a, b):
    return lax.dot_general(a.astype(bf16), b.astype(bf16), _NT, preferred_element_type=f32)


def _bdot_tn(a, b):
    return lax.dot_general(a.astype(bf16), b.astype(bf16), _TN, preferred_element_type=f32)


def _iota2(shape, axis):
    return lax.broadcasted_iota(jnp.int32, shape, axis)


def _cumsum_rows(x):
    n = x.shape[0]
    tri = (_iota2((n, n), 0) >= _iota2((n, n), 1)).astype(f32)
    return jnp.dot(tri, x, precision=_HI, preferred_element_type=f32)


def _revcumsum_rows(x):
    n = x.shape[0]
    tri = (_iota2((n, n), 1) >= _iota2((n, n), 0)).astype(f32)
    return jnp.dot(tri, x, precision=_HI, preferred_element_type=f32)


def _sigmoid(x):
    return 1.0 / (1.0 + jnp.exp(-x))


def _rms_fwd(x, g, tm, name):
    n, d = x.shape

    def body(x_ref, g_ref, o_ref):
        xv = x_ref[...]
        r = lax.rsqrt(jnp.mean(xv * xv, axis=-1, keepdims=True) + EPS)
        o_ref[...] = (xv * r * g_ref[...]).astype(o_ref.dtype)

    return pl.pallas_call(
        body, grid=(n // tm,), name=name,
        in_specs=[pl.BlockSpec((tm, d), lambda i: (i, 0)), pl.BlockSpec((1, d), lambda i: (0, 0))],
        out_specs=pl.BlockSpec((tm, d), lambda i: (i, 0)),
        out_shape=SDS((n, d), bf16), compiler_params=_params(),
    )(x, g)


def _mm(a, w, out_dtype, tm, tn, name):
    n, k = a.shape
    m = w.shape[1]

    def body(a_ref, w_ref, o_ref):
        o_ref[...] = _bdot(a_ref[...], w_ref[...]).astype(o_ref.dtype)

    return pl.pallas_call(
        body, grid=(m // tn, n // tm), name=name,
        in_specs=[pl.BlockSpec((tm, k), lambda j, i: (i, 0)), pl.BlockSpec((k, tn), lambda j, i: (0, j))],
        out_specs=pl.BlockSpec((tm, tn), lambda j, i: (i, j)),
        out_shape=SDS((n, m), out_dtype), compiler_params=_params(),
    )(a, w)


def _mm_norm_res(a, w, g, h, next_gains, tm, name):
    n, k = a.shape
    d = w.shape[1]
    nn = len(next_gains)

    def body(a_ref, w_ref, g_ref, h_ref, *rest):
        ng_refs, (hn_ref, mix_ref), out_refs = rest[:nn], rest[nn:nn + 2], rest[nn + 2:]
        mix = _bdot(a_ref[...], w_ref[...])
        r = lax.rsqrt(jnp.mean(mix * mix, axis=-1, keepdims=True) + EPS)
        mix_ref[...] = mix
        hn = h_ref[...] + mix * r * g_ref[...]
        hn_ref[...] = hn
        if nn:
            rn = lax.rsqrt(jnp.mean(hn * hn, axis=-1, keepdims=True) + EPS)
            for ng_ref, o_ref in zip(ng_refs, out_refs):
                o_ref[...] = (hn * rn * ng_ref[...]).astype(o_ref.dtype)

    row = pl.BlockSpec((tm, d), lambda i: (i, 0))
    vec = pl.BlockSpec((1, d), lambda i: (0, 0))
    return pl.pallas_call(
        body, grid=(n // tm,), name=name,
        in_specs=[pl.BlockSpec((tm, k), lambda i: (i, 0)), pl.BlockSpec((k, d), lambda i: (0, 0)), vec, row] + [vec] * nn,
        out_specs=[row] * (2 + nn),
        out_shape=[SDS((n, d), f32), SDS((n, d), f32)] + [SDS((n, d), bf16)] * nn, compiler_params=_params(),
    )(a, w, g, h, *next_gains)


def _norm_bwd_tile(xv, gv, dyv):
    r = lax.rsqrt(jnp.mean(xv * xv, axis=-1, keepdims=True) + EPS)
    xr = xv * r
    gdy = dyv * gv
    return r * gdy - xr * (r * r) * jnp.mean(xv * gdy, axis=-1, keepdims=True), jnp.sum(dyv * xr, axis=0, keepdims=True)


def _rms_bwd(x, g, dy, dh_in, out_dtype, tm, name, then=None):
    n, d = x.shape
    has_add = dh_in is not None
    has_next = then is not None

    norm_bwd = _norm_bwd_tile

    def body(*refs):
        refs = list(refs)
        x_ref, g_ref, dy_ref = refs[:3]
        dh_ref = refs[3] if has_add else None
        ins_end = 3 + has_add + 2 * has_next
        o_ref, dg_ref = refs[ins_end:ins_end + 2]
        dx, dg = norm_bwd(x_ref[...], g_ref[...], dy_ref[...].astype(f32))
        if has_add:
            dx = dx + dh_ref[...]
        o_ref[...] = dx.astype(o_ref.dtype)

        @pl.when(pl.program_id(0) == 0)
        def _():
            for ref in refs[ins_end + 1::2]:
                ref[...] = jnp.zeros_like(ref)

        dg_ref[...] += dg
        if has_next:
            x2_ref, g2_ref = refs[ins_end - 2:ins_end]
            o2_ref, dg2_ref = refs[ins_end + 2:]
            dx2, dg2 = norm_bwd(x2_ref[...], g2_ref[...], dx)
            o2_ref[...] = dx2.astype(o2_ref.dtype)
            dg2_ref[...] += dg2

    row = pl.BlockSpec((tm, d), lambda i: (i, 0))
    vec = pl.BlockSpec((1, d), lambda i: (0, 0))
    ins = [x, g, dy] + ([dh_in] if has_add else []) + (list(then) if has_next else [])
    return pl.pallas_call(
        body, grid=(n // tm,), name=name,
        in_specs=[row, vec, row] + ([row] if has_add else []) + ([row, vec] if has_next else []),
        out_specs=[row, vec] + ([row, vec] if has_next else []),
        out_shape=[SDS((n, d), out_dtype), SDS((1, d), f32)] + ([SDS((n, d), bf16), SDS((1, d), f32)] if has_next else []),
        compiler_params=_params(),
    )(*ins)


def _mm_nt(pairs, out_dtype, tm, tk, name):
    n = pairs[0][0].shape[0]
    k = pairs[0][1].shape[0]
    np_ = len(pairs)

    def body(*refs):
        o_ref = refs[-1]
        acc = None
        for p in range(np_):
            t = _bdot_nt(refs[2 * p][...], refs[2 * p + 1][...])
            acc = t if acc is None else acc + t
        o_ref[...] = acc.astype(o_ref.dtype)

    in_specs, ins = [], []
    for dy, w in pairs:
        m = dy.shape[1]
        in_specs += [pl.BlockSpec((tm, m), lambda j, i: (i, 0)), pl.BlockSpec((tk, m), lambda j, i: (j, 0))]
        ins += [dy, w]
    return pl.pallas_call(
        body, grid=(k // tk, n // tm), name=name, in_specs=in_specs,
        out_specs=pl.BlockSpec((tm, tk), lambda j, i: (i, j)),
        out_shape=SDS((n, k), out_dtype), compiler_params=_params(),
    )(*ins)


def _mm_tn(x, dy, tm, tk, tn, name):
    n, k = x.shape
    m = dy.shape[1]

    def body(x_ref, dy_ref, o_ref):
        @pl.when(pl.program_id(2) == 0)
        def _():
            o_ref[...] = jnp.zeros_like(o_ref)

        o_ref[...] += _bdot_tn(x_ref[...], dy_ref[...])

    return pl.pallas_call(
        body, grid=(k // tk, m // tn, n // tm), name=name,
        in_specs=[pl.BlockSpec((tm, tk), lambda a, b, i: (i, a)), pl.BlockSpec((tm, tn), lambda a, b, i: (i, b))],
        out_specs=pl.BlockSpec((tk, tn), lambda a, b, i: (a, b)),
        out_shape=SDS((k, m), f32), compiler_params=_params(),
    )(x, dy)


def _split3(x):
    hi = x.astype(bf16)
    r = x - hi.astype(f32)
    mid = r.astype(bf16)
    return hi, mid, (r - mid.astype(f32)).astype(bf16)


def _mask_dot(mask, x):
    hi, mid, lo = _split3(x)
    dot = lambda p: jnp.dot(mask, p, preferred_element_type=f32)
    return dot(hi) + dot(mid) + dot(lo)


_BNN = (((2,), (1,)), ((0,), (0,)))
_BNT = (((2,), (2,)), ((0,), (0,)))
_BTN = (((1,), (1,)), ((0,), (0,)))


def _hdot(a, b, dims):
    return lax.dot_general(a.astype(bf16), b.astype(bf16), dims, preferred_element_type=f32)


def _heads(x, nhb):
    return jnp.stack([x[:, h * A_DK:(h + 1) * A_DK] for h in range(nhb)])


def _mask_dot_heads(mask, x):
    return jnp.stack([_mask_dot(mask, x[h]) for h in range(x.shape[0])])


def _chunk_rows(parts, cl):
    tiles = [jnp.broadcast_to(p, (p.shape[0], cl, p.shape[2])) for p in parts]
    return tiles[0] if len(tiles) == 1 else jnp.concatenate(tiles, axis=1)


def _cat(parts):
    return parts[0] if len(parts) == 1 else jnp.concatenate(parts, axis=1)


def _gla_group_fwd(qg, fg, vg, lb, st, nc, cl):
    g = nc * cl
    sg = _sigmoid(fg)
    f = lb + (1.0 - lb) * sg
    k = 1.0 - f
    row, col = _iota2((g, g), 0), _iota2((g, g), 1)
    chunk_of = lambda idx: sum((idx >= u * cl).astype(jnp.int32) for u in range(1, nc)) if nc > 1 else 0
    same = chunk_of(row) == chunk_of(col) if nc > 1 else None
    causal = row >= col if nc == 1 else jnp.logical_and(same, row >= col)
    anti = col >= row if nc == 1 else jnp.logical_and(same, col >= row)
    b = _mask_dot_heads(causal.astype(bf16), jnp.log(f))
    bls = [b[:, (u + 1) * cl - 1:(u + 1) * cl, :] for u in range(nc)]
    ebls = [jnp.exp(x) for x in bls]
    e = jnp.exp(b)
    ei = jnp.exp(-b)
    eo = jnp.exp(_chunk_rows(bls, cl) - b)
    qi, ki, ko = qg * e, k * ei, k * eo
    att = jnp.where(causal[None], _hdot(qi, ki, _BNT), 0.0)
    o_intra = _hdot(att, vg, _BNN)
    sl = [slice(u * cl, (u + 1) * cl) for u in range(nc)]
    ds = [_hdot(vg[:, s], ko[:, s], _BTN) for s in sl]
    sts = [st]
    for u in range(nc):
        sts.append(sts[u] * ebls[u] + ds[u])
    o = o_intra + _cat([_hdot(qi[:, sl[u]], sts[u], _BNT) for u in range(nc)])
    return dict(sg=sg, f=f, e=e, ei=ei, eo=eo, ebls=ebls, qi=qi, ki=ki, ko=ko, att=att, o=o, sts=sts, causal=causal,
                anti=anti, sl=sl)


def _gla_group(nreal, want):
    while nreal % want:
        want //= 2
    return max(want, 1)


def _head_out(o, ggc, hg):
    r = lax.rsqrt(jnp.mean(o * o, axis=-1, keepdims=True) + EPS)
    return o * r * hg * (ggc * _sigmoid(ggc))


def _gla_fwd(pmat, lb, hg, bl_, t, nm, name):
    n, d4 = pmat.shape
    d = d4 // 4
    nh = d // A_DK
    nreal = (t - nm) // A_CHUNK
    nch = nreal + 1
    un = _gla_group(nreal, GLA_GROUP)
    hb = _gla_group(nh, GLA_HEADS)
    ng = nh // hb
    wide = hb * A_DK

    def body(q_ref, f_ref, i_ref, gg_ref, lb_ref, hg_ref, og_ref, ss_ref):
        lbv, hgv = _heads(lb_ref[...], hb), _heads(hg_ref[...], hb)
        take = lambda ref, rows: _heads(ref[rows, :], hb)

        def run(rows, st, idx, nc, cl):
            w = _gla_group_fwd(take(q_ref, rows), take(f_ref, rows), take(i_ref, rows), lbv, st, nc, cl)
            out = _head_out(w["o"], take(gg_ref, rows), hgv)
            for h in range(hb):
                for u in range(nc):
                    ss_ref[h, idx + u] = w["sts"][u][h]
                og_ref[rows, h * A_DK:(h + 1) * A_DK] = out[h].astype(og_ref.dtype)
            return w["sts"][nc]

        st = run(pl.ds(0, nm), jnp.zeros((hb, A_DK, A_DK), f32), 0, 1, nm)

        def step(it, st):
            rows = pl.ds(pl.multiple_of(nm + it * (un * A_CHUNK), BF16_ROWS), un * A_CHUNK)
            return run(rows, st, 1 + it * un, un, A_CHUNK)

        lax.fori_loop(0, nreal // un, step, st)

    col = lambda o: pl.BlockSpec((t, wide), lambda b, h: (b, o * ng + h))
    vec = pl.BlockSpec((1, wide), lambda b, h: (0, h))
    return pl.pallas_call(
        body, grid=(bl_, ng), name=name,
        in_specs=[col(0), col(1), col(2), col(3), vec, vec],
        out_specs=[pl.BlockSpec((t, wide), lambda b, h: (b, h)),
                   pl.BlockSpec((hb, nch, A_DK, A_DK), lambda b, h: (b * ng + h, 0, 0, 0))],
        out_shape=[SDS((n, d), bf16), SDS((bl_ * nh, nch, A_DK, A_DK), f32)], compiler_params=_params(),
    )(pmat, pmat, pmat, pmat, lb, hg)


def _gla_bwd(pmat, ss, dog, lb, hg, bl_, t, nm, name):
    n, d4 = pmat.shape
    d = d4 // 4
    nh = d // A_DK
    nreal = (t - nm) // A_CHUNK
    nch = nreal + 1
    un = _gla_group(nreal, GLA_GROUP)
    hb = _gla_group(nh, GLA_HEADS)
    ng = nh // hb
    wide = hb * A_DK

    def body(q_ref, f_ref, i_ref, gg_ref, ss_ref, dog_ref, lb_ref, hg_ref,
             dq_ref, df_ref, di_ref, dgg_ref, dlb_ref, dhg_ref):
        lbv, hgv = _heads(lb_ref[...], hb), _heads(hg_ref[...], hb)
        take = lambda ref, rows: _heads(ref[rows, :], hb)

        def put(ref, rows, val):
            for h in range(hb):
                ref[rows, h * A_DK:(h + 1) * A_DK] = val[h].astype(ref.dtype)

        def run(rows, idx, carry, nc, cl):
            dst, dlb, dhg = carry
            qg, fg, vg, ggc = take(q_ref, rows), take(f_ref, rows), take(i_ref, rows), take(gg_ref, rows)
            dogc = take(dog_ref, rows).astype(f32)
            st_in = jnp.stack([ss_ref[h, idx] for h in range(hb)])
            w = _gla_group_fwd(qg, fg, vg, lbv, st_in, nc, cl)
            o, qi, ki, ko, sl, sts, ebls = w["o"], w["qi"], w["ki"], w["ko"], w["sl"], w["sts"], w["ebls"]
            r = lax.rsqrt(jnp.mean(o * o, axis=-1, keepdims=True) + EPS)
            sgg = _sigmoid(ggc)
            sil = ggc * sgg
            on = o * r
            dhg = dhg + jnp.sum(dogc * sil * on, axis=1, keepdims=True)
            put(dgg_ref, rows, dogc * on * hgv * (sgg * (1.0 + ggc * (1.0 - sgg))))
            tt = dogc * sil * hgv
            do = r * tt - on * (r * r) * jnp.mean(o * tt, axis=-1, keepdims=True)
            xs = [_hdot(do[:, s], qi[:, s], _BTN) for s in sl]
            dsts = [None] * nc + [dst]
            for u in reversed(range(nc)):
                dsts[u] = dsts[u + 1] * ebls[u] + xs[u]
            datt = jnp.where(w["causal"][None], _hdot(do, vg, _BNT), 0.0)
            dv = _hdot(w["att"], do, _BTN) + _cat([_hdot(ko[:, sl[u]], dsts[u + 1], _BNT) for u in range(nc)])
            dko = _cat([_hdot(vg[:, sl[u]], dsts[u + 1], _BNN) for u in range(nc)])
            dqi = _hdot(datt, ki, _BNN) + _cat([_hdot(do[:, sl[u]], sts[u], _BNN) for u in range(nc)])
            dki = _hdot(datt, qi, _BTN)
            dk = dki * w["ei"] + dko * w["eo"]
            dkoko = dko * ko
            db = dqi * qi - dki * ki - dkoko
            rowi = lax.broadcasted_iota(jnp.int32, db.shape, 1)
            for u in range(nc):
                d_ebl = jnp.sum(dsts[u + 1] * sts[u], axis=1, keepdims=True)
                dbl = jnp.sum(dkoko[:, sl[u]], axis=1, keepdims=True) + d_ebl * ebls[u]
                db = db + jnp.where(rowi == (u + 1) * cl - 1, dbl, 0.0)
            dlogf = _mask_dot_heads(w["anti"].astype(bf16), db)
            df = dlogf / w["f"] - dk
            sg = w["sg"]
            put(dq_ref, rows, dqi * w["e"])
            put(df_ref, rows, df * (1.0 - lbv) * sg * (1.0 - sg))
            put(di_ref, rows, dv)
            dlb = dlb + jnp.sum(df * (1.0 - sg), axis=1, keepdims=True)
            return dsts[0], dlb, dhg

        zero = jnp.zeros((hb, 1, A_DK), f32)
        ngroups = nreal // un

        def step(it, carry):
            grp = ngroups - 1 - it
            rows = pl.ds(pl.multiple_of(nm + grp * (un * A_CHUNK), BF16_ROWS), un * A_CHUNK)
            return run(rows, 1 + grp * un, carry, un, A_CHUNK)

        carry = lax.fori_loop(0, ngroups, step, (jnp.zeros((hb, A_DK, A_DK), f32), zero, zero))
        _, dlb, dhg = run(pl.ds(0, nm), 0, carry, 1, nm)

        @pl.when(pl.program_id(1) == 0)
        def _():
            dlb_ref[...] = jnp.zeros_like(dlb_ref)
            dhg_ref[...] = jnp.zeros_like(dhg_ref)

        for h in range(hb):
            dlb_ref[:, h * A_DK:(h + 1) * A_DK] += dlb[h]
            dhg_ref[:, h * A_DK:(h + 1) * A_DK] += dhg[h]

    col = lambda o: pl.BlockSpec((t, wide), lambda h, b: (b, o * ng + h))
    blk = pl.BlockSpec((t, wide), lambda h, b: (b, h))
    vec = pl.BlockSpec((1, wide), lambda h, b: (0, h))
    return pl.pallas_call(
        body, grid=(ng, bl_), name=name,
        in_specs=[col(0), col(1), col(2), col(3),
                  pl.BlockSpec((hb, nch, A_DK, A_DK), lambda h, b: (b * ng + h, 0, 0, 0)), blk, vec, vec],
        out_specs=[blk, blk, blk, blk, vec, vec],
        out_shape=[SDS((n, d), bf16)] * 4 + [SDS((1, d), f32)] * 2, compiler_params=_params(),
    )(pmat, pmat, pmat, pmat, ss, dog, lb, hg)


def _shifted(x, halo, before):
    n = x.shape[0]
    both = jnp.concatenate([halo, x] if before else [x, halo], axis=0)
    row, col = _iota2((n, n + BF16_ROWS), 0), _iota2((n, n + BF16_ROWS), 1)
    src = row + BF16_ROWS if before else row
    step = -1 if before else 1
    pick = lambda s: jnp.dot((col == src + step * s).astype(bf16), both, preferred_element_type=f32)
    return pick(1), pick(2)


def _conv3(xb, halo, w):
    x = xb.astype(f32)
    x1, x2 = _shifted(xb, halo, True)
    return x, x1, x2, w[0:1, :] * x2 + w[1:2, :] * x1 + w[2:3, :] * x


def _conv_gate_fwd(ug, uv, cwg, cwv, bl_, t, tc, name):
    n, ff = ug.shape
    nt = t // tc

    def body(ug_ref, uv_ref, wg_ref, wv_ref, a_ref, hg_ref, hv_ref):
        @pl.when(pl.program_id(1) == 0)
        def _():
            hg_ref[...] = jnp.zeros_like(hg_ref)
            hv_ref[...] = jnp.zeros_like(hv_ref)

        xg, xv = ug_ref[...], uv_ref[...]
        cg = _conv3(xg, hg_ref[...], wg_ref[...])[3]
        cv = _conv3(xv, hv_ref[...], wv_ref[...])[3]
        a_ref[...] = (cg * _sigmoid(cg) * cv).astype(a_ref.dtype)
        hg_ref[...] = xg[tc - BF16_ROWS:tc, :].astype(hg_ref.dtype)
        hv_ref[...] = xv[tc - BF16_ROWS:tc, :].astype(hv_ref.dtype)

    row = pl.BlockSpec((tc, ff), lambda b, i: (b * nt + i, 0))
    wsp = pl.BlockSpec((3, ff), lambda b, i: (0, 0))
    return pl.pallas_call(
        body, grid=(bl_, nt), name=name, in_specs=[row, row, wsp, wsp], out_specs=row,
        out_shape=SDS((n, ff), bf16),
        scratch_shapes=[pltpu.VMEM((BF16_ROWS, ff), bf16), pltpu.VMEM((BF16_ROWS, ff), bf16)], compiler_params=_params(),
    )(ug, uv, cwg, cwv)


def _conv_gate_bwd(ug, uv, cwg, cwv, da, bl_, t, tc, name):
    n, ff = ug.shape
    nt = t // tc
    per = tc // BF16_ROWS

    def body(ug_ref, uv_ref, pg_ref, pv_ref, wg_ref, wv_ref, da_ref, dug_ref, duv_ref, dwg_ref, dwv_ref, ng_ref, nv_ref):
        first = jnp.logical_and(pl.program_id(0) == 0, pl.program_id(1) == 0)

        @pl.when(first)
        def _():
            dwg_ref[...] = jnp.zeros_like(dwg_ref)
            dwv_ref[...] = jnp.zeros_like(dwv_ref)

        @pl.when(pl.program_id(1) == 0)
        def _():
            ng_ref[...] = jnp.zeros_like(ng_ref)
            nv_ref[...] = jnp.zeros_like(nv_ref)

        seq_start = pl.program_id(1) == nt - 1
        dav = da_ref[...].astype(f32)

        def half(u_ref, p_ref, w_ref):
            halo = p_ref[...]
            return _conv3(u_ref[...], jnp.where(seq_start, jnp.zeros_like(halo), halo), w_ref[...])

        xg, xg1, xg2, cg = half(ug_ref, pg_ref, wg_ref)
        xv, xv1, xv2, cv = half(uv_ref, pv_ref, wv_ref)
        sg = _sigmoid(cg)
        dcg = dav * cv * (sg * (1.0 + cg * (1.0 - sg)))
        dcv = dav * (cg * sg)

        def back(dc, x, x1, x2, w_ref, nx_ref, du_ref, dw_ref):
            w = w_ref[...]
            dcb = dc.astype(bf16)
            dc1, dc2 = _shifted(dcb, nx_ref[...], False)
            du = w[2:3, :] * dc + w[1:2, :] * dc1 + w[0:1, :] * dc2
            du_ref[...] = du.astype(du_ref.dtype)
            dw_ref[0:1, :] += jnp.sum(dc * x2, axis=0, keepdims=True)
            dw_ref[1:2, :] += jnp.sum(dc * x1, axis=0, keepdims=True)
            dw_ref[2:3, :] += jnp.sum(dc * x, axis=0, keepdims=True)
            nx_ref[...] = dcb[0:BF16_ROWS, :].astype(nx_ref.dtype)

        back(dcg, xg, xg1, xg2, wg_ref, ng_ref, dug_ref, dwg_ref)
        back(dcv, xv, xv1, xv2, wv_ref, nv_ref, duv_ref, dwv_ref)

    row = pl.BlockSpec((tc, ff), lambda b, i: (b * nt + nt - 1 - i, 0))
    prev = pl.BlockSpec((BF16_ROWS, ff), lambda b, i: (jnp.maximum((b * nt + nt - 1 - i) * per - 1, 0), 0))
    wsp = pl.BlockSpec((3, ff), lambda b, i: (0, 0))
    return pl.pallas_call(
        body, grid=(bl_, nt), name=name, in_specs=[row, row, prev, prev, wsp, wsp, row],
        out_specs=[row, row, wsp, wsp],
        out_shape=[SDS((n, ff), bf16), SDS((n, ff), bf16), SDS((3, ff), f32), SDS((3, ff), f32)],
        scratch_shapes=[pltpu.VMEM((BF16_ROWS, ff), bf16), pltpu.VMEM((BF16_ROWS, ff), bf16)], compiler_params=_params(),
    )(ug, uv, ug, uv, cwg, cwv, da)


def _zf_c(hk, wzf, fgb, bl_, t, tm, name):
    n, d = hk.shape
    nt = t // tm

    def body(hk_ref, w_ref, b_ref, zf_ref, c_ref, carry_ref):
        @pl.when(pl.program_id(1) == 0)
        def _():
            carry_ref[...] = jnp.zeros_like(carry_ref)

        z = _bdot(hk_ref[...], w_ref[...]) + b_ref[...]
        ls = jnp.minimum(z, 0.0) - jnp.log(1.0 + jnp.exp(-jnp.abs(z)))
        c = _cumsum_rows(ls) + carry_ref[...]
        zf_ref[...] = z
        c_ref[...] = c
        carry_ref[...] = c[tm - 1:tm, :]

    row = lambda w: pl.BlockSpec((tm, w), lambda b, i: (b * nt + i, 0))
    return pl.pallas_call(
        body, grid=(bl_, nt), name=name,
        in_specs=[row(d), pl.BlockSpec((d, LANES), lambda b, i: (0, 0)), pl.BlockSpec((1, LANES), lambda b, i: (0, 0))],
        out_specs=[row(LANES), row(LANES)],
        out_shape=[SDS((n, LANES), f32), SDS((n, LANES), f32)],
        scratch_shapes=[pltpu.VMEM((1, LANES), f32)], compiler_params=_params(),
    )(hk, wzf, fgb)


def _c_bwd(dc, zf, bl_, t, tm, name):
    n = dc.shape[0]
    nt = t // tm

    def body(dc_ref, zf_ref, dzf_ref, dfg_ref, carry_ref):
        @pl.when(jnp.logical_and(pl.program_id(0) == 0, pl.program_id(1) == 0))
        def _():
            dfg_ref[...] = jnp.zeros_like(dfg_ref)

        @pl.when(pl.program_id(1) == 0)
        def _():
            carry_ref[...] = jnp.zeros_like(carry_ref)

        rc = _revcumsum_rows(dc_ref[...]) + carry_ref[...]
        dz = rc * _sigmoid(-zf_ref[...])
        dzf_ref[...] = dz.astype(dzf_ref.dtype)
        dfg_ref[...] += jnp.sum(dz, axis=0, keepdims=True)
        carry_ref[...] = rc[0:1, :]

    row = pl.BlockSpec((tm, LANES), lambda b, i: (b * nt + nt - 1 - i, 0))
    vec = pl.BlockSpec((1, LANES), lambda b, i: (0, 0))
    return pl.pallas_call(
        body, grid=(bl_, nt), name=name, in_specs=[row, row], out_specs=[row, vec],
        out_shape=[SDS((n, LANES), bf16), SDS((1, LANES), f32)],
        scratch_shapes=[pltpu.VMEM((1, LANES), f32)], compiler_params=_params(),
    )(dc, zf)


def _is_pow2(x):
    m, _ = math.frexp(x)
    return m == 0.5


def _prescale(qh, scale):
    return (qh.astype(f32) * scale).astype(bf16)


def _attn_fwd(q, kv, ck, bl_, t, tq, hd, name):
    n, d = q.shape
    npair = d // LANES
    hp = LANES // hd
    nq = t // tq
    scale = 1.0 / (hd ** 0.5)

    pre = _is_pow2(scale)

    def body(q_ref, k_ref, v_ref, ck_ref, o_ref, lse_ref):
        i = pl.program_id(2)
        diag = _iota2((tq, tq), 0) >= _iota2((tq, tq), 1)
        for hh in range(hp):
            lanes = slice(hh * hd, (hh + 1) * hd)
            qh = _prescale(q_ref[:, lanes], scale) if pre else q_ref[:, lanes]

            def block(j, carry, masked, lanes=lanes, qh=qh, hh=hh):
                m, l, acc = carry
                rows = pl.ds(pl.multiple_of(j * tq, BF16_ROWS), tq)
                s = _bdot_nt(qh, k_ref[rows, lanes])
                s = (s if pre else s * scale) - ck_ref[0, 0, j, hh:hh + 1, :]
                if masked:
                    s = jnp.where(diag, s, -1e30)
                m2 = jnp.maximum(m, jnp.max(s, axis=-1, keepdims=True))
                p = jnp.exp(s - m2)
                a = jnp.exp(m - m2)
                return m2, a * l + jnp.sum(p, axis=-1, keepdims=True), a * acc + _bdot(p, v_ref[rows, lanes])

            init = (jnp.full((tq, 1), -1e30, f32), jnp.zeros((tq, 1), f32), jnp.zeros((tq, hd), f32))
            carry = lax.fori_loop(0, i, functools.partial(block, masked=False), init)
            m, l, acc = block(i, carry, True)
            o_ref[:, lanes] = (acc / l).astype(o_ref.dtype)
            lse_ref[:, lanes] = jnp.broadcast_to(m + jnp.log(l), (tq, hd))

    nk = nq
    return pl.pallas_call(
        body, grid=(bl_, npair, nq), name=name,
        in_specs=[pl.BlockSpec((tq, LANES), lambda b, p, i: (b * nq + i, p)),
                  pl.BlockSpec((t, LANES), lambda b, p, i: (b, p)),
                  pl.BlockSpec((t, LANES), lambda b, p, i: (b, npair + p)),
                  pl.BlockSpec((1, 1, nk, hp, tq), lambda b, p, i: (b, p, 0, 0, 0))],
        out_specs=[pl.BlockSpec((tq, LANES), lambda b, p, i: (b * nq + i, p)),
                   pl.BlockSpec((tq, LANES), lambda b, p, i: (b * nq + i, p))],
        out_shape=[SDS((n, d), f32), SDS((n, d), f32)], compiler_params=_params(),
    )(q, kv, kv, ck)


def _attn_bwd(q, kv, o, do, lse, ck, bl_, t, tq, hd, name):
    n, d = q.shape
    npair = d // LANES
    hp = LANES // hd
    nq = t // tq
    scale = 1.0 / (hd ** 0.5)

    pre = _is_pow2(scale)

    def body(q_ref, k_ref, v_ref, o_ref, do_ref, lse_ref, ck_ref, dq_ref, dk_ref, dv_ref, dck_ref, dcq_ref):
        j = pl.program_id(2)

        @pl.when(j == 0)
        def _():
            dq_ref[...] = jnp.zeros_like(dq_ref)
            dcq_ref[...] = jnp.zeros_like(dcq_ref)

        diag = _iota2((tq, tq), 0) >= _iota2((tq, tq), 1)
        for hh in range(hp):
            lanes = slice(hh * hd, (hh + 1) * hd)
            kh = k_ref[:, lanes]
            vh = v_ref[:, lanes]
            kt = kh.astype(f32).T.astype(bf16)
            cs = ck_ref[0, 0, 0, hh:hh + 1, :]

            def block(i, carry, masked, lanes=lanes, kh=kh, vh=vh, kt=kt, cs=cs, hh=hh):
                dkt, dvt, dcs = carry
                rows = pl.ds(pl.multiple_of(i * tq, BF16_ROWS), tq)
                qh = _prescale(q_ref[rows, lanes], scale) if pre else q_ref[rows, lanes]
                doh = do_ref[rows, lanes]
                s = _bdot_nt(qh, kh)
                s = (s if pre else s * scale) - cs
                if masked:
                    s = jnp.where(diag, s, -1e30)
                p = jnp.exp(s - lse_ref[rows, hh * hd:hh * hd + 1])
                delta = jnp.sum(doh.astype(f32) * o_ref[rows, lanes].astype(f32), axis=-1, keepdims=True)
                ds = p * (_bdot_nt(doh, vh) - delta)
                dsb = ds.astype(bf16)
                dq_ref[rows, lanes] += _bdot_nt(kt, dsb).T * scale
                dcq_ref[0, rows, hh:hh + 1] += jnp.sum(ds, axis=-1, keepdims=True)
                dkq = _bdot_tn(qh, dsb)
                return (dkt + (dkq if pre else dkq * scale), dvt + _bdot_tn(doh, p), dcs - jnp.sum(ds, axis=0, keepdims=True))

            init = (jnp.zeros((hd, tq), f32), jnp.zeros((hd, tq), f32), jnp.zeros((1, tq), f32))
            dkt, dvt, dcs = lax.fori_loop(j + 1, nq, functools.partial(block, masked=False), block(j, init, True))
            dk_ref[:, lanes] = dkt.T.astype(dk_ref.dtype)
            dv_ref[:, lanes] = dvt.T.astype(dv_ref.dtype)
            dck_ref[0, 0, 0, hh:hh + 1, :] = dcs

    whole = lambda c0: pl.BlockSpec((t, LANES), lambda b, p, j: (b, c0 + p))
    tile = lambda c0: pl.BlockSpec((tq, LANES), lambda b, p, j: (b * nq + j, c0 + p))
    ckspec = pl.BlockSpec((1, 1, 1, hp, tq), lambda b, p, j: (b, p, j, 0, 0))
    cqspec = pl.BlockSpec((1, t, hp), lambda b, p, j: (p, b, 0))
    return pl.pallas_call(
        body, grid=(bl_, npair, nq), name=name,
        in_specs=[whole(0), tile(0), tile(npair), whole(0), whole(0), whole(0), ckspec],
        out_specs=[whole(0), tile(0), tile(0), ckspec, cqspec],
        out_shape=[SDS((n, d), f32), SDS((n, d), bf16), SDS((n, d), bf16), SDS((bl_, npair, nq, hp, tq), f32),
                   SDS((npair, n, hp), f32)],
        compiler_params=_params(),
    )(q, kv, kv, o, do, lse, ck)


def _loss_head(h, target, mix, g, t, nm, tm, name):
    n, d = h.shape
    nt = t // tm

    def body(h_ref, t_ref, mix_ref, g_ref, loss_ref, dh_ref, dmix_ref, dg_ref):
        i = pl.program_id(0)

        @pl.when(i == 0)
        def _():
            loss_ref[...] = jnp.zeros_like(loss_ref)
            dg_ref[...] = jnp.zeros_like(dg_ref)

        pos = (i % nt) * tm + _iota2((tm, d), 0)
        err = jnp.where(pos >= nm, h_ref[...] - t_ref[...], 0.0)
        dh = err * (1.0 / d)
        dh_ref[...] = dh
        loss_ref[...] += 0.5 * jnp.sum(jnp.mean(err * err, axis=-1, keepdims=True))
        dmix, dg = _norm_bwd_tile(mix_ref[...], g_ref[...], dh)
        dmix_ref[...] = dmix.astype(dmix_ref.dtype)
        dg_ref[...] += dg

    row = pl.BlockSpec((tm, d), lambda i: (i, 0))
    vec = pl.BlockSpec((1, d), lambda i: (0, 0))
    return pl.pallas_call(
        body, grid=(n // tm,), name=name, in_specs=[row, row, row, vec],
        out_specs=[pl.BlockSpec((8, LANES), lambda i: (0, 0)), row, row, vec],
        out_shape=[SDS((8, LANES), f32), SDS((n, d), f32), SDS((n, d), bf16), SDS((1, d), f32)], compiler_params=_params(),
    )(h, target, mix, g)


def _c_key_rows(c, bl_, t, tq, bh, hp):
    npair = bh // hp
    nk = t // tq
    return c[:, :bh].reshape(bl_, nk, tq, npair, hp).transpose(0, 3, 1, 4, 2)


def _dc_rows(dck, dcq, bl_, t, bh):
    d = dck.transpose(0, 2, 4, 1, 3).reshape(bl_ * t, bh) + dcq.transpose(1, 0, 2).reshape(bl_ * t, bh)
    return jnp.pad(d, ((0, 0), (0, LANES - bh)))


_ANY = pl.BlockSpec(memory_space=pl.ANY)


def _all_gather(xs, name):
    na = len(xs)

    def body(*refs):
        x_refs, out_refs = refs[:na], refs[na:2 * na]
        send_sems, recv_sems, local_sems = refs[2 * na:]
        mx, my, mc = lax.axis_index("x"), lax.axis_index("y"), lax.axis_index("c")
        me, sibling = (mx, my, mc), (mx, my, 1 - mc)
        chips = [(1 - mx, my), (mx, 1 - my), (1 - mx, 1 - my)]

        def copy(a, k, block, to, own=False):
            px, py, pc = block
            rows = out_refs[a].at[4 * px + 2 * py + pc]
            return pltpu.make_async_remote_copy(
                src_ref=x_refs[a] if own else rows, dst_ref=rows,
                send_sem=send_sems.at[a, k], recv_sem=recv_sems.at[a, k], device_id=to, device_id_type=MESH)

        arrays = range(na)
        mine = [pltpu.make_async_copy(x_refs[a], out_refs[a].at[4 * mx + 2 * my + mc], local_sems.at[a]) for a in arrays]
        for cp in mine:
            cp.start()
        first = [copy(a, 1 + j, me, (*chip, mc), own=True) for j, chip in enumerate(chips) for a in arrays]
        first += [copy(a, 0, me, sibling, own=True) for a in arrays]
        for cp in first:
            cp.start()
        passed = []
        for j, chip in enumerate(chips):
            for a in arrays:
                copy(a, 1 + j, (*chip, mc), me).wait_recv()
                cp = copy(a, 4 + j, (*chip, mc), sibling)
                cp.start()
                passed.append(cp)
        for a in arrays:
            copy(a, 0, sibling, me).wait_recv()
        for j, chip in enumerate(chips):
            for a in arrays:
                copy(a, 4 + j, (*chip, 1 - mc), me).wait_recv()
        for cp in first + passed:
            cp.wait_send()
        for cp in mine:
            cp.wait()

    return pl.pallas_call(
        body, name=name, out_shape=[SDS((N_DEV,) + x.shape, x.dtype) for x in xs],
        in_specs=[_ANY] * na, out_specs=[_ANY] * na,
        scratch_shapes=[pltpu.SemaphoreType.DMA((na, 7)), pltpu.SemaphoreType.DMA((na, 7)), pltpu.SemaphoreType.DMA((na,))],
    )(*xs)


_HBM = pl.BlockSpec(memory_space=pltpu.HBM)
_SEM = pl.BlockSpec(memory_space=pltpu.SEMAPHORE)
_DATAFLOW = pltpu.SideEffectType.DATAFLOW_SIDE_EFFECTING
N_PEERS = N_DEV - 1


def _device_index():
    return 4 * lax.axis_index("x") + 2 * lax.axis_index("y") + lax.axis_index("c")


def _peers():
    mx, my, mc = lax.axis_index("x"), lax.axis_index("y"), lax.axis_index("c")
    peers = []
    for r in (2, 3, 4, 5, 6, 7, 1):
        px = 1 - mx if r & 4 else mx
        py = 1 - my if r & 2 else my
        pc = 1 - mc if r & 1 else mc
        peers.append(((px, py, pc), 4 * px + 2 * py + pc))
    return 4 * mx + 2 * my + mc, peers


def _push_copy(src_ref, land_ref, send_sems, recv_sems, a, k, dev, src_row, land_row, scatter):
    return pltpu.make_async_remote_copy(
        src_ref=src_ref.at[src_row] if scatter else src_ref, dst_ref=land_ref.at[land_row],
        send_sem=send_sems.at[a * N_PEERS + k], recv_sem=recv_sems.at[a * N_PEERS + k], device_id=dev, device_id_type=MESH)


def _landing(own, me):
    return lax.dynamic_update_index_in_dim(lax.empty((N_DEV,) + own.shape, own.dtype), own, me, 0)


def _push_start(srcs, lands, scatter, name):
    na = len(srcs)

    def body(*refs):
        src_refs, land_refs = refs[:na], refs[na:2 * na]
        send_sems, recv_sems = refs[2 * na], refs[2 * na + 1]
        token = refs[-1]
        me, peers = _peers()
        for a in range(na):
            for k, (dev, idx) in enumerate(peers):
                _push_copy(src_refs[a], land_refs[a], send_sems, recv_sems, a, k, dev, idx, me, scatter).start()
        token[...] = jnp.zeros_like(token)

    hbm = lambda arrs: [pltpu.HBM(a.shape, a.dtype) for a in arrs]
    out = pl.pallas_call(
        body, name=name,
        out_shape=(pltpu.SemaphoreType.DMA((na * N_PEERS,)), pltpu.SemaphoreType.DMA((na * N_PEERS,)), *hbm(srcs), *hbm(lands),
                   SDS((8, LANES), f32)),
        in_specs=[_HBM] * (2 * na),
        out_specs=(_SEM, _SEM, *([_HBM] * (2 * na)), pl.BlockSpec(memory_space=pltpu.VMEM)),
        input_output_aliases={i: 2 + i for i in range(2 * na)},
        compiler_params=pltpu.CompilerParams(has_side_effects=_DATAFLOW),
    )(*[pltpu.with_memory_space_constraint(a, pltpu.HBM) for a in list(srcs) + list(lands)])
    return out[0], out[1], list(out[2:2 + na]), list(out[2 + na:2 + 2 * na]), out[-1]


def _push_wait(send_sems, recv_sems, srcs, lands, which, after, scatter, name):
    nw = len(which)

    def body(*refs):
        src_refs, land_refs = refs[:nw], refs[nw:2 * nw]
        send_sems_, recv_sems_ = refs[2 * nw], refs[2 * nw + 1]
        _, peers = _peers()
        for j, a in enumerate(which):
            for k, (dev, idx) in enumerate(peers):
                cp = _push_copy(src_refs[j], land_refs[j], send_sems_, recv_sems_, a, k, dev, idx, idx, scatter)
                cp.wait_send()
                cp.wait_recv()

    hbm = lambda arrs: [pltpu.HBM(a.shape, a.dtype) for a in arrs]
    out = pl.pallas_call(
        body, name=name, out_shape=(*hbm(srcs), *hbm(lands)),
        in_specs=[_HBM] * (2 * nw) + [_SEM, _SEM, _ANY], out_specs=[_HBM] * (2 * nw),
        input_output_aliases={i: i for i in range(2 * nw)},
        compiler_params=pltpu.CompilerParams(has_side_effects=_DATAFLOW),
    )(*srcs, *lands, send_sems, recv_sems, after)
    return list(out[nw:])


def _adamw(parts, w, m, v, tr, name):
    g, r, c = parts.shape

    def body(p_ref, w_ref, m_ref, v_ref, g_ref, d_ref, m2_ref, v2_ref):
        gr = p_ref[0].astype(f32)
        for k in range(1, g):
            gr = gr + p_ref[k].astype(f32)
        m2 = ADAM_B1 * m_ref[...] + (1.0 - ADAM_B1) * gr
        v2 = ADAM_B2 * v_ref[...] + (1.0 - ADAM_B2) * (gr * gr)
        m_hat = m2 / (1.0 - ADAM_B1 ** ADAM_STEP)
        v_hat = v2 / (1.0 - ADAM_B2 ** ADAM_STEP)
        g_ref[...] = gr
        d_ref[...] = -ADAM_LR * (m_hat / (jnp.sqrt(v_hat) + ADAM_EPS) + ADAM_WD * w_ref[...])
        m2_ref[...] = m2
        v2_ref[...] = v2

    row = pl.BlockSpec((tr, c), lambda i: (i, 0))
    return pl.pallas_call(
        body, grid=(r // tr,), name=name, in_specs=[pl.BlockSpec((g, tr, c), lambda i: (0, i, 0)), row, row, row],
        out_specs=[row] * 4, out_shape=[SDS((r, c), f32)] * 4, compiler_params=_params(),
    )(parts, w, m, v)


_SHARD_AXIS = dict(meta_tokens=1, norm_gains=2, a_w_in=2, a_lb_logits=1, a_head_norm=1, a_w_out=1, kv_w=1,
                   b_w_q=1, b_w_out=1, ffn_w_up=2, ffn_conv=2, ffn_w_down=1)
_VECTORS = ("meta_tokens", "norm_gains", "a_lb_logits", "a_head_norm", "ffn_conv")
_REPLICATED = ("kv_norm", "fg_b")
_ROW_TILE_CAP = 512


def _pack(arrs, dtype, cols, row_mult):
    lead = arrs[0].shape[:-1] if arrs[0].ndim > 1 else ()
    flat = jnp.concatenate([a.astype(dtype) for a in arrs], axis=-1)
    size = flat.shape[-1]
    per = cols * row_mult
    total = -(-size // per) * per
    flat = jnp.pad(flat, [(0, 0)] * len(lead) + [(0, total - size)])
    return flat.reshape(lead + (total // cols, cols))


def _unpack(flat, shapes):
    out, off = [], 0
    lead = flat.shape[:-1]
    for shp in shapes:
        size = 1
        for s in shp:
            size *= s
        out.append(flat[..., off:off + size].reshape(lead + tuple(shp)))
        off += size
    return out


def _unshard(seg, axis):
    a = jnp.moveaxis(seg, 0, axis)
    shp = a.shape
    return a.reshape(shp[:axis] + (shp[axis] * shp[axis + 1],) + shp[axis + 2:])


def _shard8(full, axis):
    shp = full.shape
    a = full.reshape(shp[:axis] + (N_DEV, shp[axis] // N_DEV) + shp[axis + 1:])
    return jnp.moveaxis(a, axis, 0)


def _rows(a, lead=0):
    return a.reshape(a.shape[:lead] + (-1, a.shape[-1]))


def kernel(x, meta_tokens, norm_gains, a_w_in, a_lb_logits, a_head_norm, a_w_out, kv_norm, kv_w, fg_b, b_w_q, b_w_out, ffn_w_up, ffn_conv, ffn_w_down, loss_target, m_meta_tokens, m_norm_gains, m_a_w_in, m_a_lb_logits, m_a_head_norm, m_a_w_out, m_kv_norm, m_kv_w, m_fg_b, m_b_w_q, m_b_w_out, m_ffn_w_up, m_ffn_conv, m_ffn_w_down, v_meta_tokens, v_norm_gains, v_a_w_in, v_a_lb_logits, v_a_head_norm, v_a_w_out, v_kv_norm, v_kv_w, v_fg_b, v_b_w_q, v_b_w_out, v_ffn_w_up, v_ffn_conv, v_ffn_w_down):
    names = ("meta_tokens", "norm_gains", "a_w_in", "a_lb_logits", "a_head_norm", "a_w_out", "kv_norm", "kv_w", "fg_b",
             "b_w_q", "b_w_out", "ffn_w_up", "ffn_conv", "ffn_w_down")
    w = dict(zip(names, (meta_tokens, norm_gains, a_w_in, a_lb_logits, a_head_norm, a_w_out, kv_norm, kv_w, fg_b,
                         b_w_q, b_w_out, ffn_w_up, ffn_conv, ffn_w_down)))
    mom = dict(zip(names, (m_meta_tokens, m_norm_gains, m_a_w_in, m_a_lb_logits, m_a_head_norm, m_a_w_out, m_kv_norm,
                           m_kv_w, m_fg_b, m_b_w_q, m_b_w_out, m_ffn_w_up, m_ffn_conv, m_ffn_w_down)))
    var = dict(zip(names, (v_meta_tokens, v_norm_gains, v_a_w_in, v_a_lb_logits, v_a_head_norm, v_a_w_out, v_kv_norm,
                           v_kv_w, v_fg_b, v_b_w_q, v_b_w_out, v_ffn_w_up, v_ffn_conv, v_ffn_w_down)))

    bl_, seq, d = x.shape
    nm = meta_tokens.shape[0]
    t = nm + seq
    n = bl_ * t
    bh = fg_b.shape[0]
    hd = d // bh
    hp = LANES // hd
    ff = ffn_w_down.shape[1] * N_DEV
    tm = _div_tile(t, TOKEN_TILE_CAP)
    tc = _div_tile(t, 64)
    tn = min(d, MODEL_TILE_CAP)

    me = _device_index()
    vec_pack = _pack([w[k].reshape(-1) for k in _VECTORS], f32, LANES, 8)
    first_src = [w["a_w_in"].astype(bf16), vec_pack]
    f_send, f_recv, first_src, first_land, first_token = _push_start(
        first_src, [_landing(a, me) for a in first_src], False, "gather_first_start")
    later_names = ("a_w_out", "ffn_w_up", "ffn_w_down", "kv_w", "b_w_q", "b_w_out", "ffn_w_up", "ffn_w_down")
    later_layer = (None, 0, 0, None, None, None, 1, 1)
    later = [(w[k] if l is None else w[k][l]).astype(bf16) for k, l in zip(later_names, later_layer)]
    later, _ = lax.optimization_barrier((later, first_token))
    g_send, g_recv, later_src, later_land, rest_token = _push_start(
        later, [_landing(a, me) for a in later], False, "gather_rest_start")
    target = jnp.concatenate([jnp.zeros((bl_, nm, d), f32) + rest_token[0, 0], loss_target], axis=1).reshape(n, d)
    first = _push_wait(f_send, f_recv, first_src, first_land, (0, 1), target, False, "gather_first_wait")
    vec_segs = _unpack(first[1].reshape(N_DEV, -1), [w[k].shape for k in _VECTORS])
    small = {k: _unshard(a, _SHARD_AXIS[k]) for k, a in zip(_VECTORS, vec_segs)}
    w_in = _unshard(first[0], _SHARD_AXIS["a_w_in"])[0]

    def gathered(which, after, name):
        lands = _push_wait(g_send, g_recv, [later_src[i] for i in which], [later_land[i] for i in which], which, after,
                           False, name)
        return [_unshard(a, _SHARD_AXIS[later_names[i]] - (later_layer[i] is not None)) for i, a in zip(which, lands)]

    gains_box = [small["norm_gains"]]
    gain = lambda l, j: gains_box[0][l, j][None]
    cw_gate, cw_val = small["ffn_conv"][:, :, :ff], small["ffn_conv"][:, :, ff:]
    head_gain = small["a_head_norm"]
    lb = jax.nn.softmax(small["a_lb_logits"], axis=0)[0:1]
    kvn = kv_norm[None]
    fgb_pad = jnp.pad(fg_b, (0, LANES - bh))[None]

    h0 = jnp.concatenate([jnp.broadcast_to(small["meta_tokens"][None], (bl_, nm, d)), x], axis=1).reshape(n, d)

    def ffn_fwd(l, h_in, fi, next_gains, which):
        w_up, = gathered(which[:1], h_in, f"gather_wait_ffn{l}_up")
        w_gate[l], w_val[l] = w_up[:, :ff], w_up[:, ff:]
        ug = _mm(fi, w_gate[l], bf16, tm, ff, f"ffn{l}_up_gate")
        uv = _mm(fi, w_val[l], bf16, tm, ff, f"ffn{l}_up_val")
        w_down[l], = gathered(which[1:], uv, f"gather_wait_ffn{l}_down")
        act = _conv_gate_fwd(ug, uv, cw_gate[l], cw_val[l], bl_, t, tc, f"ffn{l}_conv_gate")
        h_out, mix, *normed = _mm_norm_res(act, w_down[l], gain(l, 3), h_in, next_gains, tm, f"ffn{l}_down")
        return h_out, (h_in, fi, ug, uv, act, mix), normed

    hn0 = _rms_fwd(h0, gain(0, 0), tm, "a_norm")
    pmat = _mm(hn0, w_in, f32, tm, tn, "a_in_proj")
    og, states = _gla_fwd(pmat, lb, head_gain, bl_, t, nm, "a_gla_fwd")
    w_out_a = gathered((0,), og, "gather_wait_a")[0][0]
    h1, mix_a, fi0 = _mm_norm_res(og, w_out_a, gain(0, 1), h0, [gain(0, 2)], tm, "a_out_proj")
    w_gate, w_val, w_down = {}, {}, {}
    h2, ffn0, (hk, hn1) = ffn_fwd(0, h1, fi0, [kvn, gain(1, 0)], (1, 2))

    w_kv_zf, w_q, w_out_b = gathered((3, 4, 5), h2, "gather_wait_b")
    w_kv, w_zf = w_kv_zf[:, :2 * d], jnp.pad(w_kv_zf[:, 2 * d:], ((0, 0), (0, LANES - bh)))
    w_q, w_out_b = w_q[0], w_out_b[0]
    kvp = _mm(hk, w_kv, bf16, tm, tn, "kv_proj")
    zf, cum = _zf_c(hk, w_zf, fgb_pad, bl_, t, tm, "forget_cumsum")
    ck = _c_key_rows(cum, bl_, t, tm, bh, hp)
    q = _mm(hn1, w_q, bf16, tm, tn, "b_q_proj")
    o, lse = _attn_fwd(q, kvp, ck, bl_, t, tm, hd, "b_attn_fwd")
    h3, mix_b, fi1 = _mm_norm_res(o, w_out_b, gain(1, 1), h2, [gain(1, 2)], tm, "b_out_proj")
    h4, ffn1, _ = ffn_fwd(1, h3, fi1, [], (6, 7))

    dgain = {}
    loss8, dh, dmix, dgain[1, 3] = _loss_head(h4, target, ffn1[5], gain(1, 3), t, nm, tm, "loss_head")
    loss = lax.psum(loss8[0, 0], ("x", "y", "c"))

    def ffn_bwd(l, saved, dh_out, dmix, below):
        h_in, fi, ug, uv, act, mix = saved
        dact = _mm_nt([(dmix, w_down[l])], bf16, tm, ff, f"ffn{l}_down_dx")
        dw_down = _mm_tn(act, dmix, tm, ff, tn, f"ffn{l}_down_dw")
        dug, duv, dcg, dcv = _conv_gate_bwd(ug, uv, cw_gate[l], cw_val[l], dact, bl_, t, tc, f"ffn{l}_conv_gate_bwd")
        dfi = _mm_nt([(dug, w_gate[l]), (duv, w_val[l])], bf16, tm, tn // 2, f"ffn{l}_up_dx")
        dw_up = jnp.concatenate([_mm_tn(fi, dug, tm, tn, ff, f"ffn{l}_up_gate_dw"),
                                 _mm_tn(fi, duv, tm, tn, ff, f"ffn{l}_up_val_dw")], axis=1)
        mix_below, gain_below, key_below = below
        dh_in, dgain[l, 2], dmix_below, dgain[key_below] = _rms_bwd(
            h_in, gain(l, 2), dfi, dh_out, f32, tm, f"ffn{l}_norm_bwd", then=(mix_below, gain_below))
        return dh_in, dmix_below, dw_up, jnp.concatenate([dcg, dcv], axis=1), dw_down

    def shards(full, axis):
        return _rows(_shard8(full, axis), 1).astype(bf16)

    def push_grads(bufs, name, tie=None):
        lands = [_landing(lax.dynamic_index_in_dim(b, me, 0, keepdims=False), me) for b in bufs]
        s_sem, r_sem, srcs, lands, token = _push_start(bufs, lands, True, name)
        gains_box[0] = gains_box[0] + token[0, 0]
        return (s_sem, r_sem, srcs, lands), (None if tie is None else tie + token[0, 0])

    def landed(handle, after, name):
        s_sem, r_sem, srcs, lands = handle
        return _push_wait(s_sem, r_sem, srcs, lands, tuple(range(len(srcs))), after, True, name)

    dh, dmix, dw_up1, dconv1, dw_down1 = ffn_bwd(1, ffn1, dh, dmix, (mix_b, gain(1, 1), (1, 1)))
    push1, ck = push_grads([shards(dw_up1, 1), shards(dw_down1, 0)], "grad_push_ffn1", ck)

    do = _mm_nt([(dmix, w_out_b)], bf16, tm, tn, "b_out_dx")
    dw_out_b = _mm_tn(o, dmix, tm, tn, tn, "b_out_dw")
    dq, dk, dv, dck, dcq = _attn_bwd(q, kvp, o, do, lse, ck, bl_, t, tm, hd, "b_attn_bwd")
    dhn1 = _mm_nt([(dq, w_q)], bf16, tm, tn, "b_q_dx")
    dw_q = _mm_tn(hn1, dq, tm, tn, tn, "b_q_dw")
    dh, dgain[1, 0] = _rms_bwd(h2, gain(1, 0), dhn1, dh, f32, tm, "b_norm_bwd")

    dzf, dfgb = _c_bwd(_dc_rows(dck, dcq, bl_, t, bh), zf, bl_, t, tm, "forget_cumsum_bwd")
    dhk = _mm_nt([(dk, w_kv[:, :d]), (dv, w_kv[:, d:]), (dzf, w_zf)], bf16, tm, tn, "kv_dx")
    dw_kv = jnp.concatenate([_mm_tn(hk, dk, tm, tn, tn, "k_dw"), _mm_tn(hk, dv, tm, tn, tn, "v_dw"),
                             _mm_tn(hk, dzf, tm, tn, LANES, "zf_dw")[:, :bh]], axis=1)
    dh, dkvn, dmix, dgain[0, 3] = _rms_bwd(h2, kvn, dhk, dh, f32, tm, "kv_norm_bwd", then=(ffn0[5], gain(0, 3)))
    push2, cw_gate = push_grads([shards(dw_out_b, 0), shards(dw_q, 0), shards(dw_kv, 1)], "grad_push_b", cw_gate)

    dh, dmix, dw_up0, dconv0, dw_down0 = ffn_bwd(0, ffn0, dh, dmix, (mix_a, gain(0, 1), (0, 1)))
    push3, head_gain = push_grads([shards(dw_up0, 1), shards(dw_down0, 0)], "grad_push_ffn0", head_gain)

    dog = _mm_nt([(dmix, w_out_a)], bf16, tm, tn, "a_out_dx")
    dw_out_a = _mm_tn(og, dmix, tm, tn, tn, "a_out_dw")
    dpq, dpf, dpi, dpg, dlb, dhg = _gla_bwd(pmat, states, dog, lb, head_gain, bl_, t, nm, "a_gla_bwd")
    dps = (dpq, dpf, dpi, dpg)
    dw_in = jnp.concatenate([_mm_tn(hn0, dp, tm, tn, tn, f"a_in_dw{j}") for j, dp in enumerate(dps)], axis=1)
    push4, _ = push_grads([shards(dw_out_a, 0), shards(dw_in, 1)], "grad_push_a")
    dhn0 = _mm_nt([(dp, w_in[:, j * d:(j + 1) * d]) for j, dp in enumerate(dps)], bf16, tm, tn, "a_in_dx")
    dh, dgain[0, 0] = _rms_bwd(h0, gain(0, 0), dhn0, dh, f32, tm, "a_norm_bwd")

    dh = dh.reshape(bl_, t, d)
    grad_x = dh[:, nm:]
    dl0 = dlb * lb * (1.0 - lb)
    vec_grads = dict(
        meta_tokens=jnp.sum(dh[:, :nm], axis=0),
        norm_gains=jnp.stack([jnp.concatenate([dgain[l, j] for j in range(4)], axis=0) for l in range(2)]),
        a_lb_logits=jnp.concatenate([dl0, -dl0], axis=0), a_head_norm=dhg, ffn_conv=jnp.stack([dconv0, dconv1]))
    vec_send = _pack([_shard8(vec_grads[k], _SHARD_AXIS[k]).reshape(N_DEV, -1) for k in _VECTORS], bf16, LANES, BF16_ROWS)
    push5, _ = push_grads([vec_send], "grad_push_vectors")

    g_s, d_s, m_s, v_s = {}, {}, {}, {}
    outs = (g_s, d_s, m_s, v_s)

    def update(part, srcs, label):
        rows = part.shape[1]
        return _adamw(part, *srcs, rows if rows <= _ROW_TILE_CAP else _div_tile(rows, _ROW_TILE_CAP), label)

    def update_matrix(k, part, layer=None):
        pick = (lambda a: a) if layer is None else (lambda a: a[layer])
        label = f"adamw_{k}" if layer is None else f"adamw_{k}{layer}"
        res = update(part, [_rows(pick(src[k])) for src in (w, mom, var)], label)
        return [r.reshape(pick(w[k]).shape) for r in res]

    def put(k, res):
        for dst, r in zip(outs, res):
            dst[k] = r

    up1, down1 = (update_matrix(k, p, 1) for k, p in zip(("ffn_w_up", "ffn_w_down"), landed(push1, gains_box[0], "grad_wait_ffn1")))
    for k, p in zip(("b_w_out", "b_w_q", "kv_w"), landed(push2, up1[0], "grad_wait_b")):
        put(k, update_matrix(k, p))
    up0, down0 = (update_matrix(k, p, 0) for k, p in zip(("ffn_w_up", "ffn_w_down"), landed(push3, g_s["kv_w"], "grad_wait_ffn0")))
    put("ffn_w_up", [jnp.stack(pair) for pair in zip(up0, up1)])
    put("ffn_w_down", [jnp.stack(pair) for pair in zip(down0, down1)])
    part_out_a, part_in = landed(push4, down0[0], "grad_wait_a")
    put("a_w_out", update_matrix("a_w_out", part_out_a))
    put("a_w_in", update_matrix("a_w_in", part_in))
    part_vec, = landed(push5, g_s["a_w_in"], "grad_wait_vectors")
    vec_packs = [_pack([src[k].reshape(-1) for k in _VECTORS], f32, LANES, BF16_ROWS) for src in (w, mom, var)]
    vec_shapes = [w[k].shape for k in _VECTORS]
    for dst, r in zip(outs, update(part_vec, vec_packs, "adamw_vectors")):
        dst.update(zip(_VECTORS, _unpack(r.reshape(-1), vec_shapes)))

    rep_local = _pack([dkvn.reshape(-1), dfgb[0, :bh]], f32, LANES, 8)
    rep_parts = _all_gather([rep_local], "gather_replicated_grads")[0]
    rpacks = [_pack([src[k].reshape(-1) for k in _REPLICATED], f32, LANES, 8) for src in (w, mom, var)]
    rres = _adamw(rep_parts, *rpacks, rep_local.shape[0], "adamw_replicated")
    rshapes = [w[k].shape for k in _REPLICATED]
    g_r, d_r, m_r, v_r = ({k: a for k, a in zip(_REPLICATED, _unpack(r.reshape(-1), rshapes))} for r in rres)

    out = [loss, grad_x]
    for sh, rp in ((g_s, g_r), (d_s, d_r), (m_s, m_r), (v_s, v_r)):
        out += [sh[k] if k in sh else rp[k] for k in names]
    return tuple(out)
```

```python
import functools
import math

import jax
import jax.numpy as jnp
from jax import lax
from jax.experimental import pallas as pl
from jax.experimental.pallas import tpu as pltpu

f32 = jnp.float32
bf16 = jnp.bfloat16
SDS = jax.ShapeDtypeStruct

EPS = 1e-6
A_DK = 128
A_CHUNK = 64
GLA_GROUP = 4
GLA_HEADS = 2
TOKEN_TILE_CAP = 1024
MODEL_TILE_CAP = 1024
LANES = 128
SUBLANES = 8
BF16_ROWS = 16
VMEM_LIMIT = 56 * 1024 * 1024
ADAM_LR, ADAM_B1, ADAM_B2, ADAM_EPS, ADAM_WD, ADAM_STEP = 0.001, 0.9, 0.999, 1e-08, 0.01, 10
N_DEV = 8
MESH = pl.DeviceIdType.MESH

_NT = (((1,), (1,)), ((), ()))
_TN = (((0,), (0,)), ((), ()))
_HI = lax.Precision.HIGHEST


def _params(**kw):
    return pltpu.CompilerParams(vmem_limit_bytes=VMEM_LIMIT, **kw)


def _div_tile(n, cap, mult=BF16_ROWS):
    best = None
    for t in range(mult, min(n, cap) + 1, mult):
        if n % t == 0:
            best = t
    assert best is not None, (n, cap, mult)
    return best


def _bdot(a, b):
    return jnp.dot(a.astype(bf16), b.astype(bf16), preferred_element_type=f32)


def _bdot_nt(a, b):
    return lax.dot_general(a.astype(bf16), b.astype(bf16), _NT, preferred_element_type=f32)


def _bdot_tn(a, b):
    return lax.dot_general(a.astype(bf16), b.astype(bf16), _TN, preferred_element_type=f32)


def _iota2(shape, axis):
    return lax.broadcasted_iota(jnp.int32, shape, axis)


def _cumsum_rows(x):
    n = x.shape[0]
    tri = (_iota2((n, n), 0) >= _iota2((n, n), 1)).astype(f32)
    return jnp.dot(tri, x, precision=_HI, preferred_element_type=f32)


def _revcumsum_rows(x):
    n = x.shape[0]
    tri = (_iota2((n, n), 1) >= _iota2((n, n), 0)).astype(f32)
    return jnp.dot(tri, x, precision=_HI, preferred_element_type=f32)


def _sigmoid(x):
    return 1.0 / (1.0 + jnp.exp(-x))


def _rms_fwd(x, g, tm, name):
    n, d = x.shape

    def body(x_ref, g_ref, o_ref):
        xv = x_ref[...]
        r = lax.rsqrt(jnp.mean(xv * xv, axis=-1, keepdims=True) + EPS)
        o_ref[...] = (xv * r * g_ref[...]).astype(o_ref.dtype)

    return pl.pallas_call(
        body, grid=(n // tm,), name=name,
        in_specs=[pl.BlockSpec((tm, d), lambda i: (i, 0)), pl.BlockSpec((1, d), lambda i: (0, 0))],
        out_specs=pl.BlockSpec((tm, d), lambda i: (i, 0)),
        out_shape=SDS((n, d), bf16), compiler_params=_params(),
    )(x, g)


def _mm(a, w, out_dtype, tm, tn, name):
    n, k = a.shape
    m = w.shape[1]

    def body(a_ref, w_ref, o_ref):
        o_ref[...] = _bdot(a_ref[...], w_ref[...]).astype(o_ref.dtype)

    return pl.pallas_call(
        body, grid=(m // tn, n // tm), name=name,
        in_specs=[pl.BlockSpec((tm, k), lambda j, i: (i, 0)), pl.BlockSpec((k, tn), lambda j, i: (0, j))],
        out_specs=pl.BlockSpec((tm, tn), lambda j, i: (i, j)),
        out_shape=SDS((n, m), out_dtype), compiler_params=_params(),
    )(a, w)


def _mm_norm_res(a, w, g, h, next_gains, tm, name):
    n, k = a.shape
    d = w.shape[1]
    nn = len(next_gains)

    def body(a_ref, w_ref, g_ref, h_ref, *rest):
        ng_refs, (hn_ref, mix_ref), out_refs = rest[:nn], rest[nn:nn + 2], rest[nn + 2:]
        mix = _bdot(a_ref[...], w_ref[...])
        r = lax.rsqrt(jnp.mean(mix * mix, axis=-1, keepdims=True) + EPS)
        mix_ref[...] = mix
        hn = h_ref[...] + mix * r * g_ref[...]
        hn_ref[...] = hn
        if nn:
            rn = lax.rsqrt(jnp.mean(hn * hn, axis=-1, keepdims=True) + EPS)
            for ng_ref, o_ref in zip(ng_refs, out_refs):
                o_ref[...] = (hn * rn * ng_ref[...]).astype(o_ref.dtype)

    row = pl.BlockSpec((tm, d), lambda i: (i, 0))
    vec = pl.BlockSpec((1, d), lambda i: (0, 0))
    return pl.pallas_call(
        body, grid=(n // tm,), name=name,
        in_specs=[pl.BlockSpec((tm, k), lambda i: (i, 0)), pl.BlockSpec((k, d), lambda i: (0, 0)), vec, row] + [vec] * nn,
        out_specs=[row] * (2 + nn),
        out_shape=[SDS((n, d), f32), SDS((n, d), f32)] + [SDS((n, d), bf16)] * nn, compiler_params=_params(),
    )(a, w, g, h, *next_gains)


def _norm_bwd_tile(xv, gv, dyv):
    r = lax.rsqrt(jnp.mean(xv * xv, axis=-1, keepdims=True) + EPS)
    xr = xv * r
    gdy = dyv * gv
    return r * gdy - xr * (r * r) * jnp.mean(xv * gdy, axis=-1, keepdims=True), jnp.sum(dyv * xr, axis=0, keepdims=True)


def _rms_bwd(x, g, dy, dh_in, out_dtype, tm, name, then=None):
    n, d = x.shape
    has_add = dh_in is not None
    has_next = then is not None

    norm_bwd = _norm_bwd_tile

    def body(*refs):
        refs = list(refs)
        x_ref, g_ref, dy_ref = refs[:3]
        dh_ref = refs[3] if has_add else None
        ins_end = 3 + has_add + 2 * has_next
        o_ref, dg_ref = refs[ins_end:ins_end + 2]
        dx, dg = norm_bwd(x_ref[...], g_ref[...], dy_ref[...].astype(f32))
        if has_add:
            dx = dx + dh_ref[...]
        o_ref[...] = dx.astype(o_ref.dtype)

        @pl.when(pl.program_id(0) == 0)
        def _():
            for ref in refs[ins_end + 1::2]:
                ref[...] = jnp.zeros_like(ref)

        dg_ref[...] += dg
        if has_next:
            x2_ref, g2_ref = refs[ins_end - 2:ins_end]
            o2_ref, dg2_ref = refs[ins_end + 2:]
            dx2, dg2 = norm_bwd(x2_ref[...], g2_ref[...], dx)
            o2_ref[...] = dx2.astype(o2_ref.dtype)
            dg2_ref[...] += dg2

    row = pl.BlockSpec((tm, d), lambda i: (i, 0))
    vec = pl.BlockSpec((1, d), lambda i: (0, 0))
    ins = [x, g, dy] + ([dh_in] if has_add else []) + (list(then) if has_next else [])
    return pl.pallas_call(
        body, grid=(n // tm,), name=name,
        in_specs=[row, vec, row] + ([row] if has_add else []) + ([row, vec] if has_next else []),
        out_specs=[row, vec] + ([row, vec] if has_next else []),
        out_shape=[SDS((n, d), out_dtype), SDS((1, d), f32)] + ([SDS((n, d), bf16), SDS((1, d), f32)] if has_next else []),
        compiler_params=_params(),
    )(*ins)


def _mm_nt(pairs, out_dtype, tm, tk, name):
    n = pairs[0][0].shape[0]
    k = pairs[0][1].shape[0]
    np_ = len(pairs)

    def body(*refs):
        o_ref = refs[-1]
        acc = None
        for p in range(np_):
            t = _bdot_nt(refs[2 * p][...], refs[2 * p + 1][...])
            acc = t if acc is None else acc + t
        o_ref[...] = acc.astype(o_ref.dtype)

    in_specs, ins = [], []
    for dy, w in pairs:
        m = dy.shape[1]
        in_specs += [pl.BlockSpec((tm, m), lambda j, i: (i, 0)), pl.BlockSpec((tk, m), lambda j, i: (j, 0))]
        ins += [dy, w]
    return pl.pallas_call(
        body, grid=(k // tk, n // tm), name=name, in_specs=in_specs,
        out_specs=pl.BlockSpec((tm, tk), lambda j, i: (i, j)),
        out_shape=SDS((n, k), out_dtype), compiler_params=_params(),
    )(*ins)


def _mm_tn(x, dy, tm, tk, tn, name):
    n, k = x.shape
    m = dy.shape[1]

    def body(x_ref, dy_ref, o_ref):
        @pl.when(pl.program_id(2) == 0)
        def _():
            o_ref[...] = jnp.zeros_like(o_ref)

        o_ref[...] += _bdot_tn(x_ref[...], dy_ref[...])

    return pl.pallas_call(
        body, grid=(k // tk, m // tn, n // tm), name=name,
        in_specs=[pl.BlockSpec((tm, tk), lambda a, b, i: (i, a)), pl.BlockSpec((tm, tn), lambda a, b, i: (i, b))],
        out_specs=pl.BlockSpec((tk, tn), lambda a, b, i: (a, b)),
        out_shape=SDS((k, m), f32), compiler_params=_params(),
    )(x, dy)


def _split3(x):
    hi = x.astype(bf16)
    r = x - hi.astype(f32)
    mid = r.astype(bf16)
    return hi, mid, (r - mid.astype(f32)).astype(bf16)


def _mask_dot(mask, x):
    hi, mid, lo = _split3(x)
    dot = lambda p: jnp.dot(mask, p, preferred_element_type=f32)
    return dot(hi) + dot(mid) + dot(lo)


_BNN = (((2,), (1,)), ((0,), (0,)))
_BNT = (((2,), (2,)), ((0,), (0,)))
_BTN = (((1,), (1,)), ((0,), (0,)))


def _hdot(a, b, dims):
    return lax.dot_general(a.astype(bf16), b.astype(bf16), dims, preferred_element_type=f32)


def _heads(x, nhb):
    return jnp.stack([x[:, h * A_DK:(h + 1) * A_DK] for h in range(nhb)])


def _mask_dot_heads(mask, x):
    return jnp.stack([_mask_dot(mask, x[h]) for h in range(x.shape[0])])


def _chunk_rows(parts, cl):
    tiles = [jnp.broadcast_to(p, (p.shape[0], cl, p.shape[2])) for p in parts]
    return tiles[0] if len(tiles) == 1 else jnp.concatenate(tiles, axis=1)


def _cat(parts):
    return parts[0] if len(parts) == 1 else jnp.concatenate(parts, axis=1)


def _gla_group_fwd(qg, fg, vg, lb, st, nc, cl):
    g = nc * cl
    sg = _sigmoid(fg)
    f = lb + (1.0 - lb) * sg
    k = 1.0 - f
    row, col = _iota2((g, g), 0), _iota2((g, g), 1)
    chunk_of = lambda idx: sum((idx >= u * cl).astype(jnp.int32) for u in range(1, nc)) if nc > 1 else 0
    same = chunk_of(row) == chunk_of(col) if nc > 1 else None
    causal = row >= col if nc == 1 else jnp.logical_and(same, row >= col)
    anti = col >= row if nc == 1 else jnp.logical_and(same, col >= row)
    b = _mask_dot_heads(causal.astype(bf16), jnp.log(f))
    bls = [b[:, (u + 1) * cl - 1:(u + 1) * cl, :] for u in range(nc)]
    ebls = [jnp.exp(x) for x in bls]
    e = jnp.exp(b)
    ei = jnp.exp(-b)
    eo = jnp.exp(_chunk_rows(bls, cl) - b)
    qi, ki, ko = qg * e, k * ei, k * eo
    att = jnp.where(causal[None], _hdot(qi, ki, _BNT), 0.0)
    o_intra = _hdot(att, vg, _BNN)
    sl = [slice(u * cl, (u + 1) * cl) for u in range(nc)]
    ds = [_hdot(vg[:, s], ko[:, s], _BTN) for s in sl]
    sts = [st]
    for u in range(nc):
        sts.append(sts[u] * ebls[u] + ds[u])
    o = o_intra + _cat([_hdot(qi[:, sl[u]], sts[u], _BNT) for u in range(nc)])
    return dict(sg=sg, f=f, e=e, ei=ei, eo=eo, ebls=ebls, qi=qi, ki=ki, ko=ko, att=att, o=o, sts=sts, causal=causal,
                anti=anti, sl=sl)


def _gla_group(nreal, want):
    while nreal % want:
        want //= 2
    return max(want, 1)


def _head_out(o, ggc, hg):
    r = lax.rsqrt(jnp.mean(o * o, axis=-1, keepdims=True) + EPS)
    return o * r * hg * (ggc * _sigmoid(ggc))


def _gla_fwd(pmat, lb, hg, bl_, t, nm, name):
    n, d4 = pmat.shape
    d = d4 // 4
    nh = d // A_DK
    nreal = (t - nm) // A_CHUNK
    nch = nreal + 1
    un = _gla_group(nreal, GLA_GROUP)
    hb = _gla_group(nh, GLA_HEADS)
    ng = nh // hb
    wide = hb * A_DK

    def body(q_ref, f_ref, i_ref, gg_ref, lb_ref, hg_ref, og_ref, ss_ref):
        lbv, hgv = _heads(lb_ref[...], hb), _heads(hg_ref[...], hb)
        take = lambda ref, rows: _heads(ref[rows, :], hb)

        def run(rows, st, idx, nc, cl):
            w = _gla_group_fwd(take(q_ref, rows), take(f_ref, rows), take(i_ref, rows), lbv, st, nc, cl)
            out = _head_out(w["o"], take(gg_ref, rows), hgv)
            for h in range(hb):
                for u in range(nc):
                    ss_ref[h, idx + u] = w["sts"][u][h]
                og_ref[rows, h * A_DK:(h + 1) * A_DK] = out[h].astype(og_ref.dtype)
            return w["sts"][nc]

        st = run(pl.ds(0, nm), jnp.zeros((hb, A_DK, A_DK), f32), 0, 1, nm)

        def step(it, st):
            rows = pl.ds(pl.multiple_of(nm + it * (un * A_CHUNK), BF16_ROWS), un * A_CHUNK)
            return run(rows, st, 1 + it * un, un, A_CHUNK)

        lax.fori_loop(0, nreal // un, step, st)

    col = lambda o: pl.BlockSpec((t, wide), lambda b, h: (b, o * ng + h))
    vec = pl.BlockSpec((1, wide), lambda b, h: (0, h))
    return pl.pallas_call(
        body, grid=(bl_, ng), name=name,
        in_specs=[col(0), col(1), col(2), col(3), vec, vec],
        out_specs=[pl.BlockSpec((t, wide), lambda b, h: (b, h)),
                   pl.BlockSpec((hb, nch, A_DK, A_DK), lambda b, h: (b * ng + h, 0, 0, 0))],
        out_shape=[SDS((n, d), bf16), SDS((bl_ * nh, nch, A_DK, A_DK), f32)], compiler_params=_params(),
    )(pmat, pmat, pmat, pmat, lb, hg)


def _gla_bwd(pmat, ss, dog, lb, hg, bl_, t, nm, name):
    n, d4 = pmat.shape
    d = d4 // 4
    nh = d // A_DK
    nreal = (t - nm) // A_CHUNK
    nch = nreal + 1
    un = _gla_group(nreal, GLA_GROUP)
    hb = _gla_group(nh, GLA_HEADS)
    ng = nh // hb
    wide = hb * A_DK

    def body(q_ref, f_ref, i_ref, gg_ref, ss_ref, dog_ref, lb_ref, hg_ref,
             dq_ref, df_ref, di_ref, dgg_ref, dlb_ref, dhg_ref):
        lbv, hgv = _heads(lb_ref[...], hb), _heads(hg_ref[...], hb)
        take = lambda ref, rows: _heads(ref[rows, :], hb)

        def put(ref, rows, val):
            for h in range(hb):
                ref[rows, h * A_DK:(h + 1) * A_DK] = val[h].astype(ref.dtype)

        def run(rows, idx, carry, nc, cl):
            dst, dlb, dhg = carry
            qg, fg, vg, ggc = take(q_ref, rows), take(f_ref, rows), take(i_ref, rows), take(gg_ref, rows)
            dogc = take(dog_ref, rows).astype(f32)
            st_in = jnp.stack([ss_ref[h, idx] for h in range(hb)])
            w = _gla_group_fwd(qg, fg, vg, lbv, st_in, nc, cl)
            o, qi, ki, ko, sl, sts, ebls = w["o"], w["qi"], w["ki"], w["ko"], w["sl"], w["sts"], w["ebls"]
            r = lax.rsqrt(jnp.mean(o * o, axis=-1, keepdims=True) + EPS)
            sgg = _sigmoid(ggc)
            sil = ggc * sgg
            on = o * r
            dhg = dhg + jnp.sum(dogc * sil * on, axis=1, keepdims=True)
            put(dgg_ref, rows, dogc * on * hgv * (sgg * (1.0 + ggc * (1.0 - sgg))))
            tt = dogc * sil * hgv
            do = r * tt - on * (r * r) * jnp.mean(o * tt, axis=-1, keepdims=True)
            xs = [_hdot(do[:, s], qi[:, s], _BTN) for s in sl]
            dsts = [None] * nc + [dst]
            for u in reversed(range(nc)):
                dsts[u] = dsts[u + 1] * ebls[u] + xs[u]
            datt = jnp.where(w["causal"][None], _hdot(do, vg, _BNT), 0.0)
            dv = _hdot(w["att"], do, _BTN) + _cat([_hdot(ko[:, sl[u]], dsts[u + 1], _BNT) for u in range(nc)])
            dko = _cat([_hdot(vg[:, sl[u]], dsts[u + 1], _BNN) for u in range(nc)])
            dqi = _hdot(datt, ki, _BNN) + _cat([_hdot(do[:, sl[u]], sts[u], _BNN) for u in range(nc)])
            dki = _hdot(datt, qi, _BTN)
            dk = dki * w["ei"] + dko * w["eo"]
            dkoko = dko * ko
            db = dqi * qi - dki * ki - dkoko
            rowi = lax.broadcasted_iota(jnp.int32, db.shape, 1)
            for u in range(nc):
                d_ebl = jnp.sum(dsts[u + 1] * sts[u], axis=1, keepdims=True)
                dbl = jnp.sum(dkoko[:, sl[u]], axis=1, keepdims=True) + d_ebl * ebls[u]
                db = db + jnp.where(rowi == (u + 1) * cl - 1, dbl, 0.0)
            dlogf = _mask_dot_heads(w["anti"].astype(bf16), db)
            df = dlogf / w["f"] - dk
            sg = w["sg"]
            put(dq_ref, rows, dqi * w["e"])
            put(df_ref, rows, df * (1.0 - lbv) * sg * (1.0 - sg))
            put(di_ref, rows, dv)
            dlb = dlb + jnp.sum(df * (1.0 - sg), axis=1, keepdims=True)
            return dsts[0], dlb, dhg

        zero = jnp.zeros((hb, 1, A_DK), f32)
        ngroups = nreal // un

        def step(it, carry):
            grp = ngroups - 1 - it
            rows = pl.ds(pl.multiple_of(nm + grp * (un * A_CHUNK), BF16_ROWS), un * A_CHUNK)
            return run(rows, 1 + grp * un, carry, un, A_CHUNK)

        carry = lax.fori_loop(0, ngroups, step, (jnp.zeros((hb, A_DK, A_DK), f32), zero, zero))
        _, dlb, dhg = run(pl.ds(0, nm), 0, carry, 1, nm)

        @pl.when(pl.program_id(1) == 0)
        def _():
            dlb_ref[...] = jnp.zeros_like(dlb_ref)
            dhg_ref[...] = jnp.zeros_like(dhg_ref)

        for h in range(hb):
            dlb_ref[:, h * A_DK:(h + 1) * A_DK] += dlb[h]
            dhg_ref[:, h * A_DK:(h + 1) * A_DK] += dhg[h]

    col = lambda o: pl.BlockSpec((t, wide), lambda h, b: (b, o * ng + h))
    blk = pl.BlockSpec((t, wide), lambda h, b: (b, h))
    vec = pl.BlockSpec((1, wide), lambda h, b: (0, h))
    return pl.pallas_call(
        body, grid=(ng, bl_), name=name,
        in_specs=[col(0), col(1), col(2), col(3),
                  pl.BlockSpec((hb, nch, A_DK, A_DK), lambda h, b: (b * ng + h, 0, 0, 0)), blk, vec, vec],
        out_specs=[blk, blk, blk, blk, vec, vec],
        out_shape=[SDS((n, d), bf16)] * 4 + [SDS((1, d), f32)] * 2, compiler_params=_params(),
    )(pmat, pmat, pmat, pmat, ss, dog, lb, hg)


def _shifted(x, halo, before):
    n = x.shape[0]
    both = jnp.concatenate([halo, x] if before else [x, halo], axis=0)
    row, col = _iota2((n, n + BF16_ROWS), 0), _iota2((n, n + BF16_ROWS), 1)
    src = row + BF16_ROWS if before else row
    step = -1 if before else 1
    pick = lambda s: jnp.dot((col == src + step * s).astype(bf16), both, preferred_element_type=f32)
    return pick(1), pick(2)


def _conv3(xb, halo, w):
    x = xb.astype(f32)
    x1, x2 = _shifted(xb, halo, True)
    return x, x1, x2, w[0:1, :] * x2 + w[1:2, :] * x1 + w[2:3, :] * x


def _conv_gate_fwd(ug, uv, cwg, cwv, bl_, t, tc, name):
    n, ff = ug.shape
    nt = t // tc

    def body(ug_ref, uv_ref, wg_ref, wv_ref, a_ref, hg_ref, hv_ref):
        @pl.when(pl.program_id(1) == 0)
        def _():
            hg_ref[...] = jnp.zeros_like(hg_ref)
            hv_ref[...] = jnp.zeros_like(hv_ref)

        xg, xv = ug_ref[...], uv_ref[...]
        cg = _conv3(xg, hg_ref[...], wg_ref[...])[3]
        cv = _conv3(xv, hv_ref[...], wv_ref[...])[3]
        a_ref[...] = (cg * _sigmoid(cg) * cv).astype(a_ref.dtype)
        hg_ref[...] = xg[tc - BF16_ROWS:tc, :].astype(hg_ref.dtype)
        hv_ref[...] = xv[tc - BF16_ROWS:tc, :].astype(hv_ref.dtype)

    row = pl.BlockSpec((tc, ff), lambda b, i: (b * nt + i, 0))
    wsp = pl.BlockSpec((3, ff), lambda b, i: (0, 0))
    return pl.pallas_call(
        body, grid=(bl_, nt), name=name, in_specs=[row, row, wsp, wsp], out_specs=row,
        out_shape=SDS((n, ff), bf16),
        scratch_shapes=[pltpu.VMEM((BF16_ROWS, ff), bf16), pltpu.VMEM((BF16_ROWS, ff), bf16)], compiler_params=_params(),
    )(ug, uv, cwg, cwv)


def _conv_gate_bwd(ug, uv, cwg, cwv, da, bl_, t, tc, name):
    n, ff = ug.shape
    nt = t // tc
    per = tc // BF16_ROWS

    def body(ug_ref, uv_ref, pg_ref, pv_ref, wg_ref, wv_ref, da_ref, dug_ref, duv_ref, dwg_ref, dwv_ref, ng_ref, nv_ref):
        first = jnp.logical_and(pl.program_id(0) == 0, pl.program_id(1) == 0)

        @pl.when(first)
        def _():
            dwg_ref[...] = jnp.zeros_like(dwg_ref)
            dwv_ref[...] = jnp.zeros_like(dwv_ref)

        @pl.when(pl.program_id(1) == 0)
        def _():
            ng_ref[...] = jnp.zeros_like(ng_ref)
            nv_ref[...] = jnp.zeros_like(nv_ref)

        seq_start = pl.program_id(1) == nt - 1
        dav = da_ref[...].astype(f32)

        def half(u_ref, p_ref, w_ref):
            halo = p_ref[...]
            return _conv3(u_ref[...], jnp.where(seq_start, jnp.zeros_like(halo), halo), w_ref[...])

        xg, xg1, xg2, cg = half(ug_ref, pg_ref, wg_ref)
        xv, xv1, xv2, cv = half(uv_ref, pv_ref, wv_ref)
        sg = _sigmoid(cg)
        dcg = dav * cv * (sg * (1.0 + cg * (1.0 - sg)))
        dcv = dav * (cg * sg)

        def back(dc, x, x1, x2, w_ref, nx_ref, du_ref, dw_ref):
            w = w_ref[...]
            dcb = dc.astype(bf16)
            dc1, dc2 = _shifted(dcb, nx_ref[...], False)
            du = w[2:3, :] * dc + w[1:2, :] * dc1 + w[0:1, :] * dc2
            du_ref[...] = du.astype(du_ref.dtype)
            dw_ref[0:1, :] += jnp.sum(dc * x2, axis=0, keepdims=True)
            dw_ref[1:2, :] += jnp.sum(dc * x1, axis=0, keepdims=True)
            dw_ref[2:3, :] += jnp.sum(dc * x, axis=0, keepdims=True)
            nx_ref[...] = dcb[0:BF16_ROWS, :].astype(nx_ref.dtype)

        back(dcg, xg, xg1, xg2, wg_ref, ng_ref, dug_ref, dwg_ref)
        back(dcv, xv, xv1, xv2, wv_ref, nv_ref, duv_ref, dwv_ref)

    row = pl.BlockSpec((tc, ff), lambda b, i: (b * nt + nt - 1 - i, 0))
    prev = pl.BlockSpec((BF16_ROWS, ff), lambda b, i: (jnp.maximum((b * nt + nt - 1 - i) * per - 1, 0), 0))
    wsp = pl.BlockSpec((3, ff), lambda b, i: (0, 0))
    return pl.pallas_call(
        body, grid=(bl_, nt), name=name, in_specs=[row, row, prev, prev, wsp, wsp, row],
        out_specs=[row, row, wsp, wsp],
        out_shape=[SDS((n, ff), bf16), SDS((n, ff), bf16), SDS((3, ff), f32), SDS((3, ff), f32)],
        scratch_shapes=[pltpu.VMEM((BF16_ROWS, ff), bf16), pltpu.VMEM((BF16_ROWS, ff), bf16)], compiler_params=_params(),
    )(ug, uv, ug, uv, cwg, cwv, da)


def _zf_c(hk, wzf, fgb, bl_, t, tm, name):
    n, d = hk.shape
    nt = t // tm

    def body(hk_ref, w_ref, b_ref, zf_ref, c_ref, carry_ref):
        @pl.when(pl.program_id(1) == 0)
        def _():
            carry_ref[...] = jnp.zeros_like(carry_ref)

        z = _bdot(hk_ref[...], w_ref[...]) + b_ref[...]
        ls = jnp.minimum(z, 0.0) - jnp.log(1.0 + jnp.exp(-jnp.abs(z)))
        c = _cumsum_rows(ls) + carry_ref[...]
        zf_ref[...] = z
        c_ref[...] = c
        carry_ref[...] = c[tm - 1:tm, :]

    row = lambda w: pl.BlockSpec((tm, w), lambda b, i: (b * nt + i, 0))
    return pl.pallas_call(
        body, grid=(bl_, nt), name=name,
        in_specs=[row(d), pl.BlockSpec((d, LANES), lambda b, i: (0, 0)), pl.BlockSpec((1, LANES), lambda b, i: (0, 0))],
        out_specs=[row(LANES), row(LANES)],
        out_shape=[SDS((n, LANES), f32), SDS((n, LANES), f32)],
        scratch_shapes=[pltpu.VMEM((1, LANES), f32)], compiler_params=_params(),
    )(hk, wzf, fgb)


def _c_bwd(dc, zf, bl_, t, tm, name):
    n = dc.shape[0]
    nt = t // tm

    def body(dc_ref, zf_ref, dzf_ref, dfg_ref, carry_ref):
        @pl.when(jnp.logical_and(pl.program_id(0) == 0, pl.program_id(1) == 0))
        def _():
            dfg_ref[...] = jnp.zeros_like(dfg_ref)

        @pl.when(pl.program_id(1) == 0)
        def _():
            carry_ref[...] = jnp.zeros_like(carry_ref)

        rc = _revcumsum_rows(dc_ref[...]) + carry_ref[...]
        dz = rc * _sigmoid(-zf_ref[...])
        dzf_ref[...] = dz.astype(dzf_ref.dtype)
        dfg_ref[...] += jnp.sum(dz, axis=0, keepdims=True)
        carry_ref[...] = rc[0:1, :]

    row = pl.BlockSpec((tm, LANES), lambda b, i: (b * nt + nt - 1 - i, 0))
    vec = pl.BlockSpec((1, LANES), lambda b, i: (0, 0))
    return pl.pallas_call(
        body, grid=(bl_, nt), name=name, in_specs=[row, row], out_specs=[row, vec],
        out_shape=[SDS((n, LANES), bf16), SDS((1, LANES), f32)],
        scratch_shapes=[pltpu.VMEM((1, LANES), f32)], compiler_params=_params(),
    )(dc, zf)


def _is_pow2(x):
    m, _ = math.frexp(x)
    return m == 0.5


def _prescale(qh, scale):
    return (qh.astype(f32) * scale).astype(bf16)


def _attn_fwd(q, kv, ck, bl_, t, tq, hd, name):
    n, d = q.shape
    npair = d // LANES
    hp = LANES // hd
    nq = t // tq
    scale = 1.0 / (hd ** 0.5)

    pre = _is_pow2(scale)

    def body(q_ref, k_ref, v_ref, ck_ref, o_ref, lse_ref):
        i = pl.program_id(2)
        diag = _iota2((tq, tq), 0) >= _iota2((tq, tq), 1)
        cut = tq // 2 // BF16_ROWS * BF16_ROWS
        halves = [(0, tq)] if cut == 0 else [(0, cut), (cut, tq)]
        for hh in range(hp):
            lanes = slice(hh * hd, (hh + 1) * hd)
            qh = _prescale(q_ref[:, lanes], scale) if pre else q_ref[:, lanes]

            def block(j, carry, masked, lanes=lanes, qh=qh, hh=hh):
                rows = pl.ds(pl.multiple_of(j * tq, BF16_ROWS), tq)
                kj, vj, cj = k_ref[rows, lanes], v_ref[rows, lanes], ck_ref[0, 0, j, hh:hh + 1, :]
                out = []
                for (lo, hi), (m, l, acc) in zip(halves, carry):
                    s = _bdot_nt(qh[lo:hi], kj)
                    s = (s if pre else s * scale) - cj
                    if masked:
                        s = jnp.where(diag[lo:hi], s, -1e30)
                    m2 = jnp.maximum(m, jnp.max(s, axis=-1, keepdims=True))
                    p = jnp.exp(s - m2)
                    a = jnp.exp(m - m2)
                    out.append((m2, a * l + jnp.sum(p, axis=-1, keepdims=True), a * acc + _bdot(p, vj)))
                return tuple(out)

            init = tuple((jnp.full((hi - lo, 1), -1e30, f32), jnp.zeros((hi - lo, 1), f32), jnp.zeros((hi - lo, hd), f32))
                         for lo, hi in halves)
            carry = lax.fori_loop(0, i, functools.partial(block, masked=False), init)
            for (lo, hi), (m, l, acc) in zip(halves, block(i, carry, True)):
                o_ref[lo:hi, lanes] = (acc / l).astype(o_ref.dtype)
                lse_ref[lo:hi, lanes] = jnp.broadcast_to(m + jnp.log(l), (hi - lo, hd))

    nk = nq
    return pl.pallas_call(
        body, grid=(bl_, npair, nq), name=name,
        in_specs=[pl.BlockSpec((tq, LANES), lambda b, p, i: (b * nq + i, p)),
                  pl.BlockSpec((t, LANES), lambda b, p, i: (b, p)),
                  pl.BlockSpec((t, LANES), lambda b, p, i: (b, npair + p)),
                  pl.BlockSpec((1, 1, nk, hp, tq), lambda b, p, i: (b, p, 0, 0, 0))],
        out_specs=[pl.BlockSpec((tq, LANES), lambda b, p, i: (b * nq + i, p)),
                   pl.BlockSpec((tq, LANES), lambda b, p, i: (b * nq + i, p))],
        out_shape=[SDS((n, d), f32), SDS((n, d), f32)], compiler_params=_params(),
    )(q, kv, kv, ck)


def _attn_bwd(q, kv, o, do, lse, ck, bl_, t, tq, hd, name):
    n, d = q.shape
    npair = d // LANES
    hp = LANES // hd
    nq = t // tq
    scale = 1.0 / (hd ** 0.5)

    pre = _is_pow2(scale)

    def body(q_ref, k_ref, v_ref, o_ref, do_ref, lse_ref, ck_ref, dq_ref, dk_ref, dv_ref, dck_ref, dcq_ref):
        j = pl.program_id(2)

        @pl.when(j == 0)
        def _():
            dq_ref[...] = jnp.zeros_like(dq_ref)
            dcq_ref[...] = jnp.zeros_like(dcq_ref)

        diag = _iota2((tq, tq), 0) >= _iota2((tq, tq), 1)
        for hh in range(hp):
            lanes = slice(hh * hd, (hh + 1) * hd)
            kh = k_ref[:, lanes]
            vh = v_ref[:, lanes]
            kt = kh.astype(f32).T.astype(bf16)
            cs = ck_ref[0, 0, 0, hh:hh + 1, :]

            def block(i, carry, masked, lanes=lanes, kh=kh, vh=vh, kt=kt, cs=cs, hh=hh):
                dkt, dvt, dcs = carry
                rows = pl.ds(pl.multiple_of(i * tq, BF16_ROWS), tq)
                qh = _prescale(q_ref[rows, lanes], scale) if pre else q_ref[rows, lanes]
                doh = do_ref[rows, lanes]
                s = _bdot_nt(qh, kh)
                s = (s if pre else s * scale) - cs
                if masked:
                    s = jnp.where(diag, s, -1e30)
                p = jnp.exp(s - lse_ref[rows, hh * hd:hh * hd + 1])
                delta = jnp.sum(doh.astype(f32) * o_ref[rows, lanes].astype(f32), axis=-1, keepdims=True)
                ds = p * (_bdot_nt(doh, vh) - delta)
                dsb = ds.astype(bf16)
                dq_ref[rows, lanes] += _bdot_nt(kt, dsb).T * scale
                dcq_ref[0, rows, hh:hh + 1] += jnp.sum(ds, axis=-1, keepdims=True)
                dkq = _bdot_tn(qh, dsb)
                return (dkt + (dkq if pre else dkq * scale), dvt + _bdot_tn(doh, p), dcs - jnp.sum(ds, axis=0, keepdims=True))

            init = (jnp.zeros((hd, tq), f32), jnp.zeros((hd, tq), f32), jnp.zeros((1, tq), f32))
            dkt, dvt, dcs = lax.fori_loop(j + 1, nq, functools.partial(block, masked=False), block(j, init, True))
            dk_ref[:, lanes] = dkt.T.astype(dk_ref.dtype)
            dv_ref[:, lanes] = dvt.T.astype(dv_ref.dtype)
            dck_ref[0, 0, 0, hh:hh + 1, :] = dcs

    whole = lambda c0: pl.BlockSpec((t, LANES), lambda b, p, j: (b, c0 + p))
    tile = lambda c0: pl.BlockSpec((tq, LANES), lambda b, p, j: (b * nq + j, c0 + p))
    ckspec = pl.BlockSpec((1, 1, 1, hp, tq), lambda b, p, j: (b, p, j, 0, 0))
    cqspec = pl.BlockSpec((1, t, hp), lambda b, p, j: (p, b, 0))
    return pl.pallas_call(
        body, grid=(bl_, npair, nq), name=name,
        in_specs=[whole(0), tile(0), tile(npair), whole(0), whole(0), whole(0), ckspec],
        out_specs=[whole(0), tile(0), tile(0), ckspec, cqspec],
        out_shape=[SDS((n, d), f32), SDS((n, d), bf16), SDS((n, d), bf16), SDS((bl_, npair, nq, hp, tq), f32),
                   SDS((npair, n, hp), f32)],
        compiler_params=_params(),
    )(q, kv, kv, o, do, lse, ck)


def _loss_head(h, target, mix, g, t, nm, tm, name):
    n, d = h.shape
    nt = t // tm

    def body(h_ref, t_ref, mix_ref, g_ref, loss_ref, dh_ref, dmix_ref, dg_ref):
        i = pl.program_id(0)

        @pl.when(i == 0)
        def _():
            loss_ref[...] = jnp.zeros_like(loss_ref)
            dg_ref[...] = jnp.zeros_like(dg_ref)

        pos = (i % nt) * tm + _iota2((tm, d), 0)
        err = jnp.where(pos >= nm, h_ref[...] - t_ref[...], 0.0)
        dh = err * (1.0 / d)
        dh_ref[...] = dh
        loss_ref[...] += 0.5 * jnp.sum(jnp.mean(err * err, axis=-1, keepdims=True))
        dmix, dg = _norm_bwd_tile(mix_ref[...], g_ref[...], dh)
        dmix_ref[...] = dmix.astype(dmix_ref.dtype)
        dg_ref[...] += dg

    row = pl.BlockSpec((tm, d), lambda i: (i, 0))
    vec = pl.BlockSpec((1, d), lambda i: (0, 0))
    return pl.pallas_call(
        body, grid=(n // tm,), name=name, in_specs=[row, row, row, vec],
        out_specs=[pl.BlockSpec((8, LANES), lambda i: (0, 0)), row, row, vec],
        out_shape=[SDS((8, LANES), f32), SDS((n, d), f32), SDS((n, d), bf16), SDS((1, d), f32)], compiler_params=_params(),
    )(h, target, mix, g)


def _c_key_rows(c, bl_, t, tq, bh, hp):
    npair = bh // hp
    nk = t // tq
    return c[:, :bh].reshape(bl_, nk, tq, npair, hp).transpose(0, 3, 1, 4, 2)


def _dc_rows(dck, dcq, bl_, t, bh):
    d = dck.transpose(0, 2, 4, 1, 3).reshape(bl_ * t, bh) + dcq.transpose(1, 0, 2).reshape(bl_ * t, bh)
    return jnp.pad(d, ((0, 0), (0, LANES - bh)))


_ANY = pl.BlockSpec(memory_space=pl.ANY)


def _all_gather(xs, name):
    na = len(xs)

    def body(*refs):
        x_refs, out_refs = refs[:na], refs[na:2 * na]
        send_sems, recv_sems, local_sems = refs[2 * na:]
        mx, my, mc = lax.axis_index("x"), lax.axis_index("y"), lax.axis_index("c")
        me, sibling = (mx, my, mc), (mx, my, 1 - mc)
        chips = [(1 - mx, my), (mx, 1 - my), (1 - mx, 1 - my)]

        def copy(a, k, block, to, own=False):
            px, py, pc = block
            rows = out_refs[a].at[4 * px + 2 * py + pc]
            return pltpu.make_async_remote_copy(
                src_ref=x_refs[a] if own else rows, dst_ref=rows,
                send_sem=send_sems.at[a, k], recv_sem=recv_sems.at[a, k], device_id=to, device_id_type=MESH)

        arrays = range(na)
        mine = [pltpu.make_async_copy(x_refs[a], out_refs[a].at[4 * mx + 2 * my + mc], local_sems.at[a]) for a in arrays]
        for cp in mine:
            cp.start()
        first = [copy(a, 1 + j, me, (*chip, mc), own=True) for j, chip in enumerate(chips) for a in arrays]
        first += [copy(a, 0, me, sibling, own=True) for a in arrays]
        for cp in first:
            cp.start()
        passed = []
        for j, chip in enumerate(chips):
            for a in arrays:
                copy(a, 1 + j, (*chip, mc), me).wait_recv()
                cp = copy(a, 4 + j, (*chip, mc), sibling)
                cp.start()
                passed.append(cp)
        for a in arrays:
            copy(a, 0, sibling, me).wait_recv()
        for j, chip in enumerate(chips):
            for a in arrays:
                copy(a, 4 + j, (*chip, 1 - mc), me).wait_recv()
        for cp in first + passed:
            cp.wait_send()
        for cp in mine:
            cp.wait()

    return pl.pallas_call(
        body, name=name, out_shape=[SDS((N_DEV,) + x.shape, x.dtype) for x in xs],
        in_specs=[_ANY] * na, out_specs=[_ANY] * na,
        scratch_shapes=[pltpu.SemaphoreType.DMA((na, 7)), pltpu.SemaphoreType.DMA((na, 7)), pltpu.SemaphoreType.DMA((na,))],
    )(*xs)


_HBM = pl.BlockSpec(memory_space=pltpu.HBM)
_SEM = pl.BlockSpec(memory_space=pltpu.SEMAPHORE)
_DATAFLOW = pltpu.SideEffectType.DATAFLOW_SIDE_EFFECTING
N_PEERS = N_DEV - 1


def _device_index():
    return 4 * lax.axis_index("x") + 2 * lax.axis_index("y") + lax.axis_index("c")


def _peers():
    mx, my, mc = lax.axis_index("x"), lax.axis_index("y"), lax.axis_index("c")
    peers = []
    for r in (2, 3, 4, 5, 6, 7, 1):
        px = 1 - mx if r & 4 else mx
        py = 1 - my if r & 2 else my
        pc = 1 - mc if r & 1 else mc
        peers.append(((px, py, pc), 4 * px + 2 * py + pc))
    return 4 * mx + 2 * my + mc, peers


def _push_copy(src_ref, land_ref, send_sems, recv_sems, a, k, dev, src_row, land_row, scatter):
    return pltpu.make_async_remote_copy(
        src_ref=src_ref.at[src_row] if scatter else src_ref, dst_ref=land_ref.at[land_row],
        send_sem=send_sems.at[a * N_PEERS + k], recv_sem=recv_sems.at[a * N_PEERS + k], device_id=dev, device_id_type=MESH)


def _landing(own, me):
    return lax.dynamic_update_index_in_dim(lax.empty((N_DEV,) + own.shape, own.dtype), own, me, 0)


def _push_start(srcs, lands, scatter, name):
    na = len(srcs)

    def body(*refs):
        src_refs, land_refs = refs[:na], refs[na:2 * na]
        send_sems, recv_sems = refs[2 * na], refs[2 * na + 1]
        token = refs[-1]
        me, peers = _peers()
        for a in range(na):
            for k, (dev, idx) in enumerate(peers):
                _push_copy(src_refs[a], land_refs[a], send_sems, recv_sems, a, k, dev, idx, me, scatter).start()
        token[...] = jnp.zeros_like(token)

    hbm = lambda arrs: [pltpu.HBM(a.shape, a.dtype) for a in arrs]
    out = pl.pallas_call(
        body, name=name,
        out_shape=(pltpu.SemaphoreType.DMA((na * N_PEERS,)), pltpu.SemaphoreType.DMA((na * N_PEERS,)), *hbm(srcs), *hbm(lands),
                   SDS((8, LANES), f32)),
        in_specs=[_HBM] * (2 * na),
        out_specs=(_SEM, _SEM, *([_HBM] * (2 * na)), pl.BlockSpec(memory_space=pltpu.VMEM)),
        input_output_aliases={i: 2 + i for i in range(2 * na)},
        compiler_params=pltpu.CompilerParams(has_side_effects=_DATAFLOW),
    )(*[pltpu.with_memory_space_constraint(a, pltpu.HBM) for a in list(srcs) + list(lands)])
    return out[0], out[1], list(out[2:2 + na]), list(out[2 + na:2 + 2 * na]), out[-1]


def _push_wait(send_sems, recv_sems, srcs, lands, which, after, scatter, name):
    nw = len(which)

    def body(*refs):
        src_refs, land_refs = refs[:nw], refs[nw:2 * nw]
        send_sems_, recv_sems_ = refs[2 * nw], refs[2 * nw + 1]
        _, peers = _peers()
        for j, a in enumerate(which):
            for k, (dev, idx) in enumerate(peers):
                cp = _push_copy(src_refs[j], land_refs[j], send_sems_, recv_sems_, a, k, dev, idx, idx, scatter)
                cp.wait_send()
                cp.wait_recv()

    hbm = lambda arrs: [pltpu.HBM(a.shape, a.dtype) for a in arrs]
    out = pl.pallas_call(
        body, name=name, out_shape=(*hbm(srcs), *hbm(lands)),
        in_specs=[_HBM] * (2 * nw) + [_SEM, _SEM, _ANY], out_specs=[_HBM] * (2 * nw),
        input_output_aliases={i: i for i in range(2 * nw)},
        compiler_params=pltpu.CompilerParams(has_side_effects=_DATAFLOW),
    )(*srcs, *lands, send_sems, recv_sems, after)
    return list(out[nw:])


def _adamw(parts, w, m, v, tr, name):
    g, r, c = parts.shape

    def body(p_ref, w_ref, m_ref, v_ref, g_ref, d_ref, m2_ref, v2_ref):
        gr = p_ref[0].astype(f32)
        for k in range(1, g):
            gr = gr + p_ref[k].astype(f32)
        m2 = ADAM_B1 * m_ref[...] + (1.0 - ADAM_B1) * gr
        v2 = ADAM_B2 * v_ref[...] + (1.0 - ADAM_B2) * (gr * gr)
        m_hat = m2 / (1.0 - ADAM_B1 ** ADAM_STEP)
        v_hat = v2 / (1.0 - ADAM_B2 ** ADAM_STEP)
        g_ref[...] = gr
        d_ref[...] = -ADAM_LR * (m_hat / (jnp.sqrt(v_hat) + ADAM_EPS) + ADAM_WD * w_ref[...])
        m2_ref[...] = m2
        v2_ref[...] = v2

    row = pl.BlockSpec((tr, c), lambda i: (i, 0))
    return pl.pallas_call(
        body, grid=(r // tr,), name=name, in_specs=[pl.BlockSpec((g, tr, c), lambda i: (0, i, 0)), row, row, row],
        out_specs=[row] * 4, out_shape=[SDS((r, c), f32)] * 4, compiler_params=_params(),
    )(parts, w, m, v)


_SHARD_AXIS = dict(meta_tokens=1, norm_gains=2, a_w_in=2, a_lb_logits=1, a_head_norm=1, a_w_out=1, kv_w=1,
                   b_w_q=1, b_w_out=1, ffn_w_up=2, ffn_conv=2, ffn_w_down=1)
_VECTORS = ("meta_tokens", "norm_gains", "a_lb_logits", "a_head_norm", "ffn_conv")
_REPLICATED = ("kv_norm", "fg_b")
_ROW_TILE_CAP = 512


def _pack(arrs, dtype, cols, row_mult):
    lead = arrs[0].shape[:-1] if arrs[0].ndim > 1 else ()
    flat = jnp.concatenate([a.astype(dtype) for a in arrs], axis=-1)
    size = flat.shape[-1]
    per = cols * row_mult
    total = -(-size // per) * per
    flat = jnp.pad(flat, [(0, 0)] * len(lead) + [(0, total - size)])
    return flat.reshape(lead + (total // cols, cols))


def _unpack(flat, shapes):
    out, off = [], 0
    lead = flat.shape[:-1]
    for shp in shapes:
        size = 1
        for s in shp:
            size *= s
        out.append(flat[..., off:off + size].reshape(lead + tuple(shp)))
        off += size
    return out


def _unshard(seg, axis):
    a = jnp.moveaxis(seg, 0, axis)
    shp = a.shape
    return a.reshape(shp[:axis] + (shp[axis] * shp[axis + 1],) + shp[axis + 2:])


def _shard8(full, axis):
    shp = full.shape
    a = full.reshape(shp[:axis] + (N_DEV, shp[axis] // N_DEV) + shp[axis + 1:])
    return jnp.moveaxis(a, axis, 0)


def _rows(a, lead=0):
    return a.reshape(a.shape[:lead] + (-1, a.shape[-1]))


def kernel(x, meta_tokens, norm_gains, a_w_in, a_lb_logits, a_head_norm, a_w_out, kv_norm, kv_w, fg_b, b_w_q, b_w_out, ffn_w_up, ffn_conv, ffn_w_down, loss_target, m_meta_tokens, m_norm_gains, m_a_w_in, m_a_lb_logits, m_a_head_norm, m_a_w_out, m_kv_norm, m_kv_w, m_fg_b, m_b_w_q, m_b_w_out, m_ffn_w_up, m_ffn_conv, m_ffn_w_down, v_meta_tokens, v_norm_gains, v_a_w_in, v_a_lb_logits, v_a_head_norm, v_a_w_out, v_kv_norm, v_kv_w, v_fg_b, v_b_w_q, v_b_w_out, v_ffn_w_up, v_ffn_conv, v_ffn_w_down):
    names = ("meta_tokens", "norm_gains", "a_w_in", "a_lb_logits", "a_head_norm", "a_w_out", "kv_norm", "kv_w", "fg_b",
             "b_w_q", "b_w_out", "ffn_w_up", "ffn_conv", "ffn_w_down")
    w = dict(zip(names, (meta_tokens, norm_gains, a_w_in, a_lb_logits, a_head_norm, a_w_out, kv_norm, kv_w, fg_b,
                         b_w_q, b_w_out, ffn_w_up, ffn_conv, ffn_w_down)))
    mom = dict(zip(names, (m_meta_tokens, m_norm_gains, m_a_w_in, m_a_lb_logits, m_a_head_norm, m_a_w_out, m_kv_norm,
                           m_kv_w, m_fg_b, m_b_w_q, m_b_w_out, m_ffn_w_up, m_ffn_conv, m_ffn_w_down)))
    var = dict(zip(names, (v_meta_tokens, v_norm_gains, v_a_w_in, v_a_lb_logits, v_a_head_norm, v_a_w_out, v_kv_norm,
                           v_kv_w, v_fg_b, v_b_w_q, v_b_w_out, v_ffn_w_up, v_ffn_conv, v_ffn_w_down)))

    bl_, seq, d = x.shape
    nm = meta_tokens.shape[0]
    t = nm + seq
    n = bl_ * t
    bh = fg_b.shape[0]
    hd = d // bh
    hp = LANES // hd
    ff = ffn_w_down.shape[1] * N_DEV
    tm = _div_tile(t, TOKEN_TILE_CAP)
    tc = _div_tile(t, 64)
    tn = min(d, MODEL_TILE_CAP)

    vec_pack = _pack([w[k].reshape(-1) for k in _VECTORS], f32, LANES, 8)
    first = _all_gather([w["a_w_in"].astype(bf16), vec_pack], "gather_first")
    vec_segs = _unpack(first[1].reshape(N_DEV, -1), [w[k].shape for k in _VECTORS])
    small = {k: _unshard(a, _SHARD_AXIS[k]) for k, a in zip(_VECTORS, vec_segs)}
    w_in = _unshard(first[0], _SHARD_AXIS["a_w_in"])[0]
    me = _device_index()
    later_names = ("a_w_out", "ffn_w_up", "ffn_w_down", "kv_w", "b_w_q", "b_w_out", "ffn_w_up", "ffn_w_down")
    later_layer = (None, 0, 0, None, None, None, 1, 1)
    later = [(w[k] if l is None else w[k][l]).astype(bf16) for k, l in zip(later_names, later_layer)]
    later, _ = lax.optimization_barrier((later, first[1]))
    g_send, g_recv, later_src, later_land, _ = _push_start(later, [_landing(a, me) for a in later], False, "gather_rest_start")

    def gathered(which, after, name):
        lands = _push_wait(g_send, g_recv, [later_src[i] for i in which], [later_land[i] for i in which], which, after,
                           False, name)
        return [_unshard(a, _SHARD_AXIS[later_names[i]] - (later_layer[i] is not None)) for i, a in zip(which, lands)]

    gains_box = [small["norm_gains"]]
    gain = lambda l, j: gains_box[0][l, j][None]
    cw_gate, cw_val = small["ffn_conv"][:, :, :ff], small["ffn_conv"][:, :, ff:]
    head_gain = small["a_head_norm"]
    lb = jax.nn.softmax(small["a_lb_logits"], axis=0)[0:1]
    kvn = kv_norm[None]
    fgb_pad = jnp.pad(fg_b, (0, LANES - bh))[None]

    h0 = jnp.concatenate([jnp.broadcast_to(small["meta_tokens"][None], (bl_, nm, d)), x], axis=1).reshape(n, d)

    def ffn_fwd(l, h_in, fi, next_gains, which):
        w_up, = gathered(which[:1], h_in, f"gather_wait_ffn{l}_up")
        w_gate[l], w_val[l] = w_up[:, :ff], w_up[:, ff:]
        ug = _mm(fi, w_gate[l], bf16, tm, ff, f"ffn{l}_up_gate")
        uv = _mm(fi, w_val[l], bf16, tm, ff, f"ffn{l}_up_val")
        w_down[l], = gathered(which[1:], uv, f"gather_wait_ffn{l}_down")
        act = _conv_gate_fwd(ug, uv, cw_gate[l], cw_val[l], bl_, t, tc, f"ffn{l}_conv_gate")
        h_out, mix, *normed = _mm_norm_res(act, w_down[l], gain(l, 3), h_in, next_gains, tm, f"ffn{l}_down")
        return h_out, (h_in, fi, ug, uv, act, mix), normed

    hn0 = _rms_fwd(h0, gain(0, 0), tm, "a_norm")
    pmat = _mm(hn0, w_in, f32, tm, tn, "a_in_proj")
    og, states = _gla_fwd(pmat, lb, head_gain, bl_, t, nm, "a_gla_fwd")
    w_out_a = gathered((0,), og, "gather_wait_a")[0][0]
    h1, mix_a, fi0 = _mm_norm_res(og, w_out_a, gain(0, 1), h0, [gain(0, 2)], tm, "a_out_proj")
    w_gate, w_val, w_down = {}, {}, {}
    h2, ffn0, (hk, hn1) = ffn_fwd(0, h1, fi0, [kvn, gain(1, 0)], (1, 2))

    w_kv_zf, w_q, w_out_b = gathered((3, 4, 5), h2, "gather_wait_b")
    w_kv, w_zf = w_kv_zf[:, :2 * d], jnp.pad(w_kv_zf[:, 2 * d:], ((0, 0), (0, LANES - bh)))
    w_q, w_out_b = w_q[0], w_out_b[0]
    kvp = _mm(hk, w_kv, bf16, tm, tn, "kv_proj")
    zf, cum = _zf_c(hk, w_zf, fgb_pad, bl_, t, tm, "forget_cumsum")
    ck = _c_key_rows(cum, bl_, t, tm, bh, hp)
    q = _mm(hn1, w_q, bf16, tm, tn, "b_q_proj")
    o, lse = _attn_fwd(q, kvp, ck, bl_, t, tm, hd, "b_attn_fwd")
    h3, mix_b, fi1 = _mm_norm_res(o, w_out_b, gain(1, 1), h2, [gain(1, 2)], tm, "b_out_proj")
    h4, ffn1, _ = ffn_fwd(1, h3, fi1, [], (6, 7))

    dgain = {}
    target = jnp.concatenate([jnp.zeros((bl_, nm, d), f32), loss_target], axis=1).reshape(n, d)
    loss8, dh, dmix, dgain[1, 3] = _loss_head(h4, target, ffn1[5], gain(1, 3), t, nm, tm, "loss_head")
    loss = lax.psum(loss8[0, 0], ("x", "y", "c"))

    def ffn_bwd(l, saved, dh_out, dmix, below):
        h_in, fi, ug, uv, act, mix = saved
        dact = _mm_nt([(dmix, w_down[l])], bf16, tm, ff, f"ffn{l}_down_dx")
        dw_down = _mm_tn(act, dmix, tm, ff, tn, f"ffn{l}_down_dw")
        dug, duv, dcg, dcv = _conv_gate_bwd(ug, uv, cw_gate[l], cw_val[l], dact, bl_, t, tc, f"ffn{l}_conv_gate_bwd")
        dfi = _mm_nt([(dug, w_gate[l]), (duv, w_val[l])], bf16, tm, tn // 2, f"ffn{l}_up_dx")
        dw_up = jnp.concatenate([_mm_tn(fi, dug, tm, tn, ff, f"ffn{l}_up_gate_dw"),
                                 _mm_tn(fi, duv, tm, tn, ff, f"ffn{l}_up_val_dw")], axis=1)
        mix_below, gain_below, key_below = below
        dh_in, dgain[l, 2], dmix_below, dgain[key_below] = _rms_bwd(
            h_in, gain(l, 2), dfi, dh_out, f32, tm, f"ffn{l}_norm_bwd", then=(mix_below, gain_below))
        return dh_in, dmix_below, dw_up, jnp.concatenate([dcg, dcv], axis=1), dw_down

    def shards(full, axis):
        return _rows(_shard8(full, axis), 1).astype(bf16)

    def push_grads(bufs, name, tie=None):
        lands = [_landing(lax.dynamic_index_in_dim(b, me, 0, keepdims=False), me) for b in bufs]
        s_sem, r_sem, srcs, lands, token = _push_start(bufs, lands, True, name)
        gains_box[0] = gains_box[0] + token[0, 0]
        return (s_sem, r_sem, srcs, lands), (None if tie is None else tie + token[0, 0])

    def landed(handle, after, name):
        s_sem, r_sem, srcs, lands = handle
        return _push_wait(s_sem, r_sem, srcs, lands, tuple(range(len(srcs))), after, True, name)

    dh, dmix, dw_up1, dconv1, dw_down1 = ffn_bwd(1, ffn1, dh, dmix, (mix_b, gain(1, 1), (1, 1)))
    push1, ck = push_grads([shards(dw_up1, 1), shards(dw_down1, 0)], "grad_push_ffn1", ck)

    do = _mm_nt([(dmix, w_out_b)], bf16, tm, tn, "b_out_dx")
    dw_out_b = _mm_tn(o, dmix, tm, tn, tn, "b_out_dw")
    dq, dk, dv, dck, dcq = _attn_bwd(q, kvp, o, do, lse, ck, bl_, t, tm, hd, "b_attn_bwd")
    dhn1 = _mm_nt([(dq, w_q)], bf16, tm, tn, "b_q_dx")
    dw_q = _mm_tn(hn1, dq, tm, tn, tn, "b_q_dw")
    dh, dgain[1, 0] = _rms_bwd(h2, gain(1, 0), dhn1, dh, f32, tm, "b_norm_bwd")

    dzf, dfgb = _c_bwd(_dc_rows(dck, dcq, bl_, t, bh), zf, bl_, t, tm, "forget_cumsum_bwd")
    dhk = _mm_nt([(dk, w_kv[:, :d]), (dv, w_kv[:, d:]), (dzf, w_zf)], bf16, tm, tn, "kv_dx")
    dw_kv = jnp.concatenate([_mm_tn(hk, dk, tm, tn, tn, "k_dw"), _mm_tn(hk, dv, tm, tn, tn, "v_dw"),
                             _mm_tn(hk, dzf, tm, tn, LANES, "zf_dw")[:, :bh]], axis=1)
    dh, dkvn, dmix, dgain[0, 3] = _rms_bwd(h2, kvn, dhk, dh, f32, tm, "kv_norm_bwd", then=(ffn0[5], gain(0, 3)))
    push2, cw_gate = push_grads([shards(dw_out_b, 0), shards(dw_q, 0), shards(dw_kv, 1)], "grad_push_b", cw_gate)

    dh, dmix, dw_up0, dconv0, dw_down0 = ffn_bwd(0, ffn0, dh, dmix, (mix_a, gain(0, 1), (0, 1)))
    push3, head_gain = push_grads([shards(dw_up0, 1), shards(dw_down0, 0)], "grad_push_ffn0", head_gain)

    dog = _mm_nt([(dmix, w_out_a)], bf16, tm, tn, "a_out_dx")
    dw_out_a = _mm_tn(og, dmix, tm, tn, tn, "a_out_dw")
    dpq, dpf, dpi, dpg, dlb, dhg = _gla_bwd(pmat, states, dog, lb, head_gain, bl_, t, nm, "a_gla_bwd")
    dps = (dpq, dpf, dpi, dpg)
    dw_in = jnp.concatenate([_mm_tn(hn0, dp, tm, tn, tn, f"a_in_dw{j}") for j, dp in enumerate(dps)], axis=1)
    push4, _ = push_grads([shards(dw_out_a, 0), shards(dw_in, 1)], "grad_push_a")
    dhn0 = _mm_nt([(dp, w_in[:, j * d:(j + 1) * d]) for j, dp in enumerate(dps)], bf16, tm, tn, "a_in_dx")
    dh, dgain[0, 0] = _rms_bwd(h0, gain(0, 0), dhn0, dh, f32, tm, "a_norm_bwd")

    dh = dh.reshape(bl_, t, d)
    grad_x = dh[:, nm:]
    dl0 = dlb * lb * (1.0 - lb)
    vec_grads = dict(
        meta_tokens=jnp.sum(dh[:, :nm], axis=0),
        norm_gains=jnp.stack([jnp.concatenate([dgain[l, j] for j in range(4)], axis=0) for l in range(2)]),
        a_lb_logits=jnp.concatenate([dl0, -dl0], axis=0), a_head_norm=dhg, ffn_conv=jnp.stack([dconv0, dconv1]))
    vec_send = _pack([_shard8(vec_grads[k], _SHARD_AXIS[k]).reshape(N_DEV, -1) for k in _VECTORS], bf16, LANES, BF16_ROWS)
    push5, _ = push_grads([vec_send], "grad_push_vectors")

    g_s, d_s, m_s, v_s = {}, {}, {}, {}
    outs = (g_s, d_s, m_s, v_s)

    def update(part, srcs, label):
        rows = part.shape[1]
        return _adamw(part, *srcs, rows if rows <= _ROW_TILE_CAP else _div_tile(rows, _ROW_TILE_CAP), label)

    def update_matrix(k, part, layer=None):
        pick = (lambda a: a) if layer is None else (lambda a: a[layer])
        label = f"adamw_{k}" if layer is None else f"adamw_{k}{layer}"
        res = update(part, [_rows(pick(src[k])) for src in (w, mom, var)], label)
        return [r.reshape(pick(w[k]).shape) for r in res]

    def put(k, res):
        for dst, r in zip(outs, res):
            dst[k] = r

    up1, down1 = (update_matrix(k, p, 1) for k, p in zip(("ffn_w_up", "ffn_w_down"), landed(push1, gains_box[0], "grad_wait_ffn1")))
    for k, p in zip(("b_w_out", "b_w_q", "kv_w"), landed(push2, up1[0], "grad_wait_b")):
        put(k, update_matrix(k, p))
    up0, down0 = (update_matrix(k, p, 0) for k, p in zip(("ffn_w_up", "ffn_w_down"), landed(push3, g_s["kv_w"], "grad_wait_ffn0")))
    put("ffn_w_up", [jnp.stack(pair) for pair in zip(up0, up1)])
    put("ffn_w_down", [jnp.stack(pair) for pair in zip(down0, down1)])
    part_out_a, part_in = landed(push4, down0[0], "grad_wait_a")
    put("a_w_out", update_matrix("a_w_out", part_out_a))
    put("a_w_in", update_matrix("a_w_in", part_in))
    part_vec, = landed(push5, g_s["a_w_in"], "grad_wait_vectors")
    vec_packs = [_pack([src[k].reshape(-1) for k in _VECTORS], f32, LANES, BF16_ROWS) for src in (w, mom, var)]
    vec_shapes = [w[k].shape for k in _VECTORS]
    for dst, r in zip(outs, update(part_vec, vec_packs, "adamw_vectors")):
        dst.update(zip(_VECTORS, _unpack(r.reshape(-1), vec_shapes)))

    rep_local = _pack([dkvn.reshape(-1), dfgb[0, :bh]], f32, LANES, 8)
    rep_parts = _all_gather([rep_local], "gather_replicated_grads")[0]
    rpacks = [_pack([src[k].reshape(-1) for k in _REPLICATED], f32, LANES, 8) for src in (w, mom, var)]
    rres = _adamw(rep_parts, *rpacks, rep_local.shape[0], "adamw_replicated")
    rshapes = [w[k].shape for k in _REPLICATED]
    g_r, d_r, m_r, v_r = ({k: a for k, a in zip(_REPLICATED, _unpack(r.reshape(-1), rshapes))} for r in rres)

    out = [loss, grad_x]
    for sh, rp in ((g_s, g_r), (d_s, d_r), (m_s, m_r), (v_s, v_r)):
        out += [sh[k] if k in sh else rp[k] for k in names]
    return tuple(out)
```
